```python
import math
import jax, jax.numpy as jnp
from jax import lax
import numpy as np


D_MODEL = 1024
BATCH = 8
SEQ = 2048
DEPTH = 1
DEC_BATCH = 128
DEC_SEQ = 4
PAST_LEN = 16384
PAGE_SIZE = 128

CONV_WIDTH = D_MODEL
CONV_K = 3
GDN_HEADS = 8
GDN_DK = 128
GDN_DV = 128
GDN_CONV_K = 4
GDN_CHUNK = 64
GDN_QK = GDN_HEADS * GDN_DK
GDN_V = GDN_HEADS * GDN_DV
GDN_CONV_CH = 2 * GDN_QK + GDN_V
MEM_TOKENS = 256
MEM_HEADS = 4
MEM_HEAD_DIM = D_MODEL // MEM_HEADS
MEM_Q = MEM_HEADS * MEM_HEAD_DIM
PEER_HEADS = 8
PEER_N_KEYS = 128
PEER_N_EXPERTS = PEER_N_KEYS * PEER_N_KEYS
PEER_D_KEY = 256
PEER_D_HALF = PEER_D_KEY // 2
PEER_TOPK = 16
PEER_BLOCK = 256

N_BRANCHES = 3
EPS = 1e-6
SPLITS = (CONV_WIDTH, CONV_WIDTH, CONV_WIDTH,
          GDN_QK, GDN_QK, GDN_V, GDN_V, GDN_HEADS, GDN_HEADS,
          MEM_Q, N_BRANCHES * D_MODEL)
IN_COLS = 3 * CONV_WIDTH + 2 * GDN_QK + 2 * GDN_V + 2 * GDN_HEADS + MEM_Q + N_BRANCHES * D_MODEL

kernel_name = 'hybrid_conv_gdn_mem_peer_step'


def _split_cols(p):
    outs = []
    start = 0
    for width in SPLITS:
        outs.append(p[..., start:start + width])
        start += width
    return outs


def rmsnorm(x, g):
    xf = x.astype(jnp.float32)
    r = lax.rsqrt(jnp.mean(xf * xf, axis=-1, keepdims=True) + EPS)
    return (xf * r).astype(x.dtype) * g


def l2norm(x):
    xf = x.astype(jnp.float32)
    return xf * lax.rsqrt(jnp.sum(xf * xf, axis=-1, keepdims=True) + EPS)


def causal_dwconv(u, buf, w):
    K = w.shape[0]
    T = u.shape[1]
    full = jnp.concatenate([buf.astype(u.dtype), u], axis=1)
    y = sum(full[:, j:j + T] * w[j] for j in range(K))
    return y, full[:, T:]


def gated_delta_chunked(q, k, v, g, beta, S0):
    Bsz, T, H, dk = q.shape
    dv = v.shape[-1]
    C = math.gcd(T, GDN_CHUNK)
    N = T // C

    def to_chunks(a):
        a = a.astype(jnp.float32).reshape((Bsz, N, C, H) + a.shape[3:])
        return jnp.moveaxis(a, (1, 3), (0, 2))

    qc, kc, vc = to_chunks(q), to_chunks(k), to_chunks(v)
    gc = jnp.cumsum(to_chunks(g), axis=-1)
    bc = to_chunks(beta)
    tri_incl = jnp.tril(jnp.ones((C, C), dtype=bool))
    tri_strict = jnp.tril(jnp.ones((C, C), dtype=bool), -1)
    diff = gc[..., :, None] - gc[..., None, :]
    decay = jnp.exp(jnp.where(tri_incl, diff, -jnp.inf))
    kb = kc * bc[..., None]
    L = jnp.where(tri_strict, jnp.einsum('nbhid,nbhjd->nbhij', kb, kc) * decay, 0.0)
    eye = jnp.eye(C, dtype=jnp.float32)
    Tm = lax.linalg.triangular_solve(eye + L, jnp.broadcast_to(eye, L.shape), left_side=True, lower=True)
    u_val = jnp.einsum('nbhij,nbhjd->nbhid', Tm, vc * bc[..., None])
    w_k = jnp.einsum('nbhij,nbhjd->nbhid', Tm, kb * jnp.exp(gc)[..., None])
    attn = jnp.where(tri_incl, jnp.einsum('nbhid,nbhjd->nbhij', qc, kc) * decay, 0.0)
    q_dec = qc * jnp.exp(gc)[..., None]
    k_dec = kc * jnp.exp(gc[..., -1:] - gc)[..., None]
    g_last = jnp.exp(gc[..., -1])

    def step(S, inp):
        u_i, w_i, a_i, qd_i, kd_i, gl_i = inp
        v_new = u_i - jnp.einsum('bhcd,bhde->bhce', w_i, S)
        o = jnp.einsum('bhcd,bhde->bhce', qd_i, S) + jnp.einsum('bhij,bhje->bhie', a_i, v_new)
        S = S * gl_i[..., None, None] + jnp.einsum('bhcd,bhce->bhde', kd_i, v_new)
        return S, o

    S, o = lax.scan(step, S0.astype(jnp.float32), (u_val, w_k, attn, q_dec, k_dec, g_last))
    o = jnp.moveaxis(o, (0, 2), (1, 3)).reshape(Bsz, T, H, dv)
    return o, S


def mem_kv(mem, g, w_mem_kv):
    kv = rmsnorm(mem, g) @ w_mem_kv
    Bsz, M, _ = mem.shape
    mk = kv[..., :MEM_Q].reshape(Bsz, M, MEM_HEADS, MEM_HEAD_DIM)
    mv = kv[..., MEM_Q:].reshape(Bsz, M, MEM_HEADS, MEM_HEAD_DIM)
    return mk, mv


def mem_attend(q, mk, mv):
    s = jnp.einsum('bthd,bmhd->bhtm', q, mk.astype(q.dtype)).astype(jnp.float32) * (MEM_HEAD_DIM ** -0.5)
    p = jax.nn.softmax(s, axis=-1).astype(q.dtype)
    o = jnp.einsum('bhtm,bmhd->bthd', p, mv.astype(q.dtype))
    return o.reshape(q.shape[0], q.shape[1], MEM_Q)


def peer(x, w_q, sub_keys, u_tab, v_tab):
    Bsz, T, D = x.shape
    n = Bsz * T
    blk = math.gcd(n, PEER_BLOCK)

    def block(xb):
        q = (xb @ w_q).reshape(blk, PEER_HEADS, 2, PEER_D_HALF)
        s = jnp.einsum('nhpd,phkd->nhpk', q, sub_keys).astype(jnp.float32)
        sv, si = lax.top_k(s, PEER_TOPK)
        cand = (sv[:, :, 0, :, None] + sv[:, :, 1, None, :]).reshape(blk, PEER_HEADS, PEER_TOPK * PEER_TOPK)
        cidx = (si[:, :, 0, :, None] * PEER_N_KEYS + si[:, :, 1, None, :]).reshape(blk, PEER_HEADS, PEER_TOPK * PEER_TOPK)
        tv, ti = lax.top_k(cand, PEER_TOPK)
        eidx = jnp.take_along_axis(cidx, ti, axis=-1)
        gates = jax.nn.softmax(tv, axis=-1)
        u = jnp.take(u_tab, eidx, axis=0)
        act = jax.nn.gelu(jnp.einsum('nd,nhkd->nhk', xb, u).astype(jnp.float32), approximate=False)
        coef = (gates * act).astype(xb.dtype)
        return jnp.einsum('nhk,nhkd->nd', coef, jnp.take(v_tab, eidx, axis=0))

    y = lax.map(block, x.reshape(n // blk, blk, D))
    return y.reshape(Bsz, T, D)


def layer(x, mk, mv, conv_buf, gdn_buf, gdn_S,
          norm_mix_g, w_in, conv_w, gdn_conv_w, gdn_a_log, gdn_dt_bias, gdn_norm_g,
          w_br_conv, w_br_gdn, w_br_mem, w_out, norm_ffn_g, peer_w_q, peer_sub_keys, peer_u, peer_v):
    Bsz, T, _ = x.shape
    h = rmsnorm(x, norm_mix_g)
    (a_u, a_b, a_c, g_q, g_k, g_v, g_z, g_a, g_b, m_q, br_gates) = _split_cols(h @ w_in)
    ya, conv_buf_new = causal_dwconv(a_c * a_u, conv_buf, conv_w)
    ya = a_b * ya
    qkv, gdn_buf_new = causal_dwconv(jnp.concatenate([g_q, g_k, g_v], axis=-1), gdn_buf, gdn_conv_w)
    qkv = jax.nn.silu(qkv)
    q = l2norm(qkv[..., :GDN_QK].reshape(Bsz, T, GDN_HEADS, GDN_DK)) * (GDN_DK ** -0.5)
    k = l2norm(qkv[..., GDN_QK:2 * GDN_QK].reshape(Bsz, T, GDN_HEADS, GDN_DK))
    v = qkv[..., 2 * GDN_QK:].reshape(Bsz, T, GDN_HEADS, GDN_DV)
    beta = jax.nn.sigmoid(g_b.astype(jnp.float32))
    g_log = -jnp.exp(gdn_a_log.astype(jnp.float32)) * jax.nn.softplus(g_a.astype(jnp.float32) + gdn_dt_bias)
    o, S_new = gated_delta_chunked(q, k, v, g_log, beta, gdn_S)
    z = g_z.reshape(Bsz, T, GDN_HEADS, GDN_DV)
    yb = (rmsnorm(o.astype(x.dtype), gdn_norm_g) * jax.nn.silu(z)).reshape(Bsz, T, GDN_V)
    ym = mem_attend(m_q.reshape(Bsz, T, MEM_HEADS, MEM_HEAD_DIM), mk, mv)
    gates = jax.nn.sigmoid(br_gates).reshape(Bsz, T, N_BRANCHES, D_MODEL)
    merged = (gates[:, :, 0] * (ya @ w_br_conv) + gates[:, :, 1] * (yb @ w_br_gdn)
              + gates[:, :, 2] * (ym @ w_br_mem))
    x = x + merged @ w_out
    x = x + peer(rmsnorm(x, norm_ffn_g), peer_w_q, peer_sub_keys, peer_u, peer_v)
    return x, conv_buf_new, gdn_buf_new, S_new


def setup_inputs(seed: int = 0) -> dict:
    key = jax.random.key(seed)
    ks = jax.random.split(key, 32)
    f32 = jnp.float32

    def nrm(k, shape, s):
        return jax.random.normal(k, shape, f32) * s

    return {
        'x_prompt': nrm(ks[0], (BATCH, SEQ, D_MODEL), 1.0),
        'x_sample': nrm(ks[1], (DEC_BATCH, DEC_SEQ, D_MODEL), 1.0),
        'mem_prompt': nrm(ks[2], (BATCH, MEM_TOKENS, D_MODEL), 1.0),
        'cache_mem_k': nrm(ks[3], (DEPTH, DEC_BATCH, MEM_TOKENS, MEM_HEADS, MEM_HEAD_DIM), 1.0),
        'cache_mem_v': nrm(ks[4], (DEPTH, DEC_BATCH, MEM_TOKENS, MEM_HEADS, MEM_HEAD_DIM), 1.0),
        'state_conv': nrm(ks[5], (DEPTH, DEC_BATCH, CONV_K - 1, CONV_WIDTH), 1.0),
        'state_gdn_conv': nrm(ks[6], (DEPTH, DEC_BATCH, GDN_CONV_K - 1, GDN_CONV_CH), 1.0),
        'state_gdn': nrm(ks[7], (DEPTH, DEC_BATCH, GDN_HEADS, GDN_DK, GDN_DV), 0.1),
        'norm_mix_g': 1.0 + nrm(ks[8], (DEPTH, D_MODEL), 0.02),
        'w_in': nrm(ks[9], (DEPTH, D_MODEL, IN_COLS), D_MODEL ** -0.5),
        'conv_w': nrm(ks[10], (DEPTH, CONV_K, CONV_WIDTH), CONV_K ** -0.5),
        'gdn_conv_w': nrm(ks[11], (DEPTH, GDN_CONV_K, GDN_CONV_CH), GDN_CONV_K ** -0.5),
        'gdn_a_log': jnp.log(jax.random.uniform(ks[12], (DEPTH, GDN_HEADS), f32, 1.0, 16.0)),
        'gdn_dt_bias': nrm(ks[13], (DEPTH, GDN_HEADS), 0.1),
        'gdn_norm_g': 1.0 + nrm(ks[14], (DEPTH, GDN_DV), 0.02),
        'mem_norm_g': 1.0 + nrm(ks[15], (DEPTH, D_MODEL), 0.02),
        'w_mem_kv': nrm(ks[16], (DEPTH, D_MODEL, 2 * MEM_Q), D_MODEL ** -0.5),
        'w_br_conv': nrm(ks[17], (DEPTH, CONV_WIDTH, D_MODEL), CONV_WIDTH ** -0.5),
        'w_br_gdn': nrm(ks[18], (DEPTH, GDN_V, D_MODEL), GDN_V ** -0.5),
        'w_br_mem': nrm(ks[19], (DEPTH, MEM_Q, D_MODEL), MEM_Q ** -0.5),
        'w_out': nrm(ks[20], (DEPTH, D_MODEL, D_MODEL), D_MODEL ** -0.5),
        'norm_ffn_g': 1.0 + nrm(ks[21], (DEPTH, D_MODEL), 0.02),
        'peer_w_q': nrm(ks[22], (DEPTH, D_MODEL, PEER_HEADS * PEER_D_KEY), D_MODEL ** -0.5),
        'peer_sub_keys': nrm(ks[23], (DEPTH, 2, PEER_HEADS, PEER_N_KEYS, PEER_D_HALF), PEER_D_HALF ** -0.5),
        'peer_u': nrm(ks[24], (DEPTH, PEER_N_EXPERTS, D_MODEL), D_MODEL ** -0.5),
        'peer_v': nrm(ks[25], (DEPTH, PEER_N_EXPERTS, D_MODEL), PEER_HEADS ** -0.5),
        'final_norm_g': 1.0 + nrm(ks[26], (D_MODEL,), 0.02),
    }


def reference(x_prompt, x_sample, mem_prompt, cache_mem_k, cache_mem_v, state_conv, state_gdn_conv, state_gdn,
              norm_mix_g, w_in, conv_w, gdn_conv_w, gdn_a_log, gdn_dt_bias, gdn_norm_g, mem_norm_g, w_mem_kv,
              w_br_conv, w_br_gdn, w_br_mem, w_out, norm_ffn_g, peer_w_q, peer_sub_keys, peer_u, peer_v,
              final_norm_g):
    xp, xs = x_prompt, x_sample
    mk_p_l, mv_p_l, cb_p_l, gb_p_l, S_p_l = [], [], [], [], []
    cb_s_l, gb_s_l, S_s_l = [], [], []
    for l in range(DEPTH):
        params = (norm_mix_g[l], w_in[l], conv_w[l], gdn_conv_w[l], gdn_a_log[l], gdn_dt_bias[l], gdn_norm_g[l],
                  w_br_conv[l], w_br_gdn[l], w_br_mem[l], w_out[l], norm_ffn_g[l], peer_w_q[l], peer_sub_keys[l],
                  peer_u[l], peer_v[l])
        mk_p, mv_p = mem_kv(mem_prompt, mem_norm_g[l], w_mem_kv[l])
        zero_cb = jnp.zeros((xp.shape[0], CONV_K - 1, CONV_WIDTH), xp.dtype)
        zero_gb = jnp.zeros((xp.shape[0], GDN_CONV_K - 1, GDN_CONV_CH), xp.dtype)
        zero_S = jnp.zeros((xp.shape[0], GDN_HEADS, GDN_DK, GDN_DV), jnp.float32)
        xp, cb_p, gb_p, S_p = layer(xp, mk_p, mv_p, zero_cb, zero_gb, zero_S, *params)
        xs, cb_s, gb_s, S_s = layer(xs, cache_mem_k[l], cache_mem_v[l], state_conv[l], state_gdn_conv[l],
                                    state_gdn[l], *params)
        mk_p_l.append(mk_p); mv_p_l.append(mv_p); cb_p_l.append(cb_p); gb_p_l.append(gb_p); S_p_l.append(S_p)
        cb_s_l.append(cb_s); gb_s_l.append(gb_s); S_s_l.append(S_s)
    y_prompt = rmsnorm(xp, final_norm_g)
    y_sample = rmsnorm(xs, final_norm_g)
    new_mem_k_prompt = jnp.stack(mk_p_l)
    new_mem_v_prompt = jnp.stack(mv_p_l)
    new_state_conv_prompt = jnp.stack(cb_p_l)
    new_state_gdn_conv_prompt = jnp.stack(gb_p_l)
    new_state_gdn_prompt = jnp.stack(S_p_l)
    new_state_conv_sample = jnp.stack(cb_s_l)
    new_state_gdn_conv_sample = jnp.stack(gb_s_l)
    new_state_gdn_sample = jnp.stack(S_s_l)
    return (y_prompt, y_sample, new_mem_k_prompt, new_mem_v_prompt, new_state_conv_prompt,
            new_state_gdn_conv_prompt, new_state_gdn_prompt, new_state_conv_sample,
            new_state_gdn_conv_sample, new_state_gdn_sample)
```

```python
import functools
import math

import jax
import jax.numpy as jnp
from jax import lax
from jax.experimental import pallas as pl
from jax.experimental.pallas import tpu as pltpu

F32 = jnp.float32
BF16 = jnp.bfloat16

EPS = 1e-6
D_MODEL = 1024
CONV_K = 3
GDN_HEADS = 8
GDN_DK = 128
GDN_DV = 128
GDN_CONV_K = 4
MEM_HEADS = 4
MEM_HEAD_DIM = 256
PEER_HEADS = 8
PEER_N_KEYS = 128
PEER_D_HALF = 128
PEER_TOPK = 16
N_BRANCHES = 3

COL_AU, COL_AB, COL_AC, COL_GQ, COL_GK, COL_GV, COL_GZ, COL_MQ, COL_BG = 0, 1, 2, 3, 4, 5, 6, 7, 8
N_MAIN_BLOCKS = 11
SMALL_COLS = 128
SUBLANES = 8
SAMPLE_PAD_T = 8
VMEM_LIMIT = 56 * 1024 * 1024


def _params(sem):
    return pltpu.CompilerParams(dimension_semantics=sem, vmem_limit_bytes=VMEM_LIMIT)


def _dot(a, b):
    return jnp.dot(a.astype(BF16), b.astype(BF16), preferred_element_type=F32)


def _dot_nt(a, b):
    return lax.dot_general(a.astype(BF16), b.astype(BF16), (((1,), (1,)), ((), ())),
                           preferred_element_type=F32)


def _split2(x):
    hi = x.astype(BF16)
    lo = (x - hi.astype(F32)).astype(BF16)
    return hi, lo


def _dot3(a, b):
    ah, al = _split2(a)
    bh, bl = _split2(b)
    d = functools.partial(jnp.dot, preferred_element_type=F32)
    return d(ah, bh) + (d(ah, bl) + d(al, bh))


def _rms_scale(x):
    return lax.rsqrt(jnp.mean(x * x, axis=-1, keepdims=True) + EPS)


def _inproj_kernel(x_ref, g_ref, w_ref, wab_ref, p_ref, pab_ref, h_ref):
    @pl.when(pl.program_id(1) == 0)
    def _():
        x = x_ref[...]
        h = ((x * _rms_scale(x)) * g_ref[...]).astype(BF16)
        h_ref[...] = h
        pab_ref[...] = jnp.dot(h, wab_ref[...], preferred_element_type=F32)

    p_ref[...] = jnp.dot(h_ref[...], w_ref[...], preferred_element_type=F32)


def _inproj(x, g, w_main, w_ab, tm):
    n, d = x.shape
    nblk = w_main.shape[1] // d
    return pl.pallas_call(
        _inproj_kernel,
        grid=(n // tm, nblk),
        in_specs=[
            pl.BlockSpec((tm, d), lambda i, j: (i, 0)),
            pl.BlockSpec((1, d), lambda i, j: (0, 0)),
            pl.BlockSpec((d, d), lambda i, j: (0, j)),
            pl.BlockSpec((d, SMALL_COLS), lambda i, j: (0, 0)),
        ],
        out_specs=[
            pl.BlockSpec((tm, d), lambda i, j: (i, j)),
            pl.BlockSpec((tm, SMALL_COLS), lambda i, j: (i, 0)),
        ],
        out_shape=[
            jax.ShapeDtypeStruct((n, nblk * d), F32),
            jax.ShapeDtypeStruct((n, SMALL_COLS), F32),
        ],
        scratch_shapes=[pltpu.VMEM((tm, d), BF16)],
        compiler_params=_params(("parallel", "arbitrary")),
        name="inproj",
    )(x, g.reshape(1, d), w_main, w_ab)


def _norm_matmul_kernel(x_ref, g_ref, w_ref, o_ref):
    x = x_ref[...]
    h = ((x * _rms_scale(x)) * g_ref[...]).astype(BF16)
    o_ref[...] = jnp.dot(h, w_ref[...], preferred_element_type=F32)


def _norm_matmul(x, g, w, tm, tn):
    n, d = x.shape
    c = w.shape[1]
    return pl.pallas_call(
        _norm_matmul_kernel,
        grid=(n // tm, c // tn),
        in_specs=[
            pl.BlockSpec((tm, d), lambda i, j: (i, 0)),
            pl.BlockSpec((1, d), lambda i, j: (0, 0)),
            pl.BlockSpec((d, tn), lambda i, j: (0, j)),
        ],
        out_specs=pl.BlockSpec((tm, tn), lambda i, j: (i, j)),
        out_shape=jax.ShapeDtypeStruct((n, c), F32),
        compiler_params=_params(("parallel", "parallel")),
        name="mem_kv",
    )(x, g.reshape(1, d), w)


def _causal_conv(prev, cur, w_ref, k):
    full = jnp.concatenate([prev, cur], axis=0)
    y = cur * w_ref[k - 1:k, :]
    for s in range(1, k):
        y = y + pltpu.roll(full, s, axis=0)[SUBLANES:, :] * w_ref[k - 1 - s:k - s, :]
    return y


def _prep_kernel(*refs, rows, tiles_per_seq, sample):
    if sample:
        (au, ab, ac, gq, gk, gv, pab, au_p, ac_p, gq_p, gk_p, gv_p, cw, gw, alog, dtb, cst, gst,
         ya_o, q_o, k_o, v_o, gb_o, ut_o) = refs
    else:
        (au, ab, ac, gq, gk, gv, pab, au_p, ac_p, gq_p, gk_p, gv_p, cw, gw, alog, dtb,
         ya_o, q_o, k_o, v_o, gb_o, ut_o) = refs
    d = D_MODEL
    first = (pl.program_id(0) % tiles_per_seq) == 0
    keep = jnp.where(first, 0.0, 1.0).astype(F32)

    u = ac[...] * au[...]
    u_prev = (ac_p[...] * au_p[...]) * keep
    if sample:
        real = (lax.broadcasted_iota(jnp.int32, (rows, 1), 0) % SAMPLE_PAD_T) >= (SAMPLE_PAD_T // 2)
        u = jnp.where(real, u, cst[...])
    ya_o[...] = (ab[...] * _causal_conv(u_prev, u, cw, CONV_K)).astype(BF16)
    ut_o[...] = u[rows - ut_o.shape[0]:, :]

    outs = (q_o, k_o, v_o)
    for idx, (cur_ref, prev_ref) in enumerate(((gq, gq_p), (gk, gk_p), (gv, gv_p))):
        cur = cur_ref[...]
        if sample:
            cur = jnp.where(real, cur, gst[:, idx * d:(idx + 1) * d])
        prev = prev_ref[...] * keep
        wslab = gw.at[:, idx * d:(idx + 1) * d]
        y = _causal_conv(prev, cur, wslab, GDN_CONV_K)
        y = y * jax.nn.sigmoid(y)
        if idx < 2:
            scale = GDN_DK ** -0.5 if idx == 0 else 1.0
            for h in range(GDN_HEADS):
                yh = y[:, h * GDN_DK:(h + 1) * GDN_DK]
                r = lax.rsqrt(jnp.sum(yh * yh, axis=-1, keepdims=True) + EPS)
                outs[idx][:, h * GDN_DK:(h + 1) * GDN_DK] = yh * (r * scale)
        else:
            outs[idx][...] = y

    pa = pab[...]
    lane = lax.broadcasted_iota(jnp.int32, pa.shape, 1)
    z = pa + dtb[...]
    softplus = jnp.maximum(z, 0.0) + jnp.log1p(jnp.exp(-jnp.abs(z)))
    g_log = -jnp.exp(alog[...]) * softplus
    gb = jnp.where(lane < GDN_HEADS, g_log, jnp.where(lane < 2 * GDN_HEADS, jax.nn.sigmoid(pa), 0.0))
    if sample:
        gb = jnp.where(real, gb, 0.0)
    gb_o[...] = gb


def _prep(p, pab, conv_w, gdn_conv_w, a_log, dt_bias, rows, tiles_per_seq, conv_state=None, gdn_state=None):
    n = p.shape[0]
    d = D_MODEL
    sample = conv_state is not None
    nt = n // rows
    rpb = rows // SUBLANES

    def cur(c):
        return pl.BlockSpec((rows, d), lambda i, c=c: (i, c))

    def prev(c):
        return pl.BlockSpec((SUBLANES, d), lambda i, c=c: (jnp.maximum(i * rpb - 1, 0), c))

    alog128 = jnp.zeros((1, SMALL_COLS), F32).at[0, :GDN_HEADS].set(a_log)
    dtb128 = jnp.zeros((1, SMALL_COLS), F32).at[0, :GDN_HEADS].set(dt_bias)
    in_specs = [cur(COL_AU), cur(COL_AB), cur(COL_AC), cur(COL_GQ), cur(COL_GK), cur(COL_GV),
                pl.BlockSpec((rows, SMALL_COLS), lambda i: (i, 0)),
                prev(COL_AU), prev(COL_AC), prev(COL_GQ), prev(COL_GK), prev(COL_GV),
                pl.BlockSpec((CONV_K, d), lambda i: (0, 0)),
                pl.BlockSpec((GDN_CONV_K, 3 * d), lambda i: (0, 0)),
                pl.BlockSpec((1, SMALL_COLS), lambda i: (0, 0)),
                pl.BlockSpec((1, SMALL_COLS), lambda i: (0, 0))]
    args = [p, p, p, p, p, p, pab, p, p, p, p, p, conv_w, gdn_conv_w, alog128, dtb128]
    if sample:
        in_specs += [pl.BlockSpec((rows, d), lambda i: (i, 0)), pl.BlockSpec((rows, 3 * d), lambda i: (i, 0))]
        args += [conv_state, gdn_state]
    tail = rows if sample else SUBLANES
    out_specs = [pl.BlockSpec((rows, d), lambda i: (i, 0))] * 4 + [
        pl.BlockSpec((rows, SMALL_COLS), lambda i: (i, 0)),
        pl.BlockSpec((tail, d), lambda i: (i, 0))]
    out_shape = [jax.ShapeDtypeStruct((n, d), BF16)] + [jax.ShapeDtypeStruct((n, d), F32)] * 3 + [
        jax.ShapeDtypeStruct((n, SMALL_COLS), F32),
        jax.ShapeDtypeStruct((nt * tail, d), F32)]
    return pl.pallas_call(
        functools.partial(_prep_kernel, rows=rows, tiles_per_seq=tiles_per_seq, sample=sample),
        grid=(nt,),
        in_specs=in_specs,
        out_specs=out_specs,
        out_shape=out_shape,
        compiler_params=_params(("parallel",)),
        name="prep",
    )(*args)


def _unit_lower_inverse(l_mat, c):
    eye = (lax.broadcasted_iota(jnp.int32, (c, c), 0) == lax.broadcasted_iota(jnp.int32, (c, c), 1)).astype(F32)
    neg = -l_mat
    t = eye + neg
    span = 2
    pw = _dot3(neg, neg) if span < c else None
    while span < c:
        span *= 2
        if span < c:
            both = _dot3(pw, jnp.concatenate([pw, t], axis=1))
            pw = both[:, :c]
            t = t + both[:, c:]
        else:
            t = t + _dot3(pw, t)
    return t


def _gdn_kernel(*refs, c, has_s0):
    if has_s0:
        q_ref, k_ref, v_ref, z_ref, gb_ref, ng_ref, s0_ref, yb_ref, sout_ref, s_scr = refs
    else:
        q_ref, k_ref, v_ref, z_ref, gb_ref, ng_ref, yb_ref, sout_ref, s_scr = refs
    ci = pl.program_id(1)

    @pl.when(ci == 0)
    def _():
        if has_s0:
            s_scr[...] = s0_ref[0]
        else:
            s_scr[...] = jnp.zeros_like(s_scr)

    gb = gb_ref[...]
    row = lax.broadcasted_iota(jnp.int32, gb.shape, 0)
    gc = gb
    s = 1
    while s < c:
        gc = gc + jnp.where(row >= s, pltpu.roll(gc, s, axis=0), 0.0)
        s *= 2
    gct = gc.T
    g_last = gc[c - 1:c, :]
    e_g = jnp.exp(gc)
    e_kd = jnp.exp(g_last - gc)
    e_last = jnp.exp(g_last)
    ri = lax.broadcasted_iota(jnp.int32, (c, c), 0)
    cj = lax.broadcasted_iota(jnp.int32, (c, c), 1)
    incl = ri >= cj
    strict = ri > cj
    ng = ng_ref[...]

    for h in range(GDN_HEADS):
        sl = slice(h * GDN_DK, (h + 1) * GDN_DK)
        qh, kh, vh = q_ref[:, sl], k_ref[:, sl], v_ref[:, sl]
        beta = gb[:, GDN_HEADS + h:GDN_HEADS + h + 1]
        decay = jnp.exp(jnp.where(incl, gc[:, h:h + 1] - gct[h:h + 1, :], -jnp.inf))
        kb = kh * beta
        aq = _dot_nt(jnp.concatenate([kb, qh], axis=0), kh)
        l_mat = jnp.where(strict, aq[:c] * decay, 0.0)
        attn = jnp.where(incl, aq[c:] * decay, 0.0)
        t_mat = _unit_lower_inverse(l_mat, c)
        uw = _dot(t_mat, jnp.concatenate([vh * beta, kb * e_g[:, h:h + 1]], axis=1))
        u_val, w_k = uw[:, :GDN_DV], uw[:, GDN_DV:]
        q_dec = qh * e_g[:, h:h + 1]
        k_dec = kh * e_kd[:, h:h + 1]
        s_old = s_scr[h]
        ws = _dot(jnp.concatenate([w_k, q_dec], axis=0), s_old)
        v_new = u_val - ws[:c]
        o = ws[c:] + _dot(attn, v_new)
        s_scr[h] = s_old * e_last[:, h:h + 1] + _dot(k_dec.T, v_new)
        zh = z_ref[:, sl]
        yb_ref[:, sl] = (((o * _rms_scale(o)) * ng) * (zh * jax.nn.sigmoid(zh))).astype(BF16)

    @pl.when(ci == pl.num_programs(1) - 1)
    def _():
        sout_ref[0] = s_scr[...]


def _gdn(q, k, v, p, gb, norm_g, bsz, t, c, s0=None):
    d = D_MODEL
    nc = t // c
    has_s0 = s0 is not None

    def tok(col=0):
        return pl.BlockSpec((c, d), lambda b, i, col=col: (b * nc + i, col))

    in_specs = [tok(), tok(), tok(), tok(COL_GZ),
                pl.BlockSpec((c, SMALL_COLS), lambda b, i: (b * nc + i, 0)),
                pl.BlockSpec((1, GDN_DV), lambda b, i: (0, 0))]
    args = [q, k, v, p, gb, norm_g.reshape(1, GDN_DV)]
    s_spec = pl.BlockSpec((1, GDN_HEADS, GDN_DK, GDN_DV), lambda b, i: (b, 0, 0, 0))
    if has_s0:
        in_specs.append(s_spec)
        args.append(s0)
    return pl.pallas_call(
        functools.partial(_gdn_kernel, c=c, has_s0=has_s0),
        grid=(bsz, nc),
        in_specs=in_specs,
        out_specs=[tok(), s_spec],
        out_shape=[jax.ShapeDtypeStruct((bsz * t, d), BF16),
                   jax.ShapeDtypeStruct((bsz, GDN_HEADS, GDN_DK, GDN_DV), F32)],
        scratch_shapes=[pltpu.VMEM((GDN_HEADS, GDN_DK, GDN_DV), F32)],
        compiler_params=_params(("parallel", "arbitrary")),
        name="gdn_scan",
    )(*args)


def _mem_attn_kernel(q_ref, k_ref, v_ref, o_ref):
    scale = MEM_HEAD_DIM ** -0.5
    for h in range(MEM_HEADS):
        sl = slice(h * MEM_HEAD_DIM, (h + 1) * MEM_HEAD_DIM)
        s = _dot_nt(q_ref[:, sl], k_ref[0, :, sl]) * scale
        e = jnp.exp(s - jnp.max(s, axis=-1, keepdims=True))
        prob = e / jnp.sum(e, axis=-1, keepdims=True)
        o_ref[:, sl] = _dot(prob, v_ref[0, :, sl]).astype(BF16)


def _mem_attn(p, mk, mv, bsz, t, tq):
    d = D_MODEL
    m = mk.shape[1]
    nq = t // tq
    kv_spec = pl.BlockSpec((1, m, d), lambda b, i: (b, 0, 0))
    return pl.pallas_call(
        _mem_attn_kernel,
        grid=(bsz, nq),
        in_specs=[pl.BlockSpec((tq, d), lambda b, i: (b * nq + i, COL_MQ)), kv_spec, kv_spec],
        out_specs=pl.BlockSpec((tq, d), lambda b, i: (b * nq + i, 0)),
        out_shape=jax.ShapeDtypeStruct((bsz * t, d), BF16),
        compiler_params=_params(("parallel", "arbitrary")),
        name="mem_attn",
    )(p, mk, mv)


def _merge_kernel(x_ref, ya_ref, yb_ref, ym_ref, g0_ref, g1_ref, g2_ref, wc_ref, wg_ref, wm_ref, wo_ref,
                  ng_ref, x1_ref, h2_ref):
    d = functools.partial(jnp.dot, preferred_element_type=F32)
    merged = (jax.nn.sigmoid(g0_ref[...]) * d(ya_ref[...], wc_ref[...])
              + jax.nn.sigmoid(g1_ref[...]) * d(yb_ref[...], wg_ref[...])
              + jax.nn.sigmoid(g2_ref[...]) * d(ym_ref[...], wm_ref[...]))
    x1 = x_ref[...] + d(merged.astype(BF16), wo_ref[...])
    x1_ref[...] = x1
    h2_ref[...] = ((x1 * _rms_scale(x1)) * ng_ref[...]).astype(BF16)


def _merge(x, ya, yb, ym, p, wc, wg, wm, wo, norm_g, tm):
    n, d = x.shape
    tok = pl.BlockSpec((tm, d), lambda i: (i, 0))
    wsp = pl.BlockSpec((d, d), lambda i: (0, 0))

    def gate(j):
        return pl.BlockSpec((tm, d), lambda i, j=j: (i, COL_BG + j))

    return pl.pallas_call(
        _merge_kernel,
        grid=(n // tm,),
        in_specs=[tok, tok, tok, tok, gate(0), gate(1), gate(2), wsp, wsp, wsp, wsp,
                  pl.BlockSpec((1, d), lambda i: (0, 0))],
        out_specs=[tok, tok],
        out_shape=[jax.ShapeDtypeStruct((n, d), F32), jax.ShapeDtypeStruct((n, d), BF16)],
        compiler_params=_params(("parallel",)),
        name="merge",
    )(x, ya, yb, ym, p, p, p, wc, wg, wm, wo, norm_g.reshape(1, d))


_CAND_CELLS = tuple((a, b) for a in range(PEER_TOPK) for b in range(PEER_TOPK) if (a + 1) * (b + 1) <= PEER_TOPK)


def _top_values(x, count):
    vals = []
    cur = x
    for r in range(count):
        m = jnp.max(cur, axis=0, keepdims=True)
        vals.append(m)
        if r + 1 < count:
            cur = jnp.where(cur == m, -jnp.inf, cur)
    return vals


def _route_kernel(h_ref, wq_ref, keys_ref, s0_ref, e0_ref, s1_ref, e1_ref, thr_ref):
    q = jnp.dot(h_ref[...], wq_ref[...], preferred_element_type=F32).astype(BF16)
    thr_rows = []
    for h in range(PEER_HEADS):
        sc, top = [], []
        for part in range(2):
            col = (2 * h + part) * PEER_D_HALF
            s_t = lax.dot_general(keys_ref[part, h], q[:, col:col + PEER_D_HALF],
                                  (((1,), (1,)), ((), ())), preferred_element_type=F32)
            sc.append(s_t)
            top.append(_top_values(s_t, PEER_TOPK))
        cand = jnp.concatenate([top[0][a] + top[1][b] for a, b in _CAND_CELLS], axis=0)
        best = _top_values(cand, PEER_TOPK)
        c_max, thr = best[0], best[-1]
        z = jnp.exp(best[0] - c_max)
        for r in range(1, PEER_TOPK):
            z = z + jnp.exp(best[r] - c_max)
        s0_ref[h] = sc[0]
        s1_ref[h] = sc[1]
        e0_ref[h] = jnp.exp(sc[0] - top[0][0]) / z
        e1_ref[h] = jnp.exp(sc[1] - top[1][0])
        thr_rows.append(thr)
    thr_ref[...] = jnp.concatenate(thr_rows, axis=0)


def _route(h2, wq, keys, tn):
    n, d = h2.shape
    nk = PEER_N_KEYS
    big = pl.BlockSpec((PEER_HEADS, nk, tn), lambda i: (0, 0, i))
    big_shape = jax.ShapeDtypeStruct((PEER_HEADS, nk, n), F32)
    return pl.pallas_call(
        _route_kernel,
        grid=(n // tn,),
        in_specs=[pl.BlockSpec((tn, d), lambda i: (i, 0)),
                  pl.BlockSpec(wq.shape, lambda i: (0, 0)),
                  pl.BlockSpec(keys.shape, lambda i: (0, 0, 0, 0))],
        out_specs=[big, big, big, big, pl.BlockSpec((PEER_HEADS, tn), lambda i: (0, i))],
        out_shape=[big_shape, big_shape, big_shape, big_shape, jax.ShapeDtypeStruct((PEER_HEADS, n), F32)],
        compiler_params=_params(("parallel",)),
        name="peer_route",
    )(h2, wq, keys)


def _gelu(x):
    return 0.5 * x * (1.0 + lax.erf(x * (2.0 ** -0.5)))


def _peer_kernel(h_ref, x1_ref, u_ref, vt_ref, s0_ref, e0_ref, s1_ref, e1_ref, thr_ref, ng_ref,
                 y_ref, acc_ref, coef_ref, *, te):
    ei = pl.program_id(1)

    @pl.when(ei == 0)
    def _():
        acc_ref[...] = jnp.zeros_like(acc_ref)

    act = _gelu(lax.dot_general(u_ref[...], h_ref[...], (((1,), (1,)), ((), ())),
                                preferred_element_type=F32))
    groups = te // PEER_N_KEYS
    for ii in range(groups):
        i_key = ei * groups + ii
        gate = None
        for h in range(PEER_HEADS):
            s0_row = s0_ref[h, pl.ds(i_key, 1), :]
            e0_row = e0_ref[h, pl.ds(i_key, 1), :]
            sel = (s0_row + s1_ref[h]) >= thr_ref[h:h + 1, :]
            term = jnp.where(sel, e0_row * e1_ref[h], 0.0)
            gate = term if gate is None else gate + term
        rows = slice(ii * PEER_N_KEYS, (ii + 1) * PEER_N_KEYS)
        coef_ref[rows, :] = (gate * act[rows, :]).astype(BF16)
    acc_ref[...] += jnp.dot(vt_ref[...], coef_ref[...], preferred_element_type=F32)

    @pl.when(ei == pl.num_programs(1) - 1)
    def _():
        x2 = x1_ref[...] + acc_ref[...].T
        y_ref[...] = (x2 * _rms_scale(x2)) * ng_ref[...]


def _peer(h2, x1, u_bf, vt_bf, s0, e0, s1, e1, thr, norm_g, tn, te):
    n, d = h2.shape
    ne = u_bf.shape[0]
    nk = PEER_N_KEYS
    tok = pl.BlockSpec((tn, d), lambda i, e: (i, 0))
    big = pl.BlockSpec((PEER_HEADS, nk, tn), lambda i, e: (0, 0, i))
    return pl.pallas_call(
        functools.partial(_peer_kernel, te=te),
        grid=(n // tn, ne // te),
        in_specs=[tok, tok,
                  pl.BlockSpec((te, d), lambda i, e: (e, 0)),
                  pl.BlockSpec((d, te), lambda i, e: (0, e)),
                  big, big, big, big,
                  pl.BlockSpec((PEER_HEADS, tn), lambda i, e: (0, i)),
                  pl.BlockSpec((1, d), lambda i, e: (0, 0))],
        out_specs=tok,
        out_shape=jax.ShapeDtypeStruct((n, d), F32),
        scratch_shapes=[pltpu.VMEM((d, tn), F32), pltpu.VMEM((te, tn), BF16)],
        compiler_params=_params(("parallel", "arbitrary")),
        name="peer_mix",
    )(h2, x1, u_bf, vt_bf, s0, e0, s1, e1, thr, norm_g.reshape(1, d))


def _tile(n, pref):
    return math.gcd(n, pref)


def _layer(x, mk, mv, bsz, t, weights, conv_state=None, gdn_conv_state=None, gdn_s0=None):
    (norm_mix_g, w_main, w_ab, conv_w, gdn_conv_w, a_log, dt_bias, gdn_norm_g, wc, wg, wm, wo, norm_ffn_g,
     wq, keys, u_bf, vt_bf, final_norm_g) = weights
    n = bsz * t
    sample = conv_state is not None
    p, pab = _inproj(x, norm_mix_g, w_main, w_ab, _tile(n, 1024))
    rows = _tile(n, 256) if sample else _tile(t, 512)
    ya, q, k, v, gb, u_tail = _prep(p, pab, conv_w, gdn_conv_w, a_log, dt_bias, rows, max(t // rows, 1),
                                    conv_state, gdn_conv_state)
    chunk = _tile(t, 64)
    yb, s_new = _gdn(q, k, v, p, gb, gdn_norm_g, bsz, t, chunk, gdn_s0)
    ym = _mem_attn(p, mk, mv, bsz, t, _tile(t, 512))
    x1, h2 = _merge(x, ya, yb, ym, p, wc, wg, wm, wo, norm_ffn_g, _tile(n, 256))
    tn = _tile(n, 512)
    s0, e0, s1, e1, thr = _route(h2, wq, keys, tn)
    y = _peer(h2, x1, u_bf, vt_bf, s0, e0, s1, e1, thr, final_norm_g, tn, 1024)
    return y, u_tail, p, s_new


def kernel(x_prompt, x_sample, mem_prompt, cache_mem_k, cache_mem_v, state_conv, state_gdn_conv, state_gdn, norm_mix_g, w_in, conv_w, gdn_conv_w, gdn_a_log, gdn_dt_bias, gdn_norm_g, mem_norm_g, w_mem_kv, w_br_conv, w_br_gdn, w_br_mem, w_out, norm_ffn_g, peer_w_q, peer_sub_keys, peer_u, peer_v, final_norm_g):
    depth = w_in.shape[0]
    assert depth == 1, "single-layer step"
    d = D_MODEL
    bp, tp, _ = x_prompt.shape
    bs, ts, _ = x_sample.shape
    assert ts == SAMPLE_PAD_T // 2
    m_tok = mem_prompt.shape[1]
    mq = MEM_HEADS * MEM_HEAD_DIM
    l = 0

    w = w_in[l]
    ga0 = 7 * d
    w_main = jnp.concatenate([w[:, :ga0], w[:, ga0 + 2 * GDN_HEADS:]], axis=1).astype(BF16)
    w_ab = jnp.pad(w[:, ga0:ga0 + 2 * GDN_HEADS], ((0, 0), (0, SMALL_COLS - 2 * GDN_HEADS))).astype(BF16)
    weights = (norm_mix_g[l], w_main, w_ab, conv_w[l], gdn_conv_w[l], gdn_a_log[l], gdn_dt_bias[l], gdn_norm_g[l],
               w_br_conv[l].astype(BF16), w_br_gdn[l].astype(BF16), w_br_mem[l].astype(BF16), w_out[l].astype(BF16),
               norm_ffn_g[l], peer_w_q[l].astype(BF16), peer_sub_keys[l].astype(BF16),
               peer_u[l].astype(BF16), peer_v[l].T.astype(BF16), final_norm_g)

    kv = _norm_matmul(mem_prompt.reshape(bp * m_tok, d), mem_norm_g[l], w_mem_kv[l].astype(BF16), 512, 1024)
    mk_p, mv_p = kv[:, :mq], kv[:, mq:]
    y_p, u_tail_p, p_p, s_p = _layer(x_prompt.reshape(bp * tp, d), mk_p.reshape(bp, m_tok, mq),
                                     mv_p.reshape(bp, m_tok, mq), bp, tp, weights)
    rows_p = _tile(tp, 512)
    cb_p = u_tail_p.reshape(bp, tp // rows_p, SUBLANES, d)[:, -1, SUBLANES - (CONV_K - 1):, :]
    gb_p = p_p.reshape(bp, tp, -1)[:, tp - (GDN_CONV_K - 1):, COL_GQ * d:(COL_GV + 1) * d]

    pad_t = SAMPLE_PAD_T
    lead = pad_t - ts
    x_s = jnp.pad(x_sample, ((0, 0), (lead, 0), (0, 0))).reshape(bs * pad_t, d)
    cst = jnp.pad(state_conv[l], ((0, 0), (lead - (CONV_K - 1), ts), (0, 0))).reshape(bs * pad_t, d)
    gst = jnp.pad(state_gdn_conv[l], ((0, 0), (lead - (GDN_CONV_K - 1), ts), (0, 0))).reshape(bs * pad_t, 3 * d)
    y_s, u_s, p_s, s_s = _layer(x_s, cache_mem_k[l].reshape(bs, -1, mq), cache_mem_v[l].reshape(bs, -1, mq),
                                bs, pad_t, weights, cst, gst, state_gdn[l])
    y_s = y_s.reshape(bs, pad_t, d)[:, lead:, :]
    cb_s = u_s.reshape(bs, pad_t, d)[:, pad_t - (CONV_K - 1):, :]
    gb_s = p_s.reshape(bs, pad_t, -1)[:, pad_t - (GDN_CONV_K - 1):, COL_GQ * d:(COL_GV + 1) * d]

    mshape = (depth, bp, m_tok, MEM_HEADS, MEM_HEAD_DIM)
    return (y_p.reshape(bp, tp, d), y_s,
            mk_p.reshape(mshape), mv_p.reshape(mshape),
            cb_p[None], gb_p[None], s_p[None],
            cb_s[None], gb_s[None], s_s[None])
```

```python
import functools
import math

import jax
import jax.numpy as jnp
from jax import lax
from jax.experimental import pallas as pl
from jax.experimental.pallas import tpu as pltpu

F32 = jnp.float32
BF16 = jnp.bfloat16

EPS = 1e-6
D_MODEL = 1024
CONV_K = 3
GDN_HEADS = 8
GDN_DK = 128
GDN_DV = 128
GDN_CONV_K = 4
MEM_HEADS = 4
MEM_HEAD_DIM = 256
PEER_HEADS = 8
PEER_N_KEYS = 128
PEER_D_HALF = 128
PEER_TOPK = 16
N_BRANCHES = 3

COL_AU, COL_AB, COL_AC, COL_GQ, COL_GK, COL_GV, COL_GZ, COL_MQ, COL_BG = 0, 1, 2, 3, 4, 5, 6, 7, 8
N_MAIN_BLOCKS = 11
SMALL_COLS = 128
SUBLANES = 8
LANES = 128
SAMPLE_PAD_T = 8
VMEM_LIMIT = 56 * 1024 * 1024


def _params(sem):
    return pltpu.CompilerParams(dimension_semantics=sem, vmem_limit_bytes=VMEM_LIMIT)


def _dot(a, b):
    return jnp.dot(a.astype(BF16), b.astype(BF16), preferred_element_type=F32)


def _dot_nt(a, b):
    return lax.dot_general(a.astype(BF16), b.astype(BF16), (((1,), (1,)), ((), ())),
                           preferred_element_type=F32)


def _rms_scale(x):
    return lax.rsqrt(jnp.mean(x * x, axis=-1, keepdims=True) + EPS)


def _inproj_kernel(x_ref, g_ref, w_ref, wab_ref, p_ref, pab_ref, h_ref):
    @pl.when(pl.program_id(1) == 0)
    def _():
        x = x_ref[...]
        h = ((x * _rms_scale(x)) * g_ref[...]).astype(BF16)
        h_ref[...] = h
        pab_ref[...] = jnp.dot(h, wab_ref[...], preferred_element_type=F32)

    p_ref[...] = jnp.dot(h_ref[...], w_ref[...], preferred_element_type=F32)


def _inproj(x, g, w_main, w_ab, tm):
    n, d = x.shape
    nblk = w_main.shape[1] // d
    return pl.pallas_call(
        _inproj_kernel,
        grid=(n // tm, nblk),
        in_specs=[
            pl.BlockSpec((tm, d), lambda i, j: (i, 0)),
            pl.BlockSpec((1, d), lambda i, j: (0, 0)),
            pl.BlockSpec((d, d), lambda i, j: (0, j)),
            pl.BlockSpec((d, SMALL_COLS), lambda i, j: (0, 0)),
        ],
        out_specs=[
            pl.BlockSpec((tm, d), lambda i, j: (i, j)),
            pl.BlockSpec((tm, SMALL_COLS), lambda i, j: (i, 0)),
        ],
        out_shape=[
            jax.ShapeDtypeStruct((n, nblk * d), F32),
            jax.ShapeDtypeStruct((n, SMALL_COLS), F32),
        ],
        scratch_shapes=[pltpu.VMEM((tm, d), BF16)],
        compiler_params=_params(("parallel", "arbitrary")),
        name="inproj",
    )(x, g.reshape(1, d), w_main, w_ab)


def _norm_matmul_kernel(x_ref, g_ref, w_ref, o_ref):
    x = x_ref[...]
    h = ((x * _rms_scale(x)) * g_ref[...]).astype(BF16)
    o_ref[...] = jnp.dot(h, w_ref[...], preferred_element_type=F32)


def _norm_matmul(x, g, w, tm, tn):
    n, d = x.shape
    c = w.shape[1]
    return pl.pallas_call(
        _norm_matmul_kernel,
        grid=(n // tm, c // tn),
        in_specs=[
            pl.BlockSpec((tm, d), lambda i, j: (i, 0)),
            pl.BlockSpec((1, d), lambda i, j: (0, 0)),
            pl.BlockSpec((d, tn), lambda i, j: (0, j)),
        ],
        out_specs=pl.BlockSpec((tm, tn), lambda i, j: (i, j)),
        out_shape=jax.ShapeDtypeStruct((n, c), F32),
        compiler_params=_params(("parallel", "parallel")),
        name="mem_kv",
    )(x, g.reshape(1, d), w)


def _causal_conv(prev, cur, w_ref, k):
    full = jnp.concatenate([prev, cur], axis=0)
    y = cur * w_ref[k - 1:k, :]
    for s in range(1, k):
        y = y + pltpu.roll(full, s, axis=0)[SUBLANES:, :] * w_ref[k - 1 - s:k - s, :]
    return y


def _prep_kernel(*refs, rows, tiles_per_seq, sample):
    if sample:
        (au, ab, ac, gq, gk, gv, pab, au_p, ac_p, gq_p, gk_p, gv_p, cw, gw, alog, dtb, cst, gst,
         ya_o, q_o, k_o, v_o, gb_o, ut_o) = refs
    else:
        (au, ab, ac, gq, gk, gv, pab, au_p, ac_p, gq_p, gk_p, gv_p, cw, gw, alog, dtb,
         ya_o, q_o, k_o, v_o, gb_o, ut_o) = refs
    d = D_MODEL
    first = (pl.program_id(0) % tiles_per_seq) == 0
    keep = jnp.where(first, 0.0, 1.0).astype(F32)

    u = ac[...] * au[...]
    u_prev = (ac_p[...] * au_p[...]) * keep
    if sample:
        real = (lax.broadcasted_iota(jnp.int32, (rows, 1), 0) % SAMPLE_PAD_T) >= (SAMPLE_PAD_T // 2)
        u = jnp.where(real, u, cst[...])
    ya_o[...] = (ab[...] * _causal_conv(u_prev, u, cw, CONV_K)).astype(BF16)
    ut_o[...] = u[rows - ut_o.shape[0]:, :]

    outs = (q_o, k_o, v_o)
    for idx, (cur_ref, prev_ref) in enumerate(((gq, gq_p), (gk, gk_p), (gv, gv_p))):
        cur = cur_ref[...]
        if sample:
            cur = jnp.where(real, cur, gst[:, idx * d:(idx + 1) * d])
        prev = prev_ref[...] * keep
        wslab = gw.at[:, idx * d:(idx + 1) * d]
        y = _causal_conv(prev, cur, wslab, GDN_CONV_K)
        y = y * jax.nn.sigmoid(y)
        if idx < 2:
            scale = GDN_DK ** -0.5 if idx == 0 else 1.0
            for h in range(GDN_HEADS):
                yh = y[:, h * GDN_DK:(h + 1) * GDN_DK]
                r = lax.rsqrt(jnp.sum(yh * yh, axis=-1, keepdims=True) + EPS)
                outs[idx][:, h * GDN_DK:(h + 1) * GDN_DK] = yh * (r * scale)
        else:
            outs[idx][...] = y

    pa = pab[...]
    lane = lax.broadcasted_iota(jnp.int32, pa.shape, 1)
    z = pa + dtb[...]
    softplus = jnp.maximum(z, 0.0) + jnp.log1p(jnp.exp(-jnp.abs(z)))
    g_log = -jnp.exp(alog[...]) * softplus
    gb = jnp.where(lane < GDN_HEADS, g_log, jnp.where(lane < 2 * GDN_HEADS, jax.nn.sigmoid(pa), 0.0))
    if sample:
        gb = jnp.where(real, gb, 0.0)
    gb_o[...] = gb


def _prep(p, pab, conv_w, gdn_conv_w, a_log, dt_bias, rows, tiles_per_seq, conv_state=None, gdn_state=None):
    n = p.shape[0]
    d = D_MODEL
    sample = conv_state is not None
    nt = n // rows
    rpb = rows // SUBLANES

    def cur(c):
        return pl.BlockSpec((rows, d), lambda i, c=c: (i, c))

    def prev(c):
        return pl.BlockSpec((SUBLANES, d), lambda i, c=c: (jnp.maximum(i * rpb - 1, 0), c))

    alog128 = jnp.zeros((1, SMALL_COLS), F32).at[0, :GDN_HEADS].set(a_log)
    dtb128 = jnp.zeros((1, SMALL_COLS), F32).at[0, :GDN_HEADS].set(dt_bias)
    in_specs = [cur(COL_AU), cur(COL_AB), cur(COL_AC), cur(COL_GQ), cur(COL_GK), cur(COL_GV),
                pl.BlockSpec((rows, SMALL_COLS), lambda i: (i, 0)),
                prev(COL_AU), prev(COL_AC), prev(COL_GQ), prev(COL_GK), prev(COL_GV),
                pl.BlockSpec((CONV_K, d), lambda i: (0, 0)),
                pl.BlockSpec((GDN_CONV_K, 3 * d), lambda i: (0, 0)),
                pl.BlockSpec((1, SMALL_COLS), lambda i: (0, 0)),
                pl.BlockSpec((1, SMALL_COLS), lambda i: (0, 0))]
    args = [p, p, p, p, p, p, pab, p, p, p, p, p, conv_w, gdn_conv_w, alog128, dtb128]
    if sample:
        in_specs += [pl.BlockSpec((rows, d), lambda i: (i, 0)), pl.BlockSpec((rows, 3 * d), lambda i: (i, 0))]
        args += [conv_state, gdn_state]
    tail = rows if sample else SUBLANES
    out_specs = [pl.BlockSpec((rows, d), lambda i: (i, 0))] * 4 + [
        pl.BlockSpec((rows, SMALL_COLS), lambda i: (i, 0)),
        pl.BlockSpec((tail, d), lambda i: (i, 0))]
    out_shape = [jax.ShapeDtypeStruct((n, d), BF16)] + [jax.ShapeDtypeStruct((n, d), F32)] * 3 + [
        jax.ShapeDtypeStruct((n, SMALL_COLS), F32),
        jax.ShapeDtypeStruct((nt * tail, d), F32)]
    return pl.pallas_call(
        functools.partial(_prep_kernel, rows=rows, tiles_per_seq=tiles_per_seq, sample=sample),
        grid=(nt,),
        in_specs=in_specs,
        out_specs=out_specs,
        out_shape=out_shape,
        compiler_params=_params(("parallel",)),
        name="prep",
    )(*args)


def _chunk_cumsum(g, c):
    row = lax.broadcasted_iota(jnp.int32, g.shape, 0)
    sh = 1
    while sh < c:
        g = g + jnp.where(row >= sh, pltpu.roll(g, sh, axis=0), 0.0)
        sh *= 2
    return g


def _gdn_stacked_kernel(q_ref, k_ref, v_ref, z_ref, gb_ref, ng_ref, yb_ref, sout_ref, s_scr, *, c, grp, nb):
    ci = pl.program_id(1)

    @pl.when(ci == 0)
    def _():
        s_scr[...] = jnp.zeros_like(s_scr)

    r = grp * c
    ng = ng_ref[...]
    seqs = []
    for s in range(nb):
        gb = gb_ref[s]
        gc = _chunk_cumsum(gb, c)
        g_last = gc[c - 1:c, :]
        seqs.append(dict(gb=gb, gc=gc, gct=gc.T, e_g=jnp.exp(gc), e_kd=jnp.exp(g_last - gc), e_last=jnp.exp(g_last)))

    ri = lax.broadcasted_iota(jnp.int32, (r, r), 0)
    cj = lax.broadcasted_iota(jnp.int32, (r, r), 1)
    same = (ri // c) == (cj // c)
    incl = same & (ri >= cj)
    strict = same & (ri > cj)
    head_block = (lax.broadcasted_iota(jnp.int32, (r, grp * GDN_DK), 0) // c
                  == lax.broadcasted_iota(jnp.int32, (r, grp * GDN_DK), 1) // GDN_DK)

    def block_layout(x):
        return jnp.where(head_block, jnp.concatenate([x] * grp, axis=1), 0.0)

    units = []
    for s in range(nb):
        for g0 in range(0, GDN_HEADS, grp):
            heads = list(range(g0, g0 + grp))
            sq = seqs[s]

            def stack(ref, s=s, heads=heads):
                return jnp.concatenate([ref[s, :, h * GDN_DK:(h + 1) * GDN_DK] for h in heads], axis=0)

            def col(a, off=0, heads=heads):
                return jnp.concatenate([a[:, off + h:off + h + 1] for h in heads], axis=0)

            qs, ks, vs = stack(q_ref), stack(k_ref), stack(v_ref)
            beta = col(sq["gb"], GDN_HEADS)
            kb = ks * beta
            units.append(dict(s=s, heads=heads, qs=qs, ks=ks, kb=kb, vb=vs * beta,
                              gcol=col(sq["gc"]), egcol=col(sq["e_g"]), ekdcol=col(sq["e_kd"]),
                              grow=jnp.concatenate([sq["gct"][h:h + 1, :] for h in heads], axis=1),
                              rows=slice((s * GDN_HEADS + g0) * GDN_DK, (s * GDN_HEADS + g0 + grp) * GDN_DK)))

    for u in units:
        u["aq"] = _dot_nt(jnp.concatenate([u["kb"], u["qs"]], axis=0), u["ks"])
    for u in units:
        decay = jnp.exp(jnp.where(incl, u["gcol"] - u["grow"], -jnp.inf))
        u["pw"] = -jnp.where(strict, u["aq"][:r] * decay, 0.0)
        u["attn"] = jnp.where(incl, u["aq"][r:] * decay, 0.0)
        u["x"] = jnp.concatenate([u["vb"], u["kb"] * u["egcol"]], axis=1)
    span = 1
    while span < c:
        span *= 2
        for u in units:
            if span < c:
                both = _dot(u["pw"], jnp.concatenate([u["pw"], u["x"]], axis=1))
                u["pw"] = both[:, :r]
                u["x"] = u["x"] + both[:, r:]
            else:
                u["x"] = u["x"] + _dot(u["pw"], u["x"])
    for u in units:
        u_val, w_k = u["x"][:, :GDN_DV], u["x"][:, GDN_DV:]
        u["s_old"] = s_scr[u["rows"], :]
        ws = _dot(jnp.concatenate([block_layout(w_k), block_layout(u["qs"] * u["egcol"])], axis=0), u["s_old"])
        u["v_new"] = u_val - ws[:r]
        u["o"] = ws[r:]
    for u in units:
        e_last = seqs[u["s"]]["e_last"]
        o = u["o"] + _dot(u["attn"], u["v_new"])
        e_rows = jnp.concatenate([jnp.broadcast_to(e_last[:, h:h + 1], (GDN_DK, 1)) for h in u["heads"]], axis=0)
        kd_t = block_layout(u["ks"] * u["ekdcol"]).T
        s_scr[u["rows"], :] = u["s_old"] * e_rows + _dot(kd_t, u["v_new"])
        for i, h in enumerate(u["heads"]):
            sl = slice(h * GDN_DK, (h + 1) * GDN_DK)
            oh = o[i * c:(i + 1) * c, :]
            zh = z_ref[u["s"], :, sl]
            yb_ref[u["s"], :, sl] = (((oh * _rms_scale(oh)) * ng) * (zh * jax.nn.sigmoid(zh))).astype(BF16)

    @pl.when(ci == pl.num_programs(1) - 1)
    def _():
        sout_ref[...] = s_scr[...]


def _gdn_stacked(q, k, v, p, gb, norm_g, bsz, t, c, nb):
    d = D_MODEL
    nc = t // c
    grp = min(GDN_HEADS, 256 // c)
    s_rows = nb * GDN_HEADS * GDN_DK

    def tok(col=0):
        return pl.BlockSpec((nb, c, d), lambda b, i, col=col: (b, i, col))

    yb, s_new = pl.pallas_call(
        functools.partial(_gdn_stacked_kernel, c=c, grp=grp, nb=nb),
        grid=(bsz // nb, nc),
        in_specs=[tok(), tok(), tok(), tok(COL_GZ),
                  pl.BlockSpec((nb, c, SMALL_COLS), lambda b, i: (b, i, 0)),
                  pl.BlockSpec((1, GDN_DV), lambda b, i: (0, 0))],
        out_specs=[tok(), pl.BlockSpec((s_rows, GDN_DV), lambda b, i: (b, 0))],
        out_shape=[jax.ShapeDtypeStruct((bsz, t, d), BF16),
                   jax.ShapeDtypeStruct((bsz * GDN_HEADS * GDN_DK, GDN_DV), F32)],
        scratch_shapes=[pltpu.VMEM((s_rows, GDN_DV), F32)],
        compiler_params=_params(("parallel", "arbitrary")),
        name="gdn_scan_prompt",
    )(*[a.reshape(bsz, t, -1) for a in (q, k, v, p, gb)], norm_g.reshape(1, GDN_DV))
    return yb.reshape(bsz * t, d), s_new.reshape(bsz, GDN_HEADS, GDN_DK, GDN_DV)


def _gdn_kernel(*refs, c, nb, has_s0):
    if has_s0:
        q_ref, k_ref, v_ref, z_ref, gb_ref, ng_ref, s0_ref, yb_ref, sout_ref, s_scr = refs
    else:
        q_ref, k_ref, v_ref, z_ref, gb_ref, ng_ref, yb_ref, sout_ref, s_scr = refs
    ci = pl.program_id(1)

    @pl.when(ci == 0)
    def _():
        if has_s0:
            s_scr[...] = s0_ref[...]
        else:
            s_scr[...] = jnp.zeros_like(s_scr)

    ri = lax.broadcasted_iota(jnp.int32, (c, c), 0)
    cj = lax.broadcasted_iota(jnp.int32, (c, c), 1)
    incl = ri >= cj
    strict = ri > cj
    ng = ng_ref[...]
    row = lax.broadcasted_iota(jnp.int32, (c, SMALL_COLS), 0)
    chains = [(s, h) for s in range(nb) for h in range(GDN_HEADS)]

    seq = []
    for s in range(nb):
        gb = gb_ref[s]
        gc = gb
        sh = 1
        while sh < c:
            gc = gc + jnp.where(row >= sh, pltpu.roll(gc, sh, axis=0), 0.0)
            sh *= 2
        g_last = gc[c - 1:c, :]
        seq.append(dict(gb=gb, gc=gc, gct=gc.T, e_g=jnp.exp(gc), e_kd=jnp.exp(g_last - gc), e_last=jnp.exp(g_last)))

    st = []
    for s, h in chains:
        sl = slice(h * GDN_DK, (h + 1) * GDN_DK)
        sq = seq[s]
        qh, kh, vh = q_ref[s, :, sl], k_ref[s, :, sl], v_ref[s, :, sl]
        beta = sq["gb"][:, GDN_HEADS + h:GDN_HEADS + h + 1]
        kb = kh * beta
        aq = _dot_nt(jnp.concatenate([kb, qh], axis=0), kh)
        st.append(dict(sl=sl, qh=qh, kh=kh, kb=kb, vb=vh * beta, aq=aq))

    for (s, h), e in zip(chains, st):
        sq = seq[s]
        decay = jnp.exp(jnp.where(incl, sq["gc"][:, h:h + 1] - sq["gct"][h:h + 1, :], -jnp.inf))
        e["neg"] = -jnp.where(strict, e["aq"][:c] * decay, 0.0)
        e["attn"] = jnp.where(incl, e["aq"][c:] * decay, 0.0)
        e["x"] = jnp.concatenate([e["vb"], e["kb"] * sq["e_g"][:, h:h + 1]], axis=1)
        e["pw"] = e["neg"]

    span = 1
    while span < c:
        span *= 2
        for e in st:
            if span < c:
                both = _dot(e["pw"], jnp.concatenate([e["pw"], e["x"]], axis=1))
                e["pw"] = both[:, :c]
                e["x"] = e["x"] + both[:, c:]
            else:
                e["x"] = e["x"] + _dot(e["pw"], e["x"])

    for (s, h), e in zip(chains, st):
        sq = seq[s]
        u_val, w_k = e["x"][:, :GDN_DV], e["x"][:, GDN_DV:]
        q_dec = e["qh"] * sq["e_g"][:, h:h + 1]
        e["s_old"] = s_scr[s, h]
        ws = _dot(jnp.concatenate([w_k, q_dec], axis=0), e["s_old"])
        e["v_new"] = u_val - ws[:c]
        e["o_inter"] = ws[c:]

    for (s, h), e in zip(chains, st):
        sq = seq[s]
        o = e["o_inter"] + _dot(e["attn"], e["v_new"])
        k_dec = e["kh"] * sq["e_kd"][:, h:h + 1]
        s_scr[s, h] = e["s_old"] * sq["e_last"][:, h:h + 1] + _dot(k_dec.T, e["v_new"])
        zh = z_ref[s, :, e["sl"]]
        yb_ref[s, :, e["sl"]] = (((o * _rms_scale(o)) * ng) * (zh * jax.nn.sigmoid(zh))).astype(BF16)

    @pl.when(ci == pl.num_programs(1) - 1)
    def _():
        sout_ref[...] = s_scr[...]


def _gdn(q, k, v, p, gb, norm_g, bsz, t, c, nb, s0=None):
    d = D_MODEL
    nc = t // c
    has_s0 = s0 is not None

    def tok(col=0):
        return pl.BlockSpec((nb, c, d), lambda b, i, col=col: (b, i, col))

    in_specs = [tok(), tok(), tok(), tok(COL_GZ),
                pl.BlockSpec((nb, c, SMALL_COLS), lambda b, i: (b, i, 0)),
                pl.BlockSpec((1, GDN_DV), lambda b, i: (0, 0))]
    args = [a.reshape(bsz, t, -1) for a in (q, k, v, p, gb)] + [norm_g.reshape(1, GDN_DV)]
    s_spec = pl.BlockSpec((nb, GDN_HEADS, GDN_DK, GDN_DV), lambda b, i: (b, 0, 0, 0))
    if has_s0:
        in_specs.append(s_spec)
        args.append(s0)
    yb, s_new = pl.pallas_call(
        functools.partial(_gdn_kernel, c=c, nb=nb, has_s0=has_s0),
        grid=(bsz // nb, nc),
        in_specs=in_specs,
        out_specs=[tok(), s_spec],
        out_shape=[jax.ShapeDtypeStruct((bsz, t, d), BF16),
                   jax.ShapeDtypeStruct((bsz, GDN_HEADS, GDN_DK, GDN_DV), F32)],
        scratch_shapes=[pltpu.VMEM((nb, GDN_HEADS, GDN_DK, GDN_DV), F32)],
        compiler_params=_params(("parallel", "arbitrary")),
        name="gdn_scan",
    )(*args)
    return yb.reshape(bsz * t, d), s_new


def _mem_attn_kernel(q_ref, k_ref, v_ref, o_ref):
    scale = MEM_HEAD_DIM ** -0.5
    for h in range(MEM_HEADS):
        sl = slice(h * MEM_HEAD_DIM, (h + 1) * MEM_HEAD_DIM)
        s = _dot_nt(q_ref[:, sl], k_ref[0, :, sl]) * scale
        e = jnp.exp(s - jnp.max(s, axis=-1, keepdims=True))
        prob = e / jnp.sum(e, axis=-1, keepdims=True)
        o_ref[:, sl] = _dot(prob, v_ref[0, :, sl]).astype(BF16)


def _mem_attn(p, mk, mv, bsz, t, tq):
    d = D_MODEL
    m = mk.shape[1]
    nq = t // tq
    kv_spec = pl.BlockSpec((1, m, d), lambda b, i: (b, 0, 0))
    return pl.pallas_call(
        _mem_attn_kernel,
        grid=(bsz, nq),
        in_specs=[pl.BlockSpec((tq, d), lambda b, i: (b * nq + i, COL_MQ)), kv_spec, kv_spec],
        out_specs=pl.BlockSpec((tq, d), lambda b, i: (b * nq + i, 0)),
        out_shape=jax.ShapeDtypeStruct((bsz * t, d), BF16),
        compiler_params=_params(("parallel", "arbitrary")),
        name="mem_attn",
    )(p, mk, mv)


def _merge_kernel(x_ref, ya_ref, yb_ref, ym_ref, g0_ref, g1_ref, g2_ref, wc_ref, wg_ref, wm_ref, wo_ref,
                  ng_ref, x1_ref, h2_ref):
    d = functools.partial(jnp.dot, preferred_element_type=F32)
    merged = (jax.nn.sigmoid(g0_ref[...]) * d(ya_ref[...], wc_ref[...])
              + jax.nn.sigmoid(g1_ref[...]) * d(yb_ref[...], wg_ref[...])
              + jax.nn.sigmoid(g2_ref[...]) * d(ym_ref[...], wm_ref[...]))
    x1 = x_ref[...] + d(merged.astype(BF16), wo_ref[...])
    x1_ref[...] = x1
    h2_ref[...] = ((x1 * _rms_scale(x1)) * ng_ref[...]).astype(BF16)


def _merge(x, ya, yb, ym, p, wc, wg, wm, wo, norm_g, tm):
    n, d = x.shape
    tok = pl.BlockSpec((tm, d), lambda i: (i, 0))
    wsp = pl.BlockSpec((d, d), lambda i: (0, 0))

    def gate(j):
        return pl.BlockSpec((tm, d), lambda i, j=j: (i, COL_BG + j))

    return pl.pallas_call(
        _merge_kernel,
        grid=(n // tm,),
        in_specs=[tok, tok, tok, tok, gate(0), gate(1), gate(2), wsp, wsp, wsp, wsp,
                  pl.BlockSpec((1, d), lambda i: (0, 0))],
        out_specs=[tok, tok],
        out_shape=[jax.ShapeDtypeStruct((n, d), F32), jax.ShapeDtypeStruct((n, d), BF16)],
        compiler_params=_params(("parallel",)),
        name="merge",
    )(x, ya, yb, ym, p, p, p, wc, wg, wm, wo, norm_g.reshape(1, d))


_CAND_CELLS = tuple((a, b) for a in range(PEER_TOPK) for b in range(PEER_TOPK) if (a + 1) * (b + 1) <= PEER_TOPK)


def _top_values(x, count):
    vals = []
    cur = x
    for r in range(count):
        m = jnp.max(cur, axis=0, keepdims=True)
        vals.append(m)
        if r + 1 < count:
            cur = jnp.where(cur == m, -jnp.inf, cur)
    return vals


def _route_kernel(h_ref, wq_ref, keys_ref, s0_ref, e0_ref, s1_ref, e1_ref, thr_ref):
    q = jnp.dot(h_ref[...], wq_ref[...], preferred_element_type=F32).astype(BF16)
    thr_rows = []
    for h in range(PEER_HEADS):
        sc, top = [], []
        for part in range(2):
            col = (2 * h + part) * PEER_D_HALF
            s_t = lax.dot_general(keys_ref[part, h], q[:, col:col + PEER_D_HALF],
                                  (((1,), (1,)), ((), ())), preferred_element_type=F32)
            sc.append(s_t)
            top.append(_top_values(s_t, PEER_TOPK))
        cand = jnp.concatenate([top[0][a] + top[1][b] for a, b in _CAND_CELLS], axis=0)
        best = _top_values(cand, PEER_TOPK)
        c_max, thr = best[0], best[-1]
        z = jnp.exp(best[0] - c_max)
        for r in range(1, PEER_TOPK):
            z = z + jnp.exp(best[r] - c_max)
        s0_ref[h] = sc[0]
        s1_ref[h] = sc[1]
        e0_ref[h] = jnp.exp(sc[0] - top[0][0]) / z
        e1_ref[h] = jnp.exp(sc[1] - top[1][0])
        thr_rows.append(thr)
    thr_ref[...] = jnp.concatenate(thr_rows, axis=0)


def _route(h2, wq, keys, tn):
    n, d = h2.shape
    nk = PEER_N_KEYS
    big = pl.BlockSpec((PEER_HEADS, nk, tn), lambda i: (0, 0, i))
    big_shape = jax.ShapeDtypeStruct((PEER_HEADS, nk, n), F32)
    return pl.pallas_call(
        _route_kernel,
        grid=(n // tn,),
        in_specs=[pl.BlockSpec((tn, d), lambda i: (i, 0)),
                  pl.BlockSpec(wq.shape, lambda i: (0, 0)),
                  pl.BlockSpec(keys.shape, lambda i: (0, 0, 0, 0))],
        out_specs=[big, big, big, big, pl.BlockSpec((PEER_HEADS, tn), lambda i: (0, i))],
        out_shape=[big_shape, big_shape, big_shape, big_shape, jax.ShapeDtypeStruct((PEER_HEADS, n), F32)],
        compiler_params=_params(("parallel",)),
        name="peer_route",
    )(h2, wq, keys)


def _gelu(x):
    return 0.5 * x * (1.0 + lax.erf(x * (2.0 ** -0.5)))


def _peer_kernel(h_ref, x1_ref, u_ref, vt_ref, s0_ref, e0_ref, s1_ref, e1_ref, thr_ref, ng_ref,
                 y_ref, acc_ref, coef_ref, *, te):
    ei = pl.program_id(1)

    @pl.when(ei == 0)
    def _():
        acc_ref[...] = jnp.zeros_like(acc_ref)

    act = _gelu(lax.dot_general(u_ref[...], h_ref[...], (((1,), (1,)), ((), ())),
                                preferred_element_type=F32))
    groups = te // PEER_N_KEYS
    tn = act.shape[1]
    for ii in range(groups):
        i_key = ei * groups + ii
        rows = slice(ii * PEER_N_KEYS, (ii + 1) * PEER_N_KEYS)
        s0_rows = [s0_ref[h, pl.ds(i_key, 1), :] for h in range(PEER_HEADS)]
        e0_rows = [e0_ref[h, pl.ds(i_key, 1), :] for h in range(PEER_HEADS)]
        for ns in range(tn // LANES):
            cols = slice(ns * LANES, (ns + 1) * LANES)
            gate = None
            for h in range(PEER_HEADS):
                sel = (s0_rows[h][:, cols] + s1_ref[h, :, cols]) >= thr_ref[h:h + 1, cols]
                term = jnp.where(sel, e0_rows[h][:, cols] * e1_ref[h, :, cols], 0.0)
                gate = term if gate is None else gate + term
            coef_ref[rows, cols] = (gate * act[rows, cols]).astype(BF16)
    acc_ref[...] += jnp.dot(vt_ref[...], coef_ref[...], preferred_element_type=F32)

    @pl.when(ei == pl.num_programs(1) - 1)
    def _():
        x2 = x1_ref[...] + acc_ref[...].T
        y_ref[...] = (x2 * _rms_scale(x2)) * ng_ref[...]


def _peer(h2, x1, u_bf, vt_bf, s0, e0, s1, e1, thr, norm_g, tn, te):
    n, d = h2.shape
    ne = u_bf.shape[0]
    nk = PEER_N_KEYS
    tok = pl.BlockSpec((tn, d), lambda i, e: (i, 0))
    big = pl.BlockSpec((PEER_HEADS, nk, tn), lambda i, e: (0, 0, i))
    return pl.pallas_call(
        functools.partial(_peer_kernel, te=te),
        grid=(n // tn, ne // te),
        in_specs=[tok, tok,
                  pl.BlockSpec((te, d), lambda i, e: (e, 0)),
                  pl.BlockSpec((d, te), lambda i, e: (0, e)),
                  big, big, big, big,
                  pl.BlockSpec((PEER_HEADS, tn), lambda i, e: (0, i)),
                  pl.BlockSpec((1, d), lambda i, e: (0, 0))],
        out_specs=tok,
        out_shape=jax.ShapeDtypeStruct((n, d), F32),
        scratch_shapes=[pltpu.VMEM((d, tn), F32), pltpu.VMEM((te, tn), BF16)],
        compiler_params=_params(("parallel", "arbitrary")),
        name="peer_mix",
    )(h2, x1, u_bf, vt_bf, s0, e0, s1, e1, thr, norm_g.reshape(1, d))


def _tile(n, pref):
    return math.gcd(n, pref)


def _layer(x, mk, mv, bsz, t, weights, conv_state=None, gdn_conv_state=None, gdn_s0=None):
    (norm_mix_g, w_main, w_ab, conv_w, gdn_conv_w, a_log, dt_bias, gdn_norm_g, wc, wg, wm, wo, norm_ffn_g,
     wq, keys, u_bf, vt_bf, final_norm_g) = weights
    n = bsz * t
    sample = conv_state is not None
    p, pab = _inproj(x, norm_mix_g, w_main, w_ab, _tile(n, 1024))
    rows = _tile(n, 256) if sample else _tile(t, 512)
    ya, q, k, v, gb, u_tail = _prep(p, pab, conv_w, gdn_conv_w, a_log, dt_bias, rows, max(t // rows, 1),
                                    conv_state, gdn_conv_state)
    chunk = _tile(t, 64)
    if sample:
        yb, s_new = _gdn(q, k, v, p, gb, gdn_norm_g, bsz, t, chunk, 4, gdn_s0)
    else:
        yb, s_new = _gdn_stacked(q, k, v, p, gb, gdn_norm_g, bsz, t, chunk, _tile(bsz, 2))
    ym = _mem_attn(p, mk, mv, bsz, t, _tile(t, 512))
    x1, h2 = _merge(x, ya, yb, ym, p, wc, wg, wm, wo, norm_ffn_g, _tile(n, 256))
    tn = _tile(n, 512)
    s0, e0, s1, e1, thr = _route(h2, wq, keys, tn)
    y = _peer(h2, x1, u_bf, vt_bf, s0, e0, s1, e1, thr, final_norm_g, tn, 1024)
    return y, u_tail, p, s_new


def kernel(x_prompt, x_sample, mem_prompt, cache_mem_k, cache_mem_v, state_conv, state_gdn_conv, state_gdn, norm_mix_g, w_in, conv_w, gdn_conv_w, gdn_a_log, gdn_dt_bias, gdn_norm_g, mem_norm_g, w_mem_kv, w_br_conv, w_br_gdn, w_br_mem, w_out, norm_ffn_g, peer_w_q, peer_sub_keys, peer_u, peer_v, final_norm_g):
    depth = w_in.shape[0]
    assert depth == 1, "single-layer step"
    d = D_MODEL
    bp, tp, _ = x_prompt.shape
    bs, ts, _ = x_sample.shape
    assert ts == SAMPLE_PAD_T // 2
    m_tok = mem_prompt.shape[1]
    mq = MEM_HEADS * MEM_HEAD_DIM
    l = 0

    w = w_in[l]
    ga0 = 7 * d
    w_main = jnp.concatenate([w[:, :ga0], w[:, ga0 + 2 * GDN_HEADS:]], axis=1).astype(BF16)
    w_ab = jnp.pad(w[:, ga0:ga0 + 2 * GDN_HEADS], ((0, 0), (0, SMALL_COLS - 2 * GDN_HEADS))).astype(BF16)
    weights = (norm_mix_g[l], w_main, w_ab, conv_w[l], gdn_conv_w[l], gdn_a_log[l], gdn_dt_bias[l], gdn_norm_g[l],
               w_br_conv[l].astype(BF16), w_br_gdn[l].astype(BF16), w_br_mem[l].astype(BF16), w_out[l].astype(BF16),
               norm_ffn_g[l], peer_w_q[l].astype(BF16), peer_sub_keys[l].astype(BF16),
               peer_u[l].astype(BF16), peer_v[l].T.astype(BF16), final_norm_g)

    kv = _norm_matmul(mem_prompt.reshape(bp * m_tok, d), mem_norm_g[l], w_mem_kv[l].astype(BF16),
                      _tile(bp * m_tok, 512), 1024)
    mk_p, mv_p = kv[:, :mq], kv[:, mq:]
    y_p, u_tail_p, p_p, s_p = _layer(x_prompt.reshape(bp * tp, d), mk_p.reshape(bp, m_tok, mq),
                                     mv_p.reshape(bp, m_tok, mq), bp, tp, weights)
    rows_p = _tile(tp, 512)
    cb_p = u_tail_p.reshape(bp, tp // rows_p, SUBLANES, d)[:, -1, SUBLANES - (CONV_K - 1):, :]
    gb_p = p_p.reshape(bp, tp, -1)[:, tp - (GDN_CONV_K - 1):, COL_GQ * d:(COL_GV + 1) * d]

    pad_t = SAMPLE_PAD_T
    lead = pad_t - ts
    x_s = jnp.pad(x_sample, ((0, 0), (lead, 0), (0, 0))).reshape(bs * pad_t, d)
    cst = jnp.pad(state_conv[l], ((0, 0), (lead - (CONV_K - 1), ts), (0, 0))).reshape(bs * pad_t, d)
    gst = jnp.pad(state_gdn_conv[l], ((0, 0), (lead - (GDN_CONV_K - 1), ts), (0, 0))).reshape(bs * pad_t, 3 * d)
    y_s, u_s, p_s, s_s = _layer(x_s, cache_mem_k.reshape(bs, -1, mq), cache_mem_v.reshape(bs, -1, mq),
                                bs, pad_t, weights, cst, gst, state_gdn.reshape(bs, GDN_HEADS, GDN_DK, GDN_DV))
    y_s = y_s.reshape(bs, pad_t, d)[:, lead:, :]
    cb_s = u_s.reshape(bs, pad_t, d)[:, pad_t - (CONV_K - 1):, :]
    gb_s = p_s.reshape(bs, pad_t, -1)[:, pad_t - (GDN_CONV_K - 1):, COL_GQ * d:(COL_GV + 1) * d]

    mshape = (depth, bp, m_tok, MEM_HEADS, MEM_HEAD_DIM)
    return (y_p.reshape(bp, tp, d), y_s,
            mk_p.reshape(mshape), mv_p.reshape(mshape),
            cb_p[None], gb_p[None], s_p[None],
            cb_s[None], gb_s[None], s_s[None])
```

```python
import functools
import math

import jax
import jax.numpy as jnp
from jax import lax
from jax.experimental import pallas as pl
from jax.experimental.pallas import tpu as pltpu

F32 = jnp.float32
BF16 = jnp.bfloat16

EPS = 1e-6
D_MODEL = 1024
CONV_K = 3
GDN_HEADS = 8
GDN_DK = 128
GDN_DV = 128
GDN_CONV_K = 4
MEM_HEADS = 4
MEM_HEAD_DIM = 256
PEER_HEADS = 8
PEER_N_KEYS = 128
PEER_D_HALF = 128
PEER_TOPK = 16
N_BRANCHES = 3

COL_AU, COL_AB, COL_AC, COL_GQ, COL_GK, COL_GV, COL_GZ, COL_MQ, COL_BG = 0, 1, 2, 3, 4, 5, 6, 7, 8
N_MAIN_BLOCKS = 11
SMALL_COLS = 128
SUBLANES = 8
LANES = 128
SAMPLE_PAD_T = 8
VMEM_LIMIT = 56 * 1024 * 1024


def _params(sem):
    return pltpu.CompilerParams(dimension_semantics=sem, vmem_limit_bytes=VMEM_LIMIT)


def _dot(a, b):
    return jnp.dot(a.astype(BF16), b.astype(BF16), preferred_element_type=F32)


def _dot_nt(a, b):
    return lax.dot_general(a.astype(BF16), b.astype(BF16), (((1,), (1,)), ((), ())),
                           preferred_element_type=F32)


def _rms_scale(x):
    return lax.rsqrt(jnp.mean(x * x, axis=-1, keepdims=True) + EPS)


def _inproj_kernel(x_ref, g_ref, w_ref, wab_ref, p_ref, pab_ref, h_ref):
    @pl.when(pl.program_id(1) == 0)
    def _():
        x = x_ref[...]
        h = ((x * _rms_scale(x)) * g_ref[...]).astype(BF16)
        h_ref[...] = h
        pab_ref[...] = jnp.dot(h, wab_ref[...], preferred_element_type=F32)

    p_ref[...] = jnp.dot(h_ref[...], w_ref[...], preferred_element_type=F32)


def _inproj(x, g, w_main, w_ab, tm):
    n, d = x.shape
    nblk = w_main.shape[1] // d
    return pl.pallas_call(
        _inproj_kernel,
        grid=(n // tm, nblk),
        in_specs=[
            pl.BlockSpec((tm, d), lambda i, j: (i, 0)),
            pl.BlockSpec((1, d), lambda i, j: (0, 0)),
            pl.BlockSpec((d, d), lambda i, j: (0, j)),
            pl.BlockSpec((d, SMALL_COLS), lambda i, j: (0, 0)),
        ],
        out_specs=[
            pl.BlockSpec((tm, d), lambda i, j: (i, j)),
            pl.BlockSpec((tm, SMALL_COLS), lambda i, j: (i, 0)),
        ],
        out_shape=[
            jax.ShapeDtypeStruct((n, nblk * d), F32),
            jax.ShapeDtypeStruct((n, SMALL_COLS), F32),
        ],
        scratch_shapes=[pltpu.VMEM((tm, d), BF16)],
        compiler_params=_params(("parallel", "arbitrary")),
        name="inproj",
    )(x, g.reshape(1, d), w_main, w_ab)


def _mem_kv_kernel(x_ref, g_ref, w_ref, k_ref, v_ref):
    x = x_ref[...]
    h = ((x * _rms_scale(x)) * g_ref[...]).astype(BF16)
    kv = jnp.dot(h, w_ref[...], preferred_element_type=F32)
    half = kv.shape[1] // 2
    k_ref[...] = kv[:, :half]
    v_ref[...] = kv[:, half:]


def _mem_kv(x, g, w, tm):
    n, d = x.shape
    c = w.shape[1]
    out = jax.ShapeDtypeStruct((n, c // 2), F32)
    ospec = pl.BlockSpec((tm, c // 2), lambda i: (i, 0))
    return pl.pallas_call(
        _mem_kv_kernel,
        grid=(n // tm,),
        in_specs=[
            pl.BlockSpec((tm, d), lambda i: (i, 0)),
            pl.BlockSpec((1, d), lambda i: (0, 0)),
            pl.BlockSpec((d, c), lambda i: (0, 0)),
        ],
        out_specs=[ospec, ospec],
        out_shape=[out, out],
        compiler_params=_params(("parallel",)),
        name="mem_kv",
    )(x, g.reshape(1, d), w)


def _causal_conv(prev, cur, w_ref, k):
    full = jnp.concatenate([prev, cur], axis=0)
    y = cur * w_ref[k - 1:k, :]
    for s in range(1, k):
        y = y + pltpu.roll(full, s, axis=0)[SUBLANES:, :] * w_ref[k - 1 - s:k - s, :]
    return y


def _prep_kernel(*refs, rows, tiles_per_seq, sample):
    if sample:
        (au, ab, ac, gq, gk, gv, pab, au_p, ac_p, gq_p, gk_p, gv_p, cw, gw, alog, dtb, cst, gst,
         ya_o, q_o, k_o, v_o, gb_o, ut_o) = refs
    else:
        (au, ab, ac, gq, gk, gv, pab, au_p, ac_p, gq_p, gk_p, gv_p, cw, gw, alog, dtb,
         ya_o, q_o, k_o, v_o, gb_o, ut_o) = refs
    d = D_MODEL
    first = (pl.program_id(0) % tiles_per_seq) == 0
    keep = jnp.where(first, 0.0, 1.0).astype(F32)

    u = ac[...] * au[...]
    u_prev = (ac_p[...] * au_p[...]) * keep
    if sample:
        real = (lax.broadcasted_iota(jnp.int32, (rows, 1), 0) % SAMPLE_PAD_T) >= (SAMPLE_PAD_T // 2)
        u = jnp.where(real, u, cst[...])
    ya_o[...] = (ab[...] * _causal_conv(u_prev, u, cw, CONV_K)).astype(BF16)
    ut_o[...] = u[rows - ut_o.shape[0]:, :]

    outs = (q_o, k_o, v_o)
    for idx, (cur_ref, prev_ref) in enumerate(((gq, gq_p), (gk, gk_p), (gv, gv_p))):
        cur = cur_ref[...]
        if sample:
            cur = jnp.where(real, cur, gst[:, idx * d:(idx + 1) * d])
        prev = prev_ref[...] * keep
        wslab = gw.at[:, idx * d:(idx + 1) * d]
        y = _causal_conv(prev, cur, wslab, GDN_CONV_K)
        y = y * jax.nn.sigmoid(y)
        if idx < 2:
            scale = GDN_DK ** -0.5 if idx == 0 else 1.0
            for h in range(GDN_HEADS):
                yh = y[:, h * GDN_DK:(h + 1) * GDN_DK]
                r = lax.rsqrt(jnp.sum(yh * yh, axis=-1, keepdims=True) + EPS)
                outs[idx][:, h * GDN_DK:(h + 1) * GDN_DK] = yh * (r * scale)
        else:
            outs[idx][...] = y

    pa = pab[...]
    lane = lax.broadcasted_iota(jnp.int32, pa.shape, 1)
    z = pa + dtb[...]
    softplus = jnp.maximum(z, 0.0) + jnp.log1p(jnp.exp(-jnp.abs(z)))
    g_log = -jnp.exp(alog[...]) * softplus
    gb = jnp.where(lane < GDN_HEADS, g_log, jnp.where(lane < 2 * GDN_HEADS, jax.nn.sigmoid(pa), 0.0))
    if sample:
        gb = jnp.where(real, gb, 0.0)
    gb_o[...] = gb


def _prep(p, pab, conv_w, gdn_conv_w, a_log, dt_bias, rows, tiles_per_seq, conv_state=None, gdn_state=None):
    n = p.shape[0]
    d = D_MODEL
    sample = conv_state is not None
    nt = n // rows
    rpb = rows // SUBLANES

    def cur(c):
        return pl.BlockSpec((rows, d), lambda i, c=c: (i, c))

    def prev(c):
        return pl.BlockSpec((SUBLANES, d), lambda i, c=c: (jnp.maximum(i * rpb - 1, 0), c))

    alog128 = jnp.zeros((1, SMALL_COLS), F32).at[0, :GDN_HEADS].set(a_log)
    dtb128 = jnp.zeros((1, SMALL_COLS), F32).at[0, :GDN_HEADS].set(dt_bias)
    in_specs = [cur(COL_AU), cur(COL_AB), cur(COL_AC), cur(COL_GQ), cur(COL_GK), cur(COL_GV),
                pl.BlockSpec((rows, SMALL_COLS), lambda i: (i, 0)),
                prev(COL_AU), prev(COL_AC), prev(COL_GQ), prev(COL_GK), prev(COL_GV),
                pl.BlockSpec((CONV_K, d), lambda i: (0, 0)),
                pl.BlockSpec((GDN_CONV_K, 3 * d), lambda i: (0, 0)),
                pl.BlockSpec((1, SMALL_COLS), lambda i: (0, 0)),
                pl.BlockSpec((1, SMALL_COLS), lambda i: (0, 0))]
    args = [p, p, p, p, p, p, pab, p, p, p, p, p, conv_w, gdn_conv_w, alog128, dtb128]
    if sample:
        in_specs += [pl.BlockSpec((rows, d), lambda i: (i, 0)), pl.BlockSpec((rows, 3 * d), lambda i: (i, 0))]
        args += [conv_state, gdn_state]
    tail = rows if sample else SUBLANES
    out_specs = [pl.BlockSpec((rows, d), lambda i: (i, 0))] * 4 + [
        pl.BlockSpec((rows, SMALL_COLS), lambda i: (i, 0)),
        pl.BlockSpec((tail, d), lambda i: (i, 0))]
    out_shape = [jax.ShapeDtypeStruct((n, d), BF16)] + [jax.ShapeDtypeStruct((n, d), F32)] * 3 + [
        jax.ShapeDtypeStruct((n, SMALL_COLS), F32),
        jax.ShapeDtypeStruct((nt * tail, d), F32)]
    return pl.pallas_call(
        functools.partial(_prep_kernel, rows=rows, tiles_per_seq=tiles_per_seq, sample=sample),
        grid=(nt,),
        in_specs=in_specs,
        out_specs=out_specs,
        out_shape=out_shape,
        compiler_params=_params(("parallel",)),
        name="prep",
    )(*args)


def _chunk_cumsum(g, c):
    row = lax.broadcasted_iota(jnp.int32, g.shape, 0)
    sh = 1
    while sh < c:
        g = g + jnp.where(row >= sh, pltpu.roll(g, sh, axis=0), 0.0)
        sh *= 2
    return g


def _gdn_stacked_kernel(q_ref, k_ref, v_ref, z_ref, gb_ref, ng_ref, yb_ref, sout_ref, s_scr, *, c, grp, nb):
    ci = pl.program_id(1)

    @pl.when(ci == 0)
    def _():
        s_scr[...] = jnp.zeros_like(s_scr)

    r = grp * c
    ng = ng_ref[...]
    seqs = []
    for s in range(nb):
        gb = gb_ref[s]
        gc = _chunk_cumsum(gb, c)
        g_last = gc[c - 1:c, :]
        seqs.append(dict(gb=gb, gc=gc, gct=gc.T, e_g=jnp.exp(gc), e_kd=jnp.exp(g_last - gc), e_last=jnp.exp(g_last)))

    ri = lax.broadcasted_iota(jnp.int32, (r, r), 0)
    cj = lax.broadcasted_iota(jnp.int32, (r, r), 1)
    same = (ri // c) == (cj // c)
    incl = same & (ri >= cj)
    strict = same & (ri > cj)
    head_block = (lax.broadcasted_iota(jnp.int32, (r, grp * GDN_DK), 0) // c
                  == lax.broadcasted_iota(jnp.int32, (r, grp * GDN_DK), 1) // GDN_DK)

    def block_layout(x):
        return jnp.where(head_block, jnp.concatenate([x] * grp, axis=1), 0.0)

    units = []
    for s in range(nb):
        for g0 in range(0, GDN_HEADS, grp):
            heads = list(range(g0, g0 + grp))
            sq = seqs[s]

            def stack(ref, s=s, heads=heads):
                return jnp.concatenate([ref[s, :, h * GDN_DK:(h + 1) * GDN_DK] for h in heads], axis=0)

            def col(a, off=0, heads=heads):
                return jnp.concatenate([a[:, off + h:off + h + 1] for h in heads], axis=0)

            qs, ks, vs = stack(q_ref), stack(k_ref), stack(v_ref)
            beta = col(sq["gb"], GDN_HEADS)
            kb = ks * beta
            units.append(dict(s=s, heads=heads, qs=qs, ks=ks, kb=kb, vb=vs * beta,
                              gcol=col(sq["gc"]), egcol=col(sq["e_g"]), ekdcol=col(sq["e_kd"]),
                              grow=jnp.concatenate([sq["gct"][h:h + 1, :] for h in heads], axis=1),
                              rows=slice((s * GDN_HEADS + g0) * GDN_DK, (s * GDN_HEADS + g0 + grp) * GDN_DK)))

    for u in units:
        u["aq"] = _dot_nt(jnp.concatenate([u["kb"], u["qs"]], axis=0), u["ks"])
    for u in units:
        decay = jnp.exp(jnp.where(incl, u["gcol"] - u["grow"], -jnp.inf))
        u["pw"] = -jnp.where(strict, u["aq"][:r] * decay, 0.0)
        u["attn"] = jnp.where(incl, u["aq"][r:] * decay, 0.0)
        u["x"] = jnp.concatenate([u["vb"], u["kb"] * u["egcol"]], axis=1)
    span = 1
    while span < c:
        span *= 2
        for u in units:
            if span < c:
                both = _dot(u["pw"], jnp.concatenate([u["pw"], u["x"]], axis=1))
                u["pw"] = both[:, :r]
                u["x"] = u["x"] + both[:, r:]
            else:
                u["x"] = u["x"] + _dot(u["pw"], u["x"])
    for u in units:
        u_val, w_k = u["x"][:, :GDN_DV], u["x"][:, GDN_DV:]
        u["s_old"] = s_scr[u["rows"], :]
        ws = _dot(jnp.concatenate([block_layout(w_k), block_layout(u["qs"] * u["egcol"])], axis=0), u["s_old"])
        u["v_new"] = u_val - ws[:r]
        u["o"] = ws[r:]
    for u in units:
        e_last = seqs[u["s"]]["e_last"]
        o = u["o"] + _dot(u["attn"], u["v_new"])
        e_rows = jnp.concatenate([jnp.broadcast_to(e_last[:, h:h + 1], (GDN_DK, 1)) for h in u["heads"]], axis=0)
        kd_t = block_layout(u["ks"] * u["ekdcol"]).T
        s_scr[u["rows"], :] = u["s_old"] * e_rows + _dot(kd_t, u["v_new"])
        for i, h in enumerate(u["heads"]):
            sl = slice(h * GDN_DK, (h + 1) * GDN_DK)
            oh = o[i * c:(i + 1) * c, :]
            zh = z_ref[u["s"], :, sl]
            yb_ref[u["s"], :, sl] = (((oh * _rms_scale(oh)) * ng) * (zh * jax.nn.sigmoid(zh))).astype(BF16)

    @pl.when(ci == pl.num_programs(1) - 1)
    def _():
        sout_ref[...] = s_scr[...]


def _gdn_stacked(q, k, v, p, gb, norm_g, bsz, t, c, nb):
    d = D_MODEL
    nc = t // c
    grp = min(GDN_HEADS, 256 // c)
    s_rows = nb * GDN_HEADS * GDN_DK

    def tok(col=0):
        return pl.BlockSpec((nb, c, d), lambda b, i, col=col: (b, i, col))

    yb, s_new = pl.pallas_call(
        functools.partial(_gdn_stacked_kernel, c=c, grp=grp, nb=nb),
        grid=(bsz // nb, nc),
        in_specs=[tok(), tok(), tok(), tok(COL_GZ),
                  pl.BlockSpec((nb, c, SMALL_COLS), lambda b, i: (b, i, 0)),
                  pl.BlockSpec((1, GDN_DV), lambda b, i: (0, 0))],
        out_specs=[tok(), pl.BlockSpec((s_rows, GDN_DV), lambda b, i: (b, 0))],
        out_shape=[jax.ShapeDtypeStruct((bsz, t, d), BF16),
                   jax.ShapeDtypeStruct((bsz * GDN_HEADS * GDN_DK, GDN_DV), F32)],
        scratch_shapes=[pltpu.VMEM((s_rows, GDN_DV), F32)],
        compiler_params=_params(("parallel", "arbitrary")),
        name="gdn_scan_prompt",
    )(*[a.reshape(bsz, t, -1) for a in (q, k, v, p, gb)], norm_g.reshape(1, GDN_DV))
    return yb.reshape(bsz * t, d), s_new.reshape(bsz, GDN_HEADS, GDN_DK, GDN_DV)


def _gdn_kernel(*refs, c, nb, has_s0):
    if has_s0:
        q_ref, k_ref, v_ref, z_ref, gb_ref, ng_ref, s0_ref, yb_ref, sout_ref, s_scr = refs
    else:
        q_ref, k_ref, v_ref, z_ref, gb_ref, ng_ref, yb_ref, sout_ref, s_scr = refs
    ci = pl.program_id(1)

    @pl.when(ci == 0)
    def _():
        if has_s0:
            s_scr[...] = s0_ref[...]
        else:
            s_scr[...] = jnp.zeros_like(s_scr)

    ri = lax.broadcasted_iota(jnp.int32, (c, c), 0)
    cj = lax.broadcasted_iota(jnp.int32, (c, c), 1)
    incl = ri >= cj
    strict = ri > cj
    ng = ng_ref[...]
    row = lax.broadcasted_iota(jnp.int32, (c, SMALL_COLS), 0)
    chains = [(s, h) for s in range(nb) for h in range(GDN_HEADS)]

    seq = []
    for s in range(nb):
        gb = gb_ref[s]
        gc = gb
        sh = 1
        while sh < c:
            gc = gc + jnp.where(row >= sh, pltpu.roll(gc, sh, axis=0), 0.0)
            sh *= 2
        g_last = gc[c - 1:c, :]
        seq.append(dict(gb=gb, gc=gc, gct=gc.T, e_g=jnp.exp(gc), e_kd=jnp.exp(g_last - gc), e_last=jnp.exp(g_last)))

    st = []
    for s, h in chains:
        sl = slice(h * GDN_DK, (h + 1) * GDN_DK)
        sq = seq[s]
        qh, kh, vh = q_ref[s, :, sl], k_ref[s, :, sl], v_ref[s, :, sl]
        beta = sq["gb"][:, GDN_HEADS + h:GDN_HEADS + h + 1]
        kb = kh * beta
        aq = _dot_nt(jnp.concatenate([kb, qh], axis=0), kh)
        st.append(dict(sl=sl, qh=qh, kh=kh, kb=kb, vb=vh * beta, aq=aq))

    for (s, h), e in zip(chains, st):
        sq = seq[s]
        decay = jnp.exp(jnp.where(incl, sq["gc"][:, h:h + 1] - sq["gct"][h:h + 1, :], -jnp.inf))
        e["neg"] = -jnp.where(strict, e["aq"][:c] * decay, 0.0)
        e["attn"] = jnp.where(incl, e["aq"][c:] * decay, 0.0)
        e["x"] = jnp.concatenate([e["vb"], e["kb"] * sq["e_g"][:, h:h + 1]], axis=1)
        e["pw"] = e["neg"]

    span = 1
    while span < c:
        span *= 2
        for e in st:
            if span < c:
                both = _dot(e["pw"], jnp.concatenate([e["pw"], e["x"]], axis=1))
                e["pw"] = both[:, :c]
                e["x"] = e["x"] + both[:, c:]
            else:
                e["x"] = e["x"] + _dot(e["pw"], e["x"])

    for (s, h), e in zip(chains, st):
        sq = seq[s]
        u_val, w_k = e["x"][:, :GDN_DV], e["x"][:, GDN_DV:]
        q_dec = e["qh"] * sq["e_g"][:, h:h + 1]
        e["s_old"] = s_scr[s, h]
        ws = _dot(jnp.concatenate([w_k, q_dec], axis=0), e["s_old"])
        e["v_new"] = u_val - ws[:c]
        e["o_inter"] = ws[c:]

    for (s, h), e in zip(chains, st):
        sq = seq[s]
        o = e["o_inter"] + _dot(e["attn"], e["v_new"])
        k_dec = e["kh"] * sq["e_kd"][:, h:h + 1]
        s_scr[s, h] = e["s_old"] * sq["e_last"][:, h:h + 1] + _dot(k_dec.T, e["v_new"])
        zh = z_ref[s, :, e["sl"]]
        yb_ref[s, :, e["sl"]] = (((o * _rms_scale(o)) * ng) * (zh * jax.nn.sigmoid(zh))).astype(BF16)

    @pl.when(ci == pl.num_programs(1) - 1)
    def _():
        sout_ref[...] = s_scr[...]


def _gdn(q, k, v, p, gb, norm_g, bsz, t, c, nb, s0=None):
    d = D_MODEL
    nc = t // c
    has_s0 = s0 is not None

    def tok(col=0):
        return pl.BlockSpec((nb, c, d), lambda b, i, col=col: (b, i, col))

    in_specs = [tok(), tok(), tok(), tok(COL_GZ),
                pl.BlockSpec((nb, c, SMALL_COLS), lambda b, i: (b, i, 0)),
                pl.BlockSpec((1, GDN_DV), lambda b, i: (0, 0))]
    args = [a.reshape(bsz, t, -1) for a in (q, k, v, p, gb)] + [norm_g.reshape(1, GDN_DV)]
    s_spec = pl.BlockSpec((nb, GDN_HEADS, GDN_DK, GDN_DV), lambda b, i: (b, 0, 0, 0))
    if has_s0:
        in_specs.append(s_spec)
        args.append(s0)
    yb, s_new = pl.pallas_call(
        functools.partial(_gdn_kernel, c=c, nb=nb, has_s0=has_s0),
        grid=(bsz // nb, nc),
        in_specs=in_specs,
        out_specs=[tok(), s_spec],
        out_shape=[jax.ShapeDtypeStruct((bsz, t, d), BF16),
                   jax.ShapeDtypeStruct((bsz, GDN_HEADS, GDN_DK, GDN_DV), F32)],
        scratch_shapes=[pltpu.VMEM((nb, GDN_HEADS, GDN_DK, GDN_DV), F32)],
        compiler_params=_params(("parallel", "arbitrary")),
        name="gdn_scan",
    )(*args)
    return yb.reshape(bsz * t, d), s_new


def _mem_attn_kernel(q_ref, k_ref, v_ref, o_ref):
    scale = MEM_HEAD_DIM ** -0.5
    for h in range(MEM_HEADS):
        sl = slice(h * MEM_HEAD_DIM, (h + 1) * MEM_HEAD_DIM)
        s = _dot_nt(q_ref[:, sl], k_ref[0, :, sl]) * scale
        e = jnp.exp(s - jnp.max(s, axis=-1, keepdims=True))
        prob = e / jnp.sum(e, axis=-1, keepdims=True)
        o_ref[:, sl] = _dot(prob, v_ref[0, :, sl]).astype(BF16)


def _mem_attn(p, mk, mv, bsz, t, tq):
    d = D_MODEL
    m = mk.shape[1]
    nq = t // tq
    kv_spec = pl.BlockSpec((1, m, d), lambda b, i: (b, 0, 0))
    return pl.pallas_call(
        _mem_attn_kernel,
        grid=(bsz, nq),
        in_specs=[pl.BlockSpec((tq, d), lambda b, i: (b * nq + i, COL_MQ)), kv_spec, kv_spec],
        out_specs=pl.BlockSpec((tq, d), lambda b, i: (b * nq + i, 0)),
        out_shape=jax.ShapeDtypeStruct((bsz * t, d), BF16),
        compiler_params=_params(("parallel", "arbitrary")),
        name="mem_attn",
    )(p, mk, mv)


def _merge_kernel(x_ref, ya_ref, yb_ref, ym_ref, g0_ref, g1_ref, g2_ref, wc_ref, wg_ref, wm_ref, wo_ref,
                  ng_ref, x1_ref, h2_ref):
    d = functools.partial(jnp.dot, preferred_element_type=F32)
    merged = (jax.nn.sigmoid(g0_ref[...]) * d(ya_ref[...], wc_ref[...])
              + jax.nn.sigmoid(g1_ref[...]) * d(yb_ref[...], wg_ref[...])
              + jax.nn.sigmoid(g2_ref[...]) * d(ym_ref[...], wm_ref[...]))
    x1 = x_ref[...] + d(merged.astype(BF16), wo_ref[...])
    x1_ref[...] = x1
    h2_ref[...] = ((x1 * _rms_scale(x1)) * ng_ref[...]).astype(BF16)


def _merge(x, ya, yb, ym, p, wc, wg, wm, wo, norm_g, tm):
    n, d = x.shape
    tok = pl.BlockSpec((tm, d), lambda i: (i, 0))
    wsp = pl.BlockSpec((d, d), lambda i: (0, 0))

    def gate(j):
        return pl.BlockSpec((tm, d), lambda i, j=j: (i, COL_BG + j))

    return pl.pallas_call(
        _merge_kernel,
        grid=(n // tm,),
        in_specs=[tok, tok, tok, tok, gate(0), gate(1), gate(2), wsp, wsp, wsp, wsp,
                  pl.BlockSpec((1, d), lambda i: (0, 0))],
        out_specs=[tok, tok],
        out_shape=[jax.ShapeDtypeStruct((n, d), F32), jax.ShapeDtypeStruct((n, d), BF16)],
        compiler_params=_params(("parallel",)),
        name="merge",
    )(x, ya, yb, ym, p, p, p, wc, wg, wm, wo, norm_g.reshape(1, d))


_CAND_CELLS = tuple((a, b) for a in range(PEER_TOPK) for b in range(PEER_TOPK) if (a + 1) * (b + 1) <= PEER_TOPK)


def _top_values(x, count):
    vals = []
    cur = x
    for r in range(count):
        m = jnp.max(cur, axis=0, keepdims=True)
        vals.append(m)
        if r + 1 < count:
            cur = jnp.where(cur == m, -jnp.inf, cur)
    return vals


def _route_kernel(h_ref, wq_ref, keys_ref, s0_ref, e0_ref, s1_ref, e1_ref, thr_ref):
    q = jnp.dot(h_ref[...], wq_ref[...], preferred_element_type=F32).astype(BF16)
    thr_rows = []
    for h in range(PEER_HEADS):
        sc, top = [], []
        for part in range(2):
            col = (2 * h + part) * PEER_D_HALF
            s_t = lax.dot_general(keys_ref[part, h], q[:, col:col + PEER_D_HALF],
                                  (((1,), (1,)), ((), ())), preferred_element_type=F32)
            sc.append(s_t)
            top.append(_top_values(s_t, PEER_TOPK))
        cand = jnp.concatenate([top[0][a] + top[1][b] for a, b in _CAND_CELLS], axis=0)
        best = _top_values(cand, PEER_TOPK)
        c_max, thr = best[0], best[-1]
        z = jnp.exp(best[0] - c_max)
        for r in range(1, PEER_TOPK):
            z = z + jnp.exp(best[r] - c_max)
        s0_ref[h] = sc[0]
        s1_ref[h] = sc[1]
        e0_ref[h] = jnp.exp(sc[0] - top[0][0]) / z
        e1_ref[h] = jnp.exp(sc[1] - top[1][0])
        thr_rows.append(thr)
    thr_ref[...] = jnp.concatenate(thr_rows, axis=0)


def _route(h2, wq, keys, tn):
    n, d = h2.shape
    nk = PEER_N_KEYS
    big = pl.BlockSpec((PEER_HEADS, nk, tn), lambda i: (0, 0, i))
    big_shape = jax.ShapeDtypeStruct((PEER_HEADS, nk, n), F32)
    return pl.pallas_call(
        _route_kernel,
        grid=(n // tn,),
        in_specs=[pl.BlockSpec((tn, d), lambda i: (i, 0)),
                  pl.BlockSpec(wq.shape, lambda i: (0, 0)),
                  pl.BlockSpec(keys.shape, lambda i: (0, 0, 0, 0))],
        out_specs=[big, big, big, big, pl.BlockSpec((PEER_HEADS, tn), lambda i: (0, i))],
        out_shape=[big_shape, big_shape, big_shape, big_shape, jax.ShapeDtypeStruct((PEER_HEADS, n), F32)],
        compiler_params=_params(("parallel",)),
        name="peer_route",
    )(h2, wq, keys)


def _gelu(x):
    return 0.5 * x * (1.0 + lax.erf(x * (2.0 ** -0.5)))


PEER_CHUNK = 256


def _peer_kernel(h_ref, x1_ref, u_ref, vt_ref, s0_ref, e0_ref, s1_ref, e1_ref, thr_ref, ng_ref,
                 y_ref, acc_ref, act_ref, coef_ref, *, n_chunks):
    ei = pl.program_id(1)
    tn = h_ref.shape[0]
    groups = PEER_CHUNK // PEER_N_KEYS

    @pl.when(ei == 0)
    def _():
        acc_ref[...] = jnp.zeros_like(acc_ref)

    def phase_a(c, slot):
        s = lax.dot_general(u_ref[c], h_ref[...], (((1,), (1,)), ((), ())), preferred_element_type=F32)
        act_ref[slot] = _gelu(s)

    def phase_b(c, slot):
        for ii in range(groups):
            i_key = (ei * n_chunks + c) * groups + ii
            rows = slice(ii * PEER_N_KEYS, (ii + 1) * PEER_N_KEYS)
            s0_rows = [s0_ref[h, pl.ds(i_key, 1), :] for h in range(PEER_HEADS)]
            e0_rows = [e0_ref[h, pl.ds(i_key, 1), :] for h in range(PEER_HEADS)]
            for ns in range(tn // LANES):
                cols = slice(ns * LANES, (ns + 1) * LANES)
                gate = None
                for h in range(PEER_HEADS):
                    sel = (s0_rows[h][:, cols] + s1_ref[h, :, cols]) >= thr_ref[h:h + 1, cols]
                    term = jnp.where(sel, e0_rows[h][:, cols] * e1_ref[h, :, cols], 0.0)
                    gate = term if gate is None else gate + term
                coef_ref[slot, rows, cols] = (gate * act_ref[slot, rows, cols]).astype(BF16)

    def phase_c(c, slot):
        acc_ref[...] += jnp.dot(vt_ref[c], coef_ref[slot], preferred_element_type=F32)

    phase_a(0, 0)
    phase_b(0, 0)
    phase_a(1, 1)

    def steady(j, carry):
        c = 2 * j
        phase_c(c - 2, 0)
        phase_b(c - 1, 1)
        phase_a(c, 0)
        phase_c(c - 1, 1)
        phase_b(c, 0)
        phase_a(c + 1, 1)
        return carry

    lax.fori_loop(1, n_chunks // 2, steady, 0)
    phase_c(n_chunks - 2, 0)
    phase_b(n_chunks - 1, 1)
    phase_c(n_chunks - 1, 1)

    @pl.when(ei == pl.num_programs(1) - 1)
    def _():
        x2 = x1_ref[...] + acc_ref[...].T
        y_ref[...] = (x2 * _rms_scale(x2)) * ng_ref[...]


def _peer(h2, x1, u3, vt3, s0, e0, s1, e1, thr, norm_g, tn, n_chunks):
    n, d = h2.shape
    nk = PEER_N_KEYS
    assert n_chunks % 2 == 0 and u3.shape[0] % n_chunks == 0
    tok = pl.BlockSpec((tn, d), lambda i, e: (i, 0))
    big = pl.BlockSpec((PEER_HEADS, nk, tn), lambda i, e: (0, 0, i))
    return pl.pallas_call(
        functools.partial(_peer_kernel, n_chunks=n_chunks),
        grid=(n // tn, u3.shape[0] // n_chunks),
        in_specs=[tok, tok,
                  pl.BlockSpec((n_chunks, PEER_CHUNK, d), lambda i, e: (e, 0, 0)),
                  pl.BlockSpec((n_chunks, d, PEER_CHUNK), lambda i, e: (e, 0, 0)),
                  big, big, big, big,
                  pl.BlockSpec((PEER_HEADS, tn), lambda i, e: (0, i)),
                  pl.BlockSpec((1, d), lambda i, e: (0, 0))],
        out_specs=tok,
        out_shape=jax.ShapeDtypeStruct((n, d), F32),
        scratch_shapes=[pltpu.VMEM((d, tn), F32), pltpu.VMEM((2, PEER_CHUNK, tn), F32),
                        pltpu.VMEM((2, PEER_CHUNK, tn), BF16)],
        compiler_params=_params(("parallel", "arbitrary")),
        name="peer_mix",
    )(h2, x1, u3, vt3, s0, e0, s1, e1, thr, norm_g.reshape(1, d))


def _tile(n, pref):
    return math.gcd(n, pref)


def _layer(x, mk, mv, bsz, t, weights, conv_state=None, gdn_conv_state=None, gdn_s0=None):
    (norm_mix_g, w_main, w_ab, conv_w, gdn_conv_w, a_log, dt_bias, gdn_norm_g, wc, wg, wm, wo, norm_ffn_g,
     wq, keys, u_bf, vt_bf, final_norm_g) = weights
    n = bsz * t
    sample = conv_state is not None
    p, pab = _inproj(x, norm_mix_g, w_main, w_ab, _tile(n, 1024))
    rows = _tile(n, 256) if sample else _tile(t, 512)
    ya, q, k, v, gb, u_tail = _prep(p, pab, conv_w, gdn_conv_w, a_log, dt_bias, rows, max(t // rows, 1),
                                    conv_state, gdn_conv_state)
    chunk = _tile(t, 64)
    if sample:
        yb, s_new = _gdn(q, k, v, p, gb, gdn_norm_g, bsz, t, chunk, 4, gdn_s0)
    else:
        yb, s_new = _gdn_stacked(q, k, v, p, gb, gdn_norm_g, bsz, t, chunk, _tile(bsz, 2))
    ym = _mem_attn(p, mk, mv, bsz, t, _tile(t, 512))
    x1, h2 = _merge(x, ya, yb, ym, p, wc, wg, wm, wo, norm_ffn_g, _tile(n, 256))
    tn = _tile(n, 512)
    s0, e0, s1, e1, thr = _route(h2, wq, keys, tn)
    y = _peer(h2, x1, u_bf, vt_bf, s0, e0, s1, e1, thr, final_norm_g, tn, 8)
    return y, u_tail, p, s_new


def kernel(x_prompt, x_sample, mem_prompt, cache_mem_k, cache_mem_v, state_conv, state_gdn_conv, state_gdn, norm_mix_g, w_in, conv_w, gdn_conv_w, gdn_a_log, gdn_dt_bias, gdn_norm_g, mem_norm_g, w_mem_kv, w_br_conv, w_br_gdn, w_br_mem, w_out, norm_ffn_g, peer_w_q, peer_sub_keys, peer_u, peer_v, final_norm_g):
    depth = w_in.shape[0]
    assert depth == 1, "single-layer step"
    d = D_MODEL
    bp, tp, _ = x_prompt.shape
    bs, ts, _ = x_sample.shape
    assert ts == SAMPLE_PAD_T // 2
    m_tok = mem_prompt.shape[1]
    mq = MEM_HEADS * MEM_HEAD_DIM
    l = 0

    w = w_in[l]
    ga0 = 7 * d
    w_main = jnp.concatenate([w[:, :ga0], w[:, ga0 + 2 * GDN_HEADS:]], axis=1).astype(BF16)
    w_ab = jnp.pad(w[:, ga0:ga0 + 2 * GDN_HEADS], ((0, 0), (0, SMALL_COLS - 2 * GDN_HEADS))).astype(BF16)
    weights = (norm_mix_g[l], w_main, w_ab, conv_w[l], gdn_conv_w[l], gdn_a_log[l], gdn_dt_bias[l], gdn_norm_g[l],
               w_br_conv[l].astype(BF16), w_br_gdn[l].astype(BF16), w_br_mem[l].astype(BF16), w_out[l].astype(BF16),
               norm_ffn_g[l], peer_w_q[l].astype(BF16), peer_sub_keys[l].astype(BF16),
               peer_u[l].astype(BF16).reshape(-1, PEER_CHUNK, d),
               peer_v[l].astype(BF16).reshape(-1, PEER_CHUNK, d).transpose(0, 2, 1), final_norm_g)

    kv_shape = (bp, m_tok, mq)
    mk_p, mv_p = _mem_kv(mem_prompt.reshape(bp * m_tok, d), mem_norm_g[l], w_mem_kv[l].astype(BF16),
                         _tile(bp * m_tok, 256))
    y_p, u_tail_p, p_p, s_p = _layer(x_prompt.reshape(bp * tp, d), mk_p.reshape(kv_shape), mv_p.reshape(kv_shape),
                                     bp, tp, weights)
    rows_p = _tile(tp, 512)
    cb_p = u_tail_p.reshape(bp, tp // rows_p, SUBLANES, d)[:, -1, SUBLANES - (CONV_K - 1):, :]
    gb_p = p_p.reshape(bp, tp, -1)[:, tp - (GDN_CONV_K - 1):, COL_GQ * d:(COL_GV + 1) * d]

    pad_t = SAMPLE_PAD_T
    lead = pad_t - ts
    x_s = jnp.pad(x_sample, ((0, 0), (lead, 0), (0, 0))).reshape(bs * pad_t, d)
    cst = jnp.pad(state_conv[l], ((0, 0), (lead - (CONV_K - 1), ts), (0, 0))).reshape(bs * pad_t, d)
    gst = jnp.pad(state_gdn_conv[l], ((0, 0), (lead - (GDN_CONV_K - 1), ts), (0, 0))).reshape(bs * pad_t, 3 * d)
    y_s, u_s, p_s, s_s = _layer(x_s, cache_mem_k.reshape(bs, -1, mq), cache_mem_v.reshape(bs, -1, mq),
                                bs, pad_t, weights, cst, gst, state_gdn.reshape(bs, GDN_HEADS, GDN_DK, GDN_DV))
    y_s = y_s.reshape(bs, pad_t, d)[:, lead:, :]
    cb_s = u_s.reshape(bs, pad_t, d)[:, pad_t - (CONV_K - 1):, :]
    gb_s = p_s.reshape(bs, pad_t, -1)[:, pad_t - (GDN_CONV_K - 1):, COL_GQ * d:(COL_GV + 1) * d]

    mshape = (depth, bp, m_tok, MEM_HEADS, MEM_HEAD_DIM)
    return (y_p.reshape(bp, tp, d), y_s,
            mk_p.reshape(mshape), mv_p.reshape(mshape),
            cb_p[None], gb_p[None], s_p[None],
            cb_s[None], gb_s[None], s_s[None])
```

```python
import functools
import math

import jax
import jax.numpy as jnp
from jax import lax
from jax.experimental import pallas as pl
from jax.experimental.pallas import tpu as pltpu

F32 = jnp.float32
BF16 = jnp.bfloat16

EPS = 1e-6
D_MODEL = 1024
CONV_K = 3
GDN_HEADS = 8
GDN_DK = 128
GDN_DV = 128
GDN_CONV_K = 4
MEM_HEADS = 4
MEM_HEAD_DIM = 256
PEER_HEADS = 8
PEER_N_KEYS = 128
PEER_D_HALF = 128
PEER_TOPK = 16
N_BRANCHES = 3

COL_AU, COL_AB, COL_AC, COL_GQ, COL_GK, COL_GV, COL_GZ, COL_MQ, COL_BG = 0, 1, 2, 3, 4, 5, 6, 7, 8
N_MAIN_BLOCKS = 11
SMALL_COLS = 128
SUBLANES = 8
LANES = 128
SAMPLE_PAD_T = 8
VMEM_LIMIT = 56 * 1024 * 1024


def _params(sem):
    return pltpu.CompilerParams(dimension_semantics=sem, vmem_limit_bytes=VMEM_LIMIT)


def _dot(a, b):
    return jnp.dot(a.astype(BF16), b.astype(BF16), preferred_element_type=F32)


def _dot_nt(a, b):
    return lax.dot_general(a.astype(BF16), b.astype(BF16), (((1,), (1,)), ((), ())),
                           preferred_element_type=F32)


def _rms_scale(x):
    return lax.rsqrt(jnp.mean(x * x, axis=-1, keepdims=True) + EPS)


def _inproj_kernel(x_ref, g_ref, w_ref, wab_ref, p_ref, pab_ref, h_ref):
    @pl.when(pl.program_id(1) == 0)
    def _():
        x = x_ref[...]
        h = ((x * _rms_scale(x)) * g_ref[...]).astype(BF16)
        h_ref[...] = h
        pab_ref[...] = jnp.dot(h, wab_ref[...], preferred_element_type=F32)

    p_ref[...] = jnp.dot(h_ref[...], w_ref[...], preferred_element_type=F32)


def _inproj(x, g, w_main, w_ab, tm):
    n, d = x.shape
    nblk = w_main.shape[1] // d
    return pl.pallas_call(
        _inproj_kernel,
        grid=(n // tm, nblk),
        in_specs=[
            pl.BlockSpec((tm, d), lambda i, j: (i, 0)),
            pl.BlockSpec((1, d), lambda i, j: (0, 0)),
            pl.BlockSpec((d, d), lambda i, j: (0, j)),
            pl.BlockSpec((d, SMALL_COLS), lambda i, j: (0, 0)),
        ],
        out_specs=[
            pl.BlockSpec((tm, d), lambda i, j: (i, j)),
            pl.BlockSpec((tm, SMALL_COLS), lambda i, j: (i, 0)),
        ],
        out_shape=[
            jax.ShapeDtypeStruct((n, nblk * d), F32),
            jax.ShapeDtypeStruct((n, SMALL_COLS), F32),
        ],
        scratch_shapes=[pltpu.VMEM((tm, d), BF16)],
        compiler_params=_params(("parallel", "arbitrary")),
        name="inproj",
    )(x, g.reshape(1, d), w_main, w_ab)


def _mem_kv_kernel(x_ref, g_ref, w_ref, k_ref, v_ref):
    x = x_ref[...]
    h = ((x * _rms_scale(x)) * g_ref[...]).astype(BF16)
    kv = jnp.dot(h, w_ref[...], preferred_element_type=F32)
    half = kv.shape[1] // 2
    k_ref[...] = kv[:, :half]
    v_ref[...] = kv[:, half:]


def _mem_kv(x, g, w, tm):
    n, d = x.shape
    c = w.shape[1]
    out = jax.ShapeDtypeStruct((n, c // 2), F32)
    ospec = pl.BlockSpec((tm, c // 2), lambda i: (i, 0))
    return pl.pallas_call(
        _mem_kv_kernel,
        grid=(n // tm,),
        in_specs=[
            pl.BlockSpec((tm, d), lambda i: (i, 0)),
            pl.BlockSpec((1, d), lambda i: (0, 0)),
            pl.BlockSpec((d, c), lambda i: (0, 0)),
        ],
        out_specs=[ospec, ospec],
        out_shape=[out, out],
        compiler_params=_params(("parallel",)),
        name="mem_kv",
    )(x, g.reshape(1, d), w)


def _causal_conv(prev, cur, w_ref, k):
    full = jnp.concatenate([prev, cur], axis=0)
    y = cur * w_ref[k - 1:k, :]
    for s in range(1, k):
        y = y + pltpu.roll(full, s, axis=0)[SUBLANES:, :] * w_ref[k - 1 - s:k - s, :]
    return y


def _prep_kernel(*refs, rows, tiles_per_seq, sample):
    if sample:
        (au, ab, ac, gq, gk, gv, pab, au_p, ac_p, gq_p, gk_p, gv_p, cw, gw, alog, dtb, cst, gst,
         ya_o, q_o, k_o, v_o, gb_o, ut_o) = refs
    else:
        (au, ab, ac, gq, gk, gv, pab, au_p, ac_p, gq_p, gk_p, gv_p, cw, gw, alog, dtb,
         ya_o, q_o, k_o, v_o, gb_o, ut_o) = refs
    d = D_MODEL
    first = (pl.program_id(0) % tiles_per_seq) == 0
    keep = jnp.where(first, 0.0, 1.0).astype(F32)

    u = ac[...] * au[...]
    u_prev = (ac_p[...] * au_p[...]) * keep
    if sample:
        real = (lax.broadcasted_iota(jnp.int32, (rows, 1), 0) % SAMPLE_PAD_T) >= (SAMPLE_PAD_T // 2)
        u = jnp.where(real, u, cst[...])
    ya_o[...] = (ab[...] * _causal_conv(u_prev, u, cw, CONV_K)).astype(BF16)
    ut_o[...] = u[rows - ut_o.shape[0]:, :]

    outs = (q_o, k_o, v_o)
    for idx, (cur_ref, prev_ref) in enumerate(((gq, gq_p), (gk, gk_p), (gv, gv_p))):
        cur = cur_ref[...]
        if sample:
            cur = jnp.where(real, cur, gst[:, idx * d:(idx + 1) * d])
        prev = prev_ref[...] * keep
        wslab = gw.at[:, idx * d:(idx + 1) * d]
        y = _causal_conv(prev, cur, wslab, GDN_CONV_K)
        y = y * jax.nn.sigmoid(y)
        if idx < 2:
            scale = GDN_DK ** -0.5 if idx == 0 else 1.0
            for h in range(GDN_HEADS):
                yh = y[:, h * GDN_DK:(h + 1) * GDN_DK]
                r = lax.rsqrt(jnp.sum(yh * yh, axis=-1, keepdims=True) + EPS)
                outs[idx][:, h * GDN_DK:(h + 1) * GDN_DK] = yh * (r * scale)
        else:
            outs[idx][...] = y

    pa = pab[...]
    lane = lax.broadcasted_iota(jnp.int32, pa.shape, 1)
    z = pa + dtb[...]
    softplus = jnp.maximum(z, 0.0) + jnp.log1p(jnp.exp(-jnp.abs(z)))
    g_log = -jnp.exp(alog[...]) * softplus
    gb = jnp.where(lane < GDN_HEADS, g_log, jnp.where(lane < 2 * GDN_HEADS, jax.nn.sigmoid(pa), 0.0))
    if sample:
        gb = jnp.where(real, gb, 0.0)
    gb_o[...] = gb


def _prep(p, pab, conv_w, gdn_conv_w, a_log, dt_bias, rows, tiles_per_seq, conv_state=None, gdn_state=None):
    n = p.shape[0]
    d = D_MODEL
    sample = conv_state is not None
    nt = n // rows
    rpb = rows // SUBLANES

    def cur(c):
        return pl.BlockSpec((rows, d), lambda i, c=c: (i, c))

    def prev(c):
        return pl.BlockSpec((SUBLANES, d), lambda i, c=c: (jnp.maximum(i * rpb - 1, 0), c))

    alog128 = jnp.zeros((1, SMALL_COLS), F32).at[0, :GDN_HEADS].set(a_log)
    dtb128 = jnp.zeros((1, SMALL_COLS), F32).at[0, :GDN_HEADS].set(dt_bias)
    in_specs = [cur(COL_AU), cur(COL_AB), cur(COL_AC), cur(COL_GQ), cur(COL_GK), cur(COL_GV),
                pl.BlockSpec((rows, SMALL_COLS), lambda i: (i, 0)),
                prev(COL_AU), prev(COL_AC), prev(COL_GQ), prev(COL_GK), prev(COL_GV),
                pl.BlockSpec((CONV_K, d), lambda i: (0, 0)),
                pl.BlockSpec((GDN_CONV_K, 3 * d), lambda i: (0, 0)),
                pl.BlockSpec((1, SMALL_COLS), lambda i: (0, 0)),
                pl.BlockSpec((1, SMALL_COLS), lambda i: (0, 0))]
    args = [p, p, p, p, p, p, pab, p, p, p, p, p, conv_w, gdn_conv_w, alog128, dtb128]
    if sample:
        in_specs += [pl.BlockSpec((rows, d), lambda i: (i, 0)), pl.BlockSpec((rows, 3 * d), lambda i: (i, 0))]
        args += [conv_state, gdn_state]
    tail = rows if sample else SUBLANES
    out_specs = [pl.BlockSpec((rows, d), lambda i: (i, 0))] * 4 + [
        pl.BlockSpec((rows, SMALL_COLS), lambda i: (i, 0)),
        pl.BlockSpec((tail, d), lambda i: (i, 0))]
    out_shape = [jax.ShapeDtypeStruct((n, d), BF16)] + [jax.ShapeDtypeStruct((n, d), F32)] * 3 + [
        jax.ShapeDtypeStruct((n, SMALL_COLS), F32),
        jax.ShapeDtypeStruct((nt * tail, d), F32)]
    return pl.pallas_call(
        functools.partial(_prep_kernel, rows=rows, tiles_per_seq=tiles_per_seq, sample=sample),
        grid=(nt,),
        in_specs=in_specs,
        out_specs=out_specs,
        out_shape=out_shape,
        compiler_params=_params(("parallel",)),
        name="prep",
    )(*args)


def _chunk_cumsum(g, c):
    row = lax.broadcasted_iota(jnp.int32, g.shape, 0)
    sh = 1
    while sh < c:
        g = g + jnp.where(row >= sh, pltpu.roll(g, sh, axis=0), 0.0)
        sh *= 2
    return g


def _gdn_stacked_kernel(q_ref, k_ref, v_ref, z_ref, gb_ref, ng_ref, yb_ref, sout_ref, s_scr, *, c, grp, nb):
    ci = pl.program_id(1)

    @pl.when(ci == 0)
    def _():
        s_scr[...] = jnp.zeros_like(s_scr)

    r = grp * c
    ng = ng_ref[...]
    seqs = []
    for s in range(nb):
        gb = gb_ref[s]
        gc = _chunk_cumsum(gb, c)
        g_last = gc[c - 1:c, :]
        seqs.append(dict(gb=gb, gc=gc, gct=gc.T, e_g=jnp.exp(gc), e_kd=jnp.exp(g_last - gc), e_last=jnp.exp(g_last)))

    ri = lax.broadcasted_iota(jnp.int32, (r, r), 0)
    cj = lax.broadcasted_iota(jnp.int32, (r, r), 1)
    same = (ri // c) == (cj // c)
    incl = same & (ri >= cj)
    strict = same & (ri > cj)
    head_block = (lax.broadcasted_iota(jnp.int32, (r, grp * GDN_DK), 0) // c
                  == lax.broadcasted_iota(jnp.int32, (r, grp * GDN_DK), 1) // GDN_DK)

    def block_layout(x):
        return jnp.where(head_block, jnp.concatenate([x] * grp, axis=1), 0.0)

    units = []
    for s in range(nb):
        for g0 in range(0, GDN_HEADS, grp):
            heads = list(range(g0, g0 + grp))
            sq = seqs[s]

            def stack(ref, s=s, heads=heads):
                return jnp.concatenate([ref[s, :, h * GDN_DK:(h + 1) * GDN_DK] for h in heads], axis=0)

            def col(a, off=0, heads=heads):
                return jnp.concatenate([a[:, off + h:off + h + 1] for h in heads], axis=0)

            qs, ks, vs = stack(q_ref), stack(k_ref), stack(v_ref)
            beta = col(sq["gb"], GDN_HEADS)
            kb = ks * beta
            units.append(dict(s=s, heads=heads, qs=qs, ks=ks, kb=kb, vb=vs * beta,
                              gcol=col(sq["gc"]), egcol=col(sq["e_g"]), ekdcol=col(sq["e_kd"]),
                              grow=jnp.concatenate([sq["gct"][h:h + 1, :] for h in heads], axis=1),
                              rows=slice((s * GDN_HEADS + g0) * GDN_DK, (s * GDN_HEADS + g0 + grp) * GDN_DK)))

    for u in units:
        u["aq"] = _dot_nt(jnp.concatenate([u["kb"], u["qs"]], axis=0), u["ks"])
    for u in units:
        decay = jnp.exp(jnp.where(incl, u["gcol"] - u["grow"], -jnp.inf))
        u["pw"] = -jnp.where(strict, u["aq"][:r] * decay, 0.0)
        u["attn"] = jnp.where(incl, u["aq"][r:] * decay, 0.0)
        u["x"] = jnp.concatenate([u["vb"], u["kb"] * u["egcol"]], axis=1)
    span = 1
    while span < c:
        span *= 2
        for u in units:
            if span < c:
                both = _dot(u["pw"], jnp.concatenate([u["pw"], u["x"]], axis=1))
                u["pw"] = both[:, :r]
                u["x"] = u["x"] + both[:, r:]
            else:
                u["x"] = u["x"] + _dot(u["pw"], u["x"])
    for u in units:
        u_val, w_k = u["x"][:, :GDN_DV], u["x"][:, GDN_DV:]
        u["s_old"] = s_scr[u["rows"], :]
        ws = _dot(jnp.concatenate([block_layout(w_k), block_layout(u["qs"] * u["egcol"])], axis=0), u["s_old"])
        u["v_new"] = u_val - ws[:r]
        u["o"] = ws[r:]
    for u in units:
        e_last = seqs[u["s"]]["e_last"]
        o = u["o"] + _dot(u["attn"], u["v_new"])
        e_rows = jnp.concatenate([jnp.broadcast_to(e_last[:, h:h + 1], (GDN_DK, 1)) for h in u["heads"]], axis=0)
        kd_t = block_layout(u["ks"] * u["ekdcol"]).T
        s_scr[u["rows"], :] = u["s_old"] * e_rows + _dot(kd_t, u["v_new"])
        for i, h in enumerate(u["heads"]):
            sl = slice(h * GDN_DK, (h + 1) * GDN_DK)
            oh = o[i * c:(i + 1) * c, :]
            zh = z_ref[u["s"], :, sl]
            yb_ref[u["s"], :, sl] = (((oh * _rms_scale(oh)) * ng) * (zh * jax.nn.sigmoid(zh))).astype(BF16)

    @pl.when(ci == pl.num_programs(1) - 1)
    def _():
        sout_ref[...] = s_scr[...]


def _gdn_stacked(q, k, v, p, gb, norm_g, bsz, t, c, nb):
    d = D_MODEL
    nc = t // c
    grp = min(GDN_HEADS, 256 // c)
    s_rows = nb * GDN_HEADS * GDN_DK

    def tok(col=0):
        return pl.BlockSpec((nb, c, d), lambda b, i, col=col: (b, i, col))

    yb, s_new = pl.pallas_call(
        functools.partial(_gdn_stacked_kernel, c=c, grp=grp, nb=nb),
        grid=(bsz // nb, nc),
        in_specs=[tok(), tok(), tok(), tok(COL_GZ),
                  pl.BlockSpec((nb, c, SMALL_COLS), lambda b, i: (b, i, 0)),
                  pl.BlockSpec((1, GDN_DV), lambda b, i: (0, 0))],
        out_specs=[tok(), pl.BlockSpec((s_rows, GDN_DV), lambda b, i: (b, 0))],
        out_shape=[jax.ShapeDtypeStruct((bsz, t, d), BF16),
                   jax.ShapeDtypeStruct((bsz * GDN_HEADS * GDN_DK, GDN_DV), F32)],
        scratch_shapes=[pltpu.VMEM((s_rows, GDN_DV), F32)],
        compiler_params=_params(("parallel", "arbitrary")),
        name="gdn_scan_prompt",
    )(*[a.reshape(bsz, t, -1) for a in (q, k, v, p, gb)], norm_g.reshape(1, GDN_DV))
    return yb.reshape(bsz * t, d), s_new.reshape(bsz, GDN_HEADS, GDN_DK, GDN_DV)


def _gdn_kernel(*refs, c, nb, has_s0):
    if has_s0:
        q_ref, k_ref, v_ref, z_ref, gb_ref, ng_ref, s0_ref, yb_ref, sout_ref, s_scr = refs
    else:
        q_ref, k_ref, v_ref, z_ref, gb_ref, ng_ref, yb_ref, sout_ref, s_scr = refs
    ci = pl.program_id(1)

    @pl.when(ci == 0)
    def _():
        if has_s0:
            s_scr[...] = s0_ref[...]
        else:
            s_scr[...] = jnp.zeros_like(s_scr)

    ri = lax.broadcasted_iota(jnp.int32, (c, c), 0)
    cj = lax.broadcasted_iota(jnp.int32, (c, c), 1)
    incl = ri >= cj
    strict = ri > cj
    ng = ng_ref[...]
    row = lax.broadcasted_iota(jnp.int32, (c, SMALL_COLS), 0)
    chains = [(s, h) for s in range(nb) for h in range(GDN_HEADS)]

    seq = []
    for s in range(nb):
        gb = gb_ref[s]
        gc = gb
        sh = 1
        while sh < c:
            gc = gc + jnp.where(row >= sh, pltpu.roll(gc, sh, axis=0), 0.0)
            sh *= 2
        g_last = gc[c - 1:c, :]
        seq.append(dict(gb=gb, gc=gc, gct=gc.T, e_g=jnp.exp(gc), e_kd=jnp.exp(g_last - gc), e_last=jnp.exp(g_last)))

    st = []
    for s, h in chains:
        sl = slice(h * GDN_DK, (h + 1) * GDN_DK)
        sq = seq[s]
        qh, kh, vh = q_ref[s, :, sl], k_ref[s, :, sl], v_ref[s, :, sl]
        beta = sq["gb"][:, GDN_HEADS + h:GDN_HEADS + h + 1]
        kb = kh * beta
        aq = _dot_nt(jnp.concatenate([kb, qh], axis=0), kh)
        st.append(dict(sl=sl, qh=qh, kh=kh, kb=kb, vb=vh * beta, aq=aq))

    for (s, h), e in zip(chains, st):
        sq = seq[s]
        decay = jnp.exp(jnp.where(incl, sq["gc"][:, h:h + 1] - sq["gct"][h:h + 1, :], -jnp.inf))
        e["neg"] = -jnp.where(strict, e["aq"][:c] * decay, 0.0)
        e["attn"] = jnp.where(incl, e["aq"][c:] * decay, 0.0)
        e["x"] = jnp.concatenate([e["vb"], e["kb"] * sq["e_g"][:, h:h + 1]], axis=1)
        e["pw"] = e["neg"]

    span = 1
    while span < c:
        span *= 2
        for e in st:
            if span < c:
                both = _dot(e["pw"], jnp.concatenate([e["pw"], e["x"]], axis=1))
                e["pw"] = both[:, :c]
                e["x"] = e["x"] + both[:, c:]
            else:
                e["x"] = e["x"] + _dot(e["pw"], e["x"])

    for (s, h), e in zip(chains, st):
        sq = seq[s]
        u_val, w_k = e["x"][:, :GDN_DV], e["x"][:, GDN_DV:]
        q_dec = e["qh"] * sq["e_g"][:, h:h + 1]
        e["s_old"] = s_scr[s, h]
        ws = _dot(jnp.concatenate([w_k, q_dec], axis=0), e["s_old"])
        e["v_new"] = u_val - ws[:c]
        e["o_inter"] = ws[c:]

    for (s, h), e in zip(chains, st):
        sq = seq[s]
        o = e["o_inter"] + _dot(e["attn"], e["v_new"])
        k_dec = e["kh"] * sq["e_kd"][:, h:h + 1]
        s_scr[s, h] = e["s_old"] * sq["e_last"][:, h:h + 1] + _dot(k_dec.T, e["v_new"])
        zh = z_ref[s, :, e["sl"]]
        yb_ref[s, :, e["sl"]] = (((o * _rms_scale(o)) * ng) * (zh * jax.nn.sigmoid(zh))).astype(BF16)

    @pl.when(ci == pl.num_programs(1) - 1)
    def _():
        sout_ref[...] = s_scr[...]


def _gdn(q, k, v, p, gb, norm_g, bsz, t, c, nb, s0=None):
    d = D_MODEL
    nc = t // c
    has_s0 = s0 is not None

    def tok(col=0):
        return pl.BlockSpec((nb, c, d), lambda b, i, col=col: (b, i, col))

    in_specs = [tok(), tok(), tok(), tok(COL_GZ),
                pl.BlockSpec((nb, c, SMALL_COLS), lambda b, i: (b, i, 0)),
                pl.BlockSpec((1, GDN_DV), lambda b, i: (0, 0))]
    args = [a.reshape(bsz, t, -1) for a in (q, k, v, p, gb)] + [norm_g.reshape(1, GDN_DV)]
    s_spec = pl.BlockSpec((nb, GDN_HEADS, GDN_DK, GDN_DV), lambda b, i: (b, 0, 0, 0))
    if has_s0:
        in_specs.append(s_spec)
        args.append(s0)
    yb, s_new = pl.pallas_call(
        functools.partial(_gdn_kernel, c=c, nb=nb, has_s0=has_s0),
        grid=(bsz // nb, nc),
        in_specs=in_specs,
        out_specs=[tok(), s_spec],
        out_shape=[jax.ShapeDtypeStruct((bsz, t, d), BF16),
                   jax.ShapeDtypeStruct((bsz, GDN_HEADS, GDN_DK, GDN_DV), F32)],
        scratch_shapes=[pltpu.VMEM((nb, GDN_HEADS, GDN_DK, GDN_DV), F32)],
        compiler_params=_params(("parallel", "arbitrary")),
        name="gdn_scan",
    )(*args)
    return yb.reshape(bsz * t, d), s_new


def _mem_attn_kernel(q_ref, k_ref, v_ref, o_ref):
    scale = MEM_HEAD_DIM ** -0.5
    for h in range(MEM_HEADS):
        sl = slice(h * MEM_HEAD_DIM, (h + 1) * MEM_HEAD_DIM)
        s = _dot_nt(q_ref[:, sl], k_ref[0, :, sl]) * scale
        e = jnp.exp(s - jnp.max(s, axis=-1, keepdims=True))
        prob = e / jnp.sum(e, axis=-1, keepdims=True)
        o_ref[:, sl] = _dot(prob, v_ref[0, :, sl]).astype(BF16)


def _mem_attn(p, mk, mv, bsz, t, tq):
    d = D_MODEL
    m = mk.shape[1]
    nq = t // tq
    kv_spec = pl.BlockSpec((1, m, d), lambda b, i: (b, 0, 0))
    return pl.pallas_call(
        _mem_attn_kernel,
        grid=(bsz, nq),
        in_specs=[pl.BlockSpec((tq, d), lambda b, i: (b * nq + i, COL_MQ)), kv_spec, kv_spec],
        out_specs=pl.BlockSpec((tq, d), lambda b, i: (b * nq + i, 0)),
        out_shape=jax.ShapeDtypeStruct((bsz * t, d), BF16),
        compiler_params=_params(("parallel", "arbitrary")),
        name="mem_attn",
    )(p, mk, mv)


def _merge_kernel(x_ref, ya_ref, yb_ref, ym_ref, g0_ref, g1_ref, g2_ref, wc_ref, wg_ref, wm_ref, wo_ref,
                  ng_ref, x1_ref, h2_ref):
    d = functools.partial(jnp.dot, preferred_element_type=F32)
    merged = (jax.nn.sigmoid(g0_ref[...]) * d(ya_ref[...], wc_ref[...])
              + jax.nn.sigmoid(g1_ref[...]) * d(yb_ref[...], wg_ref[...])
              + jax.nn.sigmoid(g2_ref[...]) * d(ym_ref[...], wm_ref[...]))
    x1 = x_ref[...] + d(merged.astype(BF16), wo_ref[...])
    x1_ref[...] = x1
    h2_ref[...] = ((x1 * _rms_scale(x1)) * ng_ref[...]).astype(BF16)


def _merge(x, ya, yb, ym, p, wc, wg, wm, wo, norm_g, tm):
    n, d = x.shape
    tok = pl.BlockSpec((tm, d), lambda i: (i, 0))
    wsp = pl.BlockSpec((d, d), lambda i: (0, 0))

    def gate(j):
        return pl.BlockSpec((tm, d), lambda i, j=j: (i, COL_BG + j))

    return pl.pallas_call(
        _merge_kernel,
        grid=(n // tm,),
        in_specs=[tok, tok, tok, tok, gate(0), gate(1), gate(2), wsp, wsp, wsp, wsp,
                  pl.BlockSpec((1, d), lambda i: (0, 0))],
        out_specs=[tok, tok],
        out_shape=[jax.ShapeDtypeStruct((n, d), F32), jax.ShapeDtypeStruct((n, d), BF16)],
        compiler_params=_params(("parallel",)),
        name="merge",
    )(x, ya, yb, ym, p, p, p, wc, wg, wm, wo, norm_g.reshape(1, d))


_CAND_CELLS = tuple((a, b) for a in range(PEER_TOPK) for b in range(PEER_TOPK) if (a + 1) * (b + 1) <= PEER_TOPK)


def _top_values(x, count):
    vals = []
    cur = x
    for r in range(count):
        m = jnp.max(cur, axis=0, keepdims=True)
        vals.append(m)
        if r + 1 < count:
            cur = jnp.where(cur == m, -jnp.inf, cur)
    return vals


def _route_kernel(h_ref, wq_ref, keys_ref, cut_ref, e0_ref, s1_ref, e1_ref):
    q = jnp.dot(h_ref[...], wq_ref[...], preferred_element_type=F32).astype(BF16)
    for h in range(PEER_HEADS):
        sc, top = [], []
        for part in range(2):
            col = (2 * h + part) * PEER_D_HALF
            s_t = lax.dot_general(keys_ref[part, h], q[:, col:col + PEER_D_HALF],
                                  (((1,), (1,)), ((), ())), preferred_element_type=F32)
            sc.append(s_t)
            top.append(_top_values(s_t, PEER_TOPK))
        cand = jnp.concatenate([top[0][a] + top[1][b] for a, b in _CAND_CELLS], axis=0)
        best = _top_values(cand, PEER_TOPK)
        c_max, thr = best[0], best[-1]
        z = jnp.exp(best[0] - c_max)
        for r in range(1, PEER_TOPK):
            z = z + jnp.exp(best[r] - c_max)
        cut = jnp.full(sc[0].shape, jnp.inf, F32)
        for b in range(PEER_TOPK):
            cut = jnp.where(sc[0] + top[1][b] >= thr, top[1][b], cut)
        cut_ref[h] = cut
        e0_ref[h] = jnp.exp(sc[0] - top[0][0]) / z
        e1 = jnp.exp(sc[1] - top[1][0])
        for ns in range(sc[1].shape[1] // LANES):
            cols = slice(ns * LANES, (ns + 1) * LANES)
            s1_ref[h, ns] = sc[1][:, cols]
            e1_ref[h, ns] = e1[:, cols]


def _route(h2, wq, keys, tn):
    n, d = h2.shape
    nk = PEER_N_KEYS
    rowmaj = pl.BlockSpec((PEER_HEADS, nk, tn), lambda i: (0, 0, i))
    tiled = pl.BlockSpec((PEER_HEADS, tn // LANES, nk, LANES), lambda i: (0, i, 0, 0))
    rowmaj_shape = jax.ShapeDtypeStruct((PEER_HEADS, nk, n), F32)
    tiled_shape = jax.ShapeDtypeStruct((PEER_HEADS, n // LANES, nk, LANES), F32)
    return pl.pallas_call(
        _route_kernel,
        grid=(n // tn,),
        in_specs=[pl.BlockSpec((tn, d), lambda i: (i, 0)),
                  pl.BlockSpec(wq.shape, lambda i: (0, 0)),
                  pl.BlockSpec(keys.shape, lambda i: (0, 0, 0, 0))],
        out_specs=[rowmaj, rowmaj, tiled, tiled],
        out_shape=[rowmaj_shape, rowmaj_shape, tiled_shape, tiled_shape],
        compiler_params=_params(("parallel",)),
        name="peer_route",
    )(h2, wq, keys)


def _gelu(x):
    return 0.5 * x * (1.0 + lax.erf(x * (2.0 ** -0.5)))


def _peer_kernel(h_ref, x1_ref, u_ref, vt_ref, cut_ref, e0_ref, s1_ref, e1_ref, ng_ref,
                 y_ref, acc_ref, act_ref, coef_ref, *, te):
    ei = pl.program_id(1)
    tn = h_ref.shape[0]
    groups = te // PEER_N_KEYS

    @pl.when(ei == 0)
    def _():
        acc_ref[...] = jnp.zeros_like(acc_ref)

    act_ref[...] = _gelu(lax.dot_general(u_ref[...], h_ref[...], (((1,), (1,)), ((), ())),
                                         preferred_element_type=F32))

    def gate_rows(ii, carry):
        i_key = ei * groups + ii
        rows = pl.ds(pl.multiple_of(ii * PEER_N_KEYS, PEER_N_KEYS), PEER_N_KEYS)
        cut_rows = [cut_ref[h, pl.ds(i_key, 1), :] for h in range(PEER_HEADS)]
        e0_rows = [e0_ref[h, pl.ds(i_key, 1), :] for h in range(PEER_HEADS)]
        for ns in range(tn // LANES):
            cols = slice(ns * LANES, (ns + 1) * LANES)
            gate = None
            for h in range(PEER_HEADS):
                sel = s1_ref[h, ns] >= cut_rows[h][:, cols]
                term = jnp.where(sel, e1_ref[h, ns], 0.0) * e0_rows[h][:, cols]
                gate = term if gate is None else gate + term
            coef_ref[rows, cols] = (gate * act_ref[rows, cols]).astype(BF16)
        return carry

    for ii in range(groups):
        gate_rows(ii, 0)
    acc_ref[...] += jnp.dot(vt_ref[...], coef_ref[...], preferred_element_type=F32)

    @pl.when(ei == pl.num_programs(1) - 1)
    def _():
        x2 = x1_ref[...] + acc_ref[...].T
        y_ref[...] = (x2 * _rms_scale(x2)) * ng_ref[...]


def _peer(h2, x1, u_bf, vt_bf, cut, e0, s1, e1, norm_g, tn, te):
    n, d = h2.shape
    ne = u_bf.shape[0]
    nk = PEER_N_KEYS
    tok = pl.BlockSpec((tn, d), lambda i, e: (i, 0))
    rowmaj = pl.BlockSpec((PEER_HEADS, nk, tn), lambda i, e: (0, 0, i))
    tiled = pl.BlockSpec((PEER_HEADS, tn // LANES, nk, LANES), lambda i, e: (0, i, 0, 0))
    return pl.pallas_call(
        functools.partial(_peer_kernel, te=te),
        grid=(n // tn, ne // te),
        in_specs=[tok, tok,
                  pl.BlockSpec((te, d), lambda i, e: (e, 0)),
                  pl.BlockSpec((d, te), lambda i, e: (0, e)),
                  rowmaj, rowmaj, tiled, tiled,
                  pl.BlockSpec((1, d), lambda i, e: (0, 0))],
        out_specs=tok,
        out_shape=jax.ShapeDtypeStruct((n, d), F32),
        scratch_shapes=[pltpu.VMEM((d, tn), F32), pltpu.VMEM((te, tn), F32), pltpu.VMEM((te, tn), BF16)],
        compiler_params=_params(("parallel", "arbitrary")),
        name="peer_mix",
    )(h2, x1, u_bf, vt_bf, cut, e0, s1, e1, norm_g.reshape(1, d))


def _tile(n, pref):
    return math.gcd(n, pref)


def _layer(x, mk, mv, bsz, t, weights, conv_state=None, gdn_conv_state=None, gdn_s0=None):
    (norm_mix_g, w_main, w_ab, conv_w, gdn_conv_w, a_log, dt_bias, gdn_norm_g, wc, wg, wm, wo, norm_ffn_g,
     wq, keys, u_bf, vt_bf, final_norm_g) = weights
    n = bsz * t
    sample = conv_state is not None
    p, pab = _inproj(x, norm_mix_g, w_main, w_ab, _tile(n, 1024))
    rows = _tile(n, 256) if sample else _tile(t, 512)
    ya, q, k, v, gb, u_tail = _prep(p, pab, conv_w, gdn_conv_w, a_log, dt_bias, rows, max(t // rows, 1),
                                    conv_state, gdn_conv_state)
    chunk = _tile(t, 64)
    if sample:
        yb, s_new = _gdn(q, k, v, p, gb, gdn_norm_g, bsz, t, chunk, 4, gdn_s0)
    else:
        yb, s_new = _gdn_stacked(q, k, v, p, gb, gdn_norm_g, bsz, t, chunk, _tile(bsz, 2))
    ym = _mem_attn(p, mk, mv, bsz, t, _tile(t, 512))
    x1, h2 = _merge(x, ya, yb, ym, p, wc, wg, wm, wo, norm_ffn_g, _tile(n, 256))
    tn = _tile(n, 512)
    cut, e0, s1, e1 = _route(h2, wq, keys, tn)
    y = _peer(h2, x1, u_bf, vt_bf, cut, e0, s1, e1, final_norm_g, tn, 1024)
    return y, u_tail, p, s_new


def kernel(x_prompt, x_sample, mem_prompt, cache_mem_k, cache_mem_v, state_conv, state_gdn_conv, state_gdn, norm_mix_g, w_in, conv_w, gdn_conv_w, gdn_a_log, gdn_dt_bias, gdn_norm_g, mem_norm_g, w_mem_kv, w_br_conv, w_br_gdn, w_br_mem, w_out, norm_ffn_g, peer_w_q, peer_sub_keys, peer_u, peer_v, final_norm_g):
    depth = w_in.shape[0]
    assert depth == 1, "single-layer step"
    d = D_MODEL
    bp, tp, _ = x_prompt.shape
    bs, ts, _ = x_sample.shape
    assert ts == SAMPLE_PAD_T // 2
    m_tok = mem_prompt.shape[1]
    mq = MEM_HEADS * MEM_HEAD_DIM
    l = 0

    w = w_in[l]
    ga0 = 7 * d
    w_main = jnp.concatenate([w[:, :ga0], w[:, ga0 + 2 * GDN_HEADS:]], axis=1).astype(BF16)
    w_ab = jnp.pad(w[:, ga0:ga0 + 2 * GDN_HEADS], ((0, 0), (0, SMALL_COLS - 2 * GDN_HEADS))).astype(BF16)
    weights = (norm_mix_g[l], w_main, w_ab, conv_w[l], gdn_conv_w[l], gdn_a_log[l], gdn_dt_bias[l], gdn_norm_g[l],
               w_br_conv[l].astype(BF16), w_br_gdn[l].astype(BF16), w_br_mem[l].astype(BF16), w_out[l].astype(BF16),
               norm_ffn_g[l], peer_w_q[l].astype(BF16), peer_sub_keys[l].astype(BF16),
               peer_u[l].astype(BF16), peer_v[l].T.astype(BF16), final_norm_g)

    kv_shape = (bp, m_tok, mq)
    mk_p, mv_p = _mem_kv(mem_prompt.reshape(bp * m_tok, d), mem_norm_g[l], w_mem_kv[l].astype(BF16),
                         _tile(bp * m_tok, 256))
    y_p, u_tail_p, p_p, s_p = _layer(x_prompt.reshape(bp * tp, d), mk_p.reshape(kv_shape), mv_p.reshape(kv_shape),
                                     bp, tp, weights)
    rows_p = _tile(tp, 512)
    cb_p = u_tail_p.reshape(bp, tp // rows_p, SUBLANES, d)[:, -1, SUBLANES - (CONV_K - 1):, :]
    gb_p = p_p.reshape(bp, tp, -1)[:, tp - (GDN_CONV_K - 1):, COL_GQ * d:(COL_GV + 1) * d]

    pad_t = SAMPLE_PAD_T
    lead = pad_t - ts
    x_s = jnp.pad(x_sample, ((0, 0), (lead, 0), (0, 0))).reshape(bs * pad_t, d)
    cst = jnp.pad(state_conv[l], ((0, 0), (lead - (CONV_K - 1), ts), (0, 0))).reshape(bs * pad_t, d)
    gst = jnp.pad(state_gdn_conv[l], ((0, 0), (lead - (GDN_CONV_K - 1), ts), (0, 0))).reshape(bs * pad_t, 3 * d)
    y_s, u_s, p_s, s_s = _layer(x_s, cache_mem_k.reshape(bs, -1, mq), cache_mem_v.reshape(bs, -1, mq),
                                bs, pad_t, weights, cst, gst, state_gdn.reshape(bs, GDN_HEADS, GDN_DK, GDN_DV))
    y_s = y_s.reshape(bs, pad_t, d)[:, lead:, :]
    cb_s = u_s.reshape(bs, pad_t, d)[:, pad_t - (CONV_K - 1):, :]
    gb_s = p_s.reshape(bs, pad_t, -1)[:, pad_t - (GDN_CONV_K - 1):, COL_GQ * d:(COL_GV + 1) * d]

    mshape = (depth, bp, m_tok, MEM_HEADS, MEM_HEAD_DIM)
    return (y_p.reshape(bp, tp, d), y_s,
            mk_p.reshape(mshape), mv_p.reshape(mshape),
            cb_p[None], gb_p[None], s_p[None],
            cb_s[None], gb_s[None], s_s[None])
```

```python
import functools
import math

import jax
import jax.numpy as jnp
from jax import lax
from jax.experimental import pallas as pl
from jax.experimental.pallas import tpu as pltpu

F32 = jnp.float32
BF16 = jnp.bfloat16

EPS = 1e-6
D_MODEL = 1024
CONV_K = 3
GDN_HEADS = 8
GDN_DK = 128
GDN_DV = 128
GDN_CONV_K = 4
MEM_HEADS = 4
MEM_HEAD_DIM = 256
PEER_HEADS = 8
PEER_N_KEYS = 128
PEER_D_HALF = 128
PEER_TOPK = 16
N_BRANCHES = 3

COL_AU, COL_AB, COL_AC, COL_GQ, COL_GK, COL_GV, COL_GZ, COL_MQ, COL_BG = 0, 1, 2, 3, 4, 5, 6, 7, 8
N_MAIN_BLOCKS = 11
SMALL_COLS = 128
SUBLANES = 8
LANES = 128
SAMPLE_PAD_T = 8
VMEM_LIMIT = 56 * 1024 * 1024


def _params(sem):
    return pltpu.CompilerParams(dimension_semantics=sem, vmem_limit_bytes=VMEM_LIMIT)


def _dot(a, b):
    return jnp.dot(a.astype(BF16), b.astype(BF16), preferred_element_type=F32)


def _dot_nt(a, b):
    return lax.dot_general(a.astype(BF16), b.astype(BF16), (((1,), (1,)), ((), ())),
                           preferred_element_type=F32)


def _rms_scale(x):
    return lax.rsqrt(jnp.mean(x * x, axis=-1, keepdims=True) + EPS)


def _inproj_kernel(x_ref, g_ref, w_ref, wab_ref, p_ref, pab_ref, h_ref):
    @pl.when(pl.program_id(1) == 0)
    def _():
        x = x_ref[...]
        h = ((x * _rms_scale(x)) * g_ref[...]).astype(BF16)
        h_ref[...] = h
        pab_ref[...] = jnp.dot(h, wab_ref[...], preferred_element_type=F32)

    p_ref[...] = jnp.dot(h_ref[...], w_ref[...], preferred_element_type=F32)


def _inproj(x, g, w_main, w_ab, tm):
    n, d = x.shape
    nblk = w_main.shape[1] // d
    return pl.pallas_call(
        _inproj_kernel,
        grid=(n // tm, nblk),
        in_specs=[
            pl.BlockSpec((tm, d), lambda i, j: (i, 0)),
            pl.BlockSpec((1, d), lambda i, j: (0, 0)),
            pl.BlockSpec((d, d), lambda i, j: (0, j)),
            pl.BlockSpec((d, SMALL_COLS), lambda i, j: (0, 0)),
        ],
        out_specs=[
            pl.BlockSpec((tm, d), lambda i, j: (i, j)),
            pl.BlockSpec((tm, SMALL_COLS), lambda i, j: (i, 0)),
        ],
        out_shape=[
            jax.ShapeDtypeStruct((n, nblk * d), F32),
            jax.ShapeDtypeStruct((n, SMALL_COLS), F32),
        ],
        scratch_shapes=[pltpu.VMEM((tm, d), BF16)],
        compiler_params=_params(("parallel", "arbitrary")),
        name="inproj",
    )(x, g.reshape(1, d), w_main, w_ab)


def _mem_kv_kernel(x_ref, g_ref, w_ref, k_ref, v_ref):
    x = x_ref[...]
    h = ((x * _rms_scale(x)) * g_ref[...]).astype(BF16)
    kv = jnp.dot(h, w_ref[...], preferred_element_type=F32)
    half = kv.shape[1] // 2
    k_ref[...] = kv[:, :half]
    v_ref[...] = kv[:, half:]


def _mem_kv(x, g, w, tm):
    n, d = x.shape
    c = w.shape[1]
    out = jax.ShapeDtypeStruct((n, c // 2), F32)
    ospec = pl.BlockSpec((tm, c // 2), lambda i: (i, 0))
    return pl.pallas_call(
        _mem_kv_kernel,
        grid=(n // tm,),
        in_specs=[
            pl.BlockSpec((tm, d), lambda i: (i, 0)),
            pl.BlockSpec((1, d), lambda i: (0, 0)),
            pl.BlockSpec((d, c), lambda i: (0, 0)),
        ],
        out_specs=[ospec, ospec],
        out_shape=[out, out],
        compiler_params=_params(("parallel",)),
        name="mem_kv",
    )(x, g.reshape(1, d), w)


def _causal_conv(prev, cur, w_ref, k):
    full = jnp.concatenate([prev, cur], axis=0)
    y = cur * w_ref[k - 1:k, :]
    for s in range(1, k):
        y = y + pltpu.roll(full, s, axis=0)[SUBLANES:, :] * w_ref[k - 1 - s:k - s, :]
    return y


def _prep_kernel(*refs, rows, tiles_per_seq, sample):
    if sample:
        (au, ab, ac, gq, gk, gv, pab, au_p, ac_p, gq_p, gk_p, gv_p, cw, gw, alog, dtb, cst, gst,
         ya_o, q_o, k_o, v_o, gb_o, ut_o) = refs
    else:
        (au, ab, ac, gq, gk, gv, pab, au_p, ac_p, gq_p, gk_p, gv_p, cw, gw, alog, dtb,
         ya_o, q_o, k_o, v_o, gb_o, ut_o) = refs
    d = D_MODEL
    first = (pl.program_id(0) % tiles_per_seq) == 0
    keep = jnp.where(first, 0.0, 1.0).astype(F32)

    u = ac[...] * au[...]
    u_prev = (ac_p[...] * au_p[...]) * keep
    if sample:
        real = (lax.broadcasted_iota(jnp.int32, (rows, 1), 0) % SAMPLE_PAD_T) >= (SAMPLE_PAD_T // 2)
        u = jnp.where(real, u, cst[...])
    ya_o[...] = (ab[...] * _causal_conv(u_prev, u, cw, CONV_K)).astype(BF16)
    ut_o[...] = u[rows - ut_o.shape[0]:, :]

    outs = (q_o, k_o, v_o)
    for idx, (cur_ref, prev_ref) in enumerate(((gq, gq_p), (gk, gk_p), (gv, gv_p))):
        cur = cur_ref[...]
        if sample:
            cur = jnp.where(real, cur, gst[:, idx * d:(idx + 1) * d])
        prev = prev_ref[...] * keep
        wslab = gw.at[:, idx * d:(idx + 1) * d]
        y = _causal_conv(prev, cur, wslab, GDN_CONV_K)
        y = y * jax.nn.sigmoid(y)
        if idx < 2:
            scale = GDN_DK ** -0.5 if idx == 0 else 1.0
            for h in range(GDN_HEADS):
                yh = y[:, h * GDN_DK:(h + 1) * GDN_DK]
                r = lax.rsqrt(jnp.sum(yh * yh, axis=-1, keepdims=True) + EPS)
                outs[idx][:, h * GDN_DK:(h + 1) * GDN_DK] = yh * (r * scale)
        else:
            outs[idx][...] = y

    pa = pab[...]
    lane = lax.broadcasted_iota(jnp.int32, pa.shape, 1)
    z = pa + dtb[...]
    softplus = jnp.maximum(z, 0.0) + jnp.log1p(jnp.exp(-jnp.abs(z)))
    g_log = -jnp.exp(alog[...]) * softplus
    gb = jnp.where(lane < GDN_HEADS, g_log, jnp.where(lane < 2 * GDN_HEADS, jax.nn.sigmoid(pa), 0.0))
    if sample:
        gb = jnp.where(real, gb, 0.0)
    gb_o[...] = gb


def _prep(p, pab, conv_w, gdn_conv_w, a_log, dt_bias, rows, tiles_per_seq, conv_state=None, gdn_state=None):
    n = p.shape[0]
    d = D_MODEL
    sample = conv_state is not None
    nt = n // rows
    rpb = rows // SUBLANES

    def cur(c):
        return pl.BlockSpec((rows, d), lambda i, c=c: (i, c))

    def prev(c):
        return pl.BlockSpec((SUBLANES, d), lambda i, c=c: (jnp.maximum(i * rpb - 1, 0), c))

    alog128 = jnp.zeros((1, SMALL_COLS), F32).at[0, :GDN_HEADS].set(a_log)
    dtb128 = jnp.zeros((1, SMALL_COLS), F32).at[0, :GDN_HEADS].set(dt_bias)
    in_specs = [cur(COL_AU), cur(COL_AB), cur(COL_AC), cur(COL_GQ), cur(COL_GK), cur(COL_GV),
                pl.BlockSpec((rows, SMALL_COLS), lambda i: (i, 0)),
                prev(COL_AU), prev(COL_AC), prev(COL_GQ), prev(COL_GK), prev(COL_GV),
                pl.BlockSpec((CONV_K, d), lambda i: (0, 0)),
                pl.BlockSpec((GDN_CONV_K, 3 * d), lambda i: (0, 0)),
                pl.BlockSpec((1, SMALL_COLS), lambda i: (0, 0)),
                pl.BlockSpec((1, SMALL_COLS), lambda i: (0, 0))]
    args = [p, p, p, p, p, p, pab, p, p, p, p, p, conv_w, gdn_conv_w, alog128, dtb128]
    if sample:
        in_specs += [pl.BlockSpec((rows, d), lambda i: (i, 0)), pl.BlockSpec((rows, 3 * d), lambda i: (i, 0))]
        args += [conv_state, gdn_state]
    tail = rows if sample else SUBLANES
    out_specs = [pl.BlockSpec((rows, d), lambda i: (i, 0))] * 4 + [
        pl.BlockSpec((rows, SMALL_COLS), lambda i: (i, 0)),
        pl.BlockSpec((tail, d), lambda i: (i, 0))]
    out_shape = [jax.ShapeDtypeStruct((n, d), BF16)] + [jax.ShapeDtypeStruct((n, d), F32)] * 3 + [
        jax.ShapeDtypeStruct((n, SMALL_COLS), F32),
        jax.ShapeDtypeStruct((nt * tail, d), F32)]
    return pl.pallas_call(
        functools.partial(_prep_kernel, rows=rows, tiles_per_seq=tiles_per_seq, sample=sample),
        grid=(nt,),
        in_specs=in_specs,
        out_specs=out_specs,
        out_shape=out_shape,
        compiler_params=_params(("parallel",)),
        name="prep",
    )(*args)


def _chunk_cumsum(g, c):
    row = lax.broadcasted_iota(jnp.int32, g.shape, 0)
    sh = 1
    while sh < c:
        g = g + jnp.where(row >= sh, pltpu.roll(g, sh, axis=0), 0.0)
        sh *= 2
    return g


def _gdn_stacked_kernel(q_ref, k_ref, v_ref, z_ref, gb_ref, ng_ref, yb_ref, sout_ref, s_scr, *, c, grp, nb):
    ci = pl.program_id(1)

    @pl.when(ci == 0)
    def _():
        s_scr[...] = jnp.zeros_like(s_scr)

    r = grp * c
    ng = ng_ref[...]
    seqs = []
    for s in range(nb):
        gb = gb_ref[s]
        gc = _chunk_cumsum(gb, c)
        g_last = gc[c - 1:c, :]
        seqs.append(dict(gb=gb, gc=gc, gct=gc.T, e_g=jnp.exp(gc), e_kd=jnp.exp(g_last - gc), e_last=jnp.exp(g_last)))

    ri = lax.broadcasted_iota(jnp.int32, (r, r), 0)
    cj = lax.broadcasted_iota(jnp.int32, (r, r), 1)
    same = (ri // c) == (cj // c)
    incl = same & (ri >= cj)
    strict = same & (ri > cj)
    head_block = (lax.broadcasted_iota(jnp.int32, (r, grp * GDN_DK), 0) // c
                  == lax.broadcasted_iota(jnp.int32, (r, grp * GDN_DK), 1) // GDN_DK)

    def block_layout(x):
        return jnp.where(head_block, jnp.concatenate([x] * grp, axis=1), 0.0)

    units = []
    for s in range(nb):
        for g0 in range(0, GDN_HEADS, grp):
            heads = list(range(g0, g0 + grp))
            sq = seqs[s]

            def stack(ref, s=s, heads=heads):
                return jnp.concatenate([ref[s, :, h * GDN_DK:(h + 1) * GDN_DK] for h in heads], axis=0)

            def col(a, off=0, heads=heads):
                return jnp.concatenate([a[:, off + h:off + h + 1] for h in heads], axis=0)

            qs, ks, vs = stack(q_ref), stack(k_ref), stack(v_ref)
            beta = col(sq["gb"], GDN_HEADS)
            kb = ks * beta
            units.append(dict(s=s, heads=heads, qs=qs, ks=ks, kb=kb, vb=vs * beta,
                              gcol=col(sq["gc"]), egcol=col(sq["e_g"]), ekdcol=col(sq["e_kd"]),
                              grow=jnp.concatenate([sq["gct"][h:h + 1, :] for h in heads], axis=1),
                              rows=slice((s * GDN_HEADS + g0) * GDN_DK, (s * GDN_HEADS + g0 + grp) * GDN_DK)))

    for u in units:
        u["aq"] = _dot_nt(jnp.concatenate([u["kb"], u["qs"]], axis=0), u["ks"])
    for u in units:
        decay = jnp.exp(jnp.where(incl, u["gcol"] - u["grow"], -jnp.inf))
        u["pw"] = -jnp.where(strict, u["aq"][:r] * decay, 0.0)
        u["attn"] = jnp.where(incl, u["aq"][r:] * decay, 0.0)
        u["x"] = jnp.concatenate([u["vb"], u["kb"] * u["egcol"]], axis=1)
    span = 1
    while span < c:
        span *= 2
        for u in units:
            if span < c:
                both = _dot(u["pw"], jnp.concatenate([u["pw"], u["x"]], axis=1))
                u["pw"] = both[:, :r]
                u["x"] = u["x"] + both[:, r:]
            else:
                u["x"] = u["x"] + _dot(u["pw"], u["x"])
    for u in units:
        u_val, w_k = u["x"][:, :GDN_DV], u["x"][:, GDN_DV:]
        u["s_old"] = s_scr[u["rows"], :]
        ws = _dot(jnp.concatenate([block_layout(w_k), block_layout(u["qs"] * u["egcol"])], axis=0), u["s_old"])
        u["v_new"] = u_val - ws[:r]
        u["o"] = ws[r:]
    for u in units:
        e_last = seqs[u["s"]]["e_last"]
        o = u["o"] + _dot(u["attn"], u["v_new"])
        e_rows = jnp.concatenate([jnp.broadcast_to(e_last[:, h:h + 1], (GDN_DK, 1)) for h in u["heads"]], axis=0)
        kd_t = block_layout(u["ks"] * u["ekdcol"]).T
        s_scr[u["rows"], :] = u["s_old"] * e_rows + _dot(kd_t, u["v_new"])
        for i, h in enumerate(u["heads"]):
            sl = slice(h * GDN_DK, (h + 1) * GDN_DK)
            oh = o[i * c:(i + 1) * c, :]
            zh = z_ref[u["s"], :, sl]
            yb_ref[u["s"], :, sl] = (((oh * _rms_scale(oh)) * ng) * (zh * jax.nn.sigmoid(zh))).astype(BF16)

    @pl.when(ci == pl.num_programs(1) - 1)
    def _():
        sout_ref[...] = s_scr[...]


def _gdn_stacked(q, k, v, p, gb, norm_g, bsz, t, c, nb):
    d = D_MODEL
    nc = t // c
    grp = min(GDN_HEADS, 256 // c)
    s_rows = nb * GDN_HEADS * GDN_DK

    def tok(col=0):
        return pl.BlockSpec((nb, c, d), lambda b, i, col=col: (b, i, col))

    yb, s_new = pl.pallas_call(
        functools.partial(_gdn_stacked_kernel, c=c, grp=grp, nb=nb),
        grid=(bsz // nb, nc),
        in_specs=[tok(), tok(), tok(), tok(COL_GZ),
                  pl.BlockSpec((nb, c, SMALL_COLS), lambda b, i: (b, i, 0)),
                  pl.BlockSpec((1, GDN_DV), lambda b, i: (0, 0))],
        out_specs=[tok(), pl.BlockSpec((s_rows, GDN_DV), lambda b, i: (b, 0))],
        out_shape=[jax.ShapeDtypeStruct((bsz, t, d), BF16),
                   jax.ShapeDtypeStruct((bsz * GDN_HEADS * GDN_DK, GDN_DV), F32)],
        scratch_shapes=[pltpu.VMEM((s_rows, GDN_DV), F32)],
        compiler_params=_params(("parallel", "arbitrary")),
        name="gdn_scan_prompt",
    )(*[a.reshape(bsz, t, -1) for a in (q, k, v, p, gb)], norm_g.reshape(1, GDN_DV))
    return yb.reshape(bsz * t, d), s_new.reshape(bsz, GDN_HEADS, GDN_DK, GDN_DV)


def _gdn_kernel(*refs, c, nb, has_s0):
    if has_s0:
        q_ref, k_ref, v_ref, z_ref, gb_ref, ng_ref, s0_ref, yb_ref, sout_ref, s_scr = refs
    else:
        q_ref, k_ref, v_ref, z_ref, gb_ref, ng_ref, yb_ref, sout_ref, s_scr = refs
    ci = pl.program_id(1)

    @pl.when(ci == 0)
    def _():
        if has_s0:
            s_scr[...] = s0_ref[...]
        else:
            s_scr[...] = jnp.zeros_like(s_scr)

    ri = lax.broadcasted_iota(jnp.int32, (c, c), 0)
    cj = lax.broadcasted_iota(jnp.int32, (c, c), 1)
    incl = ri >= cj
    strict = ri > cj
    ng = ng_ref[...]
    row = lax.broadcasted_iota(jnp.int32, (c, SMALL_COLS), 0)
    chains = [(s, h) for s in range(nb) for h in range(GDN_HEADS)]

    seq = []
    for s in range(nb):
        gb = gb_ref[s]
        gc = gb
        sh = 1
        while sh < c:
            gc = gc + jnp.where(row >= sh, pltpu.roll(gc, sh, axis=0), 0.0)
            sh *= 2
        g_last = gc[c - 1:c, :]
        seq.append(dict(gb=gb, gc=gc, gct=gc.T, e_g=jnp.exp(gc), e_kd=jnp.exp(g_last - gc), e_last=jnp.exp(g_last)))

    st = []
    for s, h in chains:
        sl = slice(h * GDN_DK, (h + 1) * GDN_DK)
        sq = seq[s]
        qh, kh, vh = q_ref[s, :, sl], k_ref[s, :, sl], v_ref[s, :, sl]
        beta = sq["gb"][:, GDN_HEADS + h:GDN_HEADS + h + 1]
        kb = kh * beta
        aq = _dot_nt(jnp.concatenate([kb, qh], axis=0), kh)
        st.append(dict(sl=sl, qh=qh, kh=kh, kb=kb, vb=vh * beta, aq=aq))

    for (s, h), e in zip(chains, st):
        sq = seq[s]
        decay = jnp.exp(jnp.where(incl, sq["gc"][:, h:h + 1] - sq["gct"][h:h + 1, :], -jnp.inf))
        e["neg"] = -jnp.where(strict, e["aq"][:c] * decay, 0.0)
        e["attn"] = jnp.where(incl, e["aq"][c:] * decay, 0.0)
        e["x"] = jnp.concatenate([e["vb"], e["kb"] * sq["e_g"][:, h:h + 1]], axis=1)
        e["pw"] = e["neg"]

    span = 1
    while span < c:
        span *= 2
        for e in st:
            if span < c:
                both = _dot(e["pw"], jnp.concatenate([e["pw"], e["x"]], axis=1))
                e["pw"] = both[:, :c]
                e["x"] = e["x"] + both[:, c:]
            else:
                e["x"] = e["x"] + _dot(e["pw"], e["x"])

    for (s, h), e in zip(chains, st):
        sq = seq[s]
        u_val, w_k = e["x"][:, :GDN_DV], e["x"][:, GDN_DV:]
        q_dec = e["qh"] * sq["e_g"][:, h:h + 1]
        e["s_old"] = s_scr[s, h]
        ws = _dot(jnp.concatenate([w_k, q_dec], axis=0), e["s_old"])
        e["v_new"] = u_val - ws[:c]
        e["o_inter"] = ws[c:]

    for (s, h), e in zip(chains, st):
        sq = seq[s]
        o = e["o_inter"] + _dot(e["attn"], e["v_new"])
        k_dec = e["kh"] * sq["e_kd"][:, h:h + 1]
        s_scr[s, h] = e["s_old"] * sq["e_last"][:, h:h + 1] + _dot(k_dec.T, e["v_new"])
        zh = z_ref[s, :, e["sl"]]
        yb_ref[s, :, e["sl"]] = (((o * _rms_scale(o)) * ng) * (zh * jax.nn.sigmoid(zh))).astype(BF16)

    @pl.when(ci == pl.num_programs(1) - 1)
    def _():
        sout_ref[...] = s_scr[...]


def _gdn(q, k, v, p, gb, norm_g, bsz, t, c, nb, s0=None):
    d = D_MODEL
    nc = t // c
    has_s0 = s0 is not None

    def tok(col=0):
        return pl.BlockSpec((nb, c, d), lambda b, i, col=col: (b, i, col))

    in_specs = [tok(), tok(), tok(), tok(COL_GZ),
                pl.BlockSpec((nb, c, SMALL_COLS), lambda b, i: (b, i, 0)),
                pl.BlockSpec((1, GDN_DV), lambda b, i: (0, 0))]
    args = [a.reshape(bsz, t, -1) for a in (q, k, v, p, gb)] + [norm_g.reshape(1, GDN_DV)]
    s_spec = pl.BlockSpec((nb, GDN_HEADS, GDN_DK, GDN_DV), lambda b, i: (b, 0, 0, 0))
    if has_s0:
        in_specs.append(s_spec)
        args.append(s0)
    yb, s_new = pl.pallas_call(
        functools.partial(_gdn_kernel, c=c, nb=nb, has_s0=has_s0),
        grid=(bsz // nb, nc),
        in_specs=in_specs,
        out_specs=[tok(), s_spec],
        out_shape=[jax.ShapeDtypeStruct((bsz, t, d), BF16),
                   jax.ShapeDtypeStruct((bsz, GDN_HEADS, GDN_DK, GDN_DV), F32)],
        scratch_shapes=[pltpu.VMEM((nb, GDN_HEADS, GDN_DK, GDN_DV), F32)],
        compiler_params=_params(("parallel", "arbitrary")),
        name="gdn_scan",
    )(*args)
    return yb.reshape(bsz * t, d), s_new


def _mem_attn_kernel(q_ref, k_ref, v_ref, o_ref):
    scale = MEM_HEAD_DIM ** -0.5
    for h in range(MEM_HEADS):
        sl = slice(h * MEM_HEAD_DIM, (h + 1) * MEM_HEAD_DIM)
        s = _dot_nt(q_ref[:, sl], k_ref[0, :, sl]) * scale
        e = jnp.exp(s - jnp.max(s, axis=-1, keepdims=True))
        prob = e / jnp.sum(e, axis=-1, keepdims=True)
        o_ref[:, sl] = _dot(prob, v_ref[0, :, sl]).astype(BF16)


def _mem_attn(p, mk, mv, bsz, t, tq):
    d = D_MODEL
    m = mk.shape[1]
    nq = t // tq
    kv_spec = pl.BlockSpec((1, m, d), lambda b, i: (b, 0, 0))
    return pl.pallas_call(
        _mem_attn_kernel,
        grid=(bsz, nq),
        in_specs=[pl.BlockSpec((tq, d), lambda b, i: (b * nq + i, COL_MQ)), kv_spec, kv_spec],
        out_specs=pl.BlockSpec((tq, d), lambda b, i: (b * nq + i, 0)),
        out_shape=jax.ShapeDtypeStruct((bsz * t, d), BF16),
        compiler_params=_params(("parallel", "arbitrary")),
        name="mem_attn",
    )(p, mk, mv)


def _merge_kernel(x_ref, ya_ref, yb_ref, ym_ref, g0_ref, g1_ref, g2_ref, wc_ref, wg_ref, wm_ref, wo_ref,
                  ng_ref, x1_ref, h2_ref):
    d = functools.partial(jnp.dot, preferred_element_type=F32)
    merged = (jax.nn.sigmoid(g0_ref[...]) * d(ya_ref[...], wc_ref[...])
              + jax.nn.sigmoid(g1_ref[...]) * d(yb_ref[...], wg_ref[...])
              + jax.nn.sigmoid(g2_ref[...]) * d(ym_ref[...], wm_ref[...]))
    x1 = x_ref[...] + d(merged.astype(BF16), wo_ref[...])
    x1_ref[...] = x1
    h2_ref[...] = ((x1 * _rms_scale(x1)) * ng_ref[...]).astype(BF16)


def _merge(x, ya, yb, ym, p, wc, wg, wm, wo, norm_g, tm):
    n, d = x.shape
    tok = pl.BlockSpec((tm, d), lambda i: (i, 0))
    wsp = pl.BlockSpec((d, d), lambda i: (0, 0))

    def gate(j):
        return pl.BlockSpec((tm, d), lambda i, j=j: (i, COL_BG + j))

    return pl.pallas_call(
        _merge_kernel,
        grid=(n // tm,),
        in_specs=[tok, tok, tok, tok, gate(0), gate(1), gate(2), wsp, wsp, wsp, wsp,
                  pl.BlockSpec((1, d), lambda i: (0, 0))],
        out_specs=[tok, tok],
        out_shape=[jax.ShapeDtypeStruct((n, d), F32), jax.ShapeDtypeStruct((n, d), BF16)],
        compiler_params=_params(("parallel",)),
        name="merge",
    )(x, ya, yb, ym, p, p, p, wc, wg, wm, wo, norm_g.reshape(1, d))


_CAND_CELLS = tuple((a, b) for a in range(PEER_TOPK) for b in range(PEER_TOPK) if (a + 1) * (b + 1) <= PEER_TOPK)


def _top_values(x, count):
    vals = []
    cur = x
    for r in range(count):
        m = jnp.max(cur, axis=0, keepdims=True)
        vals.append(m)
        if r + 1 < count:
            cur = jnp.where(cur == m, -jnp.inf, cur)
    return vals


def _route_kernel(h_ref, wq_ref, keys_ref, cut_ref, e0_ref, s1_ref, e1_ref):
    q = jnp.dot(h_ref[...], wq_ref[...], preferred_element_type=F32).astype(BF16)
    for h in range(PEER_HEADS):
        sc, top = [], []
        for part in range(2):
            col = (2 * h + part) * PEER_D_HALF
            s_t = lax.dot_general(keys_ref[part, h], q[:, col:col + PEER_D_HALF],
                                  (((1,), (1,)), ((), ())), preferred_element_type=F32)
            sc.append(s_t)
            top.append(_top_values(s_t, PEER_TOPK))
        cand = jnp.concatenate([top[0][a] + top[1][b] for a, b in _CAND_CELLS], axis=0)
        best = _top_values(cand, PEER_TOPK)
        c_max, thr = best[0], best[-1]
        z = jnp.exp(best[0] - c_max)
        for r in range(1, PEER_TOPK):
            z = z + jnp.exp(best[r] - c_max)
        cut = jnp.full(sc[0].shape, jnp.inf, F32)
        for b in range(PEER_TOPK):
            cut = jnp.where(sc[0] + top[1][b] >= thr, top[1][b], cut)
        e0 = jnp.exp(sc[0] - top[0][0]) / z
        e1 = jnp.exp(sc[1] - top[1][0])
        for ns in range(sc[1].shape[1] // LANES):
            cols = slice(ns * LANES, (ns + 1) * LANES)
            cut_ref[h, ns] = cut[:, cols]
            e0_ref[h, ns] = e0[:, cols]
            s1_ref[h, ns] = sc[1][:, cols]
            e1_ref[h, ns] = e1[:, cols]


def _route(h2, wq, keys, tn):
    n, d = h2.shape
    nk = PEER_N_KEYS
    tiled = pl.BlockSpec((PEER_HEADS, tn // LANES, nk, LANES), lambda i: (0, i, 0, 0))
    tiled_shape = jax.ShapeDtypeStruct((PEER_HEADS, n // LANES, nk, LANES), F32)
    return pl.pallas_call(
        _route_kernel,
        grid=(n // tn,),
        in_specs=[pl.BlockSpec((tn, d), lambda i: (i, 0)),
                  pl.BlockSpec(wq.shape, lambda i: (0, 0)),
                  pl.BlockSpec(keys.shape, lambda i: (0, 0, 0, 0))],
        out_specs=[tiled, tiled, tiled, tiled],
        out_shape=[tiled_shape, tiled_shape, tiled_shape, tiled_shape],
        compiler_params=_params(("parallel",)),
        name="peer_route",
    )(h2, wq, keys)


def _gelu(x):
    return 0.5 * x * (1.0 + lax.erf(x * (2.0 ** -0.5)))


PEER_TILE = 1024
GATE_HALF = 32
GATE_ROWS = 8


def _peer_kernel(h_ref, x1_ref, u_ref, vt_ref, cut_ref, e0_ref, s1_ref, e1_ref, ng_ref,
                 y_ref, acc_ref, act_ref, coef_ref, *, te):
    ei = pl.program_id(1)
    tn = h_ref.shape[0]
    groups = te // PEER_N_KEYS

    @pl.when(ei == 0)
    def _():
        acc_ref[...] = jnp.zeros_like(acc_ref)

    nsb = tn // LANES
    act = _gelu(lax.dot_general(u_ref[...], h_ref[...], (((1,), (1,)), ((), ())),
                                preferred_element_type=F32))
    for ns in range(nsb):
        act_ref[ns] = act[:, ns * LANES:(ns + 1) * LANES]

    def lane_block(ns, carry):
        for jh in range(PEER_N_KEYS // GATE_HALF):
            jrows = slice(jh * GATE_HALF, (jh + 1) * GATE_HALF)
            for g0 in range(0, groups, GATE_ROWS):
                gates = [None] * GATE_ROWS
                for h in range(PEER_HEADS):
                    s1_t = s1_ref[h, ns, jrows, :]
                    e1_t = e1_ref[h, ns, jrows, :]
                    for k in range(GATE_ROWS):
                        i_key = ei * groups + g0 + k
                        sel = s1_t >= cut_ref[h, ns, pl.ds(i_key, 1), :]
                        term = jnp.where(sel, e1_t, 0.0) * e0_ref[h, ns, pl.ds(i_key, 1), :]
                        gates[k] = term if gates[k] is None else gates[k] + term
                for k in range(GATE_ROWS):
                    r0 = (g0 + k) * PEER_N_KEYS + jh * GATE_HALF
                    coef_ref[ns, r0:r0 + GATE_HALF, :] = (gates[k] * act_ref[ns, r0:r0 + GATE_HALF, :]).astype(BF16)
        return carry

    lax.fori_loop(0, nsb, lane_block, 0)
    coef = jnp.concatenate([coef_ref[ns] for ns in range(nsb)], axis=1)
    acc_ref[...] += jnp.dot(vt_ref[0], coef, preferred_element_type=F32)

    @pl.when(ei == pl.num_programs(1) - 1)
    def _():
        x2 = x1_ref[...] + acc_ref[...].T
        y_ref[...] = (x2 * _rms_scale(x2)) * ng_ref[...]


def _peer(h2, x1, u_bf, vt_bf, cut, e0, s1, e1, norm_g, tn):
    n, d = h2.shape
    ne = u_bf.shape[0]
    te = vt_bf.shape[2]
    nk = PEER_N_KEYS
    tok = pl.BlockSpec((tn, d), lambda i, e: (i, 0))
    tiled = pl.BlockSpec((PEER_HEADS, tn // LANES, nk, LANES), lambda i, e: (0, i, 0, 0))
    return pl.pallas_call(
        functools.partial(_peer_kernel, te=te),
        grid=(n // tn, ne // te),
        in_specs=[tok, tok,
                  pl.BlockSpec((te, d), lambda i, e: (e, 0)),
                  pl.BlockSpec((1, d, te), lambda i, e: (e, 0, 0)),
                  tiled, tiled, tiled, tiled,
                  pl.BlockSpec((1, d), lambda i, e: (0, 0))],
        out_specs=tok,
        out_shape=jax.ShapeDtypeStruct((n, d), F32),
        scratch_shapes=[pltpu.VMEM((d, tn), F32), pltpu.VMEM((tn // LANES, te, LANES), F32),
                        pltpu.VMEM((tn // LANES, te, LANES), BF16)],
        compiler_params=_params(("parallel", "arbitrary")),
        name="peer_mix",
    )(h2, x1, u_bf, vt_bf, cut, e0, s1, e1, norm_g.reshape(1, d))


def _tile(n, pref):
    return math.gcd(n, pref)


def _layer(x, mk, mv, bsz, t, weights, conv_state=None, gdn_conv_state=None, gdn_s0=None):
    (norm_mix_g, w_main, w_ab, conv_w, gdn_conv_w, a_log, dt_bias, gdn_norm_g, wc, wg, wm, wo, norm_ffn_g,
     wq, keys, u_bf, vt_bf, final_norm_g) = weights
    n = bsz * t
    sample = conv_state is not None
    p, pab = _inproj(x, norm_mix_g, w_main, w_ab, _tile(n, 1024))
    rows = _tile(n, 256) if sample else _tile(t, 512)
    ya, q, k, v, gb, u_tail = _prep(p, pab, conv_w, gdn_conv_w, a_log, dt_bias, rows, max(t // rows, 1),
                                    conv_state, gdn_conv_state)
    chunk = _tile(t, 64)
    if sample:
        yb, s_new = _gdn(q, k, v, p, gb, gdn_norm_g, bsz, t, chunk, 4, gdn_s0)
    else:
        yb, s_new = _gdn_stacked(q, k, v, p, gb, gdn_norm_g, bsz, t, chunk, _tile(bsz, 2))
    ym = _mem_attn(p, mk, mv, bsz, t, _tile(t, 512))
    x1, h2 = _merge(x, ya, yb, ym, p, wc, wg, wm, wo, norm_ffn_g, _tile(n, 256))
    tn = _tile(n, 512)
    cut, e0, s1, e1 = _route(h2, wq, keys, tn)
    y = _peer(h2, x1, u_bf, vt_bf, cut, e0, s1, e1, final_norm_g, tn)
    return y, u_tail, p, s_new


def kernel(x_prompt, x_sample, mem_prompt, cache_mem_k, cache_mem_v, state_conv, state_gdn_conv, state_gdn, norm_mix_g, w_in, conv_w, gdn_conv_w, gdn_a_log, gdn_dt_bias, gdn_norm_g, mem_norm_g, w_mem_kv, w_br_conv, w_br_gdn, w_br_mem, w_out, norm_ffn_g, peer_w_q, peer_sub_keys, peer_u, peer_v, final_norm_g):
    depth = w_in.shape[0]
    assert depth == 1, "single-layer step"
    d = D_MODEL
    bp, tp, _ = x_prompt.shape
    bs, ts, _ = x_sample.shape
    assert ts == SAMPLE_PAD_T // 2
    m_tok = mem_prompt.shape[1]
    mq = MEM_HEADS * MEM_HEAD_DIM
    l = 0

    w = w_in[l]
    ga0 = 7 * d
    w_main = jnp.concatenate([w[:, :ga0], w[:, ga0 + 2 * GDN_HEADS:]], axis=1).astype(BF16)
    w_ab = jnp.pad(w[:, ga0:ga0 + 2 * GDN_HEADS], ((0, 0), (0, SMALL_COLS - 2 * GDN_HEADS))).astype(BF16)
    weights = (norm_mix_g[l], w_main, w_ab, conv_w[l], gdn_conv_w[l], gdn_a_log[l], gdn_dt_bias[l], gdn_norm_g[l],
               w_br_conv[l].astype(BF16), w_br_gdn[l].astype(BF16), w_br_mem[l].astype(BF16), w_out[l].astype(BF16),
               norm_ffn_g[l], peer_w_q[l].astype(BF16), peer_sub_keys[l].astype(BF16),
               peer_u[l].astype(BF16),
               peer_v[l].astype(BF16).reshape(-1, PEER_TILE, d).transpose(0, 2, 1), final_norm_g)

    kv_shape = (bp, m_tok, mq)
    mk_p, mv_p = _mem_kv(mem_prompt.reshape(bp * m_tok, d), mem_norm_g[l], w_mem_kv[l].astype(BF16),
                         _tile(bp * m_tok, 256))
    y_p, u_tail_p, p_p, s_p = _layer(x_prompt.reshape(bp * tp, d), mk_p.reshape(kv_shape), mv_p.reshape(kv_shape),
                                     bp, tp, weights)
    rows_p = _tile(tp, 512)
    cb_p = u_tail_p.reshape(bp, tp // rows_p, SUBLANES, d)[:, -1, SUBLANES - (CONV_K - 1):, :]
    gb_p = p_p.reshape(bp, tp, -1)[:, tp - (GDN_CONV_K - 1):, COL_GQ * d:(COL_GV + 1) * d]

    pad_t = SAMPLE_PAD_T
    lead = pad_t - ts
    x_s = jnp.pad(x_sample, ((0, 0), (lead, 0), (0, 0))).reshape(bs * pad_t, d)
    cst = jnp.pad(state_conv[l], ((0, 0), (lead - (CONV_K - 1), ts), (0, 0))).reshape(bs * pad_t, d)
    gst = jnp.pad(state_gdn_conv[l], ((0, 0), (lead - (GDN_CONV_K - 1), ts), (0, 0))).reshape(bs * pad_t, 3 * d)
    y_s, u_s, p_s, s_s = _layer(x_s, cache_mem_k.reshape(bs, -1, mq), cache_mem_v.reshape(bs, -1, mq),
                                bs, pad_t, weights, cst, gst, state_gdn.reshape(bs, GDN_HEADS, GDN_DK, GDN_DV))
    y_s = y_s.reshape(bs, pad_t, d)[:, lead:, :]
    cb_s = u_s.reshape(bs, pad_t, d)[:, pad_t - (CONV_K - 1):, :]
    gb_s = p_s.reshape(bs, pad_t, -1)[:, pad_t - (GDN_CONV_K - 1):, COL_GQ * d:(COL_GV + 1) * d]

    mshape = (depth, bp, m_tok, MEM_HEADS, MEM_HEAD_DIM)
    return (y_p.reshape(bp, tp, d), y_s,
            mk_p.reshape(mshape), mv_p.reshape(mshape),
            cb_p[None], gb_p[None], s_p[None],
            cb_s[None], gb_s[None], s_s[None])
```

```python
import functools
import math

import jax
import jax.numpy as jnp
from jax import lax
from jax.experimental import pallas as pl
from jax.experimental.pallas import tpu as pltpu

F32 = jnp.float32
BF16 = jnp.bfloat16

EPS = 1e-6
D_MODEL = 1024
CONV_K = 3
GDN_HEADS = 8
GDN_DK = 128
GDN_DV = 128
GDN_CONV_K = 4
MEM_HEADS = 4
MEM_HEAD_DIM = 256
PEER_HEADS = 8
PEER_N_KEYS = 128
PEER_D_HALF = 128
PEER_TOPK = 16
N_BRANCHES = 3

COL_AU, COL_AB, COL_AC, COL_GQ, COL_GK, COL_GV, COL_GZ, COL_MQ, COL_BG = 0, 1, 2, 3, 4, 5, 6, 7, 8
N_MAIN_BLOCKS = 11
SMALL_COLS = 128
SUBLANES = 8
LANES = 128
SAMPLE_PAD_T = 8
VMEM_LIMIT = 56 * 1024 * 1024


def _params(sem):
    return pltpu.CompilerParams(dimension_semantics=sem, vmem_limit_bytes=VMEM_LIMIT)


def _dot(a, b):
    return jnp.dot(a.astype(BF16), b.astype(BF16), preferred_element_type=F32)


def _dot_nt(a, b):
    return lax.dot_general(a.astype(BF16), b.astype(BF16), (((1,), (1,)), ((), ())),
                           preferred_element_type=F32)


def _rms_scale(x):
    return lax.rsqrt(jnp.mean(x * x, axis=-1, keepdims=True) + EPS)


def _inproj_kernel(x_ref, g_ref, w_ref, wab_ref, p_ref, pab_ref, h_ref):
    @pl.when(pl.program_id(1) == 0)
    def _():
        x = x_ref[...]
        h = ((x * _rms_scale(x)) * g_ref[...]).astype(BF16)
        h_ref[...] = h
        pab_ref[...] = jnp.dot(h, wab_ref[...], preferred_element_type=F32)

    p_ref[...] = jnp.dot(h_ref[...], w_ref[...], preferred_element_type=F32).astype(p_ref.dtype)


def _inproj(x, g, w_main, w_ab, tm):
    n, d = x.shape
    nblk = w_main.shape[1] // d
    return pl.pallas_call(
        _inproj_kernel,
        grid=(n // tm, nblk),
        in_specs=[
            pl.BlockSpec((tm, d), lambda i, j: (i, 0)),
            pl.BlockSpec((1, d), lambda i, j: (0, 0)),
            pl.BlockSpec((d, d), lambda i, j: (0, j)),
            pl.BlockSpec((d, SMALL_COLS), lambda i, j: (0, 0)),
        ],
        out_specs=[
            pl.BlockSpec((tm, d), lambda i, j: (i, j)),
            pl.BlockSpec((tm, SMALL_COLS), lambda i, j: (i, 0)),
        ],
        out_shape=[
            jax.ShapeDtypeStruct((n, nblk * d), BF16),
            jax.ShapeDtypeStruct((n, SMALL_COLS), F32),
        ],
        scratch_shapes=[pltpu.VMEM((tm, d), BF16)],
        compiler_params=_params(("parallel", "arbitrary")),
        name="inproj",
    )(x, g.reshape(1, d), w_main, w_ab)


def _mem_kv_kernel(x_ref, g_ref, w_ref, k_ref, v_ref):
    x = x_ref[...]
    h = ((x * _rms_scale(x)) * g_ref[...]).astype(BF16)
    kv = jnp.dot(h, w_ref[...], preferred_element_type=F32)
    half = kv.shape[1] // 2
    k_ref[...] = kv[:, :half]
    v_ref[...] = kv[:, half:]


def _mem_kv(x, g, w, tm):
    n, d = x.shape
    c = w.shape[1]
    out = jax.ShapeDtypeStruct((n, c // 2), F32)
    ospec = pl.BlockSpec((tm, c // 2), lambda i: (i, 0))
    return pl.pallas_call(
        _mem_kv_kernel,
        grid=(n // tm,),
        in_specs=[
            pl.BlockSpec((tm, d), lambda i: (i, 0)),
            pl.BlockSpec((1, d), lambda i: (0, 0)),
            pl.BlockSpec((d, c), lambda i: (0, 0)),
        ],
        out_specs=[ospec, ospec],
        out_shape=[out, out],
        compiler_params=_params(("parallel",)),
        name="mem_kv",
    )(x, g.reshape(1, d), w)


def _causal_conv(prev, cur, w_ref, k):
    full = jnp.concatenate([prev, cur], axis=0)
    y = cur * w_ref[k - 1:k, :]
    for s in range(1, k):
        y = y + pltpu.roll(full, s, axis=0)[prev.shape[0]:, :] * w_ref[k - 1 - s:k - s, :]
    return y


def _f32(ref):
    return ref[...].astype(F32)


def _prep_kernel(*refs, rows, tiles_per_seq, sample):
    if sample:
        (au, ab, ac, gq, gk, gv, pab, au_p, ac_p, gq_p, gk_p, gv_p, cw, gw, alog, dtb, cst, gst,
         ya_o, q_o, k_o, v_o, gb_o, ut_o) = refs
    else:
        (au, ab, ac, gq, gk, gv, pab, au_p, ac_p, gq_p, gk_p, gv_p, cw, gw, alog, dtb,
         ya_o, q_o, k_o, v_o, gb_o, ut_o) = refs
    d = D_MODEL
    first = (pl.program_id(0) % tiles_per_seq) == 0
    keep = jnp.where(first, 0.0, 1.0).astype(F32)

    u = _f32(ac) * _f32(au)
    u_prev = (_f32(ac_p) * _f32(au_p)) * keep
    if sample:
        real = (lax.broadcasted_iota(jnp.int32, (rows, 1), 0) % SAMPLE_PAD_T) >= (SAMPLE_PAD_T // 2)
        u = jnp.where(real, u, cst[...])
    ya_o[...] = (_f32(ab) * _causal_conv(u_prev, u, cw, CONV_K)).astype(BF16)
    ut_o[...] = u[rows - ut_o.shape[0]:, :]

    outs = (q_o, k_o, v_o)
    for idx, (cur_ref, prev_ref) in enumerate(((gq, gq_p), (gk, gk_p), (gv, gv_p))):
        cur = _f32(cur_ref)
        if sample:
            cur = jnp.where(real, cur, gst[:, idx * d:(idx + 1) * d])
        prev = _f32(prev_ref) * keep
        wslab = gw.at[:, idx * d:(idx + 1) * d]
        y = _causal_conv(prev, cur, wslab, GDN_CONV_K)
        y = y * jax.nn.sigmoid(y)
        if idx < 2:
            scale = GDN_DK ** -0.5 if idx == 0 else 1.0
            for h in range(GDN_HEADS):
                yh = y[:, h * GDN_DK:(h + 1) * GDN_DK]
                r = lax.rsqrt(jnp.sum(yh * yh, axis=-1, keepdims=True) + EPS)
                outs[idx][:, h * GDN_DK:(h + 1) * GDN_DK] = yh * (r * scale)
        else:
            outs[idx][...] = y

    pa = pab[...]
    lane = lax.broadcasted_iota(jnp.int32, pa.shape, 1)
    z = pa + dtb[...]
    softplus = jnp.maximum(z, 0.0) + jnp.log1p(jnp.exp(-jnp.abs(z)))
    g_log = -jnp.exp(alog[...]) * softplus
    gb = jnp.where(lane < GDN_HEADS, g_log, jnp.where(lane < 2 * GDN_HEADS, jax.nn.sigmoid(pa), 0.0))
    if sample:
        gb = jnp.where(real, gb, 0.0)
    gb_o[...] = gb


def _prep(p, pab, conv_w, gdn_conv_w, a_log, dt_bias, rows, tiles_per_seq, conv_state=None, gdn_state=None):
    n = p.shape[0]
    d = D_MODEL
    sample = conv_state is not None
    nt = n // rows
    prev_rows = 2 * SUBLANES
    rpb = rows // prev_rows

    def cur(c):
        return pl.BlockSpec((rows, d), lambda i, c=c: (i, c))

    def prev(c):
        return pl.BlockSpec((prev_rows, d), lambda i, c=c: (jnp.maximum(i * rpb - 1, 0), c))

    alog128 = jnp.zeros((1, SMALL_COLS), F32).at[0, :GDN_HEADS].set(a_log)
    dtb128 = jnp.zeros((1, SMALL_COLS), F32).at[0, :GDN_HEADS].set(dt_bias)
    in_specs = [cur(COL_AU), cur(COL_AB), cur(COL_AC), cur(COL_GQ), cur(COL_GK), cur(COL_GV),
                pl.BlockSpec((rows, SMALL_COLS), lambda i: (i, 0)),
                prev(COL_AU), prev(COL_AC), prev(COL_GQ), prev(COL_GK), prev(COL_GV),
                pl.BlockSpec((CONV_K, d), lambda i: (0, 0)),
                pl.BlockSpec((GDN_CONV_K, 3 * d), lambda i: (0, 0)),
                pl.BlockSpec((1, SMALL_COLS), lambda i: (0, 0)),
                pl.BlockSpec((1, SMALL_COLS), lambda i: (0, 0))]
    args = [p, p, p, p, p, p, pab, p, p, p, p, p, conv_w, gdn_conv_w, alog128, dtb128]
    if sample:
        in_specs += [pl.BlockSpec((rows, d), lambda i: (i, 0)), pl.BlockSpec((rows, 3 * d), lambda i: (i, 0))]
        args += [conv_state, gdn_state]
    tail = rows if sample else SUBLANES
    out_specs = [pl.BlockSpec((rows, d), lambda i: (i, 0))] * 4 + [
        pl.BlockSpec((rows, SMALL_COLS), lambda i: (i, 0)),
        pl.BlockSpec((tail, d), lambda i: (i, 0))]
    out_shape = [jax.ShapeDtypeStruct((n, d), BF16)] + [jax.ShapeDtypeStruct((n, d), F32)] * 3 + [
        jax.ShapeDtypeStruct((n, SMALL_COLS), F32),
        jax.ShapeDtypeStruct((nt * tail, d), F32)]
    return pl.pallas_call(
        functools.partial(_prep_kernel, rows=rows, tiles_per_seq=tiles_per_seq, sample=sample),
        grid=(nt,),
        in_specs=in_specs,
        out_specs=out_specs,
        out_shape=out_shape,
        compiler_params=_params(("parallel",)),
        name="prep",
    )(*args)


def _chunk_cumsum(g, c):
    row = lax.broadcasted_iota(jnp.int32, g.shape, 0)
    sh = 1
    while sh < c:
        g = g + jnp.where(row >= sh, pltpu.roll(g, sh, axis=0), 0.0)
        sh *= 2
    return g


def _gdn_stacked_kernel(q_ref, k_ref, v_ref, z_ref, gb_ref, ng_ref, yb_ref, sout_ref, s_scr, *, c, grp, nb):
    ci = pl.program_id(1)

    @pl.when(ci == 0)
    def _():
        s_scr[...] = jnp.zeros_like(s_scr)

    r = grp * c
    ng = ng_ref[...]
    seqs = []
    for s in range(nb):
        gb = gb_ref[s]
        gc = _chunk_cumsum(gb, c)
        g_last = gc[c - 1:c, :]
        seqs.append(dict(gb=gb, gc=gc, gct=gc.T, e_g=jnp.exp(gc), e_kd=jnp.exp(g_last - gc), e_last=jnp.exp(g_last)))

    ri = lax.broadcasted_iota(jnp.int32, (r, r), 0)
    cj = lax.broadcasted_iota(jnp.int32, (r, r), 1)
    same = (ri // c) == (cj // c)
    incl = same & (ri >= cj)
    strict = same & (ri > cj)
    head_block = (lax.broadcasted_iota(jnp.int32, (r, grp * GDN_DK), 0) // c
                  == lax.broadcasted_iota(jnp.int32, (r, grp * GDN_DK), 1) // GDN_DK)

    def block_layout(x):
        return jnp.where(head_block, jnp.concatenate([x] * grp, axis=1), 0.0)

    units = []
    for s in range(nb):
        for g0 in range(0, GDN_HEADS, grp):
            heads = list(range(g0, g0 + grp))
            sq = seqs[s]

            def stack(ref, s=s, heads=heads):
                return jnp.concatenate([ref[s, :, h * GDN_DK:(h + 1) * GDN_DK] for h in heads], axis=0)

            def col(a, off=0, heads=heads):
                return jnp.concatenate([a[:, off + h:off + h + 1] for h in heads], axis=0)

            qs, ks, vs = stack(q_ref), stack(k_ref), stack(v_ref)
            beta = col(sq["gb"], GDN_HEADS)
            kb = ks * beta
            units.append(dict(s=s, heads=heads, qs=qs, ks=ks, kb=kb, vb=vs * beta,
                              gcol=col(sq["gc"]), egcol=col(sq["e_g"]), ekdcol=col(sq["e_kd"]),
                              grow=jnp.concatenate([sq["gct"][h:h + 1, :] for h in heads], axis=1),
                              rows=slice((s * GDN_HEADS + g0) * GDN_DK, (s * GDN_HEADS + g0 + grp) * GDN_DK)))

    for u in units:
        u["aq"] = _dot_nt(jnp.concatenate([u["kb"], u["qs"]], axis=0), u["ks"])
    for u in units:
        decay = jnp.exp(jnp.where(incl, u["gcol"] - u["grow"], -jnp.inf))
        u["pw"] = -jnp.where(strict, u["aq"][:r] * decay, 0.0)
        u["attn"] = jnp.where(incl, u["aq"][r:] * decay, 0.0)
        u["x"] = jnp.concatenate([u["vb"], u["kb"] * u["egcol"]], axis=1)
    span = 1
    while span < c:
        span *= 2
        for u in units:
            if span < c:
                both = _dot(u["pw"], jnp.concatenate([u["pw"], u["x"]], axis=1))
                u["pw"] = both[:, :r]
                u["x"] = u["x"] + both[:, r:]
            else:
                u["x"] = u["x"] + _dot(u["pw"], u["x"])
    for u in units:
        u_val, w_k = u["x"][:, :GDN_DV], u["x"][:, GDN_DV:]
        u["s_old"] = s_scr[u["rows"], :]
        ws = _dot(jnp.concatenate([block_layout(w_k), block_layout(u["qs"] * u["egcol"])], axis=0), u["s_old"])
        u["v_new"] = u_val - ws[:r]
        u["o"] = ws[r:]
    for u in units:
        e_last = seqs[u["s"]]["e_last"]
        o = u["o"] + _dot(u["attn"], u["v_new"])
        e_rows = jnp.concatenate([jnp.broadcast_to(e_last[:, h:h + 1], (GDN_DK, 1)) for h in u["heads"]], axis=0)
        kd_t = block_layout(u["ks"] * u["ekdcol"]).T
        s_scr[u["rows"], :] = u["s_old"] * e_rows + _dot(kd_t, u["v_new"])
        for i, h in enumerate(u["heads"]):
            sl = slice(h * GDN_DK, (h + 1) * GDN_DK)
            oh = o[i * c:(i + 1) * c, :]
            zh = z_ref[u["s"], :, sl].astype(F32)
            yb_ref[u["s"], :, sl] = (((oh * _rms_scale(oh)) * ng) * (zh * jax.nn.sigmoid(zh))).astype(BF16)

    @pl.when(ci == pl.num_programs(1) - 1)
    def _():
        sout_ref[...] = s_scr[...]


def _gdn_stacked(q, k, v, p, gb, norm_g, bsz, t, c, nb):
    d = D_MODEL
    nc = t // c
    grp = min(GDN_HEADS, 256 // c)
    s_rows = nb * GDN_HEADS * GDN_DK

    def tok(col=0):
        return pl.BlockSpec((nb, c, d), lambda b, i, col=col: (b, i, col))

    yb, s_new = pl.pallas_call(
        functools.partial(_gdn_stacked_kernel, c=c, grp=grp, nb=nb),
        grid=(bsz // nb, nc),
        in_specs=[tok(), tok(), tok(), tok(COL_GZ),
                  pl.BlockSpec((nb, c, SMALL_COLS), lambda b, i: (b, i, 0)),
                  pl.BlockSpec((1, GDN_DV), lambda b, i: (0, 0))],
        out_specs=[tok(), pl.BlockSpec((s_rows, GDN_DV), lambda b, i: (b, 0))],
        out_shape=[jax.ShapeDtypeStruct((bsz, t, d), BF16),
                   jax.ShapeDtypeStruct((bsz * GDN_HEADS * GDN_DK, GDN_DV), F32)],
        scratch_shapes=[pltpu.VMEM((s_rows, GDN_DV), F32)],
        compiler_params=_params(("parallel", "arbitrary")),
        name="gdn_scan_prompt",
    )(*[a.reshape(bsz, t, -1) for a in (q, k, v, p, gb)], norm_g.reshape(1, GDN_DV))
    return yb.reshape(bsz * t, d), s_new.reshape(bsz, GDN_HEADS, GDN_DK, GDN_DV)


def _gdn_kernel(*refs, c, nb, has_s0):
    if has_s0:
        q_ref, k_ref, v_ref, z_ref, gb_ref, ng_ref, s0_ref, yb_ref, sout_ref, s_scr = refs
    else:
        q_ref, k_ref, v_ref, z_ref, gb_ref, ng_ref, yb_ref, sout_ref, s_scr = refs
    ci = pl.program_id(1)

    @pl.when(ci == 0)
    def _():
        if has_s0:
            s_scr[...] = s0_ref[...]
        else:
            s_scr[...] = jnp.zeros_like(s_scr)

    ri = lax.broadcasted_iota(jnp.int32, (c, c), 0)
    cj = lax.broadcasted_iota(jnp.int32, (c, c), 1)
    incl = ri >= cj
    strict = ri > cj
    ng = ng_ref[...]
    row = lax.broadcasted_iota(jnp.int32, (c, SMALL_COLS), 0)
    chains = [(s, h) for s in range(nb) for h in range(GDN_HEADS)]

    seq = []
    for s in range(nb):
        gb = gb_ref[s]
        gc = gb
        sh = 1
        while sh < c:
            gc = gc + jnp.where(row >= sh, pltpu.roll(gc, sh, axis=0), 0.0)
            sh *= 2
        g_last = gc[c - 1:c, :]
        seq.append(dict(gb=gb, gc=gc, gct=gc.T, e_g=jnp.exp(gc), e_kd=jnp.exp(g_last - gc), e_last=jnp.exp(g_last)))

    st = []
    for s, h in chains:
        sl = slice(h * GDN_DK, (h + 1) * GDN_DK)
        sq = seq[s]
        qh, kh, vh = q_ref[s, :, sl], k_ref[s, :, sl], v_ref[s, :, sl]
        beta = sq["gb"][:, GDN_HEADS + h:GDN_HEADS + h + 1]
        kb = kh * beta
        aq = _dot_nt(jnp.concatenate([kb, qh], axis=0), kh)
        st.append(dict(sl=sl, qh=qh, kh=kh, kb=kb, vb=vh * beta, aq=aq))

    for (s, h), e in zip(chains, st):
        sq = seq[s]
        decay = jnp.exp(jnp.where(incl, sq["gc"][:, h:h + 1] - sq["gct"][h:h + 1, :], -jnp.inf))
        e["neg"] = -jnp.where(strict, e["aq"][:c] * decay, 0.0)
        e["attn"] = jnp.where(incl, e["aq"][c:] * decay, 0.0)
        e["x"] = jnp.concatenate([e["vb"], e["kb"] * sq["e_g"][:, h:h + 1]], axis=1)
        e["pw"] = e["neg"]

    span = 1
    while span < c:
        span *= 2
        for e in st:
            if span < c:
                both = _dot(e["pw"], jnp.concatenate([e["pw"], e["x"]], axis=1))
                e["pw"] = both[:, :c]
                e["x"] = e["x"] + both[:, c:]
            else:
                e["x"] = e["x"] + _dot(e["pw"], e["x"])

    for (s, h), e in zip(chains, st):
        sq = seq[s]
        u_val, w_k = e["x"][:, :GDN_DV], e["x"][:, GDN_DV:]
        q_dec = e["qh"] * sq["e_g"][:, h:h + 1]
        e["s_old"] = s_scr[s, h]
        ws = _dot(jnp.concatenate([w_k, q_dec], axis=0), e["s_old"])
        e["v_new"] = u_val - ws[:c]
        e["o_inter"] = ws[c:]

    for (s, h), e in zip(chains, st):
        sq = seq[s]
        o = e["o_inter"] + _dot(e["attn"], e["v_new"])
        k_dec = e["kh"] * sq["e_kd"][:, h:h + 1]
        s_scr[s, h] = e["s_old"] * sq["e_last"][:, h:h + 1] + _dot(k_dec.T, e["v_new"])
        zh = z_ref[s, :, e["sl"]].astype(F32)
        yb_ref[s, :, e["sl"]] = (((o * _rms_scale(o)) * ng) * (zh * jax.nn.sigmoid(zh))).astype(BF16)

    @pl.when(ci == pl.num_programs(1) - 1)
    def _():
        sout_ref[...] = s_scr[...]


def _gdn(q, k, v, p, gb, norm_g, bsz, t, c, nb, s0=None):
    d = D_MODEL
    nc = t // c
    has_s0 = s0 is not None

    def tok(col=0):
        return pl.BlockSpec((nb, c, d), lambda b, i, col=col: (b, i, col))

    in_specs = [tok(), tok(), tok(), tok(COL_GZ),
                pl.BlockSpec((nb, c, SMALL_COLS), lambda b, i: (b, i, 0)),
                pl.BlockSpec((1, GDN_DV), lambda b, i: (0, 0))]
    args = [a.reshape(bsz, t, -1) for a in (q, k, v, p, gb)] + [norm_g.reshape(1, GDN_DV)]
    s_spec = pl.BlockSpec((nb, GDN_HEADS, GDN_DK, GDN_DV), lambda b, i: (b, 0, 0, 0))
    if has_s0:
        in_specs.append(s_spec)
        args.append(s0)
    yb, s_new = pl.pallas_call(
        functools.partial(_gdn_kernel, c=c, nb=nb, has_s0=has_s0),
        grid=(bsz // nb, nc),
        in_specs=in_specs,
        out_specs=[tok(), s_spec],
        out_shape=[jax.ShapeDtypeStruct((bsz, t, d), BF16),
                   jax.ShapeDtypeStruct((bsz, GDN_HEADS, GDN_DK, GDN_DV), F32)],
        scratch_shapes=[pltpu.VMEM((nb, GDN_HEADS, GDN_DK, GDN_DV), F32)],
        compiler_params=_params(("parallel", "arbitrary")),
        name="gdn_scan",
    )(*args)
    return yb.reshape(bsz * t, d), s_new


def _mem_attn_kernel(q_ref, k_ref, v_ref, o_ref, *, nb, tq):
    scale = MEM_HEAD_DIM ** -0.5
    for s in range(nb):
        rows = slice(s * tq, (s + 1) * tq)
        for h in range(MEM_HEADS):
            sl = slice(h * MEM_HEAD_DIM, (h + 1) * MEM_HEAD_DIM)
            sc = _dot_nt(q_ref[rows, sl], k_ref[s, :, sl]) * scale
            e = jnp.exp(sc - jnp.max(sc, axis=-1, keepdims=True))
            prob = e / jnp.sum(e, axis=-1, keepdims=True)
            o_ref[rows, sl] = _dot(prob, v_ref[s, :, sl]).astype(BF16)


def _mem_attn(p, mk, mv, bsz, t, tq, nb):
    d = D_MODEL
    m = mk.shape[1]
    nq = t // tq
    assert nb == 1 or nq == 1
    kv_spec = pl.BlockSpec((nb, m, d), lambda b, i: (b, 0, 0))
    return pl.pallas_call(
        functools.partial(_mem_attn_kernel, nb=nb, tq=tq),
        grid=(bsz // nb, nq),
        in_specs=[pl.BlockSpec((nb * tq, d), lambda b, i: (b * nq + i, COL_MQ)), kv_spec, kv_spec],
        out_specs=pl.BlockSpec((nb * tq, d), lambda b, i: (b * nq + i, 0)),
        out_shape=jax.ShapeDtypeStruct((bsz * t, d), BF16),
        compiler_params=_params(("parallel", "arbitrary")),
        name="mem_attn",
    )(p, mk, mv)


def _merge_kernel(x_ref, ya_ref, yb_ref, ym_ref, g0_ref, g1_ref, g2_ref, wc_ref, wg_ref, wm_ref, wo_ref,
                  ng_ref, x1_ref, h2_ref):
    d = functools.partial(jnp.dot, preferred_element_type=F32)
    merged = (jax.nn.sigmoid(_f32(g0_ref)) * d(ya_ref[...], wc_ref[...])
              + jax.nn.sigmoid(_f32(g1_ref)) * d(yb_ref[...], wg_ref[...])
              + jax.nn.sigmoid(_f32(g2_ref)) * d(ym_ref[...], wm_ref[...]))
    x1 = x_ref[...] + d(merged.astype(BF16), wo_ref[...])
    x1_ref[...] = x1
    h2_ref[...] = ((x1 * _rms_scale(x1)) * ng_ref[...]).astype(BF16)


def _merge(x, ya, yb, ym, p, wc, wg, wm, wo, norm_g, tm):
    n, d = x.shape
    tok = pl.BlockSpec((tm, d), lambda i: (i, 0))
    wsp = pl.BlockSpec((d, d), lambda i: (0, 0))

    def gate(j):
        return pl.BlockSpec((tm, d), lambda i, j=j: (i, COL_BG + j))

    return pl.pallas_call(
        _merge_kernel,
        grid=(n // tm,),
        in_specs=[tok, tok, tok, tok, gate(0), gate(1), gate(2), wsp, wsp, wsp, wsp,
                  pl.BlockSpec((1, d), lambda i: (0, 0))],
        out_specs=[tok, tok],
        out_shape=[jax.ShapeDtypeStruct((n, d), F32), jax.ShapeDtypeStruct((n, d), BF16)],
        compiler_params=_params(("parallel",)),
        name="merge",
    )(x, ya, yb, ym, p, p, p, wc, wg, wm, wo, norm_g.reshape(1, d))


_CAND_CELLS = tuple((a, b) for a in range(PEER_TOPK) for b in range(PEER_TOPK) if (a + 1) * (b + 1) <= PEER_TOPK)


def _top_values(x, count):
    vals = []
    cur = x
    for r in range(count):
        m = jnp.max(cur, axis=0, keepdims=True)
        vals.append(m)
        if r + 1 < count:
            cur = jnp.where(cur == m, -jnp.inf, cur)
    return vals


def _route_kernel(h_ref, wq_ref, keys_ref, cut_ref, e0_ref, s1_ref, e1_ref):
    q = jnp.dot(h_ref[...], wq_ref[...], preferred_element_type=F32).astype(BF16)
    for h in range(PEER_HEADS):
        sc, top = [], []
        for part in range(2):
            col = (2 * h + part) * PEER_D_HALF
            s_t = lax.dot_general(keys_ref[part, h], q[:, col:col + PEER_D_HALF],
                                  (((1,), (1,)), ((), ())), preferred_element_type=F32)
            sc.append(s_t)
            top.append(_top_values(s_t, PEER_TOPK))
        cand = jnp.concatenate([top[0][a] + top[1][b] for a, b in _CAND_CELLS], axis=0)
        best = _top_values(cand, PEER_TOPK)
        c_max, thr = best[0], best[-1]
        z = jnp.exp(best[0] - c_max)
        for r in range(1, PEER_TOPK):
            z = z + jnp.exp(best[r] - c_max)
        cut = jnp.full(sc[0].shape, jnp.inf, F32)
        for b in range(PEER_TOPK):
            cut = jnp.where(sc[0] + top[1][b] >= thr, top[1][b], cut)
        e0 = jnp.exp(sc[0] - top[0][0]) / z
        e1 = jnp.exp(sc[1] - top[1][0])
        for ns in range(sc[1].shape[1] // LANES):
            cols = slice(ns * LANES, (ns + 1) * LANES)
            cut_ref[h, ns] = cut[:, cols]
            e0_ref[h, ns] = e0[:, cols]
            s1_ref[h, ns] = sc[1][:, cols]
            e1_ref[h, ns] = e1[:, cols]


def _route(h2, wq, keys, tn):
    n, d = h2.shape
    nk = PEER_N_KEYS
    tiled = pl.BlockSpec((PEER_HEADS, tn // LANES, nk, LANES), lambda i: (0, i, 0, 0))
    tiled_shape = jax.ShapeDtypeStruct((PEER_HEADS, n // LANES, nk, LANES), F32)
    return pl.pallas_call(
        _route_kernel,
        grid=(n // tn,),
        in_specs=[pl.BlockSpec((tn, d), lambda i: (i, 0)),
                  pl.BlockSpec(wq.shape, lambda i: (0, 0)),
                  pl.BlockSpec(keys.shape, lambda i: (0, 0, 0, 0))],
        out_specs=[tiled, tiled, tiled, tiled],
        out_shape=[tiled_shape, tiled_shape, tiled_shape, tiled_shape],
        compiler_params=_params(("parallel",)),
        name="peer_route",
    )(h2, wq, keys)


def _gelu(x):
    return 0.5 * x * (1.0 + lax.erf(x * (2.0 ** -0.5)))


PEER_TILE = 1024
GATE_HALF = 32
GATE_ROWS = 8


def _peer_kernel(h_ref, x1_ref, u_ref, vt_ref, cut_ref, e0_ref, s1_ref, e1_ref, ng_ref,
                 y_ref, acc_ref, act_ref, coef_ref, *, te):
    ei = pl.program_id(1)
    tn = h_ref.shape[0]
    groups = te // PEER_N_KEYS

    @pl.when(ei == 0)
    def _():
        acc_ref[...] = jnp.zeros_like(acc_ref)

    nsb = tn // LANES
    act = _gelu(lax.dot_general(u_ref[...], h_ref[...], (((1,), (1,)), ((), ())),
                                preferred_element_type=F32))
    for ns in range(nsb):
        act_ref[ns] = act[:, ns * LANES:(ns + 1) * LANES]

    def lane_block(ns, carry):
        for jh in range(PEER_N_KEYS // GATE_HALF):
            jrows = slice(jh * GATE_HALF, (jh + 1) * GATE_HALF)
            for g0 in range(0, groups, GATE_ROWS):
                gates = [None] * GATE_ROWS
                for h in range(PEER_HEADS):
                    s1_t = s1_ref[h, ns, jrows, :]
                    e1_t = e1_ref[h, ns, jrows, :]
                    for k in range(GATE_ROWS):
                        i_key = ei * groups + g0 + k
                        sel = s1_t >= cut_ref[h, ns, pl.ds(i_key, 1), :]
                        term = jnp.where(sel, e1_t, 0.0) * e0_ref[h, ns, pl.ds(i_key, 1), :]
                        gates[k] = term if gates[k] is None else gates[k] + term
                for k in range(GATE_ROWS):
                    r0 = (g0 + k) * PEER_N_KEYS + jh * GATE_HALF
                    coef_ref[ns, r0:r0 + GATE_HALF, :] = (gates[k] * act_ref[ns, r0:r0 + GATE_HALF, :]).astype(BF16)
        return carry

    lax.fori_loop(0, nsb, lane_block, 0)
    coef = jnp.concatenate([coef_ref[ns] for ns in range(nsb)], axis=1)
    acc_ref[...] += jnp.dot(vt_ref[0], coef, preferred_element_type=F32)

    @pl.when(ei == pl.num_programs(1) - 1)
    def _():
        x2 = x1_ref[...] + acc_ref[...].T
        y_ref[...] = (x2 * _rms_scale(x2)) * ng_ref[...]


def _peer(h2, x1, u_bf, vt_bf, cut, e0, s1, e1, norm_g, tn):
    n, d = h2.shape
    ne = u_bf.shape[0]
    te = vt_bf.shape[2]
    nk = PEER_N_KEYS
    tok = pl.BlockSpec((tn, d), lambda i, e: (i, 0))
    tiled = pl.BlockSpec((PEER_HEADS, tn // LANES, nk, LANES), lambda i, e: (0, i, 0, 0))
    return pl.pallas_call(
        functools.partial(_peer_kernel, te=te),
        grid=(n // tn, ne // te),
        in_specs=[tok, tok,
                  pl.BlockSpec((te, d), lambda i, e: (e, 0)),
                  pl.BlockSpec((1, d, te), lambda i, e: (e, 0, 0)),
                  tiled, tiled, tiled, tiled,
                  pl.BlockSpec((1, d), lambda i, e: (0, 0))],
        out_specs=tok,
        out_shape=jax.ShapeDtypeStruct((n, d), F32),
        scratch_shapes=[pltpu.VMEM((d, tn), F32), pltpu.VMEM((tn // LANES, te, LANES), F32),
                        pltpu.VMEM((tn // LANES, te, LANES), BF16)],
        compiler_params=_params(("parallel", "arbitrary")),
        name="peer_mix",
    )(h2, x1, u_bf, vt_bf, cut, e0, s1, e1, norm_g.reshape(1, d))


def _tile(n, pref):
    return math.gcd(n, pref)


def _layer(x, mk, mv, bsz, t, weights, conv_state=None, gdn_conv_state=None, gdn_s0=None):
    (norm_mix_g, w_main, w_ab, conv_w, gdn_conv_w, a_log, dt_bias, gdn_norm_g, wc, wg, wm, wo, norm_ffn_g,
     wq, keys, u_bf, vt_bf, final_norm_g) = weights
    n = bsz * t
    sample = conv_state is not None
    p, pab = _inproj(x, norm_mix_g, w_main, w_ab, _tile(n, 1024))
    rows = _tile(n, 256) if sample else _tile(t, 512)
    ya, q, k, v, gb, u_tail = _prep(p, pab, conv_w, gdn_conv_w, a_log, dt_bias, rows, max(t // rows, 1),
                                    conv_state, gdn_conv_state)
    chunk = _tile(t, 64)
    if sample:
        yb, s_new = _gdn(q, k, v, p, gb, gdn_norm_g, bsz, t, chunk, 4, gdn_s0)
    else:
        yb, s_new = _gdn_stacked(q, k, v, p, gb, gdn_norm_g, bsz, t, chunk, _tile(bsz, 2))
    ym = _mem_attn(p, mk, mv, bsz, t, _tile(t, 512), _tile(bsz, 8) if sample else 1)
    x1, h2 = _merge(x, ya, yb, ym, p, wc, wg, wm, wo, norm_ffn_g, _tile(n, 256))
    tn = _tile(n, 512)
    cut, e0, s1, e1 = _route(h2, wq, keys, tn)
    y = _peer(h2, x1, u_bf, vt_bf, cut, e0, s1, e1, final_norm_g, tn)
    return y, u_tail, p, s_new


def kernel(x_prompt, x_sample, mem_prompt, cache_mem_k, cache_mem_v, state_conv, state_gdn_conv, state_gdn, norm_mix_g, w_in, conv_w, gdn_conv_w, gdn_a_log, gdn_dt_bias, gdn_norm_g, mem_norm_g, w_mem_kv, w_br_conv, w_br_gdn, w_br_mem, w_out, norm_ffn_g, peer_w_q, peer_sub_keys, peer_u, peer_v, final_norm_g):
    depth = w_in.shape[0]
    assert depth == 1, "single-layer step"
    d = D_MODEL
    bp, tp, _ = x_prompt.shape
    bs, ts, _ = x_sample.shape
    assert ts == SAMPLE_PAD_T // 2
    m_tok = mem_prompt.shape[1]
    mq = MEM_HEADS * MEM_HEAD_DIM
    l = 0

    w = w_in[l]
    ga0 = 7 * d
    w_main = jnp.concatenate([w[:, :ga0], w[:, ga0 + 2 * GDN_HEADS:]], axis=1).astype(BF16)
    w_ab = jnp.pad(w[:, ga0:ga0 + 2 * GDN_HEADS], ((0, 0), (0, SMALL_COLS - 2 * GDN_HEADS))).astype(BF16)
    weights = (norm_mix_g[l], w_main, w_ab, conv_w[l], gdn_conv_w[l], gdn_a_log[l], gdn_dt_bias[l], gdn_norm_g[l],
               w_br_conv[l].astype(BF16), w_br_gdn[l].astype(BF16), w_br_mem[l].astype(BF16), w_out[l].astype(BF16),
               norm_ffn_g[l], peer_w_q[l].astype(BF16), peer_sub_keys[l].astype(BF16),
               peer_u[l].astype(BF16),
               peer_v[l].astype(BF16).reshape(-1, PEER_TILE, d).transpose(0, 2, 1), final_norm_g)

    kv_shape = (bp, m_tok, mq)
    mk_p, mv_p = _mem_kv(mem_prompt.reshape(bp * m_tok, d), mem_norm_g[l], w_mem_kv[l].astype(BF16),
                         _tile(bp * m_tok, 256))
    y_p, u_tail_p, p_p, s_p = _layer(x_prompt.reshape(bp * tp, d), mk_p.reshape(kv_shape), mv_p.reshape(kv_shape),
                                     bp, tp, weights)
    rows_p = _tile(tp, 512)
    cb_p = u_tail_p.reshape(bp, tp // rows_p, SUBLANES, d)[:, -1, SUBLANES - (CONV_K - 1):, :]
    gb_p = p_p.reshape(bp, tp, -1)[:, tp - (GDN_CONV_K - 1):, COL_GQ * d:(COL_GV + 1) * d].astype(F32)

    pad_t = SAMPLE_PAD_T
    lead = pad_t - ts
    x_s = jnp.pad(x_sample, ((0, 0), (lead, 0), (0, 0))).reshape(bs * pad_t, d)
    cst = jnp.pad(state_conv[l], ((0, 0), (lead - (CONV_K - 1), ts), (0, 0))).reshape(bs * pad_t, d)
    gst = jnp.pad(state_gdn_conv[l], ((0, 0), (lead - (GDN_CONV_K - 1), ts), (0, 0))).reshape(bs * pad_t, 3 * d)
    y_s, u_s, p_s, s_s = _layer(x_s, cache_mem_k.reshape(bs, -1, mq).astype(BF16),
                                cache_mem_v.reshape(bs, -1, mq).astype(BF16),
                                bs, pad_t, weights, cst, gst, state_gdn.reshape(bs, GDN_HEADS, GDN_DK, GDN_DV))
    y_s = y_s.reshape(bs, pad_t, d)[:, lead:, :]
    cb_s = u_s.reshape(bs, pad_t, d)[:, pad_t - (CONV_K - 1):, :]
    gb_s = p_s.reshape(bs, pad_t, -1)[:, pad_t - (GDN_CONV_K - 1):, COL_GQ * d:(COL_GV + 1) * d].astype(F32)

    mshape = (depth, bp, m_tok, MEM_HEADS, MEM_HEAD_DIM)
    return (y_p.reshape(bp, tp, d), y_s,
            mk_p.reshape(mshape), mv_p.reshape(mshape),
            cb_p[None], gb_p[None], s_p[None],
            cb_s[None], gb_s[None], s_s[None])
```

```python
import functools
import math

import jax
import jax.numpy as jnp
from jax import lax
from jax.experimental import pallas as pl
from jax.experimental.pallas import tpu as pltpu

F32 = jnp.float32
BF16 = jnp.bfloat16

EPS = 1e-6
D_MODEL = 1024
CONV_K = 3
GDN_HEADS = 8
GDN_DK = 128
GDN_DV = 128
GDN_CONV_K = 4
MEM_HEADS = 4
MEM_HEAD_DIM = 256
PEER_HEADS = 8
PEER_N_KEYS = 128
PEER_D_HALF = 128
PEER_TOPK = 16
N_BRANCHES = 3

COL_AU, COL_AB, COL_AC, COL_GQ, COL_GK, COL_GV, COL_GZ, COL_MQ, COL_BG = 0, 1, 2, 3, 4, 5, 6, 7, 8
N_MAIN_BLOCKS = 11
SMALL_COLS = 128
SUBLANES = 8
LANES = 128
SAMPLE_PAD_T = 8
VMEM_LIMIT = 56 * 1024 * 1024


def _params(sem):
    return pltpu.CompilerParams(dimension_semantics=sem, vmem_limit_bytes=VMEM_LIMIT)


def _dot(a, b):
    return jnp.dot(a.astype(BF16), b.astype(BF16), preferred_element_type=F32)


def _dot_nt(a, b):
    return lax.dot_general(a.astype(BF16), b.astype(BF16), (((1,), (1,)), ((), ())),
                           preferred_element_type=F32)


def _rms_scale(x):
    return lax.rsqrt(jnp.mean(x * x, axis=-1, keepdims=True) + EPS)


def _inproj_kernel(x_ref, g_ref, w_ref, wab_ref, p_ref, pab_ref, h_ref):
    @pl.when(pl.program_id(1) == 0)
    def _():
        x = x_ref[...]
        h = ((x * _rms_scale(x)) * g_ref[...]).astype(BF16)
        h_ref[...] = h
        pab_ref[...] = jnp.dot(h, wab_ref[...], preferred_element_type=F32)

    p_ref[...] = jnp.dot(h_ref[...], w_ref[...], preferred_element_type=F32).astype(p_ref.dtype)


def _inproj(x, g, w_main, w_ab, tm):
    n, d = x.shape
    nblk = w_main.shape[1] // d
    return pl.pallas_call(
        _inproj_kernel,
        grid=(n // tm, nblk),
        in_specs=[
            pl.BlockSpec((tm, d), lambda i, j: (i, 0)),
            pl.BlockSpec((1, d), lambda i, j: (0, 0)),
            pl.BlockSpec((d, d), lambda i, j: (0, j)),
            pl.BlockSpec((d, SMALL_COLS), lambda i, j: (0, 0)),
        ],
        out_specs=[
            pl.BlockSpec((tm, d), lambda i, j: (i, j)),
            pl.BlockSpec((tm, SMALL_COLS), lambda i, j: (i, 0)),
        ],
        out_shape=[
            jax.ShapeDtypeStruct((n, nblk * d), BF16),
            jax.ShapeDtypeStruct((n, SMALL_COLS), F32),
        ],
        scratch_shapes=[pltpu.VMEM((tm, d), BF16)],
        compiler_params=_params(("parallel", "arbitrary")),
        name="inproj",
    )(x, g.reshape(1, d), w_main, w_ab)


def _mem_kv_kernel(x_ref, g_ref, w_ref, k_ref, v_ref):
    x = x_ref[...]
    h = ((x * _rms_scale(x)) * g_ref[...]).astype(BF16)
    kv = jnp.dot(h, w_ref[...], preferred_element_type=F32)
    half = kv.shape[1] // 2
    k_ref[...] = kv[:, :half]
    v_ref[...] = kv[:, half:]


def _mem_kv(x, g, w, tm):
    n, d = x.shape
    c = w.shape[1]
    out = jax.ShapeDtypeStruct((n, c // 2), F32)
    ospec = pl.BlockSpec((tm, c // 2), lambda i: (i, 0))
    return pl.pallas_call(
        _mem_kv_kernel,
        grid=(n // tm,),
        in_specs=[
            pl.BlockSpec((tm, d), lambda i: (i, 0)),
            pl.BlockSpec((1, d), lambda i: (0, 0)),
            pl.BlockSpec((d, c), lambda i: (0, 0)),
        ],
        out_specs=[ospec, ospec],
        out_shape=[out, out],
        compiler_params=_params(("parallel",)),
        name="mem_kv",
    )(x, g.reshape(1, d), w)


def _causal_conv(prev, cur, w_ref, k):
    full = jnp.concatenate([prev, cur], axis=0)
    y = cur * w_ref[k - 1:k, :]
    for s in range(1, k):
        y = y + pltpu.roll(full, s, axis=0)[prev.shape[0]:, :] * w_ref[k - 1 - s:k - s, :]
    return y


def _f32(ref):
    return ref[...].astype(F32)


def _prep_kernel(*refs, rows, tiles_per_seq, sample):
    if sample:
        (au, ab, ac, gq, gk, gv, pab, au_p, ac_p, gq_p, gk_p, gv_p, cw, gw, alog, dtb, cst, gst,
         ya_o, q_o, k_o, v_o, gb_o, ut_o) = refs
    else:
        (au, ab, ac, gq, gk, gv, pab, au_p, ac_p, gq_p, gk_p, gv_p, cw, gw, alog, dtb,
         ya_o, q_o, k_o, v_o, gb_o, ut_o) = refs
    d = D_MODEL
    first = (pl.program_id(0) % tiles_per_seq) == 0
    keep = jnp.where(first, 0.0, 1.0).astype(F32)

    u = _f32(ac) * _f32(au)
    u_prev = (_f32(ac_p) * _f32(au_p)) * keep
    if sample:
        real = (lax.broadcasted_iota(jnp.int32, (rows, 1), 0) % SAMPLE_PAD_T) >= (SAMPLE_PAD_T // 2)
        u = jnp.where(real, u, cst[...])
    ya_o[...] = (_f32(ab) * _causal_conv(u_prev, u, cw, CONV_K)).astype(BF16)
    ut_o[...] = u[rows - ut_o.shape[0]:, :]

    outs = (q_o, k_o, v_o)
    for idx, (cur_ref, prev_ref) in enumerate(((gq, gq_p), (gk, gk_p), (gv, gv_p))):
        cur = _f32(cur_ref)
        if sample:
            cur = jnp.where(real, cur, gst[:, idx * d:(idx + 1) * d])
        prev = _f32(prev_ref) * keep
        wslab = gw.at[:, idx * d:(idx + 1) * d]
        y = _causal_conv(prev, cur, wslab, GDN_CONV_K)
        y = y * jax.nn.sigmoid(y)
        if idx < 2:
            scale = GDN_DK ** -0.5 if idx == 0 else 1.0
            for h in range(GDN_HEADS):
                yh = y[:, h * GDN_DK:(h + 1) * GDN_DK]
                r = lax.rsqrt(jnp.sum(yh * yh, axis=-1, keepdims=True) + EPS)
                outs[idx][:, h * GDN_DK:(h + 1) * GDN_DK] = yh * (r * scale)
        else:
            outs[idx][...] = y

    pa = pab[...]
    lane = lax.broadcasted_iota(jnp.int32, pa.shape, 1)
    z = pa + dtb[...]
    softplus = jnp.maximum(z, 0.0) + jnp.log1p(jnp.exp(-jnp.abs(z)))
    g_log = -jnp.exp(alog[...]) * softplus
    gb = jnp.where(lane < GDN_HEADS, g_log, jnp.where(lane < 2 * GDN_HEADS, jax.nn.sigmoid(pa), 0.0))
    if sample:
        gb = jnp.where(real, gb, 0.0)
    gb_o[...] = gb


def _prep(p, pab, conv_w, gdn_conv_w, a_log, dt_bias, rows, tiles_per_seq, conv_state=None, gdn_state=None):
    n = p.shape[0]
    d = D_MODEL
    sample = conv_state is not None
    nt = n // rows
    prev_rows = 2 * SUBLANES
    rpb = rows // prev_rows

    def cur(c):
        return pl.BlockSpec((rows, d), lambda i, c=c: (i, c))

    def prev(c):
        return pl.BlockSpec((prev_rows, d), lambda i, c=c: (jnp.maximum(i * rpb - 1, 0), c))

    alog128 = jnp.zeros((1, SMALL_COLS), F32).at[0, :GDN_HEADS].set(a_log)
    dtb128 = jnp.zeros((1, SMALL_COLS), F32).at[0, :GDN_HEADS].set(dt_bias)
    in_specs = [cur(COL_AU), cur(COL_AB), cur(COL_AC), cur(COL_GQ), cur(COL_GK), cur(COL_GV),
                pl.BlockSpec((rows, SMALL_COLS), lambda i: (i, 0)),
                prev(COL_AU), prev(COL_AC), prev(COL_GQ), prev(COL_GK), prev(COL_GV),
                pl.BlockSpec((CONV_K, d), lambda i: (0, 0)),
                pl.BlockSpec((GDN_CONV_K, 3 * d), lambda i: (0, 0)),
                pl.BlockSpec((1, SMALL_COLS), lambda i: (0, 0)),
                pl.BlockSpec((1, SMALL_COLS), lambda i: (0, 0))]
    args = [p, p, p, p, p, p, pab, p, p, p, p, p, conv_w, gdn_conv_w, alog128, dtb128]
    if sample:
        in_specs += [pl.BlockSpec((rows, d), lambda i: (i, 0)), pl.BlockSpec((rows, 3 * d), lambda i: (i, 0))]
        args += [conv_state, gdn_state]
    tail = rows if sample else SUBLANES
    out_specs = [pl.BlockSpec((rows, d), lambda i: (i, 0))] * 4 + [
        pl.BlockSpec((rows, SMALL_COLS), lambda i: (i, 0)),
        pl.BlockSpec((tail, d), lambda i: (i, 0))]
    out_shape = [jax.ShapeDtypeStruct((n, d), BF16)] + [jax.ShapeDtypeStruct((n, d), F32)] * 3 + [
        jax.ShapeDtypeStruct((n, SMALL_COLS), F32),
        jax.ShapeDtypeStruct((nt * tail, d), F32)]
    return pl.pallas_call(
        functools.partial(_prep_kernel, rows=rows, tiles_per_seq=tiles_per_seq, sample=sample),
        grid=(nt,),
        in_specs=in_specs,
        out_specs=out_specs,
        out_shape=out_shape,
        compiler_params=_params(("parallel",)),
        name="prep",
    )(*args)


def _chunk_cumsum(g, c):
    row = lax.broadcasted_iota(jnp.int32, g.shape, 0)
    sh = 1
    while sh < c:
        g = g + jnp.where(row >= sh, pltpu.roll(g, sh, axis=0), 0.0)
        sh *= 2
    return g


def _gdn_stacked_kernel(q_ref, k_ref, v_ref, z_ref, gb_ref, ng_ref, yb_ref, sout_ref, s_scr, *, c, grp, nb):
    ci = pl.program_id(1)

    @pl.when(ci == 0)
    def _():
        s_scr[...] = jnp.zeros_like(s_scr)

    r = grp * c
    ng = ng_ref[...]
    seqs = []
    for s in range(nb):
        gb = gb_ref[s]
        gc = _chunk_cumsum(gb, c)
        g_last = gc[c - 1:c, :]
        seqs.append(dict(gb=gb, gc=gc, gct=gc.T, e_g=jnp.exp(gc), e_kd=jnp.exp(g_last - gc), e_last=jnp.exp(g_last)))

    ri = lax.broadcasted_iota(jnp.int32, (r, r), 0)
    cj = lax.broadcasted_iota(jnp.int32, (r, r), 1)
    same = (ri // c) == (cj // c)
    incl = same & (ri >= cj)
    strict = same & (ri > cj)
    head_block = (lax.broadcasted_iota(jnp.int32, (r, grp * GDN_DK), 0) // c
                  == lax.broadcasted_iota(jnp.int32, (r, grp * GDN_DK), 1) // GDN_DK)

    def block_layout(x):
        return jnp.where(head_block, jnp.concatenate([x] * grp, axis=1), 0.0)

    units = []
    for s in range(nb):
        for g0 in range(0, GDN_HEADS, grp):
            heads = list(range(g0, g0 + grp))
            sq = seqs[s]

            def stack(ref, s=s, heads=heads):
                return jnp.concatenate([ref[s, :, h * GDN_DK:(h + 1) * GDN_DK] for h in heads], axis=0)

            def col(a, off=0, heads=heads):
                return jnp.concatenate([a[:, off + h:off + h + 1] for h in heads], axis=0)

            qs, ks, vs = stack(q_ref), stack(k_ref), stack(v_ref)
            beta = col(sq["gb"], GDN_HEADS)
            kb = ks * beta
            units.append(dict(s=s, heads=heads, qs=qs, ks=ks, kb=kb, vb=vs * beta,
                              gcol=col(sq["gc"]), egcol=col(sq["e_g"]), ekdcol=col(sq["e_kd"]),
                              grow=jnp.concatenate([sq["gct"][h:h + 1, :] for h in heads], axis=1),
                              rows=slice((s * GDN_HEADS + g0) * GDN_DK, (s * GDN_HEADS + g0 + grp) * GDN_DK)))

    for u in units:
        u["aq"] = _dot_nt(jnp.concatenate([u["kb"], u["qs"]], axis=0), u["ks"])
    for u in units:
        decay = jnp.exp(jnp.where(incl, u["gcol"] - u["grow"], -jnp.inf))
        u["pw"] = -jnp.where(strict, u["aq"][:r] * decay, 0.0)
        u["attn"] = jnp.where(incl, u["aq"][r:] * decay, 0.0)
        u["x"] = jnp.concatenate([u["vb"], u["kb"] * u["egcol"]], axis=1)
    span = 1
    while span < c:
        span *= 2
        for u in units:
            if span < c:
                both = _dot(u["pw"], jnp.concatenate([u["pw"], u["x"]], axis=1))
                u["pw"] = both[:, :r]
                u["x"] = u["x"] + both[:, r:]
            else:
                u["x"] = u["x"] + _dot(u["pw"], u["x"])
    for u in units:
        u_val, w_k = u["x"][:, :GDN_DV], u["x"][:, GDN_DV:]
        u["s_old"] = s_scr[u["rows"], :]
        ws = _dot(jnp.concatenate([block_layout(w_k), block_layout(u["qs"] * u["egcol"])], axis=0), u["s_old"])
        u["v_new"] = u_val - ws[:r]
        u["o"] = ws[r:]
    for u in units:
        e_last = seqs[u["s"]]["e_last"]
        o = u["o"] + _dot(u["attn"], u["v_new"])
        e_rows = jnp.concatenate([jnp.broadcast_to(e_last[:, h:h + 1], (GDN_DK, 1)) for h in u["heads"]], axis=0)
        kd_t = block_layout(u["ks"] * u["ekdcol"]).T
        s_scr[u["rows"], :] = u["s_old"] * e_rows + _dot(kd_t, u["v_new"])
        for i, h in enumerate(u["heads"]):
            sl = slice(h * GDN_DK, (h + 1) * GDN_DK)
            oh = o[i * c:(i + 1) * c, :]
            zh = z_ref[u["s"], :, sl].astype(F32)
            yb_ref[u["s"], :, sl] = (((oh * _rms_scale(oh)) * ng) * (zh * jax.nn.sigmoid(zh))).astype(BF16)

    @pl.when(ci == pl.num_programs(1) - 1)
    def _():
        sout_ref[...] = s_scr[...]


def _gdn_stacked(q, k, v, p, gb, norm_g, bsz, t, c, nb):
    d = D_MODEL
    nc = t // c
    grp = min(GDN_HEADS, 256 // c)
    s_rows = nb * GDN_HEADS * GDN_DK

    def tok(col=0):
        return pl.BlockSpec((nb, c, d), lambda b, i, col=col: (b, i, col))

    yb, s_new = pl.pallas_call(
        functools.partial(_gdn_stacked_kernel, c=c, grp=grp, nb=nb),
        grid=(bsz // nb, nc),
        in_specs=[tok(), tok(), tok(), tok(COL_GZ),
                  pl.BlockSpec((nb, c, SMALL_COLS), lambda b, i: (b, i, 0)),
                  pl.BlockSpec((1, GDN_DV), lambda b, i: (0, 0))],
        out_specs=[tok(), pl.BlockSpec((s_rows, GDN_DV), lambda b, i: (b, 0))],
        out_shape=[jax.ShapeDtypeStruct((bsz, t, d), BF16),
                   jax.ShapeDtypeStruct((bsz * GDN_HEADS * GDN_DK, GDN_DV), F32)],
        scratch_shapes=[pltpu.VMEM((s_rows, GDN_DV), F32)],
        compiler_params=_params(("parallel", "arbitrary")),
        name="gdn_scan_prompt",
    )(*[a.reshape(bsz, t, -1) for a in (q, k, v, p, gb)], norm_g.reshape(1, GDN_DV))
    return yb.reshape(bsz * t, d), s_new.reshape(bsz, GDN_HEADS, GDN_DK, GDN_DV)


def _gdn_kernel(*refs, c, nb, has_s0):
    if has_s0:
        q_ref, k_ref, v_ref, z_ref, gb_ref, ng_ref, s0_ref, yb_ref, sout_ref, s_scr = refs
    else:
        q_ref, k_ref, v_ref, z_ref, gb_ref, ng_ref, yb_ref, sout_ref, s_scr = refs
    ci = pl.program_id(1)

    @pl.when(ci == 0)
    def _():
        if has_s0:
            s_scr[...] = s0_ref[...]
        else:
            s_scr[...] = jnp.zeros_like(s_scr)

    ri = lax.broadcasted_iota(jnp.int32, (c, c), 0)
    cj = lax.broadcasted_iota(jnp.int32, (c, c), 1)
    incl = ri >= cj
    strict = ri > cj
    ng = ng_ref[...]
    row = lax.broadcasted_iota(jnp.int32, (c, SMALL_COLS), 0)
    chains = [(s, h) for s in range(nb) for h in range(GDN_HEADS)]

    seq = []
    for s in range(nb):
        gb = gb_ref[s]
        gc = gb
        sh = 1
        while sh < c:
            gc = gc + jnp.where(row >= sh, pltpu.roll(gc, sh, axis=0), 0.0)
            sh *= 2
        g_last = gc[c - 1:c, :]
        seq.append(dict(gb=gb, gc=gc, gct=gc.T, e_g=jnp.exp(gc), e_kd=jnp.exp(g_last - gc), e_last=jnp.exp(g_last)))

    st = []
    for s, h in chains:
        sl = slice(h * GDN_DK, (h + 1) * GDN_DK)
        sq = seq[s]
        qh, kh, vh = q_ref[s, :, sl], k_ref[s, :, sl], v_ref[s, :, sl]
        beta = sq["gb"][:, GDN_HEADS + h:GDN_HEADS + h + 1]
        kb = kh * beta
        aq = _dot_nt(jnp.concatenate([kb, qh], axis=0), kh)
        st.append(dict(sl=sl, qh=qh, kh=kh, kb=kb, vb=vh * beta, aq=aq))

    for (s, h), e in zip(chains, st):
        sq = seq[s]
        decay = jnp.exp(jnp.where(incl, sq["gc"][:, h:h + 1] - sq["gct"][h:h + 1, :], -jnp.inf))
        e["neg"] = -jnp.where(strict, e["aq"][:c] * decay, 0.0)
        e["attn"] = jnp.where(incl, e["aq"][c:] * decay, 0.0)
        e["x"] = jnp.concatenate([e["vb"], e["kb"] * sq["e_g"][:, h:h + 1]], axis=1)
        e["pw"] = e["neg"]

    span = 1
    while span < c:
        span *= 2
        for e in st:
            if span < c:
                both = _dot(e["pw"], jnp.concatenate([e["pw"], e["x"]], axis=1))
                e["pw"] = both[:, :c]
                e["x"] = e["x"] + both[:, c:]
            else:
                e["x"] = e["x"] + _dot(e["pw"], e["x"])

    for (s, h), e in zip(chains, st):
        sq = seq[s]
        u_val, w_k = e["x"][:, :GDN_DV], e["x"][:, GDN_DV:]
        q_dec = e["qh"] * sq["e_g"][:, h:h + 1]
        e["s_old"] = s_scr[s, h]
        ws = _dot(jnp.concatenate([w_k, q_dec], axis=0), e["s_old"])
        e["v_new"] = u_val - ws[:c]
        e["o_inter"] = ws[c:]

    for (s, h), e in zip(chains, st):
        sq = seq[s]
        o = e["o_inter"] + _dot(e["attn"], e["v_new"])
        k_dec = e["kh"] * sq["e_kd"][:, h:h + 1]
        s_scr[s, h] = e["s_old"] * sq["e_last"][:, h:h + 1] + _dot(k_dec.T, e["v_new"])
        zh = z_ref[s, :, e["sl"]].astype(F32)
        yb_ref[s, :, e["sl"]] = (((o * _rms_scale(o)) * ng) * (zh * jax.nn.sigmoid(zh))).astype(BF16)

    @pl.when(ci == pl.num_programs(1) - 1)
    def _():
        sout_ref[...] = s_scr[...]


def _gdn(q, k, v, p, gb, norm_g, bsz, t, c, nb, s0=None):
    d = D_MODEL
    nc = t // c
    has_s0 = s0 is not None

    def tok(col=0):
        return pl.BlockSpec((nb, c, d), lambda b, i, col=col: (b, i, col))

    in_specs = [tok(), tok(), tok(), tok(COL_GZ),
                pl.BlockSpec((nb, c, SMALL_COLS), lambda b, i: (b, i, 0)),
                pl.BlockSpec((1, GDN_DV), lambda b, i: (0, 0))]
    args = [a.reshape(bsz, t, -1) for a in (q, k, v, p, gb)] + [norm_g.reshape(1, GDN_DV)]
    s_spec = pl.BlockSpec((nb, GDN_HEADS, GDN_DK, GDN_DV), lambda b, i: (b, 0, 0, 0))
    if has_s0:
        in_specs.append(s_spec)
        args.append(s0)
    yb, s_new = pl.pallas_call(
        functools.partial(_gdn_kernel, c=c, nb=nb, has_s0=has_s0),
        grid=(bsz // nb, nc),
        in_specs=in_specs,
        out_specs=[tok(), s_spec],
        out_shape=[jax.ShapeDtypeStruct((bsz, t, d), BF16),
                   jax.ShapeDtypeStruct((bsz, GDN_HEADS, GDN_DK, GDN_DV), F32)],
        scratch_shapes=[pltpu.VMEM((nb, GDN_HEADS, GDN_DK, GDN_DV), F32)],
        compiler_params=_params(("parallel", "arbitrary")),
        name="gdn_scan",
    )(*args)
    return yb.reshape(bsz * t, d), s_new


def _mem_attn_kernel(q_ref, k_ref, v_ref, o_ref, *, nb, tq):
    scale = MEM_HEAD_DIM ** -0.5
    for s in range(nb):
        rows = slice(s * tq, (s + 1) * tq)
        for h in range(MEM_HEADS):
            sl = slice(h * MEM_HEAD_DIM, (h + 1) * MEM_HEAD_DIM)
            sc = _dot_nt(q_ref[rows, sl], k_ref[s, :, sl]) * scale
            e = jnp.exp(sc - jnp.max(sc, axis=-1, keepdims=True))
            prob = e / jnp.sum(e, axis=-1, keepdims=True)
            o_ref[rows, sl] = _dot(prob, v_ref[s, :, sl]).astype(BF16)


def _mem_attn(p, mk, mv, bsz, t, tq, nb):
    d = D_MODEL
    m = mk.shape[1]
    nq = t // tq
    assert nb == 1 or nq == 1
    kv_spec = pl.BlockSpec((nb, m, d), lambda b, i: (b, 0, 0))
    return pl.pallas_call(
        functools.partial(_mem_attn_kernel, nb=nb, tq=tq),
        grid=(bsz // nb, nq),
        in_specs=[pl.BlockSpec((nb * tq, d), lambda b, i: (b * nq + i, COL_MQ)), kv_spec, kv_spec],
        out_specs=pl.BlockSpec((nb * tq, d), lambda b, i: (b * nq + i, 0)),
        out_shape=jax.ShapeDtypeStruct((bsz * t, d), BF16),
        compiler_params=_params(("parallel", "arbitrary")),
        name="mem_attn",
    )(p, mk, mv)


def _merge_kernel(x_ref, ya_ref, yb_ref, ym_ref, g0_ref, g1_ref, g2_ref, wc_ref, wg_ref, wm_ref, wo_ref,
                  ng_ref, x1_ref, h2_ref):
    d = functools.partial(jnp.dot, preferred_element_type=F32)
    merged = (jax.nn.sigmoid(_f32(g0_ref)) * d(ya_ref[...], wc_ref[...])
              + jax.nn.sigmoid(_f32(g1_ref)) * d(yb_ref[...], wg_ref[...])
              + jax.nn.sigmoid(_f32(g2_ref)) * d(ym_ref[...], wm_ref[...]))
    x1 = x_ref[...] + d(merged.astype(BF16), wo_ref[...])
    x1_ref[...] = x1
    h2_ref[...] = ((x1 * _rms_scale(x1)) * ng_ref[...]).astype(BF16)


def _merge(x, ya, yb, ym, p, wc, wg, wm, wo, norm_g, tm):
    n, d = x.shape
    tok = pl.BlockSpec((tm, d), lambda i: (i, 0))
    wsp = pl.BlockSpec((d, d), lambda i: (0, 0))

    def gate(j):
        return pl.BlockSpec((tm, d), lambda i, j=j: (i, COL_BG + j))

    return pl.pallas_call(
        _merge_kernel,
        grid=(n // tm,),
        in_specs=[tok, tok, tok, tok, gate(0), gate(1), gate(2), wsp, wsp, wsp, wsp,
                  pl.BlockSpec((1, d), lambda i: (0, 0))],
        out_specs=[tok, tok],
        out_shape=[jax.ShapeDtypeStruct((n, d), F32), jax.ShapeDtypeStruct((n, d), BF16)],
        compiler_params=_params(("parallel",)),
        name="merge",
    )(x, ya, yb, ym, p, p, p, wc, wg, wm, wo, norm_g.reshape(1, d))


_CAND_CELLS = tuple((a, b) for a in range(PEER_TOPK) for b in range(PEER_TOPK) if (a + 1) * (b + 1) <= PEER_TOPK)


def _top_values(x, count):
    vals = []
    cur = x
    for r in range(count):
        m = jnp.max(cur, axis=0, keepdims=True)
        vals.append(m)
        if r + 1 < count:
            cur = jnp.where(cur == m, -jnp.inf, cur)
    return vals


def _top_ranked(x, count):
    n_rows = x.shape[0]
    iota = lax.broadcasted_iota(jnp.int32, x.shape, 0).astype(F32)
    vals = []
    cur = x
    rank = jnp.full(x.shape, float(count), F32)
    for r in range(count):
        m = jnp.max(cur, axis=0, keepdims=True)
        first = jnp.min(jnp.where(cur == m, iota, float(n_rows)), axis=0, keepdims=True)
        one = iota == first
        rank = jnp.where(one, float(r), rank)
        cur = jnp.where(one, -jnp.inf, cur)
        vals.append(m)
    return vals, rank


def _count_rows(mask):
    return jnp.sum(jnp.where(mask, 1.0, 0.0), axis=0, keepdims=True)


def _route_kernel(h_ref, wq_ref, keys_ref, cut_ref, e0_ref, s1_ref, e1_ref, q_scr):
    tn = h_ref.shape[0]
    nsb = tn // LANES
    q = jnp.dot(h_ref[...], wq_ref[...], preferred_element_type=F32).astype(BF16)

    def store(h, cut, e0, s1, e1):
        for ns in range(nsb):
            cols = slice(ns * LANES, (ns + 1) * LANES)
            cut_ref[h, ns] = cut[:, cols]
            e0_ref[h, ns] = e0[:, cols]
            s1_ref[h, ns] = s1[:, cols]
            e1_ref[h, ns] = e1[:, cols]

    tied = jnp.zeros((1, tn), F32)
    for h in range(PEER_HEADS):
        sc, top = [], []
        for part in range(2):
            col = (2 * h + part) * PEER_D_HALF
            q_scr[2 * h + part] = q[:, col:col + PEER_D_HALF]
            s_t = lax.dot_general(keys_ref[part, h], q[:, col:col + PEER_D_HALF],
                                  (((1,), (1,)), ((), ())), preferred_element_type=F32)
            sc.append(s_t)
            top.append(_top_values(s_t, PEER_TOPK))
            tied = jnp.maximum(tied, jnp.abs(_count_rows(s_t >= top[part][-1]) - PEER_TOPK))
        cand = jnp.concatenate([top[0][a] + top[1][b] for a, b in _CAND_CELLS], axis=0)
        best = _top_values(cand, PEER_TOPK)
        c_max, thr = best[0], best[-1]
        z = jnp.exp(best[0] - c_max)
        for r in range(1, PEER_TOPK):
            z = z + jnp.exp(best[r] - c_max)
        top1 = jnp.concatenate(top[1], axis=0)
        n_sel = sum(_count_rows(top[0][a] + top1 >= thr) for a in range(PEER_TOPK))
        tied = jnp.maximum(tied, jnp.abs(n_sel - PEER_TOPK))
        cut = jnp.full(sc[0].shape, jnp.inf, F32)
        for b in range(PEER_TOPK):
            cut = jnp.where(sc[0] + top[1][b] >= thr, top[1][b], cut)
        store(h, cut, jnp.exp(sc[0] - top[0][0]) / z, sc[1], jnp.exp(sc[1] - top[1][0]))

    @pl.when(jnp.max(tied) > 0.0)
    def _():
        def exact_head(h, carry):
            sc, top, rank = [], [], []
            for part in range(2):
                s_t = lax.dot_general(keys_ref[part, h], q_scr[2 * h + part],
                                      (((1,), (1,)), ((), ())), preferred_element_type=F32)
                vals, rk = _top_ranked(s_t, PEER_TOPK)
                sc.append(s_t)
                top.append(vals)
                rank.append(rk)
            top1 = jnp.concatenate(top[1], axis=0)
            cand = jnp.concatenate([top[0][a] + top1 for a in range(PEER_TOPK)], axis=0)
            best, pos = _top_ranked(cand, PEER_TOPK)
            z = jnp.exp(best[0] - best[0])
            for r in range(1, PEER_TOPK):
                z = z + jnp.exp(best[r] - best[0])
            cnt = jnp.zeros_like(sc[0])
            for a in range(PEER_TOPK):
                cnt_a = _count_rows(pos[a * PEER_TOPK:(a + 1) * PEER_TOPK] < PEER_TOPK)
                cnt = jnp.where(rank[0] == float(a), cnt_a, cnt)
            cut = jnp.where(cnt > 0.0, 1.0 - cnt, jnp.inf)
            store(h, cut, jnp.exp(sc[0] - top[0][0]) / z, -rank[1], jnp.exp(sc[1] - top[1][0]))
            return carry

        lax.fori_loop(0, PEER_HEADS, exact_head, 0)


def _route(h2, wq, keys, tn):
    n, d = h2.shape
    nk = PEER_N_KEYS
    tiled = pl.BlockSpec((PEER_HEADS, tn // LANES, nk, LANES), lambda i: (0, i, 0, 0))
    tiled_shape = jax.ShapeDtypeStruct((PEER_HEADS, n // LANES, nk, LANES), F32)
    return pl.pallas_call(
        _route_kernel,
        grid=(n // tn,),
        in_specs=[pl.BlockSpec((tn, d), lambda i: (i, 0)),
                  pl.BlockSpec(wq.shape, lambda i: (0, 0)),
                  pl.BlockSpec(keys.shape, lambda i: (0, 0, 0, 0))],
        out_specs=[tiled, tiled, tiled, tiled],
        out_shape=[tiled_shape, tiled_shape, tiled_shape, tiled_shape],
        scratch_shapes=[pltpu.VMEM((2 * PEER_HEADS, tn, PEER_D_HALF), BF16)],
        compiler_params=_params(("parallel",)),
        name="peer_route",
    )(h2, wq, keys)


def _gelu(x):
    return 0.5 * x * (1.0 + lax.erf(x * (2.0 ** -0.5)))


PEER_TILE = 1024
GATE_HALF = 32
GATE_ROWS = 8


def _peer_kernel(h_ref, x1_ref, u_ref, vt_ref, cut_ref, e0_ref, s1_ref, e1_ref, ng_ref,
                 y_ref, acc_ref, act_ref, coef_ref, *, te):
    ei = pl.program_id(1)
    tn = h_ref.shape[0]
    groups = te // PEER_N_KEYS

    @pl.when(ei == 0)
    def _():
        acc_ref[...] = jnp.zeros_like(acc_ref)

    nsb = tn // LANES
    act = _gelu(lax.dot_general(u_ref[...], h_ref[...], (((1,), (1,)), ((), ())),
                                preferred_element_type=F32))
    for ns in range(nsb):
        act_ref[ns] = act[:, ns * LANES:(ns + 1) * LANES]

    def gate_block(ns):
        for jh in range(PEER_N_KEYS // GATE_HALF):
            jrows = slice(jh * GATE_HALF, (jh + 1) * GATE_HALF)
            for g0 in range(0, groups, GATE_ROWS):
                gates = [None] * GATE_ROWS
                for h in range(PEER_HEADS):
                    s1_t = s1_ref[h, ns, jrows, :]
                    e1_t = e1_ref[h, ns, jrows, :]
                    for k in range(GATE_ROWS):
                        i_key = ei * groups + g0 + k
                        sel = s1_t >= cut_ref[h, ns, pl.ds(i_key, 1), :]
                        term = jnp.where(sel, e1_t, 0.0) * e0_ref[h, ns, pl.ds(i_key, 1), :]
                        gates[k] = term if gates[k] is None else gates[k] + term
                for k in range(GATE_ROWS):
                    r0 = (g0 + k) * PEER_N_KEYS + jh * GATE_HALF
                    coef_ref[ns, r0:r0 + GATE_HALF, :] = (gates[k] * act_ref[ns, r0:r0 + GATE_HALF, :]).astype(BF16)

    def lane_block(ns, carry):
        gate_block(ns)
        return carry

    lax.fori_loop(0, nsb, lane_block, 0)
    coef = jnp.concatenate([coef_ref[ns] for ns in range(nsb)], axis=1)
    acc_ref[...] += jnp.dot(vt_ref[0], coef, preferred_element_type=F32)

    @pl.when(ei == pl.num_programs(1) - 1)
    def _():
        x2 = x1_ref[...] + acc_ref[...].T
        y_ref[...] = (x2 * _rms_scale(x2)) * ng_ref[...]


def _peer(h2, x1, u_bf, vt_bf, cut, e0, s1, e1, norm_g, tn):
    n, d = h2.shape
    ne = u_bf.shape[0]
    te = vt_bf.shape[2]
    nk = PEER_N_KEYS
    tok = pl.BlockSpec((tn, d), lambda i, e: (i, 0))
    tiled = pl.BlockSpec((PEER_HEADS, tn // LANES, nk, LANES), lambda i, e: (0, i, 0, 0))
    return pl.pallas_call(
        functools.partial(_peer_kernel, te=te),
        grid=(n // tn, ne // te),
        in_specs=[tok, tok,
                  pl.BlockSpec((te, d), lambda i, e: (e, 0)),
                  pl.BlockSpec((1, d, te), lambda i, e: (e, 0, 0)),
                  tiled, tiled, tiled, tiled,
                  pl.BlockSpec((1, d), lambda i, e: (0, 0))],
        out_specs=tok,
        out_shape=jax.ShapeDtypeStruct((n, d), F32),
        scratch_shapes=[pltpu.VMEM((d, tn), F32), pltpu.VMEM((tn // LANES, te, LANES), F32),
                        pltpu.VMEM((tn // LANES, te, LANES), BF16)],
        compiler_params=_params(("parallel", "arbitrary")),
        name="peer_mix",
    )(h2, x1, u_bf, vt_bf, cut, e0, s1, e1, norm_g.reshape(1, d))


def _tile(n, pref):
    return math.gcd(n, pref)


def _layer(x, mk, mv, bsz, t, weights, conv_state=None, gdn_conv_state=None, gdn_s0=None):
    (norm_mix_g, w_main, w_ab, conv_w, gdn_conv_w, a_log, dt_bias, gdn_norm_g, wc, wg, wm, wo, norm_ffn_g,
     wq, keys, u_bf, vt_bf, final_norm_g) = weights
    n = bsz * t
    sample = conv_state is not None
    p, pab = _inproj(x, norm_mix_g, w_main, w_ab, _tile(n, 1024))
    rows = _tile(n, 256) if sample else _tile(t, 512)
    ya, q, k, v, gb, u_tail = _prep(p, pab, conv_w, gdn_conv_w, a_log, dt_bias, rows, max(t // rows, 1),
                                    conv_state, gdn_conv_state)
    chunk = _tile(t, 64)
    if sample:
        yb, s_new = _gdn(q, k, v, p, gb, gdn_norm_g, bsz, t, chunk, 4, gdn_s0)
    else:
        yb, s_new = _gdn_stacked(q, k, v, p, gb, gdn_norm_g, bsz, t, chunk, _tile(bsz, 2))
    ym = _mem_attn(p, mk, mv, bsz, t, _tile(t, 512), _tile(bsz, 8) if sample else 1)
    x1, h2 = _merge(x, ya, yb, ym, p, wc, wg, wm, wo, norm_ffn_g, _tile(n, 256))
    tn = _tile(n, 512)
    cut, e0, s1, e1 = _route(h2, wq, keys, tn)
    y = _peer(h2, x1, u_bf, vt_bf, cut, e0, s1, e1, final_norm_g, tn)
    return y, u_tail, p, s_new


def kernel(x_prompt, x_sample, mem_prompt, cache_mem_k, cache_mem_v, state_conv, state_gdn_conv, state_gdn, norm_mix_g, w_in, conv_w, gdn_conv_w, gdn_a_log, gdn_dt_bias, gdn_norm_g, mem_norm_g, w_mem_kv, w_br_conv, w_br_gdn, w_br_mem, w_out, norm_ffn_g, peer_w_q, peer_sub_keys, peer_u, peer_v, final_norm_g):
    depth = w_in.shape[0]
    assert depth == 1, "single-layer step"
    d = D_MODEL
    bp, tp, _ = x_prompt.shape
    bs, ts, _ = x_sample.shape
    assert ts == SAMPLE_PAD_T // 2
    m_tok = mem_prompt.shape[1]
    mq = MEM_HEADS * MEM_HEAD_DIM
    l = 0

    w = w_in[l]
    ga0 = 7 * d
    w_main = jnp.concatenate([w[:, :ga0], w[:, ga0 + 2 * GDN_HEADS:]], axis=1).astype(BF16)
    w_ab = jnp.pad(w[:, ga0:ga0 + 2 * GDN_HEADS], ((0, 0), (0, SMALL_COLS - 2 * GDN_HEADS))).astype(BF16)
    weights = (norm_mix_g[l], w_main, w_ab, conv_w[l], gdn_conv_w[l], gdn_a_log[l], gdn_dt_bias[l], gdn_norm_g[l],
               w_br_conv[l].astype(BF16), w_br_gdn[l].astype(BF16), w_br_mem[l].astype(BF16), w_out[l].astype(BF16),
               norm_ffn_g[l], peer_w_q[l].astype(BF16), peer_sub_keys[l].astype(BF16),
               peer_u[l].astype(BF16),
               peer_v[l].astype(BF16).reshape(-1, PEER_TILE, d).transpose(0, 2, 1), final_norm_g)

    kv_shape = (bp, m_tok, mq)
    mk_p, mv_p = _mem_kv(mem_prompt.reshape(bp * m_tok, d), mem_norm_g[l], w_mem_kv[l].astype(BF16),
                         _tile(bp * m_tok, 256))
    y_p, u_tail_p, p_p, s_p = _layer(x_prompt.reshape(bp * tp, d), mk_p.reshape(kv_shape), mv_p.reshape(kv_shape),
                                     bp, tp, weights)
    rows_p = _tile(tp, 512)
    cb_p = u_tail_p.reshape(bp, tp // rows_p, SUBLANES, d)[:, -1, SUBLANES - (CONV_K - 1):, :]
    gb_p = p_p.reshape(bp, tp, -1)[:, tp - (GDN_CONV_K - 1):, COL_GQ * d:(COL_GV + 1) * d].astype(F32)

    pad_t = SAMPLE_PAD_T
    lead = pad_t - ts
    x_s = jnp.pad(x_sample, ((0, 0), (lead, 0), (0, 0))).reshape(bs * pad_t, d)
    cst = jnp.pad(state_conv[l], ((0, 0), (lead - (CONV_K - 1), ts), (0, 0))).reshape(bs * pad_t, d)
    gst = jnp.pad(state_gdn_conv[l], ((0, 0), (lead - (GDN_CONV_K - 1), ts), (0, 0))).reshape(bs * pad_t, 3 * d)
    y_s, u_s, p_s, s_s = _layer(x_s, cache_mem_k.astype(BF16).reshape(bs, -1, mq),
                                cache_mem_v.astype(BF16).reshape(bs, -1, mq),
                                bs, pad_t, weights, cst, gst, state_gdn.reshape(bs, GDN_HEADS, GDN_DK, GDN_DV))
    y_s = y_s.reshape(bs, pad_t, d)[:, lead:, :]
    cb_s = u_s.reshape(bs, pad_t, d)[:, pad_t - (CONV_K - 1):, :]
    gb_s = p_s.reshape(bs, pad_t, -1)[:, pad_t - (GDN_CONV_K - 1):, COL_GQ * d:(COL_GV + 1) * d].astype(F32)

    mshape = (depth, bp, m_tok, MEM_HEADS, MEM_HEAD_DIM)
    return (y_p.reshape(bp, tp, d), y_s,
            mk_p.reshape(mshape), mv_p.reshape(mshape),
            cb_p[None], gb_p[None], s_p[None],
            cb_s[None], gb_s[None], s_s[None])
```

```python
import functools
import math

import jax
import jax.numpy as jnp
from jax import lax
from jax.experimental import pallas as pl
from jax.experimental.pallas import tpu as pltpu

F32 = jnp.float32
BF16 = jnp.bfloat16

EPS = 1e-6
D_MODEL = 1024
CONV_K = 3
GDN_HEADS = 8
GDN_DK = 128
GDN_DV = 128
GDN_CONV_K = 4
MEM_HEADS = 4
MEM_HEAD_DIM = 256
PEER_HEADS = 8
PEER_N_KEYS = 128
PEER_D_HALF = 128
PEER_TOPK = 16
N_BRANCHES = 3

COL_AU, COL_AB, COL_AC, COL_GQ, COL_GK, COL_GV, COL_GZ, COL_MQ, COL_BG = 0, 1, 2, 3, 4, 5, 6, 7, 8
N_MAIN_BLOCKS = 11
SMALL_COLS = 128
SUBLANES = 8
LANES = 128
SAMPLE_PAD_T = 8
VMEM_LIMIT = 56 * 1024 * 1024


def _params(sem):
    return pltpu.CompilerParams(dimension_semantics=sem, vmem_limit_bytes=VMEM_LIMIT)


def _dot(a, b):
    return jnp.dot(a.astype(BF16), b.astype(BF16), preferred_element_type=F32)


def _dot_nt(a, b):
    return lax.dot_general(a.astype(BF16), b.astype(BF16), (((1,), (1,)), ((), ())),
                           preferred_element_type=F32)


def _rms_scale(x):
    return lax.rsqrt(jnp.mean(x * x, axis=-1, keepdims=True) + EPS)


def _inproj_kernel(x_ref, g_ref, w_ref, wab_ref, p_ref, pab_ref, h_ref):
    @pl.when(pl.program_id(1) == 0)
    def _():
        x = x_ref[...]
        h = ((x * _rms_scale(x)) * g_ref[...]).astype(BF16)
        h_ref[...] = h
        pab_ref[...] = jnp.dot(h, wab_ref[...], preferred_element_type=F32)

    p_ref[...] = jnp.dot(h_ref[...], w_ref[...], preferred_element_type=F32).astype(p_ref.dtype)


def _inproj(x, g, w_main, w_ab, tm):
    n, d = x.shape
    nblk = w_main.shape[1] // d
    return pl.pallas_call(
        _inproj_kernel,
        grid=(n // tm, nblk),
        in_specs=[
            pl.BlockSpec((tm, d), lambda i, j: (i, 0)),
            pl.BlockSpec((1, d), lambda i, j: (0, 0)),
            pl.BlockSpec((d, d), lambda i, j: (0, j)),
            pl.BlockSpec((d, SMALL_COLS), lambda i, j: (0, 0)),
        ],
        out_specs=[
            pl.BlockSpec((tm, d), lambda i, j: (i, j)),
            pl.BlockSpec((tm, SMALL_COLS), lambda i, j: (i, 0)),
        ],
        out_shape=[
            jax.ShapeDtypeStruct((n, nblk * d), BF16),
            jax.ShapeDtypeStruct((n, SMALL_COLS), F32),
        ],
        scratch_shapes=[pltpu.VMEM((tm, d), BF16)],
        compiler_params=_params(("parallel", "arbitrary")),
        name="inproj",
    )(x, g.reshape(1, d), w_main, w_ab)


def _mem_kv_kernel(x_ref, g_ref, w_ref, k_ref, v_ref):
    x = x_ref[...]
    h = ((x * _rms_scale(x)) * g_ref[...]).astype(BF16)
    kv = jnp.dot(h, w_ref[...], preferred_element_type=F32)
    half = kv.shape[1] // 2
    k_ref[...] = kv[:, :half]
    v_ref[...] = kv[:, half:]


def _mem_kv(x, g, w, tm):
    n, d = x.shape
    c = w.shape[1]
    out = jax.ShapeDtypeStruct((n, c // 2), F32)
    ospec = pl.BlockSpec((tm, c // 2), lambda i: (i, 0))
    return pl.pallas_call(
        _mem_kv_kernel,
        grid=(n // tm,),
        in_specs=[
            pl.BlockSpec((tm, d), lambda i: (i, 0)),
            pl.BlockSpec((1, d), lambda i: (0, 0)),
            pl.BlockSpec((d, c), lambda i: (0, 0)),
        ],
        out_specs=[ospec, ospec],
        out_shape=[out, out],
        compiler_params=_params(("parallel",)),
        name="mem_kv",
    )(x, g.reshape(1, d), w)


def _causal_conv(prev, cur, w_ref, k):
    full = jnp.concatenate([prev, cur], axis=0)
    y = cur * w_ref[k - 1:k, :]
    for s in range(1, k):
        y = y + pltpu.roll(full, s, axis=0)[prev.shape[0]:, :] * w_ref[k - 1 - s:k - s, :]
    return y


def _f32(ref):
    return ref[...].astype(F32)


def _prep_kernel(*refs, rows, tiles_per_seq, sample):
    if sample:
        (au, ab, ac, gq, gk, gv, pab, au_p, ac_p, gq_p, gk_p, gv_p, cw, gw, alog, dtb, cst, gst,
         ya_o, q_o, k_o, v_o, gb_o, ut_o) = refs
    else:
        (au, ab, ac, gq, gk, gv, pab, au_p, ac_p, gq_p, gk_p, gv_p, cw, gw, alog, dtb,
         ya_o, q_o, k_o, v_o, gb_o, ut_o) = refs
    d = D_MODEL
    first = (pl.program_id(0) % tiles_per_seq) == 0
    keep = jnp.where(first, 0.0, 1.0).astype(F32)

    u = _f32(ac) * _f32(au)
    u_prev = (_f32(ac_p) * _f32(au_p)) * keep
    if sample:
        real = (lax.broadcasted_iota(jnp.int32, (rows, 1), 0) % SAMPLE_PAD_T) >= (SAMPLE_PAD_T // 2)
        u = jnp.where(real, u, cst[...])
    ya_o[...] = (_f32(ab) * _causal_conv(u_prev, u, cw, CONV_K)).astype(BF16)
    ut_o[...] = u[rows - ut_o.shape[0]:, :]

    outs = (q_o, k_o, v_o)
    for idx, (cur_ref, prev_ref) in enumerate(((gq, gq_p), (gk, gk_p), (gv, gv_p))):
        cur = _f32(cur_ref)
        if sample:
            cur = jnp.where(real, cur, gst[:, idx * d:(idx + 1) * d])
        prev = _f32(prev_ref) * keep
        wslab = gw.at[:, idx * d:(idx + 1) * d]
        y = _causal_conv(prev, cur, wslab, GDN_CONV_K)
        y = y * jax.nn.sigmoid(y)
        if idx < 2:
            scale = GDN_DK ** -0.5 if idx == 0 else 1.0
            for h in range(GDN_HEADS):
                yh = y[:, h * GDN_DK:(h + 1) * GDN_DK]
                r = lax.rsqrt(jnp.sum(yh * yh, axis=-1, keepdims=True) + EPS)
                outs[idx][:, h * GDN_DK:(h + 1) * GDN_DK] = yh * (r * scale)
        else:
            outs[idx][...] = y

    pa = pab[...]
    lane = lax.broadcasted_iota(jnp.int32, pa.shape, 1)
    z = pa + dtb[...]
    softplus = jnp.maximum(z, 0.0) + jnp.log1p(jnp.exp(-jnp.abs(z)))
    g_log = -jnp.exp(alog[...]) * softplus
    gb = jnp.where(lane < GDN_HEADS, g_log, jnp.where(lane < 2 * GDN_HEADS, jax.nn.sigmoid(pa), 0.0))
    if sample:
        gb = jnp.where(real, gb, 0.0)
    gb_o[...] = gb


def _prep(p, pab, conv_w, gdn_conv_w, a_log, dt_bias, rows, tiles_per_seq, conv_state=None, gdn_state=None):
    n = p.shape[0]
    d = D_MODEL
    sample = conv_state is not None
    nt = n // rows
    prev_rows = 2 * SUBLANES
    rpb = rows // prev_rows

    def cur(c):
        return pl.BlockSpec((rows, d), lambda i, c=c: (i, c))

    def prev(c):
        return pl.BlockSpec((prev_rows, d), lambda i, c=c: (jnp.maximum(i * rpb - 1, 0), c))

    alog128 = jnp.zeros((1, SMALL_COLS), F32).at[0, :GDN_HEADS].set(a_log)
    dtb128 = jnp.zeros((1, SMALL_COLS), F32).at[0, :GDN_HEADS].set(dt_bias)
    in_specs = [cur(COL_AU), cur(COL_AB), cur(COL_AC), cur(COL_GQ), cur(COL_GK), cur(COL_GV),
                pl.BlockSpec((rows, SMALL_COLS), lambda i: (i, 0)),
                prev(COL_AU), prev(COL_AC), prev(COL_GQ), prev(COL_GK), prev(COL_GV),
                pl.BlockSpec((CONV_K, d), lambda i: (0, 0)),
                pl.BlockSpec((GDN_CONV_K, 3 * d), lambda i: (0, 0)),
                pl.BlockSpec((1, SMALL_COLS), lambda i: (0, 0)),
                pl.BlockSpec((1, SMALL_COLS), lambda i: (0, 0))]
    args = [p, p, p, p, p, p, pab, p, p, p, p, p, conv_w, gdn_conv_w, alog128, dtb128]
    if sample:
        in_specs += [pl.BlockSpec((rows, d), lambda i: (i, 0)), pl.BlockSpec((rows, 3 * d), lambda i: (i, 0))]
        args += [conv_state, gdn_state]
    tail = rows if sample else SUBLANES
    out_specs = [pl.BlockSpec((rows, d), lambda i: (i, 0))] * 4 + [
        pl.BlockSpec((rows, SMALL_COLS), lambda i: (i, 0)),
        pl.BlockSpec((tail, d), lambda i: (i, 0))]
    out_shape = [jax.ShapeDtypeStruct((n, d), BF16)] + [jax.ShapeDtypeStruct((n, d), F32)] * 3 + [
        jax.ShapeDtypeStruct((n, SMALL_COLS), F32),
        jax.ShapeDtypeStruct((nt * tail, d), F32)]
    return pl.pallas_call(
        functools.partial(_prep_kernel, rows=rows, tiles_per_seq=tiles_per_seq, sample=sample),
        grid=(nt,),
        in_specs=in_specs,
        out_specs=out_specs,
        out_shape=out_shape,
        compiler_params=_params(("parallel",)),
        name="prep",
    )(*args)


def _chunk_cumsum(g, c):
    row = lax.broadcasted_iota(jnp.int32, g.shape, 0)
    sh = 1
    while sh < c:
        g = g + jnp.where(row >= sh, pltpu.roll(g, sh, axis=0), 0.0)
        sh *= 2
    return g


def _gdn_stacked_kernel(q_ref, k_ref, v_ref, z_ref, gb_ref, ng_ref, yb_ref, sout_ref, s_scr, *, c, grp, nb):
    ci = pl.program_id(1)

    @pl.when(ci == 0)
    def _():
        s_scr[...] = jnp.zeros_like(s_scr)

    r = grp * c
    ng = ng_ref[...]
    seqs = []
    for s in range(nb):
        gb = gb_ref[s]
        gc = _chunk_cumsum(gb, c)
        g_last = gc[c - 1:c, :]
        seqs.append(dict(gb=gb, gc=gc, gct=gc.T, e_g=jnp.exp(gc), e_kd=jnp.exp(g_last - gc), e_last=jnp.exp(g_last)))

    ri = lax.broadcasted_iota(jnp.int32, (r, r), 0)
    cj = lax.broadcasted_iota(jnp.int32, (r, r), 1)
    same = (ri // c) == (cj // c)
    incl = same & (ri >= cj)
    strict = same & (ri > cj)
    head_block = (lax.broadcasted_iota(jnp.int32, (r, grp * GDN_DK), 0) // c
                  == lax.broadcasted_iota(jnp.int32, (r, grp * GDN_DK), 1) // GDN_DK)

    def block_layout(x):
        return jnp.where(head_block, jnp.concatenate([x] * grp, axis=1), 0.0)

    units = []
    for s in range(nb):
        for g0 in range(0, GDN_HEADS, grp):
            heads = list(range(g0, g0 + grp))
            sq = seqs[s]

            def stack(ref, s=s, heads=heads):
                return jnp.concatenate([ref[s, :, h * GDN_DK:(h + 1) * GDN_DK] for h in heads], axis=0)

            def col(a, off=0, heads=heads):
                return jnp.concatenate([a[:, off + h:off + h + 1] for h in heads], axis=0)

            qs, ks, vs = stack(q_ref), stack(k_ref), stack(v_ref)
            beta = col(sq["gb"], GDN_HEADS)
            kb = ks * beta
            units.append(dict(s=s, heads=heads, qs=qs, ks=ks, kb=kb, vb=vs * beta,
                              gcol=col(sq["gc"]), egcol=col(sq["e_g"]), ekdcol=col(sq["e_kd"]),
                              grow=jnp.concatenate([sq["gct"][h:h + 1, :] for h in heads], axis=1),
                              rows=slice((s * GDN_HEADS + g0) * GDN_DK, (s * GDN_HEADS + g0 + grp) * GDN_DK)))

    for u in units:
        u["aq"] = _dot_nt(jnp.concatenate([u["kb"], u["qs"]], axis=0), u["ks"])
    for u in units:
        decay = jnp.exp(jnp.where(incl, u["gcol"] - u["grow"], -jnp.inf))
        u["pw"] = -jnp.where(strict, u["aq"][:r] * decay, 0.0)
        u["attn"] = jnp.where(incl, u["aq"][r:] * decay, 0.0)
        u["x"] = jnp.concatenate([u["vb"], u["kb"] * u["egcol"]], axis=1)
    span = 1
    while span < c:
        span *= 2
        for u in units:
            if span < c:
                both = _dot(u["pw"], jnp.concatenate([u["pw"], u["x"]], axis=1))
                u["pw"] = both[:, :r]
                u["x"] = u["x"] + both[:, r:]
            else:
                u["x"] = u["x"] + _dot(u["pw"], u["x"])
    for u in units:
        u_val, w_k = u["x"][:, :GDN_DV], u["x"][:, GDN_DV:]
        u["s_old"] = s_scr[u["rows"], :]
        ws = _dot(jnp.concatenate([block_layout(w_k), block_layout(u["qs"] * u["egcol"])], axis=0), u["s_old"])
        u["v_new"] = u_val - ws[:r]
        u["o"] = ws[r:]
    for u in units:
        e_last = seqs[u["s"]]["e_last"]
        o = u["o"] + _dot(u["attn"], u["v_new"])
        e_rows = jnp.concatenate([jnp.broadcast_to(e_last[:, h:h + 1], (GDN_DK, 1)) for h in u["heads"]], axis=0)
        kd_t = block_layout(u["ks"] * u["ekdcol"]).T
        s_scr[u["rows"], :] = u["s_old"] * e_rows + _dot(kd_t, u["v_new"])
        for i, h in enumerate(u["heads"]):
            sl = slice(h * GDN_DK, (h + 1) * GDN_DK)
            oh = o[i * c:(i + 1) * c, :]
            zh = z_ref[u["s"], :, sl].astype(F32)
            yb_ref[u["s"], :, sl] = (((oh * _rms_scale(oh)) * ng) * (zh * jax.nn.sigmoid(zh))).astype(BF16)

    @pl.when(ci == pl.num_programs(1) - 1)
    def _():
        sout_ref[...] = s_scr[...]


def _gdn_stacked(q, k, v, p, gb, norm_g, bsz, t, c, nb):
    d = D_MODEL
    nc = t // c
    grp = min(GDN_HEADS, 256 // c)
    s_rows = nb * GDN_HEADS * GDN_DK

    def tok(col=0):
        return pl.BlockSpec((nb, c, d), lambda b, i, col=col: (b, i, col))

    yb, s_new = pl.pallas_call(
        functools.partial(_gdn_stacked_kernel, c=c, grp=grp, nb=nb),
        grid=(bsz // nb, nc),
        in_specs=[tok(), tok(), tok(), tok(COL_GZ),
                  pl.BlockSpec((nb, c, SMALL_COLS), lambda b, i: (b, i, 0)),
                  pl.BlockSpec((1, GDN_DV), lambda b, i: (0, 0))],
        out_specs=[tok(), pl.BlockSpec((s_rows, GDN_DV), lambda b, i: (b, 0))],
        out_shape=[jax.ShapeDtypeStruct((bsz, t, d), BF16),
                   jax.ShapeDtypeStruct((bsz * GDN_HEADS * GDN_DK, GDN_DV), F32)],
        scratch_shapes=[pltpu.VMEM((s_rows, GDN_DV), F32)],
        compiler_params=_params(("parallel", "arbitrary")),
        name="gdn_scan_prompt",
    )(*[a.reshape(bsz, t, -1) for a in (q, k, v, p, gb)], norm_g.reshape(1, GDN_DV))
    return yb.reshape(bsz * t, d), s_new.reshape(bsz, GDN_HEADS, GDN_DK, GDN_DV)


def _gdn_kernel(*refs, c, nb, has_s0):
    if has_s0:
        q_ref, k_ref, v_ref, z_ref, gb_ref, ng_ref, s0_ref, yb_ref, sout_ref, s_scr = refs
    else:
        q_ref, k_ref, v_ref, z_ref, gb_ref, ng_ref, yb_ref, sout_ref, s_scr = refs
    ci = pl.program_id(1)

    @pl.when(ci == 0)
    def _():
        if has_s0:
            s_scr[...] = s0_ref[...]
        else:
            s_scr[...] = jnp.zeros_like(s_scr)

    ri = lax.broadcasted_iota(jnp.int32, (c, c), 0)
    cj = lax.broadcasted_iota(jnp.int32, (c, c), 1)
    incl = ri >= cj
    strict = ri > cj
    ng = ng_ref[...]
    row = lax.broadcasted_iota(jnp.int32, (c, SMALL_COLS), 0)
    chains = [(s, h) for s in range(nb) for h in range(GDN_HEADS)]

    seq = []
    for s in range(nb):
        gb = gb_ref[s]
        gc = gb
        sh = 1
        while sh < c:
            gc = gc + jnp.where(row >= sh, pltpu.roll(gc, sh, axis=0), 0.0)
            sh *= 2
        g_last = gc[c - 1:c, :]
        seq.append(dict(gb=gb, gc=gc, gct=gc.T, e_g=jnp.exp(gc), e_kd=jnp.exp(g_last - gc), e_last=jnp.exp(g_last)))

    st = []
    for s, h in chains:
        sl = slice(h * GDN_DK, (h + 1) * GDN_DK)
        sq = seq[s]
        qh, kh, vh = q_ref[s, :, sl], k_ref[s, :, sl], v_ref[s, :, sl]
        beta = sq["gb"][:, GDN_HEADS + h:GDN_HEADS + h + 1]
        kb = kh * beta
        aq = _dot_nt(jnp.concatenate([kb, qh], axis=0), kh)
        st.append(dict(sl=sl, qh=qh, kh=kh, kb=kb, vb=vh * beta, aq=aq))

    for (s, h), e in zip(chains, st):
        sq = seq[s]
        decay = jnp.exp(jnp.where(incl, sq["gc"][:, h:h + 1] - sq["gct"][h:h + 1, :], -jnp.inf))
        e["neg"] = -jnp.where(strict, e["aq"][:c] * decay, 0.0)
        e["attn"] = jnp.where(incl, e["aq"][c:] * decay, 0.0)
        e["x"] = jnp.concatenate([e["vb"], e["kb"] * sq["e_g"][:, h:h + 1]], axis=1)
        e["pw"] = e["neg"]

    span = 1
    while span < c:
        span *= 2
        for e in st:
            if span < c:
                both = _dot(e["pw"], jnp.concatenate([e["pw"], e["x"]], axis=1))
                e["pw"] = both[:, :c]
                e["x"] = e["x"] + both[:, c:]
            else:
                e["x"] = e["x"] + _dot(e["pw"], e["x"])

    for (s, h), e in zip(chains, st):
        sq = seq[s]
        u_val, w_k = e["x"][:, :GDN_DV], e["x"][:, GDN_DV:]
        q_dec = e["qh"] * sq["e_g"][:, h:h + 1]
        e["s_old"] = s_scr[s, h]
        ws = _dot(jnp.concatenate([w_k, q_dec], axis=0), e["s_old"])
        e["v_new"] = u_val - ws[:c]
        e["o_inter"] = ws[c:]

    for (s, h), e in zip(chains, st):
        sq = seq[s]
        o = e["o_inter"] + _dot(e["attn"], e["v_new"])
        k_dec = e["kh"] * sq["e_kd"][:, h:h + 1]
        s_scr[s, h] = e["s_old"] * sq["e_last"][:, h:h + 1] + _dot(k_dec.T, e["v_new"])
        zh = z_ref[s, :, e["sl"]].astype(F32)
        yb_ref[s, :, e["sl"]] = (((o * _rms_scale(o)) * ng) * (zh * jax.nn.sigmoid(zh))).astype(BF16)

    @pl.when(ci == pl.num_programs(1) - 1)
    def _():
        sout_ref[...] = s_scr[...]


def _gdn(q, k, v, p, gb, norm_g, bsz, t, c, nb, s0=None):
    d = D_MODEL
    nc = t // c
    has_s0 = s0 is not None

    def tok(col=0):
        return pl.BlockSpec((nb, c, d), lambda b, i, col=col: (b, i, col))

    in_specs = [tok(), tok(), tok(), tok(COL_GZ),
                pl.BlockSpec((nb, c, SMALL_COLS), lambda b, i: (b, i, 0)),
                pl.BlockSpec((1, GDN_DV), lambda b, i: (0, 0))]
    args = [a.reshape(bsz, t, -1) for a in (q, k, v, p, gb)] + [norm_g.reshape(1, GDN_DV)]
    s_spec = pl.BlockSpec((nb, GDN_HEADS, GDN_DK, GDN_DV), lambda b, i: (b, 0, 0, 0))
    if has_s0:
        in_specs.append(s_spec)
        args.append(s0)
    yb, s_new = pl.pallas_call(
        functools.partial(_gdn_kernel, c=c, nb=nb, has_s0=has_s0),
        grid=(bsz // nb, nc),
        in_specs=in_specs,
        out_specs=[tok(), s_spec],
        out_shape=[jax.ShapeDtypeStruct((bsz, t, d), BF16),
                   jax.ShapeDtypeStruct((bsz, GDN_HEADS, GDN_DK, GDN_DV), F32)],
        scratch_shapes=[pltpu.VMEM((nb, GDN_HEADS, GDN_DK, GDN_DV), F32)],
        compiler_params=_params(("parallel", "arbitrary")),
        name="gdn_scan",
    )(*args)
    return yb.reshape(bsz * t, d), s_new


def _mem_attn_kernel(q_ref, k_ref, v_ref, o_ref, *, nb, tq):
    scale = MEM_HEAD_DIM ** -0.5
    for s in range(nb):
        rows = slice(s * tq, (s + 1) * tq)
        for h in range(MEM_HEADS):
            sl = slice(h * MEM_HEAD_DIM, (h + 1) * MEM_HEAD_DIM)
            sc = _dot_nt(q_ref[rows, sl], k_ref[s, :, sl]) * scale
            e = jnp.exp(sc - jnp.max(sc, axis=-1, keepdims=True))
            prob = e / jnp.sum(e, axis=-1, keepdims=True)
            o_ref[rows, sl] = _dot(prob, v_ref[s, :, sl]).astype(BF16)


def _mem_attn(p, mk, mv, bsz, t, tq, nb):
    d = D_MODEL
    m = mk.shape[1]
    nq = t // tq
    assert nb == 1 or nq == 1
    kv_spec = pl.BlockSpec((nb, m, d), lambda b, i: (b, 0, 0))
    return pl.pallas_call(
        functools.partial(_mem_attn_kernel, nb=nb, tq=tq),
        grid=(bsz // nb, nq),
        in_specs=[pl.BlockSpec((nb * tq, d), lambda b, i: (b * nq + i, COL_MQ)), kv_spec, kv_spec],
        out_specs=pl.BlockSpec((nb * tq, d), lambda b, i: (b * nq + i, 0)),
        out_shape=jax.ShapeDtypeStruct((bsz * t, d), BF16),
        compiler_params=_params(("parallel", "arbitrary")),
        name="mem_attn",
    )(p, mk, mv)


def _merge_kernel(x_ref, ya_ref, yb_ref, ym_ref, g0_ref, g1_ref, g2_ref, wc_ref, wg_ref, wm_ref, wo_ref,
                  ng_ref, x1_ref, h2_ref):
    d = functools.partial(jnp.dot, preferred_element_type=F32)
    merged = (jax.nn.sigmoid(_f32(g0_ref)) * d(ya_ref[...], wc_ref[...])
              + jax.nn.sigmoid(_f32(g1_ref)) * d(yb_ref[...], wg_ref[...])
              + jax.nn.sigmoid(_f32(g2_ref)) * d(ym_ref[...], wm_ref[...]))
    x1 = x_ref[...] + d(merged.astype(BF16), wo_ref[...])
    x1_ref[...] = x1
    h2_ref[...] = ((x1 * _rms_scale(x1)) * ng_ref[...]).astype(BF16)


def _merge(x, ya, yb, ym, p, wc, wg, wm, wo, norm_g, tm):
    n, d = x.shape
    tok = pl.BlockSpec((tm, d), lambda i: (i, 0))
    wsp = pl.BlockSpec((d, d), lambda i: (0, 0))

    def gate(j):
        return pl.BlockSpec((tm, d), lambda i, j=j: (i, COL_BG + j))

    return pl.pallas_call(
        _merge_kernel,
        grid=(n // tm,),
        in_specs=[tok, tok, tok, tok, gate(0), gate(1), gate(2), wsp, wsp, wsp, wsp,
                  pl.BlockSpec((1, d), lambda i: (0, 0))],
        out_specs=[tok, tok],
        out_shape=[jax.ShapeDtypeStruct((n, d), F32), jax.ShapeDtypeStruct((n, d), BF16)],
        compiler_params=_params(("parallel",)),
        name="merge",
    )(x, ya, yb, ym, p, p, p, wc, wg, wm, wo, norm_g.reshape(1, d))


_CAND_CELLS = tuple((a, b) for a in range(PEER_TOPK) for b in range(PEER_TOPK) if (a + 1) * (b + 1) <= PEER_TOPK)


def _top_values(x, count):
    vals = []
    cur = x
    for r in range(count):
        m = jnp.max(cur, axis=0, keepdims=True)
        vals.append(m)
        if r + 1 < count:
            cur = jnp.where(cur == m, -jnp.inf, cur)
    return vals


def _top_ranked(x, count):
    n_rows = x.shape[0]
    iota = lax.broadcasted_iota(jnp.int32, x.shape, 0).astype(F32)
    vals = []
    cur = x
    rank = jnp.full(x.shape, float(count), F32)
    for r in range(count):
        m = jnp.max(cur, axis=0, keepdims=True)
        first = jnp.min(jnp.where(cur == m, iota, float(n_rows)), axis=0, keepdims=True)
        one = iota == first
        rank = jnp.where(one, float(r), rank)
        cur = jnp.where(one, -jnp.inf, cur)
        vals.append(m)
    return vals, rank


def _count_rows(mask):
    return jnp.sum(jnp.where(mask, 1.0, 0.0), axis=0, keepdims=True)


def _route_kernel(h_ref, wq_ref, keys_ref, cut_ref, e0_ref, s1_ref, e1_ref, q_scr, tie_ref):
    tn = h_ref.shape[0]
    nsb = tn // LANES
    q = jnp.dot(h_ref[...], wq_ref[...], preferred_element_type=F32).astype(BF16)

    def store(h, cut, e0, s1, e1):
        for ns in range(nsb):
            cols = slice(ns * LANES, (ns + 1) * LANES)
            cut_ref[h, ns] = cut[:, cols]
            e0_ref[h, ns] = e0[:, cols]
            s1_ref[h, ns] = s1[:, cols]
            e1_ref[h, ns] = e1[:, cols]

    for h in range(PEER_HEADS):
        sc, top = [], []
        tied = jnp.zeros((1, tn), F32)
        for part in range(2):
            col = (2 * h + part) * PEER_D_HALF
            q_scr[2 * h + part] = q[:, col:col + PEER_D_HALF]
            s_t = lax.dot_general(keys_ref[part, h], q[:, col:col + PEER_D_HALF],
                                  (((1,), (1,)), ((), ())), preferred_element_type=F32)
            sc.append(s_t)
            top.append(_top_values(s_t, PEER_TOPK))
            tied = jnp.maximum(tied, jnp.abs(_count_rows(s_t >= top[part][-1]) - PEER_TOPK))
        cand = jnp.concatenate([top[0][a] + top[1][b] for a, b in _CAND_CELLS], axis=0)
        best = _top_values(cand, PEER_TOPK)
        c_max, thr = best[0], best[-1]
        z = jnp.exp(best[0] - c_max)
        for r in range(1, PEER_TOPK):
            z = z + jnp.exp(best[r] - c_max)
        top1 = jnp.concatenate(top[1], axis=0)
        n_sel = sum(_count_rows(top[0][a] + top1 >= thr) for a in range(PEER_TOPK))
        tied = jnp.maximum(tied, jnp.abs(n_sel - PEER_TOPK))
        cut = jnp.full(sc[0].shape, jnp.inf, F32)
        for b in range(PEER_TOPK):
            cut = jnp.where(sc[0] + top[1][b] >= thr, top[1][b], cut)
        store(h, cut, jnp.exp(sc[0] - top[0][0]) / z, sc[1], jnp.exp(sc[1] - top[1][0]))
        for ns in range(nsb):
            tie_ref[h * nsb + ns] = jnp.max(tied[:, ns * LANES:(ns + 1) * LANES])

    def exact_block(idx, carry):
        @pl.when(tie_ref[idx] > 0.0)
        def _():
            h = idx // nsb
            ns = idx % nsb
            rows = pl.ds(pl.multiple_of(ns * LANES, LANES), LANES)
            sc, top, rank = [], [], []
            for part in range(2):
                s_t = lax.dot_general(keys_ref[part, h], q_scr[2 * h + part, rows, :],
                                      (((1,), (1,)), ((), ())), preferred_element_type=F32)
                vals, rk = _top_ranked(s_t, PEER_TOPK)
                sc.append(s_t)
                top.append(vals)
                rank.append(rk)
            top1 = jnp.concatenate(top[1], axis=0)
            cand = jnp.concatenate([top[0][a] + top1 for a in range(PEER_TOPK)], axis=0)
            best, pos = _top_ranked(cand, PEER_TOPK)
            z = jnp.exp(best[0] - best[0])
            for r in range(1, PEER_TOPK):
                z = z + jnp.exp(best[r] - best[0])
            cnt = jnp.zeros_like(sc[0])
            for a in range(PEER_TOPK):
                cnt_a = _count_rows(pos[a * PEER_TOPK:(a + 1) * PEER_TOPK] < PEER_TOPK)
                cnt = jnp.where(rank[0] == float(a), cnt_a, cnt)
            cut_ref[h, ns] = jnp.where(cnt > 0.0, 1.0 - cnt, jnp.inf)
            e0_ref[h, ns] = jnp.exp(sc[0] - top[0][0]) / z
            s1_ref[h, ns] = -rank[1]
            e1_ref[h, ns] = jnp.exp(sc[1] - top[1][0])
        return carry

    lax.fori_loop(0, PEER_HEADS * nsb, exact_block, 0)


def _route(h2, wq, keys, tn):
    n, d = h2.shape
    nk = PEER_N_KEYS
    tiled = pl.BlockSpec((PEER_HEADS, tn // LANES, nk, LANES), lambda i: (0, i, 0, 0))
    tiled_shape = jax.ShapeDtypeStruct((PEER_HEADS, n // LANES, nk, LANES), F32)
    return pl.pallas_call(
        _route_kernel,
        grid=(n // tn,),
        in_specs=[pl.BlockSpec((tn, d), lambda i: (i, 0)),
                  pl.BlockSpec(wq.shape, lambda i: (0, 0)),
                  pl.BlockSpec(keys.shape, lambda i: (0, 0, 0, 0))],
        out_specs=[tiled, tiled, tiled, tiled],
        out_shape=[tiled_shape, tiled_shape, tiled_shape, tiled_shape],
        scratch_shapes=[pltpu.VMEM((2 * PEER_HEADS, tn, PEER_D_HALF), BF16),
                        pltpu.SMEM((PEER_HEADS * (tn // LANES),), F32)],
        compiler_params=_params(("parallel",)),
        name="peer_route",
    )(h2, wq, keys)


def _gelu(x):
    return 0.5 * x * (1.0 + lax.erf(x * (2.0 ** -0.5)))


PEER_TILE = 1024
GATE_HALF = 32
GATE_ROWS = 8


def _peer_kernel(h_ref, x1_ref, u_ref, vt_ref, cut_ref, e0_ref, s1_ref, e1_ref, ng_ref,
                 y_ref, acc_ref, act_ref, coef_ref, *, te):
    ei = pl.program_id(1)
    tn = h_ref.shape[0]
    groups = te // PEER_N_KEYS

    @pl.when(ei == 0)
    def _():
        acc_ref[...] = jnp.zeros_like(acc_ref)

    nsb = tn // LANES
    act = _gelu(lax.dot_general(u_ref[...], h_ref[...], (((1,), (1,)), ((), ())),
                                preferred_element_type=F32))
    for ns in range(nsb):
        act_ref[ns] = act[:, ns * LANES:(ns + 1) * LANES]

    def gate_block(ns):
        for jh in range(PEER_N_KEYS // GATE_HALF):
            jrows = slice(jh * GATE_HALF, (jh + 1) * GATE_HALF)
            for g0 in range(0, groups, GATE_ROWS):
                gates = [None] * GATE_ROWS
                for h in range(PEER_HEADS):
                    s1_t = s1_ref[h, ns, jrows, :]
                    e1_t = e1_ref[h, ns, jrows, :]
                    for k in range(GATE_ROWS):
                        i_key = ei * groups + g0 + k
                        sel = s1_t >= cut_ref[h, ns, pl.ds(i_key, 1), :]
                        term = jnp.where(sel, e1_t, 0.0) * e0_ref[h, ns, pl.ds(i_key, 1), :]
                        gates[k] = term if gates[k] is None else gates[k] + term
                for k in range(GATE_ROWS):
                    r0 = (g0 + k) * PEER_N_KEYS + jh * GATE_HALF
                    coef_ref[ns, r0:r0 + GATE_HALF, :] = (gates[k] * act_ref[ns, r0:r0 + GATE_HALF, :]).astype(BF16)

    def lane_block(ns, carry):
        gate_block(ns)
        return carry

    lax.fori_loop(0, nsb, lane_block, 0)
    coef = jnp.concatenate([coef_ref[ns] for ns in range(nsb)], axis=1)
    acc_ref[...] += jnp.dot(vt_ref[0], coef, preferred_element_type=F32)

    @pl.when(ei == pl.num_programs(1) - 1)
    def _():
        x2 = x1_ref[...] + acc_ref[...].T
        y_ref[...] = (x2 * _rms_scale(x2)) * ng_ref[...]


def _peer(h2, x1, u_bf, vt_bf, cut, e0, s1, e1, norm_g, tn):
    n, d = h2.shape
    ne = u_bf.shape[0]
    te = vt_bf.shape[2]
    nk = PEER_N_KEYS
    tok = pl.BlockSpec((tn, d), lambda i, e: (i, 0))
    tiled = pl.BlockSpec((PEER_HEADS, tn // LANES, nk, LANES), lambda i, e: (0, i, 0, 0))
    return pl.pallas_call(
        functools.partial(_peer_kernel, te=te),
        grid=(n // tn, ne // te),
        in_specs=[tok, tok,
                  pl.BlockSpec((te, d), lambda i, e: (e, 0)),
                  pl.BlockSpec((1, d, te), lambda i, e: (e, 0, 0)),
                  tiled, tiled, tiled, tiled,
                  pl.BlockSpec((1, d), lambda i, e: (0, 0))],
        out_specs=tok,
        out_shape=jax.ShapeDtypeStruct((n, d), F32),
        scratch_shapes=[pltpu.VMEM((d, tn), F32), pltpu.VMEM((tn // LANES, te, LANES), F32),
                        pltpu.VMEM((tn // LANES, te, LANES), BF16)],
        compiler_params=_params(("parallel", "arbitrary")),
        name="peer_mix",
    )(h2, x1, u_bf, vt_bf, cut, e0, s1, e1, norm_g.reshape(1, d))


def _tile(n, pref):
    return math.gcd(n, pref)


def _layer(x, mk, mv, bsz, t, weights, conv_state=None, gdn_conv_state=None, gdn_s0=None):
    (norm_mix_g, w_main, w_ab, conv_w, gdn_conv_w, a_log, dt_bias, gdn_norm_g, wc, wg, wm, wo, norm_ffn_g,
     wq, keys, u_bf, vt_bf, final_norm_g) = weights
    n = bsz * t
    sample = conv_state is not None
    p, pab = _inproj(x, norm_mix_g, w_main, w_ab, _tile(n, 1024))
    rows = _tile(n, 256) if sample else _tile(t, 512)
    ya, q, k, v, gb, u_tail = _prep(p, pab, conv_w, gdn_conv_w, a_log, dt_bias, rows, max(t // rows, 1),
                                    conv_state, gdn_conv_state)
    chunk = _tile(t, 64)
    if sample:
        yb, s_new = _gdn(q, k, v, p, gb, gdn_norm_g, bsz, t, chunk, 4, gdn_s0)
    else:
        yb, s_new = _gdn_stacked(q, k, v, p, gb, gdn_norm_g, bsz, t, chunk, _tile(bsz, 2))
    ym = _mem_attn(p, mk, mv, bsz, t, _tile(t, 512), _tile(bsz, 8) if sample else 1)
    x1, h2 = _merge(x, ya, yb, ym, p, wc, wg, wm, wo, norm_ffn_g, _tile(n, 256))
    tn = _tile(n, 512)
    cut, e0, s1, e1 = _route(h2, wq, keys, tn)
    y = _peer(h2, x1, u_bf, vt_bf, cut, e0, s1, e1, final_norm_g, tn)
    return y, u_tail, p, s_new


def kernel(x_prompt, x_sample, mem_prompt, cache_mem_k, cache_mem_v, state_conv, state_gdn_conv, state_gdn, norm_mix_g, w_in, conv_w, gdn_conv_w, gdn_a_log, gdn_dt_bias, gdn_norm_g, mem_norm_g, w_mem_kv, w_br_conv, w_br_gdn, w_br_mem, w_out, norm_ffn_g, peer_w_q, peer_sub_keys, peer_u, peer_v, final_norm_g):
    depth = w_in.shape[0]
    assert depth == 1, "single-layer step"
    d = D_MODEL
    bp, tp, _ = x_prompt.shape
    bs, ts, _ = x_sample.shape
    assert ts == SAMPLE_PAD_T // 2
    m_tok = mem_prompt.shape[1]
    mq = MEM_HEADS * MEM_HEAD_DIM
    l = 0

    w = w_in[l]
    ga0 = 7 * d
    w_main = jnp.concatenate([w[:, :ga0], w[:, ga0 + 2 * GDN_HEADS:]], axis=1).astype(BF16)
    w_ab = jnp.pad(w[:, ga0:ga0 + 2 * GDN_HEADS], ((0, 0), (0, SMALL_COLS - 2 * GDN_HEADS))).astype(BF16)
    weights = (norm_mix_g[l], w_main, w_ab, conv_w[l], gdn_conv_w[l], gdn_a_log[l], gdn_dt_bias[l], gdn_norm_g[l],
               w_br_conv[l].astype(BF16), w_br_gdn[l].astype(BF16), w_br_mem[l].astype(BF16), w_out[l].astype(BF16),
               norm_ffn_g[l], peer_w_q[l].astype(BF16), peer_sub_keys[l].astype(BF16),
               peer_u[l].astype(BF16),
               peer_v[l].astype(BF16).reshape(-1, PEER_TILE, d).transpose(0, 2, 1), final_norm_g)

    kv_shape = (bp, m_tok, mq)
    mk_p, mv_p = _mem_kv(mem_prompt.reshape(bp * m_tok, d), mem_norm_g[l], w_mem_kv[l].astype(BF16),
                         _tile(bp * m_tok, 256))
    y_p, u_tail_p, p_p, s_p = _layer(x_prompt.reshape(bp * tp, d), mk_p.reshape(kv_shape), mv_p.reshape(kv_shape),
                                     bp, tp, weights)
    rows_p = _tile(tp, 512)
    cb_p = u_tail_p.reshape(bp, tp // rows_p, SUBLANES, d)[:, -1, SUBLANES - (CONV_K - 1):, :]
    gb_p = p_p.reshape(bp, tp, -1)[:, tp - (GDN_CONV_K - 1):, COL_GQ * d:(COL_GV + 1) * d].astype(F32)

    pad_t = SAMPLE_PAD_T
    lead = pad_t - ts
    x_s = jnp.pad(x_sample, ((0, 0), (lead, 0), (0, 0))).reshape(bs * pad_t, d)
    cst = jnp.pad(state_conv[l], ((0, 0), (lead - (CONV_K - 1), ts), (0, 0))).reshape(bs * pad_t, d)
    gst = jnp.pad(state_gdn_conv[l], ((0, 0), (lead - (GDN_CONV_K - 1), ts), (0, 0))).reshape(bs * pad_t, 3 * d)
    y_s, u_s, p_s, s_s = _layer(x_s, cache_mem_k.astype(BF16).reshape(bs, -1, mq),
                                cache_mem_v.astype(BF16).reshape(bs, -1, mq),
                                bs, pad_t, weights, cst, gst, state_gdn.reshape(bs, GDN_HEADS, GDN_DK, GDN_DV))
    y_s = y_s.reshape(bs, pad_t, d)[:, lead:, :]
    cb_s = u_s.reshape(bs, pad_t, d)[:, pad_t - (CONV_K - 1):, :]
    gb_s = p_s.reshape(bs, pad_t, -1)[:, pad_t - (GDN_CONV_K - 1):, COL_GQ * d:(COL_GV + 1) * d].astype(F32)

    mshape = (depth, bp, m_tok, MEM_HEADS, MEM_HEAD_DIM)
    return (y_p.reshape(bp, tp, d), y_s,
            mk_p.reshape(mshape), mv_p.reshape(mshape),
            cb_p[None], gb_p[None], s_p[None],
            cb_s[None], gb_s[None], s_s[None])
```

```python
import functools
import math

import jax
import jax.numpy as jnp
from jax import lax
from jax.experimental import pallas as pl
from jax.experimental.pallas import tpu as pltpu

F32 = jnp.float32
BF16 = jnp.bfloat16

EPS = 1e-6
D_MODEL = 1024
CONV_K = 3
GDN_HEADS = 8
GDN_DK = 128
GDN_DV = 128
GDN_CONV_K = 4
MEM_HEADS = 4
MEM_HEAD_DIM = 256
PEER_HEADS = 8
PEER_N_KEYS = 128
PEER_D_HALF = 128
PEER_TOPK = 16
N_BRANCHES = 3

COL_AU, COL_AB, COL_AC, COL_GQ, COL_GK, COL_GV, COL_GZ, COL_MQ, COL_BG = 0, 1, 2, 3, 4, 5, 6, 7, 8
N_MAIN_BLOCKS = 11
SMALL_COLS = 128
SUBLANES = 8
LANES = 128
SAMPLE_PAD_T = 8
VMEM_LIMIT = 56 * 1024 * 1024


def _params(sem):
    return pltpu.CompilerParams(dimension_semantics=sem, vmem_limit_bytes=VMEM_LIMIT)


def _dot(a, b):
    return jnp.dot(a.astype(BF16), b.astype(BF16), preferred_element_type=F32)


def _dot_nt(a, b):
    return lax.dot_general(a.astype(BF16), b.astype(BF16), (((1,), (1,)), ((), ())),
                           preferred_element_type=F32)


def _rms_scale(x):
    return lax.rsqrt(jnp.mean(x * x, axis=-1, keepdims=True) + EPS)


def _inproj_kernel(x_ref, g_ref, w_ref, wab_ref, p_ref, pab_ref, h_ref):
    @pl.when(pl.program_id(1) == 0)
    def _():
        x = x_ref[...]
        h = ((x * _rms_scale(x)) * g_ref[...]).astype(BF16)
        h_ref[...] = h
        pab_ref[...] = jnp.dot(h, wab_ref[...], preferred_element_type=F32)

    p_ref[...] = jnp.dot(h_ref[...], w_ref[...], preferred_element_type=F32).astype(p_ref.dtype)


def _inproj(x, g, w_main, w_ab, tm):
    n, d = x.shape
    nblk = w_main.shape[1] // d
    return pl.pallas_call(
        _inproj_kernel,
        grid=(n // tm, nblk),
        in_specs=[
            pl.BlockSpec((tm, d), lambda i, j: (i, 0)),
            pl.BlockSpec((1, d), lambda i, j: (0, 0)),
            pl.BlockSpec((d, d), lambda i, j: (0, j)),
            pl.BlockSpec((d, SMALL_COLS), lambda i, j: (0, 0)),
        ],
        out_specs=[
            pl.BlockSpec((tm, d), lambda i, j: (i, j)),
            pl.BlockSpec((tm, SMALL_COLS), lambda i, j: (i, 0)),
        ],
        out_shape=[
            jax.ShapeDtypeStruct((n, nblk * d), BF16),
            jax.ShapeDtypeStruct((n, SMALL_COLS), F32),
        ],
        scratch_shapes=[pltpu.VMEM((tm, d), BF16)],
        compiler_params=_params(("parallel", "arbitrary")),
        name="inproj",
    )(x, g.reshape(1, d), w_main, w_ab)


def _mem_kv_kernel(x_ref, g_ref, w_ref, k_ref, v_ref):
    x = x_ref[...]
    h = ((x * _rms_scale(x)) * g_ref[...]).astype(BF16)
    kv = jnp.dot(h, w_ref[...], preferred_element_type=F32)
    half = kv.shape[1] // 2
    k_ref[...] = kv[:, :half]
    v_ref[...] = kv[:, half:]


def _mem_kv(x, g, w, tm):
    n, d = x.shape
    c = w.shape[1]
    out = jax.ShapeDtypeStruct((n, c // 2), F32)
    ospec = pl.BlockSpec((tm, c // 2), lambda i: (i, 0))
    return pl.pallas_call(
        _mem_kv_kernel,
        grid=(n // tm,),
        in_specs=[
            pl.BlockSpec((tm, d), lambda i: (i, 0)),
            pl.BlockSpec((1, d), lambda i: (0, 0)),
            pl.BlockSpec((d, c), lambda i: (0, 0)),
        ],
        out_specs=[ospec, ospec],
        out_shape=[out, out],
        compiler_params=_params(("parallel",)),
        name="mem_kv",
    )(x, g.reshape(1, d), w)


def _causal_conv(prev, cur, w_ref, k):
    full = jnp.concatenate([prev, cur], axis=0)
    y = cur * w_ref[k - 1:k, :]
    for s in range(1, k):
        y = y + pltpu.roll(full, s, axis=0)[prev.shape[0]:, :] * w_ref[k - 1 - s:k - s, :]
    return y


def _f32(ref):
    return ref[...].astype(F32)


def _prep_kernel(*refs, rows, tiles_per_seq, sample):
    if sample:
        (au, ab, ac, gq, gk, gv, pab, au_p, ac_p, gq_p, gk_p, gv_p, cw, gw, alog, dtb, cst, gst,
         ya_o, q_o, k_o, v_o, gb_o, ut_o) = refs
    else:
        (au, ab, ac, gq, gk, gv, pab, au_p, ac_p, gq_p, gk_p, gv_p, cw, gw, alog, dtb,
         ya_o, q_o, k_o, v_o, gb_o, ut_o) = refs
    d = D_MODEL
    first = (pl.program_id(0) % tiles_per_seq) == 0
    keep = jnp.where(first, 0.0, 1.0).astype(F32)

    u = _f32(ac) * _f32(au)
    u_prev = (_f32(ac_p) * _f32(au_p)) * keep
    if sample:
        real = (lax.broadcasted_iota(jnp.int32, (rows, 1), 0) % SAMPLE_PAD_T) >= (SAMPLE_PAD_T // 2)
        u = jnp.where(real, u, cst[...])
    ya_o[...] = (_f32(ab) * _causal_conv(u_prev, u, cw, CONV_K)).astype(BF16)
    ut_o[...] = u[rows - ut_o.shape[0]:, :]

    outs = (q_o, k_o, v_o)
    for idx, (cur_ref, prev_ref) in enumerate(((gq, gq_p), (gk, gk_p), (gv, gv_p))):
        cur = _f32(cur_ref)
        if sample:
            cur = jnp.where(real, cur, gst[:, idx * d:(idx + 1) * d])
        prev = _f32(prev_ref) * keep
        wslab = gw.at[:, idx * d:(idx + 1) * d]
        y = _causal_conv(prev, cur, wslab, GDN_CONV_K)
        y = y * jax.nn.sigmoid(y)
        if idx < 2:
            scale = GDN_DK ** -0.5 if idx == 0 else 1.0
            for h in range(GDN_HEADS):
                yh = y[:, h * GDN_DK:(h + 1) * GDN_DK]
                r = lax.rsqrt(jnp.sum(yh * yh, axis=-1, keepdims=True) + EPS)
                outs[idx][:, h * GDN_DK:(h + 1) * GDN_DK] = yh * (r * scale)
        else:
            outs[idx][...] = y

    pa = pab[...]
    lane = lax.broadcasted_iota(jnp.int32, pa.shape, 1)
    z = pa + dtb[...]
    softplus = jnp.maximum(z, 0.0) + jnp.log1p(jnp.exp(-jnp.abs(z)))
    g_log = -jnp.exp(alog[...]) * softplus
    gb = jnp.where(lane < GDN_HEADS, g_log, jnp.where(lane < 2 * GDN_HEADS, jax.nn.sigmoid(pa), 0.0))
    if sample:
        gb = jnp.where(real, gb, 0.0)
    gb_o[...] = gb


def _prep(p, pab, conv_w, gdn_conv_w, a_log, dt_bias, rows, tiles_per_seq, conv_state=None, gdn_state=None):
    n = p.shape[0]
    d = D_MODEL
    sample = conv_state is not None
    nt = n // rows
    prev_rows = 2 * SUBLANES
    rpb = rows // prev_rows

    def cur(c):
        return pl.BlockSpec((rows, d), lambda i, c=c: (i, c))

    def prev(c):
        return pl.BlockSpec((prev_rows, d), lambda i, c=c: (jnp.maximum(i * rpb - 1, 0), c))

    alog128 = jnp.zeros((1, SMALL_COLS), F32).at[0, :GDN_HEADS].set(a_log)
    dtb128 = jnp.zeros((1, SMALL_COLS), F32).at[0, :GDN_HEADS].set(dt_bias)
    in_specs = [cur(COL_AU), cur(COL_AB), cur(COL_AC), cur(COL_GQ), cur(COL_GK), cur(COL_GV),
                pl.BlockSpec((rows, SMALL_COLS), lambda i: (i, 0)),
                prev(COL_AU), prev(COL_AC), prev(COL_GQ), prev(COL_GK), prev(COL_GV),
                pl.BlockSpec((CONV_K, d), lambda i: (0, 0)),
                pl.BlockSpec((GDN_CONV_K, 3 * d), lambda i: (0, 0)),
                pl.BlockSpec((1, SMALL_COLS), lambda i: (0, 0)),
                pl.BlockSpec((1, SMALL_COLS), lambda i: (0, 0))]
    args = [p, p, p, p, p, p, pab, p, p, p, p, p, conv_w, gdn_conv_w, alog128, dtb128]
    if sample:
        in_specs += [pl.BlockSpec((rows, d), lambda i: (i, 0)), pl.BlockSpec((rows, 3 * d), lambda i: (i, 0))]
        args += [conv_state, gdn_state]
    tail = rows if sample else SUBLANES
    out_specs = [pl.BlockSpec((rows, d), lambda i: (i, 0))] * 4 + [
        pl.BlockSpec((rows, SMALL_COLS), lambda i: (i, 0)),
        pl.BlockSpec((tail, d), lambda i: (i, 0))]
    out_shape = [jax.ShapeDtypeStruct((n, d), BF16)] + [jax.ShapeDtypeStruct((n, d), F32)] * 3 + [
        jax.ShapeDtypeStruct((n, SMALL_COLS), F32),
        jax.ShapeDtypeStruct((nt * tail, d), F32)]
    return pl.pallas_call(
        functools.partial(_prep_kernel, rows=rows, tiles_per_seq=tiles_per_seq, sample=sample),
        grid=(nt,),
        in_specs=in_specs,
        out_specs=out_specs,
        out_shape=out_shape,
        compiler_params=_params(("parallel",)),
        name="prep",
    )(*args)


def _chunk_cumsum(g, c):
    row = lax.broadcasted_iota(jnp.int32, g.shape, 0)
    sh = 1
    while sh < c:
        g = g + jnp.where(row >= sh, pltpu.roll(g, sh, axis=0), 0.0)
        sh *= 2
    return g


def _gdn_stacked_kernel(q_ref, k_ref, v_ref, z_ref, gb_ref, ng_ref, yb_ref, sout_ref, s_scr, *, c, grp, nb):
    ci = pl.program_id(1)

    @pl.when(ci == 0)
    def _():
        s_scr[...] = jnp.zeros_like(s_scr)

    r = grp * c
    ng = ng_ref[...]
    seqs = []
    for s in range(nb):
        gb = gb_ref[s]
        gc = _chunk_cumsum(gb, c)
        g_last = gc[c - 1:c, :]
        seqs.append(dict(gb=gb, gc=gc, gct=gc.T, e_g=jnp.exp(gc), e_kd=jnp.exp(g_last - gc), e_last=jnp.exp(g_last)))

    ri = lax.broadcasted_iota(jnp.int32, (r, r), 0)
    cj = lax.broadcasted_iota(jnp.int32, (r, r), 1)
    same = (ri // c) == (cj // c)
    incl = same & (ri >= cj)
    strict = same & (ri > cj)
    head_block = (lax.broadcasted_iota(jnp.int32, (r, grp * GDN_DK), 0) // c
                  == lax.broadcasted_iota(jnp.int32, (r, grp * GDN_DK), 1) // GDN_DK)

    def block_layout(x):
        return jnp.where(head_block, jnp.concatenate([x] * grp, axis=1), 0.0)

    units = []
    for s in range(nb):
        for g0 in range(0, GDN_HEADS, grp):
            heads = list(range(g0, g0 + grp))
            sq = seqs[s]

            def stack(ref, s=s, heads=heads):
                return jnp.concatenate([ref[s, :, h * GDN_DK:(h + 1) * GDN_DK] for h in heads], axis=0)

            def col(a, off=0, heads=heads):
                return jnp.concatenate([a[:, off + h:off + h + 1] for h in heads], axis=0)

            qs, ks, vs = stack(q_ref), stack(k_ref), stack(v_ref)
            beta = col(sq["gb"], GDN_HEADS)
            kb = ks * beta
            units.append(dict(s=s, heads=heads, qs=qs, ks=ks, kb=kb, vb=vs * beta,
                              gcol=col(sq["gc"]), egcol=col(sq["e_g"]), ekdcol=col(sq["e_kd"]),
                              grow=jnp.concatenate([sq["gct"][h:h + 1, :] for h in heads], axis=1),
                              rows=slice((s * GDN_HEADS + g0) * GDN_DK, (s * GDN_HEADS + g0 + grp) * GDN_DK)))

    for u in units:
        u["aq"] = _dot_nt(jnp.concatenate([u["kb"], u["qs"]], axis=0), u["ks"])
    for u in units:
        decay = jnp.exp(jnp.where(incl, u["gcol"] - u["grow"], -jnp.inf))
        u["pw"] = -jnp.where(strict, u["aq"][:r] * decay, 0.0)
        u["attn"] = jnp.where(incl, u["aq"][r:] * decay, 0.0)
        u["x"] = jnp.concatenate([u["vb"], u["kb"] * u["egcol"]], axis=1)
    span = 1
    while span < c:
        span *= 2
        for u in units:
            if span < c:
                both = _dot(u["pw"], jnp.concatenate([u["pw"], u["x"]], axis=1))
                u["pw"] = both[:, :r]
                u["x"] = u["x"] + both[:, r:]
            else:
                u["x"] = u["x"] + _dot(u["pw"], u["x"])
    for u in units:
        u_val, w_k = u["x"][:, :GDN_DV], u["x"][:, GDN_DV:]
        u["s_old"] = s_scr[u["rows"], :]
        ws = _dot(jnp.concatenate([block_layout(w_k), block_layout(u["qs"] * u["egcol"])], axis=0), u["s_old"])
        u["v_new"] = u_val - ws[:r]
        u["o"] = ws[r:]
    for u in units:
        e_last = seqs[u["s"]]["e_last"]
        o = u["o"] + _dot(u["attn"], u["v_new"])
        e_rows = jnp.concatenate([jnp.broadcast_to(e_last[:, h:h + 1], (GDN_DK, 1)) for h in u["heads"]], axis=0)
        kd_t = block_layout(u["ks"] * u["ekdcol"]).T
        s_scr[u["rows"], :] = u["s_old"] * e_rows + _dot(kd_t, u["v_new"])
        for i, h in enumerate(u["heads"]):
            sl = slice(h * GDN_DK, (h + 1) * GDN_DK)
            oh = o[i * c:(i + 1) * c, :]
            zh = z_ref[u["s"], :, sl].astype(F32)
            yb_ref[u["s"], :, sl] = (((oh * _rms_scale(oh)) * ng) * (zh * jax.nn.sigmoid(zh))).astype(BF16)

    @pl.when(ci == pl.num_programs(1) - 1)
    def _():
        sout_ref[...] = s_scr[...]


def _gdn_stacked(q, k, v, p, gb, norm_g, bsz, t, c, nb):
    d = D_MODEL
    nc = t // c
    grp = min(GDN_HEADS, 256 // c)
    s_rows = nb * GDN_HEADS * GDN_DK

    def tok(col=0):
        return pl.BlockSpec((nb, c, d), lambda b, i, col=col: (b, i, col))

    yb, s_new = pl.pallas_call(
        functools.partial(_gdn_stacked_kernel, c=c, grp=grp, nb=nb),
        grid=(bsz // nb, nc),
        in_specs=[tok(), tok(), tok(), tok(COL_GZ),
                  pl.BlockSpec((nb, c, SMALL_COLS), lambda b, i: (b, i, 0)),
                  pl.BlockSpec((1, GDN_DV), lambda b, i: (0, 0))],
        out_specs=[tok(), pl.BlockSpec((s_rows, GDN_DV), lambda b, i: (b, 0))],
        out_shape=[jax.ShapeDtypeStruct((bsz, t, d), BF16),
                   jax.ShapeDtypeStruct((bsz * GDN_HEADS * GDN_DK, GDN_DV), F32)],
        scratch_shapes=[pltpu.VMEM((s_rows, GDN_DV), F32)],
        compiler_params=_params(("parallel", "arbitrary")),
        name="gdn_scan_prompt",
    )(*[a.reshape(bsz, t, -1) for a in (q, k, v, p, gb)], norm_g.reshape(1, GDN_DV))
    return yb.reshape(bsz * t, d), s_new.reshape(bsz, GDN_HEADS, GDN_DK, GDN_DV)


def _gdn_kernel(*refs, c, nb, has_s0):
    if has_s0:
        q_ref, k_ref, v_ref, z_ref, gb_ref, ng_ref, s0_ref, yb_ref, sout_ref, s_scr = refs
    else:
        q_ref, k_ref, v_ref, z_ref, gb_ref, ng_ref, yb_ref, sout_ref, s_scr = refs
    ci = pl.program_id(1)

    @pl.when(ci == 0)
    def _():
        if has_s0:
            s_scr[...] = s0_ref[...]
        else:
            s_scr[...] = jnp.zeros_like(s_scr)

    ri = lax.broadcasted_iota(jnp.int32, (c, c), 0)
    cj = lax.broadcasted_iota(jnp.int32, (c, c), 1)
    incl = ri >= cj
    strict = ri > cj
    ng = ng_ref[...]
    row = lax.broadcasted_iota(jnp.int32, (c, SMALL_COLS), 0)
    chains = [(s, h) for s in range(nb) for h in range(GDN_HEADS)]

    seq = []
    for s in range(nb):
        gb = gb_ref[s]
        gc = gb
        sh = 1
        while sh < c:
            gc = gc + jnp.where(row >= sh, pltpu.roll(gc, sh, axis=0), 0.0)
            sh *= 2
        g_last = gc[c - 1:c, :]
        seq.append(dict(gb=gb, gc=gc, gct=gc.T, e_g=jnp.exp(gc), e_kd=jnp.exp(g_last - gc), e_last=jnp.exp(g_last)))

    st = []
    for s, h in chains:
        sl = slice(h * GDN_DK, (h + 1) * GDN_DK)
        sq = seq[s]
        qh, kh, vh = q_ref[s, :, sl], k_ref[s, :, sl], v_ref[s, :, sl]
        beta = sq["gb"][:, GDN_HEADS + h:GDN_HEADS + h + 1]
        kb = kh * beta
        aq = _dot_nt(jnp.concatenate([kb, qh], axis=0), kh)
        st.append(dict(sl=sl, qh=qh, kh=kh, kb=kb, vb=vh * beta, aq=aq))

    for (s, h), e in zip(chains, st):
        sq = seq[s]
        decay = jnp.exp(jnp.where(incl, sq["gc"][:, h:h + 1] - sq["gct"][h:h + 1, :], -jnp.inf))
        e["neg"] = -jnp.where(strict, e["aq"][:c] * decay, 0.0)
        e["attn"] = jnp.where(incl, e["aq"][c:] * decay, 0.0)
        e["x"] = jnp.concatenate([e["vb"], e["kb"] * sq["e_g"][:, h:h + 1]], axis=1)
        e["pw"] = e["neg"]

    span = 1
    while span < c:
        span *= 2
        for e in st:
            if span < c:
                both = _dot(e["pw"], jnp.concatenate([e["pw"], e["x"]], axis=1))
                e["pw"] = both[:, :c]
                e["x"] = e["x"] + both[:, c:]
            else:
                e["x"] = e["x"] + _dot(e["pw"], e["x"])

    for (s, h), e in zip(chains, st):
        sq = seq[s]
        u_val, w_k = e["x"][:, :GDN_DV], e["x"][:, GDN_DV:]
        q_dec = e["qh"] * sq["e_g"][:, h:h + 1]
        e["s_old"] = s_scr[s, h]
        ws = _dot(jnp.concatenate([w_k, q_dec], axis=0), e["s_old"])
        e["v_new"] = u_val - ws[:c]
        e["o_inter"] = ws[c:]

    for (s, h), e in zip(chains, st):
        sq = seq[s]
        o = e["o_inter"] + _dot(e["attn"], e["v_new"])
        k_dec = e["kh"] * sq["e_kd"][:, h:h + 1]
        s_scr[s, h] = e["s_old"] * sq["e_last"][:, h:h + 1] + _dot(k_dec.T, e["v_new"])
        zh = z_ref[s, :, e["sl"]].astype(F32)
        yb_ref[s, :, e["sl"]] = (((o * _rms_scale(o)) * ng) * (zh * jax.nn.sigmoid(zh))).astype(BF16)

    @pl.when(ci == pl.num_programs(1) - 1)
    def _():
        sout_ref[...] = s_scr[...]


def _gdn(q, k, v, p, gb, norm_g, bsz, t, c, nb, s0=None):
    d = D_MODEL
    nc = t // c
    has_s0 = s0 is not None

    def tok(col=0):
        return pl.BlockSpec((nb, c, d), lambda b, i, col=col: (b, i, col))

    in_specs = [tok(), tok(), tok(), tok(COL_GZ),
                pl.BlockSpec((nb, c, SMALL_COLS), lambda b, i: (b, i, 0)),
                pl.BlockSpec((1, GDN_DV), lambda b, i: (0, 0))]
    args = [a.reshape(bsz, t, -1) for a in (q, k, v, p, gb)] + [norm_g.reshape(1, GDN_DV)]
    s_spec = pl.BlockSpec((nb, GDN_HEADS, GDN_DK, GDN_DV), lambda b, i: (b, 0, 0, 0))
    if has_s0:
        in_specs.append(s_spec)
        args.append(s0)
    yb, s_new = pl.pallas_call(
        functools.partial(_gdn_kernel, c=c, nb=nb, has_s0=has_s0),
        grid=(bsz // nb, nc),
        in_specs=in_specs,
        out_specs=[tok(), s_spec],
        out_shape=[jax.ShapeDtypeStruct((bsz, t, d), BF16),
                   jax.ShapeDtypeStruct((bsz, GDN_HEADS, GDN_DK, GDN_DV), F32)],
        scratch_shapes=[pltpu.VMEM((nb, GDN_HEADS, GDN_DK, GDN_DV), F32)],
        compiler_params=_params(("parallel", "arbitrary")),
        name="gdn_scan",
    )(*args)
    return yb.reshape(bsz * t, d), s_new


def _mem_attn_kernel(q_ref, k_ref, v_ref, o_ref, *, nb, tq):
    scale = MEM_HEAD_DIM ** -0.5
    for s in range(nb):
        rows = slice(s * tq, (s + 1) * tq)
        for h in range(MEM_HEADS):
            sl = slice(h * MEM_HEAD_DIM, (h + 1) * MEM_HEAD_DIM)
            sc = _dot_nt(q_ref[rows, sl], k_ref[s, :, sl]) * scale
            e = jnp.exp(sc - jnp.max(sc, axis=-1, keepdims=True))
            prob = e / jnp.sum(e, axis=-1, keepdims=True)
            o_ref[rows, sl] = _dot(prob, v_ref[s, :, sl]).astype(BF16)


def _mem_attn(p, mk, mv, bsz, t, tq, nb):
    d = D_MODEL
    m = mk.shape[1]
    nq = t // tq
    assert nb == 1 or nq == 1
    kv_spec = pl.BlockSpec((nb, m, d), lambda b, i: (b, 0, 0))
    return pl.pallas_call(
        functools.partial(_mem_attn_kernel, nb=nb, tq=tq),
        grid=(bsz // nb, nq),
        in_specs=[pl.BlockSpec((nb * tq, d), lambda b, i: (b * nq + i, COL_MQ)), kv_spec, kv_spec],
        out_specs=pl.BlockSpec((nb * tq, d), lambda b, i: (b * nq + i, 0)),
        out_shape=jax.ShapeDtypeStruct((bsz * t, d), BF16),
        compiler_params=_params(("parallel", "arbitrary")),
        name="mem_attn",
    )(p, mk, mv)


def _merge_kernel(x_ref, ya_ref, yb_ref, ym_ref, g0_ref, g1_ref, g2_ref, wc_ref, wg_ref, wm_ref, wo_ref,
                  ng_ref, x1_ref, h2_ref):
    d = functools.partial(jnp.dot, preferred_element_type=F32)
    merged = (jax.nn.sigmoid(_f32(g0_ref)) * d(ya_ref[...], wc_ref[...])
              + jax.nn.sigmoid(_f32(g1_ref)) * d(yb_ref[...], wg_ref[...])
              + jax.nn.sigmoid(_f32(g2_ref)) * d(ym_ref[...], wm_ref[...]))
    x1 = x_ref[...] + d(merged.astype(BF16), wo_ref[...])
    x1_ref[...] = x1
    h2_ref[...] = ((x1 * _rms_scale(x1)) * ng_ref[...]).astype(BF16)


def _merge(x, ya, yb, ym, p, wc, wg, wm, wo, norm_g, tm):
    n, d = x.shape
    tok = pl.BlockSpec((tm, d), lambda i: (i, 0))
    wsp = pl.BlockSpec((d, d), lambda i: (0, 0))

    def gate(j):
        return pl.BlockSpec((tm, d), lambda i, j=j: (i, COL_BG + j))

    return pl.pallas_call(
        _merge_kernel,
        grid=(n // tm,),
        in_specs=[tok, tok, tok, tok, gate(0), gate(1), gate(2), wsp, wsp, wsp, wsp,
                  pl.BlockSpec((1, d), lambda i: (0, 0))],
        out_specs=[tok, tok],
        out_shape=[jax.ShapeDtypeStruct((n, d), F32), jax.ShapeDtypeStruct((n, d), BF16)],
        compiler_params=_params(("parallel",)),
        name="merge",
    )(x, ya, yb, ym, p, p, p, wc, wg, wm, wo, norm_g.reshape(1, d))


_CAND_CELLS = tuple((a, b) for a in range(PEER_TOPK) for b in range(PEER_TOPK) if (a + 1) * (b + 1) <= PEER_TOPK)


def _top_values(x, count):
    vals = []
    cur = x
    for r in range(count):
        m = jnp.max(cur, axis=0, keepdims=True)
        vals.append(m)
        if r + 1 < count:
            cur = jnp.where(cur == m, -jnp.inf, cur)
    return vals


def _top_ranked(x, count):
    n_rows = x.shape[0]
    iota = lax.broadcasted_iota(jnp.int32, x.shape, 0).astype(F32)
    vals = []
    cur = x
    rank = jnp.full(x.shape, float(count), F32)
    for r in range(count):
        m = jnp.max(cur, axis=0, keepdims=True)
        first = jnp.min(jnp.where(cur == m, iota, float(n_rows)), axis=0, keepdims=True)
        one = iota == first
        rank = jnp.where(one, float(r), rank)
        cur = jnp.where(one, -jnp.inf, cur)
        vals.append(m)
    return vals, rank


def _count_rows(mask):
    return jnp.sum(jnp.where(mask, 1.0, 0.0), axis=0, keepdims=True)


def _route_kernel(h_ref, wq_ref, keys_ref, cut_ref, e0_ref, s1_ref, e1_ref, q_scr, tie_ref):
    tn = h_ref.shape[0]
    nsb = tn // LANES
    q = jnp.dot(h_ref[...], wq_ref[...], preferred_element_type=F32).astype(BF16)

    def store(h, cut, e0, s1, e1):
        for ns in range(nsb):
            cols = slice(ns * LANES, (ns + 1) * LANES)
            cut_ref[h, ns] = cut[:, cols]
            e0_ref[h, ns] = e0[:, cols]
            s1_ref[h, ns] = s1[:, cols]
            e1_ref[h, ns] = e1[:, cols]

    for h in range(PEER_HEADS):
        sc, top = [], []
        tied = jnp.zeros((1, tn), F32)
        for part in range(2):
            col = (2 * h + part) * PEER_D_HALF
            q_scr[2 * h + part] = q[:, col:col + PEER_D_HALF]
            s_t = lax.dot_general(keys_ref[part, h], q[:, col:col + PEER_D_HALF],
                                  (((1,), (1,)), ((), ())), preferred_element_type=F32)
            sc.append(s_t)
            top.append(_top_values(s_t, PEER_TOPK))
            tied = jnp.maximum(tied, jnp.abs(_count_rows(s_t >= top[part][-1]) - PEER_TOPK))
        cand = jnp.concatenate([top[0][a] + top[1][b] for a, b in _CAND_CELLS], axis=0)
        best = _top_values(cand, PEER_TOPK)
        c_max, thr = best[0], best[-1]
        z = jnp.exp(best[0] - c_max)
        for r in range(1, PEER_TOPK):
            z = z + jnp.exp(best[r] - c_max)
        top1 = jnp.concatenate(top[1], axis=0)
        n_sel = sum(_count_rows(top[0][a] + top1 >= thr) for a in range(PEER_TOPK))
        tied = jnp.maximum(tied, jnp.abs(n_sel - PEER_TOPK))
        cut = jnp.full(sc[0].shape, jnp.inf, F32)
        for b in range(PEER_TOPK):
            cut = jnp.where(sc[0] + top[1][b] >= thr, top[1][b], cut)
        store(h, cut, jnp.exp(sc[0] - top[0][0]) / z, sc[1], jnp.exp(sc[1] - top[1][0]))
        for ns in range(nsb):
            tie_ref[h * nsb + ns] = jnp.max(tied[:, ns * LANES:(ns + 1) * LANES])

    def exact_block(idx, carry):
        @pl.when(tie_ref[idx] > 0.0)
        def _():
            h = idx // nsb
            ns = idx % nsb
            rows = pl.ds(pl.multiple_of(ns * LANES, LANES), LANES)
            sc, top, rank = [], [], []
            for part in range(2):
                s_t = lax.dot_general(keys_ref[part, h], q_scr[2 * h + part, rows, :],
                                      (((1,), (1,)), ((), ())), preferred_element_type=F32)
                vals, rk = _top_ranked(s_t, PEER_TOPK)
                sc.append(s_t)
                top.append(vals)
                rank.append(rk)
            top1 = jnp.concatenate(top[1], axis=0)
            cand = jnp.concatenate([top[0][a] + top1 for a in range(PEER_TOPK)], axis=0)
            best, pos = _top_ranked(cand, PEER_TOPK)
            z = jnp.exp(best[0] - best[0])
            for r in range(1, PEER_TOPK):
                z = z + jnp.exp(best[r] - best[0])
            cnt = jnp.zeros_like(sc[0])
            for a in range(PEER_TOPK):
                cnt_a = _count_rows(pos[a * PEER_TOPK:(a + 1) * PEER_TOPK] < PEER_TOPK)
                cnt = jnp.where(rank[0] == float(a), cnt_a, cnt)
            cut_ref[h, ns] = jnp.where(cnt > 0.0, 1.0 - cnt, jnp.inf)
            e0_ref[h, ns] = jnp.exp(sc[0] - top[0][0]) / z
            s1_ref[h, ns] = -rank[1]
            e1_ref[h, ns] = jnp.exp(sc[1] - top[1][0])
        return carry

    lax.fori_loop(0, PEER_HEADS * nsb, exact_block, 0)


def _route(h2, wq, keys, tn):
    n, d = h2.shape
    nk = PEER_N_KEYS
    tiled = pl.BlockSpec((PEER_HEADS, tn // LANES, nk, LANES), lambda i: (0, i, 0, 0))
    tiled_shape = jax.ShapeDtypeStruct((PEER_HEADS, n // LANES, nk, LANES), F32)
    return pl.pallas_call(
        _route_kernel,
        grid=(n // tn,),
        in_specs=[pl.BlockSpec((tn, d), lambda i: (i, 0)),
                  pl.BlockSpec(wq.shape, lambda i: (0, 0)),
                  pl.BlockSpec(keys.shape, lambda i: (0, 0, 0, 0))],
        out_specs=[tiled, tiled, tiled, tiled],
        out_shape=[tiled_shape, tiled_shape, tiled_shape, tiled_shape],
        scratch_shapes=[pltpu.VMEM((2 * PEER_HEADS, tn, PEER_D_HALF), BF16),
                        pltpu.SMEM((PEER_HEADS * (tn // LANES),), F32)],
        compiler_params=_params(("parallel",)),
        name="peer_route",
    )(h2, wq, keys)


def _gelu(x):
    return 0.5 * x * (1.0 + lax.erf(x * (2.0 ** -0.5)))


PEER_TILE = 1024
GATE_HALF = 32
GATE_ROWS = 8


def _peer_kernel(h_ref, x1_ref, u_ref, vt_ref, cut_ref, e0_ref, s1_ref, e1_ref, ng_ref,
                 y_ref, acc_ref, act_ref, coef_ref, *, te):
    ei = pl.program_id(1)
    tn = h_ref.shape[0]
    groups = te // PEER_N_KEYS

    @pl.when(ei == 0)
    def _():
        acc_ref[...] = jnp.zeros_like(acc_ref)

    nsb = tn // LANES
    act = _gelu(lax.dot_general(u_ref[...], h_ref[...], (((1,), (1,)), ((), ())),
                                preferred_element_type=F32))
    for ns in range(nsb):
        act_ref[ns] = act[:, ns * LANES:(ns + 1) * LANES]

    def gate_block(ns):
        for jh in range(PEER_N_KEYS // GATE_HALF):
            jrows = slice(jh * GATE_HALF, (jh + 1) * GATE_HALF)
            for g0 in range(0, groups, GATE_ROWS):
                gates = [None] * GATE_ROWS
                for h in range(PEER_HEADS):
                    s1_t = s1_ref[h, ns, jrows, :]
                    e1_t = e1_ref[h, ns, jrows, :]
                    for k in range(GATE_ROWS):
                        i_key = ei * groups + g0 + k
                        sel = s1_t >= cut_ref[h, ns, pl.ds(i_key, 1), :]
                        term = jnp.where(sel, e1_t, 0.0) * e0_ref[h, ns, pl.ds(i_key, 1), :]
                        gates[k] = term if gates[k] is None else gates[k] + term
                for k in range(GATE_ROWS):
                    r0 = (g0 + k) * PEER_N_KEYS + jh * GATE_HALF
                    coef_ref[ns, r0:r0 + GATE_HALF, :] = (gates[k] * act_ref[ns, r0:r0 + GATE_HALF, :]).astype(BF16)

    def lane_block(ns, carry):
        gate_block(ns)
        return carry

    lax.fori_loop(0, nsb, lane_block, 0)
    coef = jnp.concatenate([coef_ref[ns] for ns in range(nsb)], axis=1)
    acc_ref[...] += jnp.dot(vt_ref[0], coef, preferred_element_type=F32)

    @pl.when(ei == pl.num_programs(1) - 1)
    def _():
        x2 = x1_ref[...] + acc_ref[...].T
        y_ref[...] = (x2 * _rms_scale(x2)) * ng_ref[...]


def _peer(h2, x1, u_bf, vt_bf, cut, e0, s1, e1, norm_g, tn):
    n, d = h2.shape
    ne = u_bf.shape[0]
    te = vt_bf.shape[2]
    nk = PEER_N_KEYS
    tok = pl.BlockSpec((tn, d), lambda i, e: (i, 0))
    tiled = pl.BlockSpec((PEER_HEADS, tn // LANES, nk, LANES), lambda i, e: (0, i, 0, 0))
    return pl.pallas_call(
        functools.partial(_peer_kernel, te=te),
        grid=(n // tn, ne // te),
        in_specs=[tok, tok,
                  pl.BlockSpec((te, d), lambda i, e: (e, 0)),
                  pl.BlockSpec((1, d, te), lambda i, e: (e, 0, 0)),
                  tiled, tiled, tiled, tiled,
                  pl.BlockSpec((1, d), lambda i, e: (0, 0))],
        out_specs=tok,
        out_shape=jax.ShapeDtypeStruct((n, d), F32),
        scratch_shapes=[pltpu.VMEM((d, tn), F32), pltpu.VMEM((tn // LANES, te, LANES), F32),
                        pltpu.VMEM((tn // LANES, te, LANES), BF16)],
        compiler_params=_params(("parallel", "arbitrary")),
        name="peer_mix",
    )(h2, x1, u_bf, vt_bf, cut, e0, s1, e1, norm_g.reshape(1, d))


def _tile(n, pref):
    return math.gcd(n, pref)


def _layer(x, mk, mv, bsz, t, weights, conv_state=None, gdn_conv_state=None, gdn_s0=None):
    (norm_mix_g, w_main, w_ab, conv_w, gdn_conv_w, a_log, dt_bias, gdn_norm_g, wc, wg, wm, wo, norm_ffn_g,
     wq, keys, u_bf, vt_bf, final_norm_g) = weights
    n = bsz * t
    sample = conv_state is not None
    p, pab = _inproj(x, norm_mix_g, w_main, w_ab, _tile(n, 1024))
    rows = _tile(n, 256) if sample else _tile(t, 512)
    ya, q, k, v, gb, u_tail = _prep(p, pab, conv_w, gdn_conv_w, a_log, dt_bias, rows, max(t // rows, 1),
                                    conv_state, gdn_conv_state)
    chunk = _tile(t, 64)
    if sample:
        yb, s_new = _gdn(q, k, v, p, gb, gdn_norm_g, bsz, t, chunk, 4, gdn_s0)
    else:
        yb, s_new = _gdn_stacked(q, k, v, p, gb, gdn_norm_g, bsz, t, chunk, _tile(bsz, 2))
    ym = _mem_attn(p, mk, mv, bsz, t, _tile(t, 512), _tile(bsz, 4) if sample else 1)
    x1, h2 = _merge(x, ya, yb, ym, p, wc, wg, wm, wo, norm_ffn_g, _tile(n, 256))
    tn = _tile(n, 512)
    cut, e0, s1, e1 = _route(h2, wq, keys, tn)
    y = _peer(h2, x1, u_bf, vt_bf, cut, e0, s1, e1, final_norm_g, tn)
    return y, u_tail, p, s_new


def kernel(x_prompt, x_sample, mem_prompt, cache_mem_k, cache_mem_v, state_conv, state_gdn_conv, state_gdn, norm_mix_g, w_in, conv_w, gdn_conv_w, gdn_a_log, gdn_dt_bias, gdn_norm_g, mem_norm_g, w_mem_kv, w_br_conv, w_br_gdn, w_br_mem, w_out, norm_ffn_g, peer_w_q, peer_sub_keys, peer_u, peer_v, final_norm_g):
    depth = w_in.shape[0]
    assert depth == 1, "single-layer step"
    d = D_MODEL
    bp, tp, _ = x_prompt.shape
    bs, ts, _ = x_sample.shape
    assert ts == SAMPLE_PAD_T // 2
    m_tok = mem_prompt.shape[1]
    mq = MEM_HEADS * MEM_HEAD_DIM
    l = 0

    w = w_in[l]
    ga0 = 7 * d
    w_main = jnp.concatenate([w[:, :ga0], w[:, ga0 + 2 * GDN_HEADS:]], axis=1).astype(BF16)
    w_ab = jnp.pad(w[:, ga0:ga0 + 2 * GDN_HEADS], ((0, 0), (0, SMALL_COLS - 2 * GDN_HEADS))).astype(BF16)
    weights = (norm_mix_g[l], w_main, w_ab, conv_w[l], gdn_conv_w[l], gdn_a_log[l], gdn_dt_bias[l], gdn_norm_g[l],
               w_br_conv[l].astype(BF16), w_br_gdn[l].astype(BF16), w_br_mem[l].astype(BF16), w_out[l].astype(BF16),
               norm_ffn_g[l], peer_w_q[l].astype(BF16), peer_sub_keys[l].astype(BF16),
               peer_u[l].astype(BF16),
               peer_v[l].astype(BF16).reshape(-1, PEER_TILE, d).transpose(0, 2, 1), final_norm_g)

    kv_shape = (bp, m_tok, mq)
    mk_p, mv_p = _mem_kv(mem_prompt.reshape(bp * m_tok, d), mem_norm_g[l], w_mem_kv[l].astype(BF16),
                         _tile(bp * m_tok, 256))
    y_p, u_tail_p, p_p, s_p = _layer(x_prompt.reshape(bp * tp, d), mk_p.reshape(kv_shape), mv_p.reshape(kv_shape),
                                     bp, tp, weights)
    rows_p = _tile(tp, 512)
    cb_p = u_tail_p.reshape(bp, tp // rows_p, SUBLANES, d)[:, -1, SUBLANES - (CONV_K - 1):, :]
    gb_p = p_p.reshape(bp, tp, -1)[:, tp - (GDN_CONV_K - 1):, COL_GQ * d:(COL_GV + 1) * d].astype(F32)

    pad_t = SAMPLE_PAD_T
    lead = pad_t - ts
    x_s = jnp.pad(x_sample, ((0, 0), (lead, 0), (0, 0))).reshape(bs * pad_t, d)
    cst = jnp.pad(state_conv[l], ((0, 0), (lead - (CONV_K - 1), ts), (0, 0))).reshape(bs * pad_t, d)
    gst = jnp.pad(state_gdn_conv[l], ((0, 0), (lead - (GDN_CONV_K - 1), ts), (0, 0))).reshape(bs * pad_t, 3 * d)
    y_s, u_s, p_s, s_s = _layer(x_s, cache_mem_k.reshape(bs, -1, mq), cache_mem_v.reshape(bs, -1, mq),
                                bs, pad_t, weights, cst, gst, state_gdn.reshape(bs, GDN_HEADS, GDN_DK, GDN_DV))
    y_s = y_s.reshape(bs, pad_t, d)[:, lead:, :]
    cb_s = u_s.reshape(bs, pad_t, d)[:, pad_t - (CONV_K - 1):, :]
    gb_s = p_s.reshape(bs, pad_t, -1)[:, pad_t - (GDN_CONV_K - 1):, COL_GQ * d:(COL_GV + 1) * d].astype(F32)

    mshape = (depth, bp, m_tok, MEM_HEADS, MEM_HEAD_DIM)
    return (y_p.reshape(bp, tp, d), y_s,
            mk_p.reshape(mshape), mv_p.reshape(mshape),
            cb_p[None], gb_p[None], s_p[None],
            cb_s[None], gb_s[None], s_s[None])
```

```python
import functools
import math

import jax
import jax.numpy as jnp
from jax import lax
from jax.experimental import pallas as pl
from jax.experimental.pallas import tpu as pltpu

F32 = jnp.float32
BF16 = jnp.bfloat16

EPS = 1e-6
D_MODEL = 1024
CONV_K = 3
GDN_HEADS = 8
GDN_DK = 128
GDN_DV = 128
GDN_CONV_K = 4
MEM_HEADS = 4
MEM_HEAD_DIM = 256
PEER_HEADS = 8
PEER_N_KEYS = 128
PEER_D_HALF = 128
PEER_TOPK = 16
N_BRANCHES = 3

COL_AU, COL_AB, COL_AC, COL_GQ, COL_GK, COL_GV, COL_GZ, COL_MQ, COL_BG = 0, 1, 2, 3, 4, 5, 6, 7, 8
N_MAIN_BLOCKS = 11
SMALL_COLS = 128
SUBLANES = 8
LANES = 128
SAMPLE_PAD_T = 8
VMEM_LIMIT = 56 * 1024 * 1024


def _params(sem):
    return pltpu.CompilerParams(dimension_semantics=sem, vmem_limit_bytes=VMEM_LIMIT)


def _dot(a, b):
    return jnp.dot(a.astype(BF16), b.astype(BF16), preferred_element_type=F32)


def _dot_nt(a, b):
    return lax.dot_general(a.astype(BF16), b.astype(BF16), (((1,), (1,)), ((), ())),
                           preferred_element_type=F32)


def _rms_scale(x):
    return lax.rsqrt(jnp.mean(x * x, axis=-1, keepdims=True) + EPS)


def _inproj_kernel(x_ref, g_ref, w_ref, wab_ref, p_ref, pab_ref, h_ref):
    @pl.when(pl.program_id(1) == 0)
    def _():
        x = x_ref[...]
        h = ((x * _rms_scale(x)) * g_ref[...]).astype(BF16)
        h_ref[...] = h
        pab_ref[...] = jnp.dot(h, wab_ref[...], preferred_element_type=F32)

    p_ref[...] = jnp.dot(h_ref[...], w_ref[...], preferred_element_type=F32).astype(p_ref.dtype)


def _inproj(x, g, w_main, w_ab, tm):
    n, d = x.shape
    nblk = w_main.shape[1] // d
    return pl.pallas_call(
        _inproj_kernel,
        grid=(n // tm, nblk),
        in_specs=[
            pl.BlockSpec((tm, d), lambda i, j: (i, 0)),
            pl.BlockSpec((1, d), lambda i, j: (0, 0)),
            pl.BlockSpec((d, d), lambda i, j: (0, j)),
            pl.BlockSpec((d, SMALL_COLS), lambda i, j: (0, 0)),
        ],
        out_specs=[
            pl.BlockSpec((tm, d), lambda i, j: (i, j)),
            pl.BlockSpec((tm, SMALL_COLS), lambda i, j: (i, 0)),
        ],
        out_shape=[
            jax.ShapeDtypeStruct((n, nblk * d), BF16),
            jax.ShapeDtypeStruct((n, SMALL_COLS), F32),
        ],
        scratch_shapes=[pltpu.VMEM((tm, d), BF16)],
        compiler_params=_params(("parallel", "arbitrary")),
        name="inproj",
    )(x, g.reshape(1, d), w_main, w_ab)


def _mem_kv_kernel(x_ref, g_ref, w_ref, k_ref, v_ref):
    x = x_ref[...]
    h = ((x * _rms_scale(x)) * g_ref[...]).astype(BF16)
    kv = jnp.dot(h, w_ref[...], preferred_element_type=F32)
    half = kv.shape[1] // 2
    k_ref[...] = kv[:, :half]
    v_ref[...] = kv[:, half:]


def _mem_kv(x, g, w, tm):
    n, d = x.shape
    c = w.shape[1]
    out = jax.ShapeDtypeStruct((n, c // 2), F32)
    ospec = pl.BlockSpec((tm, c // 2), lambda i: (i, 0))
    return pl.pallas_call(
        _mem_kv_kernel,
        grid=(n // tm,),
        in_specs=[
            pl.BlockSpec((tm, d), lambda i: (i, 0)),
            pl.BlockSpec((1, d), lambda i: (0, 0)),
            pl.BlockSpec((d, c), lambda i: (0, 0)),
        ],
        out_specs=[ospec, ospec],
        out_shape=[out, out],
        compiler_params=_params(("parallel",)),
        name="mem_kv",
    )(x, g.reshape(1, d), w)


def _causal_conv(prev, cur, w_ref, k):
    full = jnp.concatenate([prev, cur], axis=0)
    y = cur * w_ref[k - 1:k, :]
    for s in range(1, k):
        y = y + pltpu.roll(full, s, axis=0)[prev.shape[0]:, :] * w_ref[k - 1 - s:k - s, :]
    return y


def _f32(ref):
    return ref[...].astype(F32)


def _prep_kernel(*refs, rows, tiles_per_seq, sample):
    if sample:
        (au, ab, ac, gq, gk, gv, pab, au_p, ac_p, gq_p, gk_p, gv_p, cw, gw, alog, dtb, cst, gst,
         ya_o, q_o, k_o, v_o, gb_o, ut_o) = refs
    else:
        (au, ab, ac, gq, gk, gv, pab, au_p, ac_p, gq_p, gk_p, gv_p, cw, gw, alog, dtb,
         ya_o, q_o, k_o, v_o, gb_o, ut_o) = refs
    d = D_MODEL
    first = (pl.program_id(0) % tiles_per_seq) == 0
    keep = jnp.where(first, 0.0, 1.0).astype(F32)

    u = _f32(ac) * _f32(au)
    u_prev = (_f32(ac_p) * _f32(au_p)) * keep
    if sample:
        real = (lax.broadcasted_iota(jnp.int32, (rows, 1), 0) % SAMPLE_PAD_T) >= (SAMPLE_PAD_T // 2)
        u = jnp.where(real, u, cst[...])
    ya_o[...] = (_f32(ab) * _causal_conv(u_prev, u, cw, CONV_K)).astype(BF16)
    ut_o[...] = u[rows - ut_o.shape[0]:, :]

    outs = (q_o, k_o, v_o)
    for idx, (cur_ref, prev_ref) in enumerate(((gq, gq_p), (gk, gk_p), (gv, gv_p))):
        cur = _f32(cur_ref)
        if sample:
            cur = jnp.where(real, cur, gst[:, idx * d:(idx + 1) * d])
        prev = _f32(prev_ref) * keep
        wslab = gw.at[:, idx * d:(idx + 1) * d]
        y = _causal_conv(prev, cur, wslab, GDN_CONV_K)
        y = y * jax.nn.sigmoid(y)
        if idx < 2:
            scale = GDN_DK ** -0.5 if idx == 0 else 1.0
            for h in range(GDN_HEADS):
                yh = y[:, h * GDN_DK:(h + 1) * GDN_DK]
                r = lax.rsqrt(jnp.sum(yh * yh, axis=-1, keepdims=True) + EPS)
                outs[idx][:, h * GDN_DK:(h + 1) * GDN_DK] = yh * (r * scale)
        else:
            outs[idx][...] = y

    pa = pab[...]
    lane = lax.broadcasted_iota(jnp.int32, pa.shape, 1)
    z = pa + dtb[...]
    softplus = jnp.maximum(z, 0.0) + jnp.log1p(jnp.exp(-jnp.abs(z)))
    g_log = -jnp.exp(alog[...]) * softplus
    gb = jnp.where(lane < GDN_HEADS, g_log, jnp.where(lane < 2 * GDN_HEADS, jax.nn.sigmoid(pa), 0.0))
    if sample:
        gb = jnp.where(real, gb, 0.0)
    gb_o[...] = gb


def _prep(p, pab, conv_w, gdn_conv_w, a_log, dt_bias, rows, tiles_per_seq, conv_state=None, gdn_state=None):
    n = p.shape[0]
    d = D_MODEL
    sample = conv_state is not None
    nt = n // rows
    prev_rows = 2 * SUBLANES
    rpb = rows // prev_rows

    def cur(c):
        return pl.BlockSpec((rows, d), lambda i, c=c: (i, c))

    def prev(c):
        return pl.BlockSpec((prev_rows, d), lambda i, c=c: (jnp.maximum(i * rpb - 1, 0), c))

    alog128 = jnp.zeros((1, SMALL_COLS), F32).at[0, :GDN_HEADS].set(a_log)
    dtb128 = jnp.zeros((1, SMALL_COLS), F32).at[0, :GDN_HEADS].set(dt_bias)
    in_specs = [cur(COL_AU), cur(COL_AB), cur(COL_AC), cur(COL_GQ), cur(COL_GK), cur(COL_GV),
                pl.BlockSpec((rows, SMALL_COLS), lambda i: (i, 0)),
                prev(COL_AU), prev(COL_AC), prev(COL_GQ), prev(COL_GK), prev(COL_GV),
                pl.BlockSpec((CONV_K, d), lambda i: (0, 0)),
                pl.BlockSpec((GDN_CONV_K, 3 * d), lambda i: (0, 0)),
                pl.BlockSpec((1, SMALL_COLS), lambda i: (0, 0)),
                pl.BlockSpec((1, SMALL_COLS), lambda i: (0, 0))]
    args = [p, p, p, p, p, p, pab, p, p, p, p, p, conv_w, gdn_conv_w, alog128, dtb128]
    if sample:
        in_specs += [pl.BlockSpec((rows, d), lambda i: (i, 0)), pl.BlockSpec((rows, 3 * d), lambda i: (i, 0))]
        args += [conv_state, gdn_state]
    tail = rows if sample else SUBLANES
    out_specs = [pl.BlockSpec((rows, d), lambda i: (i, 0))] * 4 + [
        pl.BlockSpec((rows, SMALL_COLS), lambda i: (i, 0)),
        pl.BlockSpec((tail, d), lambda i: (i, 0))]
    out_shape = [jax.ShapeDtypeStruct((n, d), BF16)] + [jax.ShapeDtypeStruct((n, d), F32)] * 3 + [
        jax.ShapeDtypeStruct((n, SMALL_COLS), F32),
        jax.ShapeDtypeStruct((nt * tail, d), F32)]
    return pl.pallas_call(
        functools.partial(_prep_kernel, rows=rows, tiles_per_seq=tiles_per_seq, sample=sample),
        grid=(nt,),
        in_specs=in_specs,
        out_specs=out_specs,
        out_shape=out_shape,
        compiler_params=_params(("parallel",)),
        name="prep",
    )(*args)


def _chunk_cumsum(g, c):
    row = lax.broadcasted_iota(jnp.int32, g.shape, 0)
    sh = 1
    while sh < c:
        g = g + jnp.where(row >= sh, pltpu.roll(g, sh, axis=0), 0.0)
        sh *= 2
    return g


def _gdn_stacked_kernel(q_ref, k_ref, v_ref, z_ref, gb_ref, ng_ref, yb_ref, sout_ref, s_scr, *, c, grp, nb):
    ci = pl.program_id(1)

    @pl.when(ci == 0)
    def _():
        s_scr[...] = jnp.zeros_like(s_scr)

    r = grp * c
    ng = ng_ref[...]
    seqs = []
    for s in range(nb):
        gb = gb_ref[s]
        gc = _chunk_cumsum(gb, c)
        g_last = gc[c - 1:c, :]
        seqs.append(dict(gb=gb, gc=gc, gct=gc.T, e_g=jnp.exp(gc), e_kd=jnp.exp(g_last - gc), e_last=jnp.exp(g_last)))

    ri = lax.broadcasted_iota(jnp.int32, (r, r), 0)
    cj = lax.broadcasted_iota(jnp.int32, (r, r), 1)
    same = (ri // c) == (cj // c)
    incl = same & (ri >= cj)
    strict = same & (ri > cj)
    head_block = (lax.broadcasted_iota(jnp.int32, (r, grp * GDN_DK), 0) // c
                  == lax.broadcasted_iota(jnp.int32, (r, grp * GDN_DK), 1) // GDN_DK)

    def block_layout(x):
        return jnp.where(head_block, jnp.concatenate([x] * grp, axis=1), 0.0)

    units = []
    for s in range(nb):
        for g0 in range(0, GDN_HEADS, grp):
            heads = list(range(g0, g0 + grp))
            sq = seqs[s]

            def stack(ref, s=s, heads=heads):
                return jnp.concatenate([ref[s, :, h * GDN_DK:(h + 1) * GDN_DK] for h in heads], axis=0)

            def col(a, off=0, heads=heads):
                return jnp.concatenate([a[:, off + h:off + h + 1] for h in heads], axis=0)

            qs, ks, vs = stack(q_ref), stack(k_ref), stack(v_ref)
            beta = col(sq["gb"], GDN_HEADS)
            kb = ks * beta
            units.append(dict(s=s, heads=heads, qs=qs, ks=ks, kb=kb, vb=vs * beta,
                              gcol=col(sq["gc"]), egcol=col(sq["e_g"]), ekdcol=col(sq["e_kd"]),
                              grow=jnp.concatenate([sq["gct"][h:h + 1, :] for h in heads], axis=1),
                              rows=slice((s * GDN_HEADS + g0) * GDN_DK, (s * GDN_HEADS + g0 + grp) * GDN_DK)))

    for u in units:
        u["aq"] = _dot_nt(jnp.concatenate([u["kb"], u["qs"]], axis=0), u["ks"])
    for u in units:
        decay = jnp.exp(jnp.where(incl, u["gcol"] - u["grow"], -jnp.inf))
        u["pw"] = -jnp.where(strict, u["aq"][:r] * decay, 0.0)
        u["attn"] = jnp.where(incl, u["aq"][r:] * decay, 0.0)
        u["x"] = jnp.concatenate([u["vb"], u["kb"] * u["egcol"]], axis=1)
    span = 1
    while span < c:
        span *= 2
        for u in units:
            if span < c:
                both = _dot(u["pw"], jnp.concatenate([u["pw"], u["x"]], axis=1))
                u["pw"] = both[:, :r]
                u["x"] = u["x"] + both[:, r:]
            else:
                u["x"] = u["x"] + _dot(u["pw"], u["x"])
    for u in units:
        u_val, w_k = u["x"][:, :GDN_DV], u["x"][:, GDN_DV:]
        u["s_old"] = s_scr[u["rows"], :]
        ws = _dot(jnp.concatenate([block_layout(w_k), block_layout(u["qs"] * u["egcol"])], axis=0), u["s_old"])
        u["v_new"] = u_val - ws[:r]
        u["o"] = ws[r:]
    for u in units:
        e_last = seqs[u["s"]]["e_last"]
        o = u["o"] + _dot(u["attn"], u["v_new"])
        e_rows = jnp.concatenate([jnp.broadcast_to(e_last[:, h:h + 1], (GDN_DK, 1)) for h in u["heads"]], axis=0)
        kd_t = block_layout(u["ks"] * u["ekdcol"]).T
        s_scr[u["rows"], :] = u["s_old"] * e_rows + _dot(kd_t, u["v_new"])
        for i, h in enumerate(u["heads"]):
            sl = slice(h * GDN_DK, (h + 1) * GDN_DK)
            oh = o[i * c:(i + 1) * c, :]
            zh = z_ref[u["s"], :, sl].astype(F32)
            yb_ref[u["s"], :, sl] = (((oh * _rms_scale(oh)) * ng) * (zh * jax.nn.sigmoid(zh))).astype(BF16)

    @pl.when(ci == pl.num_programs(1) - 1)
    def _():
        sout_ref[...] = s_scr[...]


def _gdn_stacked(q, k, v, p, gb, norm_g, bsz, t, c, nb):
    d = D_MODEL
    nc = t // c
    grp = min(GDN_HEADS, 256 // c)
    s_rows = nb * GDN_HEADS * GDN_DK

    def tok(col=0):
        return pl.BlockSpec((nb, c, d), lambda b, i, col=col: (b, i, col))

    yb, s_new = pl.pallas_call(
        functools.partial(_gdn_stacked_kernel, c=c, grp=grp, nb=nb),
        grid=(bsz // nb, nc),
        in_specs=[tok(), tok(), tok(), tok(COL_GZ),
                  pl.BlockSpec((nb, c, SMALL_COLS), lambda b, i: (b, i, 0)),
                  pl.BlockSpec((1, GDN_DV), lambda b, i: (0, 0))],
        out_specs=[tok(), pl.BlockSpec((s_rows, GDN_DV), lambda b, i: (b, 0))],
        out_shape=[jax.ShapeDtypeStruct((bsz, t, d), BF16),
                   jax.ShapeDtypeStruct((bsz * GDN_HEADS * GDN_DK, GDN_DV), F32)],
        scratch_shapes=[pltpu.VMEM((s_rows, GDN_DV), F32)],
        compiler_params=_params(("parallel", "arbitrary")),
        name="gdn_scan_prompt",
    )(*[a.reshape(bsz, t, -1) for a in (q, k, v, p, gb)], norm_g.reshape(1, GDN_DV))
    return yb.reshape(bsz * t, d), s_new.reshape(bsz, GDN_HEADS, GDN_DK, GDN_DV)


def _gdn_kernel(*refs, c, nb, has_s0):
    if has_s0:
        q_ref, k_ref, v_ref, z_ref, gb_ref, ng_ref, s0_ref, yb_ref, sout_ref, s_scr = refs
    else:
        q_ref, k_ref, v_ref, z_ref, gb_ref, ng_ref, yb_ref, sout_ref, s_scr = refs
    ci = pl.program_id(1)

    @pl.when(ci == 0)
    def _():
        if has_s0:
            s_scr[...] = s0_ref[...]
        else:
            s_scr[...] = jnp.zeros_like(s_scr)

    ri = lax.broadcasted_iota(jnp.int32, (c, c), 0)
    cj = lax.broadcasted_iota(jnp.int32, (c, c), 1)
    incl = ri >= cj
    strict = ri > cj
    ng = ng_ref[...]
    row = lax.broadcasted_iota(jnp.int32, (c, SMALL_COLS), 0)
    chains = [(s, h) for s in range(nb) for h in range(GDN_HEADS)]

    seq = []
    for s in range(nb):
        gb = gb_ref[s]
        gc = gb
        sh = 1
        while sh < c:
            gc = gc + jnp.where(row >= sh, pltpu.roll(gc, sh, axis=0), 0.0)
            sh *= 2
        g_last = gc[c - 1:c, :]
        seq.append(dict(gb=gb, gc=gc, gct=gc.T, e_g=jnp.exp(gc), e_kd=jnp.exp(g_last - gc), e_last=jnp.exp(g_last)))

    st = []
    for s, h in chains:
        sl = slice(h * GDN_DK, (h + 1) * GDN_DK)
        sq = seq[s]
        qh, kh, vh = q_ref[s, :, sl], k_ref[s, :, sl], v_ref[s, :, sl]
        beta = sq["gb"][:, GDN_HEADS + h:GDN_HEADS + h + 1]
        kb = kh * beta
        aq = _dot_nt(jnp.concatenate([kb, qh], axis=0), kh)
        st.append(dict(sl=sl, qh=qh, kh=kh, kb=kb, vb=vh * beta, aq=aq))

    for (s, h), e in zip(chains, st):
        sq = seq[s]
        decay = jnp.exp(jnp.where(incl, sq["gc"][:, h:h + 1] - sq["gct"][h:h + 1, :], -jnp.inf))
        e["neg"] = -jnp.where(strict, e["aq"][:c] * decay, 0.0)
        e["attn"] = jnp.where(incl, e["aq"][c:] * decay, 0.0)
        e["x"] = jnp.concatenate([e["vb"], e["kb"] * sq["e_g"][:, h:h + 1]], axis=1)
        e["pw"] = e["neg"]

    span = 1
    while span < c:
        span *= 2
        for e in st:
            if span < c:
                both = _dot(e["pw"], jnp.concatenate([e["pw"], e["x"]], axis=1))
                e["pw"] = both[:, :c]
                e["x"] = e["x"] + both[:, c:]
            else:
                e["x"] = e["x"] + _dot(e["pw"], e["x"])

    for (s, h), e in zip(chains, st):
        sq = seq[s]
        u_val, w_k = e["x"][:, :GDN_DV], e["x"][:, GDN_DV:]
        q_dec = e["qh"] * sq["e_g"][:, h:h + 1]
        e["s_old"] = s_scr[s, h]
        ws = _dot(jnp.concatenate([w_k, q_dec], axis=0), e["s_old"])
        e["v_new"] = u_val - ws[:c]
        e["o_inter"] = ws[c:]

    for (s, h), e in zip(chains, st):
        sq = seq[s]
        o = e["o_inter"] + _dot(e["attn"], e["v_new"])
        k_dec = e["kh"] * sq["e_kd"][:, h:h + 1]
        s_scr[s, h] = e["s_old"] * sq["e_last"][:, h:h + 1] + _dot(k_dec.T, e["v_new"])
        zh = z_ref[s, :, e["sl"]].astype(F32)
        yb_ref[s, :, e["sl"]] = (((o * _rms_scale(o)) * ng) * (zh * jax.nn.sigmoid(zh))).astype(BF16)

    @pl.when(ci == pl.num_programs(1) - 1)
    def _():
        sout_ref[...] = s_scr[...]


def _gdn(q, k, v, p, gb, norm_g, bsz, t, c, nb, s0=None):
    d = D_MODEL
    nc = t // c
    has_s0 = s0 is not None

    def tok(col=0):
        return pl.BlockSpec((nb, c, d), lambda b, i, col=col: (b, i, col))

    in_specs = [tok(), tok(), tok(), tok(COL_GZ),
                pl.BlockSpec((nb, c, SMALL_COLS), lambda b, i: (b, i, 0)),
                pl.BlockSpec((1, GDN_DV), lambda b, i: (0, 0))]
    args = [a.reshape(bsz, t, -1) for a in (q, k, v, p, gb)] + [norm_g.reshape(1, GDN_DV)]
    s_spec = pl.BlockSpec((nb, GDN_HEADS, GDN_DK, GDN_DV), lambda b, i: (b, 0, 0, 0))
    if has_s0:
        in_specs.append(s_spec)
        args.append(s0)
    yb, s_new = pl.pallas_call(
        functools.partial(_gdn_kernel, c=c, nb=nb, has_s0=has_s0),
        grid=(bsz // nb, nc),
        in_specs=in_specs,
        out_specs=[tok(), s_spec],
        out_shape=[jax.ShapeDtypeStruct((bsz, t, d), BF16),
                   jax.ShapeDtypeStruct((bsz, GDN_HEADS, GDN_DK, GDN_DV), F32)],
        scratch_shapes=[pltpu.VMEM((nb, GDN_HEADS, GDN_DK, GDN_DV), F32)],
        compiler_params=_params(("parallel", "arbitrary")),
        name="gdn_scan",
    )(*args)
    return yb.reshape(bsz * t, d), s_new


def _mem_attn_kernel(q_ref, k_ref, v_ref, o_ref, *, nb, tq):
    scale = MEM_HEAD_DIM ** -0.5
    pairs = [(s, h) for s in range(nb) for h in range(MEM_HEADS)]
    span = lambda s, h: (slice(s * tq, (s + 1) * tq), slice(h * MEM_HEAD_DIM, (h + 1) * MEM_HEAD_DIM))
    scores = [_dot_nt(q_ref[span(s, h)], k_ref[s, :, span(s, h)[1]]) * scale for s, h in pairs]
    probs = []
    for sc in scores:
        e = jnp.exp(sc - jnp.max(sc, axis=-1, keepdims=True))
        probs.append(e / jnp.sum(e, axis=-1, keepdims=True))
    for (s, h), prob in zip(pairs, probs):
        o_ref[span(s, h)] = _dot(prob, v_ref[s, :, span(s, h)[1]]).astype(BF16)


def _mem_attn(p, mk, mv, bsz, t, tq, nb):
    d = D_MODEL
    m = mk.shape[1]
    nq = t // tq
    assert nb == 1 or nq == 1
    kv_spec = pl.BlockSpec((nb, m, d), lambda b, i: (b, 0, 0))
    return pl.pallas_call(
        functools.partial(_mem_attn_kernel, nb=nb, tq=tq),
        grid=(bsz // nb, nq),
        in_specs=[pl.BlockSpec((nb * tq, d), lambda b, i: (b * nq + i, COL_MQ)), kv_spec, kv_spec],
        out_specs=pl.BlockSpec((nb * tq, d), lambda b, i: (b * nq + i, 0)),
        out_shape=jax.ShapeDtypeStruct((bsz * t, d), BF16),
        compiler_params=_params(("parallel", "arbitrary")),
        name="mem_attn",
    )(p, mk, mv)


def _merge_kernel(x_ref, ya_ref, yb_ref, ym_ref, g0_ref, g1_ref, g2_ref, wc_ref, wg_ref, wm_ref, wo_ref,
                  ng_ref, x1_ref, h2_ref):
    d = functools.partial(jnp.dot, preferred_element_type=F32)
    merged = (jax.nn.sigmoid(_f32(g0_ref)) * d(ya_ref[...], wc_ref[...])
              + jax.nn.sigmoid(_f32(g1_ref)) * d(yb_ref[...], wg_ref[...])
              + jax.nn.sigmoid(_f32(g2_ref)) * d(ym_ref[...], wm_ref[...]))
    x1 = x_ref[...] + d(merged.astype(BF16), wo_ref[...])
    x1_ref[...] = x1
    h2_ref[...] = ((x1 * _rms_scale(x1)) * ng_ref[...]).astype(BF16)


def _merge(x, ya, yb, ym, p, wc, wg, wm, wo, norm_g, tm):
    n, d = x.shape
    tok = pl.BlockSpec((tm, d), lambda i: (i, 0))
    wsp = pl.BlockSpec((d, d), lambda i: (0, 0))

    def gate(j):
        return pl.BlockSpec((tm, d), lambda i, j=j: (i, COL_BG + j))

    return pl.pallas_call(
        _merge_kernel,
        grid=(n // tm,),
        in_specs=[tok, tok, tok, tok, gate(0), gate(1), gate(2), wsp, wsp, wsp, wsp,
                  pl.BlockSpec((1, d), lambda i: (0, 0))],
        out_specs=[tok, tok],
        out_shape=[jax.ShapeDtypeStruct((n, d), F32), jax.ShapeDtypeStruct((n, d), BF16)],
        compiler_params=_params(("parallel",)),
        name="merge",
    )(x, ya, yb, ym, p, p, p, wc, wg, wm, wo, norm_g.reshape(1, d))


_CAND_CELLS = tuple((a, b) for a in range(PEER_TOPK) for b in range(PEER_TOPK) if (a + 1) * (b + 1) <= PEER_TOPK)


def _top_values(x, count):
    vals = []
    cur = x
    for r in range(count):
        m = jnp.max(cur, axis=0, keepdims=True)
        vals.append(m)
        if r + 1 < count:
            cur = jnp.where(cur == m, -jnp.inf, cur)
    return vals


def _top_ranked(x, count):
    n_rows = x.shape[0]
    iota = lax.broadcasted_iota(jnp.int32, x.shape, 0).astype(F32)
    vals = []
    cur = x
    rank = jnp.full(x.shape, float(count), F32)
    for r in range(count):
        m = jnp.max(cur, axis=0, keepdims=True)
        first = jnp.min(jnp.where(cur == m, iota, float(n_rows)), axis=0, keepdims=True)
        one = iota == first
        rank = jnp.where(one, float(r), rank)
        cur = jnp.where(one, -jnp.inf, cur)
        vals.append(m)
    return vals, rank


def _count_rows(mask):
    return jnp.sum(jnp.where(mask, 1.0, 0.0), axis=0, keepdims=True)


def _route_kernel(h_ref, wq_ref, keys_ref, cut_ref, e0_ref, s1_ref, e1_ref, q_scr, tie_ref):
    tn = h_ref.shape[0]
    nsb = tn // LANES
    q = jnp.dot(h_ref[...], wq_ref[...], preferred_element_type=F32).astype(BF16)

    def store(h, cut, e0, s1, e1):
        for ns in range(nsb):
            cols = slice(ns * LANES, (ns + 1) * LANES)
            cut_ref[h, ns] = cut[:, cols]
            e0_ref[h, ns] = e0[:, cols]
            s1_ref[h, ns] = s1[:, cols]
            e1_ref[h, ns] = e1[:, cols]

    for h in range(PEER_HEADS):
        sc, top = [], []
        tied = jnp.zeros((1, tn), F32)
        for part in range(2):
            col = (2 * h + part) * PEER_D_HALF
            q_scr[2 * h + part] = q[:, col:col + PEER_D_HALF]
            s_t = lax.dot_general(keys_ref[part, h], q[:, col:col + PEER_D_HALF],
                                  (((1,), (1,)), ((), ())), preferred_element_type=F32)
            sc.append(s_t)
            top.append(_top_values(s_t, PEER_TOPK))
            tied = jnp.maximum(tied, jnp.abs(_count_rows(s_t >= top[part][-1]) - PEER_TOPK))
        cand = jnp.concatenate([top[0][a] + top[1][b] for a, b in _CAND_CELLS], axis=0)
        best = _top_values(cand, PEER_TOPK)
        c_max, thr = best[0], best[-1]
        z = jnp.exp(best[0] - c_max)
        for r in range(1, PEER_TOPK):
            z = z + jnp.exp(best[r] - c_max)
        top1 = jnp.concatenate(top[1], axis=0)
        n_sel = sum(_count_rows(top[0][a] + top1 >= thr) for a in range(PEER_TOPK))
        tied = jnp.maximum(tied, jnp.abs(n_sel - PEER_TOPK))
        cut = jnp.full(sc[0].shape, jnp.inf, F32)
        for b in range(PEER_TOPK):
            cut = jnp.where(sc[0] + top[1][b] >= thr, top[1][b], cut)
        store(h, cut, jnp.exp(sc[0] - top[0][0]) / z, sc[1], jnp.exp(sc[1] - top[1][0]))
        for ns in range(nsb):
            tie_ref[h * nsb + ns] = jnp.max(tied[:, ns * LANES:(ns + 1) * LANES])

    def exact_block(idx, carry):
        @pl.when(tie_ref[idx] > 0.0)
        def _():
            h = idx // nsb
            ns = idx % nsb
            rows = pl.ds(pl.multiple_of(ns * LANES, LANES), LANES)
            sc, top, rank = [], [], []
            for part in range(2):
                s_t = lax.dot_general(keys_ref[part, h], q_scr[2 * h + part, rows, :],
                                      (((1,), (1,)), ((), ())), preferred_element_type=F32)
                vals, rk = _top_ranked(s_t, PEER_TOPK)
                sc.append(s_t)
                top.append(vals)
                rank.append(rk)
            top1 = jnp.concatenate(top[1], axis=0)
            cand = jnp.concatenate([top[0][a] + top1 for a in range(PEER_TOPK)], axis=0)
            best, pos = _top_ranked(cand, PEER_TOPK)
            z = jnp.exp(best[0] - best[0])
            for r in range(1, PEER_TOPK):
                z = z + jnp.exp(best[r] - best[0])
            cnt = jnp.zeros_like(sc[0])
            for a in range(PEER_TOPK):
                cnt_a = _count_rows(pos[a * PEER_TOPK:(a + 1) * PEER_TOPK] < PEER_TOPK)
                cnt = jnp.where(rank[0] == float(a), cnt_a, cnt)
            cut_ref[h, ns] = jnp.where(cnt > 0.0, 1.0 - cnt, jnp.inf)
            e0_ref[h, ns] = jnp.exp(sc[0] - top[0][0]) / z
            s1_ref[h, ns] = -rank[1]
            e1_ref[h, ns] = jnp.exp(sc[1] - top[1][0])
        return carry

    lax.fori_loop(0, PEER_HEADS * nsb, exact_block, 0)


def _route(h2, wq, keys, tn):
    n, d = h2.shape
    nk = PEER_N_KEYS
    tiled = pl.BlockSpec((PEER_HEADS, tn // LANES, nk, LANES), lambda i: (0, i, 0, 0))
    tiled_shape = jax.ShapeDtypeStruct((PEER_HEADS, n // LANES, nk, LANES), F32)
    return pl.pallas_call(
        _route_kernel,
        grid=(n // tn,),
        in_specs=[pl.BlockSpec((tn, d), lambda i: (i, 0)),
                  pl.BlockSpec(wq.shape, lambda i: (0, 0)),
                  pl.BlockSpec(keys.shape, lambda i: (0, 0, 0, 0))],
        out_specs=[tiled, tiled, tiled, tiled],
        out_shape=[tiled_shape, tiled_shape, tiled_shape, tiled_shape],
        scratch_shapes=[pltpu.VMEM((2 * PEER_HEADS, tn, PEER_D_HALF), BF16),
                        pltpu.SMEM((PEER_HEADS * (tn // LANES),), F32)],
        compiler_params=_params(("parallel",)),
        name="peer_route",
    )(h2, wq, keys)


def _gelu(x):
    return 0.5 * x * (1.0 + lax.erf(x * (2.0 ** -0.5)))


PEER_TILE = 1024
GATE_HALF = 32
GATE_ROWS = 8


def _peer_kernel(h_ref, x1_ref, u_ref, vt_ref, cut_ref, e0_ref, s1_ref, e1_ref, ng_ref,
                 y_ref, acc_ref, act_ref, coef_ref, *, te):
    ei = pl.program_id(1)
    tn = h_ref.shape[0]
    groups = te // PEER_N_KEYS

    @pl.when(ei == 0)
    def _():
        acc_ref[...] = jnp.zeros_like(acc_ref)

    nsb = tn // LANES
    act = _gelu(lax.dot_general(u_ref[...], h_ref[...], (((1,), (1,)), ((), ())),
                                preferred_element_type=F32))
    for ns in range(nsb):
        act_ref[ns] = act[:, ns * LANES:(ns + 1) * LANES]

    def gate_block(ns):
        for jh in range(PEER_N_KEYS // GATE_HALF):
            jrows = slice(jh * GATE_HALF, (jh + 1) * GATE_HALF)
            for g0 in range(0, groups, GATE_ROWS):
                gates = [None] * GATE_ROWS
                for h in range(PEER_HEADS):
                    s1_t = s1_ref[h, ns, jrows, :]
                    e1_t = e1_ref[h, ns, jrows, :]
                    for k in range(GATE_ROWS):
                        i_key = ei * groups + g0 + k
                        sel = s1_t >= cut_ref[h, ns, pl.ds(i_key, 1), :]
                        term = jnp.where(sel, e1_t, 0.0) * e0_ref[h, ns, pl.ds(i_key, 1), :]
                        gates[k] = term if gates[k] is None else gates[k] + term
                for k in range(GATE_ROWS):
                    r0 = (g0 + k) * PEER_N_KEYS + jh * GATE_HALF
                    coef_ref[ns, r0:r0 + GATE_HALF, :] = (gates[k] * act_ref[ns, r0:r0 + GATE_HALF, :]).astype(BF16)

    def lane_block(ns, carry):
        gate_block(ns)
        return carry

    lax.fori_loop(0, nsb, lane_block, 0)
    coef = jnp.concatenate([coef_ref[ns] for ns in range(nsb)], axis=1)
    acc_ref[...] += jnp.dot(vt_ref[0], coef, preferred_element_type=F32)

    @pl.when(ei == pl.num_programs(1) - 1)
    def _():
        x2 = x1_ref[...] + acc_ref[...].T
        y_ref[...] = (x2 * _rms_scale(x2)) * ng_ref[...]


def _peer(h2, x1, u_bf, vt_bf, cut, e0, s1, e1, norm_g, tn):
    n, d = h2.shape
    ne = u_bf.shape[0]
    te = vt_bf.shape[2]
    nk = PEER_N_KEYS
    tok = pl.BlockSpec((tn, d), lambda i, e: (i, 0))
    tiled = pl.BlockSpec((PEER_HEADS, tn // LANES, nk, LANES), lambda i, e: (0, i, 0, 0))
    return pl.pallas_call(
        functools.partial(_peer_kernel, te=te),
        grid=(n // tn, ne // te),
        in_specs=[tok, tok,
                  pl.BlockSpec((te, d), lambda i, e: (e, 0)),
                  pl.BlockSpec((1, d, te), lambda i, e: (e, 0, 0)),
                  tiled, tiled, tiled, tiled,
                  pl.BlockSpec((1, d), lambda i, e: (0, 0))],
        out_specs=tok,
        out_shape=jax.ShapeDtypeStruct((n, d), F32),
        scratch_shapes=[pltpu.VMEM((d, tn), F32), pltpu.VMEM((tn // LANES, te, LANES), F32),
                        pltpu.VMEM((tn // LANES, te, LANES), BF16)],
        compiler_params=_params(("parallel", "arbitrary")),
        name="peer_mix",
    )(h2, x1, u_bf, vt_bf, cut, e0, s1, e1, norm_g.reshape(1, d))


def _tile(n, pref):
    return math.gcd(n, pref)


def _layer(x, mk, mv, bsz, t, weights, conv_state=None, gdn_conv_state=None, gdn_s0=None):
    (norm_mix_g, w_main, w_ab, conv_w, gdn_conv_w, a_log, dt_bias, gdn_norm_g, wc, wg, wm, wo, norm_ffn_g,
     wq, keys, u_bf, vt_bf, final_norm_g) = weights
    n = bsz * t
    sample = conv_state is not None
    p, pab = _inproj(x, norm_mix_g, w_main, w_ab, _tile(n, 1024))
    rows = _tile(n, 256) if sample else _tile(t, 512)
    ya, q, k, v, gb, u_tail = _prep(p, pab, conv_w, gdn_conv_w, a_log, dt_bias, rows, max(t // rows, 1),
                                    conv_state, gdn_conv_state)
    chunk = _tile(t, 64)
    if sample:
        yb, s_new = _gdn(q, k, v, p, gb, gdn_norm_g, bsz, t, chunk, 4, gdn_s0)
    else:
        yb, s_new = _gdn_stacked(q, k, v, p, gb, gdn_norm_g, bsz, t, chunk, _tile(bsz, 2))
    ym = _mem_attn(p, mk, mv, bsz, t, _tile(t, 512), _tile(bsz, 4) if sample else 1)
    x1, h2 = _merge(x, ya, yb, ym, p, wc, wg, wm, wo, norm_ffn_g, _tile(n, 256))
    tn = _tile(n, 512)
    cut, e0, s1, e1 = _route(h2, wq, keys, tn)
    y = _peer(h2, x1, u_bf, vt_bf, cut, e0, s1, e1, final_norm_g, tn)
    return y, u_tail, p, s_new


def kernel(x_prompt, x_sample, mem_prompt, cache_mem_k, cache_mem_v, state_conv, state_gdn_conv, state_gdn, norm_mix_g, w_in, conv_w, gdn_conv_w, gdn_a_log, gdn_dt_bias, gdn_norm_g, mem_norm_g, w_mem_kv, w_br_conv, w_br_gdn, w_br_mem, w_out, norm_ffn_g, peer_w_q, peer_sub_keys, peer_u, peer_v, final_norm_g):
    depth = w_in.shape[0]
    assert depth == 1, "single-layer step"
    d = D_MODEL
    bp, tp, _ = x_prompt.shape
    bs, ts, _ = x_sample.shape
    assert ts == SAMPLE_PAD_T // 2
    m_tok = mem_prompt.shape[1]
    mq = MEM_HEADS * MEM_HEAD_DIM
    l = 0

    w = w_in[l]
    ga0 = 7 * d
    w_main = jnp.concatenate([w[:, :ga0], w[:, ga0 + 2 * GDN_HEADS:]], axis=1).astype(BF16)
    w_ab = jnp.pad(w[:, ga0:ga0 + 2 * GDN_HEADS], ((0, 0), (0, SMALL_COLS - 2 * GDN_HEADS))).astype(BF16)
    weights = (norm_mix_g[l], w_main, w_ab, conv_w[l], gdn_conv_w[l], gdn_a_log[l], gdn_dt_bias[l], gdn_norm_g[l],
               w_br_conv[l].astype(BF16), w_br_gdn[l].astype(BF16), w_br_mem[l].astype(BF16), w_out[l].astype(BF16),
               norm_ffn_g[l], peer_w_q[l].astype(BF16), peer_sub_keys[l].astype(BF16),
               peer_u[l].astype(BF16),
               peer_v[l].astype(BF16).reshape(-1, PEER_TILE, d).transpose(0, 2, 1), final_norm_g)

    kv_shape = (bp, m_tok, mq)
    mk_p, mv_p = _mem_kv(mem_prompt.reshape(bp * m_tok, d), mem_norm_g[l], w_mem_kv[l].astype(BF16),
                         _tile(bp * m_tok, 256))
    y_p, u_tail_p, p_p, s_p = _layer(x_prompt.reshape(bp * tp, d), mk_p.reshape(kv_shape), mv_p.reshape(kv_shape),
                                     bp, tp, weights)
    rows_p = _tile(tp, 512)
    cb_p = u_tail_p.reshape(bp, tp // rows_p, SUBLANES, d)[:, -1, SUBLANES - (CONV_K - 1):, :]
    gb_p = p_p.reshape(bp, tp, -1)[:, tp - (GDN_CONV_K - 1):, COL_GQ * d:(COL_GV + 1) * d].astype(F32)

    pad_t = SAMPLE_PAD_T
    lead = pad_t - ts
    x_s = jnp.pad(x_sample, ((0, 0), (lead, 0), (0, 0))).reshape(bs * pad_t, d)
    cst = jnp.pad(state_conv[l], ((0, 0), (lead - (CONV_K - 1), ts), (0, 0))).reshape(bs * pad_t, d)
    gst = jnp.pad(state_gdn_conv[l], ((0, 0), (lead - (GDN_CONV_K - 1), ts), (0, 0))).reshape(bs * pad_t, 3 * d)
    y_s, u_s, p_s, s_s = _layer(x_s, cache_mem_k.reshape(bs, -1, mq), cache_mem_v.reshape(bs, -1, mq),
                                bs, pad_t, weights, cst, gst, state_gdn.reshape(bs, GDN_HEADS, GDN_DK, GDN_DV))
    y_s = y_s.reshape(bs, pad_t, d)[:, lead:, :]
    cb_s = u_s.reshape(bs, pad_t, d)[:, pad_t - (CONV_K - 1):, :]
    gb_s = p_s.reshape(bs, pad_t, -1)[:, pad_t - (GDN_CONV_K - 1):, COL_GQ * d:(COL_GV + 1) * d].astype(F32)

    mshape = (depth, bp, m_tok, MEM_HEADS, MEM_HEAD_DIM)
    return (y_p.reshape(bp, tp, d), y_s,
            mk_p.reshape(mshape), mv_p.reshape(mshape),
            cb_p[None], gb_p[None], s_p[None],
            cb_s[None], gb_s[None], s_s[None])
```

```python
import functools
import math

import jax
import jax.numpy as jnp
from jax import lax
from jax.experimental import pallas as pl
from jax.experimental.pallas import tpu as pltpu

F32 = jnp.float32
BF16 = jnp.bfloat16

EPS = 1e-6
D_MODEL = 1024
CONV_K = 3
GDN_HEADS = 8
GDN_DK = 128
GDN_DV = 128
GDN_CONV_K = 4
MEM_HEADS = 4
MEM_HEAD_DIM = 256
PEER_HEADS = 8
PEER_N_KEYS = 128
PEER_D_HALF = 128
PEER_TOPK = 16
N_BRANCHES = 3

COL_AU, COL_AB, COL_AC, COL_GQ, COL_GK, COL_GV, COL_GZ, COL_MQ, COL_BG = 0, 1, 2, 3, 4, 5, 6, 7, 8
N_MAIN_BLOCKS = 11
SMALL_COLS = 128
SUBLANES = 8
LANES = 128
SAMPLE_PAD_T = 8
VMEM_LIMIT = 56 * 1024 * 1024


def _params(sem):
    return pltpu.CompilerParams(dimension_semantics=sem, vmem_limit_bytes=VMEM_LIMIT)


def _dot(a, b):
    return jnp.dot(a.astype(BF16), b.astype(BF16), preferred_element_type=F32)


def _dot_nt(a, b):
    return lax.dot_general(a.astype(BF16), b.astype(BF16), (((1,), (1,)), ((), ())),
                           preferred_element_type=F32)


def _rms_scale(x):
    return lax.rsqrt(jnp.mean(x * x, axis=-1, keepdims=True) + EPS)


def _inproj_kernel(x_ref, g_ref, w_ref, wab_ref, p_ref, pab_ref, h_ref):
    @pl.when(pl.program_id(1) == 0)
    def _():
        x = x_ref[...]
        h = ((x * _rms_scale(x)) * g_ref[...]).astype(BF16)
        h_ref[...] = h
        pab_ref[...] = jnp.dot(h, wab_ref[...], preferred_element_type=F32)

    p_ref[...] = jnp.dot(h_ref[...], w_ref[...], preferred_element_type=F32).astype(p_ref.dtype)


def _inproj(x, g, w_main, w_ab, tm):
    n, d = x.shape
    nblk = w_main.shape[1] // d
    return pl.pallas_call(
        _inproj_kernel,
        grid=(n // tm, nblk),
        in_specs=[
            pl.BlockSpec((tm, d), lambda i, j: (i, 0)),
            pl.BlockSpec((1, d), lambda i, j: (0, 0)),
            pl.BlockSpec((d, d), lambda i, j: (0, j)),
            pl.BlockSpec((d, SMALL_COLS), lambda i, j: (0, 0)),
        ],
        out_specs=[
            pl.BlockSpec((tm, d), lambda i, j: (i, j)),
            pl.BlockSpec((tm, SMALL_COLS), lambda i, j: (i, 0)),
        ],
        out_shape=[
            jax.ShapeDtypeStruct((n, nblk * d), BF16),
            jax.ShapeDtypeStruct((n, SMALL_COLS), F32),
        ],
        scratch_shapes=[pltpu.VMEM((tm, d), BF16)],
        compiler_params=_params(("parallel", "arbitrary")),
        name="inproj",
    )(x, g.reshape(1, d), w_main, w_ab)


def _mem_kv_kernel(x_ref, g_ref, w_ref, k_ref, v_ref):
    x = x_ref[...]
    h = ((x * _rms_scale(x)) * g_ref[...]).astype(BF16)
    kv = jnp.dot(h, w_ref[...], preferred_element_type=F32)
    half = kv.shape[1] // 2
    k_ref[...] = kv[:, :half]
    v_ref[...] = kv[:, half:]


def _mem_kv(x, g, w, tm):
    n, d = x.shape
    c = w.shape[1]
    out = jax.ShapeDtypeStruct((n, c // 2), F32)
    ospec = pl.BlockSpec((tm, c // 2), lambda i: (i, 0))
    return pl.pallas_call(
        _mem_kv_kernel,
        grid=(n // tm,),
        in_specs=[
            pl.BlockSpec((tm, d), lambda i: (i, 0)),
            pl.BlockSpec((1, d), lambda i: (0, 0)),
            pl.BlockSpec((d, c), lambda i: (0, 0)),
        ],
        out_specs=[ospec, ospec],
        out_shape=[out, out],
        compiler_params=_params(("parallel",)),
        name="mem_kv",
    )(x, g.reshape(1, d), w)


def _causal_conv(prev, cur, w_ref, k):
    full = jnp.concatenate([prev, cur], axis=0)
    y = cur * w_ref[k - 1:k, :]
    for s in range(1, k):
        y = y + pltpu.roll(full, s, axis=0)[prev.shape[0]:, :] * w_ref[k - 1 - s:k - s, :]
    return y


def _f32(ref):
    return ref[...].astype(F32)


def _prep_kernel(*refs, rows, tiles_per_seq, sample):
    if sample:
        (au, ab, ac, gq, gk, gv, pab, au_p, ac_p, gq_p, gk_p, gv_p, cw, gw, alog, dtb, cst, gst,
         ya_o, q_o, k_o, v_o, gb_o, ut_o) = refs
    else:
        (au, ab, ac, gq, gk, gv, pab, au_p, ac_p, gq_p, gk_p, gv_p, cw, gw, alog, dtb,
         ya_o, q_o, k_o, v_o, gb_o, ut_o) = refs
    d = D_MODEL
    first = (pl.program_id(0) % tiles_per_seq) == 0
    keep = jnp.where(first, 0.0, 1.0).astype(F32)

    u = _f32(ac) * _f32(au)
    u_prev = (_f32(ac_p) * _f32(au_p)) * keep
    if sample:
        real = (lax.broadcasted_iota(jnp.int32, (rows, 1), 0) % SAMPLE_PAD_T) >= (SAMPLE_PAD_T // 2)
        u = jnp.where(real, u, cst[...])
    ya_o[...] = (_f32(ab) * _causal_conv(u_prev, u, cw, CONV_K)).astype(BF16)
    ut_o[...] = u[rows - ut_o.shape[0]:, :]

    outs = (q_o, k_o, v_o)
    for idx, (cur_ref, prev_ref) in enumerate(((gq, gq_p), (gk, gk_p), (gv, gv_p))):
        cur = _f32(cur_ref)
        if sample:
            cur = jnp.where(real, cur, gst[:, idx * d:(idx + 1) * d])
        prev = _f32(prev_ref) * keep
        wslab = gw.at[:, idx * d:(idx + 1) * d]
        y = _causal_conv(prev, cur, wslab, GDN_CONV_K)
        y = y * jax.nn.sigmoid(y)
        if idx < 2:
            scale = GDN_DK ** -0.5 if idx == 0 else 1.0
            for h in range(GDN_HEADS):
                yh = y[:, h * GDN_DK:(h + 1) * GDN_DK]
                r = lax.rsqrt(jnp.sum(yh * yh, axis=-1, keepdims=True) + EPS)
                outs[idx][:, h * GDN_DK:(h + 1) * GDN_DK] = yh * (r * scale)
        else:
            outs[idx][...] = y

    pa = pab[...]
    lane = lax.broadcasted_iota(jnp.int32, pa.shape, 1)
    z = pa + dtb[...]
    softplus = jnp.maximum(z, 0.0) + jnp.log1p(jnp.exp(-jnp.abs(z)))
    g_log = -jnp.exp(alog[...]) * softplus
    gb = jnp.where(lane < GDN_HEADS, g_log, jnp.where(lane < 2 * GDN_HEADS, jax.nn.sigmoid(pa), 0.0))
    if sample:
        gb = jnp.where(real, gb, 0.0)
    gb_o[...] = gb


def _prep(p, pab, conv_w, gdn_conv_w, a_log, dt_bias, rows, tiles_per_seq, conv_state=None, gdn_state=None):
    n = p.shape[0]
    d = D_MODEL
    sample = conv_state is not None
    nt = n // rows
    prev_rows = 2 * SUBLANES
    rpb = rows // prev_rows

    def cur(c):
        return pl.BlockSpec((rows, d), lambda i, c=c: (i, c))

    def prev(c):
        return pl.BlockSpec((prev_rows, d), lambda i, c=c: (jnp.maximum(i * rpb - 1, 0), c))

    alog128 = jnp.zeros((1, SMALL_COLS), F32).at[0, :GDN_HEADS].set(a_log)
    dtb128 = jnp.zeros((1, SMALL_COLS), F32).at[0, :GDN_HEADS].set(dt_bias)
    in_specs = [cur(COL_AU), cur(COL_AB), cur(COL_AC), cur(COL_GQ), cur(COL_GK), cur(COL_GV),
                pl.BlockSpec((rows, SMALL_COLS), lambda i: (i, 0)),
                prev(COL_AU), prev(COL_AC), prev(COL_GQ), prev(COL_GK), prev(COL_GV),
                pl.BlockSpec((CONV_K, d), lambda i: (0, 0)),
                pl.BlockSpec((GDN_CONV_K, 3 * d), lambda i: (0, 0)),
                pl.BlockSpec((1, SMALL_COLS), lambda i: (0, 0)),
                pl.BlockSpec((1, SMALL_COLS), lambda i: (0, 0))]
    args = [p, p, p, p, p, p, pab, p, p, p, p, p, conv_w, gdn_conv_w, alog128, dtb128]
    if sample:
        in_specs += [pl.BlockSpec((rows, d), lambda i: (i, 0)), pl.BlockSpec((rows, 3 * d), lambda i: (i, 0))]
        args += [conv_state, gdn_state]
    tail = rows if sample else SUBLANES
    out_specs = [pl.BlockSpec((rows, d), lambda i: (i, 0))] * 4 + [
        pl.BlockSpec((rows, SMALL_COLS), lambda i: (i, 0)),
        pl.BlockSpec((tail, d), lambda i: (i, 0))]
    out_shape = [jax.ShapeDtypeStruct((n, d), BF16)] + [jax.ShapeDtypeStruct((n, d), F32)] * 3 + [
        jax.ShapeDtypeStruct((n, SMALL_COLS), F32),
        jax.ShapeDtypeStruct((nt * tail, d), F32)]
    return pl.pallas_call(
        functools.partial(_prep_kernel, rows=rows, tiles_per_seq=tiles_per_seq, sample=sample),
        grid=(nt,),
        in_specs=in_specs,
        out_specs=out_specs,
        out_shape=out_shape,
        compiler_params=_params(("parallel",)),
        name="prep",
    )(*args)


def _chunk_cumsum(g, c):
    row = lax.broadcasted_iota(jnp.int32, g.shape, 0)
    sh = 1
    while sh < c:
        g = g + jnp.where(row >= sh, pltpu.roll(g, sh, axis=0), 0.0)
        sh *= 2
    return g


def _gdn_stacked_kernel(q_ref, k_ref, v_ref, z_ref, gb_ref, ng_ref, yb_ref, sout_ref, s_scr, *, c, grp, nb):
    ci = pl.program_id(1)

    @pl.when(ci == 0)
    def _():
        s_scr[...] = jnp.zeros_like(s_scr)

    r = grp * c
    ng = ng_ref[...]
    seqs = []
    for s in range(nb):
        gb = gb_ref[s]
        gc = _chunk_cumsum(gb, c)
        g_last = gc[c - 1:c, :]
        seqs.append(dict(gb=gb, gc=gc, gct=gc.T, e_g=jnp.exp(gc), e_kd=jnp.exp(g_last - gc), e_last=jnp.exp(g_last)))

    ri = lax.broadcasted_iota(jnp.int32, (r, r), 0)
    cj = lax.broadcasted_iota(jnp.int32, (r, r), 1)
    same = (ri // c) == (cj // c)
    incl = same & (ri >= cj)
    strict = same & (ri > cj)
    head_block = (lax.broadcasted_iota(jnp.int32, (r, grp * GDN_DK), 0) // c
                  == lax.broadcasted_iota(jnp.int32, (r, grp * GDN_DK), 1) // GDN_DK)

    def block_layout(x):
        return jnp.where(head_block, jnp.concatenate([x] * grp, axis=1), 0.0)

    units = []
    for s in range(nb):
        for g0 in range(0, GDN_HEADS, grp):
            heads = list(range(g0, g0 + grp))
            sq = seqs[s]

            def stack(ref, s=s, heads=heads):
                return jnp.concatenate([ref[s, :, h * GDN_DK:(h + 1) * GDN_DK] for h in heads], axis=0)

            def col(a, off=0, heads=heads):
                return jnp.concatenate([a[:, off + h:off + h + 1] for h in heads], axis=0)

            qs, ks, vs = stack(q_ref), stack(k_ref), stack(v_ref)
            beta = col(sq["gb"], GDN_HEADS)
            kb = ks * beta
            units.append(dict(s=s, heads=heads, qs=qs, ks=ks, kb=kb, vb=vs * beta,
                              gcol=col(sq["gc"]), egcol=col(sq["e_g"]), ekdcol=col(sq["e_kd"]),
                              grow=jnp.concatenate([sq["gct"][h:h + 1, :] for h in heads], axis=1),
                              rows=slice((s * GDN_HEADS + g0) * GDN_DK, (s * GDN_HEADS + g0 + grp) * GDN_DK)))

    for u in units:
        u["aq"] = _dot_nt(jnp.concatenate([u["kb"], u["qs"]], axis=0), u["ks"])
    for u in units:
        decay = jnp.exp(jnp.where(incl, u["gcol"] - u["grow"], -jnp.inf))
        u["pw"] = -jnp.where(strict, u["aq"][:r] * decay, 0.0)
        u["attn"] = jnp.where(incl, u["aq"][r:] * decay, 0.0)
        u["x"] = jnp.concatenate([u["vb"], u["kb"] * u["egcol"]], axis=1)
    span = 1
    while span < c:
        span *= 2
        for u in units:
            if span < c:
                both = _dot(u["pw"], jnp.concatenate([u["pw"], u["x"]], axis=1))
                u["pw"] = both[:, :r]
                u["x"] = u["x"] + both[:, r:]
            else:
                u["x"] = u["x"] + _dot(u["pw"], u["x"])
    for u in units:
        u_val, w_k = u["x"][:, :GDN_DV], u["x"][:, GDN_DV:]
        u["s_old"] = s_scr[u["rows"], :]
        ws = _dot(jnp.concatenate([block_layout(w_k), block_layout(u["qs"] * u["egcol"])], axis=0), u["s_old"])
        u["v_new"] = u_val - ws[:r]
        u["o"] = ws[r:]
    for u in units:
        e_last = seqs[u["s"]]["e_last"]
        o = u["o"] + _dot(u["attn"], u["v_new"])
        e_rows = jnp.concatenate([jnp.broadcast_to(e_last[:, h:h + 1], (GDN_DK, 1)) for h in u["heads"]], axis=0)
        kd_t = block_layout(u["ks"] * u["ekdcol"]).T
        s_scr[u["rows"], :] = u["s_old"] * e_rows + _dot(kd_t, u["v_new"])
        for i, h in enumerate(u["heads"]):
            sl = slice(h * GDN_DK, (h + 1) * GDN_DK)
            oh = o[i * c:(i + 1) * c, :]
            zh = z_ref[u["s"], :, sl].astype(F32)
            yb_ref[u["s"], :, sl] = (((oh * _rms_scale(oh)) * ng) * (zh * jax.nn.sigmoid(zh))).astype(BF16)

    @pl.when(ci == pl.num_programs(1) - 1)
    def _():
        sout_ref[...] = s_scr[...]


def _gdn_stacked(q, k, v, p, gb, norm_g, bsz, t, c, nb):
    d = D_MODEL
    nc = t // c
    grp = min(GDN_HEADS, 256 // c)
    s_rows = nb * GDN_HEADS * GDN_DK

    def tok(col=0):
        return pl.BlockSpec((nb, c, d), lambda b, i, col=col: (b, i, col))

    yb, s_new = pl.pallas_call(
        functools.partial(_gdn_stacked_kernel, c=c, grp=grp, nb=nb),
        grid=(bsz // nb, nc),
        in_specs=[tok(), tok(), tok(), tok(COL_GZ),
                  pl.BlockSpec((nb, c, SMALL_COLS), lambda b, i: (b, i, 0)),
                  pl.BlockSpec((1, GDN_DV), lambda b, i: (0, 0))],
        out_specs=[tok(), pl.BlockSpec((s_rows, GDN_DV), lambda b, i: (b, 0))],
        out_shape=[jax.ShapeDtypeStruct((bsz, t, d), BF16),
                   jax.ShapeDtypeStruct((bsz * GDN_HEADS * GDN_DK, GDN_DV), F32)],
        scratch_shapes=[pltpu.VMEM((s_rows, GDN_DV), F32)],
        compiler_params=_params(("parallel", "arbitrary")),
        name="gdn_scan_prompt",
    )(*[a.reshape(bsz, t, -1) for a in (q, k, v, p, gb)], norm_g.reshape(1, GDN_DV))
    return yb.reshape(bsz * t, d), s_new.reshape(bsz, GDN_HEADS, GDN_DK, GDN_DV)


def _gdn_kernel(*refs, c, nb, has_s0):
    if has_s0:
        q_ref, k_ref, v_ref, z_ref, gb_ref, ng_ref, s0_ref, yb_ref, sout_ref, s_scr = refs
    else:
        q_ref, k_ref, v_ref, z_ref, gb_ref, ng_ref, yb_ref, sout_ref, s_scr = refs
    ci = pl.program_id(1)

    @pl.when(ci == 0)
    def _():
        if has_s0:
            s_scr[...] = s0_ref[...]
        else:
            s_scr[...] = jnp.zeros_like(s_scr)

    ri = lax.broadcasted_iota(jnp.int32, (c, c), 0)
    cj = lax.broadcasted_iota(jnp.int32, (c, c), 1)
    incl = ri >= cj
    strict = ri > cj
    ng = ng_ref[...]
    row = lax.broadcasted_iota(jnp.int32, (c, SMALL_COLS), 0)
    chains = [(s, h) for s in range(nb) for h in range(GDN_HEADS)]

    seq = []
    for s in range(nb):
        gb = gb_ref[s]
        gc = gb
        sh = 1
        while sh < c:
            gc = gc + jnp.where(row >= sh, pltpu.roll(gc, sh, axis=0), 0.0)
            sh *= 2
        g_last = gc[c - 1:c, :]
        seq.append(dict(gb=gb, gc=gc, gct=gc.T, e_g=jnp.exp(gc), e_kd=jnp.exp(g_last - gc), e_last=jnp.exp(g_last)))

    st = []
    for s, h in chains:
        sl = slice(h * GDN_DK, (h + 1) * GDN_DK)
        sq = seq[s]
        qh, kh, vh = q_ref[s, :, sl], k_ref[s, :, sl], v_ref[s, :, sl]
        beta = sq["gb"][:, GDN_HEADS + h:GDN_HEADS + h + 1]
        kb = kh * beta
        aq = _dot_nt(jnp.concatenate([kb, qh], axis=0), kh)
        st.append(dict(sl=sl, qh=qh, kh=kh, kb=kb, vb=vh * beta, aq=aq))

    for (s, h), e in zip(chains, st):
        sq = seq[s]
        decay = jnp.exp(jnp.where(incl, sq["gc"][:, h:h + 1] - sq["gct"][h:h + 1, :], -jnp.inf))
        e["neg"] = -jnp.where(strict, e["aq"][:c] * decay, 0.0)
        e["attn"] = jnp.where(incl, e["aq"][c:] * decay, 0.0)
        e["x"] = jnp.concatenate([e["vb"], e["kb"] * sq["e_g"][:, h:h + 1]], axis=1)
        e["pw"] = e["neg"]

    span = 1
    while span < c:
        span *= 2
        for e in st:
            if span < c:
                both = _dot(e["pw"], jnp.concatenate([e["pw"], e["x"]], axis=1))
                e["pw"] = both[:, :c]
                e["x"] = e["x"] + both[:, c:]
            else:
                e["x"] = e["x"] + _dot(e["pw"], e["x"])

    for (s, h), e in zip(chains, st):
        sq = seq[s]
        u_val, w_k = e["x"][:, :GDN_DV], e["x"][:, GDN_DV:]
        q_dec = e["qh"] * sq["e_g"][:, h:h + 1]
        e["s_old"] = s_scr[s, h]
        ws = _dot(jnp.concatenate([w_k, q_dec], axis=0), e["s_old"])
        e["v_new"] = u_val - ws[:c]
        e["o_inter"] = ws[c:]

    for (s, h), e in zip(chains, st):
        sq = seq[s]
        o = e["o_inter"] + _dot(e["attn"], e["v_new"])
        k_dec = e["kh"] * sq["e_kd"][:, h:h + 1]
        s_scr[s, h] = e["s_old"] * sq["e_last"][:, h:h + 1] + _dot(k_dec.T, e["v_new"])
        zh = z_ref[s, :, e["sl"]].astype(F32)
        yb_ref[s, :, e["sl"]] = (((o * _rms_scale(o)) * ng) * (zh * jax.nn.sigmoid(zh))).astype(BF16)

    @pl.when(ci == pl.num_programs(1) - 1)
    def _():
        sout_ref[...] = s_scr[...]


def _gdn(q, k, v, p, gb, norm_g, bsz, t, c, nb, s0=None):
    d = D_MODEL
    nc = t // c
    has_s0 = s0 is not None

    def tok(col=0):
        return pl.BlockSpec((nb, c, d), lambda b, i, col=col: (b, i, col))

    in_specs = [tok(), tok(), tok(), tok(COL_GZ),
                pl.BlockSpec((nb, c, SMALL_COLS), lambda b, i: (b, i, 0)),
                pl.BlockSpec((1, GDN_DV), lambda b, i: (0, 0))]
    args = [a.reshape(bsz, t, -1) for a in (q, k, v, p, gb)] + [norm_g.reshape(1, GDN_DV)]
    s_spec = pl.BlockSpec((nb, GDN_HEADS, GDN_DK, GDN_DV), lambda b, i: (b, 0, 0, 0))
    if has_s0:
        in_specs.append(s_spec)
        args.append(s0)
    yb, s_new = pl.pallas_call(
        functools.partial(_gdn_kernel, c=c, nb=nb, has_s0=has_s0),
        grid=(bsz // nb, nc),
        in_specs=in_specs,
        out_specs=[tok(), s_spec],
        out_shape=[jax.ShapeDtypeStruct((bsz, t, d), BF16),
                   jax.ShapeDtypeStruct((bsz, GDN_HEADS, GDN_DK, GDN_DV), F32)],
        scratch_shapes=[pltpu.VMEM((nb, GDN_HEADS, GDN_DK, GDN_DV), F32)],
        compiler_params=_params(("parallel", "arbitrary")),
        name="gdn_scan",
    )(*args)
    return yb.reshape(bsz * t, d), s_new


def _mem_attn_kernel(q_ref, k_ref, v_ref, o_ref, *, nb, tq):
    scale = MEM_HEAD_DIM ** -0.5
    pairs = [(s, h) for s in range(nb) for h in range(MEM_HEADS)]
    span = lambda s, h: (slice(s * tq, (s + 1) * tq), slice(h * MEM_HEAD_DIM, (h + 1) * MEM_HEAD_DIM))
    scores = [_dot_nt(q_ref[span(s, h)], k_ref[s, :, span(s, h)[1]]) * scale for s, h in pairs]
    probs = []
    for sc in scores:
        e = jnp.exp(sc - jnp.max(sc, axis=-1, keepdims=True))
        probs.append(e / jnp.sum(e, axis=-1, keepdims=True))
    for (s, h), prob in zip(pairs, probs):
        o_ref[span(s, h)] = _dot(prob, v_ref[s, :, span(s, h)[1]]).astype(BF16)


def _mem_attn(p, mk, mv, bsz, t, tq, nb):
    d = D_MODEL
    m = mk.shape[1]
    nq = t // tq
    assert nb == 1 or nq == 1
    kv_spec = pl.BlockSpec((nb, m, d), lambda b, i: (b, 0, 0))
    return pl.pallas_call(
        functools.partial(_mem_attn_kernel, nb=nb, tq=tq),
        grid=(bsz // nb, nq),
        in_specs=[pl.BlockSpec((nb * tq, d), lambda b, i: (b * nq + i, COL_MQ)), kv_spec, kv_spec],
        out_specs=pl.BlockSpec((nb * tq, d), lambda b, i: (b * nq + i, 0)),
        out_shape=jax.ShapeDtypeStruct((bsz * t, d), BF16),
        compiler_params=_params(("parallel", "arbitrary")),
        name="mem_attn",
    )(p, mk, mv)


def _merge_kernel(x_ref, ya_ref, yb_ref, ym_ref, g0_ref, g1_ref, g2_ref, wc_ref, wg_ref, wm_ref, wo_ref,
                  ng_ref, x1_ref, h2_ref):
    d = functools.partial(jnp.dot, preferred_element_type=F32)
    merged = (jax.nn.sigmoid(_f32(g0_ref)) * d(ya_ref[...], wc_ref[...])
              + jax.nn.sigmoid(_f32(g1_ref)) * d(yb_ref[...], wg_ref[...])
              + jax.nn.sigmoid(_f32(g2_ref)) * d(ym_ref[...], wm_ref[...]))
    x1 = x_ref[...] + d(merged.astype(BF16), wo_ref[...])
    x1_ref[...] = x1
    h2_ref[...] = ((x1 * _rms_scale(x1)) * ng_ref[...]).astype(BF16)


def _merge(x, ya, yb, ym, p, wc, wg, wm, wo, norm_g, tm):
    n, d = x.shape
    tok = pl.BlockSpec((tm, d), lambda i: (i, 0))
    wsp = pl.BlockSpec((d, d), lambda i: (0, 0))

    def gate(j):
        return pl.BlockSpec((tm, d), lambda i, j=j: (i, COL_BG + j))

    return pl.pallas_call(
        _merge_kernel,
        grid=(n // tm,),
        in_specs=[tok, tok, tok, tok, gate(0), gate(1), gate(2), wsp, wsp, wsp, wsp,
                  pl.BlockSpec((1, d), lambda i: (0, 0))],
        out_specs=[tok, tok],
        out_shape=[jax.ShapeDtypeStruct((n, d), F32), jax.ShapeDtypeStruct((n, d), BF16)],
        compiler_params=_params(("parallel",)),
        name="merge",
    )(x, ya, yb, ym, p, p, p, wc, wg, wm, wo, norm_g.reshape(1, d))


_CAND_CELLS = tuple((a, b) for a in range(PEER_TOPK) for b in range(PEER_TOPK) if (a + 1) * (b + 1) <= PEER_TOPK)


def _top_values(x, count):
    vals = []
    cur = x
    for r in range(count):
        m = jnp.max(cur, axis=0, keepdims=True)
        vals.append(m)
        if r + 1 < count:
            cur = jnp.where(cur == m, -jnp.inf, cur)
    return vals


def _top_ranked(x, count):
    n_rows = x.shape[0]
    iota = lax.broadcasted_iota(jnp.int32, x.shape, 0).astype(F32)
    vals = []
    cur = x
    rank = jnp.full(x.shape, float(count), F32)
    for r in range(count):
        m = jnp.max(cur, axis=0, keepdims=True)
        first = jnp.min(jnp.where(cur == m, iota, float(n_rows)), axis=0, keepdims=True)
        one = iota == first
        rank = jnp.where(one, float(r), rank)
        cur = jnp.where(one, -jnp.inf, cur)
        vals.append(m)
    return vals, rank


def _count_rows(mask):
    return jnp.sum(jnp.where(mask, 1.0, 0.0), axis=0, keepdims=True)


def _route_kernel(h_ref, wq_ref, keys_ref, cut_ref, e0_ref, s1_ref, e1_ref, q_scr, tie_ref):
    tn = h_ref.shape[0]
    nsb = tn // LANES
    q = jnp.dot(h_ref[...], wq_ref[...], preferred_element_type=F32).astype(BF16)

    def store(h, cut, e0, s1, e1):
        for ns in range(nsb):
            cols = slice(ns * LANES, (ns + 1) * LANES)
            cut_ref[h, ns] = cut[:, cols]
            e0_ref[h, ns] = e0[:, cols]
            s1_ref[h, ns] = s1[:, cols]
            e1_ref[h, ns] = e1[:, cols]

    for h in range(PEER_HEADS):
        sc, top = [], []
        tied = jnp.zeros((1, tn), F32)
        for part in range(2):
            col = (2 * h + part) * PEER_D_HALF
            q_scr[2 * h + part] = q[:, col:col + PEER_D_HALF]
            s_t = lax.dot_general(keys_ref[part, h], q[:, col:col + PEER_D_HALF],
                                  (((1,), (1,)), ((), ())), preferred_element_type=F32)
            sc.append(s_t)
            top.append(_top_values(s_t, PEER_TOPK))
            tied = jnp.maximum(tied, jnp.abs(_count_rows(s_t >= top[part][-1]) - PEER_TOPK))
        cand = jnp.concatenate([top[0][a] + top[1][b] for a, b in _CAND_CELLS], axis=0)
        best = _top_values(cand, PEER_TOPK)
        c_max, thr = best[0], best[-1]
        z = jnp.exp(best[0] - c_max)
        for r in range(1, PEER_TOPK):
            z = z + jnp.exp(best[r] - c_max)
        top1 = jnp.concatenate(top[1], axis=0)
        n_sel = sum(_count_rows(top[0][a] + top1 >= thr) for a in range(PEER_TOPK))
        tied = jnp.maximum(tied, jnp.abs(n_sel - PEER_TOPK))
        cut = jnp.full(sc[0].shape, jnp.inf, F32)
        for b in range(PEER_TOPK):
            cut = jnp.where(sc[0] + top[1][b] >= thr, top[1][b], cut)
        store(h, cut, jnp.exp(sc[0] - top[0][0]) / z, sc[1], jnp.exp(sc[1] - top[1][0]))
        for ns in range(nsb):
            tie_ref[h * nsb + ns] = jnp.max(tied[:, ns * LANES:(ns + 1) * LANES])

    def exact_block(idx, carry):
        @pl.when(tie_ref[idx] > 0.0)
        def _():
            h = idx // nsb
            ns = idx % nsb
            rows = pl.ds(pl.multiple_of(ns * LANES, LANES), LANES)
            sc, top, rank = [], [], []
            for part in range(2):
                s_t = lax.dot_general(keys_ref[part, h], q_scr[2 * h + part, rows, :],
                                      (((1,), (1,)), ((), ())), preferred_element_type=F32)
                vals, rk = _top_ranked(s_t, PEER_TOPK)
                sc.append(s_t)
                top.append(vals)
                rank.append(rk)
            top1 = jnp.concatenate(top[1], axis=0)
            cand = jnp.concatenate([top[0][a] + top1 for a in range(PEER_TOPK)], axis=0)
            best, pos = _top_ranked(cand, PEER_TOPK)
            z = jnp.exp(best[0] - best[0])
            for r in range(1, PEER_TOPK):
                z = z + jnp.exp(best[r] - best[0])
            cnt = jnp.zeros_like(sc[0])
            for a in range(PEER_TOPK):
                cnt_a = _count_rows(pos[a * PEER_TOPK:(a + 1) * PEER_TOPK] < PEER_TOPK)
                cnt = jnp.where(rank[0] == float(a), cnt_a, cnt)
            cut_ref[h, ns] = jnp.where(cnt > 0.0, 1.0 - cnt, jnp.inf)
            e0_ref[h, ns] = jnp.exp(sc[0] - top[0][0]) / z
            s1_ref[h, ns] = -rank[1]
            e1_ref[h, ns] = jnp.exp(sc[1] - top[1][0])
        return carry

    lax.fori_loop(0, PEER_HEADS * nsb, exact_block, 0)


def _route(h2, wq, keys, tn):
    n, d = h2.shape
    nk = PEER_N_KEYS
    tiled = pl.BlockSpec((PEER_HEADS, tn // LANES, nk, LANES), lambda i: (0, i, 0, 0))
    tiled_shape = jax.ShapeDtypeStruct((PEER_HEADS, n // LANES, nk, LANES), F32)
    return pl.pallas_call(
        _route_kernel,
        grid=(n // tn,),
        in_specs=[pl.BlockSpec((tn, d), lambda i: (i, 0)),
                  pl.BlockSpec(wq.shape, lambda i: (0, 0)),
                  pl.BlockSpec(keys.shape, lambda i: (0, 0, 0, 0))],
        out_specs=[tiled, tiled, tiled, tiled],
        out_shape=[tiled_shape, tiled_shape, tiled_shape, tiled_shape],
        scratch_shapes=[pltpu.VMEM((2 * PEER_HEADS, tn, PEER_D_HALF), BF16),
                        pltpu.SMEM((PEER_HEADS * (tn // LANES),), F32)],
        compiler_params=_params(("parallel",)),
        name="peer_route",
    )(h2, wq, keys)


def _gelu(x):
    return 0.5 * x * (1.0 + lax.erf(x * (2.0 ** -0.5)))


PEER_TILE = 1024
GATE_HALF = 32
GATE_ROWS = 8


def _peer_kernel(h_ref, x1_ref, u_ref, vt_ref, cut_ref, e0_ref, s1_ref, e1_ref, ng_ref,
                 y_ref, acc_ref, act_ref, coef_ref, *, te):
    ei = pl.program_id(1)
    tn = h_ref.shape[0]
    groups = te // PEER_N_KEYS

    @pl.when(ei == 0)
    def _():
        acc_ref[...] = jnp.zeros_like(acc_ref)

    nsb = tn // LANES
    act = _gelu(lax.dot_general(u_ref[...], h_ref[...], (((1,), (1,)), ((), ())),
                                preferred_element_type=F32))
    for ns in range(nsb):
        act_ref[ns] = act[:, ns * LANES:(ns + 1) * LANES]

    def gate_block(ns):
        for jh in range(PEER_N_KEYS // GATE_HALF):
            jrows = slice(jh * GATE_HALF, (jh + 1) * GATE_HALF)
            for g0 in range(0, groups, GATE_ROWS):
                gates = [None] * GATE_ROWS
                for h in range(PEER_HEADS):
                    s1_t = s1_ref[h, ns, jrows, :]
                    e1_t = e1_ref[h, ns, jrows, :]
                    for k in range(GATE_ROWS):
                        i_key = ei * groups + g0 + k
                        sel = s1_t >= cut_ref[h, ns, pl.ds(i_key, 1), :]
                        term = jnp.where(sel, e1_t, 0.0) * e0_ref[h, ns, pl.ds(i_key, 1), :]
                        gates[k] = term if gates[k] is None else gates[k] + term
                for k in range(GATE_ROWS):
                    r0 = (g0 + k) * PEER_N_KEYS + jh * GATE_HALF
                    coef_ref[ns, r0:r0 + GATE_HALF, :] = (gates[k] * act_ref[ns, r0:r0 + GATE_HALF, :]).astype(BF16)

    def lane_block(ns, carry):
        gate_block(ns)
        return carry

    def mix_values(lo, hi):
        coef = jnp.concatenate([coef_ref[ns] for ns in range(lo, hi)], axis=1)
        acc_ref[:, lo * LANES:hi * LANES] += jnp.dot(vt_ref[0], coef, preferred_element_type=F32)

    half = nsb // 2
    lax.fori_loop(0, half, lane_block, 0)
    mix_values(0, half)
    for ns in range(half, nsb):
        gate_block(ns)
    mix_values(half, nsb)

    @pl.when(ei == pl.num_programs(1) - 1)
    def _():
        x2 = x1_ref[...] + acc_ref[...].T
        y_ref[...] = (x2 * _rms_scale(x2)) * ng_ref[...]


def _peer(h2, x1, u_bf, vt_bf, cut, e0, s1, e1, norm_g, tn):
    n, d = h2.shape
    ne = u_bf.shape[0]
    te = vt_bf.shape[2]
    nk = PEER_N_KEYS
    tok = pl.BlockSpec((tn, d), lambda i, e: (i, 0))
    tiled = pl.BlockSpec((PEER_HEADS, tn // LANES, nk, LANES), lambda i, e: (0, i, 0, 0))
    return pl.pallas_call(
        functools.partial(_peer_kernel, te=te),
        grid=(n // tn, ne // te),
        in_specs=[tok, tok,
                  pl.BlockSpec((te, d), lambda i, e: (e, 0)),
                  pl.BlockSpec((1, d, te), lambda i, e: (e, 0, 0)),
                  tiled, tiled, tiled, tiled,
                  pl.BlockSpec((1, d), lambda i, e: (0, 0))],
        out_specs=tok,
        out_shape=jax.ShapeDtypeStruct((n, d), F32),
        scratch_shapes=[pltpu.VMEM((d, tn), F32), pltpu.VMEM((tn // LANES, te, LANES), F32),
                        pltpu.VMEM((tn // LANES, te, LANES), BF16)],
        compiler_params=_params(("parallel", "arbitrary")),
        name="peer_mix",
    )(h2, x1, u_bf, vt_bf, cut, e0, s1, e1, norm_g.reshape(1, d))


def _tile(n, pref):
    return math.gcd(n, pref)


def _layer(x, mk, mv, bsz, t, weights, conv_state=None, gdn_conv_state=None, gdn_s0=None):
    (norm_mix_g, w_main, w_ab, conv_w, gdn_conv_w, a_log, dt_bias, gdn_norm_g, wc, wg, wm, wo, norm_ffn_g,
     wq, keys, u_bf, vt_bf, final_norm_g) = weights
    n = bsz * t
    sample = conv_state is not None
    p, pab = _inproj(x, norm_mix_g, w_main, w_ab, _tile(n, 1024))
    rows = _tile(n, 256) if sample else _tile(t, 512)
    ya, q, k, v, gb, u_tail = _prep(p, pab, conv_w, gdn_conv_w, a_log, dt_bias, rows, max(t // rows, 1),
                                    conv_state, gdn_conv_state)
    chunk = _tile(t, 64)
    if sample:
        yb, s_new = _gdn(q, k, v, p, gb, gdn_norm_g, bsz, t, chunk, 4, gdn_s0)
    else:
        yb, s_new = _gdn_stacked(q, k, v, p, gb, gdn_norm_g, bsz, t, chunk, _tile(bsz, 2))
    ym = _mem_attn(p, mk, mv, bsz, t, _tile(t, 512), _tile(bsz, 4) if sample else 1)
    x1, h2 = _merge(x, ya, yb, ym, p, wc, wg, wm, wo, norm_ffn_g, _tile(n, 256))
    tn = _tile(n, 512)
    cut, e0, s1, e1 = _route(h2, wq, keys, tn)
    y = _peer(h2, x1, u_bf, vt_bf, cut, e0, s1, e1, final_norm_g, tn)
    return y, u_tail, p, s_new


def kernel(x_prompt, x_sample, mem_prompt, cache_mem_k, cache_mem_v, state_conv, state_gdn_conv, state_gdn, norm_mix_g, w_in, conv_w, gdn_conv_w, gdn_a_log, gdn_dt_bias, gdn_norm_g, mem_norm_g, w_mem_kv, w_br_conv, w_br_gdn, w_br_mem, w_out, norm_ffn_g, peer_w_q, peer_sub_keys, peer_u, peer_v, final_norm_g):
    depth = w_in.shape[0]
    assert depth == 1, "single-layer step"
    d = D_MODEL
    bp, tp, _ = x_prompt.shape
    bs, ts, _ = x_sample.shape
    assert ts == SAMPLE_PAD_T // 2
    m_tok = mem_prompt.shape[1]
    mq = MEM_HEADS * MEM_HEAD_DIM
    l = 0

    w = w_in[l]
    ga0 = 7 * d
    w_main = jnp.concatenate([w[:, :ga0], w[:, ga0 + 2 * GDN_HEADS:]], axis=1).astype(BF16)
    w_ab = jnp.pad(w[:, ga0:ga0 + 2 * GDN_HEADS], ((0, 0), (0, SMALL_COLS - 2 * GDN_HEADS))).astype(BF16)
    weights = (norm_mix_g[l], w_main, w_ab, conv_w[l], gdn_conv_w[l], gdn_a_log[l], gdn_dt_bias[l], gdn_norm_g[l],
               w_br_conv[l].astype(BF16), w_br_gdn[l].astype(BF16), w_br_mem[l].astype(BF16), w_out[l].astype(BF16),
               norm_ffn_g[l], peer_w_q[l].astype(BF16), peer_sub_keys[l].astype(BF16),
               peer_u[l].astype(BF16),
               peer_v[l].astype(BF16).reshape(-1, PEER_TILE, d).transpose(0, 2, 1), final_norm_g)

    kv_shape = (bp, m_tok, mq)
    mk_p, mv_p = _mem_kv(mem_prompt.reshape(bp * m_tok, d), mem_norm_g[l], w_mem_kv[l].astype(BF16),
                         _tile(bp * m_tok, 256))
    y_p, u_tail_p, p_p, s_p = _layer(x_prompt.reshape(bp * tp, d), mk_p.reshape(kv_shape), mv_p.reshape(kv_shape),
                                     bp, tp, weights)
    rows_p = _tile(tp, 512)
    cb_p = u_tail_p.reshape(bp, tp // rows_p, SUBLANES, d)[:, -1, SUBLANES - (CONV_K - 1):, :]
    gb_p = p_p.reshape(bp, tp, -1)[:, tp - (GDN_CONV_K - 1):, COL_GQ * d:(COL_GV + 1) * d].astype(F32)

    pad_t = SAMPLE_PAD_T
    lead = pad_t - ts
    x_s = jnp.pad(x_sample, ((0, 0), (lead, 0), (0, 0))).reshape(bs * pad_t, d)
    cst = jnp.pad(state_conv[l], ((0, 0), (lead - (CONV_K - 1), ts), (0, 0))).reshape(bs * pad_t, d)
    gst = jnp.pad(state_gdn_conv[l], ((0, 0), (lead - (GDN_CONV_K - 1), ts), (0, 0))).reshape(bs * pad_t, 3 * d)
    y_s, u_s, p_s, s_s = _layer(x_s, cache_mem_k.reshape(bs, -1, mq), cache_mem_v.reshape(bs, -1, mq),
                                bs, pad_t, weights, cst, gst, state_gdn.reshape(bs, GDN_HEADS, GDN_DK, GDN_DV))
    y_s = y_s.reshape(bs, pad_t, d)[:, lead:, :]
    cb_s = u_s.reshape(bs, pad_t, d)[:, pad_t - (CONV_K - 1):, :]
    gb_s = p_s.reshape(bs, pad_t, -1)[:, pad_t - (GDN_CONV_K - 1):, COL_GQ * d:(COL_GV + 1) * d].astype(F32)

    mshape = (depth, bp, m_tok, MEM_HEADS, MEM_HEAD_DIM)
    return (y_p.reshape(bp, tp, d), y_s,
            mk_p.reshape(mshape), mv_p.reshape(mshape),
            cb_p[None], gb_p[None], s_p[None],
            cb_s[None], gb_s[None], s_s[None])
```

```python
import functools
import math

import jax
import jax.numpy as jnp
from jax import lax
from jax.experimental import pallas as pl
from jax.experimental.pallas import tpu as pltpu

F32 = jnp.float32
BF16 = jnp.bfloat16

EPS = 1e-6
D_MODEL = 1024
CONV_K = 3
GDN_HEADS = 8
GDN_DK = 128
GDN_DV = 128
GDN_CONV_K = 4
MEM_HEADS = 4
MEM_HEAD_DIM = 256
PEER_HEADS = 8
PEER_N_KEYS = 128
PEER_D_HALF = 128
PEER_TOPK = 16
N_BRANCHES = 3

COL_AU, COL_AB, COL_AC, COL_GQ, COL_GK, COL_GV, COL_GZ, COL_MQ, COL_BG = 0, 1, 2, 3, 4, 5, 6, 7, 8
N_MAIN_BLOCKS = 11
SMALL_COLS = 128
SUBLANES = 8
LANES = 128
SAMPLE_PAD_T = 8
VMEM_LIMIT = 56 * 1024 * 1024


def _params(sem):
    return pltpu.CompilerParams(dimension_semantics=sem, vmem_limit_bytes=VMEM_LIMIT)


def _dot(a, b):
    return jnp.dot(a.astype(BF16), b.astype(BF16), preferred_element_type=F32)


def _dot_nt(a, b):
    return lax.dot_general(a.astype(BF16), b.astype(BF16), (((1,), (1,)), ((), ())),
                           preferred_element_type=F32)


def _rms_scale(x):
    return lax.rsqrt(jnp.mean(x * x, axis=-1, keepdims=True) + EPS)


def _inproj_kernel(x_ref, g_ref, w_ref, wab_ref, p_ref, pab_ref, h_ref):
    @pl.when(pl.program_id(1) == 0)
    def _():
        x = x_ref[...]
        h = ((x * _rms_scale(x)) * g_ref[...]).astype(BF16)
        h_ref[...] = h
        pab_ref[...] = jnp.dot(h, wab_ref[...], preferred_element_type=F32)

    p_ref[...] = jnp.dot(h_ref[...], w_ref[...], preferred_element_type=F32).astype(p_ref.dtype)


def _inproj(x, g, w_main, w_ab, tm):
    n, d = x.shape
    nblk = w_main.shape[1] // d
    return pl.pallas_call(
        _inproj_kernel,
        grid=(n // tm, nblk),
        in_specs=[
            pl.BlockSpec((tm, d), lambda i, j: (i, 0)),
            pl.BlockSpec((1, d), lambda i, j: (0, 0)),
            pl.BlockSpec((d, d), lambda i, j: (0, j)),
            pl.BlockSpec((d, SMALL_COLS), lambda i, j: (0, 0)),
        ],
        out_specs=[
            pl.BlockSpec((tm, d), lambda i, j: (i, j)),
            pl.BlockSpec((tm, SMALL_COLS), lambda i, j: (i, 0)),
        ],
        out_shape=[
            jax.ShapeDtypeStruct((n, nblk * d), BF16),
            jax.ShapeDtypeStruct((n, SMALL_COLS), F32),
        ],
        scratch_shapes=[pltpu.VMEM((tm, d), BF16)],
        compiler_params=_params(("parallel", "arbitrary")),
        name="inproj",
    )(x, g.reshape(1, d), w_main, w_ab)


def _mem_kv_kernel(x_ref, g_ref, w_ref, k_ref, v_ref):
    x = x_ref[...]
    h = ((x * _rms_scale(x)) * g_ref[...]).astype(BF16)
    kv = jnp.dot(h, w_ref[...], preferred_element_type=F32)
    half = kv.shape[1] // 2
    k_ref[...] = kv[:, :half]
    v_ref[...] = kv[:, half:]


def _mem_kv(x, g, w, tm):
    n, d = x.shape
    c = w.shape[1]
    out = jax.ShapeDtypeStruct((n, c // 2), F32)
    ospec = pl.BlockSpec((tm, c // 2), lambda i: (i, 0))
    return pl.pallas_call(
        _mem_kv_kernel,
        grid=(n // tm,),
        in_specs=[
            pl.BlockSpec((tm, d), lambda i: (i, 0)),
            pl.BlockSpec((1, d), lambda i: (0, 0)),
            pl.BlockSpec((d, c), lambda i: (0, 0)),
        ],
        out_specs=[ospec, ospec],
        out_shape=[out, out],
        compiler_params=_params(("parallel",)),
        name="mem_kv",
    )(x, g.reshape(1, d), w)


def _causal_conv(prev, cur, w_ref, k):
    full = jnp.concatenate([prev, cur], axis=0)
    y = cur * w_ref[k - 1:k, :]
    for s in range(1, k):
        y = y + pltpu.roll(full, s, axis=0)[prev.shape[0]:, :] * w_ref[k - 1 - s:k - s, :]
    return y


def _f32(ref):
    return ref[...].astype(F32)


def _prep_kernel(*refs, rows, tiles_per_seq, sample):
    if sample:
        (au, ab, ac, gq, gk, gv, pab, au_p, ac_p, gq_p, gk_p, gv_p, cw, gw, alog, dtb, cst, gst,
         ya_o, q_o, k_o, v_o, gb_o, ut_o) = refs
    else:
        (au, ab, ac, gq, gk, gv, pab, au_p, ac_p, gq_p, gk_p, gv_p, cw, gw, alog, dtb,
         ya_o, q_o, k_o, v_o, gb_o, ut_o) = refs
    d = D_MODEL
    first = (pl.program_id(0) % tiles_per_seq) == 0
    keep = jnp.where(first, 0.0, 1.0).astype(F32)

    u = _f32(ac) * _f32(au)
    u_prev = (_f32(ac_p) * _f32(au_p)) * keep
    if sample:
        real = (lax.broadcasted_iota(jnp.int32, (rows, 1), 0) % SAMPLE_PAD_T) >= (SAMPLE_PAD_T // 2)
        u = jnp.where(real, u, cst[...])
    ya_o[...] = (_f32(ab) * _causal_conv(u_prev, u, cw, CONV_K)).astype(BF16)
    ut_o[...] = u[rows - ut_o.shape[0]:, :]

    outs = (q_o, k_o, v_o)
    for idx, (cur_ref, prev_ref) in enumerate(((gq, gq_p), (gk, gk_p), (gv, gv_p))):
        cur = _f32(cur_ref)
        if sample:
            cur = jnp.where(real, cur, gst[:, idx * d:(idx + 1) * d])
        prev = _f32(prev_ref) * keep
        wslab = gw.at[:, idx * d:(idx + 1) * d]
        y = _causal_conv(prev, cur, wslab, GDN_CONV_K)
        y = y * jax.nn.sigmoid(y)
        if idx < 2:
            scale = GDN_DK ** -0.5 if idx == 0 else 1.0
            for h in range(GDN_HEADS):
                yh = y[:, h * GDN_DK:(h + 1) * GDN_DK]
                r = lax.rsqrt(jnp.sum(yh * yh, axis=-1, keepdims=True) + EPS)
                outs[idx][:, h * GDN_DK:(h + 1) * GDN_DK] = yh * (r * scale)
        else:
            outs[idx][...] = y

    pa = pab[...]
    lane = lax.broadcasted_iota(jnp.int32, pa.shape, 1)
    z = pa + dtb[...]
    softplus = jnp.maximum(z, 0.0) + jnp.log1p(jnp.exp(-jnp.abs(z)))
    g_log = -jnp.exp(alog[...]) * softplus
    gb = jnp.where(lane < GDN_HEADS, g_log, jnp.where(lane < 2 * GDN_HEADS, jax.nn.sigmoid(pa), 0.0))
    if sample:
        gb = jnp.where(real, gb, 0.0)
    gb_o[...] = gb


def _prep(p, pab, conv_w, gdn_conv_w, a_log, dt_bias, rows, tiles_per_seq, conv_state=None, gdn_state=None):
    n = p.shape[0]
    d = D_MODEL
    sample = conv_state is not None
    nt = n // rows
    prev_rows = 2 * SUBLANES
    rpb = rows // prev_rows

    def cur(c):
        return pl.BlockSpec((rows, d), lambda i, c=c: (i, c))

    def prev(c):
        return pl.BlockSpec((prev_rows, d), lambda i, c=c: (jnp.maximum(i * rpb - 1, 0), c))

    alog128 = jnp.zeros((1, SMALL_COLS), F32).at[0, :GDN_HEADS].set(a_log)
    dtb128 = jnp.zeros((1, SMALL_COLS), F32).at[0, :GDN_HEADS].set(dt_bias)
    in_specs = [cur(COL_AU), cur(COL_AB), cur(COL_AC), cur(COL_GQ), cur(COL_GK), cur(COL_GV),
                pl.BlockSpec((rows, SMALL_COLS), lambda i: (i, 0)),
                prev(COL_AU), prev(COL_AC), prev(COL_GQ), prev(COL_GK), prev(COL_GV),
                pl.BlockSpec((CONV_K, d), lambda i: (0, 0)),
                pl.BlockSpec((GDN_CONV_K, 3 * d), lambda i: (0, 0)),
                pl.BlockSpec((1, SMALL_COLS), lambda i: (0, 0)),
                pl.BlockSpec((1, SMALL_COLS), lambda i: (0, 0))]
    args = [p, p, p, p, p, p, pab, p, p, p, p, p, conv_w, gdn_conv_w, alog128, dtb128]
    if sample:
        in_specs += [pl.BlockSpec((rows, d), lambda i: (i, 0)), pl.BlockSpec((rows, 3 * d), lambda i: (i, 0))]
        args += [conv_state, gdn_state]
    tail = rows if sample else SUBLANES
    out_specs = [pl.BlockSpec((rows, d), lambda i: (i, 0))] * 4 + [
        pl.BlockSpec((rows, SMALL_COLS), lambda i: (i, 0)),
        pl.BlockSpec((tail, d), lambda i: (i, 0))]
    out_shape = [jax.ShapeDtypeStruct((n, d), BF16)] + [jax.ShapeDtypeStruct((n, d), F32)] * 3 + [
        jax.ShapeDtypeStruct((n, SMALL_COLS), F32),
        jax.ShapeDtypeStruct((nt * tail, d), F32)]
    return pl.pallas_call(
        functools.partial(_prep_kernel, rows=rows, tiles_per_seq=tiles_per_seq, sample=sample),
        grid=(nt,),
        in_specs=in_specs,
        out_specs=out_specs,
        out_shape=out_shape,
        compiler_params=_params(("parallel",)),
        name="prep",
    )(*args)


def _chunk_cumsum(g, c):
    row = lax.broadcasted_iota(jnp.int32, g.shape, 0)
    sh = 1
    while sh < c:
        g = g + jnp.where(row >= sh, pltpu.roll(g, sh, axis=0), 0.0)
        sh *= 2
    return g


def _gdn_stacked_kernel(q_ref, k_ref, v_ref, z_ref, gb_ref, ng_ref, yb_ref, sout_ref, s_scr, *, c, grp, nb):
    ci = pl.program_id(1)

    @pl.when(ci == 0)
    def _():
        s_scr[...] = jnp.zeros_like(s_scr)

    r = grp * c
    ng = ng_ref[...]
    seqs = []
    for s in range(nb):
        gb = gb_ref[s]
        gc = _chunk_cumsum(gb, c)
        g_last = gc[c - 1:c, :]
        seqs.append(dict(gb=gb, gc=gc, gct=gc.T, e_g=jnp.exp(gc), e_kd=jnp.exp(g_last - gc), e_last=jnp.exp(g_last)))

    ri = lax.broadcasted_iota(jnp.int32, (r, r), 0)
    cj = lax.broadcasted_iota(jnp.int32, (r, r), 1)
    same = (ri // c) == (cj // c)
    incl = same & (ri >= cj)
    strict = same & (ri > cj)
    head_block = (lax.broadcasted_iota(jnp.int32, (r, grp * GDN_DK), 0) // c
                  == lax.broadcasted_iota(jnp.int32, (r, grp * GDN_DK), 1) // GDN_DK)

    def block_layout(x):
        return jnp.where(head_block, jnp.concatenate([x] * grp, axis=1), 0.0)

    units = []
    for s in range(nb):
        for g0 in range(0, GDN_HEADS, grp):
            heads = list(range(g0, g0 + grp))
            sq = seqs[s]

            def stack(ref, s=s, heads=heads):
                return jnp.concatenate([ref[s, :, h * GDN_DK:(h + 1) * GDN_DK] for h in heads], axis=0)

            def col(a, off=0, heads=heads):
                return jnp.concatenate([a[:, off + h:off + h + 1] for h in heads], axis=0)

            qs, ks, vs = stack(q_ref), stack(k_ref), stack(v_ref)
            beta = col(sq["gb"], GDN_HEADS)
            kb = ks * beta
            units.append(dict(s=s, heads=heads, qs=qs, ks=ks, kb=kb, vb=vs * beta,
                              gcol=col(sq["gc"]), egcol=col(sq["e_g"]), ekdcol=col(sq["e_kd"]),
                              grow=jnp.concatenate([sq["gct"][h:h + 1, :] for h in heads], axis=1),
                              rows=slice((s * GDN_HEADS + g0) * GDN_DK, (s * GDN_HEADS + g0 + grp) * GDN_DK)))

    for u in units:
        u["aq"] = _dot_nt(jnp.concatenate([u["kb"], u["qs"]], axis=0), u["ks"])
    for u in units:
        decay = jnp.exp(jnp.where(incl, u["gcol"] - u["grow"], -jnp.inf))
        u["pw"] = -jnp.where(strict, u["aq"][:r] * decay, 0.0)
        u["attn"] = jnp.where(incl, u["aq"][r:] * decay, 0.0)
        u["x"] = jnp.concatenate([u["vb"], u["kb"] * u["egcol"]], axis=1)
    span = 1
    while span < c:
        span *= 2
        for u in units:
            if span < c:
                both = _dot(u["pw"], jnp.concatenate([u["pw"], u["x"]], axis=1))
                u["pw"] = both[:, :r]
                u["x"] = u["x"] + both[:, r:]
            else:
                u["x"] = u["x"] + _dot(u["pw"], u["x"])
    for u in units:
        u_val, w_k = u["x"][:, :GDN_DV], u["x"][:, GDN_DV:]
        u["s_old"] = s_scr[u["rows"], :]
        ws = _dot(jnp.concatenate([block_layout(w_k), block_layout(u["qs"] * u["egcol"])], axis=0), u["s_old"])
        u["v_new"] = u_val - ws[:r]
        u["o"] = ws[r:]
    for u in units:
        e_last = seqs[u["s"]]["e_last"]
        o = u["o"] + _dot(u["attn"], u["v_new"])
        e_rows = jnp.concatenate([jnp.broadcast_to(e_last[:, h:h + 1], (GDN_DK, 1)) for h in u["heads"]], axis=0)
        kd_t = block_layout(u["ks"] * u["ekdcol"]).T
        s_scr[u["rows"], :] = u["s_old"] * e_rows + _dot(kd_t, u["v_new"])
        for i, h in enumerate(u["heads"]):
            sl = slice(h * GDN_DK, (h + 1) * GDN_DK)
            oh = o[i * c:(i + 1) * c, :]
            zh = z_ref[u["s"], :, sl].astype(F32)
            yb_ref[u["s"], :, sl] = (((oh * _rms_scale(oh)) * ng) * (zh * jax.nn.sigmoid(zh))).astype(BF16)

    @pl.when(ci == pl.num_programs(1) - 1)
    def _():
        sout_ref[...] = s_scr[...]


def _gdn_stacked(q, k, v, p, gb, norm_g, bsz, t, c, nb):
    d = D_MODEL
    nc = t // c
    grp = min(GDN_HEADS, 256 // c)
    s_rows = nb * GDN_HEADS * GDN_DK

    def tok(col=0):
        return pl.BlockSpec((nb, c, d), lambda b, i, col=col: (b, i, col))

    yb, s_new = pl.pallas_call(
        functools.partial(_gdn_stacked_kernel, c=c, grp=grp, nb=nb),
        grid=(bsz // nb, nc),
        in_specs=[tok(), tok(), tok(), tok(COL_GZ),
                  pl.BlockSpec((nb, c, SMALL_COLS), lambda b, i: (b, i, 0)),
                  pl.BlockSpec((1, GDN_DV), lambda b, i: (0, 0))],
        out_specs=[tok(), pl.BlockSpec((s_rows, GDN_DV), lambda b, i: (b, 0))],
        out_shape=[jax.ShapeDtypeStruct((bsz, t, d), BF16),
                   jax.ShapeDtypeStruct((bsz * GDN_HEADS * GDN_DK, GDN_DV), F32)],
        scratch_shapes=[pltpu.VMEM((s_rows, GDN_DV), F32)],
        compiler_params=_params(("parallel", "arbitrary")),
        name="gdn_scan_prompt",
    )(*[a.reshape(bsz, t, -1) for a in (q, k, v, p, gb)], norm_g.reshape(1, GDN_DV))
    return yb.reshape(bsz * t, d), s_new.reshape(bsz, GDN_HEADS, GDN_DK, GDN_DV)


def _gdn_kernel(*refs, c, nb, has_s0):
    if has_s0:
        q_ref, k_ref, v_ref, z_ref, gb_ref, ng_ref, s0_ref, yb_ref, sout_ref, s_scr = refs
    else:
        q_ref, k_ref, v_ref, z_ref, gb_ref, ng_ref, yb_ref, sout_ref, s_scr = refs
    ci = pl.program_id(1)

    @pl.when(ci == 0)
    def _():
        if has_s0:
            s_scr[...] = s0_ref[...]
        else:
            s_scr[...] = jnp.zeros_like(s_scr)

    ri = lax.broadcasted_iota(jnp.int32, (c, c), 0)
    cj = lax.broadcasted_iota(jnp.int32, (c, c), 1)
    incl = ri >= cj
    strict = ri > cj
    ng = ng_ref[...]
    row = lax.broadcasted_iota(jnp.int32, (c, SMALL_COLS), 0)
    chains = [(s, h) for s in range(nb) for h in range(GDN_HEADS)]

    seq = []
    for s in range(nb):
        gb = gb_ref[s]
        gc = gb
        sh = 1
        while sh < c:
            gc = gc + jnp.where(row >= sh, pltpu.roll(gc, sh, axis=0), 0.0)
            sh *= 2
        g_last = gc[c - 1:c, :]
        seq.append(dict(gb=gb, gc=gc, gct=gc.T, e_g=jnp.exp(gc), e_kd=jnp.exp(g_last - gc), e_last=jnp.exp(g_last)))

    st = []
    for s, h in chains:
        sl = slice(h * GDN_DK, (h + 1) * GDN_DK)
        sq = seq[s]
        qh, kh, vh = q_ref[s, :, sl], k_ref[s, :, sl], v_ref[s, :, sl]
        beta = sq["gb"][:, GDN_HEADS + h:GDN_HEADS + h + 1]
        kb = kh * beta
        aq = _dot_nt(jnp.concatenate([kb, qh], axis=0), kh)
        st.append(dict(sl=sl, qh=qh, kh=kh, kb=kb, vb=vh * beta, aq=aq))

    for (s, h), e in zip(chains, st):
        sq = seq[s]
        decay = jnp.exp(jnp.where(incl, sq["gc"][:, h:h + 1] - sq["gct"][h:h + 1, :], -jnp.inf))
        e["neg"] = -jnp.where(strict, e["aq"][:c] * decay, 0.0)
        e["attn"] = jnp.where(incl, e["aq"][c:] * decay, 0.0)
        e["x"] = jnp.concatenate([e["vb"], e["kb"] * sq["e_g"][:, h:h + 1]], axis=1)
        e["pw"] = e["neg"]

    span = 1
    while span < c:
        span *= 2
        for e in st:
            if span < c:
                both = _dot(e["pw"], jnp.concatenate([e["pw"], e["x"]], axis=1))
                e["pw"] = both[:, :c]
                e["x"] = e["x"] + both[:, c:]
            else:
                e["x"] = e["x"] + _dot(e["pw"], e["x"])

    for (s, h), e in zip(chains, st):
        sq = seq[s]
        u_val, w_k = e["x"][:, :GDN_DV], e["x"][:, GDN_DV:]
        q_dec = e["qh"] * sq["e_g"][:, h:h + 1]
        e["s_old"] = s_scr[s, h]
        ws = _dot(jnp.concatenate([w_k, q_dec], axis=0), e["s_old"])
        e["v_new"] = u_val - ws[:c]
        e["o_inter"] = ws[c:]

    for (s, h), e in zip(chains, st):
        sq = seq[s]
        o = e["o_inter"] + _dot(e["attn"], e["v_new"])
        k_dec = e["kh"] * sq["e_kd"][:, h:h + 1]
        s_scr[s, h] = e["s_old"] * sq["e_last"][:, h:h + 1] + _dot(k_dec.T, e["v_new"])
        zh = z_ref[s, :, e["sl"]].astype(F32)
        yb_ref[s, :, e["sl"]] = (((o * _rms_scale(o)) * ng) * (zh * jax.nn.sigmoid(zh))).astype(BF16)

    @pl.when(ci == pl.num_programs(1) - 1)
    def _():
        sout_ref[...] = s_scr[...]


def _gdn(q, k, v, p, gb, norm_g, bsz, t, c, nb, s0=None):
    d = D_MODEL
    nc = t // c
    has_s0 = s0 is not None

    def tok(col=0):
        return pl.BlockSpec((nb, c, d), lambda b, i, col=col: (b, i, col))

    in_specs = [tok(), tok(), tok(), tok(COL_GZ),
                pl.BlockSpec((nb, c, SMALL_COLS), lambda b, i: (b, i, 0)),
                pl.BlockSpec((1, GDN_DV), lambda b, i: (0, 0))]
    args = [a.reshape(bsz, t, -1) for a in (q, k, v, p, gb)] + [norm_g.reshape(1, GDN_DV)]
    s_spec = pl.BlockSpec((nb, GDN_HEADS, GDN_DK, GDN_DV), lambda b, i: (b, 0, 0, 0))
    if has_s0:
        in_specs.append(s_spec)
        args.append(s0)
    yb, s_new = pl.pallas_call(
        functools.partial(_gdn_kernel, c=c, nb=nb, has_s0=has_s0),
        grid=(bsz // nb, nc),
        in_specs=in_specs,
        out_specs=[tok(), s_spec],
        out_shape=[jax.ShapeDtypeStruct((bsz, t, d), BF16),
                   jax.ShapeDtypeStruct((bsz, GDN_HEADS, GDN_DK, GDN_DV), F32)],
        scratch_shapes=[pltpu.VMEM((nb, GDN_HEADS, GDN_DK, GDN_DV), F32)],
        compiler_params=_params(("parallel", "arbitrary")),
        name="gdn_scan",
    )(*args)
    return yb.reshape(bsz * t, d), s_new


def _mem_attn_kernel(q_ref, k_ref, v_ref, o_ref, *, nb, tq):
    scale = MEM_HEAD_DIM ** -0.5
    pairs = [(s, h) for s in range(nb) for h in range(MEM_HEADS)]
    span = lambda s, h: (slice(s * tq, (s + 1) * tq), slice(h * MEM_HEAD_DIM, (h + 1) * MEM_HEAD_DIM))
    scores = [_dot_nt(q_ref[span(s, h)], k_ref[s, :, span(s, h)[1]]) * scale for s, h in pairs]
    probs = []
    for sc in scores:
        e = jnp.exp(sc - jnp.max(sc, axis=-1, keepdims=True))
        probs.append(e / jnp.sum(e, axis=-1, keepdims=True))
    for (s, h), prob in zip(pairs, probs):
        o_ref[span(s, h)] = _dot(prob, v_ref[s, :, span(s, h)[1]]).astype(BF16)


def _mem_attn(p, mk, mv, bsz, t, tq, nb):
    d = D_MODEL
    m = mk.shape[1]
    nq = t // tq
    assert nb == 1 or nq == 1
    kv_spec = pl.BlockSpec((nb, m, d), lambda b, i: (b, 0, 0))
    return pl.pallas_call(
        functools.partial(_mem_attn_kernel, nb=nb, tq=tq),
        grid=(bsz // nb, nq),
        in_specs=[pl.BlockSpec((nb * tq, d), lambda b, i: (b * nq + i, COL_MQ)), kv_spec, kv_spec],
        out_specs=pl.BlockSpec((nb * tq, d), lambda b, i: (b * nq + i, 0)),
        out_shape=jax.ShapeDtypeStruct((bsz * t, d), BF16),
        compiler_params=_params(("parallel", "arbitrary")),
        name="mem_attn",
    )(p, mk, mv)


def _merge_kernel(x_ref, ya_ref, yb_ref, ym_ref, g0_ref, g1_ref, g2_ref, wc_ref, wg_ref, wm_ref, wo_ref,
                  ng_ref, x1_ref, h2_ref):
    d = functools.partial(jnp.dot, preferred_element_type=F32)
    merged = (jax.nn.sigmoid(_f32(g0_ref)) * d(ya_ref[...], wc_ref[...])
              + jax.nn.sigmoid(_f32(g1_ref)) * d(yb_ref[...], wg_ref[...])
              + jax.nn.sigmoid(_f32(g2_ref)) * d(ym_ref[...], wm_ref[...]))
    x1 = x_ref[...] + d(merged.astype(BF16), wo_ref[...])
    x1_ref[...] = x1
    h2_ref[...] = ((x1 * _rms_scale(x1)) * ng_ref[...]).astype(BF16)


def _merge(x, ya, yb, ym, p, wc, wg, wm, wo, norm_g, tm):
    n, d = x.shape
    tok = pl.BlockSpec((tm, d), lambda i: (i, 0))
    wsp = pl.BlockSpec((d, d), lambda i: (0, 0))

    def gate(j):
        return pl.BlockSpec((tm, d), lambda i, j=j: (i, COL_BG + j))

    return pl.pallas_call(
        _merge_kernel,
        grid=(n // tm,),
        in_specs=[tok, tok, tok, tok, gate(0), gate(1), gate(2), wsp, wsp, wsp, wsp,
                  pl.BlockSpec((1, d), lambda i: (0, 0))],
        out_specs=[tok, tok],
        out_shape=[jax.ShapeDtypeStruct((n, d), F32), jax.ShapeDtypeStruct((n, d), BF16)],
        compiler_params=_params(("parallel",)),
        name="merge",
    )(x, ya, yb, ym, p, p, p, wc, wg, wm, wo, norm_g.reshape(1, d))


E1_SHIFT = SUBLANES
E1_PAD = 4 * SUBLANES
_CAND_CELLS = tuple((a, b) for a in range(PEER_TOPK) for b in range(PEER_TOPK) if (a + 1) * (b + 1) <= PEER_TOPK)


def _top_values(x, count):
    vals = []
    cur = x
    for r in range(count):
        m = jnp.max(cur, axis=0, keepdims=True)
        vals.append(m)
        if r + 1 < count:
            cur = jnp.where(cur == m, -jnp.inf, cur)
    return vals


def _top_ranked(x, count):
    n_rows = x.shape[0]
    iota = lax.broadcasted_iota(jnp.int32, x.shape, 0).astype(F32)
    vals = []
    cur = x
    rank = jnp.full(x.shape, float(count), F32)
    for r in range(count):
        m = jnp.max(cur, axis=0, keepdims=True)
        first = jnp.min(jnp.where(cur == m, iota, float(n_rows)), axis=0, keepdims=True)
        one = iota == first
        rank = jnp.where(one, float(r), rank)
        cur = jnp.where(one, -jnp.inf, cur)
        vals.append(m)
    return vals, rank


def _count_rows(mask):
    return jnp.sum(jnp.where(mask, 1.0, 0.0), axis=0, keepdims=True)


def _route_kernel(h_ref, wq_ref, keys_ref, cut_ref, e0_ref, s1_ref, e1_ref, q_scr, tie_ref):
    tn = h_ref.shape[0]
    nsb = tn // LANES
    q = jnp.dot(h_ref[...], wq_ref[...], preferred_element_type=F32).astype(BF16)

    def store(h, cut, e0, s1, e1):
        for ns in range(nsb):
            cols = slice(ns * LANES, (ns + 1) * LANES)
            cut_ref[h, ns] = cut[:, cols]
            e0_ref[h, ns] = e0[:, cols]
            s1_ref[h, ns] = s1[:, cols]
            e1_ref[h, ns, :E1_SHIFT, :] = jnp.zeros((E1_SHIFT, LANES), F32)
            e1_ref[h, ns, E1_SHIFT:E1_SHIFT + PEER_N_KEYS, :] = e1[:, cols]
            e1_ref[h, ns, E1_SHIFT + PEER_N_KEYS:, :] = jnp.zeros((E1_PAD - E1_SHIFT, LANES), F32)

    for h in range(PEER_HEADS):
        sc, top = [], []
        tied = jnp.zeros((1, tn), F32)
        for part in range(2):
            col = (2 * h + part) * PEER_D_HALF
            q_scr[2 * h + part] = q[:, col:col + PEER_D_HALF]
            s_t = lax.dot_general(keys_ref[part, h], q[:, col:col + PEER_D_HALF],
                                  (((1,), (1,)), ((), ())), preferred_element_type=F32)
            sc.append(s_t)
            top.append(_top_values(s_t, PEER_TOPK))
            tied = jnp.maximum(tied, jnp.abs(_count_rows(s_t >= top[part][-1]) - PEER_TOPK))
        cand = jnp.concatenate([top[0][a] + top[1][b] for a, b in _CAND_CELLS], axis=0)
        best = _top_values(cand, PEER_TOPK)
        c_max, thr = best[0], best[-1]
        z = jnp.exp(best[0] - c_max)
        for r in range(1, PEER_TOPK):
            z = z + jnp.exp(best[r] - c_max)
        top1 = jnp.concatenate(top[1], axis=0)
        n_sel = sum(_count_rows(top[0][a] + top1 >= thr) for a in range(PEER_TOPK))
        tied = jnp.maximum(tied, jnp.abs(n_sel - PEER_TOPK))
        cut = jnp.full(sc[0].shape, jnp.inf, F32)
        for b in range(PEER_TOPK):
            cut = jnp.where(sc[0] + top[1][b] >= thr, top[1][b], cut)
        store(h, cut, jnp.exp(sc[0] - top[0][0]) / z, sc[1], jnp.exp(sc[1] - top[1][0]))
        for ns in range(nsb):
            tie_ref[h * nsb + ns] = jnp.max(tied[:, ns * LANES:(ns + 1) * LANES])

    def exact_block(idx, carry):
        @pl.when(tie_ref[idx] > 0.0)
        def _():
            h = idx // nsb
            ns = idx % nsb
            rows = pl.ds(pl.multiple_of(ns * LANES, LANES), LANES)
            sc, top, rank = [], [], []
            for part in range(2):
                s_t = lax.dot_general(keys_ref[part, h], q_scr[2 * h + part, rows, :],
                                      (((1,), (1,)), ((), ())), preferred_element_type=F32)
                vals, rk = _top_ranked(s_t, PEER_TOPK)
                sc.append(s_t)
                top.append(vals)
                rank.append(rk)
            top1 = jnp.concatenate(top[1], axis=0)
            cand = jnp.concatenate([top[0][a] + top1 for a in range(PEER_TOPK)], axis=0)
            best, pos = _top_ranked(cand, PEER_TOPK)
            z = jnp.exp(best[0] - best[0])
            for r in range(1, PEER_TOPK):
                z = z + jnp.exp(best[r] - best[0])
            cnt = jnp.zeros_like(sc[0])
            for a in range(PEER_TOPK):
                cnt_a = _count_rows(pos[a * PEER_TOPK:(a + 1) * PEER_TOPK] < PEER_TOPK)
                cnt = jnp.where(rank[0] == float(a), cnt_a, cnt)
            cut_ref[h, ns] = jnp.where(cnt > 0.0, 1.0 - cnt, jnp.inf)
            e0_ref[h, ns] = jnp.exp(sc[0] - top[0][0]) / z
            s1_ref[h, ns] = -rank[1]
            e1_ref[h, ns, E1_SHIFT:E1_SHIFT + PEER_N_KEYS, :] = jnp.exp(sc[1] - top[1][0])
        return carry

    lax.fori_loop(0, PEER_HEADS * nsb, exact_block, 0)


def _route(h2, wq, keys, tn):
    n, d = h2.shape
    nk = PEER_N_KEYS
    tiled = pl.BlockSpec((PEER_HEADS, tn // LANES, nk, LANES), lambda i: (0, i, 0, 0))
    tiled_shape = jax.ShapeDtypeStruct((PEER_HEADS, n // LANES, nk, LANES), F32)
    shifted = pl.BlockSpec((PEER_HEADS, tn // LANES, nk + E1_PAD, LANES), lambda i: (0, i, 0, 0))
    shifted_shape = jax.ShapeDtypeStruct((PEER_HEADS, n // LANES, nk + E1_PAD, LANES), F32)
    return pl.pallas_call(
        _route_kernel,
        grid=(n // tn,),
        in_specs=[pl.BlockSpec((tn, d), lambda i: (i, 0)),
                  pl.BlockSpec(wq.shape, lambda i: (0, 0)),
                  pl.BlockSpec(keys.shape, lambda i: (0, 0, 0, 0))],
        out_specs=[tiled, tiled, tiled, shifted],
        out_shape=[tiled_shape, tiled_shape, tiled_shape, shifted_shape],
        scratch_shapes=[pltpu.VMEM((2 * PEER_HEADS, tn, PEER_D_HALF), BF16),
                        pltpu.SMEM((PEER_HEADS * (tn // LANES),), F32)],
        compiler_params=_params(("parallel",)),
        name="peer_route",
    )(h2, wq, keys)


def _gelu(x):
    return 0.5 * x * (1.0 + lax.erf(x * (2.0 ** -0.5)))


PEER_TILE = 1024
GATE_HALF = 32
GATE_ROWS = 8


def _peer_kernel(h_ref, x1_ref, u_ref, vt_ref, cut_ref, e0_ref, s1_ref, e1_ref, ng_ref,
                 y_ref, acc_ref, act_ref, coef_ref, *, te):
    ei = pl.program_id(1)
    tn = h_ref.shape[0]
    groups = te // PEER_N_KEYS

    @pl.when(ei == 0)
    def _():
        acc_ref[...] = jnp.zeros_like(acc_ref)

    nsb = tn // LANES
    act = _gelu(lax.dot_general(u_ref[...], h_ref[...], (((1,), (1,)), ((), ())),
                                preferred_element_type=F32))
    for ns in range(nsb):
        act_ref[ns] = act[:, ns * LANES:(ns + 1) * LANES]

    def gate_block(ns):
        for jh in range(PEER_N_KEYS // GATE_HALF):
            jrows = slice(jh * GATE_HALF, (jh + 1) * GATE_HALF)
            for g0 in range(0, groups, GATE_ROWS):
                gates = [None] * GATE_ROWS
                for h in range(PEER_HEADS):
                    s1_t = s1_ref[h, ns, jrows, :]
                    e1_t = e1_ref[h, ns, E1_SHIFT + jh * GATE_HALF:E1_SHIFT + (jh + 1) * GATE_HALF, :]
                    for k in range(GATE_ROWS):
                        i_key = ei * groups + g0 + k
                        sel = s1_t >= cut_ref[h, ns, pl.ds(i_key, 1), :]
                        term = jnp.where(sel, e1_t, 0.0) * e0_ref[h, ns, pl.ds(i_key, 1), :]
                        gates[k] = term if gates[k] is None else gates[k] + term
                for k in range(GATE_ROWS):
                    r0 = (g0 + k) * PEER_N_KEYS + jh * GATE_HALF
                    coef_ref[ns, r0:r0 + GATE_HALF, :] = (gates[k] * act_ref[ns, r0:r0 + GATE_HALF, :]).astype(BF16)

    def lane_block(ns, carry):
        gate_block(ns)
        return carry

    def mix_values(lo, hi):
        coef = jnp.concatenate([coef_ref[ns] for ns in range(lo, hi)], axis=1)
        acc_ref[:, lo * LANES:hi * LANES] += jnp.dot(vt_ref[0], coef, preferred_element_type=F32)

    half = nsb // 2
    lax.fori_loop(0, half, lane_block, 0)
    if half:
        mix_values(0, half)
    for ns in range(half, nsb):
        gate_block(ns)
    mix_values(half, nsb)

    @pl.when(ei == pl.num_programs(1) - 1)
    def _():
        x2 = x1_ref[...] + acc_ref[...].T
        y_ref[...] = (x2 * _rms_scale(x2)) * ng_ref[...]


def _peer(h2, x1, u_bf, vt_bf, cut, e0, s1, e1, norm_g, tn):
    n, d = h2.shape
    ne = u_bf.shape[0]
    te = vt_bf.shape[2]
    nk = PEER_N_KEYS
    tok = pl.BlockSpec((tn, d), lambda i, e: (i, 0))
    tiled = pl.BlockSpec((PEER_HEADS, tn // LANES, nk, LANES), lambda i, e: (0, i, 0, 0))
    shifted = pl.BlockSpec((PEER_HEADS, tn // LANES, nk + E1_PAD, LANES), lambda i, e: (0, i, 0, 0))
    return pl.pallas_call(
        functools.partial(_peer_kernel, te=te),
        grid=(n // tn, ne // te),
        in_specs=[tok, tok,
                  pl.BlockSpec((te, d), lambda i, e: (e, 0)),
                  pl.BlockSpec((1, d, te), lambda i, e: (e, 0, 0)),
                  tiled, tiled, tiled, shifted,
                  pl.BlockSpec((1, d), lambda i, e: (0, 0))],
        out_specs=tok,
        out_shape=jax.ShapeDtypeStruct((n, d), F32),
        scratch_shapes=[pltpu.VMEM((d, tn), F32), pltpu.VMEM((tn // LANES, te, LANES), F32),
                        pltpu.VMEM((tn // LANES, te, LANES), BF16)],
        compiler_params=_params(("parallel", "arbitrary")),
        name="peer_mix",
    )(h2, x1, u_bf, vt_bf, cut, e0, s1, e1, norm_g.reshape(1, d))


def _tile(n, pref):
    return math.gcd(n, pref)


def _layer(x, mk, mv, bsz, t, weights, conv_state=None, gdn_conv_state=None, gdn_s0=None):
    (norm_mix_g, w_main, w_ab, conv_w, gdn_conv_w, a_log, dt_bias, gdn_norm_g, wc, wg, wm, wo, norm_ffn_g,
     wq, keys, u_bf, vt_bf, final_norm_g) = weights
    n = bsz * t
    sample = conv_state is not None
    p, pab = _inproj(x, norm_mix_g, w_main, w_ab, _tile(n, 1024))
    rows = _tile(n, 256) if sample else _tile(t, 512)
    ya, q, k, v, gb, u_tail = _prep(p, pab, conv_w, gdn_conv_w, a_log, dt_bias, rows, max(t // rows, 1),
                                    conv_state, gdn_conv_state)
    chunk = _tile(t, 64)
    if sample:
        yb, s_new = _gdn(q, k, v, p, gb, gdn_norm_g, bsz, t, chunk, 4, gdn_s0)
    else:
        yb, s_new = _gdn_stacked(q, k, v, p, gb, gdn_norm_g, bsz, t, chunk, _tile(bsz, 2))
    ym = _mem_attn(p, mk, mv, bsz, t, _tile(t, 512), _tile(bsz, 4) if sample else 1)
    x1, h2 = _merge(x, ya, yb, ym, p, wc, wg, wm, wo, norm_ffn_g, _tile(n, 256))
    tn = _tile(n, 512)
    cut, e0, s1, e1 = _route(h2, wq, keys, tn)
    y = _peer(h2, x1, u_bf, vt_bf, cut, e0, s1, e1, final_norm_g, tn)
    return y, u_tail, p, s_new


def kernel(x_prompt, x_sample, mem_prompt, cache_mem_k, cache_mem_v, state_conv, state_gdn_conv, state_gdn, norm_mix_g, w_in, conv_w, gdn_conv_w, gdn_a_log, gdn_dt_bias, gdn_norm_g, mem_norm_g, w_mem_kv, w_br_conv, w_br_gdn, w_br_mem, w_out, norm_ffn_g, peer_w_q, peer_sub_keys, peer_u, peer_v, final_norm_g):
    depth = w_in.shape[0]
    assert depth == 1, "single-layer step"
    d = D_MODEL
    bp, tp, _ = x_prompt.shape
    bs, ts, _ = x_sample.shape
    assert ts == SAMPLE_PAD_T // 2
    m_tok = mem_prompt.shape[1]
    mq = MEM_HEADS * MEM_HEAD_DIM
    l = 0

    w = w_in[l]
    ga0 = 7 * d
    w_main = jnp.concatenate([w[:, :ga0], w[:, ga0 + 2 * GDN_HEADS:]], axis=1).astype(BF16)
    w_ab = jnp.pad(w[:, ga0:ga0 + 2 * GDN_HEADS], ((0, 0), (0, SMALL_COLS - 2 * GDN_HEADS))).astype(BF16)
    weights = (norm_mix_g[l], w_main, w_ab, conv_w[l], gdn_conv_w[l], gdn_a_log[l], gdn_dt_bias[l], gdn_norm_g[l],
               w_br_conv[l].astype(BF16), w_br_gdn[l].astype(BF16), w_br_mem[l].astype(BF16), w_out[l].astype(BF16),
               norm_ffn_g[l], peer_w_q[l].astype(BF16), peer_sub_keys[l].astype(BF16),
               peer_u[l].astype(BF16),
               peer_v[l].astype(BF16).reshape(-1, PEER_TILE, d).transpose(0, 2, 1), final_norm_g)

    kv_shape = (bp, m_tok, mq)
    mk_p, mv_p = _mem_kv(mem_prompt.reshape(bp * m_tok, d), mem_norm_g[l], w_mem_kv[l].astype(BF16),
                         _tile(bp * m_tok, 256))
    y_p, u_tail_p, p_p, s_p = _layer(x_prompt.reshape(bp * tp, d), mk_p.reshape(kv_shape), mv_p.reshape(kv_shape),
                                     bp, tp, weights)
    rows_p = _tile(tp, 512)
    cb_p = u_tail_p.reshape(bp, tp // rows_p, SUBLANES, d)[:, -1, SUBLANES - (CONV_K - 1):, :]
    gb_p = p_p.reshape(bp, tp, -1)[:, tp - (GDN_CONV_K - 1):, COL_GQ * d:(COL_GV + 1) * d].astype(F32)

    pad_t = SAMPLE_PAD_T
    lead = pad_t - ts
    x_s = jnp.pad(x_sample, ((0, 0), (lead, 0), (0, 0))).reshape(bs * pad_t, d)
    cst = jnp.pad(state_conv[l], ((0, 0), (lead - (CONV_K - 1), ts), (0, 0))).reshape(bs * pad_t, d)
    gst = jnp.pad(state_gdn_conv[l], ((0, 0), (lead - (GDN_CONV_K - 1), ts), (0, 0))).reshape(bs * pad_t, 3 * d)
    y_s, u_s, p_s, s_s = _layer(x_s, cache_mem_k.reshape(bs, -1, mq), cache_mem_v.reshape(bs, -1, mq),
                                bs, pad_t, weights, cst, gst, state_gdn.reshape(bs, GDN_HEADS, GDN_DK, GDN_DV))
    y_s = y_s.reshape(bs, pad_t, d)[:, lead:, :]
    cb_s = u_s.reshape(bs, pad_t, d)[:, pad_t - (CONV_K - 1):, :]
    gb_s = p_s.reshape(bs, pad_t, -1)[:, pad_t - (GDN_CONV_K - 1):, COL_GQ * d:(COL_GV + 1) * d].astype(F32)

    mshape = (depth, bp, m_tok, MEM_HEADS, MEM_HEAD_DIM)
    return (y_p.reshape(bp, tp, d), y_s,
            mk_p.reshape(mshape), mv_p.reshape(mshape),
            cb_p[None], gb_p[None], s_p[None],
            cb_s[None], gb_s[None], s_s[None])
```

```python
import functools
import math

import jax
import jax.numpy as jnp
from jax import lax
from jax.experimental import pallas as pl
from jax.experimental.pallas import tpu as pltpu

F32 = jnp.float32
BF16 = jnp.bfloat16

EPS = 1e-6
D_MODEL = 1024
CONV_K = 3
GDN_HEADS = 8
GDN_DK = 128
GDN_DV = 128
GDN_CONV_K = 4
MEM_HEADS = 4
MEM_HEAD_DIM = 256
PEER_HEADS = 8
PEER_N_KEYS = 128
PEER_D_HALF = 128
PEER_TOPK = 16
N_BRANCHES = 3

COL_AU, COL_AB, COL_AC, COL_GQ, COL_GK, COL_GV, COL_GZ, COL_MQ, COL_BG = 0, 1, 2, 3, 4, 5, 6, 7, 8
N_MAIN_BLOCKS = 11
SMALL_COLS = 128
SUBLANES = 8
LANES = 128
SAMPLE_PAD_T = 8
VMEM_LIMIT = 56 * 1024 * 1024


def _params(sem):
    return pltpu.CompilerParams(dimension_semantics=sem, vmem_limit_bytes=VMEM_LIMIT)


def _dot(a, b):
    return jnp.dot(a.astype(BF16), b.astype(BF16), preferred_element_type=F32)


def _dot_nt(a, b):
    return lax.dot_general(a.astype(BF16), b.astype(BF16), (((1,), (1,)), ((), ())),
                           preferred_element_type=F32)


def _rms_scale(x):
    return lax.rsqrt(jnp.mean(x * x, axis=-1, keepdims=True) + EPS)


def _inproj_kernel(x_ref, g_ref, w_ref, wab_ref, p_ref, pab_ref, h_ref):
    @pl.when(pl.program_id(1) == 0)
    def _():
        x = x_ref[...]
        h = ((x * _rms_scale(x)) * g_ref[...]).astype(BF16)
        h_ref[...] = h
        pab_ref[...] = jnp.dot(h, wab_ref[...], preferred_element_type=F32)

    p_ref[...] = jnp.dot(h_ref[...], w_ref[...], preferred_element_type=F32).astype(p_ref.dtype)


def _inproj(x, g, w_main, w_ab, tm):
    n, d = x.shape
    nblk = w_main.shape[1] // d
    return pl.pallas_call(
        _inproj_kernel,
        grid=(n // tm, nblk),
        in_specs=[
            pl.BlockSpec((tm, d), lambda i, j: (i, 0)),
            pl.BlockSpec((1, d), lambda i, j: (0, 0)),
            pl.BlockSpec((d, d), lambda i, j: (0, j)),
            pl.BlockSpec((d, SMALL_COLS), lambda i, j: (0, 0)),
        ],
        out_specs=[
            pl.BlockSpec((tm, d), lambda i, j: (i, j)),
            pl.BlockSpec((tm, SMALL_COLS), lambda i, j: (i, 0)),
        ],
        out_shape=[
            jax.ShapeDtypeStruct((n, nblk * d), BF16),
            jax.ShapeDtypeStruct((n, SMALL_COLS), F32),
        ],
        scratch_shapes=[pltpu.VMEM((tm, d), BF16)],
        compiler_params=_params(("parallel", "arbitrary")),
        name="inproj",
    )(x, g.reshape(1, d), w_main, w_ab)


def _mem_kv_kernel(x_ref, g_ref, w_ref, k_ref, v_ref):
    x = x_ref[...]
    h = ((x * _rms_scale(x)) * g_ref[...]).astype(BF16)
    kv = jnp.dot(h, w_ref[...], preferred_element_type=F32)
    half = kv.shape[1] // 2
    k_ref[...] = kv[:, :half]
    v_ref[...] = kv[:, half:]


def _mem_kv(x, g, w, tm):
    n, d = x.shape
    c = w.shape[1]
    out = jax.ShapeDtypeStruct((n, c // 2), F32)
    ospec = pl.BlockSpec((tm, c // 2), lambda i: (i, 0))
    return pl.pallas_call(
        _mem_kv_kernel,
        grid=(n // tm,),
        in_specs=[
            pl.BlockSpec((tm, d), lambda i: (i, 0)),
            pl.BlockSpec((1, d), lambda i: (0, 0)),
            pl.BlockSpec((d, c), lambda i: (0, 0)),
        ],
        out_specs=[ospec, ospec],
        out_shape=[out, out],
        compiler_params=_params(("parallel",)),
        name="mem_kv",
    )(x, g.reshape(1, d), w)


def _causal_conv(prev, cur, w_ref, k):
    full = jnp.concatenate([prev, cur], axis=0)
    y = cur * w_ref[k - 1:k, :]
    for s in range(1, k):
        y = y + pltpu.roll(full, s, axis=0)[prev.shape[0]:, :] * w_ref[k - 1 - s:k - s, :]
    return y


def _f32(ref):
    return ref[...].astype(F32)


def _prep_kernel(*refs, rows, tiles_per_seq, sample):
    if sample:
        (au, ab, ac, gq, gk, gv, pab, au_p, ac_p, gq_p, gk_p, gv_p, cw, gw, alog, dtb, cst, gst,
         ya_o, q_o, k_o, v_o, gb_o, ut_o) = refs
    else:
        (au, ab, ac, gq, gk, gv, pab, au_p, ac_p, gq_p, gk_p, gv_p, cw, gw, alog, dtb,
         ya_o, q_o, k_o, v_o, gb_o, ut_o) = refs
    d = D_MODEL
    first = (pl.program_id(0) % tiles_per_seq) == 0
    keep = jnp.where(first, 0.0, 1.0).astype(F32)

    u = _f32(ac) * _f32(au)
    u_prev = (_f32(ac_p) * _f32(au_p)) * keep
    if sample:
        real = (lax.broadcasted_iota(jnp.int32, (rows, 1), 0) % SAMPLE_PAD_T) >= (SAMPLE_PAD_T // 2)
        u = jnp.where(real, u, cst[...])
    ya_o[...] = (_f32(ab) * _causal_conv(u_prev, u, cw, CONV_K)).astype(BF16)
    ut_o[...] = u[rows - ut_o.shape[0]:, :]

    outs = (q_o, k_o, v_o)
    for idx, (cur_ref, prev_ref) in enumerate(((gq, gq_p), (gk, gk_p), (gv, gv_p))):
        cur = _f32(cur_ref)
        if sample:
            cur = jnp.where(real, cur, gst[:, idx * d:(idx + 1) * d])
        prev = _f32(prev_ref) * keep
        wslab = gw.at[:, idx * d:(idx + 1) * d]
        y = _causal_conv(prev, cur, wslab, GDN_CONV_K)
        y = y * jax.nn.sigmoid(y)
        if idx < 2:
            scale = GDN_DK ** -0.5 if idx == 0 else 1.0
            for h in range(GDN_HEADS):
                yh = y[:, h * GDN_DK:(h + 1) * GDN_DK]
                r = lax.rsqrt(jnp.sum(yh * yh, axis=-1, keepdims=True) + EPS)
                outs[idx][:, h * GDN_DK:(h + 1) * GDN_DK] = yh * (r * scale)
        else:
            outs[idx][...] = y

    pa = pab[...]
    lane = lax.broadcasted_iota(jnp.int32, pa.shape, 1)
    z = pa + dtb[...]
    softplus = jnp.maximum(z, 0.0) + jnp.log1p(jnp.exp(-jnp.abs(z)))
    g_log = -jnp.exp(alog[...]) * softplus
    gb = jnp.where(lane < GDN_HEADS, g_log, jnp.where(lane < 2 * GDN_HEADS, jax.nn.sigmoid(pa), 0.0))
    if sample:
        gb = jnp.where(real, gb, 0.0)
    gb_o[...] = gb


def _prep(p, pab, conv_w, gdn_conv_w, a_log, dt_bias, rows, tiles_per_seq, conv_state=None, gdn_state=None):
    n = p.shape[0]
    d = D_MODEL
    sample = conv_state is not None
    nt = n // rows
    prev_rows = 2 * SUBLANES
    rpb = rows // prev_rows

    def cur(c):
        return pl.BlockSpec((rows, d), lambda i, c=c: (i, c))

    def prev(c):
        return pl.BlockSpec((prev_rows, d), lambda i, c=c: (jnp.maximum(i * rpb - 1, 0), c))

    alog128 = jnp.zeros((1, SMALL_COLS), F32).at[0, :GDN_HEADS].set(a_log)
    dtb128 = jnp.zeros((1, SMALL_COLS), F32).at[0, :GDN_HEADS].set(dt_bias)
    in_specs = [cur(COL_AU), cur(COL_AB), cur(COL_AC), cur(COL_GQ), cur(COL_GK), cur(COL_GV),
                pl.BlockSpec((rows, SMALL_COLS), lambda i: (i, 0)),
                prev(COL_AU), prev(COL_AC), prev(COL_GQ), prev(COL_GK), prev(COL_GV),
                pl.BlockSpec((CONV_K, d), lambda i: (0, 0)),
                pl.BlockSpec((GDN_CONV_K, 3 * d), lambda i: (0, 0)),
                pl.BlockSpec((1, SMALL_COLS), lambda i: (0, 0)),
                pl.BlockSpec((1, SMALL_COLS), lambda i: (0, 0))]
    args = [p, p, p, p, p, p, pab, p, p, p, p, p, conv_w, gdn_conv_w, alog128, dtb128]
    if sample:
        in_specs += [pl.BlockSpec((rows, d), lambda i: (i, 0)), pl.BlockSpec((rows, 3 * d), lambda i: (i, 0))]
        args += [conv_state, gdn_state]
    tail = rows if sample else SUBLANES
    out_specs = [pl.BlockSpec((rows, d), lambda i: (i, 0))] * 4 + [
        pl.BlockSpec((rows, SMALL_COLS), lambda i: (i, 0)),
        pl.BlockSpec((tail, d), lambda i: (i, 0))]
    out_shape = [jax.ShapeDtypeStruct((n, d), BF16)] + [jax.ShapeDtypeStruct((n, d), F32)] * 3 + [
        jax.ShapeDtypeStruct((n, SMALL_COLS), F32),
        jax.ShapeDtypeStruct((nt * tail, d), F32)]
    return pl.pallas_call(
        functools.partial(_prep_kernel, rows=rows, tiles_per_seq=tiles_per_seq, sample=sample),
        grid=(nt,),
        in_specs=in_specs,
        out_specs=out_specs,
        out_shape=out_shape,
        compiler_params=_params(("parallel",)),
        name="prep",
    )(*args)


def _chunk_cumsum(g, c):
    row = lax.broadcasted_iota(jnp.int32, g.shape, 0)
    sh = 1
    while sh < c:
        g = g + jnp.where(row >= sh, pltpu.roll(g, sh, axis=0), 0.0)
        sh *= 2
    return g


def _gdn_stacked_kernel(q_ref, k_ref, v_ref, z_ref, gb_ref, ng_ref, yb_ref, sout_ref, s_scr, *, c, grp, nb):
    ci = pl.program_id(1)

    @pl.when(ci == 0)
    def _():
        s_scr[...] = jnp.zeros_like(s_scr)

    r = grp * c
    ng = ng_ref[...]
    seqs = []
    for s in range(nb):
        gb = gb_ref[s]
        gc = _chunk_cumsum(gb, c)
        g_last = gc[c - 1:c, :]
        seqs.append(dict(gb=gb, gc=gc, gct=gc.T, e_g=jnp.exp(gc), e_kd=jnp.exp(g_last - gc), e_last=jnp.exp(g_last)))

    ri = lax.broadcasted_iota(jnp.int32, (r, r), 0)
    cj = lax.broadcasted_iota(jnp.int32, (r, r), 1)
    same = (ri // c) == (cj // c)
    incl = same & (ri >= cj)
    strict = same & (ri > cj)
    head_block = (lax.broadcasted_iota(jnp.int32, (r, grp * GDN_DK), 0) // c
                  == lax.broadcasted_iota(jnp.int32, (r, grp * GDN_DK), 1) // GDN_DK)

    def block_layout(x):
        return jnp.where(head_block, jnp.concatenate([x] * grp, axis=1), 0.0)

    units = []
    for s in range(nb):
        for g0 in range(0, GDN_HEADS, grp):
            heads = list(range(g0, g0 + grp))
            sq = seqs[s]

            def stack(ref, s=s, heads=heads):
                return jnp.concatenate([ref[s, :, h * GDN_DK:(h + 1) * GDN_DK] for h in heads], axis=0)

            def col(a, off=0, heads=heads):
                return jnp.concatenate([a[:, off + h:off + h + 1] for h in heads], axis=0)

            qs, ks, vs = stack(q_ref), stack(k_ref), stack(v_ref)
            beta = col(sq["gb"], GDN_HEADS)
            kb = ks * beta
            units.append(dict(s=s, heads=heads, qs=qs, ks=ks, kb=kb, vb=vs * beta,
                              gcol=col(sq["gc"]), egcol=col(sq["e_g"]), ekdcol=col(sq["e_kd"]),
                              grow=jnp.concatenate([sq["gct"][h:h + 1, :] for h in heads], axis=1),
                              rows=slice((s * GDN_HEADS + g0) * GDN_DK, (s * GDN_HEADS + g0 + grp) * GDN_DK)))

    for u in units:
        u["aq"] = _dot_nt(jnp.concatenate([u["kb"], u["qs"]], axis=0), u["ks"])
    for u in units:
        decay = jnp.exp(jnp.where(incl, u["gcol"] - u["grow"], -jnp.inf))
        u["pw"] = -jnp.where(strict, u["aq"][:r] * decay, 0.0)
        u["attn"] = jnp.where(incl, u["aq"][r:] * decay, 0.0)
        u["x"] = jnp.concatenate([u["vb"], u["kb"] * u["egcol"]], axis=1)
    span = 1
    while span < c:
        span *= 2
        for u in units:
            if span < c:
                both = _dot(u["pw"], jnp.concatenate([u["pw"], u["x"]], axis=1))
                u["pw"] = both[:, :r]
                u["x"] = u["x"] + both[:, r:]
            else:
                u["x"] = u["x"] + _dot(u["pw"], u["x"])
    for u in units:
        u_val, w_k = u["x"][:, :GDN_DV], u["x"][:, GDN_DV:]
        u["s_old"] = s_scr[u["rows"], :]
        ws = _dot(jnp.concatenate([block_layout(w_k), block_layout(u["qs"] * u["egcol"])], axis=0), u["s_old"])
        u["v_new"] = u_val - ws[:r]
        u["o"] = ws[r:]
    for u in units:
        e_last = seqs[u["s"]]["e_last"]
        o = u["o"] + _dot(u["attn"], u["v_new"])
        e_rows = jnp.concatenate([jnp.broadcast_to(e_last[:, h:h + 1], (GDN_DK, 1)) for h in u["heads"]], axis=0)
        kd_t = block_layout(u["ks"] * u["ekdcol"]).T
        s_scr[u["rows"], :] = u["s_old"] * e_rows + _dot(kd_t, u["v_new"])
        for i, h in enumerate(u["heads"]):
            sl = slice(h * GDN_DK, (h + 1) * GDN_DK)
            oh = o[i * c:(i + 1) * c, :]
            zh = z_ref[u["s"], :, sl].astype(F32)
            yb_ref[u["s"], :, sl] = (((oh * _rms_scale(oh)) * ng) * (zh * jax.nn.sigmoid(zh))).astype(BF16)

    @pl.when(ci == pl.num_programs(1) - 1)
    def _():
        sout_ref[...] = s_scr[...]


def _gdn_stacked(q, k, v, p, gb, norm_g, bsz, t, c, nb):
    d = D_MODEL
    nc = t // c
    grp = min(GDN_HEADS, 256 // c)
    s_rows = nb * GDN_HEADS * GDN_DK

    def tok(col=0):
        return pl.BlockSpec((nb, c, d), lambda b, i, col=col: (b, i, col))

    yb, s_new = pl.pallas_call(
        functools.partial(_gdn_stacked_kernel, c=c, grp=grp, nb=nb),
        grid=(bsz // nb, nc),
        in_specs=[tok(), tok(), tok(), tok(COL_GZ),
                  pl.BlockSpec((nb, c, SMALL_COLS), lambda b, i: (b, i, 0)),
                  pl.BlockSpec((1, GDN_DV), lambda b, i: (0, 0))],
        out_specs=[tok(), pl.BlockSpec((s_rows, GDN_DV), lambda b, i: (b, 0))],
        out_shape=[jax.ShapeDtypeStruct((bsz, t, d), BF16),
                   jax.ShapeDtypeStruct((bsz * GDN_HEADS * GDN_DK, GDN_DV), F32)],
        scratch_shapes=[pltpu.VMEM((s_rows, GDN_DV), F32)],
        compiler_params=_params(("parallel", "arbitrary")),
        name="gdn_scan_prompt",
    )(*[a.reshape(bsz, t, -1) for a in (q, k, v, p, gb)], norm_g.reshape(1, GDN_DV))
    return yb.reshape(bsz * t, d), s_new.reshape(bsz, GDN_HEADS, GDN_DK, GDN_DV)


def _gdn_kernel(*refs, c, nb, has_s0):
    if has_s0:
        q_ref, k_ref, v_ref, z_ref, gb_ref, ng_ref, s0_ref, yb_ref, sout_ref, s_scr = refs
    else:
        q_ref, k_ref, v_ref, z_ref, gb_ref, ng_ref, yb_ref, sout_ref, s_scr = refs
    ci = pl.program_id(1)

    @pl.when(ci == 0)
    def _():
        if has_s0:
            s_scr[...] = s0_ref[...]
        else:
            s_scr[...] = jnp.zeros_like(s_scr)

    ri = lax.broadcasted_iota(jnp.int32, (c, c), 0)
    cj = lax.broadcasted_iota(jnp.int32, (c, c), 1)
    incl = ri >= cj
    strict = ri > cj
    ng = ng_ref[...]
    row = lax.broadcasted_iota(jnp.int32, (c, SMALL_COLS), 0)
    chains = [(s, h) for s in range(nb) for h in range(GDN_HEADS)]

    seq = []
    for s in range(nb):
        gb = gb_ref[s]
        gc = gb
        sh = 1
        while sh < c:
            gc = gc + jnp.where(row >= sh, pltpu.roll(gc, sh, axis=0), 0.0)
            sh *= 2
        g_last = gc[c - 1:c, :]
        seq.append(dict(gb=gb, gc=gc, gct=gc.T, e_g=jnp.exp(gc), e_kd=jnp.exp(g_last - gc), e_last=jnp.exp(g_last)))

    st = []
    for s, h in chains:
        sl = slice(h * GDN_DK, (h + 1) * GDN_DK)
        sq = seq[s]
        qh, kh, vh = q_ref[s, :, sl], k_ref[s, :, sl], v_ref[s, :, sl]
        beta = sq["gb"][:, GDN_HEADS + h:GDN_HEADS + h + 1]
        kb = kh * beta
        aq = _dot_nt(jnp.concatenate([kb, qh], axis=0), kh)
        st.append(dict(sl=sl, qh=qh, kh=kh, kb=kb, vb=vh * beta, aq=aq))

    for (s, h), e in zip(chains, st):
        sq = seq[s]
        decay = jnp.exp(jnp.where(incl, sq["gc"][:, h:h + 1] - sq["gct"][h:h + 1, :], -jnp.inf))
        e["neg"] = -jnp.where(strict, e["aq"][:c] * decay, 0.0)
        e["attn"] = jnp.where(incl, e["aq"][c:] * decay, 0.0)
        e["x"] = jnp.concatenate([e["vb"], e["kb"] * sq["e_g"][:, h:h + 1]], axis=1)
        e["pw"] = e["neg"]

    span = 1
    while span < c:
        span *= 2
        for e in st:
            if span < c:
                both = _dot(e["pw"], jnp.concatenate([e["pw"], e["x"]], axis=1))
                e["pw"] = both[:, :c]
                e["x"] = e["x"] + both[:, c:]
            else:
                e["x"] = e["x"] + _dot(e["pw"], e["x"])

    for (s, h), e in zip(chains, st):
        sq = seq[s]
        u_val, w_k = e["x"][:, :GDN_DV], e["x"][:, GDN_DV:]
        q_dec = e["qh"] * sq["e_g"][:, h:h + 1]
        e["s_old"] = s_scr[s, h]
        ws = _dot(jnp.concatenate([w_k, q_dec], axis=0), e["s_old"])
        e["v_new"] = u_val - ws[:c]
        e["o_inter"] = ws[c:]

    for (s, h), e in zip(chains, st):
        sq = seq[s]
        o = e["o_inter"] + _dot(e["attn"], e["v_new"])
        k_dec = e["kh"] * sq["e_kd"][:, h:h + 1]
        s_scr[s, h] = e["s_old"] * sq["e_last"][:, h:h + 1] + _dot(k_dec.T, e["v_new"])
        zh = z_ref[s, :, e["sl"]].astype(F32)
        yb_ref[s, :, e["sl"]] = (((o * _rms_scale(o)) * ng) * (zh * jax.nn.sigmoid(zh))).astype(BF16)

    @pl.when(ci == pl.num_programs(1) - 1)
    def _():
        sout_ref[...] = s_scr[...]


def _gdn(q, k, v, p, gb, norm_g, bsz, t, c, nb, s0=None):
    d = D_MODEL
    nc = t // c
    has_s0 = s0 is not None

    def tok(col=0):
        return pl.BlockSpec((nb, c, d), lambda b, i, col=col: (b, i, col))

    in_specs = [tok(), tok(), tok(), tok(COL_GZ),
                pl.BlockSpec((nb, c, SMALL_COLS), lambda b, i: (b, i, 0)),
                pl.BlockSpec((1, GDN_DV), lambda b, i: (0, 0))]
    args = [a.reshape(bsz, t, -1) for a in (q, k, v, p, gb)] + [norm_g.reshape(1, GDN_DV)]
    s_spec = pl.BlockSpec((nb, GDN_HEADS, GDN_DK, GDN_DV), lambda b, i: (b, 0, 0, 0))
    if has_s0:
        in_specs.append(s_spec)
        args.append(s0)
    yb, s_new = pl.pallas_call(
        functools.partial(_gdn_kernel, c=c, nb=nb, has_s0=has_s0),
        grid=(bsz // nb, nc),
        in_specs=in_specs,
        out_specs=[tok(), s_spec],
        out_shape=[jax.ShapeDtypeStruct((bsz, t, d), BF16),
                   jax.ShapeDtypeStruct((bsz, GDN_HEADS, GDN_DK, GDN_DV), F32)],
        scratch_shapes=[pltpu.VMEM((nb, GDN_HEADS, GDN_DK, GDN_DV), F32)],
        compiler_params=_params(("parallel", "arbitrary")),
        name="gdn_scan",
    )(*args)
    return yb.reshape(bsz * t, d), s_new


def _mem_attn_kernel(q_ref, k_ref, v_ref, o_ref, *, nb, tq):
    scale = MEM_HEAD_DIM ** -0.5
    pairs = [(s, h) for s in range(nb) for h in range(MEM_HEADS)]
    span = lambda s, h: (slice(s * tq, (s + 1) * tq), slice(h * MEM_HEAD_DIM, (h + 1) * MEM_HEAD_DIM))
    scores = [_dot_nt(q_ref[span(s, h)], k_ref[s, :, span(s, h)[1]]) * scale for s, h in pairs]
    probs = []
    for sc in scores:
        e = jnp.exp(sc - jnp.max(sc, axis=-1, keepdims=True))
        probs.append(e / jnp.sum(e, axis=-1, keepdims=True))
    for (s, h), prob in zip(pairs, probs):
        o_ref[span(s, h)] = _dot(prob, v_ref[s, :, span(s, h)[1]]).astype(BF16)


def _mem_attn(p, mk, mv, bsz, t, tq, nb):
    d = D_MODEL
    m = mk.shape[1]
    nq = t // tq
    assert nb == 1 or nq == 1
    kv_spec = pl.BlockSpec((nb, m, d), lambda b, i: (b, 0, 0))
    return pl.pallas_call(
        functools.partial(_mem_attn_kernel, nb=nb, tq=tq),
        grid=(bsz // nb, nq),
        in_specs=[pl.BlockSpec((nb * tq, d), lambda b, i: (b * nq + i, COL_MQ)), kv_spec, kv_spec],
        out_specs=pl.BlockSpec((nb * tq, d), lambda b, i: (b * nq + i, 0)),
        out_shape=jax.ShapeDtypeStruct((bsz * t, d), BF16),
        compiler_params=_params(("parallel", "arbitrary")),
        name="mem_attn",
    )(p, mk, mv)


def _merge_kernel(x_ref, ya_ref, yb_ref, ym_ref, g0_ref, g1_ref, g2_ref, wc_ref, wg_ref, wm_ref, wo_ref,
                  ng_ref, x1_ref, h2_ref):
    d = functools.partial(jnp.dot, preferred_element_type=F32)
    merged = (jax.nn.sigmoid(_f32(g0_ref)) * d(ya_ref[...], wc_ref[...])
              + jax.nn.sigmoid(_f32(g1_ref)) * d(yb_ref[...], wg_ref[...])
              + jax.nn.sigmoid(_f32(g2_ref)) * d(ym_ref[...], wm_ref[...]))
    x1 = x_ref[...] + d(merged.astype(BF16), wo_ref[...])
    x1_ref[...] = x1
    h2_ref[...] = ((x1 * _rms_scale(x1)) * ng_ref[...]).astype(BF16)


def _merge(x, ya, yb, ym, p, wc, wg, wm, wo, norm_g, tm):
    n, d = x.shape
    tok = pl.BlockSpec((tm, d), lambda i: (i, 0))
    wsp = pl.BlockSpec((d, d), lambda i: (0, 0))

    def gate(j):
        return pl.BlockSpec((tm, d), lambda i, j=j: (i, COL_BG + j))

    return pl.pallas_call(
        _merge_kernel,
        grid=(n // tm,),
        in_specs=[tok, tok, tok, tok, gate(0), gate(1), gate(2), wsp, wsp, wsp, wsp,
                  pl.BlockSpec((1, d), lambda i: (0, 0))],
        out_specs=[tok, tok],
        out_shape=[jax.ShapeDtypeStruct((n, d), F32), jax.ShapeDtypeStruct((n, d), BF16)],
        compiler_params=_params(("parallel",)),
        name="merge",
    )(x, ya, yb, ym, p, p, p, wc, wg, wm, wo, norm_g.reshape(1, d))


TILE_PAD = 4 * SUBLANES
SHIFT_S1, SHIFT_E1, SHIFT_CUT, SHIFT_E0 = 0, SUBLANES, 2 * SUBLANES, 3 * SUBLANES


def _put_tile(ref, h, ns, shift, val, pad=True):
    if pad and shift:
        ref[h, ns, :shift, :] = jnp.zeros((shift, LANES), F32)
    ref[h, ns, shift:shift + PEER_N_KEYS, :] = val
    if pad and shift < TILE_PAD:
        ref[h, ns, shift + PEER_N_KEYS:, :] = jnp.zeros((TILE_PAD - shift, LANES), F32)
_CAND_CELLS = tuple((a, b) for a in range(PEER_TOPK) for b in range(PEER_TOPK) if (a + 1) * (b + 1) <= PEER_TOPK)


def _top_values(x, count):
    vals = []
    cur = x
    for r in range(count):
        m = jnp.max(cur, axis=0, keepdims=True)
        vals.append(m)
        if r + 1 < count:
            cur = jnp.where(cur == m, -jnp.inf, cur)
    return vals


def _top_ranked(x, count):
    n_rows = x.shape[0]
    iota = lax.broadcasted_iota(jnp.int32, x.shape, 0).astype(F32)
    vals = []
    cur = x
    rank = jnp.full(x.shape, float(count), F32)
    for r in range(count):
        m = jnp.max(cur, axis=0, keepdims=True)
        first = jnp.min(jnp.where(cur == m, iota, float(n_rows)), axis=0, keepdims=True)
        one = iota == first
        rank = jnp.where(one, float(r), rank)
        cur = jnp.where(one, -jnp.inf, cur)
        vals.append(m)
    return vals, rank


def _count_rows(mask):
    return jnp.sum(jnp.where(mask, 1.0, 0.0), axis=0, keepdims=True)


def _route_kernel(h_ref, wq_ref, keys_ref, cut_ref, e0_ref, s1_ref, e1_ref, q_scr, tie_ref):
    tn = h_ref.shape[0]
    nsb = tn // LANES
    q = jnp.dot(h_ref[...], wq_ref[...], preferred_element_type=F32).astype(BF16)

    def store(h, cut, e0, s1, e1):
        for ns in range(nsb):
            cols = slice(ns * LANES, (ns + 1) * LANES)
            _put_tile(cut_ref, h, ns, SHIFT_CUT, cut[:, cols])
            _put_tile(e0_ref, h, ns, SHIFT_E0, e0[:, cols])
            _put_tile(s1_ref, h, ns, SHIFT_S1, s1[:, cols])
            _put_tile(e1_ref, h, ns, SHIFT_E1, e1[:, cols])

    for h in range(PEER_HEADS):
        sc, top = [], []
        tied = jnp.zeros((1, tn), F32)
        for part in range(2):
            col = (2 * h + part) * PEER_D_HALF
            q_scr[2 * h + part] = q[:, col:col + PEER_D_HALF]
            s_t = lax.dot_general(keys_ref[part, h], q[:, col:col + PEER_D_HALF],
                                  (((1,), (1,)), ((), ())), preferred_element_type=F32)
            sc.append(s_t)
            top.append(_top_values(s_t, PEER_TOPK))
            tied = jnp.maximum(tied, jnp.abs(_count_rows(s_t >= top[part][-1]) - PEER_TOPK))
        cand = jnp.concatenate([top[0][a] + top[1][b] for a, b in _CAND_CELLS], axis=0)
        best = _top_values(cand, PEER_TOPK)
        c_max, thr = best[0], best[-1]
        z = jnp.exp(best[0] - c_max)
        for r in range(1, PEER_TOPK):
            z = z + jnp.exp(best[r] - c_max)
        top1 = jnp.concatenate(top[1], axis=0)
        n_sel = sum(_count_rows(top[0][a] + top1 >= thr) for a in range(PEER_TOPK))
        tied = jnp.maximum(tied, jnp.abs(n_sel - PEER_TOPK))
        cut = jnp.full(sc[0].shape, jnp.inf, F32)
        for b in range(PEER_TOPK):
            cut = jnp.where(sc[0] + top[1][b] >= thr, top[1][b], cut)
        store(h, cut, jnp.exp(sc[0] - top[0][0]) / z, sc[1], jnp.exp(sc[1] - top[1][0]))
        for ns in range(nsb):
            tie_ref[h * nsb + ns] = jnp.max(tied[:, ns * LANES:(ns + 1) * LANES])

    def exact_block(idx, carry):
        @pl.when(tie_ref[idx] > 0.0)
        def _():
            h = idx // nsb
            ns = idx % nsb
            rows = pl.ds(pl.multiple_of(ns * LANES, LANES), LANES)
            sc, top, rank = [], [], []
            for part in range(2):
                s_t = lax.dot_general(keys_ref[part, h], q_scr[2 * h + part, rows, :],
                                      (((1,), (1,)), ((), ())), preferred_element_type=F32)
                vals, rk = _top_ranked(s_t, PEER_TOPK)
                sc.append(s_t)
                top.append(vals)
                rank.append(rk)
            top1 = jnp.concatenate(top[1], axis=0)
            cand = jnp.concatenate([top[0][a] + top1 for a in range(PEER_TOPK)], axis=0)
            best, pos = _top_ranked(cand, PEER_TOPK)
            z = jnp.exp(best[0] - best[0])
            for r in range(1, PEER_TOPK):
                z = z + jnp.exp(best[r] - best[0])
            cnt = jnp.zeros_like(sc[0])
            for a in range(PEER_TOPK):
                cnt_a = _count_rows(pos[a * PEER_TOPK:(a + 1) * PEER_TOPK] < PEER_TOPK)
                cnt = jnp.where(rank[0] == float(a), cnt_a, cnt)
            _put_tile(cut_ref, h, ns, SHIFT_CUT, jnp.where(cnt > 0.0, 1.0 - cnt, jnp.inf), pad=False)
            _put_tile(e0_ref, h, ns, SHIFT_E0, jnp.exp(sc[0] - top[0][0]) / z, pad=False)
            _put_tile(s1_ref, h, ns, SHIFT_S1, -rank[1], pad=False)
            _put_tile(e1_ref, h, ns, SHIFT_E1, jnp.exp(sc[1] - top[1][0]), pad=False)
        return carry

    lax.fori_loop(0, PEER_HEADS * nsb, exact_block, 0)


def _route(h2, wq, keys, tn):
    n, d = h2.shape
    nk = PEER_N_KEYS
    tiled = pl.BlockSpec((PEER_HEADS, tn // LANES, nk + TILE_PAD, LANES), lambda i: (0, i, 0, 0))
    tiled_shape = jax.ShapeDtypeStruct((PEER_HEADS, n // LANES, nk + TILE_PAD, LANES), F32)
    return pl.pallas_call(
        _route_kernel,
        grid=(n // tn,),
        in_specs=[pl.BlockSpec((tn, d), lambda i: (i, 0)),
                  pl.BlockSpec(wq.shape, lambda i: (0, 0)),
                  pl.BlockSpec(keys.shape, lambda i: (0, 0, 0, 0))],
        out_specs=[tiled, tiled, tiled, tiled],
        out_shape=[tiled_shape, tiled_shape, tiled_shape, tiled_shape],
        scratch_shapes=[pltpu.VMEM((2 * PEER_HEADS, tn, PEER_D_HALF), BF16),
                        pltpu.SMEM((PEER_HEADS * (tn // LANES),), F32)],
        compiler_params=_params(("parallel",)),
        name="peer_route",
    )(h2, wq, keys)


def _gelu(x):
    return 0.5 * x * (1.0 + lax.erf(x * (2.0 ** -0.5)))


PEER_TILE = 1024
GATE_HALF = 32
GATE_ROWS = 8


def _peer_kernel(h_ref, x1_ref, u_ref, vt_ref, cut_ref, e0_ref, s1_ref, e1_ref, ng_ref,
                 y_ref, acc_ref, act_ref, coef_ref, *, te):
    ei = pl.program_id(1)
    tn = h_ref.shape[0]
    groups = te // PEER_N_KEYS

    @pl.when(ei == 0)
    def _():
        acc_ref[...] = jnp.zeros_like(acc_ref)

    nsb = tn // LANES
    act = _gelu(lax.dot_general(u_ref[...], h_ref[...], (((1,), (1,)), ((), ())),
                                preferred_element_type=F32))
    for ns in range(nsb):
        act_ref[ns] = act[:, ns * LANES:(ns + 1) * LANES]

    def gate_block(ns):
        for jh in range(PEER_N_KEYS // GATE_HALF):
            jrows = slice(jh * GATE_HALF, (jh + 1) * GATE_HALF)
            for g0 in range(0, groups, GATE_ROWS):
                gates = [None] * GATE_ROWS
                for h in range(PEER_HEADS):
                    s1_t = s1_ref[h, ns, SHIFT_S1 + jh * GATE_HALF:SHIFT_S1 + (jh + 1) * GATE_HALF, :]
                    e1_t = e1_ref[h, ns, SHIFT_E1 + jh * GATE_HALF:SHIFT_E1 + (jh + 1) * GATE_HALF, :]
                    for k in range(GATE_ROWS):
                        i_key = ei * groups + g0 + k
                        sel = s1_t >= cut_ref[h, ns, pl.ds(SHIFT_CUT + i_key, 1), :]
                        term = jnp.where(sel, e1_t, 0.0) * e0_ref[h, ns, pl.ds(SHIFT_E0 + i_key, 1), :]
                        gates[k] = term if gates[k] is None else gates[k] + term
                for k in range(GATE_ROWS):
                    r0 = (g0 + k) * PEER_N_KEYS + jh * GATE_HALF
                    coef_ref[ns, r0:r0 + GATE_HALF, :] = (gates[k] * act_ref[ns, r0:r0 + GATE_HALF, :]).astype(BF16)

    def lane_block(ns, carry):
        gate_block(ns)
        return carry

    def mix_values(lo, hi):
        coef = jnp.concatenate([coef_ref[ns] for ns in range(lo, hi)], axis=1)
        acc_ref[:, lo * LANES:hi * LANES] += jnp.dot(vt_ref[0], coef, preferred_element_type=F32)

    half = nsb // 2
    lax.fori_loop(0, half, lane_block, 0)
    if half:
        mix_values(0, half)
    for ns in range(half, nsb):
        gate_block(ns)
    mix_values(half, nsb)

    @pl.when(ei == pl.num_programs(1) - 1)
    def _():
        x2 = x1_ref[...] + acc_ref[...].T
        y_ref[...] = (x2 * _rms_scale(x2)) * ng_ref[...]


def _peer(h2, x1, u_bf, vt_bf, cut, e0, s1, e1, norm_g, tn):
    n, d = h2.shape
    ne = u_bf.shape[0]
    te = vt_bf.shape[2]
    nk = PEER_N_KEYS
    tok = pl.BlockSpec((tn, d), lambda i, e: (i, 0))
    tiled = pl.BlockSpec((PEER_HEADS, tn // LANES, nk + TILE_PAD, LANES), lambda i, e: (0, i, 0, 0))
    return pl.pallas_call(
        functools.partial(_peer_kernel, te=te),
        grid=(n // tn, ne // te),
        in_specs=[tok, tok,
                  pl.BlockSpec((te, d), lambda i, e: (e, 0)),
                  pl.BlockSpec((1, d, te), lambda i, e: (e, 0, 0)),
                  tiled, tiled, tiled, tiled,
                  pl.BlockSpec((1, d), lambda i, e: (0, 0))],
        out_specs=tok,
        out_shape=jax.ShapeDtypeStruct((n, d), F32),
        scratch_shapes=[pltpu.VMEM((d, tn), F32), pltpu.VMEM((tn // LANES, te, LANES), F32),
                        pltpu.VMEM((tn // LANES, te, LANES), BF16)],
        compiler_params=_params(("parallel", "arbitrary")),
        name="peer_mix",
    )(h2, x1, u_bf, vt_bf, cut, e0, s1, e1, norm_g.reshape(1, d))


def _tile(n, pref):
    return math.gcd(n, pref)


def _layer(x, mk, mv, bsz, t, weights, conv_state=None, gdn_conv_state=None, gdn_s0=None):
    (norm_mix_g, w_main, w_ab, conv_w, gdn_conv_w, a_log, dt_bias, gdn_norm_g, wc, wg, wm, wo, norm_ffn_g,
     wq, keys, u_bf, vt_bf, final_norm_g) = weights
    n = bsz * t
    sample = conv_state is not None
    p, pab = _inproj(x, norm_mix_g, w_main, w_ab, _tile(n, 1024))
    rows = _tile(n, 256) if sample else _tile(t, 512)
    ya, q, k, v, gb, u_tail = _prep(p, pab, conv_w, gdn_conv_w, a_log, dt_bias, rows, max(t // rows, 1),
                                    conv_state, gdn_conv_state)
    chunk = _tile(t, 64)
    if sample:
        yb, s_new = _gdn(q, k, v, p, gb, gdn_norm_g, bsz, t, chunk, 4, gdn_s0)
    else:
        yb, s_new = _gdn_stacked(q, k, v, p, gb, gdn_norm_g, bsz, t, chunk, _tile(bsz, 2))
    ym = _mem_attn(p, mk, mv, bsz, t, _tile(t, 512), _tile(bsz, 4) if sample else 1)
    x1, h2 = _merge(x, ya, yb, ym, p, wc, wg, wm, wo, norm_ffn_g, _tile(n, 256))
    tn = _tile(n, 512)
    cut, e0, s1, e1 = _route(h2, wq, keys, tn)
    y = _peer(h2, x1, u_bf, vt_bf, cut, e0, s1, e1, final_norm_g, tn)
    return y, u_tail, p, s_new


def kernel(x_prompt, x_sample, mem_prompt, cache_mem_k, cache_mem_v, state_conv, state_gdn_conv, state_gdn, norm_mix_g, w_in, conv_w, gdn_conv_w, gdn_a_log, gdn_dt_bias, gdn_norm_g, mem_norm_g, w_mem_kv, w_br_conv, w_br_gdn, w_br_mem, w_out, norm_ffn_g, peer_w_q, peer_sub_keys, peer_u, peer_v, final_norm_g):
    depth = w_in.shape[0]
    assert depth == 1, "single-layer step"
    d = D_MODEL
    bp, tp, _ = x_prompt.shape
    bs, ts, _ = x_sample.shape
    assert ts == SAMPLE_PAD_T // 2
    m_tok = mem_prompt.shape[1]
    mq = MEM_HEADS * MEM_HEAD_DIM
    l = 0

    w = w_in[l]
    ga0 = 7 * d
    w_main = jnp.concatenate([w[:, :ga0], w[:, ga0 + 2 * GDN_HEADS:]], axis=1).astype(BF16)
    w_ab = jnp.pad(w[:, ga0:ga0 + 2 * GDN_HEADS], ((0, 0), (0, SMALL_COLS - 2 * GDN_HEADS))).astype(BF16)
    weights = (norm_mix_g[l], w_main, w_ab, conv_w[l], gdn_conv_w[l], gdn_a_log[l], gdn_dt_bias[l], gdn_norm_g[l],
               w_br_conv[l].astype(BF16), w_br_gdn[l].astype(BF16), w_br_mem[l].astype(BF16), w_out[l].astype(BF16),
               norm_ffn_g[l], peer_w_q[l].astype(BF16), peer_sub_keys[l].astype(BF16),
               peer_u[l].astype(BF16),
               peer_v[l].astype(BF16).reshape(-1, PEER_TILE, d).transpose(0, 2, 1), final_norm_g)

    kv_shape = (bp, m_tok, mq)
    mk_p, mv_p = _mem_kv(mem_prompt.reshape(bp * m_tok, d), mem_norm_g[l], w_mem_kv[l].astype(BF16),
                         _tile(bp * m_tok, 256))
    y_p, u_tail_p, p_p, s_p = _layer(x_prompt.reshape(bp * tp, d), mk_p.reshape(kv_shape), mv_p.reshape(kv_shape),
                                     bp, tp, weights)
    rows_p = _tile(tp, 512)
    cb_p = u_tail_p.reshape(bp, tp // rows_p, SUBLANES, d)[:, -1, SUBLANES - (CONV_K - 1):, :]
    gb_p = p_p.reshape(bp, tp, -1)[:, tp - (GDN_CONV_K - 1):, COL_GQ * d:(COL_GV + 1) * d].astype(F32)

    pad_t = SAMPLE_PAD_T
    lead = pad_t - ts
    x_s = jnp.pad(x_sample, ((0, 0), (lead, 0), (0, 0))).reshape(bs * pad_t, d)
    cst = jnp.pad(state_conv[l], ((0, 0), (lead - (CONV_K - 1), ts), (0, 0))).reshape(bs * pad_t, d)
    gst = jnp.pad(state_gdn_conv[l], ((0, 0), (lead - (GDN_CONV_K - 1), ts), (0, 0))).reshape(bs * pad_t, 3 * d)
    y_s, u_s, p_s, s_s = _layer(x_s, cache_mem_k.reshape(bs, -1, mq), cache_mem_v.reshape(bs, -1, mq),
                                bs, pad_t, weights, cst, gst, state_gdn.reshape(bs, GDN_HEADS, GDN_DK, GDN_DV))
    y_s = y_s.reshape(bs, pad_t, d)[:, lead:, :]
    cb_s = u_s.reshape(bs, pad_t, d)[:, pad_t - (CONV_K - 1):, :]
    gb_s = p_s.reshape(bs, pad_t, -1)[:, pad_t - (GDN_CONV_K - 1):, COL_GQ * d:(COL_GV + 1) * d].astype(F32)

    mshape = (depth, bp, m_tok, MEM_HEADS, MEM_HEAD_DIM)
    return (y_p.reshape(bp, tp, d), y_s,
            mk_p.reshape(mshape), mv_p.reshape(mshape),
            cb_p[None], gb_p[None], s_p[None],
            cb_s[None], gb_s[None], s_s[None])
```

```python
import functools
import math

import jax
import jax.numpy as jnp
from jax import lax
from jax.experimental import pallas as pl
from jax.experimental.pallas import tpu as pltpu

F32 = jnp.float32
BF16 = jnp.bfloat16

EPS = 1e-6
D_MODEL = 1024
CONV_K = 3
GDN_HEADS = 8
GDN_DK = 128
GDN_DV = 128
GDN_CONV_K = 4
MEM_HEADS = 4
MEM_HEAD_DIM = 256
PEER_HEADS = 8
PEER_N_KEYS = 128
PEER_D_HALF = 128
PEER_TOPK = 16
N_BRANCHES = 3

COL_AU, COL_AB, COL_AC, COL_GQ, COL_GK, COL_GV, COL_GZ, COL_MQ, COL_BG = 0, 1, 2, 3, 4, 5, 6, 7, 8
N_MAIN_BLOCKS = 11
SMALL_COLS = 128
SUBLANES = 8
LANES = 128
SAMPLE_PAD_T = 8
VMEM_LIMIT = 56 * 1024 * 1024


def _params(sem):
    return pltpu.CompilerParams(dimension_semantics=sem, vmem_limit_bytes=VMEM_LIMIT)


def _dot(a, b):
    return jnp.dot(a.astype(BF16), b.astype(BF16), preferred_element_type=F32)


def _dot_nt(a, b):
    return lax.dot_general(a.astype(BF16), b.astype(BF16), (((1,), (1,)), ((), ())),
                           preferred_element_type=F32)


def _rms_scale(x):
    return lax.rsqrt(jnp.mean(x * x, axis=-1, keepdims=True) + EPS)


def _inproj_kernel(x_ref, g_ref, w_ref, wab_ref, p_ref, pab_ref, h_ref):
    @pl.when(pl.program_id(1) == 0)
    def _():
        x = x_ref[...]
        h = ((x * _rms_scale(x)) * g_ref[...]).astype(BF16)
        h_ref[...] = h
        pab_ref[...] = jnp.dot(h, wab_ref[...], preferred_element_type=F32)

    p_ref[...] = jnp.dot(h_ref[...], w_ref[...], preferred_element_type=F32).astype(p_ref.dtype)


def _inproj(x, g, w_main, w_ab, tm):
    n, d = x.shape
    nblk = w_main.shape[1] // d
    return pl.pallas_call(
        _inproj_kernel,
        grid=(n // tm, nblk),
        in_specs=[
            pl.BlockSpec((tm, d), lambda i, j: (i, 0)),
            pl.BlockSpec((1, d), lambda i, j: (0, 0)),
            pl.BlockSpec((d, d), lambda i, j: (0, j)),
            pl.BlockSpec((d, SMALL_COLS), lambda i, j: (0, 0)),
        ],
        out_specs=[
            pl.BlockSpec((tm, d), lambda i, j: (i, j)),
            pl.BlockSpec((tm, SMALL_COLS), lambda i, j: (i, 0)),
        ],
        out_shape=[
            jax.ShapeDtypeStruct((n, nblk * d), BF16),
            jax.ShapeDtypeStruct((n, SMALL_COLS), F32),
        ],
        scratch_shapes=[pltpu.VMEM((tm, d), BF16)],
        compiler_params=_params(("parallel", "arbitrary")),
        name="inproj",
    )(x, g.reshape(1, d), w_main, w_ab)


def _mem_kv_kernel(x_ref, g_ref, w_ref, k_ref, v_ref):
    x = x_ref[...]
    h = ((x * _rms_scale(x)) * g_ref[...]).astype(BF16)
    kv = jnp.dot(h, w_ref[...], preferred_element_type=F32)
    half = kv.shape[1] // 2
    k_ref[...] = kv[:, :half]
    v_ref[...] = kv[:, half:]


def _mem_kv(x, g, w, tm):
    n, d = x.shape
    c = w.shape[1]
    out = jax.ShapeDtypeStruct((n, c // 2), F32)
    ospec = pl.BlockSpec((tm, c // 2), lambda i: (i, 0))
    return pl.pallas_call(
        _mem_kv_kernel,
        grid=(n // tm,),
        in_specs=[
            pl.BlockSpec((tm, d), lambda i: (i, 0)),
            pl.BlockSpec((1, d), lambda i: (0, 0)),
            pl.BlockSpec((d, c), lambda i: (0, 0)),
        ],
        out_specs=[ospec, ospec],
        out_shape=[out, out],
        compiler_params=_params(("parallel",)),
        name="mem_kv",
    )(x, g.reshape(1, d), w)


def _causal_conv(prev, cur, w_ref, k):
    full = jnp.concatenate([prev, cur], axis=0)
    y = cur * w_ref[k - 1:k, :]
    for s in range(1, k):
        y = y + pltpu.roll(full, s, axis=0)[prev.shape[0]:, :] * w_ref[k - 1 - s:k - s, :]
    return y


def _f32(ref):
    return ref[...].astype(F32)


def _prep_kernel(*refs, rows, tiles_per_seq, sample):
    if sample:
        (au, ab, ac, gq, gk, gv, pab, au_p, ac_p, gq_p, gk_p, gv_p, cw, gw, alog, dtb, cst, gst,
         ya_o, q_o, k_o, v_o, gb_o, ut_o) = refs
    else:
        (au, ab, ac, gq, gk, gv, pab, au_p, ac_p, gq_p, gk_p, gv_p, cw, gw, alog, dtb,
         ya_o, q_o, k_o, v_o, gb_o, ut_o) = refs
    d = D_MODEL
    first = (pl.program_id(0) % tiles_per_seq) == 0
    keep = jnp.where(first, 0.0, 1.0).astype(F32)

    u = _f32(ac) * _f32(au)
    u_prev = (_f32(ac_p) * _f32(au_p)) * keep
    if sample:
        real = (lax.broadcasted_iota(jnp.int32, (rows, 1), 0) % SAMPLE_PAD_T) >= (SAMPLE_PAD_T // 2)
        u = jnp.where(real, u, cst[...])
    ya_o[...] = (_f32(ab) * _causal_conv(u_prev, u, cw, CONV_K)).astype(BF16)
    ut_o[...] = u[rows - ut_o.shape[0]:, :]

    outs = (q_o, k_o, v_o)
    for idx, (cur_ref, prev_ref) in enumerate(((gq, gq_p), (gk, gk_p), (gv, gv_p))):
        cur = _f32(cur_ref)
        if sample:
            cur = jnp.where(real, cur, gst[:, idx * d:(idx + 1) * d])
        prev = _f32(prev_ref) * keep
        wslab = gw.at[:, idx * d:(idx + 1) * d]
        y = _causal_conv(prev, cur, wslab, GDN_CONV_K)
        y = y * jax.nn.sigmoid(y)
        if idx < 2:
            scale = GDN_DK ** -0.5 if idx == 0 else 1.0
            for h in range(GDN_HEADS):
                yh = y[:, h * GDN_DK:(h + 1) * GDN_DK]
                r = lax.rsqrt(jnp.sum(yh * yh, axis=-1, keepdims=True) + EPS)
                outs[idx][:, h * GDN_DK:(h + 1) * GDN_DK] = yh * (r * scale)
        else:
            outs[idx][...] = y

    pa = pab[...]
    lane = lax.broadcasted_iota(jnp.int32, pa.shape, 1)
    z = pa + dtb[...]
    softplus = jnp.maximum(z, 0.0) + jnp.log1p(jnp.exp(-jnp.abs(z)))
    g_log = -jnp.exp(alog[...]) * softplus
    gb = jnp.where(lane < GDN_HEADS, g_log, jnp.where(lane < 2 * GDN_HEADS, jax.nn.sigmoid(pa), 0.0))
    if sample:
        gb = jnp.where(real, gb, 0.0)
    gb_o[...] = gb


def _prep(p, pab, conv_w, gdn_conv_w, a_log, dt_bias, rows, tiles_per_seq, conv_state=None, gdn_state=None):
    n = p.shape[0]
    d = D_MODEL
    sample = conv_state is not None
    nt = n // rows
    prev_rows = 2 * SUBLANES
    rpb = rows // prev_rows

    def cur(c):
        return pl.BlockSpec((rows, d), lambda i, c=c: (i, c))

    def prev(c):
        return pl.BlockSpec((prev_rows, d), lambda i, c=c: (jnp.maximum(i * rpb - 1, 0), c))

    alog128 = jnp.zeros((1, SMALL_COLS), F32).at[0, :GDN_HEADS].set(a_log)
    dtb128 = jnp.zeros((1, SMALL_COLS), F32).at[0, :GDN_HEADS].set(dt_bias)
    in_specs = [cur(COL_AU), cur(COL_AB), cur(COL_AC), cur(COL_GQ), cur(COL_GK), cur(COL_GV),
                pl.BlockSpec((rows, SMALL_COLS), lambda i: (i, 0)),
                prev(COL_AU), prev(COL_AC), prev(COL_GQ), prev(COL_GK), prev(COL_GV),
                pl.BlockSpec((CONV_K, d), lambda i: (0, 0)),
                pl.BlockSpec((GDN_CONV_K, 3 * d), lambda i: (0, 0)),
                pl.BlockSpec((1, SMALL_COLS), lambda i: (0, 0)),
                pl.BlockSpec((1, SMALL_COLS), lambda i: (0, 0))]
    args = [p, p, p, p, p, p, pab, p, p, p, p, p, conv_w, gdn_conv_w, alog128, dtb128]
    if sample:
        in_specs += [pl.BlockSpec((rows, d), lambda i: (i, 0)), pl.BlockSpec((rows, 3 * d), lambda i: (i, 0))]
        args += [conv_state, gdn_state]
    tail = rows if sample else SUBLANES
    out_specs = [pl.BlockSpec((rows, d), lambda i: (i, 0))] * 4 + [
        pl.BlockSpec((rows, SMALL_COLS), lambda i: (i, 0)),
        pl.BlockSpec((tail, d), lambda i: (i, 0))]
    out_shape = [jax.ShapeDtypeStruct((n, d), BF16)] + [jax.ShapeDtypeStruct((n, d), F32)] * 3 + [
        jax.ShapeDtypeStruct((n, SMALL_COLS), F32),
        jax.ShapeDtypeStruct((nt * tail, d), F32)]
    return pl.pallas_call(
        functools.partial(_prep_kernel, rows=rows, tiles_per_seq=tiles_per_seq, sample=sample),
        grid=(nt,),
        in_specs=in_specs,
        out_specs=out_specs,
        out_shape=out_shape,
        compiler_params=_params(("parallel",)),
        name="prep",
    )(*args)


def _chunk_cumsum(g, c):
    row = lax.broadcasted_iota(jnp.int32, g.shape, 0)
    sh = 1
    while sh < c:
        g = g + jnp.where(row >= sh, pltpu.roll(g, sh, axis=0), 0.0)
        sh *= 2
    return g


def _gdn_stacked_kernel(q_ref, k_ref, v_ref, z_ref, gb_ref, ng_ref, yb_ref, sout_ref, s_scr, *, c, grp, nb):
    ci = pl.program_id(1)

    @pl.when(ci == 0)
    def _():
        s_scr[...] = jnp.zeros_like(s_scr)

    r = grp * c
    ng = ng_ref[...]
    seqs = []
    for s in range(nb):
        gb = gb_ref[s]
        gc = _chunk_cumsum(gb, c)
        g_last = gc[c - 1:c, :]
        seqs.append(dict(gb=gb, gc=gc, gct=gc.T, e_g=jnp.exp(gc), e_kd=jnp.exp(g_last - gc), e_last=jnp.exp(g_last)))

    ri = lax.broadcasted_iota(jnp.int32, (r, r), 0)
    cj = lax.broadcasted_iota(jnp.int32, (r, r), 1)
    same = (ri // c) == (cj // c)
    incl = same & (ri >= cj)
    strict = same & (ri > cj)
    head_block = (lax.broadcasted_iota(jnp.int32, (r, grp * GDN_DK), 0) // c
                  == lax.broadcasted_iota(jnp.int32, (r, grp * GDN_DK), 1) // GDN_DK)

    def block_layout(x):
        return jnp.where(head_block, jnp.concatenate([x] * grp, axis=1), 0.0)

    units = []
    for s in range(nb):
        for g0 in range(0, GDN_HEADS, grp):
            heads = list(range(g0, g0 + grp))
            sq = seqs[s]

            def stack(ref, s=s, heads=heads):
                return jnp.concatenate([ref[s, :, h * GDN_DK:(h + 1) * GDN_DK] for h in heads], axis=0)

            def col(a, off=0, heads=heads):
                return jnp.concatenate([a[:, off + h:off + h + 1] for h in heads], axis=0)

            qs, ks, vs = stack(q_ref), stack(k_ref), stack(v_ref)
            beta = col(sq["gb"], GDN_HEADS)
            kb = ks * beta
            units.append(dict(s=s, heads=heads, qs=qs, ks=ks, kb=kb, vb=vs * beta,
                              gcol=col(sq["gc"]), egcol=col(sq["e_g"]), ekdcol=col(sq["e_kd"]),
                              grow=jnp.concatenate([sq["gct"][h:h + 1, :] for h in heads], axis=1),
                              rows=slice((s * GDN_HEADS + g0) * GDN_DK, (s * GDN_HEADS + g0 + grp) * GDN_DK)))

    for u in units:
        u["aq"] = _dot_nt(jnp.concatenate([u["kb"], u["qs"]], axis=0), u["ks"])
    for u in units:
        decay = jnp.exp(jnp.where(incl, u["gcol"] - u["grow"], -jnp.inf))
        u["pw"] = -jnp.where(strict, u["aq"][:r] * decay, 0.0)
        u["attn"] = jnp.where(incl, u["aq"][r:] * decay, 0.0)
        u["x"] = jnp.concatenate([u["vb"], u["kb"] * u["egcol"]], axis=1)
    span = 1
    while span < c:
        span *= 2
        for u in units:
            if span < c:
                both = _dot(u["pw"], jnp.concatenate([u["pw"], u["x"]], axis=1))
                u["pw"] = both[:, :r]
                u["x"] = u["x"] + both[:, r:]
            else:
                u["x"] = u["x"] + _dot(u["pw"], u["x"])
    for u in units:
        u_val, w_k = u["x"][:, :GDN_DV], u["x"][:, GDN_DV:]
        u["s_old"] = s_scr[u["rows"], :]
        ws = _dot(jnp.concatenate([block_layout(w_k), block_layout(u["qs"] * u["egcol"])], axis=0), u["s_old"])
        u["v_new"] = u_val - ws[:r]
        u["o"] = ws[r:]
    for u in units:
        e_last = seqs[u["s"]]["e_last"]
        o = u["o"] + _dot(u["attn"], u["v_new"])
        e_rows = jnp.concatenate([jnp.broadcast_to(e_last[:, h:h + 1], (GDN_DK, 1)) for h in u["heads"]], axis=0)
        kd_t = block_layout(u["ks"] * u["ekdcol"]).T
        s_scr[u["rows"], :] = u["s_old"] * e_rows + _dot(kd_t, u["v_new"])
        for i, h in enumerate(u["heads"]):
            sl = slice(h * GDN_DK, (h + 1) * GDN_DK)
            oh = o[i * c:(i + 1) * c, :]
            zh = z_ref[u["s"], :, sl].astype(F32)
            yb_ref[u["s"], :, sl] = (((oh * _rms_scale(oh)) * ng) * (zh * jax.nn.sigmoid(zh))).astype(BF16)

    @pl.when(ci == pl.num_programs(1) - 1)
    def _():
        sout_ref[...] = s_scr[...]


def _gdn_stacked(q, k, v, p, gb, norm_g, bsz, t, c, nb):
    d = D_MODEL
    nc = t // c
    grp = min(GDN_HEADS, 256 // c)
    s_rows = nb * GDN_HEADS * GDN_DK

    def tok(col=0):
        return pl.BlockSpec((nb, c, d), lambda b, i, col=col: (b, i, col))

    yb, s_new = pl.pallas_call(
        functools.partial(_gdn_stacked_kernel, c=c, grp=grp, nb=nb),
        grid=(bsz // nb, nc),
        in_specs=[tok(), tok(), tok(), tok(COL_GZ),
                  pl.BlockSpec((nb, c, SMALL_COLS), lambda b, i: (b, i, 0)),
                  pl.BlockSpec((1, GDN_DV), lambda b, i: (0, 0))],
        out_specs=[tok(), pl.BlockSpec((s_rows, GDN_DV), lambda b, i: (b, 0))],
        out_shape=[jax.ShapeDtypeStruct((bsz, t, d), BF16),
                   jax.ShapeDtypeStruct((bsz * GDN_HEADS * GDN_DK, GDN_DV), F32)],
        scratch_shapes=[pltpu.VMEM((s_rows, GDN_DV), F32)],
        compiler_params=_params(("parallel", "arbitrary")),
        name="gdn_scan_prompt",
    )(*[a.reshape(bsz, t, -1) for a in (q, k, v, p, gb)], norm_g.reshape(1, GDN_DV))
    return yb.reshape(bsz * t, d), s_new.reshape(bsz, GDN_HEADS, GDN_DK, GDN_DV)


def _gdn_kernel(*refs, c, nb, has_s0):
    if has_s0:
        q_ref, k_ref, v_ref, z_ref, gb_ref, ng_ref, s0_ref, yb_ref, sout_ref, s_scr = refs
    else:
        q_ref, k_ref, v_ref, z_ref, gb_ref, ng_ref, yb_ref, sout_ref, s_scr = refs
    ci = pl.program_id(1)

    @pl.when(ci == 0)
    def _():
        if has_s0:
            s_scr[...] = s0_ref[...]
        else:
            s_scr[...] = jnp.zeros_like(s_scr)

    ri = lax.broadcasted_iota(jnp.int32, (c, c), 0)
    cj = lax.broadcasted_iota(jnp.int32, (c, c), 1)
    incl = ri >= cj
    strict = ri > cj
    ng = ng_ref[...]
    row = lax.broadcasted_iota(jnp.int32, (c, SMALL_COLS), 0)
    chains = [(s, h) for s in range(nb) for h in range(GDN_HEADS)]

    seq = []
    for s in range(nb):
        gb = gb_ref[s]
        gc = gb
        sh = 1
        while sh < c:
            gc = gc + jnp.where(row >= sh, pltpu.roll(gc, sh, axis=0), 0.0)
            sh *= 2
        g_last = gc[c - 1:c, :]
        seq.append(dict(gb=gb, gc=gc, gct=gc.T, e_g=jnp.exp(gc), e_kd=jnp.exp(g_last - gc), e_last=jnp.exp(g_last)))

    st = []
    for s, h in chains:
        sl = slice(h * GDN_DK, (h + 1) * GDN_DK)
        sq = seq[s]
        qh, kh, vh = q_ref[s, :, sl], k_ref[s, :, sl], v_ref[s, :, sl]
        beta = sq["gb"][:, GDN_HEADS + h:GDN_HEADS + h + 1]
        kb = kh * beta
        aq = _dot_nt(jnp.concatenate([kb, qh], axis=0), kh)
        st.append(dict(sl=sl, qh=qh, kh=kh, kb=kb, vb=vh * beta, aq=aq))

    for (s, h), e in zip(chains, st):
        sq = seq[s]
        decay = jnp.exp(jnp.where(incl, sq["gc"][:, h:h + 1] - sq["gct"][h:h + 1, :], -jnp.inf))
        e["neg"] = -jnp.where(strict, e["aq"][:c] * decay, 0.0)
        e["attn"] = jnp.where(incl, e["aq"][c:] * decay, 0.0)
        e["x"] = jnp.concatenate([e["vb"], e["kb"] * sq["e_g"][:, h:h + 1]], axis=1)
        e["pw"] = e["neg"]

    span = 1
    while span < c:
        span *= 2
        for e in st:
            if span < c:
                both = _dot(e["pw"], jnp.concatenate([e["pw"], e["x"]], axis=1))
                e["pw"] = both[:, :c]
                e["x"] = e["x"] + both[:, c:]
            else:
                e["x"] = e["x"] + _dot(e["pw"], e["x"])

    for (s, h), e in zip(chains, st):
        sq = seq[s]
        u_val, w_k = e["x"][:, :GDN_DV], e["x"][:, GDN_DV:]
        q_dec = e["qh"] * sq["e_g"][:, h:h + 1]
        e["s_old"] = s_scr[s, h]
        ws = _dot(jnp.concatenate([w_k, q_dec], axis=0), e["s_old"])
        e["v_new"] = u_val - ws[:c]
        e["o_inter"] = ws[c:]

    for (s, h), e in zip(chains, st):
        sq = seq[s]
        o = e["o_inter"] + _dot(e["attn"], e["v_new"])
        k_dec = e["kh"] * sq["e_kd"][:, h:h + 1]
        s_scr[s, h] = e["s_old"] * sq["e_last"][:, h:h + 1] + _dot(k_dec.T, e["v_new"])
        zh = z_ref[s, :, e["sl"]].astype(F32)
        yb_ref[s, :, e["sl"]] = (((o * _rms_scale(o)) * ng) * (zh * jax.nn.sigmoid(zh))).astype(BF16)

    @pl.when(ci == pl.num_programs(1) - 1)
    def _():
        sout_ref[...] = s_scr[...]


def _gdn(q, k, v, p, gb, norm_g, bsz, t, c, nb, s0=None):
    d = D_MODEL
    nc = t // c
    has_s0 = s0 is not None

    def tok(col=0):
        return pl.BlockSpec((nb, c, d), lambda b, i, col=col: (b, i, col))

    in_specs = [tok(), tok(), tok(), tok(COL_GZ),
                pl.BlockSpec((nb, c, SMALL_COLS), lambda b, i: (b, i, 0)),
                pl.BlockSpec((1, GDN_DV), lambda b, i: (0, 0))]
    args = [a.reshape(bsz, t, -1) for a in (q, k, v, p, gb)] + [norm_g.reshape(1, GDN_DV)]
    s_spec = pl.BlockSpec((nb, GDN_HEADS, GDN_DK, GDN_DV), lambda b, i: (b, 0, 0, 0))
    if has_s0:
        in_specs.append(s_spec)
        args.append(s0)
    yb, s_new = pl.pallas_call(
        functools.partial(_gdn_kernel, c=c, nb=nb, has_s0=has_s0),
        grid=(bsz // nb, nc),
        in_specs=in_specs,
        out_specs=[tok(), s_spec],
        out_shape=[jax.ShapeDtypeStruct((bsz, t, d), BF16),
                   jax.ShapeDtypeStruct((bsz, GDN_HEADS, GDN_DK, GDN_DV), F32)],
        scratch_shapes=[pltpu.VMEM((nb, GDN_HEADS, GDN_DK, GDN_DV), F32)],
        compiler_params=_params(("parallel", "arbitrary")),
        name="gdn_scan",
    )(*args)
    return yb.reshape(bsz * t, d), s_new


def _mem_attn_kernel(q_ref, k_ref, v_ref, o_ref, *, nb, tq):
    scale = MEM_HEAD_DIM ** -0.5
    pairs = [(s, h) for s in range(nb) for h in range(MEM_HEADS)]
    span = lambda s, h: (slice(s * tq, (s + 1) * tq), slice(h * MEM_HEAD_DIM, (h + 1) * MEM_HEAD_DIM))
    scores = [_dot_nt(q_ref[span(s, h)], k_ref[s, :, span(s, h)[1]]) * scale for s, h in pairs]
    probs = []
    for sc in scores:
        e = jnp.exp(sc - jnp.max(sc, axis=-1, keepdims=True))
        probs.append(e / jnp.sum(e, axis=-1, keepdims=True))
    for (s, h), prob in zip(pairs, probs):
        o_ref[span(s, h)] = _dot(prob, v_ref[s, :, span(s, h)[1]]).astype(BF16)


def _mem_attn(p, mk, mv, bsz, t, tq, nb):
    d = D_MODEL
    m = mk.shape[1]
    nq = t // tq
    assert nb == 1 or nq == 1
    kv_spec = pl.BlockSpec((nb, m, d), lambda b, i: (b, 0, 0))
    return pl.pallas_call(
        functools.partial(_mem_attn_kernel, nb=nb, tq=tq),
        grid=(bsz // nb, nq),
        in_specs=[pl.BlockSpec((nb * tq, d), lambda b, i: (b * nq + i, COL_MQ)), kv_spec, kv_spec],
        out_specs=pl.BlockSpec((nb * tq, d), lambda b, i: (b * nq + i, 0)),
        out_shape=jax.ShapeDtypeStruct((bsz * t, d), BF16),
        compiler_params=_params(("parallel", "arbitrary")),
        name="mem_attn",
    )(p, mk, mv)


def _merge_kernel(x_ref, ya_ref, yb_ref, ym_ref, g0_ref, g1_ref, g2_ref, wc_ref, wg_ref, wm_ref, wo_ref,
                  ng_ref, x1_ref, h2_ref):
    d = functools.partial(jnp.dot, preferred_element_type=F32)
    merged = (jax.nn.sigmoid(_f32(g0_ref)) * d(ya_ref[...], wc_ref[...])
              + jax.nn.sigmoid(_f32(g1_ref)) * d(yb_ref[...], wg_ref[...])
              + jax.nn.sigmoid(_f32(g2_ref)) * d(ym_ref[...], wm_ref[...]))
    x1 = x_ref[...] + d(merged.astype(BF16), wo_ref[...])
    x1_ref[...] = x1
    h2_ref[...] = ((x1 * _rms_scale(x1)) * ng_ref[...]).astype(BF16)


def _merge(x, ya, yb, ym, p, wc, wg, wm, wo, norm_g, tm):
    n, d = x.shape
    tok = pl.BlockSpec((tm, d), lambda i: (i, 0))
    wsp = pl.BlockSpec((d, d), lambda i: (0, 0))

    def gate(j):
        return pl.BlockSpec((tm, d), lambda i, j=j: (i, COL_BG + j))

    return pl.pallas_call(
        _merge_kernel,
        grid=(n // tm,),
        in_specs=[tok, tok, tok, tok, gate(0), gate(1), gate(2), wsp, wsp, wsp, wsp,
                  pl.BlockSpec((1, d), lambda i: (0, 0))],
        out_specs=[tok, tok],
        out_shape=[jax.ShapeDtypeStruct((n, d), F32), jax.ShapeDtypeStruct((n, d), BF16)],
        compiler_params=_params(("parallel",)),
        name="merge",
    )(x, ya, yb, ym, p, p, p, wc, wg, wm, wo, norm_g.reshape(1, d))


TILE_PAD = 4 * SUBLANES
SHIFT_S1, SHIFT_E1, SHIFT_CUT, SHIFT_E0 = 0, SUBLANES, 2 * SUBLANES, 3 * SUBLANES


def _put_tile(ref, h, ns, shift, val, pad=True):
    if pad and shift:
        ref[h, ns, :shift, :] = jnp.zeros((shift, LANES), F32)
    ref[h, ns, shift:shift + PEER_N_KEYS, :] = val
    if pad and shift < TILE_PAD:
        ref[h, ns, shift + PEER_N_KEYS:, :] = jnp.zeros((TILE_PAD - shift, LANES), F32)
_CAND_CELLS = tuple((a, b) for a in range(PEER_TOPK) for b in range(PEER_TOPK) if (a + 1) * (b + 1) <= PEER_TOPK)


def _top_values(x, count):
    vals = []
    cur = x
    for r in range(count):
        m = jnp.max(cur, axis=0, keepdims=True)
        vals.append(m)
        if r + 1 < count:
            cur = jnp.where(cur == m, -jnp.inf, cur)
    return vals


def _top_ranked(x, count):
    n_rows = x.shape[0]
    iota = lax.broadcasted_iota(jnp.int32, x.shape, 0).astype(F32)
    vals = []
    cur = x
    rank = jnp.full(x.shape, float(count), F32)
    for r in range(count):
        m = jnp.max(cur, axis=0, keepdims=True)
        first = jnp.min(jnp.where(cur == m, iota, float(n_rows)), axis=0, keepdims=True)
        one = iota == first
        rank = jnp.where(one, float(r), rank)
        cur = jnp.where(one, -jnp.inf, cur)
        vals.append(m)
    return vals, rank


def _count_rows(mask):
    return jnp.sum(jnp.where(mask, 1.0, 0.0), axis=0, keepdims=True)


def _route_kernel(h_ref, wq_ref, keys_ref, cut_ref, e0_ref, s1_ref, e1_ref, q_scr, tie_ref):
    tn = h_ref.shape[0]
    nsb = tn // LANES
    q = jnp.dot(h_ref[...], wq_ref[...], preferred_element_type=F32).astype(BF16)

    def store(h, cut, e0, s1, e1):
        for ns in range(nsb):
            cols = slice(ns * LANES, (ns + 1) * LANES)
            _put_tile(cut_ref, h, ns, SHIFT_CUT, cut[:, cols])
            _put_tile(e0_ref, h, ns, SHIFT_E0, e0[:, cols])
            _put_tile(s1_ref, h, ns, SHIFT_S1, s1[:, cols])
            _put_tile(e1_ref, h, ns, SHIFT_E1, e1[:, cols])

    for h in range(PEER_HEADS):
        sc, top = [], []
        tied = jnp.zeros((1, tn), F32)
        for part in range(2):
            col = (2 * h + part) * PEER_D_HALF
            q_scr[2 * h + part] = q[:, col:col + PEER_D_HALF]
            s_t = lax.dot_general(keys_ref[part, h], q[:, col:col + PEER_D_HALF],
                                  (((1,), (1,)), ((), ())), preferred_element_type=F32)
            sc.append(s_t)
            top.append(_top_values(s_t, PEER_TOPK))
            tied = jnp.maximum(tied, jnp.abs(_count_rows(s_t >= top[part][-1]) - PEER_TOPK))
        cand = jnp.concatenate([top[0][a] + top[1][b] for a, b in _CAND_CELLS], axis=0)
        best = _top_values(cand, PEER_TOPK)
        c_max, thr = best[0], best[-1]
        z = jnp.exp(best[0] - c_max)
        for r in range(1, PEER_TOPK):
            z = z + jnp.exp(best[r] - c_max)
        top1 = jnp.concatenate(top[1], axis=0)
        n_sel = sum(_count_rows(top[0][a] + top1 >= thr) for a in range(PEER_TOPK))
        tied = jnp.maximum(tied, jnp.abs(n_sel - PEER_TOPK))
        cut = jnp.full(sc[0].shape, jnp.inf, F32)
        for b in range(PEER_TOPK):
            cut = jnp.where(sc[0] + top[1][b] >= thr, top[1][b], cut)
        store(h, cut, jnp.exp(sc[0] - top[0][0]) / z, sc[1], jnp.exp(sc[1] - top[1][0]))
        for ns in range(nsb):
            tie_ref[h * nsb + ns] = jnp.max(tied[:, ns * LANES:(ns + 1) * LANES])

    def exact_block(idx, carry):
        @pl.when(tie_ref[idx] > 0.0)
        def _():
            h = idx // nsb
            ns = idx % nsb
            rows = pl.ds(pl.multiple_of(ns * LANES, LANES), LANES)
            sc, top, rank = [], [], []
            for part in range(2):
                s_t = lax.dot_general(keys_ref[part, h], q_scr[2 * h + part, rows, :],
                                      (((1,), (1,)), ((), ())), preferred_element_type=F32)
                vals, rk = _top_ranked(s_t, PEER_TOPK)
                sc.append(s_t)
                top.append(vals)
                rank.append(rk)
            top1 = jnp.concatenate(top[1], axis=0)
            cand = jnp.concatenate([top[0][a] + top1 for a in range(PEER_TOPK)], axis=0)
            best, pos = _top_ranked(cand, PEER_TOPK)
            z = jnp.exp(best[0] - best[0])
            for r in range(1, PEER_TOPK):
                z = z + jnp.exp(best[r] - best[0])
            cnt = jnp.zeros_like(sc[0])
            for a in range(PEER_TOPK):
                cnt_a = _count_rows(pos[a * PEER_TOPK:(a + 1) * PEER_TOPK] < PEER_TOPK)
                cnt = jnp.where(rank[0] == float(a), cnt_a, cnt)
            _put_tile(cut_ref, h, ns, SHIFT_CUT, jnp.where(cnt > 0.0, 1.0 - cnt, jnp.inf), pad=False)
            _put_tile(e0_ref, h, ns, SHIFT_E0, jnp.exp(sc[0] - top[0][0]) / z, pad=False)
            _put_tile(s1_ref, h, ns, SHIFT_S1, -rank[1], pad=False)
            _put_tile(e1_ref, h, ns, SHIFT_E1, jnp.exp(sc[1] - top[1][0]), pad=False)
        return carry

    lax.fori_loop(0, PEER_HEADS * nsb, exact_block, 0)


def _route(h2, wq, keys, tn):
    n, d = h2.shape
    nk = PEER_N_KEYS
    tiled = pl.BlockSpec((PEER_HEADS, tn // LANES, nk + TILE_PAD, LANES), lambda i: (0, i, 0, 0))
    tiled_shape = jax.ShapeDtypeStruct((PEER_HEADS, n // LANES, nk + TILE_PAD, LANES), F32)
    return pl.pallas_call(
        _route_kernel,
        grid=(n // tn,),
        in_specs=[pl.BlockSpec((tn, d), lambda i: (i, 0)),
                  pl.BlockSpec(wq.shape, lambda i: (0, 0)),
                  pl.BlockSpec(keys.shape, lambda i: (0, 0, 0, 0))],
        out_specs=[tiled, tiled, tiled, tiled],
        out_shape=[tiled_shape, tiled_shape, tiled_shape, tiled_shape],
        scratch_shapes=[pltpu.VMEM((2 * PEER_HEADS, tn, PEER_D_HALF), BF16),
                        pltpu.SMEM((PEER_HEADS * (tn // LANES),), F32)],
        compiler_params=_params(("parallel",)),
        name="peer_route",
    )(h2, wq, keys)


def _gelu(x):
    return 0.5 * x * (1.0 + lax.erf(x * (2.0 ** -0.5)))


PEER_TILE = 2048
GATE_HALF = 32
GATE_ROWS = 8


def _peer_kernel(h_ref, x1_ref, u_ref, vt_ref, cut_ref, e0_ref, s1_ref, e1_ref, ng_ref,
                 y_ref, acc_ref, act_ref, coef_ref, *, te):
    ei = pl.program_id(1)
    tn = h_ref.shape[0]
    groups = te // PEER_N_KEYS

    @pl.when(ei == 0)
    def _():
        acc_ref[...] = jnp.zeros_like(acc_ref)

    nsb = tn // LANES
    act = _gelu(lax.dot_general(u_ref[...], h_ref[...], (((1,), (1,)), ((), ())),
                                preferred_element_type=F32))
    for ns in range(nsb):
        act_ref[ns] = act[:, ns * LANES:(ns + 1) * LANES]

    def gate_block(ns):
        for jh in range(PEER_N_KEYS // GATE_HALF):
            jrows = slice(jh * GATE_HALF, (jh + 1) * GATE_HALF)
            for g0 in range(0, groups, GATE_ROWS):
                gates = [None] * GATE_ROWS
                for h in range(PEER_HEADS):
                    s1_t = s1_ref[h, ns, SHIFT_S1 + jh * GATE_HALF:SHIFT_S1 + (jh + 1) * GATE_HALF, :]
                    e1_t = e1_ref[h, ns, SHIFT_E1 + jh * GATE_HALF:SHIFT_E1 + (jh + 1) * GATE_HALF, :]
                    for k in range(GATE_ROWS):
                        i_key = ei * groups + g0 + k
                        sel = s1_t >= cut_ref[h, ns, pl.ds(SHIFT_CUT + i_key, 1), :]
                        term = jnp.where(sel, e1_t, 0.0) * e0_ref[h, ns, pl.ds(SHIFT_E0 + i_key, 1), :]
                        gates[k] = term if gates[k] is None else gates[k] + term
                for k in range(GATE_ROWS):
                    r0 = (g0 + k) * PEER_N_KEYS + jh * GATE_HALF
                    coef_ref[ns, r0:r0 + GATE_HALF, :] = (gates[k] * act_ref[ns, r0:r0 + GATE_HALF, :]).astype(BF16)

    def lane_block(ns, carry):
        gate_block(ns)
        return carry

    def mix_values(lo, hi):
        coef = jnp.concatenate([coef_ref[ns] for ns in range(lo, hi)], axis=1)
        acc_ref[:, lo * LANES:hi * LANES] += jnp.dot(vt_ref[0], coef, preferred_element_type=F32)

    half = nsb // 2
    lax.fori_loop(0, half, lane_block, 0)
    if half:
        mix_values(0, half)
    for ns in range(half, nsb):
        gate_block(ns)
    mix_values(half, nsb)

    @pl.when(ei == pl.num_programs(1) - 1)
    def _():
        x2 = x1_ref[...] + acc_ref[...].T
        y_ref[...] = (x2 * _rms_scale(x2)) * ng_ref[...]


def _peer(h2, x1, u_bf, vt_bf, cut, e0, s1, e1, norm_g, tn):
    n, d = h2.shape
    ne = u_bf.shape[0]
    te = vt_bf.shape[2]
    nk = PEER_N_KEYS
    tok = pl.BlockSpec((tn, d), lambda i, e: (i, 0))
    tiled = pl.BlockSpec((PEER_HEADS, tn // LANES, nk + TILE_PAD, LANES), lambda i, e: (0, i, 0, 0),
                         pipeline_mode=pl.Buffered(1))
    return pl.pallas_call(
        functools.partial(_peer_kernel, te=te),
        grid=(n // tn, ne // te),
        in_specs=[tok, tok,
                  pl.BlockSpec((te, d), lambda i, e: (e, 0)),
                  pl.BlockSpec((1, d, te), lambda i, e: (e, 0, 0)),
                  tiled, tiled, tiled, tiled,
                  pl.BlockSpec((1, d), lambda i, e: (0, 0))],
        out_specs=tok,
        out_shape=jax.ShapeDtypeStruct((n, d), F32),
        scratch_shapes=[pltpu.VMEM((d, tn), F32), pltpu.VMEM((tn // LANES, te, LANES), F32),
                        pltpu.VMEM((tn // LANES, te, LANES), BF16)],
        compiler_params=_params(("parallel", "arbitrary")),
        name="peer_mix",
    )(h2, x1, u_bf, vt_bf, cut, e0, s1, e1, norm_g.reshape(1, d))


def _tile(n, pref):
    return math.gcd(n, pref)


def _layer(x, mk, mv, bsz, t, weights, conv_state=None, gdn_conv_state=None, gdn_s0=None):
    (norm_mix_g, w_main, w_ab, conv_w, gdn_conv_w, a_log, dt_bias, gdn_norm_g, wc, wg, wm, wo, norm_ffn_g,
     wq, keys, u_bf, vt_bf, final_norm_g) = weights
    n = bsz * t
    sample = conv_state is not None
    p, pab = _inproj(x, norm_mix_g, w_main, w_ab, _tile(n, 1024))
    rows = _tile(n, 256) if sample else _tile(t, 512)
    ya, q, k, v, gb, u_tail = _prep(p, pab, conv_w, gdn_conv_w, a_log, dt_bias, rows, max(t // rows, 1),
                                    conv_state, gdn_conv_state)
    chunk = _tile(t, 64)
    if sample:
        yb, s_new = _gdn(q, k, v, p, gb, gdn_norm_g, bsz, t, chunk, 4, gdn_s0)
    else:
        yb, s_new = _gdn_stacked(q, k, v, p, gb, gdn_norm_g, bsz, t, chunk, _tile(bsz, 2))
    ym = _mem_attn(p, mk, mv, bsz, t, _tile(t, 512), _tile(bsz, 4) if sample else 1)
    x1, h2 = _merge(x, ya, yb, ym, p, wc, wg, wm, wo, norm_ffn_g, _tile(n, 256))
    tn = _tile(n, 512)
    cut, e0, s1, e1 = _route(h2, wq, keys, tn)
    y = _peer(h2, x1, u_bf, vt_bf, cut, e0, s1, e1, final_norm_g, tn)
    return y, u_tail, p, s_new


def kernel(x_prompt, x_sample, mem_prompt, cache_mem_k, cache_mem_v, state_conv, state_gdn_conv, state_gdn, norm_mix_g, w_in, conv_w, gdn_conv_w, gdn_a_log, gdn_dt_bias, gdn_norm_g, mem_norm_g, w_mem_kv, w_br_conv, w_br_gdn, w_br_mem, w_out, norm_ffn_g, peer_w_q, peer_sub_keys, peer_u, peer_v, final_norm_g):
    depth = w_in.shape[0]
    assert depth == 1, "single-layer step"
    d = D_MODEL
    bp, tp, _ = x_prompt.shape
    bs, ts, _ = x_sample.shape
    assert ts == SAMPLE_PAD_T // 2
    m_tok = mem_prompt.shape[1]
    mq = MEM_HEADS * MEM_HEAD_DIM
    l = 0

    w = w_in[l]
    ga0 = 7 * d
    w_main = jnp.concatenate([w[:, :ga0], w[:, ga0 + 2 * GDN_HEADS:]], axis=1).astype(BF16)
    w_ab = jnp.pad(w[:, ga0:ga0 + 2 * GDN_HEADS], ((0, 0), (0, SMALL_COLS - 2 * GDN_HEADS))).astype(BF16)
    weights = (norm_mix_g[l], w_main, w_ab, conv_w[l], gdn_conv_w[l], gdn_a_log[l], gdn_dt_bias[l], gdn_norm_g[l],
               w_br_conv[l].astype(BF16), w_br_gdn[l].astype(BF16), w_br_mem[l].astype(BF16), w_out[l].astype(BF16),
               norm_ffn_g[l], peer_w_q[l].astype(BF16), peer_sub_keys[l].astype(BF16),
               peer_u[l].astype(BF16),
               peer_v[l].astype(BF16).reshape(-1, PEER_TILE, d).transpose(0, 2, 1), final_norm_g)

    kv_shape = (bp, m_tok, mq)
    mk_p, mv_p = _mem_kv(mem_prompt.reshape(bp * m_tok, d), mem_norm_g[l], w_mem_kv[l].astype(BF16),
                         _tile(bp * m_tok, 256))
    y_p, u_tail_p, p_p, s_p = _layer(x_prompt.reshape(bp * tp, d), mk_p.reshape(kv_shape), mv_p.reshape(kv_shape),
                                     bp, tp, weights)
    rows_p = _tile(tp, 512)
    cb_p = u_tail_p.reshape(bp, tp // rows_p, SUBLANES, d)[:, -1, SUBLANES - (CONV_K - 1):, :]
    gb_p = p_p.reshape(bp, tp, -1)[:, tp - (GDN_CONV_K - 1):, COL_GQ * d:(COL_GV + 1) * d].astype(F32)

    pad_t = SAMPLE_PAD_T
    lead = pad_t - ts
    x_s = jnp.pad(x_sample, ((0, 0), (lead, 0), (0, 0))).reshape(bs * pad_t, d)
    cst = jnp.pad(state_conv[l], ((0, 0), (lead - (CONV_K - 1), ts), (0, 0))).reshape(bs * pad_t, d)
    gst = jnp.pad(state_gdn_conv[l], ((0, 0), (lead - (GDN_CONV_K - 1), ts), (0, 0))).reshape(bs * pad_t, 3 * d)
    y_s, u_s, p_s, s_s = _layer(x_s, cache_mem_k.reshape(bs, -1, mq), cache_mem_v.reshape(bs, -1, mq),
                                bs, pad_t, weights, cst, gst, state_gdn.reshape(bs, GDN_HEADS, GDN_DK, GDN_DV))
    y_s = y_s.reshape(bs, pad_t, d)[:, lead:, :]
    cb_s = u_s.reshape(bs, pad_t, d)[:, pad_t - (CONV_K - 1):, :]
    gb_s = p_s.reshape(bs, pad_t, -1)[:, pad_t - (GDN_CONV_K - 1):, COL_GQ * d:(COL_GV + 1) * d].astype(F32)

    mshape = (depth, bp, m_tok, MEM_HEADS, MEM_HEAD_DIM)
    return (y_p.reshape(bp, tp, d), y_s,
            mk_p.reshape(mshape), mv_p.reshape(mshape),
            cb_p[None], gb_p[None], s_p[None],
            cb_s[None], gb_s[None], s_s[None])
```

```python
import functools
import math

import jax
import jax.numpy as jnp
from jax import lax
from jax.experimental import pallas as pl
from jax.experimental.pallas import tpu as pltpu

F32 = jnp.float32
BF16 = jnp.bfloat16

EPS = 1e-6
D_MODEL = 1024
CONV_K = 3
GDN_HEADS = 8
GDN_DK = 128
GDN_DV = 128
GDN_CONV_K = 4
MEM_HEADS = 4
MEM_HEAD_DIM = 256
PEER_HEADS = 8
PEER_N_KEYS = 128
PEER_D_HALF = 128
PEER_TOPK = 16
N_BRANCHES = 3

COL_AU, COL_AB, COL_AC, COL_GQ, COL_GK, COL_GV, COL_GZ, COL_MQ, COL_BG = 0, 1, 2, 3, 4, 5, 6, 7, 8
N_MAIN_BLOCKS = 11
SMALL_COLS = 128
SUBLANES = 8
LANES = 128
SAMPLE_PAD_T = 8
VMEM_LIMIT = 56 * 1024 * 1024


def _params(sem):
    return pltpu.CompilerParams(dimension_semantics=sem, vmem_limit_bytes=VMEM_LIMIT)


def _dot(a, b):
    return jnp.dot(a.astype(BF16), b.astype(BF16), preferred_element_type=F32)


def _dot_nt(a, b):
    return lax.dot_general(a.astype(BF16), b.astype(BF16), (((1,), (1,)), ((), ())),
                           preferred_element_type=F32)


def _rms_scale(x):
    return lax.rsqrt(jnp.mean(x * x, axis=-1, keepdims=True) + EPS)


def _inproj_kernel(x_ref, g_ref, w_ref, wab_ref, p_ref, pab_ref, h_ref):
    @pl.when(pl.program_id(1) == 0)
    def _():
        x = x_ref[...]
        h = ((x * _rms_scale(x)) * g_ref[...]).astype(BF16)
        h_ref[...] = h
        pab_ref[...] = jnp.dot(h, wab_ref[...], preferred_element_type=F32)

    p_ref[...] = jnp.dot(h_ref[...], w_ref[...], preferred_element_type=F32).astype(p_ref.dtype)


def _inproj(x, g, w_main, w_ab, tm):
    n, d = x.shape
    nblk = w_main.shape[1] // d
    return pl.pallas_call(
        _inproj_kernel,
        grid=(n // tm, nblk),
        in_specs=[
            pl.BlockSpec((tm, d), lambda i, j: (i, 0)),
            pl.BlockSpec((1, d), lambda i, j: (0, 0)),
            pl.BlockSpec((d, d), lambda i, j: (0, j)),
            pl.BlockSpec((d, SMALL_COLS), lambda i, j: (0, 0)),
        ],
        out_specs=[
            pl.BlockSpec((tm, d), lambda i, j: (i, j)),
            pl.BlockSpec((tm, SMALL_COLS), lambda i, j: (i, 0)),
        ],
        out_shape=[
            jax.ShapeDtypeStruct((n, nblk * d), BF16),
            jax.ShapeDtypeStruct((n, SMALL_COLS), F32),
        ],
        scratch_shapes=[pltpu.VMEM((tm, d), BF16)],
        compiler_params=_params(("parallel", "arbitrary")),
        name="inproj",
    )(x, g.reshape(1, d), w_main, w_ab)


def _mem_kv_kernel(x_ref, g_ref, w_ref, k_ref, v_ref):
    x = x_ref[...]
    h = ((x * _rms_scale(x)) * g_ref[...]).astype(BF16)
    kv = jnp.dot(h, w_ref[...], preferred_element_type=F32)
    half = kv.shape[1] // 2
    k_ref[...] = kv[:, :half]
    v_ref[...] = kv[:, half:]


def _mem_kv(x, g, w, tm):
    n, d = x.shape
    c = w.shape[1]
    out = jax.ShapeDtypeStruct((n, c // 2), F32)
    ospec = pl.BlockSpec((tm, c // 2), lambda i: (i, 0))
    return pl.pallas_call(
        _mem_kv_kernel,
        grid=(n // tm,),
        in_specs=[
            pl.BlockSpec((tm, d), lambda i: (i, 0)),
            pl.BlockSpec((1, d), lambda i: (0, 0)),
            pl.BlockSpec((d, c), lambda i: (0, 0)),
        ],
        out_specs=[ospec, ospec],
        out_shape=[out, out],
        compiler_params=_params(("parallel",)),
        name="mem_kv",
    )(x, g.reshape(1, d), w)


def _causal_conv(prev, cur, w_ref, k):
    full = jnp.concatenate([prev, cur], axis=0)
    y = cur * w_ref[k - 1:k, :]
    for s in range(1, k):
        y = y + pltpu.roll(full, s, axis=0)[prev.shape[0]:, :] * w_ref[k - 1 - s:k - s, :]
    return y


def _f32(ref):
    return ref[...].astype(F32)


def _prep_kernel(*refs, rows, tiles_per_seq, sample):
    if sample:
        (au, ab, ac, gq, gk, gv, pab, au_p, ac_p, gq_p, gk_p, gv_p, cw, gw, alog, dtb, cst, gst,
         ya_o, q_o, k_o, v_o, gb_o, ut_o) = refs
    else:
        (au, ab, ac, gq, gk, gv, pab, au_p, ac_p, gq_p, gk_p, gv_p, cw, gw, alog, dtb,
         ya_o, q_o, k_o, v_o, gb_o, ut_o) = refs
    d = D_MODEL
    first = (pl.program_id(0) % tiles_per_seq) == 0
    keep = jnp.where(first, 0.0, 1.0).astype(F32)

    u = _f32(ac) * _f32(au)
    u_prev = (_f32(ac_p) * _f32(au_p)) * keep
    if sample:
        real = (lax.broadcasted_iota(jnp.int32, (rows, 1), 0) % SAMPLE_PAD_T) >= (SAMPLE_PAD_T // 2)
        u = jnp.where(real, u, cst[...])
    ya_o[...] = (_f32(ab) * _causal_conv(u_prev, u, cw, CONV_K)).astype(BF16)
    ut_o[...] = u[rows - ut_o.shape[0]:, :]

    outs = (q_o, k_o, v_o)
    for idx, (cur_ref, prev_ref) in enumerate(((gq, gq_p), (gk, gk_p), (gv, gv_p))):
        cur = _f32(cur_ref)
        if sample:
            cur = jnp.where(real, cur, gst[:, idx * d:(idx + 1) * d])
        prev = _f32(prev_ref) * keep
        wslab = gw.at[:, idx * d:(idx + 1) * d]
        y = _causal_conv(prev, cur, wslab, GDN_CONV_K)
        y = y * jax.nn.sigmoid(y)
        if idx < 2:
            scale = GDN_DK ** -0.5 if idx == 0 else 1.0
            for h in range(GDN_HEADS):
                yh = y[:, h * GDN_DK:(h + 1) * GDN_DK]
                r = lax.rsqrt(jnp.sum(yh * yh, axis=-1, keepdims=True) + EPS)
                outs[idx][:, h * GDN_DK:(h + 1) * GDN_DK] = yh * (r * scale)
        else:
            outs[idx][...] = y

    pa = pab[...]
    lane = lax.broadcasted_iota(jnp.int32, pa.shape, 1)
    z = pa + dtb[...]
    softplus = jnp.maximum(z, 0.0) + jnp.log1p(jnp.exp(-jnp.abs(z)))
    g_log = -jnp.exp(alog[...]) * softplus
    gb = jnp.where(lane < GDN_HEADS, g_log, jnp.where(lane < 2 * GDN_HEADS, jax.nn.sigmoid(pa), 0.0))
    if sample:
        gb = jnp.where(real, gb, 0.0)
    gb_o[...] = gb


def _prep(p, pab, conv_w, gdn_conv_w, a_log, dt_bias, rows, tiles_per_seq, conv_state=None, gdn_state=None):
    n = p.shape[0]
    d = D_MODEL
    sample = conv_state is not None
    nt = n // rows
    prev_rows = 2 * SUBLANES
    rpb = rows // prev_rows

    def cur(c):
        return pl.BlockSpec((rows, d), lambda i, c=c: (i, c))

    def prev(c):
        return pl.BlockSpec((prev_rows, d), lambda i, c=c: (jnp.maximum(i * rpb - 1, 0), c))

    alog128 = jnp.zeros((1, SMALL_COLS), F32).at[0, :GDN_HEADS].set(a_log)
    dtb128 = jnp.zeros((1, SMALL_COLS), F32).at[0, :GDN_HEADS].set(dt_bias)
    in_specs = [cur(COL_AU), cur(COL_AB), cur(COL_AC), cur(COL_GQ), cur(COL_GK), cur(COL_GV),
                pl.BlockSpec((rows, SMALL_COLS), lambda i: (i, 0)),
                prev(COL_AU), prev(COL_AC), prev(COL_GQ), prev(COL_GK), prev(COL_GV),
                pl.BlockSpec((CONV_K, d), lambda i: (0, 0)),
                pl.BlockSpec((GDN_CONV_K, 3 * d), lambda i: (0, 0)),
                pl.BlockSpec((1, SMALL_COLS), lambda i: (0, 0)),
                pl.BlockSpec((1, SMALL_COLS), lambda i: (0, 0))]
    args = [p, p, p, p, p, p, pab, p, p, p, p, p, conv_w, gdn_conv_w, alog128, dtb128]
    if sample:
        in_specs += [pl.BlockSpec((rows, d), lambda i: (i, 0)), pl.BlockSpec((rows, 3 * d), lambda i: (i, 0))]
        args += [conv_state, gdn_state]
    tail = rows if sample else SUBLANES
    out_specs = [pl.BlockSpec((rows, d), lambda i: (i, 0))] * 4 + [
        pl.BlockSpec((rows, SMALL_COLS), lambda i: (i, 0)),
        pl.BlockSpec((tail, d), lambda i: (i, 0))]
    out_shape = [jax.ShapeDtypeStruct((n, d), BF16)] + [jax.ShapeDtypeStruct((n, d), F32)] * 3 + [
        jax.ShapeDtypeStruct((n, SMALL_COLS), F32),
        jax.ShapeDtypeStruct((nt * tail, d), F32)]
    return pl.pallas_call(
        functools.partial(_prep_kernel, rows=rows, tiles_per_seq=tiles_per_seq, sample=sample),
        grid=(nt,),
        in_specs=in_specs,
        out_specs=out_specs,
        out_shape=out_shape,
        compiler_params=_params(("parallel",)),
        name="prep",
    )(*args)


def _chunk_cumsum(g, c):
    row = lax.broadcasted_iota(jnp.int32, g.shape, 0)
    sh = 1
    while sh < c:
        g = g + jnp.where(row >= sh, pltpu.roll(g, sh, axis=0), 0.0)
        sh *= 2
    return g


def _gdn_stacked_kernel(q_ref, k_ref, v_ref, z_ref, gb_ref, ng_ref, yb_ref, sout_ref, s_scr, *, c, grp, nb):
    ci = pl.program_id(1)

    @pl.when(ci == 0)
    def _():
        s_scr[...] = jnp.zeros_like(s_scr)

    r = grp * c
    ng = ng_ref[...]
    seqs = []
    for s in range(nb):
        gb = gb_ref[s]
        gc = _chunk_cumsum(gb, c)
        g_last = gc[c - 1:c, :]
        seqs.append(dict(gb=gb, gc=gc, gct=gc.T, e_g=jnp.exp(gc), e_kd=jnp.exp(g_last - gc), e_last=jnp.exp(g_last)))

    ri = lax.broadcasted_iota(jnp.int32, (r, r), 0)
    cj = lax.broadcasted_iota(jnp.int32, (r, r), 1)
    same = (ri // c) == (cj // c)
    incl = same & (ri >= cj)
    strict = same & (ri > cj)
    head_block = (lax.broadcasted_iota(jnp.int32, (r, grp * GDN_DK), 0) // c
                  == lax.broadcasted_iota(jnp.int32, (r, grp * GDN_DK), 1) // GDN_DK)

    def block_layout(x):
        return jnp.where(head_block, jnp.concatenate([x] * grp, axis=1), 0.0)

    units = []
    for s in range(nb):
        for g0 in range(0, GDN_HEADS, grp):
            heads = list(range(g0, g0 + grp))
            sq = seqs[s]

            def stack(ref, s=s, heads=heads):
                return jnp.concatenate([ref[s, :, h * GDN_DK:(h + 1) * GDN_DK] for h in heads], axis=0)

            def col(a, off=0, heads=heads):
                return jnp.concatenate([a[:, off + h:off + h + 1] for h in heads], axis=0)

            qs, ks, vs = stack(q_ref), stack(k_ref), stack(v_ref)
            beta = col(sq["gb"], GDN_HEADS)
            kb = ks * beta
            units.append(dict(s=s, heads=heads, qs=qs, ks=ks, kb=kb, vb=vs * beta,
                              gcol=col(sq["gc"]), egcol=col(sq["e_g"]), ekdcol=col(sq["e_kd"]),
                              grow=jnp.concatenate([sq["gct"][h:h + 1, :] for h in heads], axis=1),
                              rows=slice((s * GDN_HEADS + g0) * GDN_DK, (s * GDN_HEADS + g0 + grp) * GDN_DK)))

    for u in units:
        u["aq"] = _dot_nt(jnp.concatenate([u["kb"], u["qs"]], axis=0), u["ks"])
    for u in units:
        decay = jnp.exp(jnp.where(incl, u["gcol"] - u["grow"], -jnp.inf))
        u["pw"] = -jnp.where(strict, u["aq"][:r] * decay, 0.0)
        u["attn"] = jnp.where(incl, u["aq"][r:] * decay, 0.0)
        u["x"] = jnp.concatenate([u["vb"], u["kb"] * u["egcol"]], axis=1)
    span = 1
    while span < c:
        span *= 2
        for u in units:
            if span < c:
                both = _dot(u["pw"], jnp.concatenate([u["pw"], u["x"]], axis=1))
                u["pw"] = both[:, :r]
                u["x"] = u["x"] + both[:, r:]
            else:
                u["x"] = u["x"] + _dot(u["pw"], u["x"])
    for u in units:
        u_val, w_k = u["x"][:, :GDN_DV], u["x"][:, GDN_DV:]
        u["s_old"] = s_scr[u["rows"], :]
        ws = _dot(jnp.concatenate([block_layout(w_k), block_layout(u["qs"] * u["egcol"])], axis=0), u["s_old"])
        u["v_new"] = u_val - ws[:r]
        u["o"] = ws[r:]
    for u in units:
        e_last = seqs[u["s"]]["e_last"]
        o = u["o"] + _dot(u["attn"], u["v_new"])
        e_rows = jnp.concatenate([jnp.broadcast_to(e_last[:, h:h + 1], (GDN_DK, 1)) for h in u["heads"]], axis=0)
        kd_t = block_layout(u["ks"] * u["ekdcol"]).T
        s_scr[u["rows"], :] = u["s_old"] * e_rows + _dot(kd_t, u["v_new"])
        for i, h in enumerate(u["heads"]):
            sl = slice(h * GDN_DK, (h + 1) * GDN_DK)
            oh = o[i * c:(i + 1) * c, :]
            zh = z_ref[u["s"], :, sl].astype(F32)
            yb_ref[u["s"], :, sl] = (((oh * _rms_scale(oh)) * ng) * (zh * jax.nn.sigmoid(zh))).astype(BF16)

    @pl.when(ci == pl.num_programs(1) - 1)
    def _():
        sout_ref[...] = s_scr[...]


def _gdn_stacked(q, k, v, p, gb, norm_g, bsz, t, c, nb):
    d = D_MODEL
    nc = t // c
    grp = min(GDN_HEADS, 256 // c)
    s_rows = nb * GDN_HEADS * GDN_DK

    def tok(col=0):
        return pl.BlockSpec((nb, c, d), lambda b, i, col=col: (b, i, col))

    yb, s_new = pl.pallas_call(
        functools.partial(_gdn_stacked_kernel, c=c, grp=grp, nb=nb),
        grid=(bsz // nb, nc),
        in_specs=[tok(), tok(), tok(), tok(COL_GZ),
                  pl.BlockSpec((nb, c, SMALL_COLS), lambda b, i: (b, i, 0)),
                  pl.BlockSpec((1, GDN_DV), lambda b, i: (0, 0))],
        out_specs=[tok(), pl.BlockSpec((s_rows, GDN_DV), lambda b, i: (b, 0))],
        out_shape=[jax.ShapeDtypeStruct((bsz, t, d), BF16),
                   jax.ShapeDtypeStruct((bsz * GDN_HEADS * GDN_DK, GDN_DV), F32)],
        scratch_shapes=[pltpu.VMEM((s_rows, GDN_DV), F32)],
        compiler_params=_params(("parallel", "arbitrary")),
        name="gdn_scan_prompt",
    )(*[a.reshape(bsz, t, -1) for a in (q, k, v, p, gb)], norm_g.reshape(1, GDN_DV))
    return yb.reshape(bsz * t, d), s_new.reshape(bsz, GDN_HEADS, GDN_DK, GDN_DV)


def _gdn_kernel(*refs, c, nb, has_s0):
    if has_s0:
        q_ref, k_ref, v_ref, z_ref, gb_ref, ng_ref, s0_ref, yb_ref, sout_ref, s_scr = refs
    else:
        q_ref, k_ref, v_ref, z_ref, gb_ref, ng_ref, yb_ref, sout_ref, s_scr = refs
    ci = pl.program_id(1)

    @pl.when(ci == 0)
    def _():
        if has_s0:
            s_scr[...] = s0_ref[...]
        else:
            s_scr[...] = jnp.zeros_like(s_scr)

    ri = lax.broadcasted_iota(jnp.int32, (c, c), 0)
    cj = lax.broadcasted_iota(jnp.int32, (c, c), 1)
    incl = ri >= cj
    strict = ri > cj
    ng = ng_ref[...]
    row = lax.broadcasted_iota(jnp.int32, (c, SMALL_COLS), 0)
    chains = [(s, h) for s in range(nb) for h in range(GDN_HEADS)]

    seq = []
    for s in range(nb):
        gb = gb_ref[s]
        gc = gb
        sh = 1
        while sh < c:
            gc = gc + jnp.where(row >= sh, pltpu.roll(gc, sh, axis=0), 0.0)
            sh *= 2
        g_last = gc[c - 1:c, :]
        seq.append(dict(gb=gb, gc=gc, gct=gc.T, e_g=jnp.exp(gc), e_kd=jnp.exp(g_last - gc), e_last=jnp.exp(g_last)))

    st = []
    for s, h in chains:
        sl = slice(h * GDN_DK, (h + 1) * GDN_DK)
        sq = seq[s]
        qh, kh, vh = q_ref[s, :, sl], k_ref[s, :, sl], v_ref[s, :, sl]
        beta = sq["gb"][:, GDN_HEADS + h:GDN_HEADS + h + 1]
        kb = kh * beta
        aq = _dot_nt(jnp.concatenate([kb, qh], axis=0), kh)
        st.append(dict(sl=sl, qh=qh, kh=kh, kb=kb, vb=vh * beta, aq=aq))

    for (s, h), e in zip(chains, st):
        sq = seq[s]
        decay = jnp.exp(jnp.where(incl, sq["gc"][:, h:h + 1] - sq["gct"][h:h + 1, :], -jnp.inf))
        e["neg"] = -jnp.where(strict, e["aq"][:c] * decay, 0.0)
        e["attn"] = jnp.where(incl, e["aq"][c:] * decay, 0.0)
        e["x"] = jnp.concatenate([e["vb"], e["kb"] * sq["e_g"][:, h:h + 1]], axis=1)
        e["pw"] = e["neg"]

    span = 1
    while span < c:
        span *= 2
        for e in st:
            if span < c:
                both = _dot(e["pw"], jnp.concatenate([e["pw"], e["x"]], axis=1))
                e["pw"] = both[:, :c]
                e["x"] = e["x"] + both[:, c:]
            else:
                e["x"] = e["x"] + _dot(e["pw"], e["x"])

    for (s, h), e in zip(chains, st):
        sq = seq[s]
        u_val, w_k = e["x"][:, :GDN_DV], e["x"][:, GDN_DV:]
        q_dec = e["qh"] * sq["e_g"][:, h:h + 1]
        e["s_old"] = s_scr[s, h]
        ws = _dot(jnp.concatenate([w_k, q_dec], axis=0), e["s_old"])
        e["v_new"] = u_val - ws[:c]
        e["o_inter"] = ws[c:]

    for (s, h), e in zip(chains, st):
        sq = seq[s]
        o = e["o_inter"] + _dot(e["attn"], e["v_new"])
        k_dec = e["kh"] * sq["e_kd"][:, h:h + 1]
        s_scr[s, h] = e["s_old"] * sq["e_last"][:, h:h + 1] + _dot(k_dec.T, e["v_new"])
        zh = z_ref[s, :, e["sl"]].astype(F32)
        yb_ref[s, :, e["sl"]] = (((o * _rms_scale(o)) * ng) * (zh * jax.nn.sigmoid(zh))).astype(BF16)

    @pl.when(ci == pl.num_programs(1) - 1)
    def _():
        sout_ref[...] = s_scr[...]


def _gdn(q, k, v, p, gb, norm_g, bsz, t, c, nb, s0=None):
    d = D_MODEL
    nc = t // c
    has_s0 = s0 is not None

    def tok(col=0):
        return pl.BlockSpec((nb, c, d), lambda b, i, col=col: (b, i, col))

    in_specs = [tok(), tok(), tok(), tok(COL_GZ),
                pl.BlockSpec((nb, c, SMALL_COLS), lambda b, i: (b, i, 0)),
                pl.BlockSpec((1, GDN_DV), lambda b, i: (0, 0))]
    args = [a.reshape(bsz, t, -1) for a in (q, k, v, p, gb)] + [norm_g.reshape(1, GDN_DV)]
    s_spec = pl.BlockSpec((nb, GDN_HEADS, GDN_DK, GDN_DV), lambda b, i: (b, 0, 0, 0))
    if has_s0:
        in_specs.append(s_spec)
        args.append(s0)
    yb, s_new = pl.pallas_call(
        functools.partial(_gdn_kernel, c=c, nb=nb, has_s0=has_s0),
        grid=(bsz // nb, nc),
        in_specs=in_specs,
        out_specs=[tok(), s_spec],
        out_shape=[jax.ShapeDtypeStruct((bsz, t, d), BF16),
                   jax.ShapeDtypeStruct((bsz, GDN_HEADS, GDN_DK, GDN_DV), F32)],
        scratch_shapes=[pltpu.VMEM((nb, GDN_HEADS, GDN_DK, GDN_DV), F32)],
        compiler_params=_params(("parallel", "arbitrary")),
        name="gdn_scan",
    )(*args)
    return yb.reshape(bsz * t, d), s_new


def _mem_attn_kernel(q_ref, k_ref, v_ref, o_ref, *, nb, tq):
    scale = MEM_HEAD_DIM ** -0.5
    pairs = [(s, h) for s in range(nb) for h in range(MEM_HEADS)]
    span = lambda s, h: (slice(s * tq, (s + 1) * tq), slice(h * MEM_HEAD_DIM, (h + 1) * MEM_HEAD_DIM))
    scores = [_dot_nt(q_ref[span(s, h)], k_ref[s, :, span(s, h)[1]]) * scale for s, h in pairs]
    probs = []
    for sc in scores:
        e = jnp.exp(sc - jnp.max(sc, axis=-1, keepdims=True))
        probs.append(e / jnp.sum(e, axis=-1, keepdims=True))
    for (s, h), prob in zip(pairs, probs):
        o_ref[span(s, h)] = _dot(prob, v_ref[s, :, span(s, h)[1]]).astype(BF16)


def _mem_attn(p, mk, mv, bsz, t, tq, nb):
    d = D_MODEL
    m = mk.shape[1]
    nq = t // tq
    assert nb == 1 or nq == 1
    kv_spec = pl.BlockSpec((nb, m, d), lambda b, i: (b, 0, 0))
    return pl.pallas_call(
        functools.partial(_mem_attn_kernel, nb=nb, tq=tq),
        grid=(bsz // nb, nq),
        in_specs=[pl.BlockSpec((nb * tq, d), lambda b, i: (b * nq + i, COL_MQ)), kv_spec, kv_spec],
        out_specs=pl.BlockSpec((nb * tq, d), lambda b, i: (b * nq + i, 0)),
        out_shape=jax.ShapeDtypeStruct((bsz * t, d), BF16),
        compiler_params=_params(("parallel", "arbitrary")),
        name="mem_attn",
    )(p, mk, mv)


def _merge_kernel(x_ref, ya_ref, yb_ref, ym_ref, g0_ref, g1_ref, g2_ref, wc_ref, wg_ref, wm_ref, wo_ref,
                  ng_ref, x1_ref, h2_ref):
    d = functools.partial(jnp.dot, preferred_element_type=F32)
    merged = (jax.nn.sigmoid(_f32(g0_ref)) * d(ya_ref[...], wc_ref[...])
              + jax.nn.sigmoid(_f32(g1_ref)) * d(yb_ref[...], wg_ref[...])
              + jax.nn.sigmoid(_f32(g2_ref)) * d(ym_ref[...], wm_ref[...]))
    x1 = x_ref[...] + d(merged.astype(BF16), wo_ref[...])
    x1_ref[...] = x1
    h2_ref[...] = ((x1 * _rms_scale(x1)) * ng_ref[...]).astype(BF16)


def _merge(x, ya, yb, ym, p, wc, wg, wm, wo, norm_g, tm):
    n, d = x.shape
    tok = pl.BlockSpec((tm, d), lambda i: (i, 0))
    wsp = pl.BlockSpec((d, d), lambda i: (0, 0))

    def gate(j):
        return pl.BlockSpec((tm, d), lambda i, j=j: (i, COL_BG + j))

    return pl.pallas_call(
        _merge_kernel,
        grid=(n // tm,),
        in_specs=[tok, tok, tok, tok, gate(0), gate(1), gate(2), wsp, wsp, wsp, wsp,
                  pl.BlockSpec((1, d), lambda i: (0, 0))],
        out_specs=[tok, tok],
        out_shape=[jax.ShapeDtypeStruct((n, d), F32), jax.ShapeDtypeStruct((n, d), BF16)],
        compiler_params=_params(("parallel",)),
        name="merge",
    )(x, ya, yb, ym, p, p, p, wc, wg, wm, wo, norm_g.reshape(1, d))


TILE_PAD = 4 * SUBLANES
SHIFT_S1, SHIFT_E1, SHIFT_CUT, SHIFT_E0 = 0, SUBLANES, 2 * SUBLANES, 3 * SUBLANES


def _put_tile(ref, h, ns, shift, val, pad=True):
    if pad and shift:
        ref[h, ns, :shift, :] = jnp.zeros((shift, LANES), F32)
    ref[h, ns, shift:shift + PEER_N_KEYS, :] = val
    if pad and shift < TILE_PAD:
        ref[h, ns, shift + PEER_N_KEYS:, :] = jnp.zeros((TILE_PAD - shift, LANES), F32)
_CAND_CELLS = tuple((a, b) for a in range(PEER_TOPK) for b in range(PEER_TOPK) if (a + 1) * (b + 1) <= PEER_TOPK)


def _top_values(x, count):
    vals = []
    cur = x
    for r in range(count):
        m = jnp.max(cur, axis=0, keepdims=True)
        vals.append(m)
        if r + 1 < count:
            cur = jnp.where(cur == m, -jnp.inf, cur)
    return vals


def _top_ranked(x, count):
    n_rows = x.shape[0]
    iota = lax.broadcasted_iota(jnp.int32, x.shape, 0).astype(F32)
    vals = []
    cur = x
    rank = jnp.full(x.shape, float(count), F32)
    for r in range(count):
        m = jnp.max(cur, axis=0, keepdims=True)
        first = jnp.min(jnp.where(cur == m, iota, float(n_rows)), axis=0, keepdims=True)
        one = iota == first
        rank = jnp.where(one, float(r), rank)
        cur = jnp.where(one, -jnp.inf, cur)
        vals.append(m)
    return vals, rank


def _count_rows(mask):
    return jnp.sum(jnp.where(mask, 1.0, 0.0), axis=0, keepdims=True)


def _route_kernel(h_ref, wq_ref, keys_ref, cut_ref, e0_ref, s1_ref, e1_ref, q_scr, tie_ref):
    tn = h_ref.shape[0]
    nsb = tn // LANES
    q = jnp.dot(h_ref[...], wq_ref[...], preferred_element_type=F32).astype(BF16)

    def store(h, cut, e0, s1, e1):
        for ns in range(nsb):
            cols = slice(ns * LANES, (ns + 1) * LANES)
            _put_tile(cut_ref, h, ns, SHIFT_CUT, cut[:, cols])
            _put_tile(e0_ref, h, ns, SHIFT_E0, e0[:, cols])
            _put_tile(s1_ref, h, ns, SHIFT_S1, s1[:, cols])
            _put_tile(e1_ref, h, ns, SHIFT_E1, e1[:, cols])

    for h in range(PEER_HEADS):
        sc, top = [], []
        tied = jnp.zeros((1, tn), F32)
        for part in range(2):
            col = (2 * h + part) * PEER_D_HALF
            q_scr[2 * h + part] = q[:, col:col + PEER_D_HALF]
            s_t = lax.dot_general(keys_ref[part, h], q[:, col:col + PEER_D_HALF],
                                  (((1,), (1,)), ((), ())), preferred_element_type=F32)
            sc.append(s_t)
            top.append(_top_values(s_t, PEER_TOPK))
            tied = jnp.maximum(tied, jnp.abs(_count_rows(s_t >= top[part][-1]) - PEER_TOPK))
        cand = jnp.concatenate([top[0][a] + top[1][b] for a, b in _CAND_CELLS], axis=0)
        best = _top_values(cand, PEER_TOPK)
        c_max, thr = best[0], best[-1]
        z = jnp.exp(best[0] - c_max)
        for r in range(1, PEER_TOPK):
            z = z + jnp.exp(best[r] - c_max)
        top1 = jnp.concatenate(top[1], axis=0)
        n_sel = sum(_count_rows(top[0][a] + top1 >= thr) for a in range(PEER_TOPK))
        tied = jnp.maximum(tied, jnp.abs(n_sel - PEER_TOPK))
        cut = jnp.full(sc[0].shape, jnp.inf, F32)
        for b in range(PEER_TOPK):
            cut = jnp.where(sc[0] + top[1][b] >= thr, top[1][b], cut)
        store(h, cut, jnp.exp(sc[0] - top[0][0]) / z, sc[1], jnp.exp(sc[1] - top[1][0]))
        for ns in range(nsb):
            tie_ref[h * nsb + ns] = jnp.max(tied[:, ns * LANES:(ns + 1) * LANES])

    def exact_block(idx, carry):
        @pl.when(tie_ref[idx] > 0.0)
        def _():
            h = idx // nsb
            ns = idx % nsb
            rows = pl.ds(pl.multiple_of(ns * LANES, LANES), LANES)
            sc, top, rank = [], [], []
            for part in range(2):
                s_t = lax.dot_general(keys_ref[part, h], q_scr[2 * h + part, rows, :],
                                      (((1,), (1,)), ((), ())), preferred_element_type=F32)
                vals, rk = _top_ranked(s_t, PEER_TOPK)
                sc.append(s_t)
                top.append(vals)
                rank.append(rk)
            top1 = jnp.concatenate(top[1], axis=0)
            cand = jnp.concatenate([top[0][a] + top1 for a in range(PEER_TOPK)], axis=0)
            best, pos = _top_ranked(cand, PEER_TOPK)
            z = jnp.exp(best[0] - best[0])
            for r in range(1, PEER_TOPK):
                z = z + jnp.exp(best[r] - best[0])
            cnt = jnp.zeros_like(sc[0])
            for a in range(PEER_TOPK):
                cnt_a = _count_rows(pos[a * PEER_TOPK:(a + 1) * PEER_TOPK] < PEER_TOPK)
                cnt = jnp.where(rank[0] == float(a), cnt_a, cnt)
            _put_tile(cut_ref, h, ns, SHIFT_CUT, jnp.where(cnt > 0.0, 1.0 - cnt, jnp.inf), pad=False)
            _put_tile(e0_ref, h, ns, SHIFT_E0, jnp.exp(sc[0] - top[0][0]) / z, pad=False)
            _put_tile(s1_ref, h, ns, SHIFT_S1, -rank[1], pad=False)
            _put_tile(e1_ref, h, ns, SHIFT_E1, jnp.exp(sc[1] - top[1][0]), pad=False)
        return carry

    lax.fori_loop(0, PEER_HEADS * nsb, exact_block, 0)


def _route(h2, wq, keys, tn):
    n, d = h2.shape
    nk = PEER_N_KEYS
    tiled = pl.BlockSpec((PEER_HEADS, tn // LANES, nk + TILE_PAD, LANES), lambda i: (0, i, 0, 0))
    tiled_shape = jax.ShapeDtypeStruct((PEER_HEADS, n // LANES, nk + TILE_PAD, LANES), F32)
    return pl.pallas_call(
        _route_kernel,
        grid=(n // tn,),
        in_specs=[pl.BlockSpec((tn, d), lambda i: (i, 0)),
                  pl.BlockSpec(wq.shape, lambda i: (0, 0)),
                  pl.BlockSpec(keys.shape, lambda i: (0, 0, 0, 0))],
        out_specs=[tiled, tiled, tiled, tiled],
        out_shape=[tiled_shape, tiled_shape, tiled_shape, tiled_shape],
        scratch_shapes=[pltpu.VMEM((2 * PEER_HEADS, tn, PEER_D_HALF), BF16),
                        pltpu.SMEM((PEER_HEADS * (tn // LANES),), F32)],
        compiler_params=_params(("parallel",)),
        name="peer_route",
    )(h2, wq, keys)


def _gelu(x):
    return 0.5 * x * (1.0 + lax.erf(x * (2.0 ** -0.5)))


PEER_TILE = 2048
GATE_HALF = 32
GATE_ROWS = 8


def _peer_kernel(h_ref, x1_ref, u_ref, vt_ref, cut_ref, e0_ref, s1_ref, e1_ref, ng_ref,
                 y_ref, acc_ref, act_ref, coef_ref, *, te):
    ei = pl.program_id(1)
    tn = h_ref.shape[0]
    groups = te // PEER_N_KEYS

    @pl.when(ei == 0)
    def _():
        acc_ref[...] = jnp.zeros_like(acc_ref)

    nsb = tn // LANES
    act = _gelu(lax.dot_general(u_ref[...], h_ref[...], (((1,), (1,)), ((), ())),
                                preferred_element_type=F32))
    for ns in range(nsb):
        act_ref[ns] = act[:, ns * LANES:(ns + 1) * LANES]

    def gate_block(ns):
        for jh in range(PEER_N_KEYS // GATE_HALF):
            jrows = slice(jh * GATE_HALF, (jh + 1) * GATE_HALF)
            for g0 in range(0, groups, GATE_ROWS):
                gates = [None] * GATE_ROWS
                for h in range(PEER_HEADS):
                    s1_t = s1_ref[h, ns, SHIFT_S1 + jh * GATE_HALF:SHIFT_S1 + (jh + 1) * GATE_HALF, :]
                    e1_t = e1_ref[h, ns, SHIFT_E1 + jh * GATE_HALF:SHIFT_E1 + (jh + 1) * GATE_HALF, :]
                    for k in range(GATE_ROWS):
                        i_key = ei * groups + g0 + k
                        sel = s1_t >= cut_ref[h, ns, pl.ds(SHIFT_CUT + i_key, 1), :]
                        term = jnp.where(sel, e1_t, 0.0) * e0_ref[h, ns, pl.ds(SHIFT_E0 + i_key, 1), :]
                        gates[k] = term if gates[k] is None else gates[k] + term
                for k in range(GATE_ROWS):
                    r0 = (g0 + k) * PEER_N_KEYS + jh * GATE_HALF
                    coef_ref[ns, r0:r0 + GATE_HALF, :] = (gates[k] * act_ref[ns, r0:r0 + GATE_HALF, :]).astype(BF16)

    def lane_block(ns, carry):
        gate_block(ns)
        return carry

    def mix_values(lo, hi):
        coef = jnp.concatenate([coef_ref[ns] for ns in range(lo, hi)], axis=1)
        acc_ref[:, lo * LANES:hi * LANES] += jnp.dot(vt_ref[0], coef, preferred_element_type=F32)

    half = nsb // 2
    lax.fori_loop(0, half, lane_block, 0)
    if half:
        mix_values(0, half)
    for ns in range(half, nsb):
        gate_block(ns)
    mix_values(half, nsb)

    @pl.when(ei == pl.num_programs(1) - 1)
    def _():
        x2 = x1_ref[...] + acc_ref[...].T
        y_ref[...] = (x2 * _rms_scale(x2)) * ng_ref[...]


def _peer(h2, x1, u_bf, vt_bf, cut, e0, s1, e1, norm_g, tn):
    n, d = h2.shape
    ne = u_bf.shape[0]
    te = vt_bf.shape[2]
    nk = PEER_N_KEYS
    tok = pl.BlockSpec((tn, d), lambda i, e: (i, 0))
    tiled = pl.BlockSpec((PEER_HEADS, tn // LANES, nk + TILE_PAD, LANES), lambda i, e: (0, i, 0, 0),
                         pipeline_mode=pl.Buffered(1))
    return pl.pallas_call(
        functools.partial(_peer_kernel, te=te),
        grid=(n // tn, ne // te),
        in_specs=[tok, tok,
                  pl.BlockSpec((te, d), lambda i, e: (e, 0)),
                  pl.BlockSpec((1, d, te), lambda i, e: (e, 0, 0)),
                  tiled, tiled, tiled, tiled,
                  pl.BlockSpec((1, d), lambda i, e: (0, 0))],
        out_specs=tok,
        out_shape=jax.ShapeDtypeStruct((n, d), F32),
        scratch_shapes=[pltpu.VMEM((d, tn), F32), pltpu.VMEM((tn // LANES, te, LANES), F32),
                        pltpu.VMEM((tn // LANES, te, LANES), BF16)],
        compiler_params=_params(("parallel", "arbitrary")),
        name="peer_mix",
    )(h2, x1, u_bf, vt_bf, cut, e0, s1, e1, norm_g.reshape(1, d))


def _tile(n, pref):
    return math.gcd(n, pref)


def _layer(x, mk, mv, bsz, t, weights, conv_state=None, gdn_conv_state=None, gdn_s0=None):
    (norm_mix_g, w_main, w_ab, conv_w, gdn_conv_w, a_log, dt_bias, gdn_norm_g, wc, wg, wm, wo, norm_ffn_g,
     wq, keys, u_bf, vt_bf, final_norm_g) = weights
    n = bsz * t
    sample = conv_state is not None
    p, pab = _inproj(x, norm_mix_g, w_main, w_ab, _tile(n, 1024))
    rows = _tile(n, 256) if sample else _tile(t, 512)
    ya, q, k, v, gb, u_tail = _prep(p, pab, conv_w, gdn_conv_w, a_log, dt_bias, rows, max(t // rows, 1),
                                    conv_state, gdn_conv_state)
    chunk = _tile(t, 64)
    if sample:
        yb, s_new = _gdn(q, k, v, p, gb, gdn_norm_g, bsz, t, chunk, 4, gdn_s0)
    else:
        yb, s_new = _gdn_stacked(q, k, v, p, gb, gdn_norm_g, bsz, t, chunk, _tile(bsz, 2))
    ym = _mem_attn(p, mk, mv, bsz, t, _tile(t, 512), _tile(bsz, 4) if sample else 1)
    x1, h2 = _merge(x, ya, yb, ym, p, wc, wg, wm, wo, norm_ffn_g, _tile(n, 256))
    if sample:
        keep = lambda a: a.reshape(bsz, t, -1)[:, t // 2:, :].reshape(bsz * t // 2, -1)
        x1, h2 = keep(x1), keep(h2)
    tn = _tile(x1.shape[0], 512)
    cut, e0, s1, e1 = _route(h2, wq, keys, tn)
    y = _peer(h2, x1, u_bf, vt_bf, cut, e0, s1, e1, final_norm_g, tn)
    return y, u_tail, p, s_new


def kernel(x_prompt, x_sample, mem_prompt, cache_mem_k, cache_mem_v, state_conv, state_gdn_conv, state_gdn, norm_mix_g, w_in, conv_w, gdn_conv_w, gdn_a_log, gdn_dt_bias, gdn_norm_g, mem_norm_g, w_mem_kv, w_br_conv, w_br_gdn, w_br_mem, w_out, norm_ffn_g, peer_w_q, peer_sub_keys, peer_u, peer_v, final_norm_g):
    depth = w_in.shape[0]
    assert depth == 1, "single-layer step"
    d = D_MODEL
    bp, tp, _ = x_prompt.shape
    bs, ts, _ = x_sample.shape
    assert ts == SAMPLE_PAD_T // 2
    m_tok = mem_prompt.shape[1]
    mq = MEM_HEADS * MEM_HEAD_DIM
    l = 0

    w = w_in[l]
    ga0 = 7 * d
    w_main = jnp.concatenate([w[:, :ga0], w[:, ga0 + 2 * GDN_HEADS:]], axis=1).astype(BF16)
    w_ab = jnp.pad(w[:, ga0:ga0 + 2 * GDN_HEADS], ((0, 0), (0, SMALL_COLS - 2 * GDN_HEADS))).astype(BF16)
    weights = (norm_mix_g[l], w_main, w_ab, conv_w[l], gdn_conv_w[l], gdn_a_log[l], gdn_dt_bias[l], gdn_norm_g[l],
               w_br_conv[l].astype(BF16), w_br_gdn[l].astype(BF16), w_br_mem[l].astype(BF16), w_out[l].astype(BF16),
               norm_ffn_g[l], peer_w_q[l].astype(BF16), peer_sub_keys[l].astype(BF16),
               peer_u[l].astype(BF16),
               peer_v[l].astype(BF16).reshape(-1, PEER_TILE, d).transpose(0, 2, 1), final_norm_g)

    kv_shape = (bp, m_tok, mq)
    mk_p, mv_p = _mem_kv(mem_prompt.reshape(bp * m_tok, d), mem_norm_g[l], w_mem_kv[l].astype(BF16),
                         _tile(bp * m_tok, 256))
    y_p, u_tail_p, p_p, s_p = _layer(x_prompt.reshape(bp * tp, d), mk_p.reshape(kv_shape), mv_p.reshape(kv_shape),
                                     bp, tp, weights)
    rows_p = _tile(tp, 512)
    cb_p = u_tail_p.reshape(bp, tp // rows_p, SUBLANES, d)[:, -1, SUBLANES - (CONV_K - 1):, :]
    gb_p = p_p.reshape(bp, tp, -1)[:, tp - (GDN_CONV_K - 1):, COL_GQ * d:(COL_GV + 1) * d].astype(F32)

    pad_t = SAMPLE_PAD_T
    lead = pad_t - ts
    x_s = jnp.pad(x_sample, ((0, 0), (lead, 0), (0, 0))).reshape(bs * pad_t, d)
    cst = jnp.pad(state_conv[l], ((0, 0), (lead - (CONV_K - 1), ts), (0, 0))).reshape(bs * pad_t, d)
    gst = jnp.pad(state_gdn_conv[l], ((0, 0), (lead - (GDN_CONV_K - 1), ts), (0, 0))).reshape(bs * pad_t, 3 * d)
    y_s, u_s, p_s, s_s = _layer(x_s, cache_mem_k.reshape(bs, -1, mq), cache_mem_v.reshape(bs, -1, mq),
                                bs, pad_t, weights, cst, gst, state_gdn.reshape(bs, GDN_HEADS, GDN_DK, GDN_DV))
    y_s = y_s.reshape(bs, ts, d)
    cb_s = u_s.reshape(bs, pad_t, d)[:, pad_t - (CONV_K - 1):, :]
    gb_s = p_s.reshape(bs, pad_t, -1)[:, pad_t - (GDN_CONV_K - 1):, COL_GQ * d:(COL_GV + 1) * d].astype(F32)

    mshape = (depth, bp, m_tok, MEM_HEADS, MEM_HEAD_DIM)
    return (y_p.reshape(bp, tp, d), y_s,
            mk_p.reshape(mshape), mv_p.reshape(mshape),
            cb_p[None], gb_p[None], s_p[None],
            cb_s[None], gb_s[None], s_s[None])
```

```python
import functools
import math

import jax
import jax.numpy as jnp
from jax import lax
from jax.experimental import pallas as pl
from jax.experimental.pallas import tpu as pltpu

F32 = jnp.float32
BF16 = jnp.bfloat16

EPS = 1e-6
D_MODEL = 1024
CONV_K = 3
GDN_HEADS = 8
GDN_DK = 128
GDN_DV = 128
GDN_CONV_K = 4
MEM_HEADS = 4
MEM_HEAD_DIM = 256
PEER_HEADS = 8
PEER_N_KEYS = 128
PEER_D_HALF = 128
PEER_TOPK = 16
N_BRANCHES = 3

COL_AU, COL_AB, COL_AC, COL_GQ, COL_GK, COL_GV, COL_GZ, COL_MQ, COL_BG = 0, 1, 2, 3, 4, 5, 6, 7, 8
N_MAIN_BLOCKS = 11
SMALL_COLS = 128
SUBLANES = 8
LANES = 128
SAMPLE_PAD_T = 8
VMEM_LIMIT = 56 * 1024 * 1024


def _params(sem):
    return pltpu.CompilerParams(dimension_semantics=sem, vmem_limit_bytes=VMEM_LIMIT)


def _dot(a, b):
    return jnp.dot(a.astype(BF16), b.astype(BF16), preferred_element_type=F32)


def _dot_nt(a, b):
    return lax.dot_general(a.astype(BF16), b.astype(BF16), (((1,), (1,)), ((), ())),
                           preferred_element_type=F32)


def _rms_scale(x):
    return lax.rsqrt(jnp.mean(x * x, axis=-1, keepdims=True) + EPS)


def _inproj_kernel(x_ref, g_ref, w_ref, wab_ref, p_ref, pab_ref, h_ref):
    @pl.when(pl.program_id(1) == 0)
    def _():
        x = x_ref[...]
        h = ((x * _rms_scale(x)) * g_ref[...]).astype(BF16)
        h_ref[...] = h
        pab_ref[...] = jnp.dot(h, wab_ref[...], preferred_element_type=F32)

    p_ref[...] = jnp.dot(h_ref[...], w_ref[...], preferred_element_type=F32).astype(p_ref.dtype)


def _inproj(x, g, w_main, w_ab, tm):
    n, d = x.shape
    nblk = w_main.shape[1] // d
    return pl.pallas_call(
        _inproj_kernel,
        grid=(n // tm, nblk),
        in_specs=[
            pl.BlockSpec((tm, d), lambda i, j: (i, 0)),
            pl.BlockSpec((1, d), lambda i, j: (0, 0)),
            pl.BlockSpec((d, d), lambda i, j: (0, j)),
            pl.BlockSpec((d, SMALL_COLS), lambda i, j: (0, 0)),
        ],
        out_specs=[
            pl.BlockSpec((tm, d), lambda i, j: (i, j)),
            pl.BlockSpec((tm, SMALL_COLS), lambda i, j: (i, 0)),
        ],
        out_shape=[
            jax.ShapeDtypeStruct((n, nblk * d), BF16),
            jax.ShapeDtypeStruct((n, SMALL_COLS), F32),
        ],
        scratch_shapes=[pltpu.VMEM((tm, d), BF16)],
        compiler_params=_params(("parallel", "arbitrary")),
        name="inproj",
    )(x, g.reshape(1, d), w_main, w_ab)


def _mem_kv_kernel(x_ref, g_ref, w_ref, k_ref, v_ref):
    x = x_ref[...]
    h = ((x * _rms_scale(x)) * g_ref[...]).astype(BF16)
    kv = jnp.dot(h, w_ref[...], preferred_element_type=F32)
    half = kv.shape[1] // 2
    k_ref[...] = kv[:, :half]
    v_ref[...] = kv[:, half:]


def _mem_kv(x, g, w, tm):
    n, d = x.shape
    c = w.shape[1]
    out = jax.ShapeDtypeStruct((n, c // 2), F32)
    ospec = pl.BlockSpec((tm, c // 2), lambda i: (i, 0))
    return pl.pallas_call(
        _mem_kv_kernel,
        grid=(n // tm,),
        in_specs=[
            pl.BlockSpec((tm, d), lambda i: (i, 0)),
            pl.BlockSpec((1, d), lambda i: (0, 0)),
            pl.BlockSpec((d, c), lambda i: (0, 0)),
        ],
        out_specs=[ospec, ospec],
        out_shape=[out, out],
        compiler_params=_params(("parallel",)),
        name="mem_kv",
    )(x, g.reshape(1, d), w)


def _causal_conv(prev, cur, w_ref, k):
    full = jnp.concatenate([prev, cur], axis=0)
    y = cur * w_ref[k - 1:k, :]
    for s in range(1, k):
        y = y + pltpu.roll(full, s, axis=0)[prev.shape[0]:, :] * w_ref[k - 1 - s:k - s, :]
    return y


def _f32(ref):
    return ref[...].astype(F32)


def _prep_kernel(*refs, rows, tiles_per_seq, sample):
    if sample:
        (au, ab, ac, gq, gk, gv, pab, au_p, ac_p, gq_p, gk_p, gv_p, cw, gw, alog, dtb, cst, gst,
         ya_o, q_o, k_o, v_o, gb_o, ut_o) = refs
    else:
        (au, ab, ac, gq, gk, gv, pab, au_p, ac_p, gq_p, gk_p, gv_p, cw, gw, alog, dtb,
         ya_o, q_o, k_o, v_o, gb_o, ut_o) = refs
    d = D_MODEL
    first = (pl.program_id(0) % tiles_per_seq) == 0
    keep = jnp.where(first, 0.0, 1.0).astype(F32)

    u = _f32(ac) * _f32(au)
    u_prev = (_f32(ac_p) * _f32(au_p)) * keep
    if sample:
        real = (lax.broadcasted_iota(jnp.int32, (rows, 1), 0) % SAMPLE_PAD_T) >= (SAMPLE_PAD_T // 2)
        u = jnp.where(real, u, cst[...])
    ya_o[...] = (_f32(ab) * _causal_conv(u_prev, u, cw, CONV_K)).astype(BF16)
    ut_o[...] = u[rows - ut_o.shape[0]:, :]

    outs = (q_o, k_o, v_o)
    for idx, (cur_ref, prev_ref) in enumerate(((gq, gq_p), (gk, gk_p), (gv, gv_p))):
        cur = _f32(cur_ref)
        if sample:
            cur = jnp.where(real, cur, gst[:, idx * d:(idx + 1) * d])
        prev = _f32(prev_ref) * keep
        wslab = gw.at[:, idx * d:(idx + 1) * d]
        y = _causal_conv(prev, cur, wslab, GDN_CONV_K)
        y = y * jax.nn.sigmoid(y)
        if idx < 2:
            scale = GDN_DK ** -0.5 if idx == 0 else 1.0
            for h in range(GDN_HEADS):
                yh = y[:, h * GDN_DK:(h + 1) * GDN_DK]
                r = lax.rsqrt(jnp.sum(yh * yh, axis=-1, keepdims=True) + EPS)
                outs[idx][:, h * GDN_DK:(h + 1) * GDN_DK] = yh * (r * scale)
        else:
            outs[idx][...] = y

    pa = pab[...]
    lane = lax.broadcasted_iota(jnp.int32, pa.shape, 1)
    z = pa + dtb[...]
    softplus = jnp.maximum(z, 0.0) + jnp.log1p(jnp.exp(-jnp.abs(z)))
    g_log = -jnp.exp(alog[...]) * softplus
    gb = jnp.where(lane < GDN_HEADS, g_log, jnp.where(lane < 2 * GDN_HEADS, jax.nn.sigmoid(pa), 0.0))
    if sample:
        gb = jnp.where(real, gb, 0.0)
    gb_o[...] = gb


def _prep(p, pab, conv_w, gdn_conv_w, a_log, dt_bias, rows, tiles_per_seq, conv_state=None, gdn_state=None):
    n = p.shape[0]
    d = D_MODEL
    sample = conv_state is not None
    nt = n // rows
    prev_rows = 2 * SUBLANES
    rpb = rows // prev_rows

    def cur(c):
        return pl.BlockSpec((rows, d), lambda i, c=c: (i, c))

    def prev(c):
        return pl.BlockSpec((prev_rows, d), lambda i, c=c: (jnp.maximum(i * rpb - 1, 0), c))

    alog128 = jnp.zeros((1, SMALL_COLS), F32).at[0, :GDN_HEADS].set(a_log)
    dtb128 = jnp.zeros((1, SMALL_COLS), F32).at[0, :GDN_HEADS].set(dt_bias)
    in_specs = [cur(COL_AU), cur(COL_AB), cur(COL_AC), cur(COL_GQ), cur(COL_GK), cur(COL_GV),
                pl.BlockSpec((rows, SMALL_COLS), lambda i: (i, 0)),
                prev(COL_AU), prev(COL_AC), prev(COL_GQ), prev(COL_GK), prev(COL_GV),
                pl.BlockSpec((CONV_K, d), lambda i: (0, 0)),
                pl.BlockSpec((GDN_CONV_K, 3 * d), lambda i: (0, 0)),
                pl.BlockSpec((1, SMALL_COLS), lambda i: (0, 0)),
                pl.BlockSpec((1, SMALL_COLS), lambda i: (0, 0))]
    args = [p, p, p, p, p, p, pab, p, p, p, p, p, conv_w, gdn_conv_w, alog128, dtb128]
    if sample:
        in_specs += [pl.BlockSpec((rows, d), lambda i: (i, 0)), pl.BlockSpec((rows, 3 * d), lambda i: (i, 0))]
        args += [conv_state, gdn_state]
    tail = rows if sample else SUBLANES
    out_specs = [pl.BlockSpec((rows, d), lambda i: (i, 0))] * 4 + [
        pl.BlockSpec((rows, SMALL_COLS), lambda i: (i, 0)),
        pl.BlockSpec((tail, d), lambda i: (i, 0))]
    out_shape = [jax.ShapeDtypeStruct((n, d), BF16)] + [jax.ShapeDtypeStruct((n, d), F32)] * 3 + [
        jax.ShapeDtypeStruct((n, SMALL_COLS), F32),
        jax.ShapeDtypeStruct((nt * tail, d), F32)]
    return pl.pallas_call(
        functools.partial(_prep_kernel, rows=rows, tiles_per_seq=tiles_per_seq, sample=sample),
        grid=(nt,),
        in_specs=in_specs,
        out_specs=out_specs,
        out_shape=out_shape,
        compiler_params=_params(("parallel",)),
        name="prep",
    )(*args)


def _chunk_cumsum(g, c):
    row = lax.broadcasted_iota(jnp.int32, g.shape, 0)
    sh = 1
    while sh < c:
        g = g + jnp.where(row >= sh, pltpu.roll(g, sh, axis=0), 0.0)
        sh *= 2
    return g


def _gdn_stacked_kernel(q_ref, k_ref, v_ref, z_ref, gb_ref, ng_ref, yb_ref, sout_ref, s_scr, *, c, grp, nb):
    ci = pl.program_id(1)

    @pl.when(ci == 0)
    def _():
        s_scr[...] = jnp.zeros_like(s_scr)

    r = grp * c
    ng = ng_ref[...]
    seqs = []
    for s in range(nb):
        gb = gb_ref[s]
        gc = _chunk_cumsum(gb, c)
        g_last = gc[c - 1:c, :]
        seqs.append(dict(gb=gb, gc=gc, gct=gc.T, e_g=jnp.exp(gc), e_kd=jnp.exp(g_last - gc), e_last=jnp.exp(g_last)))

    ri = lax.broadcasted_iota(jnp.int32, (r, r), 0)
    cj = lax.broadcasted_iota(jnp.int32, (r, r), 1)
    same = (ri // c) == (cj // c)
    incl = same & (ri >= cj)
    strict = same & (ri > cj)
    head_block = (lax.broadcasted_iota(jnp.int32, (r, grp * GDN_DK), 0) // c
                  == lax.broadcasted_iota(jnp.int32, (r, grp * GDN_DK), 1) // GDN_DK)

    def block_layout(x):
        return jnp.where(head_block, jnp.concatenate([x] * grp, axis=1), 0.0)

    units = []
    for s in range(nb):
        for g0 in range(0, GDN_HEADS, grp):
            heads = list(range(g0, g0 + grp))
            sq = seqs[s]

            def stack(ref, s=s, heads=heads):
                return jnp.concatenate([ref[s, :, h * GDN_DK:(h + 1) * GDN_DK] for h in heads], axis=0)

            def col(a, off=0, heads=heads):
                return jnp.concatenate([a[:, off + h:off + h + 1] for h in heads], axis=0)

            qs, ks, vs = stack(q_ref), stack(k_ref), stack(v_ref)
            beta = col(sq["gb"], GDN_HEADS)
            kb = ks * beta
            units.append(dict(s=s, heads=heads, qs=qs, ks=ks, kb=kb, vb=vs * beta,
                              gcol=col(sq["gc"]), egcol=col(sq["e_g"]), ekdcol=col(sq["e_kd"]),
                              grow=jnp.concatenate([sq["gct"][h:h + 1, :] for h in heads], axis=1),
                              rows=slice((s * GDN_HEADS + g0) * GDN_DK, (s * GDN_HEADS + g0 + grp) * GDN_DK)))

    for u in units:
        u["aq"] = _dot_nt(jnp.concatenate([u["kb"], u["qs"]], axis=0), u["ks"])
    for u in units:
        decay = jnp.exp(jnp.where(incl, u["gcol"] - u["grow"], -jnp.inf))
        u["pw"] = -jnp.where(strict, u["aq"][:r] * decay, 0.0)
        u["attn"] = jnp.where(incl, u["aq"][r:] * decay, 0.0)
        u["x"] = jnp.concatenate([u["vb"], u["kb"] * u["egcol"]], axis=1)
    span = 1
    while span < c:
        span *= 2
        for u in units:
            if span < c:
                both = _dot(u["pw"], jnp.concatenate([u["pw"], u["x"]], axis=1))
                u["pw"] = both[:, :r]
                u["x"] = u["x"] + both[:, r:]
            else:
                u["x"] = u["x"] + _dot(u["pw"], u["x"])
    for u in units:
        u_val, w_k = u["x"][:, :GDN_DV], u["x"][:, GDN_DV:]
        u["s_old"] = s_scr[u["rows"], :]
        ws = _dot(jnp.concatenate([block_layout(w_k), block_layout(u["qs"] * u["egcol"])], axis=0), u["s_old"])
        u["v_new"] = u_val - ws[:r]
        u["o"] = ws[r:]
    for u in units:
        e_last = seqs[u["s"]]["e_last"]
        o = u["o"] + _dot(u["attn"], u["v_new"])
        e_rows = jnp.concatenate([jnp.broadcast_to(e_last[:, h:h + 1], (GDN_DK, 1)) for h in u["heads"]], axis=0)
        kd_t = block_layout(u["ks"] * u["ekdcol"]).T
        s_scr[u["rows"], :] = u["s_old"] * e_rows + _dot(kd_t, u["v_new"])
        for i, h in enumerate(u["heads"]):
            sl = slice(h * GDN_DK, (h + 1) * GDN_DK)
            oh = o[i * c:(i + 1) * c, :]
            zh = z_ref[u["s"], :, sl].astype(F32)
            yb_ref[u["s"], :, sl] = (((oh * _rms_scale(oh)) * ng) * (zh * jax.nn.sigmoid(zh))).astype(BF16)

    @pl.when(ci == pl.num_programs(1) - 1)
    def _():
        sout_ref[...] = s_scr[...]


def _gdn_stacked(q, k, v, p, gb, norm_g, bsz, t, c, nb):
    d = D_MODEL
    nc = t // c
    grp = min(GDN_HEADS, 256 // c)
    s_rows = nb * GDN_HEADS * GDN_DK

    def tok(col=0):
        return pl.BlockSpec((nb, c, d), lambda b, i, col=col: (b, i, col))

    yb, s_new = pl.pallas_call(
        functools.partial(_gdn_stacked_kernel, c=c, grp=grp, nb=nb),
        grid=(bsz // nb, nc),
        in_specs=[tok(), tok(), tok(), tok(COL_GZ),
                  pl.BlockSpec((nb, c, SMALL_COLS), lambda b, i: (b, i, 0)),
                  pl.BlockSpec((1, GDN_DV), lambda b, i: (0, 0))],
        out_specs=[tok(), pl.BlockSpec((s_rows, GDN_DV), lambda b, i: (b, 0))],
        out_shape=[jax.ShapeDtypeStruct((bsz, t, d), BF16),
                   jax.ShapeDtypeStruct((bsz * GDN_HEADS * GDN_DK, GDN_DV), F32)],
        scratch_shapes=[pltpu.VMEM((s_rows, GDN_DV), F32)],
        compiler_params=_params(("parallel", "arbitrary")),
        name="gdn_scan_prompt",
    )(*[a.reshape(bsz, t, -1) for a in (q, k, v, p, gb)], norm_g.reshape(1, GDN_DV))
    return yb.reshape(bsz * t, d), s_new.reshape(bsz, GDN_HEADS, GDN_DK, GDN_DV)


def _gdn_kernel(*refs, c, nb, has_s0):
    if has_s0:
        q_ref, k_ref, v_ref, z_ref, gb_ref, ng_ref, s0_ref, yb_ref, sout_ref, s_scr = refs
    else:
        q_ref, k_ref, v_ref, z_ref, gb_ref, ng_ref, yb_ref, sout_ref, s_scr = refs
    ci = pl.program_id(1)

    @pl.when(ci == 0)
    def _():
        if has_s0:
            s_scr[...] = s0_ref[...]
        else:
            s_scr[...] = jnp.zeros_like(s_scr)

    ri = lax.broadcasted_iota(jnp.int32, (c, c), 0)
    cj = lax.broadcasted_iota(jnp.int32, (c, c), 1)
    incl = ri >= cj
    strict = ri > cj
    ng = ng_ref[...]
    row = lax.broadcasted_iota(jnp.int32, (c, SMALL_COLS), 0)
    chains = [(s, h) for s in range(nb) for h in range(GDN_HEADS)]

    seq = []
    for s in range(nb):
        gb = gb_ref[s]
        gc = gb
        sh = 1
        while sh < c:
            gc = gc + jnp.where(row >= sh, pltpu.roll(gc, sh, axis=0), 0.0)
            sh *= 2
        g_last = gc[c - 1:c, :]
        seq.append(dict(gb=gb, gc=gc, gct=gc.T, e_g=jnp.exp(gc), e_kd=jnp.exp(g_last - gc), e_last=jnp.exp(g_last)))

    st = []
    for s, h in chains:
        sl = slice(h * GDN_DK, (h + 1) * GDN_DK)
        sq = seq[s]
        qh, kh, vh = q_ref[s, :, sl], k_ref[s, :, sl], v_ref[s, :, sl]
        beta = sq["gb"][:, GDN_HEADS + h:GDN_HEADS + h + 1]
        kb = kh * beta
        aq = _dot_nt(jnp.concatenate([kb, qh], axis=0), kh)
        st.append(dict(sl=sl, qh=qh, kh=kh, kb=kb, vb=vh * beta, aq=aq))

    for (s, h), e in zip(chains, st):
        sq = seq[s]
        decay = jnp.exp(jnp.where(incl, sq["gc"][:, h:h + 1] - sq["gct"][h:h + 1, :], -jnp.inf))
        e["neg"] = -jnp.where(strict, e["aq"][:c] * decay, 0.0)
        e["attn"] = jnp.where(incl, e["aq"][c:] * decay, 0.0)
        e["x"] = jnp.concatenate([e["vb"], e["kb"] * sq["e_g"][:, h:h + 1]], axis=1)
        e["pw"] = e["neg"]

    span = 1
    while span < c:
        span *= 2
        for e in st:
            if span < c:
                both = _dot(e["pw"], jnp.concatenate([e["pw"], e["x"]], axis=1))
                e["pw"] = both[:, :c]
                e["x"] = e["x"] + both[:, c:]
            else:
                e["x"] = e["x"] + _dot(e["pw"], e["x"])

    for (s, h), e in zip(chains, st):
        sq = seq[s]
        u_val, w_k = e["x"][:, :GDN_DV], e["x"][:, GDN_DV:]
        q_dec = e["qh"] * sq["e_g"][:, h:h + 1]
        e["s_old"] = s_scr[s, h]
        ws = _dot(jnp.concatenate([w_k, q_dec], axis=0), e["s_old"])
        e["v_new"] = u_val - ws[:c]
        e["o_inter"] = ws[c:]

    for (s, h), e in zip(chains, st):
        sq = seq[s]
        o = e["o_inter"] + _dot(e["attn"], e["v_new"])
        k_dec = e["kh"] * sq["e_kd"][:, h:h + 1]
        s_scr[s, h] = e["s_old"] * sq["e_last"][:, h:h + 1] + _dot(k_dec.T, e["v_new"])
        zh = z_ref[s, :, e["sl"]].astype(F32)
        yb_ref[s, :, e["sl"]] = (((o * _rms_scale(o)) * ng) * (zh * jax.nn.sigmoid(zh))).astype(BF16)

    @pl.when(ci == pl.num_programs(1) - 1)
    def _():
        sout_ref[...] = s_scr[...]


def _gdn(q, k, v, p, gb, norm_g, bsz, t, c, nb, s0=None):
    d = D_MODEL
    nc = t // c
    has_s0 = s0 is not None

    def tok(col=0):
        return pl.BlockSpec((nb, c, d), lambda b, i, col=col: (b, i, col))

    in_specs = [tok(), tok(), tok(), tok(COL_GZ),
                pl.BlockSpec((nb, c, SMALL_COLS), lambda b, i: (b, i, 0)),
                pl.BlockSpec((1, GDN_DV), lambda b, i: (0, 0))]
    args = [a.reshape(bsz, t, -1) for a in (q, k, v, p, gb)] + [norm_g.reshape(1, GDN_DV)]
    s_spec = pl.BlockSpec((nb, GDN_HEADS, GDN_DK, GDN_DV), lambda b, i: (b, 0, 0, 0))
    if has_s0:
        in_specs.append(s_spec)
        args.append(s0)
    yb, s_new = pl.pallas_call(
        functools.partial(_gdn_kernel, c=c, nb=nb, has_s0=has_s0),
        grid=(bsz // nb, nc),
        in_specs=in_specs,
        out_specs=[tok(), s_spec],
        out_shape=[jax.ShapeDtypeStruct((bsz, t, d), BF16),
                   jax.ShapeDtypeStruct((bsz, GDN_HEADS, GDN_DK, GDN_DV), F32)],
        scratch_shapes=[pltpu.VMEM((nb, GDN_HEADS, GDN_DK, GDN_DV), F32)],
        compiler_params=_params(("parallel", "arbitrary")),
        name="gdn_scan",
    )(*args)
    return yb.reshape(bsz * t, d), s_new


def _mem_attn_kernel(q_ref, k_ref, v_ref, o_ref, *, nb, tq):
    scale = MEM_HEAD_DIM ** -0.5
    pairs = [(s, h) for s in range(nb) for h in range(MEM_HEADS)]
    span = lambda s, h: (slice(s * tq, (s + 1) * tq), slice(h * MEM_HEAD_DIM, (h + 1) * MEM_HEAD_DIM))
    scores = [_dot_nt(q_ref[span(s, h)], k_ref[s, :, span(s, h)[1]]) * scale for s, h in pairs]
    probs = []
    for sc in scores:
        e = jnp.exp(sc - jnp.max(sc, axis=-1, keepdims=True))
        probs.append(e / jnp.sum(e, axis=-1, keepdims=True))
    for (s, h), prob in zip(pairs, probs):
        o_ref[span(s, h)] = _dot(prob, v_ref[s, :, span(s, h)[1]]).astype(BF16)


def _mem_attn(p, mk, mv, bsz, t, tq, nb):
    d = D_MODEL
    m = mk.shape[1]
    nq = t // tq
    assert nb == 1 or nq == 1
    kv_spec = pl.BlockSpec((nb, m, d), lambda b, i: (b, 0, 0))
    return pl.pallas_call(
        functools.partial(_mem_attn_kernel, nb=nb, tq=tq),
        grid=(bsz // nb, nq),
        in_specs=[pl.BlockSpec((nb * tq, d), lambda b, i: (b * nq + i, COL_MQ)), kv_spec, kv_spec],
        out_specs=pl.BlockSpec((nb * tq, d), lambda b, i: (b * nq + i, 0)),
        out_shape=jax.ShapeDtypeStruct((bsz * t, d), BF16),
        compiler_params=_params(("parallel", "arbitrary")),
        name="mem_attn",
    )(p, mk, mv)


def _merge_kernel(x_ref, ya_ref, yb_ref, ym_ref, g0_ref, g1_ref, g2_ref, wc_ref, wg_ref, wm_ref, wo_ref,
                  ng_ref, x1_ref, h2_ref):
    d = functools.partial(jnp.dot, preferred_element_type=F32)
    merged = (jax.nn.sigmoid(_f32(g0_ref)) * d(ya_ref[...], wc_ref[...])
              + jax.nn.sigmoid(_f32(g1_ref)) * d(yb_ref[...], wg_ref[...])
              + jax.nn.sigmoid(_f32(g2_ref)) * d(ym_ref[...], wm_ref[...]))
    x1 = x_ref[...] + d(merged.astype(BF16), wo_ref[...])
    x1_ref[...] = x1
    h2_ref[...] = ((x1 * _rms_scale(x1)) * ng_ref[...]).astype(BF16)


def _merge(x, ya, yb, ym, p, wc, wg, wm, wo, norm_g, tm):
    n, d = x.shape
    tok = pl.BlockSpec((tm, d), lambda i: (i, 0))
    wsp = pl.BlockSpec((d, d), lambda i: (0, 0))

    def gate(j):
        return pl.BlockSpec((tm, d), lambda i, j=j: (i, COL_BG + j))

    return pl.pallas_call(
        _merge_kernel,
        grid=(n // tm,),
        in_specs=[tok, tok, tok, tok, gate(0), gate(1), gate(2), wsp, wsp, wsp, wsp,
                  pl.BlockSpec((1, d), lambda i: (0, 0))],
        out_specs=[tok, tok],
        out_shape=[jax.ShapeDtypeStruct((n, d), F32), jax.ShapeDtypeStruct((n, d), BF16)],
        compiler_params=_params(("parallel",)),
        name="merge",
    )(x, ya, yb, ym, p, p, p, wc, wg, wm, wo, norm_g.reshape(1, d))


_CAND_CELLS = tuple((a, b) for a in range(PEER_TOPK) for b in range(PEER_TOPK) if (a + 1) * (b + 1) <= PEER_TOPK)

TILE_PAD = 4 * SUBLANES
SHIFT_S1, SHIFT_E1 = 0, SUBLANES


def _put_tile(ref, h, ns, shift, val, pad=True):
    if pad and shift:
        ref[h, ns, :shift, :] = jnp.zeros((shift, LANES), F32)
    ref[h, ns, shift:shift + PEER_N_KEYS, :] = val
    if pad and shift < TILE_PAD:
        ref[h, ns, shift + PEER_N_KEYS:, :] = jnp.zeros((TILE_PAD - shift, LANES), F32)


def _top_values(x, count):
    vals = []
    cur = x
    for r in range(count):
        m = jnp.max(cur, axis=0, keepdims=True)
        vals.append(m)
        if r + 1 < count:
            cur = jnp.where(cur == m, -jnp.inf, cur)
    return vals


def _top_ranked(x, count):
    n_rows = x.shape[0]
    iota = lax.broadcasted_iota(jnp.int32, x.shape, 0).astype(F32)
    vals = []
    cur = x
    rank = jnp.full(x.shape, float(count), F32)
    for r in range(count):
        m = jnp.max(cur, axis=0, keepdims=True)
        first = jnp.min(jnp.where(cur == m, iota, float(n_rows)), axis=0, keepdims=True)
        one = iota == first
        rank = jnp.where(one, float(r), rank)
        cur = jnp.where(one, -jnp.inf, cur)
        vals.append(m)
    return vals, rank


def _count_rows(mask):
    return jnp.sum(jnp.where(mask, 1.0, 0.0), axis=0, keepdims=True)


def _route_kernel(h_ref, wq_ref, keys_ref, cut_ref, e0_ref, s1_ref, e1_ref, q_scr, tie_ref):
    tn = h_ref.shape[0]
    nsb = tn // LANES
    q = jnp.dot(h_ref[...], wq_ref[...], preferred_element_type=F32).astype(BF16)

    def store(h, cut, e0, s1, e1):
        for ns in range(nsb):
            cols = slice(ns * LANES, (ns + 1) * LANES)
            cut_ref[h, ns] = cut[:, cols]
            e0_ref[h, ns] = e0[:, cols]
            _put_tile(s1_ref, h, ns, SHIFT_S1, s1[:, cols])
            _put_tile(e1_ref, h, ns, SHIFT_E1, e1[:, cols])

    for h in range(PEER_HEADS):
        sc, top = [], []
        tied = jnp.zeros((1, tn), F32)
        for part in range(2):
            col = (2 * h + part) * PEER_D_HALF
            q_scr[2 * h + part] = q[:, col:col + PEER_D_HALF]
            s_t = lax.dot_general(keys_ref[part, h], q[:, col:col + PEER_D_HALF],
                                  (((1,), (1,)), ((), ())), preferred_element_type=F32)
            sc.append(s_t)
            top.append(_top_values(s_t, PEER_TOPK))
            tied = jnp.maximum(tied, jnp.abs(_count_rows(s_t >= top[part][-1]) - PEER_TOPK))
        cand = jnp.concatenate([top[0][a] + top[1][b] for a, b in _CAND_CELLS], axis=0)
        best = _top_values(cand, PEER_TOPK)
        c_max, thr = best[0], best[-1]
        z = jnp.exp(best[0] - c_max)
        for r in range(1, PEER_TOPK):
            z = z + jnp.exp(best[r] - c_max)
        top1 = jnp.concatenate(top[1], axis=0)
        n_sel = sum(_count_rows(top[0][a] + top1 >= thr) for a in range(PEER_TOPK))
        tied = jnp.maximum(tied, jnp.abs(n_sel - PEER_TOPK))
        cut = jnp.full(sc[0].shape, jnp.inf, F32)
        for b in range(PEER_TOPK):
            cut = jnp.where(sc[0] + top[1][b] >= thr, top[1][b], cut)
        store(h, cut, jnp.exp(sc[0] - top[0][0]) / z, sc[1], jnp.exp(sc[1] - top[1][0]))
        for ns in range(nsb):
            tie_ref[h * nsb + ns] = jnp.max(tied[:, ns * LANES:(ns + 1) * LANES])

    def exact_block(idx, carry):
        @pl.when(tie_ref[idx] > 0.0)
        def _():
            h = idx // nsb
            ns = idx % nsb
            rows = pl.ds(pl.multiple_of(ns * LANES, LANES), LANES)
            sc, top, rank = [], [], []
            for part in range(2):
                s_t = lax.dot_general(keys_ref[part, h], q_scr[2 * h + part, rows, :],
                                      (((1,), (1,)), ((), ())), preferred_element_type=F32)
                vals, rk = _top_ranked(s_t, PEER_TOPK)
                sc.append(s_t)
                top.append(vals)
                rank.append(rk)
            top1 = jnp.concatenate(top[1], axis=0)
            cand = jnp.concatenate([top[0][a] + top1 for a in range(PEER_TOPK)], axis=0)
            best, pos = _top_ranked(cand, PEER_TOPK)
            z = jnp.exp(best[0] - best[0])
            for r in range(1, PEER_TOPK):
                z = z + jnp.exp(best[r] - best[0])
            cnt = jnp.zeros_like(sc[0])
            for a in range(PEER_TOPK):
                cnt_a = _count_rows(pos[a * PEER_TOPK:(a + 1) * PEER_TOPK] < PEER_TOPK)
                cnt = jnp.where(rank[0] == float(a), cnt_a, cnt)
            cut_ref[h, ns] = jnp.where(cnt > 0.0, 1.0 - cnt, jnp.inf)
            e0_ref[h, ns] = jnp.exp(sc[0] - top[0][0]) / z
            _put_tile(s1_ref, h, ns, SHIFT_S1, -rank[1], pad=False)
            _put_tile(e1_ref, h, ns, SHIFT_E1, jnp.exp(sc[1] - top[1][0]), pad=False)
        return carry

    lax.fori_loop(0, PEER_HEADS * nsb, exact_block, 0)


def _route(h2, wq, keys, tn):
    n, d = h2.shape
    nk = PEER_N_KEYS
    tiled = pl.BlockSpec((PEER_HEADS, tn // LANES, nk + TILE_PAD, LANES), lambda i: (0, i, 0, 0))
    tiled_shape = jax.ShapeDtypeStruct((PEER_HEADS, n // LANES, nk + TILE_PAD, LANES), F32)
    rowed = pl.BlockSpec((PEER_HEADS, tn // LANES, nk, LANES), lambda i: (0, i, 0, 0))
    rowed_shape = jax.ShapeDtypeStruct((PEER_HEADS, n // LANES, nk, LANES), F32)
    return pl.pallas_call(
        _route_kernel,
        grid=(n // tn,),
        in_specs=[pl.BlockSpec((tn, d), lambda i: (i, 0)),
                  pl.BlockSpec(wq.shape, lambda i: (0, 0)),
                  pl.BlockSpec(keys.shape, lambda i: (0, 0, 0, 0))],
        out_specs=[rowed, rowed, tiled, tiled],
        out_shape=[rowed_shape, rowed_shape, tiled_shape, tiled_shape],
        scratch_shapes=[pltpu.VMEM((2 * PEER_HEADS, tn, PEER_D_HALF), BF16),
                        pltpu.SMEM((PEER_HEADS * (tn // LANES),), F32)],
        compiler_params=_params(("parallel",)),
        name="peer_route",
    )(h2, wq, keys)


def _gelu(x):
    return 0.5 * x * (1.0 + lax.erf(x * (2.0 ** -0.5)))


PEER_TILE = 2048
GATE_HALF = 32
GATE_ROWS = 8


def _peer_kernel(h_ref, x1_ref, u_ref, vt_ref, cut_ref, e0_ref, s1_ref, e1_ref, ng_ref,
                 y_ref, acc_ref, act_ref, coef_ref, *, te):
    ei = pl.program_id(1)
    tn = h_ref.shape[0]
    groups = te // PEER_N_KEYS

    @pl.when(ei == 0)
    def _():
        acc_ref[...] = jnp.zeros_like(acc_ref)

    nsb = tn // LANES
    act = _gelu(lax.dot_general(u_ref[...], h_ref[...], (((1,), (1,)), ((), ())),
                                preferred_element_type=F32))
    for ns in range(nsb):
        act_ref[ns] = act[:, ns * LANES:(ns + 1) * LANES]

    def gate_block(ns):
        for jh in range(PEER_N_KEYS // GATE_HALF):
            jrows = slice(jh * GATE_HALF, (jh + 1) * GATE_HALF)
            for g0 in range(0, groups, GATE_ROWS):
                gates = [None] * GATE_ROWS
                for h in range(PEER_HEADS):
                    s1_t = s1_ref[h, ns, SHIFT_S1 + jh * GATE_HALF:SHIFT_S1 + (jh + 1) * GATE_HALF, :]
                    e1_t = e1_ref[h, ns, SHIFT_E1 + jh * GATE_HALF:SHIFT_E1 + (jh + 1) * GATE_HALF, :]
                    for k in range(GATE_ROWS):
                        sel = s1_t >= cut_ref[h, ns, g0 + k:g0 + k + 1, :]
                        term = jnp.where(sel, e1_t, 0.0) * e0_ref[h, ns, g0 + k:g0 + k + 1, :]
                        gates[k] = term if gates[k] is None else gates[k] + term
                for k in range(GATE_ROWS):
                    r0 = (g0 + k) * PEER_N_KEYS + jh * GATE_HALF
                    coef_ref[ns, r0:r0 + GATE_HALF, :] = (gates[k] * act_ref[ns, r0:r0 + GATE_HALF, :]).astype(BF16)

    def lane_block(ns, carry):
        gate_block(ns)
        return carry

    def mix_values(lo, hi):
        coef = jnp.concatenate([coef_ref[ns] for ns in range(lo, hi)], axis=1)
        acc_ref[:, lo * LANES:hi * LANES] += jnp.dot(vt_ref[0], coef, preferred_element_type=F32)

    half = nsb // 2
    lax.fori_loop(0, half, lane_block, 0)
    if half:
        mix_values(0, half)
    for ns in range(half, nsb):
        gate_block(ns)
    mix_values(half, nsb)

    @pl.when(ei == pl.num_programs(1) - 1)
    def _():
        x2 = x1_ref[...] + acc_ref[...].T
        y_ref[...] = (x2 * _rms_scale(x2)) * ng_ref[...]


def _peer(h2, x1, u_bf, vt_bf, cut, e0, s1, e1, norm_g, tn):
    n, d = h2.shape
    ne = u_bf.shape[0]
    te = vt_bf.shape[2]
    nk = PEER_N_KEYS
    tok = pl.BlockSpec((tn, d), lambda i, e: (i, 0))
    tiled = pl.BlockSpec((PEER_HEADS, tn // LANES, nk + TILE_PAD, LANES), lambda i, e: (0, i, 0, 0))
    rowed = pl.BlockSpec((PEER_HEADS, tn // LANES, te // nk, LANES), lambda i, e: (0, i, e, 0))
    return pl.pallas_call(
        functools.partial(_peer_kernel, te=te),
        grid=(n // tn, ne // te),
        in_specs=[tok, tok,
                  pl.BlockSpec((te, d), lambda i, e: (e, 0)),
                  pl.BlockSpec((1, d, te), lambda i, e: (e, 0, 0)),
                  rowed, rowed, tiled, tiled,
                  pl.BlockSpec((1, d), lambda i, e: (0, 0))],
        out_specs=tok,
        out_shape=jax.ShapeDtypeStruct((n, d), F32),
        scratch_shapes=[pltpu.VMEM((d, tn), F32), pltpu.VMEM((tn // LANES, te, LANES), F32),
                        pltpu.VMEM((tn // LANES, te, LANES), BF16)],
        compiler_params=_params(("parallel", "arbitrary")),
        name="peer_mix",
    )(h2, x1, u_bf, vt_bf, cut, e0, s1, e1, norm_g.reshape(1, d))


def _tile(n, pref):
    return math.gcd(n, pref)


def _layer(x, mk, mv, bsz, t, weights, conv_state=None, gdn_conv_state=None, gdn_s0=None):
    (norm_mix_g, w_main, w_ab, conv_w, gdn_conv_w, a_log, dt_bias, gdn_norm_g, wc, wg, wm, wo, norm_ffn_g,
     wq, keys, u_bf, vt_bf, final_norm_g) = weights
    n = bsz * t
    sample = conv_state is not None
    p, pab = _inproj(x, norm_mix_g, w_main, w_ab, _tile(n, 1024))
    rows = _tile(n, 256) if sample else _tile(t, 512)
    ya, q, k, v, gb, u_tail = _prep(p, pab, conv_w, gdn_conv_w, a_log, dt_bias, rows, max(t // rows, 1),
                                    conv_state, gdn_conv_state)
    chunk = _tile(t, 64)
    if sample:
        yb, s_new = _gdn(q, k, v, p, gb, gdn_norm_g, bsz, t, chunk, 4, gdn_s0)
    else:
        yb, s_new = _gdn_stacked(q, k, v, p, gb, gdn_norm_g, bsz, t, chunk, _tile(bsz, 2))
    ym = _mem_attn(p, mk, mv, bsz, t, _tile(t, 512), _tile(bsz, 4) if sample else 1)
    x1, h2 = _merge(x, ya, yb, ym, p, wc, wg, wm, wo, norm_ffn_g, _tile(n, 256))
    if sample:
        keep = lambda a: a.reshape(bsz, t, -1)[:, t // 2:, :].reshape(bsz * t // 2, -1)
        x1, h2 = keep(x1), keep(h2)
    tn = _tile(x1.shape[0], 512)
    cut, e0, s1, e1 = _route(h2, wq, keys, tn)
    y = _peer(h2, x1, u_bf, vt_bf, cut, e0, s1, e1, final_norm_g, tn)
    return y, u_tail, p, s_new


def kernel(x_prompt, x_sample, mem_prompt, cache_mem_k, cache_mem_v, state_conv, state_gdn_conv, state_gdn, norm_mix_g, w_in, conv_w, gdn_conv_w, gdn_a_log, gdn_dt_bias, gdn_norm_g, mem_norm_g, w_mem_kv, w_br_conv, w_br_gdn, w_br_mem, w_out, norm_ffn_g, peer_w_q, peer_sub_keys, peer_u, peer_v, final_norm_g):
    depth = w_in.shape[0]
    assert depth == 1, "single-layer step"
    d = D_MODEL
    bp, tp, _ = x_prompt.shape
    bs, ts, _ = x_sample.shape
    assert ts == SAMPLE_PAD_T // 2
    m_tok = mem_prompt.shape[1]
    mq = MEM_HEADS * MEM_HEAD_DIM
    l = 0

    w = w_in[l]
    ga0 = 7 * d
    w_main = jnp.concatenate([w[:, :ga0], w[:, ga0 + 2 * GDN_HEADS:]], axis=1).astype(BF16)
    w_ab = jnp.pad(w[:, ga0:ga0 + 2 * GDN_HEADS], ((0, 0), (0, SMALL_COLS - 2 * GDN_HEADS))).astype(BF16)
    weights = (norm_mix_g[l], w_main, w_ab, conv_w[l], gdn_conv_w[l], gdn_a_log[l], gdn_dt_bias[l], gdn_norm_g[l],
               w_br_conv[l].astype(BF16), w_br_gdn[l].astype(BF16), w_br_mem[l].astype(BF16), w_out[l].astype(BF16),
               norm_ffn_g[l], peer_w_q[l].astype(BF16), peer_sub_keys[l].astype(BF16),
               peer_u[l].astype(BF16),
               peer_v[l].astype(BF16).reshape(-1, PEER_TILE, d).transpose(0, 2, 1), final_norm_g)

    kv_shape = (bp, m_tok, mq)
    mk_p, mv_p = _mem_kv(mem_prompt.reshape(bp * m_tok, d), mem_norm_g[l], w_mem_kv[l].astype(BF16),
                         _tile(bp * m_tok, 256))
    y_p, u_tail_p, p_p, s_p = _layer(x_prompt.reshape(bp * tp, d), mk_p.reshape(kv_shape), mv_p.reshape(kv_shape),
                                     bp, tp, weights)
    rows_p = _tile(tp, 512)
    cb_p = u_tail_p.reshape(bp, tp // rows_p, SUBLANES, d)[:, -1, SUBLANES - (CONV_K - 1):, :]
    gb_p = p_p.reshape(bp, tp, -1)[:, tp - (GDN_CONV_K - 1):, COL_GQ * d:(COL_GV + 1) * d].astype(F32)

    pad_t = SAMPLE_PAD_T
    lead = pad_t - ts
    x_s = jnp.pad(x_sample, ((0, 0), (lead, 0), (0, 0))).reshape(bs * pad_t, d)
    cst = jnp.pad(state_conv[l], ((0, 0), (lead - (CONV_K - 1), ts), (0, 0))).reshape(bs * pad_t, d)
    gst = jnp.pad(state_gdn_conv[l], ((0, 0), (lead - (GDN_CONV_K - 1), ts), (0, 0))).reshape(bs * pad_t, 3 * d)
    y_s, u_s, p_s, s_s = _layer(x_s, cache_mem_k.reshape(bs, -1, mq), cache_mem_v.reshape(bs, -1, mq),
                                bs, pad_t, weights, cst, gst, state_gdn.reshape(bs, GDN_HEADS, GDN_DK, GDN_DV))
    y_s = y_s.reshape(bs, ts, d)
    cb_s = u_s.reshape(bs, pad_t, d)[:, pad_t - (CONV_K - 1):, :]
    gb_s = p_s.reshape(bs, pad_t, -1)[:, pad_t - (GDN_CONV_K - 1):, COL_GQ * d:(COL_GV + 1) * d].astype(F32)

    mshape = (depth, bp, m_tok, MEM_HEADS, MEM_HEAD_DIM)
    return (y_p.reshape(bp, tp, d), y_s,
            mk_p.reshape(mshape), mv_p.reshape(mshape),
            cb_p[None], gb_p[None], s_p[None],
            cb_s[None], gb_s[None], s_s[None])
```

```python
import functools
import math

import jax
import jax.numpy as jnp
from jax import lax
from jax.experimental import pallas as pl
from jax.experimental.pallas import tpu as pltpu

F32 = jnp.float32
BF16 = jnp.bfloat16

EPS = 1e-6
D_MODEL = 1024
CONV_K = 3
GDN_HEADS = 8
GDN_DK = 128
GDN_DV = 128
GDN_CONV_K = 4
MEM_HEADS = 4
MEM_HEAD_DIM = 256
PEER_HEADS = 8
PEER_N_KEYS = 128
PEER_D_HALF = 128
PEER_TOPK = 16
N_BRANCHES = 3

COL_AU, COL_AB, COL_AC, COL_GQ, COL_GK, COL_GV, COL_GZ, COL_MQ, COL_BG = 0, 1, 2, 3, 4, 5, 6, 7, 8
N_MAIN_BLOCKS = 11
SMALL_COLS = 128
SUBLANES = 8
LANES = 128
SAMPLE_PAD_T = 8
VMEM_LIMIT = 56 * 1024 * 1024


def _params(sem):
    return pltpu.CompilerParams(dimension_semantics=sem, vmem_limit_bytes=VMEM_LIMIT)


def _dot(a, b):
    return jnp.dot(a.astype(BF16), b.astype(BF16), preferred_element_type=F32)


def _dot_nt(a, b):
    return lax.dot_general(a.astype(BF16), b.astype(BF16), (((1,), (1,)), ((), ())),
                           preferred_element_type=F32)


def _rms_scale(x):
    return lax.rsqrt(jnp.mean(x * x, axis=-1, keepdims=True) + EPS)


def _inproj_kernel(x_ref, g_ref, w_ref, wab_ref, p_ref, pab_ref, h_ref):
    @pl.when(pl.program_id(1) == 0)
    def _():
        x = x_ref[...]
        h = ((x * _rms_scale(x)) * g_ref[...]).astype(BF16)
        h_ref[...] = h
        pab_ref[...] = jnp.dot(h, wab_ref[...], preferred_element_type=F32)

    p_ref[...] = jnp.dot(h_ref[...], w_ref[...], preferred_element_type=F32).astype(p_ref.dtype)


def _inproj(x, g, w_main, w_ab, tm):
    n, d = x.shape
    nblk = w_main.shape[1] // d
    return pl.pallas_call(
        _inproj_kernel,
        grid=(n // tm, nblk),
        in_specs=[
            pl.BlockSpec((tm, d), lambda i, j: (i, 0)),
            pl.BlockSpec((1, d), lambda i, j: (0, 0)),
            pl.BlockSpec((d, d), lambda i, j: (0, j)),
            pl.BlockSpec((d, SMALL_COLS), lambda i, j: (0, 0)),
        ],
        out_specs=[
            pl.BlockSpec((tm, d), lambda i, j: (i, j)),
            pl.BlockSpec((tm, SMALL_COLS), lambda i, j: (i, 0)),
        ],
        out_shape=[
            jax.ShapeDtypeStruct((n, nblk * d), BF16),
            jax.ShapeDtypeStruct((n, SMALL_COLS), F32),
        ],
        scratch_shapes=[pltpu.VMEM((tm, d), BF16)],
        compiler_params=_params(("parallel", "arbitrary")),
        name="inproj",
    )(x, g.reshape(1, d), w_main, w_ab)


def _mem_kv_kernel(x_ref, g_ref, w_ref, k_ref, v_ref):
    x = x_ref[...]
    h = ((x * _rms_scale(x)) * g_ref[...]).astype(BF16)
    kv = jnp.dot(h, w_ref[...], preferred_element_type=F32)
    half = kv.shape[1] // 2
    k_ref[...] = kv[:, :half]
    v_ref[...] = kv[:, half:]


def _mem_kv(x, g, w, tm):
    n, d = x.shape
    c = w.shape[1]
    out = jax.ShapeDtypeStruct((n, c // 2), F32)
    ospec = pl.BlockSpec((tm, c // 2), lambda i: (i, 0))
    return pl.pallas_call(
        _mem_kv_kernel,
        grid=(n // tm,),
        in_specs=[
            pl.BlockSpec((tm, d), lambda i: (i, 0)),
            pl.BlockSpec((1, d), lambda i: (0, 0)),
            pl.BlockSpec((d, c), lambda i: (0, 0)),
        ],
        out_specs=[ospec, ospec],
        out_shape=[out, out],
        compiler_params=_params(("parallel",)),
        name="mem_kv",
    )(x, g.reshape(1, d), w)


def _causal_conv(prev, cur, w_ref, k):
    full = jnp.concatenate([prev, cur], axis=0)
    y = cur * w_ref[k - 1:k, :]
    for s in range(1, k):
        y = y + pltpu.roll(full, s, axis=0)[prev.shape[0]:, :] * w_ref[k - 1 - s:k - s, :]
    return y


def _f32(ref):
    return ref[...].astype(F32)


def _prep_kernel(*refs, rows, tiles_per_seq, sample):
    if sample:
        (au, ab, ac, gq, gk, gv, pab, au_p, ac_p, gq_p, gk_p, gv_p, cw, gw, alog, dtb, cst, gst,
         ya_o, q_o, k_o, v_o, gb_o, ut_o) = refs
    else:
        (au, ab, ac, gq, gk, gv, pab, au_p, ac_p, gq_p, gk_p, gv_p, cw, gw, alog, dtb,
         ya_o, q_o, k_o, v_o, gb_o, ut_o) = refs
    d = D_MODEL
    first = (pl.program_id(0) % tiles_per_seq) == 0
    keep = jnp.where(first, 0.0, 1.0).astype(F32)

    u = _f32(ac) * _f32(au)
    u_prev = (_f32(ac_p) * _f32(au_p)) * keep
    if sample:
        real = (lax.broadcasted_iota(jnp.int32, (rows, 1), 0) % SAMPLE_PAD_T) >= (SAMPLE_PAD_T // 2)
        u = jnp.where(real, u, cst[...])
    ya_o[...] = (_f32(ab) * _causal_conv(u_prev, u, cw, CONV_K)).astype(BF16)
    ut_o[...] = u[rows - ut_o.shape[0]:, :]

    outs = (q_o, k_o, v_o)
    for idx, (cur_ref, prev_ref) in enumerate(((gq, gq_p), (gk, gk_p), (gv, gv_p))):
        cur = _f32(cur_ref)
        if sample:
            cur = jnp.where(real, cur, gst[:, idx * d:(idx + 1) * d])
        prev = _f32(prev_ref) * keep
        wslab = gw.at[:, idx * d:(idx + 1) * d]
        y = _causal_conv(prev, cur, wslab, GDN_CONV_K)
        y = y * jax.nn.sigmoid(y)
        if idx < 2:
            scale = GDN_DK ** -0.5 if idx == 0 else 1.0
            for h in range(GDN_HEADS):
                yh = y[:, h * GDN_DK:(h + 1) * GDN_DK]
                r = lax.rsqrt(jnp.sum(yh * yh, axis=-1, keepdims=True) + EPS)
                outs[idx][:, h * GDN_DK:(h + 1) * GDN_DK] = yh * (r * scale)
        else:
            outs[idx][...] = y

    pa = pab[...]
    lane = lax.broadcasted_iota(jnp.int32, pa.shape, 1)
    z = pa + dtb[...]
    softplus = jnp.maximum(z, 0.0) + jnp.log1p(jnp.exp(-jnp.abs(z)))
    g_log = -jnp.exp(alog[...]) * softplus
    gb = jnp.where(lane < GDN_HEADS, g_log, jnp.where(lane < 2 * GDN_HEADS, jax.nn.sigmoid(pa), 0.0))
    if sample:
        gb = jnp.where(real, gb, 0.0)
    gb_o[...] = gb


def _prep(p, pab, conv_w, gdn_conv_w, a_log, dt_bias, rows, tiles_per_seq, conv_state=None, gdn_state=None):
    n = p.shape[0]
    d = D_MODEL
    sample = conv_state is not None
    nt = n // rows
    prev_rows = 2 * SUBLANES
    rpb = rows // prev_rows

    def cur(c):
        return pl.BlockSpec((rows, d), lambda i, c=c: (i, c))

    def prev(c):
        return pl.BlockSpec((prev_rows, d), lambda i, c=c: (jnp.maximum(i * rpb - 1, 0), c))

    alog128 = jnp.zeros((1, SMALL_COLS), F32).at[0, :GDN_HEADS].set(a_log)
    dtb128 = jnp.zeros((1, SMALL_COLS), F32).at[0, :GDN_HEADS].set(dt_bias)
    in_specs = [cur(COL_AU), cur(COL_AB), cur(COL_AC), cur(COL_GQ), cur(COL_GK), cur(COL_GV),
                pl.BlockSpec((rows, SMALL_COLS), lambda i: (i, 0)),
                prev(COL_AU), prev(COL_AC), prev(COL_GQ), prev(COL_GK), prev(COL_GV),
                pl.BlockSpec((CONV_K, d), lambda i: (0, 0)),
                pl.BlockSpec((GDN_CONV_K, 3 * d), lambda i: (0, 0)),
                pl.BlockSpec((1, SMALL_COLS), lambda i: (0, 0)),
                pl.BlockSpec((1, SMALL_COLS), lambda i: (0, 0))]
    args = [p, p, p, p, p, p, pab, p, p, p, p, p, conv_w, gdn_conv_w, alog128, dtb128]
    if sample:
        in_specs += [pl.BlockSpec((rows, d), lambda i: (i, 0)), pl.BlockSpec((rows, 3 * d), lambda i: (i, 0))]
        args += [conv_state, gdn_state]
    tail = rows if sample else SUBLANES
    out_specs = [pl.BlockSpec((rows, d), lambda i: (i, 0))] * 4 + [
        pl.BlockSpec((rows, SMALL_COLS), lambda i: (i, 0)),
        pl.BlockSpec((tail, d), lambda i: (i, 0))]
    out_shape = [jax.ShapeDtypeStruct((n, d), BF16)] + [jax.ShapeDtypeStruct((n, d), F32)] * 3 + [
        jax.ShapeDtypeStruct((n, SMALL_COLS), F32),
        jax.ShapeDtypeStruct((nt * tail, d), F32)]
    return pl.pallas_call(
        functools.partial(_prep_kernel, rows=rows, tiles_per_seq=tiles_per_seq, sample=sample),
        grid=(nt,),
        in_specs=in_specs,
        out_specs=out_specs,
        out_shape=out_shape,
        compiler_params=_params(("parallel",)),
        name="prep",
    )(*args)


def _chunk_cumsum(g, c):
    row = lax.broadcasted_iota(jnp.int32, g.shape, 0)
    sh = 1
    while sh < c:
        g = g + jnp.where(row >= sh, pltpu.roll(g, sh, axis=0), 0.0)
        sh *= 2
    return g


def _gdn_stacked_kernel(q_ref, k_ref, v_ref, z_ref, gb_ref, ng_ref, yb_ref, sout_ref, s_scr, *, c, grp, nb):
    ci = pl.program_id(1)

    @pl.when(ci == 0)
    def _():
        s_scr[...] = jnp.zeros_like(s_scr)

    r = grp * c
    ng = ng_ref[...]
    seqs = []
    for s in range(nb):
        gb = gb_ref[s]
        gc = _chunk_cumsum(gb, c)
        g_last = gc[c - 1:c, :]
        seqs.append(dict(gb=gb, gc=gc, gct=gc.T, e_g=jnp.exp(gc), e_kd=jnp.exp(g_last - gc), e_last=jnp.exp(g_last)))

    ri = lax.broadcasted_iota(jnp.int32, (r, r), 0)
    cj = lax.broadcasted_iota(jnp.int32, (r, r), 1)
    same = (ri // c) == (cj // c)
    incl = same & (ri >= cj)
    strict = same & (ri > cj)
    head_block = (lax.broadcasted_iota(jnp.int32, (r, grp * GDN_DK), 0) // c
                  == lax.broadcasted_iota(jnp.int32, (r, grp * GDN_DK), 1) // GDN_DK)

    def block_layout(x):
        return jnp.where(head_block, jnp.concatenate([x] * grp, axis=1), 0.0)

    units = []
    for s in range(nb):
        for g0 in range(0, GDN_HEADS, grp):
            heads = list(range(g0, g0 + grp))
            sq = seqs[s]

            def stack(ref, s=s, heads=heads):
                return jnp.concatenate([ref[s, :, h * GDN_DK:(h + 1) * GDN_DK] for h in heads], axis=0)

            def col(a, off=0, heads=heads):
                return jnp.concatenate([a[:, off + h:off + h + 1] for h in heads], axis=0)

            qs, ks, vs = stack(q_ref), stack(k_ref), stack(v_ref)
            beta = col(sq["gb"], GDN_HEADS)
            kb = ks * beta
            units.append(dict(s=s, heads=heads, qs=qs, ks=ks, kb=kb, vb=vs * beta,
                              gcol=col(sq["gc"]), egcol=col(sq["e_g"]), ekdcol=col(sq["e_kd"]),
                              grow=jnp.concatenate([sq["gct"][h:h + 1, :] for h in heads], axis=1),
                              rows=slice((s * GDN_HEADS + g0) * GDN_DK, (s * GDN_HEADS + g0 + grp) * GDN_DK)))

    for u in units:
        u["aq"] = _dot_nt(jnp.concatenate([u["kb"], u["qs"]], axis=0), u["ks"])
    for u in units:
        decay = jnp.exp(jnp.where(incl, u["gcol"] - u["grow"], -jnp.inf))
        u["pw"] = -jnp.where(strict, u["aq"][:r] * decay, 0.0)
        u["attn"] = jnp.where(incl, u["aq"][r:] * decay, 0.0)
        u["x"] = jnp.concatenate([u["vb"], u["kb"] * u["egcol"]], axis=1)
    span = 1
    while span < c:
        span *= 2
        for u in units:
            if span < c:
                both = _dot(u["pw"], jnp.concatenate([u["pw"], u["x"]], axis=1))
                u["pw"] = both[:, :r]
                u["x"] = u["x"] + both[:, r:]
            else:
                u["x"] = u["x"] + _dot(u["pw"], u["x"])
    for u in units:
        u_val, w_k = u["x"][:, :GDN_DV], u["x"][:, GDN_DV:]
        u["s_old"] = s_scr[u["rows"], :]
        ws = _dot(jnp.concatenate([block_layout(w_k), block_layout(u["qs"] * u["egcol"])], axis=0), u["s_old"])
        u["v_new"] = u_val - ws[:r]
        u["o"] = ws[r:]
    for u in units:
        e_last = seqs[u["s"]]["e_last"]
        o = u["o"] + _dot(u["attn"], u["v_new"])
        e_rows = jnp.concatenate([jnp.broadcast_to(e_last[:, h:h + 1], (GDN_DK, 1)) for h in u["heads"]], axis=0)
        kd_t = block_layout(u["ks"] * u["ekdcol"]).T
        s_scr[u["rows"], :] = u["s_old"] * e_rows + _dot(kd_t, u["v_new"])
        for i, h in enumerate(u["heads"]):
            sl = slice(h * GDN_DK, (h + 1) * GDN_DK)
            oh = o[i * c:(i + 1) * c, :]
            zh = z_ref[u["s"], :, sl].astype(F32)
            yb_ref[u["s"], :, sl] = (((oh * _rms_scale(oh)) * ng) * (zh * jax.nn.sigmoid(zh))).astype(BF16)

    @pl.when(ci == pl.num_programs(1) - 1)
    def _():
        sout_ref[...] = s_scr[...]


def _gdn_stacked(q, k, v, p, gb, norm_g, bsz, t, c, nb):
    d = D_MODEL
    nc = t // c
    grp = min(GDN_HEADS, 256 // c)
    s_rows = nb * GDN_HEADS * GDN_DK

    def tok(col=0):
        return pl.BlockSpec((nb, c, d), lambda b, i, col=col: (b, i, col))

    yb, s_new = pl.pallas_call(
        functools.partial(_gdn_stacked_kernel, c=c, grp=grp, nb=nb),
        grid=(bsz // nb, nc),
        in_specs=[tok(), tok(), tok(), tok(COL_GZ),
                  pl.BlockSpec((nb, c, SMALL_COLS), lambda b, i: (b, i, 0)),
                  pl.BlockSpec((1, GDN_DV), lambda b, i: (0, 0))],
        out_specs=[tok(), pl.BlockSpec((s_rows, GDN_DV), lambda b, i: (b, 0))],
        out_shape=[jax.ShapeDtypeStruct((bsz, t, d), BF16),
                   jax.ShapeDtypeStruct((bsz * GDN_HEADS * GDN_DK, GDN_DV), F32)],
        scratch_shapes=[pltpu.VMEM((s_rows, GDN_DV), F32)],
        compiler_params=_params(("parallel", "arbitrary")),
        name="gdn_scan_prompt",
    )(*[a.reshape(bsz, t, -1) for a in (q, k, v, p, gb)], norm_g.reshape(1, GDN_DV))
    return yb.reshape(bsz * t, d), s_new.reshape(bsz, GDN_HEADS, GDN_DK, GDN_DV)


def _gdn_kernel(*refs, c, nb, has_s0):
    if has_s0:
        q_ref, k_ref, v_ref, z_ref, gb_ref, ng_ref, s0_ref, yb_ref, sout_ref, s_scr = refs
    else:
        q_ref, k_ref, v_ref, z_ref, gb_ref, ng_ref, yb_ref, sout_ref, s_scr = refs
    ci = pl.program_id(1)

    @pl.when(ci == 0)
    def _():
        if has_s0:
            s_scr[...] = s0_ref[...]
        else:
            s_scr[...] = jnp.zeros_like(s_scr)

    ri = lax.broadcasted_iota(jnp.int32, (c, c), 0)
    cj = lax.broadcasted_iota(jnp.int32, (c, c), 1)
    incl = ri >= cj
    strict = ri > cj
    ng = ng_ref[...]
    row = lax.broadcasted_iota(jnp.int32, (c, SMALL_COLS), 0)
    chains = [(s, h) for s in range(nb) for h in range(GDN_HEADS)]

    seq = []
    for s in range(nb):
        gb = gb_ref[s]
        gc = gb
        sh = 1
        while sh < c:
            gc = gc + jnp.where(row >= sh, pltpu.roll(gc, sh, axis=0), 0.0)
            sh *= 2
        g_last = gc[c - 1:c, :]
        seq.append(dict(gb=gb, gc=gc, gct=gc.T, e_g=jnp.exp(gc), e_kd=jnp.exp(g_last - gc), e_last=jnp.exp(g_last)))

    st = []
    for s, h in chains:
        sl = slice(h * GDN_DK, (h + 1) * GDN_DK)
        sq = seq[s]
        qh, kh, vh = q_ref[s, :, sl], k_ref[s, :, sl], v_ref[s, :, sl]
        beta = sq["gb"][:, GDN_HEADS + h:GDN_HEADS + h + 1]
        kb = kh * beta
        aq = _dot_nt(jnp.concatenate([kb, qh], axis=0), kh)
        st.append(dict(sl=sl, qh=qh, kh=kh, kb=kb, vb=vh * beta, aq=aq))

    for (s, h), e in zip(chains, st):
        sq = seq[s]
        decay = jnp.exp(jnp.where(incl, sq["gc"][:, h:h + 1] - sq["gct"][h:h + 1, :], -jnp.inf))
        e["neg"] = -jnp.where(strict, e["aq"][:c] * decay, 0.0)
        e["attn"] = jnp.where(incl, e["aq"][c:] * decay, 0.0)
        e["x"] = jnp.concatenate([e["vb"], e["kb"] * sq["e_g"][:, h:h + 1]], axis=1)
        e["pw"] = e["neg"]

    span = 1
    while span < c:
        span *= 2
        for e in st:
            if span < c:
                both = _dot(e["pw"], jnp.concatenate([e["pw"], e["x"]], axis=1))
                e["pw"] = both[:, :c]
                e["x"] = e["x"] + both[:, c:]
            else:
                e["x"] = e["x"] + _dot(e["pw"], e["x"])

    for (s, h), e in zip(chains, st):
        sq = seq[s]
        u_val, w_k = e["x"][:, :GDN_DV], e["x"][:, GDN_DV:]
        q_dec = e["qh"] * sq["e_g"][:, h:h + 1]
        e["s_old"] = s_scr[s, h]
        ws = _dot(jnp.concatenate([w_k, q_dec], axis=0), e["s_old"])
        e["v_new"] = u_val - ws[:c]
        e["o_inter"] = ws[c:]

    for (s, h), e in zip(chains, st):
        sq = seq[s]
        o = e["o_inter"] + _dot(e["attn"], e["v_new"])
        k_dec = e["kh"] * sq["e_kd"][:, h:h + 1]
        s_scr[s, h] = e["s_old"] * sq["e_last"][:, h:h + 1] + _dot(k_dec.T, e["v_new"])
        zh = z_ref[s, :, e["sl"]].astype(F32)
        yb_ref[s, :, e["sl"]] = (((o * _rms_scale(o)) * ng) * (zh * jax.nn.sigmoid(zh))).astype(BF16)

    @pl.when(ci == pl.num_programs(1) - 1)
    def _():
        sout_ref[...] = s_scr[...]


def _gdn(q, k, v, p, gb, norm_g, bsz, t, c, nb, s0=None):
    d = D_MODEL
    nc = t // c
    has_s0 = s0 is not None

    def tok(col=0):
        return pl.BlockSpec((nb, c, d), lambda b, i, col=col: (b, i, col))

    in_specs = [tok(), tok(), tok(), tok(COL_GZ),
                pl.BlockSpec((nb, c, SMALL_COLS), lambda b, i: (b, i, 0)),
                pl.BlockSpec((1, GDN_DV), lambda b, i: (0, 0))]
    args = [a.reshape(bsz, t, -1) for a in (q, k, v, p, gb)] + [norm_g.reshape(1, GDN_DV)]
    s_spec = pl.BlockSpec((nb, GDN_HEADS, GDN_DK, GDN_DV), lambda b, i: (b, 0, 0, 0))
    if has_s0:
        in_specs.append(s_spec)
        args.append(s0)
    yb, s_new = pl.pallas_call(
        functools.partial(_gdn_kernel, c=c, nb=nb, has_s0=has_s0),
        grid=(bsz // nb, nc),
        in_specs=in_specs,
        out_specs=[tok(), s_spec],
        out_shape=[jax.ShapeDtypeStruct((bsz, t, d), BF16),
                   jax.ShapeDtypeStruct((bsz, GDN_HEADS, GDN_DK, GDN_DV), F32)],
        scratch_shapes=[pltpu.VMEM((nb, GDN_HEADS, GDN_DK, GDN_DV), F32)],
        compiler_params=_params(("parallel", "arbitrary")),
        name="gdn_scan",
    )(*args)
    return yb.reshape(bsz * t, d), s_new


def _mem_attn_kernel(q_ref, k_ref, v_ref, o_ref, *, nb, tq):
    scale = MEM_HEAD_DIM ** -0.5
    pairs = [(s, h) for s in range(nb) for h in range(MEM_HEADS)]
    span = lambda s, h: (slice(s * tq, (s + 1) * tq), slice(h * MEM_HEAD_DIM, (h + 1) * MEM_HEAD_DIM))
    scores = [_dot_nt(q_ref[span(s, h)], k_ref[s, :, span(s, h)[1]]) * scale for s, h in pairs]
    probs = []
    for sc in scores:
        e = jnp.exp(sc - jnp.max(sc, axis=-1, keepdims=True))
        probs.append(e / jnp.sum(e, axis=-1, keepdims=True))
    for (s, h), prob in zip(pairs, probs):
        o_ref[span(s, h)] = _dot(prob, v_ref[s, :, span(s, h)[1]]).astype(BF16)


def _mem_attn(p, mk, mv, bsz, t, tq, nb):
    d = D_MODEL
    m = mk.shape[1]
    nq = t // tq
    assert nb == 1 or nq == 1
    kv_spec = pl.BlockSpec((nb, m, d), lambda b, i: (b, 0, 0))
    return pl.pallas_call(
        functools.partial(_mem_attn_kernel, nb=nb, tq=tq),
        grid=(bsz // nb, nq),
        in_specs=[pl.BlockSpec((nb * tq, d), lambda b, i: (b * nq + i, COL_MQ)), kv_spec, kv_spec],
        out_specs=pl.BlockSpec((nb * tq, d), lambda b, i: (b * nq + i, 0)),
        out_shape=jax.ShapeDtypeStruct((bsz * t, d), BF16),
        compiler_params=_params(("parallel", "arbitrary")),
        name="mem_attn",
    )(p, mk, mv)


def _merge_kernel(x_ref, ya_ref, yb_ref, ym_ref, g0_ref, g1_ref, g2_ref, wc_ref, wg_ref, wm_ref, wo_ref,
                  ng_ref, x1_ref, h2_ref):
    d = functools.partial(jnp.dot, preferred_element_type=F32)
    merged = (jax.nn.sigmoid(_f32(g0_ref)) * d(ya_ref[...], wc_ref[...])
              + jax.nn.sigmoid(_f32(g1_ref)) * d(yb_ref[...], wg_ref[...])
              + jax.nn.sigmoid(_f32(g2_ref)) * d(ym_ref[...], wm_ref[...]))
    x1 = x_ref[...] + d(merged.astype(BF16), wo_ref[...])
    x1_ref[...] = x1
    h2_ref[...] = ((x1 * _rms_scale(x1)) * ng_ref[...]).astype(BF16)


def _merge(x, ya, yb, ym, p, wc, wg, wm, wo, norm_g, tm):
    n, d = x.shape
    tok = pl.BlockSpec((tm, d), lambda i: (i, 0))
    wsp = pl.BlockSpec((d, d), lambda i: (0, 0))

    def gate(j):
        return pl.BlockSpec((tm, d), lambda i, j=j: (i, COL_BG + j))

    return pl.pallas_call(
        _merge_kernel,
        grid=(n // tm,),
        in_specs=[tok, tok, tok, tok, gate(0), gate(1), gate(2), wsp, wsp, wsp, wsp,
                  pl.BlockSpec((1, d), lambda i: (0, 0))],
        out_specs=[tok, tok],
        out_shape=[jax.ShapeDtypeStruct((n, d), F32), jax.ShapeDtypeStruct((n, d), BF16)],
        compiler_params=_params(("parallel",)),
        name="merge",
    )(x, ya, yb, ym, p, p, p, wc, wg, wm, wo, norm_g.reshape(1, d))


_CAND_CELLS = tuple((a, b) for a in range(PEER_TOPK) for b in range(PEER_TOPK) if (a + 1) * (b + 1) <= PEER_TOPK)

TILE_PAD = 4 * SUBLANES
SHIFT_S1, SHIFT_E1 = 0, SUBLANES


def _put_tile(ref, h, ns, shift, val, pad=True):
    if pad and shift:
        ref[h, ns, :shift, :] = jnp.zeros((shift, LANES), F32)
    ref[h, ns, shift:shift + PEER_N_KEYS, :] = val
    if pad and shift < TILE_PAD:
        ref[h, ns, shift + PEER_N_KEYS:, :] = jnp.zeros((TILE_PAD - shift, LANES), F32)


def _top_values(x, count):
    vals = []
    cur = x
    for r in range(count):
        m = jnp.max(cur, axis=0, keepdims=True)
        vals.append(m)
        if r + 1 < count:
            cur = jnp.where(cur == m, -jnp.inf, cur)
    return vals


def _top_ranked(x, count):
    n_rows = x.shape[0]
    iota = lax.broadcasted_iota(jnp.int32, x.shape, 0).astype(F32)
    vals = []
    cur = x
    rank = jnp.full(x.shape, float(count), F32)
    for r in range(count):
        m = jnp.max(cur, axis=0, keepdims=True)
        first = jnp.min(jnp.where(cur == m, iota, float(n_rows)), axis=0, keepdims=True)
        one = iota == first
        rank = jnp.where(one, float(r), rank)
        cur = jnp.where(one, -jnp.inf, cur)
        vals.append(m)
    return vals, rank


def _count_rows(mask):
    return jnp.sum(jnp.where(mask, 1.0, 0.0), axis=0, keepdims=True)


def _route_kernel(h_ref, wq_ref, keys_ref, cut_ref, e0_ref, s1_ref, e1_ref, q_scr, tie_ref):
    tn = h_ref.shape[0]
    nsb = tn // LANES
    q = jnp.dot(h_ref[...], wq_ref[...], preferred_element_type=F32).astype(BF16)

    def store(h, cut, e0, s1, e1):
        for ns in range(nsb):
            cols = slice(ns * LANES, (ns + 1) * LANES)
            cut_ref[h, ns] = cut[:, cols]
            e0_ref[h, ns] = e0[:, cols]
            _put_tile(s1_ref, h, ns, SHIFT_S1, s1[:, cols])
            _put_tile(e1_ref, h, ns, SHIFT_E1, e1[:, cols])

    for h in range(PEER_HEADS):
        sc, top = [], []
        tied = jnp.zeros((1, tn), F32)
        for part in range(2):
            col = (2 * h + part) * PEER_D_HALF
            q_scr[2 * h + part] = q[:, col:col + PEER_D_HALF]
            s_t = lax.dot_general(keys_ref[part, h], q[:, col:col + PEER_D_HALF],
                                  (((1,), (1,)), ((), ())), preferred_element_type=F32)
            sc.append(s_t)
            top.append(_top_values(s_t, PEER_TOPK))
            tied = jnp.maximum(tied, jnp.abs(_count_rows(s_t >= top[part][-1]) - PEER_TOPK))
        cand = jnp.concatenate([top[0][a] + top[1][b] for a, b in _CAND_CELLS], axis=0)
        best = _top_values(cand, PEER_TOPK)
        c_max, thr = best[0], best[-1]
        z = jnp.exp(best[0] - c_max)
        for r in range(1, PEER_TOPK):
            z = z + jnp.exp(best[r] - c_max)
        top1 = jnp.concatenate(top[1], axis=0)
        n_sel = sum(_count_rows(top[0][a] + top1 >= thr) for a in range(PEER_TOPK))
        tied = jnp.maximum(tied, jnp.abs(n_sel - PEER_TOPK))
        cut = jnp.full(sc[0].shape, jnp.inf, F32)
        for b in range(PEER_TOPK):
            cut = jnp.where(sc[0] + top[1][b] >= thr, top[1][b], cut)
        store(h, cut, jnp.exp(sc[0] - top[0][0]) / z, sc[1], jnp.exp(sc[1] - top[1][0]))
        for ns in range(nsb):
            tie_ref[h * nsb + ns] = jnp.max(tied[:, ns * LANES:(ns + 1) * LANES])

    def exact_block(idx, carry):
        @pl.when(tie_ref[idx] > 0.0)
        def _():
            h = idx // nsb
            ns = idx % nsb
            rows = pl.ds(pl.multiple_of(ns * LANES, LANES), LANES)
            sc, top, rank = [], [], []
            for part in range(2):
                s_t = lax.dot_general(keys_ref[part, h], q_scr[2 * h + part, rows, :],
                                      (((1,), (1,)), ((), ())), preferred_element_type=F32)
                vals, rk = _top_ranked(s_t, PEER_TOPK)
                sc.append(s_t)
                top.append(vals)
                rank.append(rk)
            top1 = jnp.concatenate(top[1], axis=0)
            cand = jnp.concatenate([top[0][a] + top1 for a in range(PEER_TOPK)], axis=0)
            best, pos = _top_ranked(cand, PEER_TOPK)
            z = jnp.exp(best[0] - best[0])
            for r in range(1, PEER_TOPK):
                z = z + jnp.exp(best[r] - best[0])
            cnt = jnp.zeros_like(sc[0])
            for a in range(PEER_TOPK):
                cnt_a = _count_rows(pos[a * PEER_TOPK:(a + 1) * PEER_TOPK] < PEER_TOPK)
                cnt = jnp.where(rank[0] == float(a), cnt_a, cnt)
            cut_ref[h, ns] = jnp.where(cnt > 0.0, 1.0 - cnt, jnp.inf)
            e0_ref[h, ns] = jnp.exp(sc[0] - top[0][0]) / z
            _put_tile(s1_ref, h, ns, SHIFT_S1, -rank[1], pad=False)
            _put_tile(e1_ref, h, ns, SHIFT_E1, jnp.exp(sc[1] - top[1][0]), pad=False)
        return carry

    lax.fori_loop(0, PEER_HEADS * nsb, exact_block, 0)


def _route(h2, wq, keys, tn):
    n, d = h2.shape
    nk = PEER_N_KEYS
    tiled = pl.BlockSpec((PEER_HEADS, tn // LANES, nk + TILE_PAD, LANES), lambda i: (0, i, 0, 0))
    tiled_shape = jax.ShapeDtypeStruct((PEER_HEADS, n // LANES, nk + TILE_PAD, LANES), F32)
    rowed = pl.BlockSpec((PEER_HEADS, tn // LANES, nk, LANES), lambda i: (0, i, 0, 0))
    rowed_shape = jax.ShapeDtypeStruct((PEER_HEADS, n // LANES, nk, LANES), F32)
    return pl.pallas_call(
        _route_kernel,
        grid=(n // tn,),
        in_specs=[pl.BlockSpec((tn, d), lambda i: (i, 0)),
                  pl.BlockSpec(wq.shape, lambda i: (0, 0)),
                  pl.BlockSpec(keys.shape, lambda i: (0, 0, 0, 0))],
        out_specs=[rowed, rowed, tiled, tiled],
        out_shape=[rowed_shape, rowed_shape, tiled_shape, tiled_shape],
        scratch_shapes=[pltpu.VMEM((2 * PEER_HEADS, tn, PEER_D_HALF), BF16),
                        pltpu.SMEM((PEER_HEADS * (tn // LANES),), F32)],
        compiler_params=_params(("parallel",)),
        name="peer_route",
    )(h2, wq, keys)


def _gelu(x):
    return 0.5 * x * (1.0 + lax.erf(x * (2.0 ** -0.5)))


PEER_TILE = 2048
GATE_HALF = 32
GATE_ROWS = 8


def _peer_kernel(h_ref, x1_ref, u_ref, vt_ref, cut_ref, e0_ref, s1_ref, e1_ref, ng_ref,
                 y_ref, acc_ref, act_ref, coef_ref, *, te):
    ei = pl.program_id(1)
    tn = h_ref.shape[0]
    groups = te // PEER_N_KEYS

    @pl.when(ei == 0)
    def _():
        acc_ref[...] = jnp.zeros_like(acc_ref)

    nsb = tn // LANES
    act = _gelu(lax.dot_general(u_ref[...], h_ref[...], (((1,), (1,)), ((), ())),
                                preferred_element_type=F32))
    for ns in range(nsb):
        act_ref[ns] = act[:, ns * LANES:(ns + 1) * LANES]

    def gate_block(ns):
        for jh in range(PEER_N_KEYS // GATE_HALF):
            jrows = slice(jh * GATE_HALF, (jh + 1) * GATE_HALF)
            for g0 in range(0, groups, GATE_ROWS):
                gates = [None] * GATE_ROWS
                for h in range(PEER_HEADS):
                    s1_t = s1_ref[h, ns, SHIFT_S1 + jh * GATE_HALF:SHIFT_S1 + (jh + 1) * GATE_HALF, :]
                    e1_t = e1_ref[h, ns, SHIFT_E1 + jh * GATE_HALF:SHIFT_E1 + (jh + 1) * GATE_HALF, :]
                    for k in range(GATE_ROWS):
                        sel = s1_t >= cut_ref[h, ns, g0 + k:g0 + k + 1, :]
                        term = jnp.where(sel, e1_t, 0.0) * e0_ref[h, ns, g0 + k:g0 + k + 1, :]
                        gates[k] = term if gates[k] is None else gates[k] + term
                for k in range(GATE_ROWS):
                    r0 = (g0 + k) * PEER_N_KEYS + jh * GATE_HALF
                    coef_ref[ns, r0:r0 + GATE_HALF, :] = (gates[k] * act_ref[ns, r0:r0 + GATE_HALF, :]).astype(BF16)

    def lane_block(ns, carry):
        gate_block(ns)
        return carry

    def mix_values(lo, hi):
        coef = jnp.concatenate([coef_ref[ns] for ns in range(lo, hi)], axis=1)
        acc_ref[:, lo * LANES:hi * LANES] += jnp.dot(vt_ref[0], coef, preferred_element_type=F32)

    half = nsb // 2
    lax.fori_loop(0, half, lane_block, 0)
    if half:
        mix_values(0, half)
    for ns in range(half, nsb):
        gate_block(ns)
    mix_values(half, nsb)

    @pl.when(ei == pl.num_programs(1) - 1)
    def _():
        x2 = x1_ref[...] + acc_ref[...].T
        y_ref[...] = (x2 * _rms_scale(x2)) * ng_ref[...]


def _peer(h2, x1, u_bf, vt_bf, cut, e0, s1, e1, norm_g, tn):
    n, d = h2.shape
    ne = u_bf.shape[0]
    te = vt_bf.shape[2]
    nk = PEER_N_KEYS
    tok = pl.BlockSpec((tn, d), lambda i, e: (i, 0))
    tiled = pl.BlockSpec((PEER_HEADS, tn // LANES, nk + TILE_PAD, LANES), lambda i, e: (0, i, 0, 0))
    rowed = pl.BlockSpec((PEER_HEADS, tn // LANES, te // nk, LANES), lambda i, e: (0, i, e, 0))
    return pl.pallas_call(
        functools.partial(_peer_kernel, te=te),
        grid=(n // tn, ne // te),
        in_specs=[tok, tok,
                  pl.BlockSpec((te, d), lambda i, e: (e, 0)),
                  pl.BlockSpec((1, d, te), lambda i, e: (e, 0, 0)),
                  rowed, rowed, tiled, tiled,
                  pl.BlockSpec((1, d), lambda i, e: (0, 0))],
        out_specs=tok,
        out_shape=jax.ShapeDtypeStruct((n, d), F32),
        scratch_shapes=[pltpu.VMEM((d, tn), F32), pltpu.VMEM((tn // LANES, te, LANES), F32),
                        pltpu.VMEM((tn // LANES, te, LANES), BF16)],
        compiler_params=_params(("parallel", "arbitrary")),
        name="peer_mix",
    )(h2, x1, u_bf, vt_bf, cut, e0, s1, e1, norm_g.reshape(1, d))


def _tile(n, pref):
    return math.gcd(n, pref)


def _layer(x, mk, mv, bsz, t, weights, conv_state=None, gdn_conv_state=None, gdn_s0=None):
    (norm_mix_g, w_main, w_ab, conv_w, gdn_conv_w, a_log, dt_bias, gdn_norm_g, wc, wg, wm, wo, norm_ffn_g,
     wq, keys, u_bf, vt_bf, final_norm_g) = weights
    n = bsz * t
    sample = conv_state is not None
    p, pab = _inproj(x, norm_mix_g, w_main, w_ab, _tile(n, 2048))
    rows = _tile(n, 256) if sample else _tile(t, 512)
    ya, q, k, v, gb, u_tail = _prep(p, pab, conv_w, gdn_conv_w, a_log, dt_bias, rows, max(t // rows, 1),
                                    conv_state, gdn_conv_state)
    chunk = _tile(t, 64)
    if sample:
        yb, s_new = _gdn(q, k, v, p, gb, gdn_norm_g, bsz, t, chunk, 4, gdn_s0)
    else:
        yb, s_new = _gdn_stacked(q, k, v, p, gb, gdn_norm_g, bsz, t, chunk, _tile(bsz, 4))
    ym = _mem_attn(p, mk, mv, bsz, t, _tile(t, 512), _tile(bsz, 4) if sample else 1)
    x1, h2 = _merge(x, ya, yb, ym, p, wc, wg, wm, wo, norm_ffn_g, _tile(n, 512))
    if sample:
        keep = lambda a: a.reshape(bsz, t, -1)[:, t // 2:, :].reshape(bsz * t // 2, -1)
        x1, h2 = keep(x1), keep(h2)
    tn = _tile(x1.shape[0], 512)
    cut, e0, s1, e1 = _route(h2, wq, keys, tn)
    y = _peer(h2, x1, u_bf, vt_bf, cut, e0, s1, e1, final_norm_g, tn)
    return y, u_tail, p, s_new


def kernel(x_prompt, x_sample, mem_prompt, cache_mem_k, cache_mem_v, state_conv, state_gdn_conv, state_gdn, norm_mix_g, w_in, conv_w, gdn_conv_w, gdn_a_log, gdn_dt_bias, gdn_norm_g, mem_norm_g, w_mem_kv, w_br_conv, w_br_gdn, w_br_mem, w_out, norm_ffn_g, peer_w_q, peer_sub_keys, peer_u, peer_v, final_norm_g):
    depth = w_in.shape[0]
    assert depth == 1, "single-layer step"
    d = D_MODEL
    bp, tp, _ = x_prompt.shape
    bs, ts, _ = x_sample.shape
    assert ts == SAMPLE_PAD_T // 2
    m_tok = mem_prompt.shape[1]
    mq = MEM_HEADS * MEM_HEAD_DIM
    l = 0

    w = w_in[l]
    ga0 = 7 * d
    w_main = jnp.concatenate([w[:, :ga0], w[:, ga0 + 2 * GDN_HEADS:]], axis=1).astype(BF16)
    w_ab = jnp.pad(w[:, ga0:ga0 + 2 * GDN_HEADS], ((0, 0), (0, SMALL_COLS - 2 * GDN_HEADS))).astype(BF16)
    weights = (norm_mix_g[l], w_main, w_ab, conv_w[l], gdn_conv_w[l], gdn_a_log[l], gdn_dt_bias[l], gdn_norm_g[l],
               w_br_conv[l].astype(BF16), w_br_gdn[l].astype(BF16), w_br_mem[l].astype(BF16), w_out[l].astype(BF16),
               norm_ffn_g[l], peer_w_q[l].astype(BF16), peer_sub_keys[l].astype(BF16),
               peer_u[l].astype(BF16),
               peer_v[l].astype(BF16).reshape(-1, PEER_TILE, d).transpose(0, 2, 1), final_norm_g)

    kv_shape = (bp, m_tok, mq)
    mk_p, mv_p = _mem_kv(mem_prompt.reshape(bp * m_tok, d), mem_norm_g[l], w_mem_kv[l].astype(BF16),
                         _tile(bp * m_tok, 256))
    y_p, u_tail_p, p_p, s_p = _layer(x_prompt.reshape(bp * tp, d), mk_p.reshape(kv_shape), mv_p.reshape(kv_shape),
                                     bp, tp, weights)
    rows_p = _tile(tp, 512)
    cb_p = u_tail_p.reshape(bp, tp // rows_p, SUBLANES, d)[:, -1, SUBLANES - (CONV_K - 1):, :]
    gb_p = p_p.reshape(bp, tp, -1)[:, tp - (GDN_CONV_K - 1):, COL_GQ * d:(COL_GV + 1) * d].astype(F32)

    pad_t = SAMPLE_PAD_T
    lead = pad_t - ts
    x_s = jnp.pad(x_sample, ((0, 0), (lead, 0), (0, 0))).reshape(bs * pad_t, d)
    cst = jnp.pad(state_conv[l], ((0, 0), (lead - (CONV_K - 1), ts), (0, 0))).reshape(bs * pad_t, d)
    gst = jnp.pad(state_gdn_conv[l], ((0, 0), (lead - (GDN_CONV_K - 1), ts), (0, 0))).reshape(bs * pad_t, 3 * d)
    y_s, u_s, p_s, s_s = _layer(x_s, cache_mem_k.reshape(bs, -1, mq), cache_mem_v.reshape(bs, -1, mq),
                                bs, pad_t, weights, cst, gst, state_gdn.reshape(bs, GDN_HEADS, GDN_DK, GDN_DV))
    y_s = y_s.reshape(bs, ts, d)
    cb_s = u_s.reshape(bs, pad_t, d)[:, pad_t - (CONV_K - 1):, :]
    gb_s = p_s.reshape(bs, pad_t, -1)[:, pad_t - (GDN_CONV_K - 1):, COL_GQ * d:(COL_GV + 1) * d].astype(F32)

    mshape = (depth, bp, m_tok, MEM_HEADS, MEM_HEAD_DIM)
    return (y_p.reshape(bp, tp, d), y_s,
            mk_p.reshape(mshape), mv_p.reshape(mshape),
            cb_p[None], gb_p[None], s_p[None],
            cb_s[None], gb_s[None], s_s[None])
```

```python
import functools
import math

import jax
import jax.numpy as jnp
from jax import lax
from jax.experimental import pallas as pl
from jax.experimental.pallas import tpu as pltpu

F32 = jnp.float32
BF16 = jnp.bfloat16

EPS = 1e-6
D_MODEL = 1024
CONV_K = 3
GDN_HEADS = 8
GDN_DK = 128
GDN_DV = 128
GDN_CONV_K = 4
MEM_HEADS = 4
MEM_HEAD_DIM = 256
PEER_HEADS = 8
PEER_N_KEYS = 128
PEER_D_HALF = 128
PEER_TOPK = 16
N_BRANCHES = 3

COL_AU, COL_AB, COL_AC, COL_GQ, COL_GK, COL_GV, COL_GZ, COL_MQ, COL_BG = 0, 1, 2, 3, 4, 5, 6, 7, 8
N_MAIN_BLOCKS = 11
SMALL_COLS = 128
SUBLANES = 8
LANES = 128
SAMPLE_PAD_T = 8
VMEM_LIMIT = 56 * 1024 * 1024


def _params(sem):
    return pltpu.CompilerParams(dimension_semantics=sem, vmem_limit_bytes=VMEM_LIMIT)


def _dot(a, b):
    return jnp.dot(a.astype(BF16), b.astype(BF16), preferred_element_type=F32)


def _dot_nt(a, b):
    return lax.dot_general(a.astype(BF16), b.astype(BF16), (((1,), (1,)), ((), ())),
                           preferred_element_type=F32)


def _rms_scale(x):
    return lax.rsqrt(jnp.mean(x * x, axis=-1, keepdims=True) + EPS)


def _inproj_kernel(x_ref, g_ref, w_ref, wab_ref, p_ref, pab_ref, h_ref):
    @pl.when(pl.program_id(1) == 0)
    def _():
        x = x_ref[...]
        h = ((x * _rms_scale(x)) * g_ref[...]).astype(BF16)
        h_ref[...] = h
        pab_ref[...] = jnp.dot(h, wab_ref[...], preferred_element_type=F32)

    p_ref[...] = jnp.dot(h_ref[...], w_ref[...], preferred_element_type=F32).astype(p_ref.dtype)


def _inproj(x, g, w_main, w_ab, tm):
    n, d = x.shape
    nblk = w_main.shape[1] // d
    return pl.pallas_call(
        _inproj_kernel,
        grid=(n // tm, nblk),
        in_specs=[
            pl.BlockSpec((tm, d), lambda i, j: (i, 0)),
            pl.BlockSpec((1, d), lambda i, j: (0, 0)),
            pl.BlockSpec((d, d), lambda i, j: (0, j)),
            pl.BlockSpec((d, SMALL_COLS), lambda i, j: (0, 0)),
        ],
        out_specs=[
            pl.BlockSpec((tm, d), lambda i, j: (i, j)),
            pl.BlockSpec((tm, SMALL_COLS), lambda i, j: (i, 0)),
        ],
        out_shape=[
            jax.ShapeDtypeStruct((n, nblk * d), BF16),
            jax.ShapeDtypeStruct((n, SMALL_COLS), F32),
        ],
        scratch_shapes=[pltpu.VMEM((tm, d), BF16)],
        compiler_params=_params(("parallel", "arbitrary")),
        name="inproj",
    )(x, g.reshape(1, d), w_main, w_ab)


def _mem_kv_kernel(x_ref, g_ref, w_ref, k_ref, v_ref):
    x = x_ref[...]
    h = ((x * _rms_scale(x)) * g_ref[...]).astype(BF16)
    kv = jnp.dot(h, w_ref[...], preferred_element_type=F32)
    half = kv.shape[1] // 2
    k_ref[...] = kv[:, :half]
    v_ref[...] = kv[:, half:]


def _mem_kv(x, g, w, tm):
    n, d = x.shape
    c = w.shape[1]
    out = jax.ShapeDtypeStruct((n, c // 2), F32)
    ospec = pl.BlockSpec((tm, c // 2), lambda i: (i, 0))
    return pl.pallas_call(
        _mem_kv_kernel,
        grid=(n // tm,),
        in_specs=[
            pl.BlockSpec((tm, d), lambda i: (i, 0)),
            pl.BlockSpec((1, d), lambda i: (0, 0)),
            pl.BlockSpec((d, c), lambda i: (0, 0)),
        ],
        out_specs=[ospec, ospec],
        out_shape=[out, out],
        compiler_params=_params(("parallel",)),
        name="mem_kv",
    )(x, g.reshape(1, d), w)


def _causal_conv(prev, cur, w_ref, k):
    full = jnp.concatenate([prev, cur], axis=0)
    y = cur * w_ref[k - 1:k, :]
    for s in range(1, k):
        y = y + pltpu.roll(full, s, axis=0)[prev.shape[0]:, :] * w_ref[k - 1 - s:k - s, :]
    return y


def _f32(ref):
    return ref[...].astype(F32)


def _prep_kernel(*refs, rows, tiles_per_seq, sample):
    if sample:
        (au, ab, ac, gq, gk, gv, pab, au_p, ac_p, gq_p, gk_p, gv_p, cw, gw, alog, dtb, cst, gst,
         ya_o, q_o, k_o, v_o, gb_o, ut_o) = refs
    else:
        (au, ab, ac, gq, gk, gv, pab, au_p, ac_p, gq_p, gk_p, gv_p, cw, gw, alog, dtb,
         ya_o, q_o, k_o, v_o, gb_o, ut_o) = refs
    d = D_MODEL
    first = (pl.program_id(0) % tiles_per_seq) == 0
    keep = jnp.where(first, 0.0, 1.0).astype(F32)

    u = _f32(ac) * _f32(au)
    u_prev = (_f32(ac_p) * _f32(au_p)) * keep
    if sample:
        real = (lax.broadcasted_iota(jnp.int32, (rows, 1), 0) % SAMPLE_PAD_T) >= (SAMPLE_PAD_T // 2)
        u = jnp.where(real, u, cst[...])
    ya_o[...] = (_f32(ab) * _causal_conv(u_prev, u, cw, CONV_K)).astype(BF16)
    ut_o[...] = u[rows - ut_o.shape[0]:, :]

    outs = (q_o, k_o, v_o)
    for idx, (cur_ref, prev_ref) in enumerate(((gq, gq_p), (gk, gk_p), (gv, gv_p))):
        cur = _f32(cur_ref)
        if sample:
            cur = jnp.where(real, cur, gst[:, idx * d:(idx + 1) * d])
        prev = _f32(prev_ref) * keep
        wslab = gw.at[:, idx * d:(idx + 1) * d]
        y = _causal_conv(prev, cur, wslab, GDN_CONV_K)
        y = y * jax.nn.sigmoid(y)
        if idx < 2:
            scale = GDN_DK ** -0.5 if idx == 0 else 1.0
            for h in range(GDN_HEADS):
                yh = y[:, h * GDN_DK:(h + 1) * GDN_DK]
                r = lax.rsqrt(jnp.sum(yh * yh, axis=-1, keepdims=True) + EPS)
                outs[idx][:, h * GDN_DK:(h + 1) * GDN_DK] = yh * (r * scale)
        else:
            outs[idx][...] = y

    pa = pab[...]
    lane = lax.broadcasted_iota(jnp.int32, pa.shape, 1)
    z = pa + dtb[...]
    softplus = jnp.maximum(z, 0.0) + jnp.log1p(jnp.exp(-jnp.abs(z)))
    g_log = -jnp.exp(alog[...]) * softplus
    gb = jnp.where(lane < GDN_HEADS, g_log, jnp.where(lane < 2 * GDN_HEADS, jax.nn.sigmoid(pa), 0.0))
    if sample:
        gb = jnp.where(real, gb, 0.0)
    gb_o[...] = gb


def _prep(p, pab, conv_w, gdn_conv_w, a_log, dt_bias, rows, tiles_per_seq, conv_state=None, gdn_state=None):
    n = p.shape[0]
    d = D_MODEL
    sample = conv_state is not None
    nt = n // rows
    prev_rows = 2 * SUBLANES
    rpb = rows // prev_rows

    def cur(c):
        return pl.BlockSpec((rows, d), lambda i, c=c: (i, c))

    def prev(c):
        return pl.BlockSpec((prev_rows, d), lambda i, c=c: (jnp.maximum(i * rpb - 1, 0), c))

    alog128 = jnp.zeros((1, SMALL_COLS), F32).at[0, :GDN_HEADS].set(a_log)
    dtb128 = jnp.zeros((1, SMALL_COLS), F32).at[0, :GDN_HEADS].set(dt_bias)
    in_specs = [cur(COL_AU), cur(COL_AB), cur(COL_AC), cur(COL_GQ), cur(COL_GK), cur(COL_GV),
                pl.BlockSpec((rows, SMALL_COLS), lambda i: (i, 0)),
                prev(COL_AU), prev(COL_AC), prev(COL_GQ), prev(COL_GK), prev(COL_GV),
                pl.BlockSpec((CONV_K, d), lambda i: (0, 0)),
                pl.BlockSpec((GDN_CONV_K, 3 * d), lambda i: (0, 0)),
                pl.BlockSpec((1, SMALL_COLS), lambda i: (0, 0)),
                pl.BlockSpec((1, SMALL_COLS), lambda i: (0, 0))]
    args = [p, p, p, p, p, p, pab, p, p, p, p, p, conv_w, gdn_conv_w, alog128, dtb128]
    if sample:
        in_specs += [pl.BlockSpec((rows, d), lambda i: (i, 0)), pl.BlockSpec((rows, 3 * d), lambda i: (i, 0))]
        args += [conv_state, gdn_state]
    tail = rows if sample else SUBLANES
    out_specs = [pl.BlockSpec((rows, d), lambda i: (i, 0))] * 4 + [
        pl.BlockSpec((rows, SMALL_COLS), lambda i: (i, 0)),
        pl.BlockSpec((tail, d), lambda i: (i, 0))]
    out_shape = [jax.ShapeDtypeStruct((n, d), BF16)] + [jax.ShapeDtypeStruct((n, d), F32)] * 3 + [
        jax.ShapeDtypeStruct((n, SMALL_COLS), F32),
        jax.ShapeDtypeStruct((nt * tail, d), F32)]
    return pl.pallas_call(
        functools.partial(_prep_kernel, rows=rows, tiles_per_seq=tiles_per_seq, sample=sample),
        grid=(nt,),
        in_specs=in_specs,
        out_specs=out_specs,
        out_shape=out_shape,
        compiler_params=_params(("parallel",)),
        name="prep",
    )(*args)


def _chunk_cumsum(g, c):
    row = lax.broadcasted_iota(jnp.int32, g.shape, 0)
    sh = 1
    while sh < c:
        g = g + jnp.where(row >= sh, pltpu.roll(g, sh, axis=0), 0.0)
        sh *= 2
    return g


def _gdn_stacked_kernel(q_ref, k_ref, v_ref, z_ref, gb_ref, ng_ref, yb_ref, sout_ref, s_scr, *, c, grp, nb):
    ci = pl.program_id(1)

    @pl.when(ci == 0)
    def _():
        s_scr[...] = jnp.zeros_like(s_scr)

    r = grp * c
    ng = ng_ref[...]
    seqs = []
    for s in range(nb):
        gb = gb_ref[s]
        gc = _chunk_cumsum(gb, c)
        g_last = gc[c - 1:c, :]
        seqs.append(dict(gb=gb, gc=gc, gct=gc.T, e_g=jnp.exp(gc), e_kd=jnp.exp(g_last - gc), e_last=jnp.exp(g_last)))

    ri = lax.broadcasted_iota(jnp.int32, (r, r), 0)
    cj = lax.broadcasted_iota(jnp.int32, (r, r), 1)
    same = (ri // c) == (cj // c)
    incl = same & (ri >= cj)
    strict = same & (ri > cj)
    head_block = (lax.broadcasted_iota(jnp.int32, (r, grp * GDN_DK), 0) // c
                  == lax.broadcasted_iota(jnp.int32, (r, grp * GDN_DK), 1) // GDN_DK)

    def block_layout(x):
        return jnp.where(head_block, jnp.concatenate([x] * grp, axis=1), 0.0)

    units = []
    for s in range(nb):
        for g0 in range(0, GDN_HEADS, grp):
            heads = list(range(g0, g0 + grp))
            sq = seqs[s]

            def stack(ref, s=s, heads=heads):
                return jnp.concatenate([ref[s, :, h * GDN_DK:(h + 1) * GDN_DK] for h in heads], axis=0)

            def col(a, off=0, heads=heads):
                return jnp.concatenate([a[:, off + h:off + h + 1] for h in heads], axis=0)

            qs, ks, vs = stack(q_ref), stack(k_ref), stack(v_ref)
            beta = col(sq["gb"], GDN_HEADS)
            kb = ks * beta
            units.append(dict(s=s, heads=heads, qs=qs, ks=ks, kb=kb, vb=vs * beta,
                              gcol=col(sq["gc"]), egcol=col(sq["e_g"]), ekdcol=col(sq["e_kd"]),
                              grow=jnp.concatenate([sq["gct"][h:h + 1, :] for h in heads], axis=1),
                              rows=slice((s * GDN_HEADS + g0) * GDN_DK, (s * GDN_HEADS + g0 + grp) * GDN_DK)))

    for u in units:
        u["aq"] = _dot_nt(jnp.concatenate([u["kb"], u["qs"]], axis=0), u["ks"])
    for u in units:
        decay = jnp.exp(jnp.where(incl, u["gcol"] - u["grow"], -jnp.inf))
        u["pw"] = -jnp.where(strict, u["aq"][:r] * decay, 0.0)
        u["attn"] = jnp.where(incl, u["aq"][r:] * decay, 0.0)
        u["x"] = jnp.concatenate([u["vb"], u["kb"] * u["egcol"]], axis=1)
    span = 1
    while span < c:
        span *= 2
        for u in units:
            if span < c:
                both = _dot(u["pw"], jnp.concatenate([u["pw"], u["x"]], axis=1))
                u["pw"] = both[:, :r]
                u["x"] = u["x"] + both[:, r:]
            else:
                u["x"] = u["x"] + _dot(u["pw"], u["x"])
    for u in units:
        u_val, w_k = u["x"][:, :GDN_DV], u["x"][:, GDN_DV:]
        u["s_old"] = s_scr[u["rows"], :]
        ws = _dot(jnp.concatenate([block_layout(w_k), block_layout(u["qs"] * u["egcol"])], axis=0), u["s_old"])
        u["v_new"] = u_val - ws[:r]
        u["o"] = ws[r:]
    for u in units:
        e_last = seqs[u["s"]]["e_last"]
        o = u["o"] + _dot(u["attn"], u["v_new"])
        e_rows = jnp.concatenate([jnp.broadcast_to(e_last[:, h:h + 1], (GDN_DK, 1)) for h in u["heads"]], axis=0)
        kd_t = block_layout(u["ks"] * u["ekdcol"]).T
        s_scr[u["rows"], :] = u["s_old"] * e_rows + _dot(kd_t, u["v_new"])
        for i, h in enumerate(u["heads"]):
            sl = slice(h * GDN_DK, (h + 1) * GDN_DK)
            oh = o[i * c:(i + 1) * c, :]
            zh = z_ref[u["s"], :, sl].astype(F32)
            yb_ref[u["s"], :, sl] = (((oh * _rms_scale(oh)) * ng) * (zh * jax.nn.sigmoid(zh))).astype(BF16)

    @pl.when(ci == pl.num_programs(1) - 1)
    def _():
        sout_ref[...] = s_scr[...]


def _gdn_stacked(q, k, v, p, gb, norm_g, bsz, t, c, nb):
    d = D_MODEL
    nc = t // c
    grp = min(GDN_HEADS, 256 // c)
    s_rows = nb * GDN_HEADS * GDN_DK

    def tok(col=0):
        return pl.BlockSpec((nb, c, d), lambda b, i, col=col: (b, i, col))

    yb, s_new = pl.pallas_call(
        functools.partial(_gdn_stacked_kernel, c=c, grp=grp, nb=nb),
        grid=(bsz // nb, nc),
        in_specs=[tok(), tok(), tok(), tok(COL_GZ),
                  pl.BlockSpec((nb, c, SMALL_COLS), lambda b, i: (b, i, 0)),
                  pl.BlockSpec((1, GDN_DV), lambda b, i: (0, 0))],
        out_specs=[tok(), pl.BlockSpec((s_rows, GDN_DV), lambda b, i: (b, 0))],
        out_shape=[jax.ShapeDtypeStruct((bsz, t, d), BF16),
                   jax.ShapeDtypeStruct((bsz * GDN_HEADS * GDN_DK, GDN_DV), F32)],
        scratch_shapes=[pltpu.VMEM((s_rows, GDN_DV), F32)],
        compiler_params=_params(("parallel", "arbitrary")),
        name="gdn_scan_prompt",
    )(*[a.reshape(bsz, t, -1) for a in (q, k, v, p, gb)], norm_g.reshape(1, GDN_DV))
    return yb.reshape(bsz * t, d), s_new.reshape(bsz, GDN_HEADS, GDN_DK, GDN_DV)


def _gdn_kernel(*refs, c, nb, has_s0):
    if has_s0:
        q_ref, k_ref, v_ref, z_ref, gb_ref, ng_ref, s0_ref, yb_ref, sout_ref, s_scr = refs
    else:
        q_ref, k_ref, v_ref, z_ref, gb_ref, ng_ref, yb_ref, sout_ref, s_scr = refs
    ci = pl.program_id(1)

    @pl.when(ci == 0)
    def _():
        if has_s0:
            s_scr[...] = s0_ref[...]
        else:
            s_scr[...] = jnp.zeros_like(s_scr)

    ri = lax.broadcasted_iota(jnp.int32, (c, c), 0)
    cj = lax.broadcasted_iota(jnp.int32, (c, c), 1)
    incl = ri >= cj
    strict = ri > cj
    ng = ng_ref[...]
    row = lax.broadcasted_iota(jnp.int32, (c, SMALL_COLS), 0)
    chains = [(s, h) for s in range(nb) for h in range(GDN_HEADS)]

    seq = []
    for s in range(nb):
        gb = gb_ref[s]
        gc = gb
        sh = 1
        while sh < c:
            gc = gc + jnp.where(row >= sh, pltpu.roll(gc, sh, axis=0), 0.0)
            sh *= 2
        g_last = gc[c - 1:c, :]
        seq.append(dict(gb=gb, gc=gc, gct=gc.T, e_g=jnp.exp(gc), e_kd=jnp.exp(g_last - gc), e_last=jnp.exp(g_last)))

    st = []
    for s, h in chains:
        sl = slice(h * GDN_DK, (h + 1) * GDN_DK)
        sq = seq[s]
        qh, kh, vh = q_ref[s, :, sl], k_ref[s, :, sl], v_ref[s, :, sl]
        beta = sq["gb"][:, GDN_HEADS + h:GDN_HEADS + h + 1]
        kb = kh * beta
        aq = _dot_nt(jnp.concatenate([kb, qh], axis=0), kh)
        st.append(dict(sl=sl, qh=qh, kh=kh, kb=kb, vb=vh * beta, aq=aq))

    for (s, h), e in zip(chains, st):
        sq = seq[s]
        decay = jnp.exp(jnp.where(incl, sq["gc"][:, h:h + 1] - sq["gct"][h:h + 1, :], -jnp.inf))
        e["neg"] = -jnp.where(strict, e["aq"][:c] * decay, 0.0)
        e["attn"] = jnp.where(incl, e["aq"][c:] * decay, 0.0)
        e["x"] = jnp.concatenate([e["vb"], e["kb"] * sq["e_g"][:, h:h + 1]], axis=1)
        e["pw"] = e["neg"]

    span = 1
    while span < c:
        span *= 2
        for e in st:
            if span < c:
                both = _dot(e["pw"], jnp.concatenate([e["pw"], e["x"]], axis=1))
                e["pw"] = both[:, :c]
                e["x"] = e["x"] + both[:, c:]
            else:
                e["x"] = e["x"] + _dot(e["pw"], e["x"])

    for (s, h), e in zip(chains, st):
        sq = seq[s]
        u_val, w_k = e["x"][:, :GDN_DV], e["x"][:, GDN_DV:]
        q_dec = e["qh"] * sq["e_g"][:, h:h + 1]
        e["s_old"] = s_scr[s, h]
        ws = _dot(jnp.concatenate([w_k, q_dec], axis=0), e["s_old"])
        e["v_new"] = u_val - ws[:c]
        e["o_inter"] = ws[c:]

    for (s, h), e in zip(chains, st):
        sq = seq[s]
        o = e["o_inter"] + _dot(e["attn"], e["v_new"])
        k_dec = e["kh"] * sq["e_kd"][:, h:h + 1]
        s_scr[s, h] = e["s_old"] * sq["e_last"][:, h:h + 1] + _dot(k_dec.T, e["v_new"])
        zh = z_ref[s, :, e["sl"]].astype(F32)
        yb_ref[s, :, e["sl"]] = (((o * _rms_scale(o)) * ng) * (zh * jax.nn.sigmoid(zh))).astype(BF16)

    @pl.when(ci == pl.num_programs(1) - 1)
    def _():
        sout_ref[...] = s_scr[...]


def _gdn(q, k, v, p, gb, norm_g, bsz, t, c, nb, s0=None):
    d = D_MODEL
    nc = t // c
    has_s0 = s0 is not None

    def tok(col=0):
        return pl.BlockSpec((nb, c, d), lambda b, i, col=col: (b, i, col))

    in_specs = [tok(), tok(), tok(), tok(COL_GZ),
                pl.BlockSpec((nb, c, SMALL_COLS), lambda b, i: (b, i, 0)),
                pl.BlockSpec((1, GDN_DV), lambda b, i: (0, 0))]
    args = [a.reshape(bsz, t, -1) for a in (q, k, v, p, gb)] + [norm_g.reshape(1, GDN_DV)]
    s_spec = pl.BlockSpec((nb, GDN_HEADS, GDN_DK, GDN_DV), lambda b, i: (b, 0, 0, 0))
    if has_s0:
        in_specs.append(s_spec)
        args.append(s0)
    yb, s_new = pl.pallas_call(
        functools.partial(_gdn_kernel, c=c, nb=nb, has_s0=has_s0),
        grid=(bsz // nb, nc),
        in_specs=in_specs,
        out_specs=[tok(), s_spec],
        out_shape=[jax.ShapeDtypeStruct((bsz, t, d), BF16),
                   jax.ShapeDtypeStruct((bsz, GDN_HEADS, GDN_DK, GDN_DV), F32)],
        scratch_shapes=[pltpu.VMEM((nb, GDN_HEADS, GDN_DK, GDN_DV), F32)],
        compiler_params=_params(("parallel", "arbitrary")),
        name="gdn_scan",
    )(*args)
    return yb.reshape(bsz * t, d), s_new


def _mem_attn_kernel(q_ref, k_ref, v_ref, o_ref, *, nb, tq):
    scale = MEM_HEAD_DIM ** -0.5
    pairs = [(s, h) for s in range(nb) for h in range(MEM_HEADS)]
    span = lambda s, h: (slice(s * tq, (s + 1) * tq), slice(h * MEM_HEAD_DIM, (h + 1) * MEM_HEAD_DIM))
    scores = [_dot_nt(q_ref[span(s, h)], k_ref[s, :, span(s, h)[1]]) * scale for s, h in pairs]
    probs = []
    for sc in scores:
        e = jnp.exp(sc - jnp.max(sc, axis=-1, keepdims=True))
        probs.append(e / jnp.sum(e, axis=-1, keepdims=True))
    for (s, h), prob in zip(pairs, probs):
        o_ref[span(s, h)] = _dot(prob, v_ref[s, :, span(s, h)[1]]).astype(BF16)


def _mem_attn(p, mk, mv, bsz, t, tq, nb):
    d = D_MODEL
    m = mk.shape[1]
    nq = t // tq
    assert nb == 1 or nq == 1
    kv_spec = pl.BlockSpec((nb, m, d), lambda b, i: (b, 0, 0))
    return pl.pallas_call(
        functools.partial(_mem_attn_kernel, nb=nb, tq=tq),
        grid=(bsz // nb, nq),
        in_specs=[pl.BlockSpec((nb * tq, d), lambda b, i: (b * nq + i, COL_MQ)), kv_spec, kv_spec],
        out_specs=pl.BlockSpec((nb * tq, d), lambda b, i: (b * nq + i, 0)),
        out_shape=jax.ShapeDtypeStruct((bsz * t, d), BF16),
        compiler_params=_params(("parallel", "arbitrary")),
        name="mem_attn",
    )(p, mk, mv)


def _mem_attn_cache_kernel(q_ref, k_ref, v_ref, o_ref, *, nb, tq):
    scale = MEM_HEAD_DIM ** -0.5
    m = k_ref.shape[1]
    shape = (MEM_HEADS * tq, m * MEM_HEADS)
    own = (lax.broadcasted_iota(jnp.int32, shape, 1) % MEM_HEADS) == (lax.broadcasted_iota(jnp.int32, shape, 0) // tq)
    qs, ks, vs = [], [], []
    for s in range(nb):
        rows = slice(s * tq, (s + 1) * tq)
        qs.append(jnp.concatenate([q_ref[rows, h * MEM_HEAD_DIM:(h + 1) * MEM_HEAD_DIM] for h in range(MEM_HEADS)],
                                  axis=0))
        ks.append(k_ref[s].reshape(m * MEM_HEADS, MEM_HEAD_DIM))
        vs.append(v_ref[s].reshape(m * MEM_HEADS, MEM_HEAD_DIM))
    scores = [jnp.where(own, _dot_nt(q, k) * scale, -jnp.inf) for q, k in zip(qs, ks)]
    probs = []
    for sc in scores:
        e = jnp.exp(sc - jnp.max(sc, axis=-1, keepdims=True))
        probs.append(e / jnp.sum(e, axis=-1, keepdims=True))
    for s, (prob, v) in enumerate(zip(probs, vs)):
        o = _dot(prob, v)
        o_ref[s * tq:(s + 1) * tq, :] = jnp.concatenate(
            [o[h * tq:(h + 1) * tq] for h in range(MEM_HEADS)], axis=1).astype(BF16)


def _mem_attn_cache(p, mk, mv, bsz, t, nb):
    d = D_MODEL
    kv_spec = pl.BlockSpec((nb,) + mk.shape[1:], lambda b: (b, 0, 0, 0))
    return pl.pallas_call(
        functools.partial(_mem_attn_cache_kernel, nb=nb, tq=t),
        grid=(bsz // nb,),
        in_specs=[pl.BlockSpec((nb * t, d), lambda b: (b, COL_MQ)), kv_spec, kv_spec],
        out_specs=pl.BlockSpec((nb * t, d), lambda b: (b, 0)),
        out_shape=jax.ShapeDtypeStruct((bsz * t, d), BF16),
        compiler_params=_params(("parallel",)),
        name="mem_attn_cache",
    )(p, mk, mv)


def _merge_kernel(x_ref, ya_ref, yb_ref, ym_ref, g0_ref, g1_ref, g2_ref, wc_ref, wg_ref, wm_ref, wo_ref,
                  ng_ref, x1_ref, h2_ref):
    d = functools.partial(jnp.dot, preferred_element_type=F32)
    merged = (jax.nn.sigmoid(_f32(g0_ref)) * d(ya_ref[...], wc_ref[...])
              + jax.nn.sigmoid(_f32(g1_ref)) * d(yb_ref[...], wg_ref[...])
              + jax.nn.sigmoid(_f32(g2_ref)) * d(ym_ref[...], wm_ref[...]))
    x1 = x_ref[...] + d(merged.astype(BF16), wo_ref[...])
    x1_ref[...] = x1
    h2_ref[...] = ((x1 * _rms_scale(x1)) * ng_ref[...]).astype(BF16)


def _merge(x, ya, yb, ym, p, wc, wg, wm, wo, norm_g, tm):
    n, d = x.shape
    tok = pl.BlockSpec((tm, d), lambda i: (i, 0))
    wsp = pl.BlockSpec((d, d), lambda i: (0, 0))

    def gate(j):
        return pl.BlockSpec((tm, d), lambda i, j=j: (i, COL_BG + j))

    return pl.pallas_call(
        _merge_kernel,
        grid=(n // tm,),
        in_specs=[tok, tok, tok, tok, gate(0), gate(1), gate(2), wsp, wsp, wsp, wsp,
                  pl.BlockSpec((1, d), lambda i: (0, 0))],
        out_specs=[tok, tok],
        out_shape=[jax.ShapeDtypeStruct((n, d), F32), jax.ShapeDtypeStruct((n, d), BF16)],
        compiler_params=_params(("parallel",)),
        name="merge",
    )(x, ya, yb, ym, p, p, p, wc, wg, wm, wo, norm_g.reshape(1, d))


_CAND_CELLS = tuple((a, b) for a in range(PEER_TOPK) for b in range(PEER_TOPK) if (a + 1) * (b + 1) <= PEER_TOPK)

TILE_PAD = 4 * SUBLANES
SHIFT_S1, SHIFT_E1 = 0, SUBLANES


def _put_tile(ref, h, ns, shift, val, pad=True):
    if pad and shift:
        ref[h, ns, :shift, :] = jnp.zeros((shift, LANES), F32)
    ref[h, ns, shift:shift + PEER_N_KEYS, :] = val
    if pad and shift < TILE_PAD:
        ref[h, ns, shift + PEER_N_KEYS:, :] = jnp.zeros((TILE_PAD - shift, LANES), F32)


def _top_values(x, count):
    vals = []
    cur = x
    for r in range(count):
        m = jnp.max(cur, axis=0, keepdims=True)
        vals.append(m)
        if r + 1 < count:
            cur = jnp.where(cur == m, -jnp.inf, cur)
    return vals


def _top_ranked(x, count):
    n_rows = x.shape[0]
    iota = lax.broadcasted_iota(jnp.int32, x.shape, 0).astype(F32)
    vals = []
    cur = x
    rank = jnp.full(x.shape, float(count), F32)
    for r in range(count):
        m = jnp.max(cur, axis=0, keepdims=True)
        first = jnp.min(jnp.where(cur == m, iota, float(n_rows)), axis=0, keepdims=True)
        one = iota == first
        rank = jnp.where(one, float(r), rank)
        cur = jnp.where(one, -jnp.inf, cur)
        vals.append(m)
    return vals, rank


def _count_rows(mask):
    return jnp.sum(jnp.where(mask, 1.0, 0.0), axis=0, keepdims=True)


def _route_kernel(h_ref, wq_ref, keys_ref, cut_ref, e0_ref, s1_ref, e1_ref, q_scr, tie_ref):
    tn = h_ref.shape[0]
    nsb = tn // LANES
    q = jnp.dot(h_ref[...], wq_ref[...], preferred_element_type=F32).astype(BF16)

    def store(h, cut, e0, s1, e1):
        for ns in range(nsb):
            cols = slice(ns * LANES, (ns + 1) * LANES)
            cut_ref[h, ns] = cut[:, cols]
            e0_ref[h, ns] = e0[:, cols]
            _put_tile(s1_ref, h, ns, SHIFT_S1, s1[:, cols])
            _put_tile(e1_ref, h, ns, SHIFT_E1, e1[:, cols])

    for h in range(PEER_HEADS):
        sc, top = [], []
        tied = jnp.zeros((1, tn), F32)
        for part in range(2):
            col = (2 * h + part) * PEER_D_HALF
            q_scr[2 * h + part] = q[:, col:col + PEER_D_HALF]
            s_t = lax.dot_general(keys_ref[part, h], q[:, col:col + PEER_D_HALF],
                                  (((1,), (1,)), ((), ())), preferred_element_type=F32)
            sc.append(s_t)
            top.append(_top_values(s_t, PEER_TOPK))
            tied = jnp.maximum(tied, jnp.abs(_count_rows(s_t >= top[part][-1]) - PEER_TOPK))
        cand = jnp.concatenate([top[0][a] + top[1][b] for a, b in _CAND_CELLS], axis=0)
        best = _top_values(cand, PEER_TOPK)
        c_max, thr = best[0], best[-1]
        z = jnp.exp(best[0] - c_max)
        for r in range(1, PEER_TOPK):
            z = z + jnp.exp(best[r] - c_max)
        top1 = jnp.concatenate(top[1], axis=0)
        n_sel = sum(_count_rows(top[0][a] + top1 >= thr) for a in range(PEER_TOPK))
        tied = jnp.maximum(tied, jnp.abs(n_sel - PEER_TOPK))
        cut = jnp.full(sc[0].shape, jnp.inf, F32)
        for b in range(PEER_TOPK):
            cut = jnp.where(sc[0] + top[1][b] >= thr, top[1][b], cut)
        store(h, cut, jnp.exp(sc[0] - top[0][0]) / z, sc[1], jnp.exp(sc[1] - top[1][0]))
        for ns in range(nsb):
            tie_ref[h * nsb + ns] = jnp.max(tied[:, ns * LANES:(ns + 1) * LANES])

    def exact_block(idx, carry):
        @pl.when(tie_ref[idx] > 0.0)
        def _():
            h = idx // nsb
            ns = idx % nsb
            rows = pl.ds(pl.multiple_of(ns * LANES, LANES), LANES)
            sc, top, rank = [], [], []
            for part in range(2):
                s_t = lax.dot_general(keys_ref[part, h], q_scr[2 * h + part, rows, :],
                                      (((1,), (1,)), ((), ())), preferred_element_type=F32)
                vals, rk = _top_ranked(s_t, PEER_TOPK)
                sc.append(s_t)
                top.append(vals)
                rank.append(rk)
            top1 = jnp.concatenate(top[1], axis=0)
            cand = jnp.concatenate([top[0][a] + top1 for a in range(PEER_TOPK)], axis=0)
            best, pos = _top_ranked(cand, PEER_TOPK)
            z = jnp.exp(best[0] - best[0])
            for r in range(1, PEER_TOPK):
                z = z + jnp.exp(best[r] - best[0])
            cnt = jnp.zeros_like(sc[0])
            for a in range(PEER_TOPK):
                cnt_a = _count_rows(pos[a * PEER_TOPK:(a + 1) * PEER_TOPK] < PEER_TOPK)
                cnt = jnp.where(rank[0] == float(a), cnt_a, cnt)
            cut_ref[h, ns] = jnp.where(cnt > 0.0, 1.0 - cnt, jnp.inf)
            e0_ref[h, ns] = jnp.exp(sc[0] - top[0][0]) / z
            _put_tile(s1_ref, h, ns, SHIFT_S1, -rank[1], pad=False)
            _put_tile(e1_ref, h, ns, SHIFT_E1, jnp.exp(sc[1] - top[1][0]), pad=False)
        return carry

    lax.fori_loop(0, PEER_HEADS * nsb, exact_block, 0)


def _route(h2, wq, keys, tn):
    n, d = h2.shape
    nk = PEER_N_KEYS
    tiled = pl.BlockSpec((PEER_HEADS, tn // LANES, nk + TILE_PAD, LANES), lambda i: (0, i, 0, 0))
    tiled_shape = jax.ShapeDtypeStruct((PEER_HEADS, n // LANES, nk + TILE_PAD, LANES), F32)
    rowed = pl.BlockSpec((PEER_HEADS, tn // LANES, nk, LANES), lambda i: (0, i, 0, 0))
    rowed_shape = jax.ShapeDtypeStruct((PEER_HEADS, n // LANES, nk, LANES), F32)
    return pl.pallas_call(
        _route_kernel,
        grid=(n // tn,),
        in_specs=[pl.BlockSpec((tn, d), lambda i: (i, 0)),
                  pl.BlockSpec(wq.shape, lambda i: (0, 0)),
                  pl.BlockSpec(keys.shape, lambda i: (0, 0, 0, 0))],
        out_specs=[rowed, rowed, tiled, tiled],
        out_shape=[rowed_shape, rowed_shape, tiled_shape, tiled_shape],
        scratch_shapes=[pltpu.VMEM((2 * PEER_HEADS, tn, PEER_D_HALF), BF16),
                        pltpu.SMEM((PEER_HEADS * (tn // LANES),), F32)],
        compiler_params=_params(("parallel",)),
        name="peer_route",
    )(h2, wq, keys)


def _gelu(x):
    return 0.5 * x * (1.0 + lax.erf(x * (2.0 ** -0.5)))


PEER_TILE = 2048
GATE_HALF = 32
GATE_ROWS = 8


def _peer_kernel(h_ref, x1_ref, u_ref, vt_ref, cut_ref, e0_ref, s1_ref, e1_ref, ng_ref,
                 y_ref, acc_ref, act_ref, coef_ref, *, te):
    ei = pl.program_id(1)
    tn = h_ref.shape[0]
    groups = te // PEER_N_KEYS

    @pl.when(ei == 0)
    def _():
        acc_ref[...] = jnp.zeros_like(acc_ref)

    nsb = tn // LANES
    act = _gelu(lax.dot_general(u_ref[...], h_ref[...], (((1,), (1,)), ((), ())),
                                preferred_element_type=F32))
    for ns in range(nsb):
        act_ref[ns] = act[:, ns * LANES:(ns + 1) * LANES]

    def gate_block(ns):
        for jh in range(PEER_N_KEYS // GATE_HALF):
            jrows = slice(jh * GATE_HALF, (jh + 1) * GATE_HALF)
            for g0 in range(0, groups, GATE_ROWS):
                gates = [None] * GATE_ROWS
                for h in range(PEER_HEADS):
                    s1_t = s1_ref[h, ns, SHIFT_S1 + jh * GATE_HALF:SHIFT_S1 + (jh + 1) * GATE_HALF, :]
                    e1_t = e1_ref[h, ns, SHIFT_E1 + jh * GATE_HALF:SHIFT_E1 + (jh + 1) * GATE_HALF, :]
                    for k in range(GATE_ROWS):
                        sel = s1_t >= cut_ref[h, ns, g0 + k:g0 + k + 1, :]
                        term = jnp.where(sel, e1_t, 0.0) * e0_ref[h, ns, g0 + k:g0 + k + 1, :]
                        gates[k] = term if gates[k] is None else gates[k] + term
                for k in range(GATE_ROWS):
                    r0 = (g0 + k) * PEER_N_KEYS + jh * GATE_HALF
                    coef_ref[ns, r0:r0 + GATE_HALF, :] = (gates[k] * act_ref[ns, r0:r0 + GATE_HALF, :]).astype(BF16)

    def lane_block(ns, carry):
        gate_block(ns)
        return carry

    def mix_values(lo, hi):
        coef = jnp.concatenate([coef_ref[ns] for ns in range(lo, hi)], axis=1)
        acc_ref[:, lo * LANES:hi * LANES] += jnp.dot(vt_ref[0], coef, preferred_element_type=F32)

    half = nsb // 2
    lax.fori_loop(0, half, lane_block, 0)
    if half:
        mix_values(0, half)
    for ns in range(half, nsb):
        gate_block(ns)
    mix_values(half, nsb)

    @pl.when(ei == pl.num_programs(1) - 1)
    def _():
        x2 = x1_ref[...] + acc_ref[...].T
        y_ref[...] = (x2 * _rms_scale(x2)) * ng_ref[...]


def _peer(h2, x1, u_bf, vt_bf, cut, e0, s1, e1, norm_g, tn):
    n, d = h2.shape
    ne = u_bf.shape[0]
    te = vt_bf.shape[2]
    nk = PEER_N_KEYS
    tok = pl.BlockSpec((tn, d), lambda i, e: (i, 0))
    tiled = pl.BlockSpec((PEER_HEADS, tn // LANES, nk + TILE_PAD, LANES), lambda i, e: (0, i, 0, 0))
    rowed = pl.BlockSpec((PEER_HEADS, tn // LANES, te // nk, LANES), lambda i, e: (0, i, e, 0))
    return pl.pallas_call(
        functools.partial(_peer_kernel, te=te),
        grid=(n // tn, ne // te),
        in_specs=[tok, tok,
                  pl.BlockSpec((te, d), lambda i, e: (e, 0)),
                  pl.BlockSpec((1, d, te), lambda i, e: (e, 0, 0)),
                  rowed, rowed, tiled, tiled,
                  pl.BlockSpec((1, d), lambda i, e: (0, 0))],
        out_specs=tok,
        out_shape=jax.ShapeDtypeStruct((n, d), F32),
        scratch_shapes=[pltpu.VMEM((d, tn), F32), pltpu.VMEM((tn // LANES, te, LANES), F32),
                        pltpu.VMEM((tn // LANES, te, LANES), BF16)],
        compiler_params=_params(("parallel", "arbitrary")),
        name="peer_mix",
    )(h2, x1, u_bf, vt_bf, cut, e0, s1, e1, norm_g.reshape(1, d))


def _tile(n, pref):
    return math.gcd(n, pref)


def _layer(x, mk, mv, bsz, t, weights, conv_state=None, gdn_conv_state=None, gdn_s0=None):
    (norm_mix_g, w_main, w_ab, conv_w, gdn_conv_w, a_log, dt_bias, gdn_norm_g, wc, wg, wm, wo, norm_ffn_g,
     wq, keys, u_bf, vt_bf, final_norm_g) = weights
    n = bsz * t
    sample = conv_state is not None
    p, pab = _inproj(x, norm_mix_g, w_main, w_ab, _tile(n, 2048))
    rows = _tile(n, 256) if sample else _tile(t, 512)
    ya, q, k, v, gb, u_tail = _prep(p, pab, conv_w, gdn_conv_w, a_log, dt_bias, rows, max(t // rows, 1),
                                    conv_state, gdn_conv_state)
    chunk = _tile(t, 64)
    if sample:
        yb, s_new = _gdn(q, k, v, p, gb, gdn_norm_g, bsz, t, chunk, 4, gdn_s0)
    else:
        yb, s_new = _gdn_stacked(q, k, v, p, gb, gdn_norm_g, bsz, t, chunk, _tile(bsz, 4))
    if mk.ndim == 4:
        ym = _mem_attn_cache(p, mk, mv, bsz, t, _tile(bsz, 4))
    else:
        ym = _mem_attn(p, mk, mv, bsz, t, _tile(t, 512), 1)
    x1, h2 = _merge(x, ya, yb, ym, p, wc, wg, wm, wo, norm_ffn_g, _tile(n, 512))
    if sample:
        keep = lambda a: a.reshape(bsz, t, -1)[:, t // 2:, :].reshape(bsz * t // 2, -1)
        x1, h2 = keep(x1), keep(h2)
    tn = _tile(x1.shape[0], 512)
    cut, e0, s1, e1 = _route(h2, wq, keys, tn)
    y = _peer(h2, x1, u_bf, vt_bf, cut, e0, s1, e1, final_norm_g, tn)
    return y, u_tail, p, s_new


def kernel(x_prompt, x_sample, mem_prompt, cache_mem_k, cache_mem_v, state_conv, state_gdn_conv, state_gdn, norm_mix_g, w_in, conv_w, gdn_conv_w, gdn_a_log, gdn_dt_bias, gdn_norm_g, mem_norm_g, w_mem_kv, w_br_conv, w_br_gdn, w_br_mem, w_out, norm_ffn_g, peer_w_q, peer_sub_keys, peer_u, peer_v, final_norm_g):
    depth = w_in.shape[0]
    assert depth == 1, "single-layer step"
    d = D_MODEL
    bp, tp, _ = x_prompt.shape
    bs, ts, _ = x_sample.shape
    assert ts == SAMPLE_PAD_T // 2
    m_tok = mem_prompt.shape[1]
    mq = MEM_HEADS * MEM_HEAD_DIM
    l = 0

    w = w_in[l]
    ga0 = 7 * d
    w_main = jnp.concatenate([w[:, :ga0], w[:, ga0 + 2 * GDN_HEADS:]], axis=1).astype(BF16)
    w_ab = jnp.pad(w[:, ga0:ga0 + 2 * GDN_HEADS], ((0, 0), (0, SMALL_COLS - 2 * GDN_HEADS))).astype(BF16)
    weights = (norm_mix_g[l], w_main, w_ab, conv_w[l], gdn_conv_w[l], gdn_a_log[l], gdn_dt_bias[l], gdn_norm_g[l],
               w_br_conv[l].astype(BF16), w_br_gdn[l].astype(BF16), w_br_mem[l].astype(BF16), w_out[l].astype(BF16),
               norm_ffn_g[l], peer_w_q[l].astype(BF16), peer_sub_keys[l].astype(BF16),
               peer_u[l].astype(BF16),
               peer_v[l].astype(BF16).reshape(-1, PEER_TILE, d).transpose(0, 2, 1), final_norm_g)

    kv_shape = (bp, m_tok, mq)
    mk_p, mv_p = _mem_kv(mem_prompt.reshape(bp * m_tok, d), mem_norm_g[l], w_mem_kv[l].astype(BF16),
                         _tile(bp * m_tok, 256))
    y_p, u_tail_p, p_p, s_p = _layer(x_prompt.reshape(bp * tp, d), mk_p.reshape(kv_shape), mv_p.reshape(kv_shape),
                                     bp, tp, weights)
    rows_p = _tile(tp, 512)
    cb_p = u_tail_p.reshape(bp, tp // rows_p, SUBLANES, d)[:, -1, SUBLANES - (CONV_K - 1):, :]
    gb_p = p_p.reshape(bp, tp, -1)[:, tp - (GDN_CONV_K - 1):, COL_GQ * d:(COL_GV + 1) * d].astype(F32)

    pad_t = SAMPLE_PAD_T
    lead = pad_t - ts
    x_s = jnp.pad(x_sample, ((0, 0), (lead, 0), (0, 0))).reshape(bs * pad_t, d)
    cst = jnp.pad(state_conv[l], ((0, 0), (lead - (CONV_K - 1), ts), (0, 0))).reshape(bs * pad_t, d)
    gst = jnp.pad(state_gdn_conv[l], ((0, 0), (lead - (GDN_CONV_K - 1), ts), (0, 0))).reshape(bs * pad_t, 3 * d)
    y_s, u_s, p_s, s_s = _layer(x_s, cache_mem_k.reshape(cache_mem_k.shape[1:]), cache_mem_v.reshape(cache_mem_v.shape[1:]),
                                bs, pad_t, weights, cst, gst, state_gdn.reshape(bs, GDN_HEADS, GDN_DK, GDN_DV))
    y_s = y_s.reshape(bs, ts, d)
    cb_s = u_s.reshape(bs, pad_t, d)[:, pad_t - (CONV_K - 1):, :]
    gb_s = p_s.reshape(bs, pad_t, -1)[:, pad_t - (GDN_CONV_K - 1):, COL_GQ * d:(COL_GV + 1) * d].astype(F32)

    mshape = (depth, bp, m_tok, MEM_HEADS, MEM_HEAD_DIM)
    return (y_p.reshape(bp, tp, d), y_s,
            mk_p.reshape(mshape), mv_p.reshape(mshape),
            cb_p[None], gb_p[None], s_p[None],
            cb_s[None], gb_s[None], s_s[None])
```

```python
import functools
import math

import jax
import jax.numpy as jnp
from jax import lax
from jax.experimental import pallas as pl
from jax.experimental.pallas import tpu as pltpu

F32 = jnp.float32
BF16 = jnp.bfloat16

EPS = 1e-6
D_MODEL = 1024
CONV_K = 3
GDN_HEADS = 8
GDN_DK = 128
GDN_DV = 128
GDN_CONV_K = 4
MEM_HEADS = 4
MEM_HEAD_DIM = 256
PEER_HEADS = 8
PEER_N_KEYS = 128
PEER_D_HALF = 128
PEER_TOPK = 16
N_BRANCHES = 3

COL_AU, COL_AB, COL_AC, COL_GQ, COL_GK, COL_GV, COL_GZ, COL_MQ, COL_BG = 0, 1, 2, 3, 4, 5, 6, 7, 8
N_MAIN_BLOCKS = 11
SMALL_COLS = 128
SUBLANES = 8
LANES = 128
SAMPLE_PAD_T = 8
VMEM_LIMIT = 56 * 1024 * 1024


def _params(sem):
    return pltpu.CompilerParams(dimension_semantics=sem, vmem_limit_bytes=VMEM_LIMIT)


def _dot(a, b):
    return jnp.dot(a.astype(BF16), b.astype(BF16), preferred_element_type=F32)


def _dot_nt(a, b):
    return lax.dot_general(a.astype(BF16), b.astype(BF16), (((1,), (1,)), ((), ())),
                           preferred_element_type=F32)


def _rms_scale(x):
    return lax.rsqrt(jnp.mean(x * x, axis=-1, keepdims=True) + EPS)


def _inproj_kernel(x_ref, g_ref, w_ref, wab_ref, p_ref, pab_ref, h_ref):
    @pl.when(pl.program_id(1) == 0)
    def _():
        x = x_ref[...]
        h = ((x * _rms_scale(x)) * g_ref[...]).astype(BF16)
        h_ref[...] = h
        pab_ref[...] = jnp.dot(h, wab_ref[...], preferred_element_type=F32)

    p_ref[...] = jnp.dot(h_ref[...], w_ref[...], preferred_element_type=F32).astype(p_ref.dtype)


def _inproj(x, g, w_main, w_ab, tm):
    n, d = x.shape
    nblk = w_main.shape[1] // d
    return pl.pallas_call(
        _inproj_kernel,
        grid=(n // tm, nblk),
        in_specs=[
            pl.BlockSpec((tm, d), lambda i, j: (i, 0)),
            pl.BlockSpec((1, d), lambda i, j: (0, 0)),
            pl.BlockSpec((d, d), lambda i, j: (0, j)),
            pl.BlockSpec((d, SMALL_COLS), lambda i, j: (0, 0)),
        ],
        out_specs=[
            pl.BlockSpec((tm, d), lambda i, j: (i, j)),
            pl.BlockSpec((tm, SMALL_COLS), lambda i, j: (i, 0)),
        ],
        out_shape=[
            jax.ShapeDtypeStruct((n, nblk * d), BF16),
            jax.ShapeDtypeStruct((n, SMALL_COLS), F32),
        ],
        scratch_shapes=[pltpu.VMEM((tm, d), BF16)],
        compiler_params=_params(("parallel", "arbitrary")),
        name="inproj",
    )(x, g.reshape(1, d), w_main, w_ab)


def _mem_kv_kernel(x_ref, g_ref, w_ref, k_ref, v_ref):
    x = x_ref[...]
    h = ((x * _rms_scale(x)) * g_ref[...]).astype(BF16)
    kv = jnp.dot(h, w_ref[...], preferred_element_type=F32)
    half = kv.shape[1] // 2
    k_ref[...] = kv[:, :half]
    v_ref[...] = kv[:, half:]


def _mem_kv(x, g, w, tm):
    n, d = x.shape
    c = w.shape[1]
    out = jax.ShapeDtypeStruct((n, c // 2), F32)
    ospec = pl.BlockSpec((tm, c // 2), lambda i: (i, 0))
    return pl.pallas_call(
        _mem_kv_kernel,
        grid=(n // tm,),
        in_specs=[
            pl.BlockSpec((tm, d), lambda i: (i, 0)),
            pl.BlockSpec((1, d), lambda i: (0, 0)),
            pl.BlockSpec((d, c), lambda i: (0, 0)),
        ],
        out_specs=[ospec, ospec],
        out_shape=[out, out],
        compiler_params=_params(("parallel",)),
        name="mem_kv",
    )(x, g.reshape(1, d), w)


def _causal_conv(prev, cur, w_ref, k):
    full = jnp.concatenate([prev, cur], axis=0)
    y = cur * w_ref[k - 1:k, :]
    for s in range(1, k):
        y = y + pltpu.roll(full, s, axis=0)[prev.shape[0]:, :] * w_ref[k - 1 - s:k - s, :]
    return y


def _f32(ref):
    return ref[...].astype(F32)


def _prep_kernel(*refs, rows, tiles_per_seq, sample):
    if sample:
        (au, ab, ac, gq, gk, gv, pab, au_p, ac_p, gq_p, gk_p, gv_p, cw, gw, alog, dtb, cst, gst,
         ya_o, q_o, k_o, v_o, gb_o, ut_o) = refs
    else:
        (au, ab, ac, gq, gk, gv, pab, au_p, ac_p, gq_p, gk_p, gv_p, cw, gw, alog, dtb,
         ya_o, q_o, k_o, v_o, gb_o, ut_o) = refs
    d = D_MODEL
    first = (pl.program_id(0) % tiles_per_seq) == 0
    keep = jnp.where(first, 0.0, 1.0).astype(F32)

    u = _f32(ac) * _f32(au)
    u_prev = (_f32(ac_p) * _f32(au_p)) * keep
    if sample:
        real = (lax.broadcasted_iota(jnp.int32, (rows, 1), 0) % SAMPLE_PAD_T) >= (SAMPLE_PAD_T // 2)
        u = jnp.where(real, u, cst[...])
    ya_o[...] = (_f32(ab) * _causal_conv(u_prev, u, cw, CONV_K)).astype(BF16)
    ut_o[...] = u[rows - ut_o.shape[0]:, :]

    outs = (q_o, k_o, v_o)
    for idx, (cur_ref, prev_ref) in enumerate(((gq, gq_p), (gk, gk_p), (gv, gv_p))):
        cur = _f32(cur_ref)
        if sample:
            cur = jnp.where(real, cur, gst[:, idx * d:(idx + 1) * d])
        prev = _f32(prev_ref) * keep
        wslab = gw.at[:, idx * d:(idx + 1) * d]
        y = _causal_conv(prev, cur, wslab, GDN_CONV_K)
        y = y * jax.nn.sigmoid(y)
        if idx < 2:
            scale = GDN_DK ** -0.5 if idx == 0 else 1.0
            for h in range(GDN_HEADS):
                yh = y[:, h * GDN_DK:(h + 1) * GDN_DK]
                r = lax.rsqrt(jnp.sum(yh * yh, axis=-1, keepdims=True) + EPS)
                outs[idx][:, h * GDN_DK:(h + 1) * GDN_DK] = yh * (r * scale)
        else:
            outs[idx][...] = y

    pa = pab[...]
    lane = lax.broadcasted_iota(jnp.int32, pa.shape, 1)
    z = pa + dtb[...]
    softplus = jnp.maximum(z, 0.0) + jnp.log1p(jnp.exp(-jnp.abs(z)))
    g_log = -jnp.exp(alog[...]) * softplus
    gb = jnp.where(lane < GDN_HEADS, g_log, jnp.where(lane < 2 * GDN_HEADS, jax.nn.sigmoid(pa), 0.0))
    if sample:
        gb = jnp.where(real, gb, 0.0)
    gb_o[...] = gb


def _prep(p, pab, conv_w, gdn_conv_w, a_log, dt_bias, rows, tiles_per_seq, conv_state=None, gdn_state=None):
    n = p.shape[0]
    d = D_MODEL
    sample = conv_state is not None
    nt = n // rows
    prev_rows = 2 * SUBLANES
    rpb = rows // prev_rows

    def cur(c):
        return pl.BlockSpec((rows, d), lambda i, c=c: (i, c))

    def prev(c):
        return pl.BlockSpec((prev_rows, d), lambda i, c=c: (jnp.maximum(i * rpb - 1, 0), c))

    alog128 = jnp.zeros((1, SMALL_COLS), F32).at[0, :GDN_HEADS].set(a_log)
    dtb128 = jnp.zeros((1, SMALL_COLS), F32).at[0, :GDN_HEADS].set(dt_bias)
    in_specs = [cur(COL_AU), cur(COL_AB), cur(COL_AC), cur(COL_GQ), cur(COL_GK), cur(COL_GV),
                pl.BlockSpec((rows, SMALL_COLS), lambda i: (i, 0)),
                prev(COL_AU), prev(COL_AC), prev(COL_GQ), prev(COL_GK), prev(COL_GV),
                pl.BlockSpec((CONV_K, d), lambda i: (0, 0)),
                pl.BlockSpec((GDN_CONV_K, 3 * d), lambda i: (0, 0)),
                pl.BlockSpec((1, SMALL_COLS), lambda i: (0, 0)),
                pl.BlockSpec((1, SMALL_COLS), lambda i: (0, 0))]
    args = [p, p, p, p, p, p, pab, p, p, p, p, p, conv_w, gdn_conv_w, alog128, dtb128]
    if sample:
        in_specs += [pl.BlockSpec((rows, d), lambda i: (i, 0)), pl.BlockSpec((rows, 3 * d), lambda i: (i, 0))]
        args += [conv_state, gdn_state]
    tail = rows if sample else SUBLANES
    out_specs = [pl.BlockSpec((rows, d), lambda i: (i, 0))] * 4 + [
        pl.BlockSpec((rows, SMALL_COLS), lambda i: (i, 0)),
        pl.BlockSpec((tail, d), lambda i: (i, 0))]
    out_shape = [jax.ShapeDtypeStruct((n, d), BF16)] + [jax.ShapeDtypeStruct((n, d), F32)] * 3 + [
        jax.ShapeDtypeStruct((n, SMALL_COLS), F32),
        jax.ShapeDtypeStruct((nt * tail, d), F32)]
    return pl.pallas_call(
        functools.partial(_prep_kernel, rows=rows, tiles_per_seq=tiles_per_seq, sample=sample),
        grid=(nt,),
        in_specs=in_specs,
        out_specs=out_specs,
        out_shape=out_shape,
        compiler_params=_params(("parallel",)),
        name="prep",
    )(*args)


def _chunk_cumsum(g, c):
    row = lax.broadcasted_iota(jnp.int32, g.shape, 0)
    sh = 1
    while sh < c:
        g = g + jnp.where(row >= sh, pltpu.roll(g, sh, axis=0), 0.0)
        sh *= 2
    return g


def _gdn_stacked_kernel(q_ref, k_ref, v_ref, z_ref, gb_ref, ng_ref, yb_ref, sout_ref, s_scr, *, c, grp, nb):
    ci = pl.program_id(1)

    @pl.when(ci == 0)
    def _():
        s_scr[...] = jnp.zeros_like(s_scr)

    r = grp * c
    ng = ng_ref[...]
    seqs = []
    for s in range(nb):
        gb = gb_ref[s]
        gc = _chunk_cumsum(gb, c)
        g_last = gc[c - 1:c, :]
        seqs.append(dict(gb=gb, gc=gc, gct=gc.T, e_g=jnp.exp(gc), e_kd=jnp.exp(g_last - gc), e_last=jnp.exp(g_last)))

    ri = lax.broadcasted_iota(jnp.int32, (r, r), 0)
    cj = lax.broadcasted_iota(jnp.int32, (r, r), 1)
    same = (ri // c) == (cj // c)
    incl = same & (ri >= cj)
    strict = same & (ri > cj)
    head_block = (lax.broadcasted_iota(jnp.int32, (r, grp * GDN_DK), 0) // c
                  == lax.broadcasted_iota(jnp.int32, (r, grp * GDN_DK), 1) // GDN_DK)

    def block_layout(x):
        return jnp.where(head_block, jnp.concatenate([x] * grp, axis=1), 0.0)

    units = []
    for s in range(nb):
        for g0 in range(0, GDN_HEADS, grp):
            heads = list(range(g0, g0 + grp))
            sq = seqs[s]

            def stack(ref, s=s, heads=heads):
                return jnp.concatenate([ref[s, :, h * GDN_DK:(h + 1) * GDN_DK] for h in heads], axis=0)

            def col(a, off=0, heads=heads):
                return jnp.concatenate([a[:, off + h:off + h + 1] for h in heads], axis=0)

            qs, ks, vs = stack(q_ref), stack(k_ref), stack(v_ref)
            beta = col(sq["gb"], GDN_HEADS)
            kb = ks * beta
            units.append(dict(s=s, heads=heads, qs=qs, ks=ks, kb=kb, vb=vs * beta,
                              gcol=col(sq["gc"]), egcol=col(sq["e_g"]), ekdcol=col(sq["e_kd"]),
                              grow=jnp.concatenate([sq["gct"][h:h + 1, :] for h in heads], axis=1),
                              rows=slice((s * GDN_HEADS + g0) * GDN_DK, (s * GDN_HEADS + g0 + grp) * GDN_DK)))

    for u in units:
        u["aq"] = _dot_nt(jnp.concatenate([u["kb"], u["qs"]], axis=0), u["ks"])
    for u in units:
        decay = jnp.exp(jnp.where(incl, u["gcol"] - u["grow"], -jnp.inf))
        u["pw"] = -jnp.where(strict, u["aq"][:r] * decay, 0.0)
        u["attn"] = jnp.where(incl, u["aq"][r:] * decay, 0.0)
        u["x"] = jnp.concatenate([u["vb"], u["kb"] * u["egcol"]], axis=1)
    span = 1
    while span < c:
        span *= 2
        for u in units:
            if span < c:
                both = _dot(u["pw"], jnp.concatenate([u["pw"], u["x"]], axis=1))
                u["pw"] = both[:, :r]
                u["x"] = u["x"] + both[:, r:]
            else:
                u["x"] = u["x"] + _dot(u["pw"], u["x"])
    for u in units:
        u_val, w_k = u["x"][:, :GDN_DV], u["x"][:, GDN_DV:]
        u["s_old"] = s_scr[u["rows"], :]
        ws = _dot(jnp.concatenate([block_layout(w_k), block_layout(u["qs"] * u["egcol"])], axis=0), u["s_old"])
        u["v_new"] = u_val - ws[:r]
        u["o"] = ws[r:]
    for u in units:
        e_last = seqs[u["s"]]["e_last"]
        o = u["o"] + _dot(u["attn"], u["v_new"])
        e_rows = jnp.concatenate([jnp.broadcast_to(e_last[:, h:h + 1], (GDN_DK, 1)) for h in u["heads"]], axis=0)
        kd_t = block_layout(u["ks"] * u["ekdcol"]).T
        s_scr[u["rows"], :] = u["s_old"] * e_rows + _dot(kd_t, u["v_new"])
        for i, h in enumerate(u["heads"]):
            sl = slice(h * GDN_DK, (h + 1) * GDN_DK)
            oh = o[i * c:(i + 1) * c, :]
            zh = z_ref[u["s"], :, sl].astype(F32)
            yb_ref[u["s"], :, sl] = (((oh * _rms_scale(oh)) * ng) * (zh * jax.nn.sigmoid(zh))).astype(BF16)

    @pl.when(ci == pl.num_programs(1) - 1)
    def _():
        sout_ref[...] = s_scr[...]


def _gdn_stacked(q, k, v, p, gb, norm_g, bsz, t, c, nb):
    d = D_MODEL
    nc = t // c
    grp = min(GDN_HEADS, 256 // c)
    s_rows = nb * GDN_HEADS * GDN_DK

    def tok(col=0):
        return pl.BlockSpec((nb, c, d), lambda b, i, col=col: (b, i, col))

    yb, s_new = pl.pallas_call(
        functools.partial(_gdn_stacked_kernel, c=c, grp=grp, nb=nb),
        grid=(bsz // nb, nc),
        in_specs=[tok(), tok(), tok(), tok(COL_GZ),
                  pl.BlockSpec((nb, c, SMALL_COLS), lambda b, i: (b, i, 0)),
                  pl.BlockSpec((1, GDN_DV), lambda b, i: (0, 0))],
        out_specs=[tok(), pl.BlockSpec((s_rows, GDN_DV), lambda b, i: (b, 0))],
        out_shape=[jax.ShapeDtypeStruct((bsz, t, d), BF16),
                   jax.ShapeDtypeStruct((bsz * GDN_HEADS * GDN_DK, GDN_DV), F32)],
        scratch_shapes=[pltpu.VMEM((s_rows, GDN_DV), F32)],
        compiler_params=_params(("parallel", "arbitrary")),
        name="gdn_scan_prompt",
    )(*[a.reshape(bsz, t, -1) for a in (q, k, v, p, gb)], norm_g.reshape(1, GDN_DV))
    return yb.reshape(bsz * t, d), s_new.reshape(bsz, GDN_HEADS, GDN_DK, GDN_DV)


def _gdn_kernel(*refs, c, nb, has_s0):
    if has_s0:
        q_ref, k_ref, v_ref, z_ref, gb_ref, ng_ref, s0_ref, yb_ref, sout_ref, s_scr = refs
    else:
        q_ref, k_ref, v_ref, z_ref, gb_ref, ng_ref, yb_ref, sout_ref, s_scr = refs
    ci = pl.program_id(1)

    @pl.when(ci == 0)
    def _():
        if has_s0:
            s_scr[...] = s0_ref[...]
        else:
            s_scr[...] = jnp.zeros_like(s_scr)

    ri = lax.broadcasted_iota(jnp.int32, (c, c), 0)
    cj = lax.broadcasted_iota(jnp.int32, (c, c), 1)
    incl = ri >= cj
    strict = ri > cj
    ng = ng_ref[...]
    row = lax.broadcasted_iota(jnp.int32, (c, SMALL_COLS), 0)
    chains = [(s, h) for s in range(nb) for h in range(GDN_HEADS)]

    seq = []
    for s in range(nb):
        gb = gb_ref[s]
        gc = gb
        sh = 1
        while sh < c:
            gc = gc + jnp.where(row >= sh, pltpu.roll(gc, sh, axis=0), 0.0)
            sh *= 2
        g_last = gc[c - 1:c, :]
        seq.append(dict(gb=gb, gc=gc, gct=gc.T, e_g=jnp.exp(gc), e_kd=jnp.exp(g_last - gc), e_last=jnp.exp(g_last)))

    st = []
    for s, h in chains:
        sl = slice(h * GDN_DK, (h + 1) * GDN_DK)
        sq = seq[s]
        qh, kh, vh = q_ref[s, :, sl], k_ref[s, :, sl], v_ref[s, :, sl]
        beta = sq["gb"][:, GDN_HEADS + h:GDN_HEADS + h + 1]
        kb = kh * beta
        aq = _dot_nt(jnp.concatenate([kb, qh], axis=0), kh)
        st.append(dict(sl=sl, qh=qh, kh=kh, kb=kb, vb=vh * beta, aq=aq))

    for (s, h), e in zip(chains, st):
        sq = seq[s]
        decay = jnp.exp(jnp.where(incl, sq["gc"][:, h:h + 1] - sq["gct"][h:h + 1, :], -jnp.inf))
        e["neg"] = -jnp.where(strict, e["aq"][:c] * decay, 0.0)
        e["attn"] = jnp.where(incl, e["aq"][c:] * decay, 0.0)
        e["x"] = jnp.concatenate([e["vb"], e["kb"] * sq["e_g"][:, h:h + 1]], axis=1)
        e["pw"] = e["neg"]

    span = 1
    while span < c:
        span *= 2
        for e in st:
            if span < c:
                both = _dot(e["pw"], jnp.concatenate([e["pw"], e["x"]], axis=1))
                e["pw"] = both[:, :c]
                e["x"] = e["x"] + both[:, c:]
            else:
                e["x"] = e["x"] + _dot(e["pw"], e["x"])

    for (s, h), e in zip(chains, st):
        sq = seq[s]
        u_val, w_k = e["x"][:, :GDN_DV], e["x"][:, GDN_DV:]
        q_dec = e["qh"] * sq["e_g"][:, h:h + 1]
        e["s_old"] = s_scr[s, h]
        ws = _dot(jnp.concatenate([w_k, q_dec], axis=0), e["s_old"])
        e["v_new"] = u_val - ws[:c]
        e["o_inter"] = ws[c:]

    for (s, h), e in zip(chains, st):
        sq = seq[s]
        o = e["o_inter"] + _dot(e["attn"], e["v_new"])
        k_dec = e["kh"] * sq["e_kd"][:, h:h + 1]
        s_scr[s, h] = e["s_old"] * sq["e_last"][:, h:h + 1] + _dot(k_dec.T, e["v_new"])
        zh = z_ref[s, :, e["sl"]].astype(F32)
        yb_ref[s, :, e["sl"]] = (((o * _rms_scale(o)) * ng) * (zh * jax.nn.sigmoid(zh))).astype(BF16)

    @pl.when(ci == pl.num_programs(1) - 1)
    def _():
        sout_ref[...] = s_scr[...]


def _gdn(q, k, v, p, gb, norm_g, bsz, t, c, nb, s0=None):
    d = D_MODEL
    nc = t // c
    has_s0 = s0 is not None

    def tok(col=0):
        return pl.BlockSpec((nb, c, d), lambda b, i, col=col: (b, i, col))

    in_specs = [tok(), tok(), tok(), tok(COL_GZ),
                pl.BlockSpec((nb, c, SMALL_COLS), lambda b, i: (b, i, 0)),
                pl.BlockSpec((1, GDN_DV), lambda b, i: (0, 0))]
    args = [a.reshape(bsz, t, -1) for a in (q, k, v, p, gb)] + [norm_g.reshape(1, GDN_DV)]
    s_spec = pl.BlockSpec((nb, GDN_HEADS, GDN_DK, GDN_DV), lambda b, i: (b, 0, 0, 0))
    if has_s0:
        in_specs.append(s_spec)
        args.append(s0)
    yb, s_new = pl.pallas_call(
        functools.partial(_gdn_kernel, c=c, nb=nb, has_s0=has_s0),
        grid=(bsz // nb, nc),
        in_specs=in_specs,
        out_specs=[tok(), s_spec],
        out_shape=[jax.ShapeDtypeStruct((bsz, t, d), BF16),
                   jax.ShapeDtypeStruct((bsz, GDN_HEADS, GDN_DK, GDN_DV), F32)],
        scratch_shapes=[pltpu.VMEM((nb, GDN_HEADS, GDN_DK, GDN_DV), F32)],
        compiler_params=_params(("parallel", "arbitrary")),
        name="gdn_scan",
    )(*args)
    return yb.reshape(bsz * t, d), s_new


def _mem_attn_kernel(q_ref, k_ref, v_ref, o_ref, *, nb, tq):
    scale = MEM_HEAD_DIM ** -0.5
    pairs = [(s, h) for s in range(nb) for h in range(MEM_HEADS)]
    span = lambda s, h: (slice(s * tq, (s + 1) * tq), slice(h * MEM_HEAD_DIM, (h + 1) * MEM_HEAD_DIM))
    scores = [_dot_nt(q_ref[span(s, h)], k_ref[s, :, span(s, h)[1]]) * scale for s, h in pairs]
    probs = []
    for sc in scores:
        e = jnp.exp(sc - jnp.max(sc, axis=-1, keepdims=True))
        probs.append(e / jnp.sum(e, axis=-1, keepdims=True))
    for (s, h), prob in zip(pairs, probs):
        o_ref[span(s, h)] = _dot(prob, v_ref[s, :, span(s, h)[1]]).astype(BF16)


def _mem_attn(p, mk, mv, bsz, t, tq, nb):
    d = D_MODEL
    m = mk.shape[1]
    nq = t // tq
    assert nb == 1 or nq == 1
    kv_spec = pl.BlockSpec((nb, m, d), lambda b, i: (b, 0, 0))
    return pl.pallas_call(
        functools.partial(_mem_attn_kernel, nb=nb, tq=tq),
        grid=(bsz // nb, nq),
        in_specs=[pl.BlockSpec((nb * tq, d), lambda b, i: (b * nq + i, COL_MQ)), kv_spec, kv_spec],
        out_specs=pl.BlockSpec((nb * tq, d), lambda b, i: (b * nq + i, 0)),
        out_shape=jax.ShapeDtypeStruct((bsz * t, d), BF16),
        compiler_params=_params(("parallel", "arbitrary")),
        name="mem_attn",
    )(p, mk, mv)


def _mem_attn_cache_kernel(q_ref, k_ref, v_ref, o_ref, *, nb, tq):
    scale = MEM_HEAD_DIM ** -0.5
    m = k_ref.shape[1]
    shape = (MEM_HEADS * tq, m * MEM_HEADS)
    own = (lax.broadcasted_iota(jnp.int32, shape, 1) % MEM_HEADS) == (lax.broadcasted_iota(jnp.int32, shape, 0) // tq)
    qs, ks, vs = [], [], []
    for s in range(nb):
        rows = slice(s * tq, (s + 1) * tq)
        qs.append(jnp.concatenate([q_ref[rows, h * MEM_HEAD_DIM:(h + 1) * MEM_HEAD_DIM] for h in range(MEM_HEADS)],
                                  axis=0))
        ks.append(k_ref[s].reshape(m * MEM_HEADS, MEM_HEAD_DIM))
        vs.append(v_ref[s].reshape(m * MEM_HEADS, MEM_HEAD_DIM))
    scores = [jnp.where(own, _dot_nt(q, k) * scale, -jnp.inf) for q, k in zip(qs, ks)]
    probs = []
    for sc in scores:
        e = jnp.exp(sc - jnp.max(sc, axis=-1, keepdims=True))
        probs.append(e / jnp.sum(e, axis=-1, keepdims=True))
    for s, (prob, v) in enumerate(zip(probs, vs)):
        o = _dot(prob, v)
        o_ref[s * tq:(s + 1) * tq, :] = jnp.concatenate(
            [o[h * tq:(h + 1) * tq] for h in range(MEM_HEADS)], axis=1).astype(BF16)


def _mem_attn_cache(p, mk, mv, bsz, t, nb):
    d = D_MODEL
    kv_spec = pl.BlockSpec((nb,) + mk.shape[1:], lambda b: (b, 0, 0, 0))
    return pl.pallas_call(
        functools.partial(_mem_attn_cache_kernel, nb=nb, tq=t),
        grid=(bsz // nb,),
        in_specs=[pl.BlockSpec((nb * t, d), lambda b: (b, COL_MQ)), kv_spec, kv_spec],
        out_specs=pl.BlockSpec((nb * t, d), lambda b: (b, 0)),
        out_shape=jax.ShapeDtypeStruct((bsz * t, d), BF16),
        compiler_params=_params(("parallel",)),
        name="mem_attn_cache",
    )(p, mk, mv)


def _merge_kernel(x_ref, ya_ref, yb_ref, ym_ref, g0_ref, g1_ref, g2_ref, wc_ref, wg_ref, wm_ref, wo_ref,
                  ng_ref, x1_ref, h2_ref):
    d = functools.partial(jnp.dot, preferred_element_type=F32)
    merged = (jax.nn.sigmoid(_f32(g0_ref)) * d(ya_ref[...], wc_ref[...])
              + jax.nn.sigmoid(_f32(g1_ref)) * d(yb_ref[...], wg_ref[...])
              + jax.nn.sigmoid(_f32(g2_ref)) * d(ym_ref[...], wm_ref[...]))
    x1 = x_ref[...] + d(merged.astype(BF16), wo_ref[...])
    x1_ref[...] = x1
    h2_ref[...] = ((x1 * _rms_scale(x1)) * ng_ref[...]).astype(BF16)


def _merge(x, ya, yb, ym, p, wc, wg, wm, wo, norm_g, tm):
    n, d = x.shape
    tok = pl.BlockSpec((tm, d), lambda i: (i, 0))
    wsp = pl.BlockSpec((d, d), lambda i: (0, 0))

    def gate(j):
        return pl.BlockSpec((tm, d), lambda i, j=j: (i, COL_BG + j))

    return pl.pallas_call(
        _merge_kernel,
        grid=(n // tm,),
        in_specs=[tok, tok, tok, tok, gate(0), gate(1), gate(2), wsp, wsp, wsp, wsp,
                  pl.BlockSpec((1, d), lambda i: (0, 0))],
        out_specs=[tok, tok],
        out_shape=[jax.ShapeDtypeStruct((n, d), F32), jax.ShapeDtypeStruct((n, d), BF16)],
        compiler_params=_params(("parallel",)),
        name="merge",
    )(x, ya, yb, ym, p, p, p, wc, wg, wm, wo, norm_g.reshape(1, d))


_CAND_CELLS = tuple((a, b) for a in range(PEER_TOPK) for b in range(PEER_TOPK) if (a + 1) * (b + 1) <= PEER_TOPK)

TILE_PAD = 4 * SUBLANES
SHIFT_S1, SHIFT_E1 = 0, SUBLANES


def _put_tile(ref, h, ns, shift, val, pad=True):
    if pad and shift:
        ref[h, ns, :shift, :] = jnp.zeros((shift, LANES), F32)
    ref[h, ns, shift:shift + PEER_N_KEYS, :] = val
    if pad and shift < TILE_PAD:
        ref[h, ns, shift + PEER_N_KEYS:, :] = jnp.zeros((TILE_PAD - shift, LANES), F32)


def _top_values(x, count):
    vals = []
    cur = x
    for r in range(count):
        m = jnp.max(cur, axis=0, keepdims=True)
        vals.append(m)
        if r + 1 < count:
            cur = jnp.where(cur == m, -jnp.inf, cur)
    return vals


def _top_ranked(x, count):
    n_rows = x.shape[0]
    iota = lax.broadcasted_iota(jnp.int32, x.shape, 0).astype(F32)
    vals = []
    cur = x
    rank = jnp.full(x.shape, float(count), F32)
    for r in range(count):
        m = jnp.max(cur, axis=0, keepdims=True)
        first = jnp.min(jnp.where(cur == m, iota, float(n_rows)), axis=0, keepdims=True)
        one = iota == first
        rank = jnp.where(one, float(r), rank)
        cur = jnp.where(one, -jnp.inf, cur)
        vals.append(m)
    return vals, rank


def _count_rows(mask):
    return jnp.sum(jnp.where(mask, 1.0, 0.0), axis=0, keepdims=True)


def _route_kernel(h_ref, wq_ref, keys_ref, cut_ref, e0_ref, s1_ref, e1_ref, q_scr, tie_ref):
    tn = h_ref.shape[0]
    nsb = tn // LANES
    q = jnp.dot(h_ref[...], wq_ref[...], preferred_element_type=F32).astype(BF16)

    def store(h, cut, e0, s1, e1):
        for ns in range(nsb):
            cols = slice(ns * LANES, (ns + 1) * LANES)
            cut_ref[h, ns] = cut[:, cols]
            e0_ref[h, ns] = e0[:, cols]
            _put_tile(s1_ref, h, ns, SHIFT_S1, s1[:, cols])
            _put_tile(e1_ref, h, ns, SHIFT_E1, e1[:, cols])

    for h in range(PEER_HEADS):
        sc, top = [], []
        tied = jnp.zeros((1, tn), F32)
        for part in range(2):
            col = (2 * h + part) * PEER_D_HALF
            q_scr[2 * h + part] = q[:, col:col + PEER_D_HALF]
            s_t = lax.dot_general(keys_ref[part, h], q[:, col:col + PEER_D_HALF],
                                  (((1,), (1,)), ((), ())), preferred_element_type=F32)
            sc.append(s_t)
            top.append(_top_values(s_t, PEER_TOPK))
            tied = jnp.maximum(tied, jnp.abs(_count_rows(s_t >= top[part][-1]) - PEER_TOPK))
        cand = jnp.concatenate([top[0][a] + top[1][b] for a, b in _CAND_CELLS], axis=0)
        best = _top_values(cand, PEER_TOPK)
        c_max, thr = best[0], best[-1]
        z = jnp.exp(best[0] - c_max)
        for r in range(1, PEER_TOPK):
            z = z + jnp.exp(best[r] - c_max)
        top1 = jnp.concatenate(top[1], axis=0)
        n_sel = sum(_count_rows(top[0][a] + top1 >= thr) for a in range(PEER_TOPK))
        tied = jnp.maximum(tied, jnp.abs(n_sel - PEER_TOPK))
        cut = jnp.full(sc[0].shape, jnp.inf, F32)
        for b in range(PEER_TOPK):
            cut = jnp.where(sc[0] + top[1][b] >= thr, top[1][b], cut)
        store(h, cut, jnp.exp(sc[0] - top[0][0]) / z, sc[1], jnp.exp(sc[1] - top[1][0]))
        for ns in range(nsb):
            tie_ref[h * nsb + ns] = jnp.max(tied[:, ns * LANES:(ns + 1) * LANES])

    def exact_block(idx, carry):
        @pl.when(tie_ref[idx] > 0.0)
        def _():
            h = idx // nsb
            ns = idx % nsb
            rows = pl.ds(pl.multiple_of(ns * LANES, LANES), LANES)
            sc, top, rank = [], [], []
            for part in range(2):
                s_t = lax.dot_general(keys_ref[part, h], q_scr[2 * h + part, rows, :],
                                      (((1,), (1,)), ((), ())), preferred_element_type=F32)
                vals, rk = _top_ranked(s_t, PEER_TOPK)
                sc.append(s_t)
                top.append(vals)
                rank.append(rk)
            top1 = jnp.concatenate(top[1], axis=0)
            cand = jnp.concatenate([top[0][a] + top1 for a in range(PEER_TOPK)], axis=0)
            best, pos = _top_ranked(cand, PEER_TOPK)
            z = jnp.exp(best[0] - best[0])
            for r in range(1, PEER_TOPK):
                z = z + jnp.exp(best[r] - best[0])
            cnt = jnp.zeros_like(sc[0])
            for a in range(PEER_TOPK):
                cnt_a = _count_rows(pos[a * PEER_TOPK:(a + 1) * PEER_TOPK] < PEER_TOPK)
                cnt = jnp.where(rank[0] == float(a), cnt_a, cnt)
            cut_ref[h, ns] = jnp.where(cnt > 0.0, 1.0 - cnt, jnp.inf)
            e0_ref[h, ns] = jnp.exp(sc[0] - top[0][0]) / z
            _put_tile(s1_ref, h, ns, SHIFT_S1, -rank[1], pad=False)
            _put_tile(e1_ref, h, ns, SHIFT_E1, jnp.exp(sc[1] - top[1][0]), pad=False)
        return carry

    lax.fori_loop(0, PEER_HEADS * nsb, exact_block, 0)


def _route(h2, wq, keys, tn):
    n, d = h2.shape
    nk = PEER_N_KEYS
    tiled = pl.BlockSpec((PEER_HEADS, tn // LANES, nk + TILE_PAD, LANES), lambda i: (0, i, 0, 0))
    tiled_shape = jax.ShapeDtypeStruct((PEER_HEADS, n // LANES, nk + TILE_PAD, LANES), F32)
    rowed = pl.BlockSpec((PEER_HEADS, tn // LANES, nk, LANES), lambda i: (0, i, 0, 0))
    rowed_shape = jax.ShapeDtypeStruct((PEER_HEADS, n // LANES, nk, LANES), F32)
    return pl.pallas_call(
        _route_kernel,
        grid=(n // tn,),
        in_specs=[pl.BlockSpec((tn, d), lambda i: (i, 0)),
                  pl.BlockSpec(wq.shape, lambda i: (0, 0)),
                  pl.BlockSpec(keys.shape, lambda i: (0, 0, 0, 0))],
        out_specs=[rowed, rowed, tiled, tiled],
        out_shape=[rowed_shape, rowed_shape, tiled_shape, tiled_shape],
        scratch_shapes=[pltpu.VMEM((2 * PEER_HEADS, tn, PEER_D_HALF), BF16),
                        pltpu.SMEM((PEER_HEADS * (tn // LANES),), F32)],
        compiler_params=_params(("parallel",)),
        name="peer_route",
    )(h2, wq, keys)


def _gelu(x):
    return 0.5 * x * (1.0 + lax.erf(x * (2.0 ** -0.5)))


PEER_TILE = 2048
GATE_HALF = 32
GATE_ROWS = 8


def _peer_kernel(h_ref, x1_ref, u_ref, vt_ref, cut_ref, e0_ref, s1_ref, e1_ref, ng_ref,
                 y_ref, acc_ref, act_ref, coef_ref, *, te):
    ei = pl.program_id(1)
    tn = h_ref.shape[0]
    groups = te // PEER_N_KEYS

    @pl.when(ei == 0)
    def _():
        acc_ref[...] = jnp.zeros_like(acc_ref)

    nsb = tn // LANES
    act = _gelu(lax.dot_general(u_ref[...], h_ref[...], (((1,), (1,)), ((), ())),
                                preferred_element_type=F32))
    for ns in range(nsb):
        act_ref[ns] = act[:, ns * LANES:(ns + 1) * LANES]

    def gate_block(ns):
        for jh in range(PEER_N_KEYS // GATE_HALF):
            jrows = slice(jh * GATE_HALF, (jh + 1) * GATE_HALF)
            for g0 in range(0, groups, GATE_ROWS):
                gates = [None] * GATE_ROWS
                for h in range(PEER_HEADS):
                    s1_t = s1_ref[h, ns, SHIFT_S1 + jh * GATE_HALF:SHIFT_S1 + (jh + 1) * GATE_HALF, :]
                    e1_t = e1_ref[h, ns, SHIFT_E1 + jh * GATE_HALF:SHIFT_E1 + (jh + 1) * GATE_HALF, :]
                    for k in range(GATE_ROWS):
                        sel = s1_t >= cut_ref[h, ns, g0 + k:g0 + k + 1, :]
                        term = jnp.where(sel, e1_t, 0.0) * e0_ref[h, ns, g0 + k:g0 + k + 1, :]
                        gates[k] = term if gates[k] is None else gates[k] + term
                for k in range(GATE_ROWS):
                    r0 = (g0 + k) * PEER_N_KEYS + jh * GATE_HALF
                    coef_ref[ns, r0:r0 + GATE_HALF, :] = (gates[k] * act_ref[ns, r0:r0 + GATE_HALF, :]).astype(BF16)

    def lane_block(ns, carry):
        gate_block(ns)
        return carry

    def mix_values(lo, hi):
        coef = jnp.concatenate([coef_ref[ns] for ns in range(lo, hi)], axis=1)
        acc_ref[:, lo * LANES:hi * LANES] += jnp.dot(vt_ref[0], coef, preferred_element_type=F32)

    lax.fori_loop(0, nsb, lane_block, 0)
    mix_values(0, nsb)

    @pl.when(ei == pl.num_programs(1) - 1)
    def _():
        x2 = x1_ref[...] + acc_ref[...].T
        y_ref[...] = (x2 * _rms_scale(x2)) * ng_ref[...]


def _peer(h2, x1, u_bf, vt_bf, cut, e0, s1, e1, norm_g, tn):
    n, d = h2.shape
    ne = u_bf.shape[0]
    te = vt_bf.shape[2]
    nk = PEER_N_KEYS
    tok = pl.BlockSpec((tn, d), lambda i, e: (i, 0))
    tiled = pl.BlockSpec((PEER_HEADS, tn // LANES, nk + TILE_PAD, LANES), lambda i, e: (0, i, 0, 0))
    rowed = pl.BlockSpec((PEER_HEADS, tn // LANES, te // nk, LANES), lambda i, e: (0, i, e, 0))
    return pl.pallas_call(
        functools.partial(_peer_kernel, te=te),
        grid=(n // tn, ne // te),
        in_specs=[tok, tok,
                  pl.BlockSpec((te, d), lambda i, e: (e, 0)),
                  pl.BlockSpec((1, d, te), lambda i, e: (e, 0, 0)),
                  rowed, rowed, tiled, tiled,
                  pl.BlockSpec((1, d), lambda i, e: (0, 0))],
        out_specs=tok,
        out_shape=jax.ShapeDtypeStruct((n, d), F32),
        scratch_shapes=[pltpu.VMEM((d, tn), F32), pltpu.VMEM((tn // LANES, te, LANES), F32),
                        pltpu.VMEM((tn // LANES, te, LANES), BF16)],
        compiler_params=_params(("parallel", "arbitrary")),
        name="peer_mix",
    )(h2, x1, u_bf, vt_bf, cut, e0, s1, e1, norm_g.reshape(1, d))


def _tile(n, pref):
    return math.gcd(n, pref)


def _layer(x, mk, mv, bsz, t, weights, conv_state=None, gdn_conv_state=None, gdn_s0=None):
    (norm_mix_g, w_main, w_ab, conv_w, gdn_conv_w, a_log, dt_bias, gdn_norm_g, wc, wg, wm, wo, norm_ffn_g,
     wq, keys, u_bf, vt_bf, final_norm_g) = weights
    n = bsz * t
    sample = conv_state is not None
    p, pab = _inproj(x, norm_mix_g, w_main, w_ab, _tile(n, 2048))
    rows = _tile(n, 256) if sample else _tile(t, 512)
    ya, q, k, v, gb, u_tail = _prep(p, pab, conv_w, gdn_conv_w, a_log, dt_bias, rows, max(t // rows, 1),
                                    conv_state, gdn_conv_state)
    chunk = _tile(t, 64)
    if sample:
        yb, s_new = _gdn(q, k, v, p, gb, gdn_norm_g, bsz, t, chunk, 4, gdn_s0)
    else:
        yb, s_new = _gdn_stacked(q, k, v, p, gb, gdn_norm_g, bsz, t, chunk, _tile(bsz, 4))
    if mk.ndim == 4:
        ym = _mem_attn_cache(p, mk, mv, bsz, t, _tile(bsz, 4))
    else:
        ym = _mem_attn(p, mk, mv, bsz, t, _tile(t, 512), 1)
    x1, h2 = _merge(x, ya, yb, ym, p, wc, wg, wm, wo, norm_ffn_g, _tile(n, 512))
    if sample:
        keep = lambda a: a.reshape(bsz, t, -1)[:, t // 2:, :].reshape(bsz * t // 2, -1)
        x1, h2 = keep(x1), keep(h2)
    tn = _tile(x1.shape[0], 512)
    cut, e0, s1, e1 = _route(h2, wq, keys, tn)
    y = _peer(h2, x1, u_bf, vt_bf, cut, e0, s1, e1, final_norm_g, tn)
    return y, u_tail, p, s_new


def kernel(x_prompt, x_sample, mem_prompt, cache_mem_k, cache_mem_v, state_conv, state_gdn_conv, state_gdn, norm_mix_g, w_in, conv_w, gdn_conv_w, gdn_a_log, gdn_dt_bias, gdn_norm_g, mem_norm_g, w_mem_kv, w_br_conv, w_br_gdn, w_br_mem, w_out, norm_ffn_g, peer_w_q, peer_sub_keys, peer_u, peer_v, final_norm_g):
    depth = w_in.shape[0]
    assert depth == 1, "single-layer step"
    d = D_MODEL
    bp, tp, _ = x_prompt.shape
    bs, ts, _ = x_sample.shape
    assert ts == SAMPLE_PAD_T // 2
    m_tok = mem_prompt.shape[1]
    mq = MEM_HEADS * MEM_HEAD_DIM
    l = 0

    w = w_in[l]
    ga0 = 7 * d
    w_main = jnp.concatenate([w[:, :ga0], w[:, ga0 + 2 * GDN_HEADS:]], axis=1).astype(BF16)
    w_ab = jnp.pad(w[:, ga0:ga0 + 2 * GDN_HEADS], ((0, 0), (0, SMALL_COLS - 2 * GDN_HEADS))).astype(BF16)
    weights = (norm_mix_g[l], w_main, w_ab, conv_w[l], gdn_conv_w[l], gdn_a_log[l], gdn_dt_bias[l], gdn_norm_g[l],
               w_br_conv[l].astype(BF16), w_br_gdn[l].astype(BF16), w_br_mem[l].astype(BF16), w_out[l].astype(BF16),
               norm_ffn_g[l], peer_w_q[l].astype(BF16), peer_sub_keys[l].astype(BF16),
               peer_u[l].astype(BF16),
               peer_v[l].astype(BF16).reshape(-1, PEER_TILE, d).transpose(0, 2, 1), final_norm_g)

    kv_shape = (bp, m_tok, mq)
    mk_p, mv_p = _mem_kv(mem_prompt.reshape(bp * m_tok, d), mem_norm_g[l], w_mem_kv[l].astype(BF16),
                         _tile(bp * m_tok, 256))
    y_p, u_tail_p, p_p, s_p = _layer(x_prompt.reshape(bp * tp, d), mk_p.reshape(kv_shape), mv_p.reshape(kv_shape),
                                     bp, tp, weights)
    rows_p = _tile(tp, 512)
    cb_p = u_tail_p.reshape(bp, tp // rows_p, SUBLANES, d)[:, -1, SUBLANES - (CONV_K - 1):, :]
    gb_p = p_p.reshape(bp, tp, -1)[:, tp - (GDN_CONV_K - 1):, COL_GQ * d:(COL_GV + 1) * d].astype(F32)

    pad_t = SAMPLE_PAD_T
    lead = pad_t - ts
    x_s = jnp.pad(x_sample, ((0, 0), (lead, 0), (0, 0))).reshape(bs * pad_t, d)
    cst = jnp.pad(state_conv[l], ((0, 0), (lead - (CONV_K - 1), ts), (0, 0))).reshape(bs * pad_t, d)
    gst = jnp.pad(state_gdn_conv[l], ((0, 0), (lead - (GDN_CONV_K - 1), ts), (0, 0))).reshape(bs * pad_t, 3 * d)
    y_s, u_s, p_s, s_s = _layer(x_s, cache_mem_k.reshape(cache_mem_k.shape[1:]), cache_mem_v.reshape(cache_mem_v.shape[1:]),
                                bs, pad_t, weights, cst, gst, state_gdn.reshape(bs, GDN_HEADS, GDN_DK, GDN_DV))
    y_s = y_s.reshape(bs, ts, d)
    cb_s = u_s.reshape(bs, pad_t, d)[:, pad_t - (CONV_K - 1):, :]
    gb_s = p_s.reshape(bs, pad_t, -1)[:, pad_t - (GDN_CONV_K - 1):, COL_GQ * d:(COL_GV + 1) * d].astype(F32)

    mshape = (depth, bp, m_tok, MEM_HEADS, MEM_HEAD_DIM)
    return (y_p.reshape(bp, tp, d), y_s,
            mk_p.reshape(mshape), mv_p.reshape(mshape),
            cb_p[None], gb_p[None], s_p[None],
            cb_s[None], gb_s[None], s_s[None])
```

```python
import functools
import math

import jax
import jax.numpy as jnp
from jax import lax
from jax.experimental import pallas as pl
from jax.experimental.pallas import tpu as pltpu

F32 = jnp.float32
BF16 = jnp.bfloat16

EPS = 1e-6
D_MODEL = 1024
CONV_K = 3
GDN_HEADS = 8
GDN_DK = 128
GDN_DV = 128
GDN_CONV_K = 4
MEM_HEADS = 4
MEM_HEAD_DIM = 256
PEER_HEADS = 8
PEER_N_KEYS = 128
PEER_D_HALF = 128
PEER_TOPK = 16
N_BRANCHES = 3

COL_AU, COL_AB, COL_AC, COL_GQ, COL_GK, COL_GV, COL_GZ, COL_MQ, COL_BG = 0, 1, 2, 3, 4, 5, 6, 7, 8
N_MAIN_BLOCKS = 11
SMALL_COLS = 128
SUBLANES = 8
LANES = 128
SAMPLE_PAD_T = 8
VMEM_LIMIT = 56 * 1024 * 1024


def _params(sem):
    return pltpu.CompilerParams(dimension_semantics=sem, vmem_limit_bytes=VMEM_LIMIT)


def _dot(a, b):
    return jnp.dot(a.astype(BF16), b.astype(BF16), preferred_element_type=F32)


def _dot_nt(a, b):
    return lax.dot_general(a.astype(BF16), b.astype(BF16), (((1,), (1,)), ((), ())),
                           preferred_element_type=F32)


def _rms_scale(x):
    return lax.rsqrt(jnp.mean(x * x, axis=-1, keepdims=True) + EPS)


def _inproj_kernel(x_ref, g_ref, w_ref, wab_ref, p_ref, pab_ref, h_ref):
    @pl.when(pl.program_id(1) == 0)
    def _():
        x = x_ref[...]
        h = ((x * _rms_scale(x)) * g_ref[...]).astype(BF16)
        h_ref[...] = h
        pab_ref[...] = jnp.dot(h, wab_ref[...], preferred_element_type=F32)

    p_ref[...] = jnp.dot(h_ref[...], w_ref[...], preferred_element_type=F32).astype(p_ref.dtype)


def _inproj(x, g, w_main, w_ab, tm):
    n, d = x.shape
    nblk = w_main.shape[1] // d
    return pl.pallas_call(
        _inproj_kernel,
        grid=(n // tm, nblk),
        in_specs=[
            pl.BlockSpec((tm, d), lambda i, j: (i, 0)),
            pl.BlockSpec((1, d), lambda i, j: (0, 0)),
            pl.BlockSpec((d, d), lambda i, j: (0, j)),
            pl.BlockSpec((d, SMALL_COLS), lambda i, j: (0, 0)),
        ],
        out_specs=[
            pl.BlockSpec((tm, d), lambda i, j: (i, j)),
            pl.BlockSpec((tm, SMALL_COLS), lambda i, j: (i, 0)),
        ],
        out_shape=[
            jax.ShapeDtypeStruct((n, nblk * d), BF16),
            jax.ShapeDtypeStruct((n, SMALL_COLS), F32),
        ],
        scratch_shapes=[pltpu.VMEM((tm, d), BF16)],
        compiler_params=_params(("parallel", "arbitrary")),
        name="inproj",
    )(x, g.reshape(1, d), w_main, w_ab)


def _mem_kv_kernel(x_ref, g_ref, w_ref, k_ref, v_ref):
    x = x_ref[...]
    h = ((x * _rms_scale(x)) * g_ref[...]).astype(BF16)
    kv = jnp.dot(h, w_ref[...], preferred_element_type=F32)
    half = kv.shape[1] // 2
    k_ref[...] = kv[:, :half]
    v_ref[...] = kv[:, half:]


def _mem_kv(x, g, w, tm):
    n, d = x.shape
    c = w.shape[1]
    out = jax.ShapeDtypeStruct((n, c // 2), F32)
    ospec = pl.BlockSpec((tm, c // 2), lambda i: (i, 0))
    return pl.pallas_call(
        _mem_kv_kernel,
        grid=(n // tm,),
        in_specs=[
            pl.BlockSpec((tm, d), lambda i: (i, 0)),
            pl.BlockSpec((1, d), lambda i: (0, 0)),
            pl.BlockSpec((d, c), lambda i: (0, 0)),
        ],
        out_specs=[ospec, ospec],
        out_shape=[out, out],
        compiler_params=_params(("parallel",)),
        name="mem_kv",
    )(x, g.reshape(1, d), w)


def _causal_conv(prev, cur, w_ref, k):
    full = jnp.concatenate([prev, cur], axis=0)
    y = cur * w_ref[k - 1:k, :]
    for s in range(1, k):
        y = y + pltpu.roll(full, s, axis=0)[prev.shape[0]:, :] * w_ref[k - 1 - s:k - s, :]
    return y


def _f32(ref):
    return ref[...].astype(F32)


def _prep_kernel(*refs, rows, tiles_per_seq, sample):
    if sample:
        (au, ab, ac, gq, gk, gv, pab, au_p, ac_p, gq_p, gk_p, gv_p, cw, gw, alog, dtb, cst, gst,
         ya_o, q_o, k_o, v_o, gb_o, ut_o) = refs
    else:
        (au, ab, ac, gq, gk, gv, pab, au_p, ac_p, gq_p, gk_p, gv_p, cw, gw, alog, dtb,
         ya_o, q_o, k_o, v_o, gb_o, ut_o) = refs
    d = D_MODEL
    first = (pl.program_id(0) % tiles_per_seq) == 0
    keep = jnp.where(first, 0.0, 1.0).astype(F32)

    u = _f32(ac) * _f32(au)
    u_prev = (_f32(ac_p) * _f32(au_p)) * keep
    if sample:
        real = (lax.broadcasted_iota(jnp.int32, (rows, 1), 0) % SAMPLE_PAD_T) >= (SAMPLE_PAD_T // 2)
        u = jnp.where(real, u, cst[...])
    ya_o[...] = (_f32(ab) * _causal_conv(u_prev, u, cw, CONV_K)).astype(BF16)
    ut_o[...] = u[rows - ut_o.shape[0]:, :]

    outs = (q_o, k_o, v_o)
    for idx, (cur_ref, prev_ref) in enumerate(((gq, gq_p), (gk, gk_p), (gv, gv_p))):
        cur = _f32(cur_ref)
        if sample:
            cur = jnp.where(real, cur, gst[:, idx * d:(idx + 1) * d])
        prev = _f32(prev_ref) * keep
        wslab = gw.at[:, idx * d:(idx + 1) * d]
        y = _causal_conv(prev, cur, wslab, GDN_CONV_K)
        y = y * jax.nn.sigmoid(y)
        if idx < 2:
            scale = GDN_DK ** -0.5 if idx == 0 else 1.0
            for h in range(GDN_HEADS):
                yh = y[:, h * GDN_DK:(h + 1) * GDN_DK]
                r = lax.rsqrt(jnp.sum(yh * yh, axis=-1, keepdims=True) + EPS)
                outs[idx][:, h * GDN_DK:(h + 1) * GDN_DK] = yh * (r * scale)
        else:
            outs[idx][...] = y

    pa = pab[...]
    lane = lax.broadcasted_iota(jnp.int32, pa.shape, 1)
    z = pa + dtb[...]
    softplus = jnp.maximum(z, 0.0) + jnp.log1p(jnp.exp(-jnp.abs(z)))
    g_log = -jnp.exp(alog[...]) * softplus
    gb = jnp.where(lane < GDN_HEADS, g_log, jnp.where(lane < 2 * GDN_HEADS, jax.nn.sigmoid(pa), 0.0))
    if sample:
        gb = jnp.where(real, gb, 0.0)
    gb_o[...] = gb


def _prep(p, pab, conv_w, gdn_conv_w, a_log, dt_bias, rows, tiles_per_seq, conv_state=None, gdn_state=None):
    n = p.shape[0]
    d = D_MODEL
    sample = conv_state is not None
    nt = n // rows
    prev_rows = 2 * SUBLANES
    rpb = rows // prev_rows

    def cur(c):
        return pl.BlockSpec((rows, d), lambda i, c=c: (i, c))

    def prev(c):
        return pl.BlockSpec((prev_rows, d), lambda i, c=c: (jnp.maximum(i * rpb - 1, 0), c))

    alog128 = jnp.zeros((1, SMALL_COLS), F32).at[0, :GDN_HEADS].set(a_log)
    dtb128 = jnp.zeros((1, SMALL_COLS), F32).at[0, :GDN_HEADS].set(dt_bias)
    in_specs = [cur(COL_AU), cur(COL_AB), cur(COL_AC), cur(COL_GQ), cur(COL_GK), cur(COL_GV),
                pl.BlockSpec((rows, SMALL_COLS), lambda i: (i, 0)),
                prev(COL_AU), prev(COL_AC), prev(COL_GQ), prev(COL_GK), prev(COL_GV),
                pl.BlockSpec((CONV_K, d), lambda i: (0, 0)),
                pl.BlockSpec((GDN_CONV_K, 3 * d), lambda i: (0, 0)),
                pl.BlockSpec((1, SMALL_COLS), lambda i: (0, 0)),
                pl.BlockSpec((1, SMALL_COLS), lambda i: (0, 0))]
    args = [p, p, p, p, p, p, pab, p, p, p, p, p, conv_w, gdn_conv_w, alog128, dtb128]
    if sample:
        in_specs += [pl.BlockSpec((rows, d), lambda i: (i, 0)), pl.BlockSpec((rows, 3 * d), lambda i: (i, 0))]
        args += [conv_state, gdn_state]
    tail = rows if sample else SUBLANES
    out_specs = [pl.BlockSpec((rows, d), lambda i: (i, 0))] * 4 + [
        pl.BlockSpec((rows, SMALL_COLS), lambda i: (i, 0)),
        pl.BlockSpec((tail, d), lambda i: (i, 0))]
    out_shape = [jax.ShapeDtypeStruct((n, d), BF16)] + [jax.ShapeDtypeStruct((n, d), F32)] * 3 + [
        jax.ShapeDtypeStruct((n, SMALL_COLS), F32),
        jax.ShapeDtypeStruct((nt * tail, d), F32)]
    return pl.pallas_call(
        functools.partial(_prep_kernel, rows=rows, tiles_per_seq=tiles_per_seq, sample=sample),
        grid=(nt,),
        in_specs=in_specs,
        out_specs=out_specs,
        out_shape=out_shape,
        compiler_params=_params(("parallel",)),
        name="prep",
    )(*args)


def _chunk_cumsum(g, c):
    row = lax.broadcasted_iota(jnp.int32, g.shape, 0)
    sh = 1
    while sh < c:
        g = g + jnp.where(row >= sh, pltpu.roll(g, sh, axis=0), 0.0)
        sh *= 2
    return g


def _gdn_stacked_kernel(q_ref, k_ref, v_ref, z_ref, gb_ref, ng_ref, yb_ref, sout_ref, s_scr, *, c, grp, nb):
    ci = pl.program_id(1)

    @pl.when(ci == 0)
    def _():
        s_scr[...] = jnp.zeros_like(s_scr)

    r = grp * c
    ng = ng_ref[...]
    seqs = []
    for s in range(nb):
        gb = gb_ref[s]
        gc = _chunk_cumsum(gb, c)
        g_last = gc[c - 1:c, :]
        seqs.append(dict(gb=gb, gc=gc, gct=gc.T, e_g=jnp.exp(gc), e_kd=jnp.exp(g_last - gc), e_last=jnp.exp(g_last)))

    ri = lax.broadcasted_iota(jnp.int32, (r, r), 0)
    cj = lax.broadcasted_iota(jnp.int32, (r, r), 1)
    same = (ri // c) == (cj // c)
    incl = same & (ri >= cj)
    strict = same & (ri > cj)
    head_block = (lax.broadcasted_iota(jnp.int32, (r, grp * GDN_DK), 0) // c
                  == lax.broadcasted_iota(jnp.int32, (r, grp * GDN_DK), 1) // GDN_DK)

    def block_layout(x):
        return jnp.where(head_block, jnp.concatenate([x] * grp, axis=1), 0.0)

    units = []
    for s in range(nb):
        for g0 in range(0, GDN_HEADS, grp):
            heads = list(range(g0, g0 + grp))
            sq = seqs[s]

            def stack(ref, s=s, heads=heads):
                return jnp.concatenate([ref[s, :, h * GDN_DK:(h + 1) * GDN_DK] for h in heads], axis=0)

            def col(a, off=0, heads=heads):
                return jnp.concatenate([a[:, off + h:off + h + 1] for h in heads], axis=0)

            qs, ks, vs = stack(q_ref), stack(k_ref), stack(v_ref)
            beta = col(sq["gb"], GDN_HEADS)
            kb = ks * beta
            units.append(dict(s=s, heads=heads, qs=qs, ks=ks, kb=kb, vb=vs * beta,
                              gcol=col(sq["gc"]), egcol=col(sq["e_g"]), ekdcol=col(sq["e_kd"]),
                              grow=jnp.concatenate([sq["gct"][h:h + 1, :] for h in heads], axis=1),
                              rows=slice((s * GDN_HEADS + g0) * GDN_DK, (s * GDN_HEADS + g0 + grp) * GDN_DK)))

    for u in units:
        u["aq"] = _dot_nt(jnp.concatenate([u["kb"], u["qs"]], axis=0), u["ks"])
    for u in units:
        decay = jnp.exp(jnp.where(incl, u["gcol"] - u["grow"], -jnp.inf))
        u["pw"] = -jnp.where(strict, u["aq"][:r] * decay, 0.0)
        u["attn"] = jnp.where(incl, u["aq"][r:] * decay, 0.0)
        u["x"] = jnp.concatenate([u["vb"], u["kb"] * u["egcol"]], axis=1)
    span = 1
    while span < c:
        span *= 2
        for u in units:
            if span < c:
                both = _dot(u["pw"], jnp.concatenate([u["pw"], u["x"]], axis=1))
                u["pw"] = both[:, :r]
                u["x"] = u["x"] + both[:, r:]
            else:
                u["x"] = u["x"] + _dot(u["pw"], u["x"])
    for u in units:
        u_val, w_k = u["x"][:, :GDN_DV], u["x"][:, GDN_DV:]
        u["s_old"] = s_scr[u["rows"], :]
        ws = _dot(jnp.concatenate([block_layout(w_k), block_layout(u["qs"] * u["egcol"])], axis=0), u["s_old"])
        u["v_new"] = u_val - ws[:r]
        u["o"] = ws[r:]
    for u in units:
        e_last = seqs[u["s"]]["e_last"]
        o = u["o"] + _dot(u["attn"], u["v_new"])
        e_rows = jnp.concatenate([jnp.broadcast_to(e_last[:, h:h + 1], (GDN_DK, 1)) for h in u["heads"]], axis=0)
        kd_t = block_layout(u["ks"] * u["ekdcol"]).T
        s_scr[u["rows"], :] = u["s_old"] * e_rows + _dot(kd_t, u["v_new"])
        for i, h in enumerate(u["heads"]):
            sl = slice(h * GDN_DK, (h + 1) * GDN_DK)
            oh = o[i * c:(i + 1) * c, :]
            zh = z_ref[u["s"], :, sl].astype(F32)
            yb_ref[u["s"], :, sl] = (((oh * _rms_scale(oh)) * ng) * (zh * jax.nn.sigmoid(zh))).astype(BF16)

    @pl.when(ci == pl.num_programs(1) - 1)
    def _():
        sout_ref[...] = s_scr[...]


def _gdn_stacked(q, k, v, p, gb, norm_g, bsz, t, c, nb):
    d = D_MODEL
    nc = t // c
    grp = min(GDN_HEADS, 256 // c)
    s_rows = nb * GDN_HEADS * GDN_DK

    def tok(col=0):
        return pl.BlockSpec((nb, c, d), lambda b, i, col=col: (b, i, col))

    yb, s_new = pl.pallas_call(
        functools.partial(_gdn_stacked_kernel, c=c, grp=grp, nb=nb),
        grid=(bsz // nb, nc),
        in_specs=[tok(), tok(), tok(), tok(COL_GZ),
                  pl.BlockSpec((nb, c, SMALL_COLS), lambda b, i: (b, i, 0)),
                  pl.BlockSpec((1, GDN_DV), lambda b, i: (0, 0))],
        out_specs=[tok(), pl.BlockSpec((s_rows, GDN_DV), lambda b, i: (b, 0))],
        out_shape=[jax.ShapeDtypeStruct((bsz, t, d), BF16),
                   jax.ShapeDtypeStruct((bsz * GDN_HEADS * GDN_DK, GDN_DV), F32)],
        scratch_shapes=[pltpu.VMEM((s_rows, GDN_DV), F32)],
        compiler_params=_params(("parallel", "arbitrary")),
        name="gdn_scan_prompt",
    )(*[a.reshape(bsz, t, -1) for a in (q, k, v, p, gb)], norm_g.reshape(1, GDN_DV))
    return yb.reshape(bsz * t, d), s_new.reshape(bsz, GDN_HEADS, GDN_DK, GDN_DV)


def _gdn_kernel(*refs, c, nb, has_s0):
    if has_s0:
        q_ref, k_ref, v_ref, z_ref, gb_ref, ng_ref, s0_ref, yb_ref, sout_ref, s_scr = refs
    else:
        q_ref, k_ref, v_ref, z_ref, gb_ref, ng_ref, yb_ref, sout_ref, s_scr = refs
    ci = pl.program_id(1)

    @pl.when(ci == 0)
    def _():
        if has_s0:
            s_scr[...] = s0_ref[...]
        else:
            s_scr[...] = jnp.zeros_like(s_scr)

    ri = lax.broadcasted_iota(jnp.int32, (c, c), 0)
    cj = lax.broadcasted_iota(jnp.int32, (c, c), 1)
    incl = ri >= cj
    strict = ri > cj
    ng = ng_ref[...]
    row = lax.broadcasted_iota(jnp.int32, (c, SMALL_COLS), 0)
    chains = [(s, h) for s in range(nb) for h in range(GDN_HEADS)]

    seq = []
    for s in range(nb):
        gb = gb_ref[s]
        gc = gb
        sh = 1
        while sh < c:
            gc = gc + jnp.where(row >= sh, pltpu.roll(gc, sh, axis=0), 0.0)
            sh *= 2
        g_last = gc[c - 1:c, :]
        seq.append(dict(gb=gb, gc=gc, gct=gc.T, e_g=jnp.exp(gc), e_kd=jnp.exp(g_last - gc), e_last=jnp.exp(g_last)))

    st = []
    for s, h in chains:
        sl = slice(h * GDN_DK, (h + 1) * GDN_DK)
        sq = seq[s]
        qh, kh, vh = q_ref[s, :, sl], k_ref[s, :, sl], v_ref[s, :, sl]
        beta = sq["gb"][:, GDN_HEADS + h:GDN_HEADS + h + 1]
        kb = kh * beta
        aq = _dot_nt(jnp.concatenate([kb, qh], axis=0), kh)
        st.append(dict(sl=sl, qh=qh, kh=kh, kb=kb, vb=vh * beta, aq=aq))

    for (s, h), e in zip(chains, st):
        sq = seq[s]
        decay = jnp.exp(jnp.where(incl, sq["gc"][:, h:h + 1] - sq["gct"][h:h + 1, :], -jnp.inf))
        e["neg"] = -jnp.where(strict, e["aq"][:c] * decay, 0.0)
        e["attn"] = jnp.where(incl, e["aq"][c:] * decay, 0.0)
        e["x"] = jnp.concatenate([e["vb"], e["kb"] * sq["e_g"][:, h:h + 1]], axis=1)
        e["pw"] = e["neg"]

    span = 1
    while span < c:
        span *= 2
        for e in st:
            if span < c:
                both = _dot(e["pw"], jnp.concatenate([e["pw"], e["x"]], axis=1))
                e["pw"] = both[:, :c]
                e["x"] = e["x"] + both[:, c:]
            else:
                e["x"] = e["x"] + _dot(e["pw"], e["x"])

    for (s, h), e in zip(chains, st):
        sq = seq[s]
        u_val, w_k = e["x"][:, :GDN_DV], e["x"][:, GDN_DV:]
        q_dec = e["qh"] * sq["e_g"][:, h:h + 1]
        e["s_old"] = s_scr[s, h]
        ws = _dot(jnp.concatenate([w_k, q_dec], axis=0), e["s_old"])
        e["v_new"] = u_val - ws[:c]
        e["o_inter"] = ws[c:]

    for (s, h), e in zip(chains, st):
        sq = seq[s]
        o = e["o_inter"] + _dot(e["attn"], e["v_new"])
        k_dec = e["kh"] * sq["e_kd"][:, h:h + 1]
        s_scr[s, h] = e["s_old"] * sq["e_last"][:, h:h + 1] + _dot(k_dec.T, e["v_new"])
        zh = z_ref[s, :, e["sl"]].astype(F32)
        yb_ref[s, :, e["sl"]] = (((o * _rms_scale(o)) * ng) * (zh * jax.nn.sigmoid(zh))).astype(BF16)

    @pl.when(ci == pl.num_programs(1) - 1)
    def _():
        sout_ref[...] = s_scr[...]


def _gdn(q, k, v, p, gb, norm_g, bsz, t, c, nb, s0=None):
    d = D_MODEL
    nc = t // c
    has_s0 = s0 is not None

    def tok(col=0):
        return pl.BlockSpec((nb, c, d), lambda b, i, col=col: (b, i, col))

    in_specs = [tok(), tok(), tok(), tok(COL_GZ),
                pl.BlockSpec((nb, c, SMALL_COLS), lambda b, i: (b, i, 0)),
                pl.BlockSpec((1, GDN_DV), lambda b, i: (0, 0))]
    args = [a.reshape(bsz, t, -1) for a in (q, k, v, p, gb)] + [norm_g.reshape(1, GDN_DV)]
    s_spec = pl.BlockSpec((nb, GDN_HEADS, GDN_DK, GDN_DV), lambda b, i: (b, 0, 0, 0))
    if has_s0:
        in_specs.append(s_spec)
        args.append(s0)
    yb, s_new = pl.pallas_call(
        functools.partial(_gdn_kernel, c=c, nb=nb, has_s0=has_s0),
        grid=(bsz // nb, nc),
        in_specs=in_specs,
        out_specs=[tok(), s_spec],
        out_shape=[jax.ShapeDtypeStruct((bsz, t, d), BF16),
                   jax.ShapeDtypeStruct((bsz, GDN_HEADS, GDN_DK, GDN_DV), F32)],
        scratch_shapes=[pltpu.VMEM((nb, GDN_HEADS, GDN_DK, GDN_DV), F32)],
        compiler_params=_params(("parallel", "arbitrary")),
        name="gdn_scan",
    )(*args)
    return yb.reshape(bsz * t, d), s_new


def _mem_attn_kernel(q_ref, k_ref, v_ref, o_ref, *, nb, tq):
    scale = MEM_HEAD_DIM ** -0.5
    pairs = [(s, h) for s in range(nb) for h in range(MEM_HEADS)]
    span = lambda s, h: (slice(s * tq, (s + 1) * tq), slice(h * MEM_HEAD_DIM, (h + 1) * MEM_HEAD_DIM))
    scores = [_dot_nt(q_ref[span(s, h)], k_ref[s, :, span(s, h)[1]]) * scale for s, h in pairs]
    probs = []
    for sc in scores:
        e = jnp.exp(sc - jnp.max(sc, axis=-1, keepdims=True))
        probs.append(e / jnp.sum(e, axis=-1, keepdims=True))
    for (s, h), prob in zip(pairs, probs):
        o_ref[span(s, h)] = _dot(prob, v_ref[s, :, span(s, h)[1]]).astype(BF16)


def _mem_attn(p, mk, mv, bsz, t, tq, nb):
    d = D_MODEL
    m = mk.shape[1]
    nq = t // tq
    assert nb == 1 or nq == 1
    kv_spec = pl.BlockSpec((nb, m, d), lambda b, i: (b, 0, 0))
    return pl.pallas_call(
        functools.partial(_mem_attn_kernel, nb=nb, tq=tq),
        grid=(bsz // nb, nq),
        in_specs=[pl.BlockSpec((nb * tq, d), lambda b, i: (b * nq + i, COL_MQ)), kv_spec, kv_spec],
        out_specs=pl.BlockSpec((nb * tq, d), lambda b, i: (b * nq + i, 0)),
        out_shape=jax.ShapeDtypeStruct((bsz * t, d), BF16),
        compiler_params=_params(("parallel", "arbitrary")),
        name="mem_attn",
    )(p, mk, mv)


def _mem_attn_cache_kernel(q_ref, k_ref, v_ref, o_ref, *, nb, tq):
    scale = MEM_HEAD_DIM ** -0.5
    m = k_ref.shape[1]
    shape = (MEM_HEADS * tq, m * MEM_HEADS)
    own = (lax.broadcasted_iota(jnp.int32, shape, 1) % MEM_HEADS) == (lax.broadcasted_iota(jnp.int32, shape, 0) // tq)
    qs, ks, vs = [], [], []
    for s in range(nb):
        rows = slice(s * tq, (s + 1) * tq)
        qs.append(jnp.concatenate([q_ref[rows, h * MEM_HEAD_DIM:(h + 1) * MEM_HEAD_DIM] for h in range(MEM_HEADS)],
                                  axis=0))
        ks.append(k_ref[s].reshape(m * MEM_HEADS, MEM_HEAD_DIM))
        vs.append(v_ref[s].reshape(m * MEM_HEADS, MEM_HEAD_DIM))
    scores = [jnp.where(own, _dot_nt(q, k) * scale, -jnp.inf) for q, k in zip(qs, ks)]
    probs = []
    for sc in scores:
        e = jnp.exp(sc - jnp.max(sc, axis=-1, keepdims=True))
        probs.append(e / jnp.sum(e, axis=-1, keepdims=True))
    for s, (prob, v) in enumerate(zip(probs, vs)):
        o = _dot(prob, v)
        o_ref[s * tq:(s + 1) * tq, :] = jnp.concatenate(
            [o[h * tq:(h + 1) * tq] for h in range(MEM_HEADS)], axis=1).astype(BF16)


def _mem_attn_cache(p, mk, mv, bsz, t, nb):
    d = D_MODEL
    kv_spec = pl.BlockSpec((nb,) + mk.shape[1:], lambda b: (b, 0, 0, 0))
    return pl.pallas_call(
        functools.partial(_mem_attn_cache_kernel, nb=nb, tq=t),
        grid=(bsz // nb,),
        in_specs=[pl.BlockSpec((nb * t, d), lambda b: (b, COL_MQ)), kv_spec, kv_spec],
        out_specs=pl.BlockSpec((nb * t, d), lambda b: (b, 0)),
        out_shape=jax.ShapeDtypeStruct((bsz * t, d), BF16),
        compiler_params=_params(("parallel",)),
        name="mem_attn_cache",
    )(p, mk, mv)


def _merge_kernel(x_ref, ya_ref, yb_ref, ym_ref, g0_ref, g1_ref, g2_ref, wc_ref, wg_ref, wm_ref, wo_ref,
                  ng_ref, x1_ref, h2_ref):
    d = functools.partial(jnp.dot, preferred_element_type=F32)
    merged = (jax.nn.sigmoid(_f32(g0_ref)) * d(ya_ref[...], wc_ref[...])
              + jax.nn.sigmoid(_f32(g1_ref)) * d(yb_ref[...], wg_ref[...])
              + jax.nn.sigmoid(_f32(g2_ref)) * d(ym_ref[...], wm_ref[...]))
    x1 = x_ref[...] + d(merged.astype(BF16), wo_ref[...])
    x1_ref[...] = x1
    h2_ref[...] = ((x1 * _rms_scale(x1)) * ng_ref[...]).astype(BF16)


def _merge(x, ya, yb, ym, p, wc, wg, wm, wo, norm_g, tm):
    n, d = x.shape
    tok = pl.BlockSpec((tm, d), lambda i: (i, 0))
    wsp = pl.BlockSpec((d, d), lambda i: (0, 0))

    def gate(j):
        return pl.BlockSpec((tm, d), lambda i, j=j: (i, COL_BG + j))

    return pl.pallas_call(
        _merge_kernel,
        grid=(n // tm,),
        in_specs=[tok, tok, tok, tok, gate(0), gate(1), gate(2), wsp, wsp, wsp, wsp,
                  pl.BlockSpec((1, d), lambda i: (0, 0))],
        out_specs=[tok, tok],
        out_shape=[jax.ShapeDtypeStruct((n, d), F32), jax.ShapeDtypeStruct((n, d), BF16)],
        compiler_params=_params(("parallel",)),
        name="merge",
    )(x, ya, yb, ym, p, p, p, wc, wg, wm, wo, norm_g.reshape(1, d))


_CAND_CELLS = tuple((a, b) for a in range(PEER_TOPK) for b in range(PEER_TOPK) if (a + 1) * (b + 1) <= PEER_TOPK)

TILE_PAD = 4 * SUBLANES
SHIFT_S1, SHIFT_E1 = 0, SUBLANES


def _put_tile(ref, h, ns, shift, val, pad=True):
    if pad and shift:
        ref[h, ns, :shift, :] = jnp.zeros((shift, LANES), F32)
    ref[h, ns, shift:shift + PEER_N_KEYS, :] = val
    if pad and shift < TILE_PAD:
        ref[h, ns, shift + PEER_N_KEYS:, :] = jnp.zeros((TILE_PAD - shift, LANES), F32)


def _top_values(x, count):
    vals = []
    cur = x
    for r in range(count):
        m = jnp.max(cur, axis=0, keepdims=True)
        vals.append(m)
        if r + 1 < count:
            cur = jnp.where(cur == m, -jnp.inf, cur)
    return vals


def _top_ranked(x, count):
    n_rows = x.shape[0]
    iota = lax.broadcasted_iota(jnp.int32, x.shape, 0).astype(F32)
    vals = []
    cur = x
    rank = jnp.full(x.shape, float(count), F32)
    for r in range(count):
        m = jnp.max(cur, axis=0, keepdims=True)
        first = jnp.min(jnp.where(cur == m, iota, float(n_rows)), axis=0, keepdims=True)
        one = iota == first
        rank = jnp.where(one, float(r), rank)
        cur = jnp.where(one, -jnp.inf, cur)
        vals.append(m)
    return vals, rank


def _count_rows(mask):
    return jnp.sum(jnp.where(mask, 1.0, 0.0), axis=0, keepdims=True)


def _route_kernel(h_ref, wq_ref, keys_ref, cut_ref, e0_ref, s1_ref, e1_ref, q_scr, tie_ref):
    tn = h_ref.shape[0]
    nsb = tn // LANES
    q = jnp.dot(h_ref[...], wq_ref[...], preferred_element_type=F32).astype(BF16)

    def store(h, cut, e0, s1, e1):
        for ns in range(nsb):
            cols = slice(ns * LANES, (ns + 1) * LANES)
            cut_ref[h, ns] = cut[:, cols]
            e0_ref[h, ns] = e0[:, cols]
            _put_tile(s1_ref, h, ns, SHIFT_S1, s1[:, cols])
            _put_tile(e1_ref, h, ns, SHIFT_E1, e1[:, cols])

    for h in range(PEER_HEADS):
        sc, top = [], []
        tied = jnp.zeros((1, tn), F32)
        for part in range(2):
            col = (2 * h + part) * PEER_D_HALF
            q_scr[2 * h + part] = q[:, col:col + PEER_D_HALF]
            s_t = lax.dot_general(keys_ref[part, h], q[:, col:col + PEER_D_HALF],
                                  (((1,), (1,)), ((), ())), preferred_element_type=F32)
            sc.append(s_t)
            top.append(_top_values(s_t, PEER_TOPK))
            tied = jnp.maximum(tied, jnp.abs(_count_rows(s_t >= top[part][-1]) - PEER_TOPK))
        cand = jnp.concatenate([top[0][a] + top[1][b] for a, b in _CAND_CELLS], axis=0)
        best = _top_values(cand, PEER_TOPK)
        c_max, thr = best[0], best[-1]
        z = jnp.exp(best[0] - c_max)
        for r in range(1, PEER_TOPK):
            z = z + jnp.exp(best[r] - c_max)
        top1 = jnp.concatenate(top[1], axis=0)
        n_sel = sum(_count_rows(top[0][a] + top1 >= thr) for a in range(PEER_TOPK))
        tied = jnp.maximum(tied, jnp.abs(n_sel - PEER_TOPK))
        cut = jnp.full(sc[0].shape, jnp.inf, F32)
        for b in range(PEER_TOPK):
            cut = jnp.where(sc[0] + top[1][b] >= thr, top[1][b], cut)
        store(h, cut, jnp.exp(sc[0] - top[0][0]) / z, sc[1], jnp.exp(sc[1] - top[1][0]))
        for ns in range(nsb):
            tie_ref[h * nsb + ns] = jnp.max(tied[:, ns * LANES:(ns + 1) * LANES])

    def exact_block(idx, carry):
        @pl.when(tie_ref[idx] > 0.0)
        def _():
            h = idx // nsb
            ns = idx % nsb
            rows = pl.ds(pl.multiple_of(ns * LANES, LANES), LANES)
            sc, top, rank = [], [], []
            for part in range(2):
                s_t = lax.dot_general(keys_ref[part, h], q_scr[2 * h + part, rows, :],
                                      (((1,), (1,)), ((), ())), preferred_element_type=F32)
                vals, rk = _top_ranked(s_t, PEER_TOPK)
                sc.append(s_t)
                top.append(vals)
                rank.append(rk)
            top1 = jnp.concatenate(top[1], axis=0)
            cand = jnp.concatenate([top[0][a] + top1 for a in range(PEER_TOPK)], axis=0)
            best, pos = _top_ranked(cand, PEER_TOPK)
            z = jnp.exp(best[0] - best[0])
            for r in range(1, PEER_TOPK):
                z = z + jnp.exp(best[r] - best[0])
            cnt = jnp.zeros_like(sc[0])
            for a in range(PEER_TOPK):
                cnt_a = _count_rows(pos[a * PEER_TOPK:(a + 1) * PEER_TOPK] < PEER_TOPK)
                cnt = jnp.where(rank[0] == float(a), cnt_a, cnt)
            cut_ref[h, ns] = jnp.where(cnt > 0.0, 1.0 - cnt, jnp.inf)
            e0_ref[h, ns] = jnp.exp(sc[0] - top[0][0]) / z
            _put_tile(s1_ref, h, ns, SHIFT_S1, -rank[1], pad=False)
            _put_tile(e1_ref, h, ns, SHIFT_E1, jnp.exp(sc[1] - top[1][0]), pad=False)
        return carry

    lax.fori_loop(0, PEER_HEADS * nsb, exact_block, 0)


def _route(h2, wq, keys, tn):
    n, d = h2.shape
    nk = PEER_N_KEYS
    tiled = pl.BlockSpec((PEER_HEADS, tn // LANES, nk + TILE_PAD, LANES), lambda i: (0, i, 0, 0))
    tiled_shape = jax.ShapeDtypeStruct((PEER_HEADS, n // LANES, nk + TILE_PAD, LANES), F32)
    rowed = pl.BlockSpec((PEER_HEADS, tn // LANES, nk, LANES), lambda i: (0, i, 0, 0))
    rowed_shape = jax.ShapeDtypeStruct((PEER_HEADS, n // LANES, nk, LANES), F32)
    return pl.pallas_call(
        _route_kernel,
        grid=(n // tn,),
        in_specs=[pl.BlockSpec((tn, d), lambda i: (i, 0)),
                  pl.BlockSpec(wq.shape, lambda i: (0, 0)),
                  pl.BlockSpec(keys.shape, lambda i: (0, 0, 0, 0))],
        out_specs=[rowed, rowed, tiled, tiled],
        out_shape=[rowed_shape, rowed_shape, tiled_shape, tiled_shape],
        scratch_shapes=[pltpu.VMEM((2 * PEER_HEADS, tn, PEER_D_HALF), BF16),
                        pltpu.SMEM((PEER_HEADS * (tn // LANES),), F32)],
        compiler_params=_params(("parallel",)),
        name="peer_route",
    )(h2, wq, keys)


def _gelu(x):
    return 0.5 * x * (1.0 + lax.erf(x * (2.0 ** -0.5)))


PEER_TILE = 2048
GATE_HALF = 32
GATE_ROWS = 8


def _peer_kernel(h_ref, x1_ref, u_ref, vt_ref, cut_ref, e0_ref, s1_ref, e1_ref, ng_ref,
                 y_ref, acc_ref, act_ref, coef_ref, *, te):
    ei = pl.program_id(1)
    tn = h_ref.shape[0]
    groups = te // PEER_N_KEYS

    @pl.when(ei == 0)
    def _():
        acc_ref[...] = jnp.zeros_like(acc_ref)

    nsb = tn // LANES

    def activations(lo, hi):
        act = _gelu(lax.dot_general(u_ref[...], h_ref[lo * LANES:hi * LANES, :], (((1,), (1,)), ((), ())),
                                    preferred_element_type=F32))
        for ns in range(lo, hi):
            act_ref[ns] = act[:, (ns - lo) * LANES:(ns - lo + 1) * LANES]

    def gate_block(ns):
        for jh in range(PEER_N_KEYS // GATE_HALF):
            jrows = slice(jh * GATE_HALF, (jh + 1) * GATE_HALF)
            for g0 in range(0, groups, GATE_ROWS):
                gates = [None] * GATE_ROWS
                for h in range(PEER_HEADS):
                    s1_t = s1_ref[h, ns, SHIFT_S1 + jh * GATE_HALF:SHIFT_S1 + (jh + 1) * GATE_HALF, :]
                    e1_t = e1_ref[h, ns, SHIFT_E1 + jh * GATE_HALF:SHIFT_E1 + (jh + 1) * GATE_HALF, :]
                    for k in range(GATE_ROWS):
                        sel = s1_t >= cut_ref[h, ns, g0 + k:g0 + k + 1, :]
                        term = jnp.where(sel, e1_t, 0.0) * e0_ref[h, ns, g0 + k:g0 + k + 1, :]
                        gates[k] = term if gates[k] is None else gates[k] + term
                for k in range(GATE_ROWS):
                    r0 = (g0 + k) * PEER_N_KEYS + jh * GATE_HALF
                    coef_ref[ns, r0:r0 + GATE_HALF, :] = (gates[k] * act_ref[ns, r0:r0 + GATE_HALF, :]).astype(BF16)

    def mix_values(lo, hi):
        coef = jnp.concatenate([coef_ref[ns] for ns in range(lo, hi)], axis=1)
        acc_ref[:, lo * LANES:hi * LANES] += jnp.dot(vt_ref[0], coef, preferred_element_type=F32)

    half = nsb // 2
    if half:
        activations(0, half)
    activations(half, nsb)
    for ns in range(half):
        gate_block(ns)
    if half:
        mix_values(0, half)
    for ns in range(half, nsb):
        gate_block(ns)
    mix_values(half, nsb)

    @pl.when(ei == pl.num_programs(1) - 1)
    def _():
        x2 = x1_ref[...] + acc_ref[...].T
        y_ref[...] = (x2 * _rms_scale(x2)) * ng_ref[...]


def _peer(h2, x1, u_bf, vt_bf, cut, e0, s1, e1, norm_g, tn):
    n, d = h2.shape
    ne = u_bf.shape[0]
    te = vt_bf.shape[2]
    nk = PEER_N_KEYS
    tok = pl.BlockSpec((tn, d), lambda i, e: (i, 0))
    tiled = pl.BlockSpec((PEER_HEADS, tn // LANES, nk + TILE_PAD, LANES), lambda i, e: (0, i, 0, 0))
    rowed = pl.BlockSpec((PEER_HEADS, tn // LANES, te // nk, LANES), lambda i, e: (0, i, e, 0))
    return pl.pallas_call(
        functools.partial(_peer_kernel, te=te),
        grid=(n // tn, ne // te),
        in_specs=[tok, tok,
                  pl.BlockSpec((te, d), lambda i, e: (e, 0)),
                  pl.BlockSpec((1, d, te), lambda i, e: (e, 0, 0)),
                  rowed, rowed, tiled, tiled,
                  pl.BlockSpec((1, d), lambda i, e: (0, 0))],
        out_specs=tok,
        out_shape=jax.ShapeDtypeStruct((n, d), F32),
        scratch_shapes=[pltpu.VMEM((d, tn), F32), pltpu.VMEM((tn // LANES, te, LANES), F32),
                        pltpu.VMEM((tn // LANES, te, LANES), BF16)],
        compiler_params=_params(("parallel", "arbitrary")),
        name="peer_mix",
    )(h2, x1, u_bf, vt_bf, cut, e0, s1, e1, norm_g.reshape(1, d))


def _tile(n, pref):
    return math.gcd(n, pref)


def _layer(x, mk, mv, bsz, t, weights, conv_state=None, gdn_conv_state=None, gdn_s0=None):
    (norm_mix_g, w_main, w_ab, conv_w, gdn_conv_w, a_log, dt_bias, gdn_norm_g, wc, wg, wm, wo, norm_ffn_g,
     wq, keys, u_bf, vt_bf, final_norm_g) = weights
    n = bsz * t
    sample = conv_state is not None
    p, pab = _inproj(x, norm_mix_g, w_main, w_ab, _tile(n, 2048))
    rows = _tile(n, 256) if sample else _tile(t, 512)
    ya, q, k, v, gb, u_tail = _prep(p, pab, conv_w, gdn_conv_w, a_log, dt_bias, rows, max(t // rows, 1),
                                    conv_state, gdn_conv_state)
    chunk = _tile(t, 64)
    if sample:
        yb, s_new = _gdn(q, k, v, p, gb, gdn_norm_g, bsz, t, chunk, 4, gdn_s0)
    else:
        yb, s_new = _gdn_stacked(q, k, v, p, gb, gdn_norm_g, bsz, t, chunk, _tile(bsz, 4))
    if mk.ndim == 4:
        ym = _mem_attn_cache(p, mk, mv, bsz, t, _tile(bsz, 4))
    else:
        ym = _mem_attn(p, mk, mv, bsz, t, _tile(t, 512), 1)
    x1, h2 = _merge(x, ya, yb, ym, p, wc, wg, wm, wo, norm_ffn_g, _tile(n, 512))
    if sample:
        keep = lambda a: a.reshape(bsz, t, -1)[:, t // 2:, :].reshape(bsz * t // 2, -1)
        x1, h2 = keep(x1), keep(h2)
    tn = _tile(x1.shape[0], 512)
    cut, e0, s1, e1 = _route(h2, wq, keys, tn)
    y = _peer(h2, x1, u_bf, vt_bf, cut, e0, s1, e1, final_norm_g, tn)
    return y, u_tail, p, s_new


def kernel(x_prompt, x_sample, mem_prompt, cache_mem_k, cache_mem_v, state_conv, state_gdn_conv, state_gdn, norm_mix_g, w_in, conv_w, gdn_conv_w, gdn_a_log, gdn_dt_bias, gdn_norm_g, mem_norm_g, w_mem_kv, w_br_conv, w_br_gdn, w_br_mem, w_out, norm_ffn_g, peer_w_q, peer_sub_keys, peer_u, peer_v, final_norm_g):
    depth = w_in.shape[0]
    assert depth == 1, "single-layer step"
    d = D_MODEL
    bp, tp, _ = x_prompt.shape
    bs, ts, _ = x_sample.shape
    assert ts == SAMPLE_PAD_T // 2
    m_tok = mem_prompt.shape[1]
    mq = MEM_HEADS * MEM_HEAD_DIM
    l = 0

    w = w_in[l]
    ga0 = 7 * d
    w_main = jnp.concatenate([w[:, :ga0], w[:, ga0 + 2 * GDN_HEADS:]], axis=1).astype(BF16)
    w_ab = jnp.pad(w[:, ga0:ga0 + 2 * GDN_HEADS], ((0, 0), (0, SMALL_COLS - 2 * GDN_HEADS))).astype(BF16)
    weights = (norm_mix_g[l], w_main, w_ab, conv_w[l], gdn_conv_w[l], gdn_a_log[l], gdn_dt_bias[l], gdn_norm_g[l],
               w_br_conv[l].astype(BF16), w_br_gdn[l].astype(BF16), w_br_mem[l].astype(BF16), w_out[l].astype(BF16),
               norm_ffn_g[l], peer_w_q[l].astype(BF16), peer_sub_keys[l].astype(BF16),
               peer_u[l].astype(BF16),
               peer_v[l].astype(BF16).reshape(-1, PEER_TILE, d).transpose(0, 2, 1), final_norm_g)

    kv_shape = (bp, m_tok, mq)
    mk_p, mv_p = _mem_kv(mem_prompt.reshape(bp * m_tok, d), mem_norm_g[l], w_mem_kv[l].astype(BF16),
                         _tile(bp * m_tok, 256))
    y_p, u_tail_p, p_p, s_p = _layer(x_prompt.reshape(bp * tp, d), mk_p.reshape(kv_shape), mv_p.reshape(kv_shape),
                                     bp, tp, weights)
    rows_p = _tile(tp, 512)
    cb_p = u_tail_p.reshape(bp, tp // rows_p, SUBLANES, d)[:, -1, SUBLANES - (CONV_K - 1):, :]
    gb_p = p_p.reshape(bp, tp, -1)[:, tp - (GDN_CONV_K - 1):, COL_GQ * d:(COL_GV + 1) * d].astype(F32)

    pad_t = SAMPLE_PAD_T
    lead = pad_t - ts
    x_s = jnp.pad(x_sample, ((0, 0), (lead, 0), (0, 0))).reshape(bs * pad_t, d)
    cst = jnp.pad(state_conv[l], ((0, 0), (lead - (CONV_K - 1), ts), (0, 0))).reshape(bs * pad_t, d)
    gst = jnp.pad(state_gdn_conv[l], ((0, 0), (lead - (GDN_CONV_K - 1), ts), (0, 0))).reshape(bs * pad_t, 3 * d)
    y_s, u_s, p_s, s_s = _layer(x_s, cache_mem_k.reshape(cache_mem_k.shape[1:]), cache_mem_v.reshape(cache_mem_v.shape[1:]),
                                bs, pad_t, weights, cst, gst, state_gdn.reshape(bs, GDN_HEADS, GDN_DK, GDN_DV))
    y_s = y_s.reshape(bs, ts, d)
    cb_s = u_s.reshape(bs, pad_t, d)[:, pad_t - (CONV_K - 1):, :]
    gb_s = p_s.reshape(bs, pad_t, -1)[:, pad_t - (GDN_CONV_K - 1):, COL_GQ * d:(COL_GV + 1) * d].astype(F32)

    mshape = (depth, bp, m_tok, MEM_HEADS, MEM_HEAD_DIM)
    return (y_p.reshape(bp, tp, d), y_s,
            mk_p.reshape(mshape), mv_p.reshape(mshape),
            cb_p[None], gb_p[None], s_p[None],
            cb_s[None], gb_s[None], s_s[None])
```

```python
import functools
import math

import jax
import jax.numpy as jnp
from jax import lax
from jax.experimental import pallas as pl
from jax.experimental.pallas import tpu as pltpu

F32 = jnp.float32
BF16 = jnp.bfloat16

EPS = 1e-6
D_MODEL = 1024
CONV_K = 3
GDN_HEADS = 8
GDN_DK = 128
GDN_DV = 128
GDN_CONV_K = 4
MEM_HEADS = 4
MEM_HEAD_DIM = 256
PEER_HEADS = 8
PEER_N_KEYS = 128
PEER_D_HALF = 128
PEER_TOPK = 16
N_BRANCHES = 3

COL_AU, COL_AB, COL_AC, COL_GQ, COL_GK, COL_GV, COL_GZ, COL_MQ, COL_BG = 0, 1, 2, 3, 4, 5, 6, 7, 8
N_MAIN_BLOCKS = 11
SMALL_COLS = 128
SUBLANES = 8
LANES = 128
SAMPLE_PAD_T = 8
VMEM_LIMIT = 56 * 1024 * 1024


def _params(sem):
    return pltpu.CompilerParams(dimension_semantics=sem, vmem_limit_bytes=VMEM_LIMIT)


def _dot(a, b):
    return jnp.dot(a.astype(BF16), b.astype(BF16), preferred_element_type=F32)


def _dot_nt(a, b):
    return lax.dot_general(a.astype(BF16), b.astype(BF16), (((1,), (1,)), ((), ())),
                           preferred_element_type=F32)


def _rms_scale(x):
    return lax.rsqrt(jnp.mean(x * x, axis=-1, keepdims=True) + EPS)


def _inproj_kernel(x_ref, g_ref, w_ref, wab_ref, p_ref, pab_ref, h_ref):
    @pl.when(pl.program_id(1) == 0)
    def _():
        x = x_ref[...]
        h = ((x * _rms_scale(x)) * g_ref[...]).astype(BF16)
        h_ref[...] = h
        pab_ref[...] = jnp.dot(h, wab_ref[...], preferred_element_type=F32)

    p_ref[...] = jnp.dot(h_ref[...], w_ref[...], preferred_element_type=F32).astype(p_ref.dtype)


def _inproj(x, g, w_main, w_ab, tm):
    n, d = x.shape
    nblk = w_main.shape[1] // d
    return pl.pallas_call(
        _inproj_kernel,
        grid=(n // tm, nblk),
        in_specs=[
            pl.BlockSpec((tm, d), lambda i, j: (i, 0)),
            pl.BlockSpec((1, d), lambda i, j: (0, 0)),
            pl.BlockSpec((d, d), lambda i, j: (0, j)),
            pl.BlockSpec((d, SMALL_COLS), lambda i, j: (0, 0)),
        ],
        out_specs=[
            pl.BlockSpec((tm, d), lambda i, j: (i, j)),
            pl.BlockSpec((tm, SMALL_COLS), lambda i, j: (i, 0)),
        ],
        out_shape=[
            jax.ShapeDtypeStruct((n, nblk * d), BF16),
            jax.ShapeDtypeStruct((n, SMALL_COLS), F32),
        ],
        scratch_shapes=[pltpu.VMEM((tm, d), BF16)],
        compiler_params=_params(("parallel", "arbitrary")),
        name="inproj",
    )(x, g.reshape(1, d), w_main, w_ab)


def _mem_kv_kernel(x_ref, g_ref, w_ref, k_ref, v_ref):
    x = x_ref[...]
    h = ((x * _rms_scale(x)) * g_ref[...]).astype(BF16)
    kv = jnp.dot(h, w_ref[...], preferred_element_type=F32)
    half = kv.shape[1] // 2
    k_ref[...] = kv[:, :half]
    v_ref[...] = kv[:, half:]


def _mem_kv(x, g, w, tm):
    n, d = x.shape
    c = w.shape[1]
    out = jax.ShapeDtypeStruct((n, c // 2), F32)
    ospec = pl.BlockSpec((tm, c // 2), lambda i: (i, 0))
    return pl.pallas_call(
        _mem_kv_kernel,
        grid=(n // tm,),
        in_specs=[
            pl.BlockSpec((tm, d), lambda i: (i, 0)),
            pl.BlockSpec((1, d), lambda i: (0, 0)),
            pl.BlockSpec((d, c), lambda i: (0, 0)),
        ],
        out_specs=[ospec, ospec],
        out_shape=[out, out],
        compiler_params=_params(("parallel",)),
        name="mem_kv",
    )(x, g.reshape(1, d), w)


def _causal_conv(prev, cur, w_ref, k):
    full = jnp.concatenate([prev, cur], axis=0)
    y = cur * w_ref[k - 1:k, :]
    for s in range(1, k):
        y = y + pltpu.roll(full, s, axis=0)[prev.shape[0]:, :] * w_ref[k - 1 - s:k - s, :]
    return y


def _f32(ref):
    return ref[...].astype(F32)


def _prep_kernel(*refs, rows, tiles_per_seq, sample):
    if sample:
        (au, ab, ac, gq, gk, gv, pab, au_p, ac_p, gq_p, gk_p, gv_p, cw, gw, alog, dtb, cst, gst,
         ya_o, q_o, k_o, v_o, gb_o, ut_o) = refs
    else:
        (au, ab, ac, gq, gk, gv, pab, au_p, ac_p, gq_p, gk_p, gv_p, cw, gw, alog, dtb,
         ya_o, q_o, k_o, v_o, gb_o, ut_o) = refs
    d = D_MODEL
    first = (pl.program_id(0) % tiles_per_seq) == 0
    keep = jnp.where(first, 0.0, 1.0).astype(F32)

    u = _f32(ac) * _f32(au)
    u_prev = (_f32(ac_p) * _f32(au_p)) * keep
    if sample:
        real = (lax.broadcasted_iota(jnp.int32, (rows, 1), 0) % SAMPLE_PAD_T) >= (SAMPLE_PAD_T // 2)
        u = jnp.where(real, u, cst[...])
    ya_o[...] = (_f32(ab) * _causal_conv(u_prev, u, cw, CONV_K)).astype(BF16)
    ut_o[...] = u[rows - ut_o.shape[0]:, :]

    outs = (q_o, k_o, v_o)
    for idx, (cur_ref, prev_ref) in enumerate(((gq, gq_p), (gk, gk_p), (gv, gv_p))):
        cur = _f32(cur_ref)
        if sample:
            cur = jnp.where(real, cur, gst[:, idx * d:(idx + 1) * d])
        prev = _f32(prev_ref) * keep
        wslab = gw.at[:, idx * d:(idx + 1) * d]
        y = _causal_conv(prev, cur, wslab, GDN_CONV_K)
        y = y * jax.nn.sigmoid(y)
        if idx < 2:
            scale = GDN_DK ** -0.5 if idx == 0 else 1.0
            for h in range(GDN_HEADS):
                yh = y[:, h * GDN_DK:(h + 1) * GDN_DK]
                r = lax.rsqrt(jnp.sum(yh * yh, axis=-1, keepdims=True) + EPS)
                outs[idx][:, h * GDN_DK:(h + 1) * GDN_DK] = yh * (r * scale)
        else:
            outs[idx][...] = y

    pa = pab[...]
    lane = lax.broadcasted_iota(jnp.int32, pa.shape, 1)
    z = pa + dtb[...]
    softplus = jnp.maximum(z, 0.0) + jnp.log1p(jnp.exp(-jnp.abs(z)))
    g_log = -jnp.exp(alog[...]) * softplus
    gb = jnp.where(lane < GDN_HEADS, g_log, jnp.where(lane < 2 * GDN_HEADS, jax.nn.sigmoid(pa), 0.0))
    if sample:
        gb = jnp.where(real, gb, 0.0)
    gb_o[...] = gb


def _prep(p, pab, conv_w, gdn_conv_w, a_log, dt_bias, rows, tiles_per_seq, conv_state=None, gdn_state=None):
    n = p.shape[0]
    d = D_MODEL
    sample = conv_state is not None
    nt = n // rows
    prev_rows = 2 * SUBLANES
    rpb = rows // prev_rows

    def cur(c):
        return pl.BlockSpec((rows, d), lambda i, c=c: (i, c))

    def prev(c):
        return pl.BlockSpec((prev_rows, d), lambda i, c=c: (jnp.maximum(i * rpb - 1, 0), c))

    alog128 = jnp.zeros((1, SMALL_COLS), F32).at[0, :GDN_HEADS].set(a_log)
    dtb128 = jnp.zeros((1, SMALL_COLS), F32).at[0, :GDN_HEADS].set(dt_bias)
    in_specs = [cur(COL_AU), cur(COL_AB), cur(COL_AC), cur(COL_GQ), cur(COL_GK), cur(COL_GV),
                pl.BlockSpec((rows, SMALL_COLS), lambda i: (i, 0)),
                prev(COL_AU), prev(COL_AC), prev(COL_GQ), prev(COL_GK), prev(COL_GV),
                pl.BlockSpec((CONV_K, d), lambda i: (0, 0)),
                pl.BlockSpec((GDN_CONV_K, 3 * d), lambda i: (0, 0)),
                pl.BlockSpec((1, SMALL_COLS), lambda i: (0, 0)),
                pl.BlockSpec((1, SMALL_COLS), lambda i: (0, 0))]
    args = [p, p, p, p, p, p, pab, p, p, p, p, p, conv_w, gdn_conv_w, alog128, dtb128]
    if sample:
        in_specs += [pl.BlockSpec((rows, d), lambda i: (i, 0)), pl.BlockSpec((rows, 3 * d), lambda i: (i, 0))]
        args += [conv_state, gdn_state]
    tail = rows if sample else SUBLANES
    out_specs = [pl.BlockSpec((rows, d), lambda i: (i, 0))] * 4 + [
        pl.BlockSpec((rows, SMALL_COLS), lambda i: (i, 0)),
        pl.BlockSpec((tail, d), lambda i: (i, 0))]
    out_shape = [jax.ShapeDtypeStruct((n, d), BF16)] + [jax.ShapeDtypeStruct((n, d), F32)] * 3 + [
        jax.ShapeDtypeStruct((n, SMALL_COLS), F32),
        jax.ShapeDtypeStruct((nt * tail, d), F32)]
    return pl.pallas_call(
        functools.partial(_prep_kernel, rows=rows, tiles_per_seq=tiles_per_seq, sample=sample),
        grid=(nt,),
        in_specs=in_specs,
        out_specs=out_specs,
        out_shape=out_shape,
        compiler_params=_params(("parallel",)),
        name="prep",
    )(*args)


def _chunk_cumsum(g, c):
    row = lax.broadcasted_iota(jnp.int32, g.shape, 0)
    sh = 1
    while sh < c:
        g = g + jnp.where(row >= sh, pltpu.roll(g, sh, axis=0), 0.0)
        sh *= 2
    return g


def _gdn_stacked_kernel(q_ref, k_ref, v_ref, z_ref, gb_ref, ng_ref, yb_ref, sout_ref, s_scr, *, c, grp, nb):
    ci = pl.program_id(1)

    @pl.when(ci == 0)
    def _():
        s_scr[...] = jnp.zeros_like(s_scr)

    r = grp * c
    ng = ng_ref[...]
    seqs = []
    for s in range(nb):
        gb = gb_ref[s]
        gc = _chunk_cumsum(gb, c)
        g_last = gc[c - 1:c, :]
        seqs.append(dict(gb=gb, gc=gc, gct=gc.T, e_g=jnp.exp(gc), e_kd=jnp.exp(g_last - gc), e_last=jnp.exp(g_last)))

    ri = lax.broadcasted_iota(jnp.int32, (r, r), 0)
    cj = lax.broadcasted_iota(jnp.int32, (r, r), 1)
    same = (ri // c) == (cj // c)
    incl = same & (ri >= cj)
    strict = same & (ri > cj)
    head_block = (lax.broadcasted_iota(jnp.int32, (r, grp * GDN_DK), 0) // c
                  == lax.broadcasted_iota(jnp.int32, (r, grp * GDN_DK), 1) // GDN_DK)

    def block_layout(x):
        return jnp.where(head_block, jnp.concatenate([x] * grp, axis=1), 0.0)

    units = []
    for s in range(nb):
        for g0 in range(0, GDN_HEADS, grp):
            heads = list(range(g0, g0 + grp))
            sq = seqs[s]

            def stack(ref, s=s, heads=heads):
                return jnp.concatenate([ref[s, :, h * GDN_DK:(h + 1) * GDN_DK] for h in heads], axis=0)

            def col(a, off=0, heads=heads):
                return jnp.concatenate([a[:, off + h:off + h + 1] for h in heads], axis=0)

            qs, ks, vs = stack(q_ref), stack(k_ref), stack(v_ref)
            beta = col(sq["gb"], GDN_HEADS)
            kb = ks * beta
            units.append(dict(s=s, heads=heads, qs=qs, ks=ks, kb=kb, vb=vs * beta,
                              gcol=col(sq["gc"]), egcol=col(sq["e_g"]), ekdcol=col(sq["e_kd"]),
                              grow=jnp.concatenate([sq["gct"][h:h + 1, :] for h in heads], axis=1),
                              rows=slice((s * GDN_HEADS + g0) * GDN_DK, (s * GDN_HEADS + g0 + grp) * GDN_DK)))

    for u in units:
        u["aq"] = _dot_nt(jnp.concatenate([u["kb"], u["qs"]], axis=0), u["ks"])
    for u in units:
        decay = jnp.exp(jnp.where(incl, u["gcol"] - u["grow"], -jnp.inf))
        u["pw"] = -jnp.where(strict, u["aq"][:r] * decay, 0.0)
        u["attn"] = jnp.where(incl, u["aq"][r:] * decay, 0.0)
        u["x"] = jnp.concatenate([u["vb"], u["kb"] * u["egcol"]], axis=1)
    span = 1
    while span < c:
        span *= 2
        for u in units:
            if span < c:
                both = _dot(u["pw"], jnp.concatenate([u["pw"], u["x"]], axis=1))
                u["pw"] = both[:, :r]
                u["x"] = u["x"] + both[:, r:]
            else:
                u["x"] = u["x"] + _dot(u["pw"], u["x"])
    for u in units:
        u_val, w_k = u["x"][:, :GDN_DV], u["x"][:, GDN_DV:]
        u["s_old"] = s_scr[u["rows"], :]
        ws = _dot(jnp.concatenate([block_layout(w_k), block_layout(u["qs"] * u["egcol"])], axis=0), u["s_old"])
        u["v_new"] = u_val - ws[:r]
        u["o"] = ws[r:]
    for u in units:
        e_last = seqs[u["s"]]["e_last"]
        o = u["o"] + _dot(u["attn"], u["v_new"])
        e_rows = jnp.concatenate([jnp.broadcast_to(e_last[:, h:h + 1], (GDN_DK, 1)) for h in u["heads"]], axis=0)
        kd_t = block_layout(u["ks"] * u["ekdcol"]).T
        s_scr[u["rows"], :] = u["s_old"] * e_rows + _dot(kd_t, u["v_new"])
        for i, h in enumerate(u["heads"]):
            sl = slice(h * GDN_DK, (h + 1) * GDN_DK)
            oh = o[i * c:(i + 1) * c, :]
            zh = z_ref[u["s"], :, sl].astype(F32)
            yb_ref[u["s"], :, sl] = (((oh * _rms_scale(oh)) * ng) * (zh * jax.nn.sigmoid(zh))).astype(BF16)

    @pl.when(ci == pl.num_programs(1) - 1)
    def _():
        sout_ref[...] = s_scr[...]


def _gdn_stacked(q, k, v, p, gb, norm_g, bsz, t, c, nb):
    d = D_MODEL
    nc = t // c
    grp = min(GDN_HEADS, 256 // c)
    s_rows = nb * GDN_HEADS * GDN_DK

    def tok(col=0):
        return pl.BlockSpec((nb, c, d), lambda b, i, col=col: (b, i, col))

    yb, s_new = pl.pallas_call(
        functools.partial(_gdn_stacked_kernel, c=c, grp=grp, nb=nb),
        grid=(bsz // nb, nc),
        in_specs=[tok(), tok(), tok(), tok(COL_GZ),
                  pl.BlockSpec((nb, c, SMALL_COLS), lambda b, i: (b, i, 0)),
                  pl.BlockSpec((1, GDN_DV), lambda b, i: (0, 0))],
        out_specs=[tok(), pl.BlockSpec((s_rows, GDN_DV), lambda b, i: (b, 0))],
        out_shape=[jax.ShapeDtypeStruct((bsz, t, d), BF16),
                   jax.ShapeDtypeStruct((bsz * GDN_HEADS * GDN_DK, GDN_DV), F32)],
        scratch_shapes=[pltpu.VMEM((s_rows, GDN_DV), F32)],
        compiler_params=_params(("parallel", "arbitrary")),
        name="gdn_scan_prompt",
    )(*[a.reshape(bsz, t, -1) for a in (q, k, v, p, gb)], norm_g.reshape(1, GDN_DV))
    return yb.reshape(bsz * t, d), s_new.reshape(bsz, GDN_HEADS, GDN_DK, GDN_DV)


def _gdn_kernel(*refs, c, nb, has_s0):
    if has_s0:
        q_ref, k_ref, v_ref, z_ref, gb_ref, ng_ref, s0_ref, yb_ref, sout_ref, s_scr = refs
    else:
        q_ref, k_ref, v_ref, z_ref, gb_ref, ng_ref, yb_ref, sout_ref, s_scr = refs
    ci = pl.program_id(1)

    @pl.when(ci == 0)
    def _():
        if has_s0:
            s_scr[...] = s0_ref[...]
        else:
            s_scr[...] = jnp.zeros_like(s_scr)

    ri = lax.broadcasted_iota(jnp.int32, (c, c), 0)
    cj = lax.broadcasted_iota(jnp.int32, (c, c), 1)
    incl = ri >= cj
    strict = ri > cj
    ng = ng_ref[...]
    row = lax.broadcasted_iota(jnp.int32, (c, SMALL_COLS), 0)
    chains = [(s, h) for s in range(nb) for h in range(GDN_HEADS)]

    seq = []
    for s in range(nb):
        gb = gb_ref[s]
        gc = gb
        sh = 1
        while sh < c:
            gc = gc + jnp.where(row >= sh, pltpu.roll(gc, sh, axis=0), 0.0)
            sh *= 2
        g_last = gc[c - 1:c, :]
        seq.append(dict(gb=gb, gc=gc, gct=gc.T, e_g=jnp.exp(gc), e_kd=jnp.exp(g_last - gc), e_last=jnp.exp(g_last)))

    st = []
    for s, h in chains:
        sl = slice(h * GDN_DK, (h + 1) * GDN_DK)
        sq = seq[s]
        qh, kh, vh = q_ref[s, :, sl], k_ref[s, :, sl], v_ref[s, :, sl]
        beta = sq["gb"][:, GDN_HEADS + h:GDN_HEADS + h + 1]
        kb = kh * beta
        aq = _dot_nt(jnp.concatenate([kb, qh], axis=0), kh)
        st.append(dict(sl=sl, qh=qh, kh=kh, kb=kb, vb=vh * beta, aq=aq))

    for (s, h), e in zip(chains, st):
        sq = seq[s]
        decay = jnp.exp(jnp.where(incl, sq["gc"][:, h:h + 1] - sq["gct"][h:h + 1, :], -jnp.inf))
        e["neg"] = -jnp.where(strict, e["aq"][:c] * decay, 0.0)
        e["attn"] = jnp.where(incl, e["aq"][c:] * decay, 0.0)
        e["x"] = jnp.concatenate([e["vb"], e["kb"] * sq["e_g"][:, h:h + 1]], axis=1)
        e["pw"] = e["neg"]

    span = 1
    while span < c:
        span *= 2
        for e in st:
            if span < c:
                both = _dot(e["pw"], jnp.concatenate([e["pw"], e["x"]], axis=1))
                e["pw"] = both[:, :c]
                e["x"] = e["x"] + both[:, c:]
            else:
                e["x"] = e["x"] + _dot(e["pw"], e["x"])

    for (s, h), e in zip(chains, st):
        sq = seq[s]
        u_val, w_k = e["x"][:, :GDN_DV], e["x"][:, GDN_DV:]
        q_dec = e["qh"] * sq["e_g"][:, h:h + 1]
        e["s_old"] = s_scr[s, h]
        ws = _dot(jnp.concatenate([w_k, q_dec], axis=0), e["s_old"])
        e["v_new"] = u_val - ws[:c]
        e["o_inter"] = ws[c:]

    for (s, h), e in zip(chains, st):
        sq = seq[s]
        o = e["o_inter"] + _dot(e["attn"], e["v_new"])
        k_dec = e["kh"] * sq["e_kd"][:, h:h + 1]
        s_scr[s, h] = e["s_old"] * sq["e_last"][:, h:h + 1] + _dot(k_dec.T, e["v_new"])
        zh = z_ref[s, :, e["sl"]].astype(F32)
        yb_ref[s, :, e["sl"]] = (((o * _rms_scale(o)) * ng) * (zh * jax.nn.sigmoid(zh))).astype(BF16)

    @pl.when(ci == pl.num_programs(1) - 1)
    def _():
        sout_ref[...] = s_scr[...]


def _gdn(q, k, v, p, gb, norm_g, bsz, t, c, nb, s0=None):
    d = D_MODEL
    nc = t // c
    has_s0 = s0 is not None

    def tok(col=0):
        return pl.BlockSpec((nb, c, d), lambda b, i, col=col: (b, i, col))

    in_specs = [tok(), tok(), tok(), tok(COL_GZ),
                pl.BlockSpec((nb, c, SMALL_COLS), lambda b, i: (b, i, 0)),
                pl.BlockSpec((1, GDN_DV), lambda b, i: (0, 0))]
    args = [a.reshape(bsz, t, -1) for a in (q, k, v, p, gb)] + [norm_g.reshape(1, GDN_DV)]
    s_spec = pl.BlockSpec((nb, GDN_HEADS, GDN_DK, GDN_DV), lambda b, i: (b, 0, 0, 0))
    if has_s0:
        in_specs.append(s_spec)
        args.append(s0)
    yb, s_new = pl.pallas_call(
        functools.partial(_gdn_kernel, c=c, nb=nb, has_s0=has_s0),
        grid=(bsz // nb, nc),
        in_specs=in_specs,
        out_specs=[tok(), s_spec],
        out_shape=[jax.ShapeDtypeStruct((bsz, t, d), BF16),
                   jax.ShapeDtypeStruct((bsz, GDN_HEADS, GDN_DK, GDN_DV), F32)],
        scratch_shapes=[pltpu.VMEM((nb, GDN_HEADS, GDN_DK, GDN_DV), F32)],
        compiler_params=_params(("parallel", "arbitrary")),
        name="gdn_scan",
    )(*args)
    return yb.reshape(bsz * t, d), s_new


def _mem_attn_kernel(q_ref, k_ref, v_ref, o_ref, *, nb, tq):
    scale = MEM_HEAD_DIM ** -0.5
    pairs = [(s, h) for s in range(nb) for h in range(MEM_HEADS)]
    span = lambda s, h: (slice(s * tq, (s + 1) * tq), slice(h * MEM_HEAD_DIM, (h + 1) * MEM_HEAD_DIM))
    scores = [_dot_nt(q_ref[span(s, h)], k_ref[s, :, span(s, h)[1]]) * scale for s, h in pairs]
    probs = []
    for sc in scores:
        e = jnp.exp(sc - jnp.max(sc, axis=-1, keepdims=True))
        probs.append(e / jnp.sum(e, axis=-1, keepdims=True))
    for (s, h), prob in zip(pairs, probs):
        o_ref[span(s, h)] = _dot(prob, v_ref[s, :, span(s, h)[1]]).astype(BF16)


def _mem_attn(p, mk, mv, bsz, t, tq, nb):
    d = D_MODEL
    m = mk.shape[1]
    nq = t // tq
    assert nb == 1 or nq == 1
    kv_spec = pl.BlockSpec((nb, m, d), lambda b, i: (b, 0, 0))
    return pl.pallas_call(
        functools.partial(_mem_attn_kernel, nb=nb, tq=tq),
        grid=(bsz // nb, nq),
        in_specs=[pl.BlockSpec((nb * tq, d), lambda b, i: (b * nq + i, COL_MQ)), kv_spec, kv_spec],
        out_specs=pl.BlockSpec((nb * tq, d), lambda b, i: (b * nq + i, 0)),
        out_shape=jax.ShapeDtypeStruct((bsz * t, d), BF16),
        compiler_params=_params(("parallel", "arbitrary")),
        name="mem_attn",
    )(p, mk, mv)


def _mem_attn_cache_kernel(q_ref, k_ref, v_ref, o_ref, *, nb, tq):
    scale = MEM_HEAD_DIM ** -0.5
    m = k_ref.shape[1]
    shape = (MEM_HEADS * tq, m * MEM_HEADS)
    own = (lax.broadcasted_iota(jnp.int32, shape, 1) % MEM_HEADS) == (lax.broadcasted_iota(jnp.int32, shape, 0) // tq)
    qs, ks, vs = [], [], []
    for s in range(nb):
        rows = slice(s * tq, (s + 1) * tq)
        qs.append(jnp.concatenate([q_ref[rows, h * MEM_HEAD_DIM:(h + 1) * MEM_HEAD_DIM] for h in range(MEM_HEADS)],
                                  axis=0))
        ks.append(k_ref[s].reshape(m * MEM_HEADS, MEM_HEAD_DIM))
        vs.append(v_ref[s].reshape(m * MEM_HEADS, MEM_HEAD_DIM))
    scores = [jnp.where(own, _dot_nt(q, k) * scale, -jnp.inf) for q, k in zip(qs, ks)]
    probs = []
    for sc in scores:
        e = jnp.exp(sc - jnp.max(sc, axis=-1, keepdims=True))
        probs.append(e / jnp.sum(e, axis=-1, keepdims=True))
    for s, (prob, v) in enumerate(zip(probs, vs)):
        o = _dot(prob, v)
        o_ref[s * tq:(s + 1) * tq, :] = jnp.concatenate(
            [o[h * tq:(h + 1) * tq] for h in range(MEM_HEADS)], axis=1).astype(BF16)


def _mem_attn_cache(p, mk, mv, bsz, t, nb):
    d = D_MODEL
    kv_spec = pl.BlockSpec((nb,) + mk.shape[1:], lambda b: (b, 0, 0, 0))
    return pl.pallas_call(
        functools.partial(_mem_attn_cache_kernel, nb=nb, tq=t),
        grid=(bsz // nb,),
        in_specs=[pl.BlockSpec((nb * t, d), lambda b: (b, COL_MQ)), kv_spec, kv_spec],
        out_specs=pl.BlockSpec((nb * t, d), lambda b: (b, 0)),
        out_shape=jax.ShapeDtypeStruct((bsz * t, d), BF16),
        compiler_params=_params(("parallel",)),
        name="mem_attn_cache",
    )(p, mk, mv)


def _merge_kernel(x_ref, ya_ref, yb_ref, ym_ref, g0_ref, g1_ref, g2_ref, wc_ref, wg_ref, wm_ref, wo_ref,
                  ng_ref, x1_ref, h2_ref):
    d = functools.partial(jnp.dot, preferred_element_type=F32)
    merged = (jax.nn.sigmoid(_f32(g0_ref)) * d(ya_ref[...], wc_ref[...])
              + jax.nn.sigmoid(_f32(g1_ref)) * d(yb_ref[...], wg_ref[...])
              + jax.nn.sigmoid(_f32(g2_ref)) * d(ym_ref[...], wm_ref[...]))
    x1 = x_ref[...] + d(merged.astype(BF16), wo_ref[...])
    x1_ref[...] = x1
    h2_ref[...] = ((x1 * _rms_scale(x1)) * ng_ref[...]).astype(BF16)


def _merge(x, ya, yb, ym, p, wc, wg, wm, wo, norm_g, tm):
    n, d = x.shape
    tok = pl.BlockSpec((tm, d), lambda i: (i, 0))
    wsp = pl.BlockSpec((d, d), lambda i: (0, 0))

    def gate(j):
        return pl.BlockSpec((tm, d), lambda i, j=j: (i, COL_BG + j))

    return pl.pallas_call(
        _merge_kernel,
        grid=(n // tm,),
        in_specs=[tok, tok, tok, tok, gate(0), gate(1), gate(2), wsp, wsp, wsp, wsp,
                  pl.BlockSpec((1, d), lambda i: (0, 0))],
        out_specs=[tok, tok],
        out_shape=[jax.ShapeDtypeStruct((n, d), F32), jax.ShapeDtypeStruct((n, d), BF16)],
        compiler_params=_params(("parallel",)),
        name="merge",
    )(x, ya, yb, ym, p, p, p, wc, wg, wm, wo, norm_g.reshape(1, d))


_CAND_CELLS = tuple((a, b) for a in range(PEER_TOPK) for b in range(PEER_TOPK) if (a + 1) * (b + 1) <= PEER_TOPK)

TILE_PAD = 4 * SUBLANES
SHIFT_S1, SHIFT_E1 = 0, SUBLANES


def _put_tile(ref, h, ns, shift, val, pad=True):
    if pad and shift:
        ref[h, ns, :shift, :] = jnp.zeros((shift, LANES), F32)
    ref[h, ns, shift:shift + PEER_N_KEYS, :] = val
    if pad and shift < TILE_PAD:
        ref[h, ns, shift + PEER_N_KEYS:, :] = jnp.zeros((TILE_PAD - shift, LANES), F32)


def _put_windows(ref, h, ns, val):
    for w in range(PEER_N_KEYS // E0_WIN):
        base = w * (E0_WIN + SUBLANES)
        ref[h, ns, base:base + SUBLANES, :] = jnp.zeros((SUBLANES, LANES), F32)
        ref[h, ns, base + SUBLANES:base + SUBLANES + E0_WIN, :] = val[w * E0_WIN:(w + 1) * E0_WIN, :]


def _top_values(x, count):
    vals = []
    cur = x
    for r in range(count):
        m = jnp.max(cur, axis=0, keepdims=True)
        vals.append(m)
        if r + 1 < count:
            cur = jnp.where(cur == m, -jnp.inf, cur)
    return vals


def _top_ranked(x, count):
    n_rows = x.shape[0]
    iota = lax.broadcasted_iota(jnp.int32, x.shape, 0).astype(F32)
    vals = []
    cur = x
    rank = jnp.full(x.shape, float(count), F32)
    for r in range(count):
        m = jnp.max(cur, axis=0, keepdims=True)
        first = jnp.min(jnp.where(cur == m, iota, float(n_rows)), axis=0, keepdims=True)
        one = iota == first
        rank = jnp.where(one, float(r), rank)
        cur = jnp.where(one, -jnp.inf, cur)
        vals.append(m)
    return vals, rank


def _count_rows(mask):
    return jnp.sum(jnp.where(mask, 1.0, 0.0), axis=0, keepdims=True)


def _route_kernel(h_ref, wq_ref, keys_ref, cut_ref, e0_ref, s1_ref, e1_ref, q_scr, tie_ref):
    tn = h_ref.shape[0]
    nsb = tn // LANES
    q = jnp.dot(h_ref[...], wq_ref[...], preferred_element_type=F32).astype(BF16)

    def store(h, cut, e0, s1, e1):
        for ns in range(nsb):
            cols = slice(ns * LANES, (ns + 1) * LANES)
            cut_ref[h, ns] = cut[:, cols]
            _put_windows(e0_ref, h, ns, e0[:, cols])
            _put_tile(s1_ref, h, ns, SHIFT_S1, s1[:, cols])
            _put_tile(e1_ref, h, ns, SHIFT_E1, e1[:, cols])

    for h in range(PEER_HEADS):
        sc, top = [], []
        tied = jnp.zeros((1, tn), F32)
        for part in range(2):
            col = (2 * h + part) * PEER_D_HALF
            q_scr[2 * h + part] = q[:, col:col + PEER_D_HALF]
            s_t = lax.dot_general(keys_ref[part, h], q[:, col:col + PEER_D_HALF],
                                  (((1,), (1,)), ((), ())), preferred_element_type=F32)
            sc.append(s_t)
            top.append(_top_values(s_t, PEER_TOPK))
            tied = jnp.maximum(tied, jnp.abs(_count_rows(s_t >= top[part][-1]) - PEER_TOPK))
        cand = jnp.concatenate([top[0][a] + top[1][b] for a, b in _CAND_CELLS], axis=0)
        best = _top_values(cand, PEER_TOPK)
        c_max, thr = best[0], best[-1]
        z = jnp.exp(best[0] - c_max)
        for r in range(1, PEER_TOPK):
            z = z + jnp.exp(best[r] - c_max)
        top1 = jnp.concatenate(top[1], axis=0)
        n_sel = sum(_count_rows(top[0][a] + top1 >= thr) for a in range(PEER_TOPK))
        tied = jnp.maximum(tied, jnp.abs(n_sel - PEER_TOPK))
        cut = jnp.full(sc[0].shape, jnp.inf, F32)
        for b in range(PEER_TOPK):
            cut = jnp.where(sc[0] + top[1][b] >= thr, top[1][b], cut)
        store(h, cut, jnp.exp(sc[0] - top[0][0]) / z, sc[1], jnp.exp(sc[1] - top[1][0]))
        for ns in range(nsb):
            tie_ref[h * nsb + ns] = jnp.max(tied[:, ns * LANES:(ns + 1) * LANES])

    def exact_block(idx, carry):
        @pl.when(tie_ref[idx] > 0.0)
        def _():
            h = idx // nsb
            ns = idx % nsb
            rows = pl.ds(pl.multiple_of(ns * LANES, LANES), LANES)
            sc, top, rank = [], [], []
            for part in range(2):
                s_t = lax.dot_general(keys_ref[part, h], q_scr[2 * h + part, rows, :],
                                      (((1,), (1,)), ((), ())), preferred_element_type=F32)
                vals, rk = _top_ranked(s_t, PEER_TOPK)
                sc.append(s_t)
                top.append(vals)
                rank.append(rk)
            top1 = jnp.concatenate(top[1], axis=0)
            cand = jnp.concatenate([top[0][a] + top1 for a in range(PEER_TOPK)], axis=0)
            best, pos = _top_ranked(cand, PEER_TOPK)
            z = jnp.exp(best[0] - best[0])
            for r in range(1, PEER_TOPK):
                z = z + jnp.exp(best[r] - best[0])
            cnt = jnp.zeros_like(sc[0])
            for a in range(PEER_TOPK):
                cnt_a = _count_rows(pos[a * PEER_TOPK:(a + 1) * PEER_TOPK] < PEER_TOPK)
                cnt = jnp.where(rank[0] == float(a), cnt_a, cnt)
            cut_ref[h, ns] = jnp.where(cnt > 0.0, 1.0 - cnt, jnp.inf)
            _put_windows(e0_ref, h, ns, jnp.exp(sc[0] - top[0][0]) / z)
            _put_tile(s1_ref, h, ns, SHIFT_S1, -rank[1], pad=False)
            _put_tile(e1_ref, h, ns, SHIFT_E1, jnp.exp(sc[1] - top[1][0]), pad=False)
        return carry

    lax.fori_loop(0, PEER_HEADS * nsb, exact_block, 0)


def _route(h2, wq, keys, tn):
    n, d = h2.shape
    nk = PEER_N_KEYS
    tiled = pl.BlockSpec((PEER_HEADS, tn // LANES, nk + TILE_PAD, LANES), lambda i: (0, i, 0, 0))
    tiled_shape = jax.ShapeDtypeStruct((PEER_HEADS, n // LANES, nk + TILE_PAD, LANES), F32)
    rowed = pl.BlockSpec((PEER_HEADS, tn // LANES, nk, LANES), lambda i: (0, i, 0, 0))
    rowed_shape = jax.ShapeDtypeStruct((PEER_HEADS, n // LANES, nk, LANES), F32)
    win_rows = (nk // E0_WIN) * (E0_WIN + SUBLANES)
    windowed = pl.BlockSpec((PEER_HEADS, tn // LANES, win_rows, LANES), lambda i: (0, i, 0, 0))
    windowed_shape = jax.ShapeDtypeStruct((PEER_HEADS, n // LANES, win_rows, LANES), F32)
    return pl.pallas_call(
        _route_kernel,
        grid=(n // tn,),
        in_specs=[pl.BlockSpec((tn, d), lambda i: (i, 0)),
                  pl.BlockSpec(wq.shape, lambda i: (0, 0)),
                  pl.BlockSpec(keys.shape, lambda i: (0, 0, 0, 0))],
        out_specs=[rowed, windowed, tiled, tiled],
        out_shape=[rowed_shape, windowed_shape, tiled_shape, tiled_shape],
        scratch_shapes=[pltpu.VMEM((2 * PEER_HEADS, tn, PEER_D_HALF), BF16),
                        pltpu.SMEM((PEER_HEADS * (tn // LANES),), F32)],
        compiler_params=_params(("parallel",)),
        name="peer_route",
    )(h2, wq, keys)


def _gelu(x):
    return 0.5 * x * (1.0 + lax.erf(x * (2.0 ** -0.5)))


PEER_TILE = 2048
E0_WIN = PEER_TILE // PEER_N_KEYS
GATE_HALF = 32
GATE_ROWS = 8


def _peer_kernel(h_ref, x1_ref, u_ref, vt_ref, cut_ref, e0_ref, s1_ref, e1_ref, ng_ref,
                 y_ref, acc_ref, act_ref, coef_ref, *, te):
    ei = pl.program_id(1)
    tn = h_ref.shape[0]
    groups = te // PEER_N_KEYS

    @pl.when(ei == 0)
    def _():
        acc_ref[...] = jnp.zeros_like(acc_ref)

    nsb = tn // LANES
    act = _gelu(lax.dot_general(u_ref[...], h_ref[...], (((1,), (1,)), ((), ())),
                                preferred_element_type=F32))
    for ns in range(nsb):
        act_ref[ns] = act[:, ns * LANES:(ns + 1) * LANES]

    def gate_block(ns):
        for jh in range(PEER_N_KEYS // GATE_HALF):
            jrows = slice(jh * GATE_HALF, (jh + 1) * GATE_HALF)
            for g0 in range(0, groups, GATE_ROWS):
                gates = [None] * GATE_ROWS
                for h in range(PEER_HEADS):
                    s1_t = s1_ref[h, ns, SHIFT_S1 + jh * GATE_HALF:SHIFT_S1 + (jh + 1) * GATE_HALF, :]
                    e1_t = e1_ref[h, ns, SHIFT_E1 + jh * GATE_HALF:SHIFT_E1 + (jh + 1) * GATE_HALF, :]
                    for k in range(GATE_ROWS):
                        sel = s1_t >= cut_ref[h, ns, g0 + k:g0 + k + 1, :]
                        term = jnp.where(sel, e1_t, 0.0) * e0_ref[h, ns, SUBLANES + g0 + k:SUBLANES + g0 + k + 1, :]
                        gates[k] = term if gates[k] is None else gates[k] + term
                for k in range(GATE_ROWS):
                    r0 = (g0 + k) * PEER_N_KEYS + jh * GATE_HALF
                    coef_ref[ns, r0:r0 + GATE_HALF, :] = (gates[k] * act_ref[ns, r0:r0 + GATE_HALF, :]).astype(BF16)

    def lane_block(ns, carry):
        gate_block(ns)
        return carry

    def mix_values(lo, hi):
        coef = jnp.concatenate([coef_ref[ns] for ns in range(lo, hi)], axis=1)
        acc_ref[:, lo * LANES:hi * LANES] += jnp.dot(vt_ref[0], coef, preferred_element_type=F32)

    half = nsb // 2
    lax.fori_loop(0, half, lane_block, 0)
    if half:
        mix_values(0, half)
    for ns in range(half, nsb):
        gate_block(ns)
    mix_values(half, nsb)

    @pl.when(ei == pl.num_programs(1) - 1)
    def _():
        x2 = x1_ref[...] + acc_ref[...].T
        y_ref[...] = (x2 * _rms_scale(x2)) * ng_ref[...]


def _peer(h2, x1, u_bf, vt_bf, cut, e0, s1, e1, norm_g, tn):
    n, d = h2.shape
    ne = u_bf.shape[0]
    te = vt_bf.shape[2]
    nk = PEER_N_KEYS
    tok = pl.BlockSpec((tn, d), lambda i, e: (i, 0))
    tiled = pl.BlockSpec((PEER_HEADS, tn // LANES, nk + TILE_PAD, LANES), lambda i, e: (0, i, 0, 0))
    rowed = pl.BlockSpec((PEER_HEADS, tn // LANES, te // nk, LANES), lambda i, e: (0, i, e, 0))
    return pl.pallas_call(
        functools.partial(_peer_kernel, te=te),
        grid=(n // tn, ne // te),
        in_specs=[tok, tok,
                  pl.BlockSpec((te, d), lambda i, e: (e, 0)),
                  pl.BlockSpec((1, d, te), lambda i, e: (e, 0, 0)),
                  rowed, pl.BlockSpec((PEER_HEADS, tn // LANES, E0_WIN + SUBLANES, LANES), lambda i, e: (0, i, e, 0)),
                  tiled, tiled,
                  pl.BlockSpec((1, d), lambda i, e: (0, 0))],
        out_specs=tok,
        out_shape=jax.ShapeDtypeStruct((n, d), F32),
        scratch_shapes=[pltpu.VMEM((d, tn), F32), pltpu.VMEM((tn // LANES, te, LANES), F32),
                        pltpu.VMEM((tn // LANES, te, LANES), BF16)],
        compiler_params=_params(("parallel", "arbitrary")),
        name="peer_mix",
    )(h2, x1, u_bf, vt_bf, cut, e0, s1, e1, norm_g.reshape(1, d))


def _tile(n, pref):
    return math.gcd(n, pref)


def _layer(x, mk, mv, bsz, t, weights, conv_state=None, gdn_conv_state=None, gdn_s0=None):
    (norm_mix_g, w_main, w_ab, conv_w, gdn_conv_w, a_log, dt_bias, gdn_norm_g, wc, wg, wm, wo, norm_ffn_g,
     wq, keys, u_bf, vt_bf, final_norm_g) = weights
    n = bsz * t
    sample = conv_state is not None
    p, pab = _inproj(x, norm_mix_g, w_main, w_ab, _tile(n, 2048))
    rows = _tile(n, 256) if sample else _tile(t, 512)
    ya, q, k, v, gb, u_tail = _prep(p, pab, conv_w, gdn_conv_w, a_log, dt_bias, rows, max(t // rows, 1),
                                    conv_state, gdn_conv_state)
    chunk = _tile(t, 64)
    if sample:
        yb, s_new = _gdn(q, k, v, p, gb, gdn_norm_g, bsz, t, chunk, 4, gdn_s0)
    else:
        yb, s_new = _gdn_stacked(q, k, v, p, gb, gdn_norm_g, bsz, t, chunk, _tile(bsz, 4))
    if mk.ndim == 4:
        ym = _mem_attn_cache(p, mk, mv, bsz, t, _tile(bsz, 4))
    else:
        ym = _mem_attn(p, mk, mv, bsz, t, _tile(t, 512), 1)
    x1, h2 = _merge(x, ya, yb, ym, p, wc, wg, wm, wo, norm_ffn_g, _tile(n, 512))
    if sample:
        keep = lambda a: a.reshape(bsz, t, -1)[:, t // 2:, :].reshape(bsz * t // 2, -1)
        x1, h2 = keep(x1), keep(h2)
    tn = _tile(x1.shape[0], 512)
    cut, e0, s1, e1 = _route(h2, wq, keys, tn)
    y = _peer(h2, x1, u_bf, vt_bf, cut, e0, s1, e1, final_norm_g, tn)
    return y, u_tail, p, s_new


def kernel(x_prompt, x_sample, mem_prompt, cache_mem_k, cache_mem_v, state_conv, state_gdn_conv, state_gdn, norm_mix_g, w_in, conv_w, gdn_conv_w, gdn_a_log, gdn_dt_bias, gdn_norm_g, mem_norm_g, w_mem_kv, w_br_conv, w_br_gdn, w_br_mem, w_out, norm_ffn_g, peer_w_q, peer_sub_keys, peer_u, peer_v, final_norm_g):
    depth = w_in.shape[0]
    assert depth == 1, "single-layer step"
    d = D_MODEL
    bp, tp, _ = x_prompt.shape
    bs, ts, _ = x_sample.shape
    assert ts == SAMPLE_PAD_T // 2
    m_tok = mem_prompt.shape[1]
    mq = MEM_HEADS * MEM_HEAD_DIM
    l = 0

    w = w_in[l]
    ga0 = 7 * d
    w_main = jnp.concatenate([w[:, :ga0], w[:, ga0 + 2 * GDN_HEADS:]], axis=1).astype(BF16)
    w_ab = jnp.pad(w[:, ga0:ga0 + 2 * GDN_HEADS], ((0, 0), (0, SMALL_COLS - 2 * GDN_HEADS))).astype(BF16)
    weights = (norm_mix_g[l], w_main, w_ab, conv_w[l], gdn_conv_w[l], gdn_a_log[l], gdn_dt_bias[l], gdn_norm_g[l],
               w_br_conv[l].astype(BF16), w_br_gdn[l].astype(BF16), w_br_mem[l].astype(BF16), w_out[l].astype(BF16),
               norm_ffn_g[l], peer_w_q[l].astype(BF16), peer_sub_keys[l].astype(BF16),
               peer_u[l].astype(BF16),
               peer_v[l].astype(BF16).reshape(-1, PEER_TILE, d).transpose(0, 2, 1), final_norm_g)

    kv_shape = (bp, m_tok, mq)
    mk_p, mv_p = _mem_kv(mem_prompt.reshape(bp * m_tok, d), mem_norm_g[l], w_mem_kv[l].astype(BF16),
                         _tile(bp * m_tok, 256))
    y_p, u_tail_p, p_p, s_p = _layer(x_prompt.reshape(bp * tp, d), mk_p.reshape(kv_shape), mv_p.reshape(kv_shape),
                                     bp, tp, weights)
    rows_p = _tile(tp, 512)
    cb_p = u_tail_p.reshape(bp, tp // rows_p, SUBLANES, d)[:, -1, SUBLANES - (CONV_K - 1):, :]
    gb_p = p_p.reshape(bp, tp, -1)[:, tp - (GDN_CONV_K - 1):, COL_GQ * d:(COL_GV + 1) * d].astype(F32)

    pad_t = SAMPLE_PAD_T
    lead = pad_t - ts
    x_s = jnp.pad(x_sample, ((0, 0), (lead, 0), (0, 0))).reshape(bs * pad_t, d)
    cst = jnp.pad(state_conv[l], ((0, 0), (lead - (CONV_K - 1), ts), (0, 0))).reshape(bs * pad_t, d)
    gst = jnp.pad(state_gdn_conv[l], ((0, 0), (lead - (GDN_CONV_K - 1), ts), (0, 0))).reshape(bs * pad_t, 3 * d)
    y_s, u_s, p_s, s_s = _layer(x_s, cache_mem_k.reshape(cache_mem_k.shape[1:]), cache_mem_v.reshape(cache_mem_v.shape[1:]),
                                bs, pad_t, weights, cst, gst, state_gdn.reshape(bs, GDN_HEADS, GDN_DK, GDN_DV))
    y_s = y_s.reshape(bs, ts, d)
    cb_s = u_s.reshape(bs, pad_t, d)[:, pad_t - (CONV_K - 1):, :]
    gb_s = p_s.reshape(bs, pad_t, -1)[:, pad_t - (GDN_CONV_K - 1):, COL_GQ * d:(COL_GV + 1) * d].astype(F32)

    mshape = (depth, bp, m_tok, MEM_HEADS, MEM_HEAD_DIM)
    return (y_p.reshape(bp, tp, d), y_s,
            mk_p.reshape(mshape), mv_p.reshape(mshape),
            cb_p[None], gb_p[None], s_p[None],
            cb_s[None], gb_s[None], s_s[None])
```
